```python
import math
import jax
import jax.numpy as jnp
from jax import lax
import numpy as np

D_MODEL = 1024
BATCH = 32
SEQ = 256
DEPTH = 4
DEC_BATCH = 4
DEC_SEQ = 2048
PAST_LEN = 256

GRID_W = 64
N_HEADS = 8
QK_NOPE = 64
ROPE_DIM = 32
V_HEAD = 64
QK_HEAD = QK_NOPE + ROPE_DIM
Q_RANK = 384
KV_RANK = 256
ROPE_THETA = 10000.0
Q_BLOCK = 128
FNET_GROUPS = 4
FNET_GC = 128
FNET_WIDTH = FNET_GROUPS * FNET_GC
POOL_WINDOWS = (2, 4, 8, 16)
POOL_GC = 128
POOL_WIDTH = len(POOL_WINDOWS) * POOL_GC
N_BRANCHES = 3
IN_SPLITS = (Q_RANK,
             Q_RANK + KV_RANK,
             Q_RANK + KV_RANK + ROPE_DIM,
             Q_RANK + KV_RANK + ROPE_DIM + FNET_WIDTH,
             Q_RANK + KV_RANK + ROPE_DIM + FNET_WIDTH + POOL_WIDTH)
IN_WIDTH = IN_SPLITS[-1] + N_BRANCHES * D_MODEL
N_EXPERTS = 32
TOP_K = 4
D_FF = D_MODEL
SWIGLU_LIMIT = 7.0
SWIGLU_ALPHA = 1.702
MOE_BLOCK = 256
RMS_EPS = 1e-6

kernel_name = "hybrid_mla_fnet_pool_moe_diffusion_step"


def rms_norm(x, g):
    xf = x.astype(jnp.float32)
    y = xf * lax.rsqrt(jnp.mean(xf * xf, axis=-1, keepdims=True) + RMS_EPS)
    return y.astype(x.dtype) * g


def adaln(cond, w_mod, b_mod):
    m = jax.nn.silu(cond) @ w_mod + b_mod
    return jnp.split(m[:, None, :], 6, axis=-1)


def axial_rope_tables(n_tokens):
    rows = n_tokens // GRID_W
    row = jnp.broadcast_to(jnp.arange(rows)[:, None], (rows, GRID_W)).reshape(n_tokens)
    col = jnp.broadcast_to(jnp.arange(GRID_W)[None, :], (rows, GRID_W)).reshape(n_tokens)
    n_freq = ROPE_DIM // 4
    inv = 1.0 / (ROPE_THETA ** (jnp.arange(n_freq, dtype=jnp.float32) / n_freq))
    ang = jnp.concatenate([row[:, None].astype(jnp.float32) * inv,
                           col[:, None].astype(jnp.float32) * inv], axis=-1)
    return jnp.cos(ang), jnp.sin(ang)


def apply_rope(x, cos, sin):
    x1, x2 = jnp.split(x, 2, axis=-1)
    cos = cos.astype(x.dtype)
    sin = sin.astype(x.dtype)
    return jnp.concatenate([x1 * cos - x2 * sin, x1 * sin + x2 * cos], axis=-1)


def mla_queries(q_lat, lp):
    B, L, _ = q_lat.shape
    q = (rms_norm(q_lat, lp["q_a_g"]) @ lp["w_q_b"]).reshape(B, L, N_HEADS, QK_HEAD)
    q_nope = rms_norm(q[..., :QK_NOPE], lp["q_nope_g"])
    q_pe = rms_norm(q[..., QK_NOPE:], lp["q_rope_g"])
    return q_nope, q_pe


def mla_keys(ckv, kpe, lp):
    B, L, _ = ckv.shape
    kv = (ckv @ lp["w_kv_b"]).reshape(B, L, N_HEADS, QK_NOPE + V_HEAD)
    k_nope = rms_norm(kv[..., :QK_NOPE], lp["k_nope_g"])
    v = kv[..., QK_NOPE:]
    k_pe = rms_norm(kpe, lp["k_rope_g"])
    return k_nope, k_pe, v


def mla_attention(q_nope, q_pe, k_nope, k_pe, v):
    B, Lq = q_nope.shape[:2]
    nb = Lq // Q_BLOCK
    scale = 1.0 / math.sqrt(QK_HEAD)

    def to_blocks(q):
        return jnp.moveaxis(q.reshape((B, nb, Q_BLOCK) + q.shape[2:]), 1, 0)

    def attend(blk):
        qn, qp = blk
        s = (jnp.einsum("bqhd,bkhd->bhqk", qn, k_nope)
             + jnp.einsum("bqhr,bkr->bhqk", qp, k_pe))
        p = jax.nn.softmax(s.astype(jnp.float32) * scale, axis=-1).astype(v.dtype)
        return jnp.einsum("bhqk,bkhd->bqhd", p, v)

    o = lax.map(attend, (to_blocks(q_nope), to_blocks(q_pe)))
    return jnp.moveaxis(o, 0, 1).reshape(B, Lq, N_HEADS * V_HEAD)


def fourier_mix(f_in):
    B, L, _ = f_in.shape
    xg = f_in.astype(jnp.float32).reshape(B, L, FNET_GROUPS, FNET_GC)
    y = jnp.fft.fft2(xg, axes=(1, 3), norm="ortho").real
    return y.reshape(B, L, FNET_WIDTH).astype(f_in.dtype)


def multiscale_pool(p_in, w_pool_grp, pool_scale):
    B, L, _ = p_in.shape
    G = len(POOL_WINDOWS)
    xf = p_in.astype(jnp.float32).reshape(B, L, G, POOL_GC)
    cs = jnp.concatenate([jnp.zeros((B, 1, G, POOL_GC), jnp.float32),
                          jnp.cumsum(xf, axis=1)], axis=1)
    t = jnp.arange(L)
    means = []
    for g, w in enumerate(POOL_WINDOWS):
        lo = jnp.maximum(t - w // 2, 0)
        hi = jnp.minimum(t + w // 2, L)
        cnt = (hi - lo).astype(jnp.float32)[:, None]
        means.append((cs[:, hi, g] - cs[:, lo, g]) / cnt)
    pooled = (jnp.stack(means, axis=2) - xf).astype(p_in.dtype)
    mixed = jnp.einsum("blgc,gcd->blgd", pooled, w_pool_grp)
    return mixed.reshape(B, L, POOL_WIDTH) * pool_scale


def token_mixer(h, lp, rope, ctx_cache):
    z = h @ lp["w_in"]
    q_lat, kv_lat, kpe, f_in, p_in, gate_logits = jnp.split(z, IN_SPLITS, axis=-1)
    ckv = rms_norm(kv_lat, lp["kv_a_g"])
    q_nope, q_pe = mla_queries(q_lat, lp)
    k_nope, k_pe, v = mla_keys(ckv, kpe, lp)
    if rope is not None:
        cos, sin = rope
        q_pe = apply_rope(q_pe, cos[:, None, :], sin[:, None, :])
        k_pe = apply_rope(k_pe, cos, sin)
    if ctx_cache is not None:
        ck_nope, ck_pe, cv = mla_keys(ctx_cache[0], ctx_cache[1], lp)
        k_nope = jnp.concatenate([k_nope, ck_nope], axis=1)
        k_pe = jnp.concatenate([k_pe, ck_pe], axis=1)
        v = jnp.concatenate([v, cv], axis=1)
    attn = mla_attention(q_nope, q_pe, k_nope, k_pe, v) @ lp["w_attn_o"]
    fnet = fourier_mix(f_in) @ lp["w_fnet_o"]
    pool = multiscale_pool(p_in, lp["w_pool_grp"], lp["pool_scale"]) @ lp["w_pool_o"]
    gates = jax.nn.sigmoid(gate_logits.astype(jnp.float32)).astype(h.dtype)
    g_attn, g_fnet, g_pool = jnp.split(gates, N_BRANCHES, axis=-1)
    out = (g_attn * attn + g_fnet * fnet + g_pool * pool) @ lp["w_out"]
    return out, ckv, kpe


def moe_ffn(h, lp):
    B, L, D = h.shape
    T = B * L
    tok = h.reshape(T, D)
    logits = (tok @ lp["w_router"] + lp["b_router"]).astype(jnp.float32)
    top_logit, top_exp = lax.top_k(logits, TOP_K)
    top_w = jax.nn.softmax(top_logit, axis=-1)
    n = T * TOP_K
    flat_exp = top_exp.reshape(n)
    order = jnp.argsort(flat_exp)
    sorted_exp = flat_exp[order]
    counts = jnp.bincount(flat_exp, length=N_EXPERTS)
    padded = (counts + MOE_BLOCK - 1) // MOE_BLOCK * MOE_BLOCK
    group_start = jnp.cumsum(counts) - counts
    padded_end = jnp.cumsum(padded)
    padded_start = padded_end - padded
    slot = padded_start[sorted_exp] + jnp.arange(n) - group_start[sorted_exp]
    n_blocks = (n + N_EXPERTS * (MOE_BLOCK - 1) + MOE_BLOCK - 1) // MOE_BLOCK
    n_slots = n_blocks * MOE_BLOCK
    slot_tok = jnp.full((n_slots,), T, jnp.int32).at[slot].set((order // TOP_K).astype(jnp.int32))
    slot_w = jnp.zeros((n_slots,), jnp.float32).at[slot].set(top_w.reshape(n)[order])
    block_exp = jnp.minimum(
        jnp.searchsorted(padded_end, jnp.arange(n_blocks) * MOE_BLOCK, side="right"),
        N_EXPERTS - 1)
    tok_pad = jnp.concatenate([tok, jnp.zeros((1, D), tok.dtype)], axis=0)
    w_gu, b_gu, w_down, b_down = lp["w_gu"], lp["b_gu"], lp["w_down"], lp["b_down"]

    def expert_block(args):
        idx, e = args
        gu = tok_pad[idx] @ w_gu[e] + b_gu[e]
        glu, lin = jnp.split(gu, 2, axis=-1)
        glu = jnp.minimum(glu, SWIGLU_LIMIT)
        lin = jnp.clip(lin, -SWIGLU_LIMIT, SWIGLU_LIMIT)
        act = glu * jax.nn.sigmoid(SWIGLU_ALPHA * glu) * (lin + 1)
        return act @ w_down[e] + b_down[e]

    y = lax.map(expert_block, (slot_tok.reshape(n_blocks, MOE_BLOCK), block_exp))
    y = (y.reshape(n_slots, D) * slot_w[:, None]).astype(tok.dtype)
    out = jnp.zeros((T + 1, D), tok.dtype).at[slot_tok].add(y)[:T]
    return out.reshape(B, L, D)


def trunk_layer(x, cond, lp, rope, ctx_cache):
    shift1, scale1, gate1, shift2, scale2, gate2 = adaln(cond, lp["w_mod"], lp["b_mod"])
    h = rms_norm(x, lp["norm1_g"]) * (1 + scale1) + shift1
    mix, ckv, kpe = token_mixer(h, lp, rope, ctx_cache)
    x = x + gate1 * mix
    h = rms_norm(x, lp["norm2_g"]) * (1 + scale2) + shift2
    x = x + gate2 * moe_ffn(h, lp)
    return x, ckv, kpe


def setup_inputs(seed: int = 0) -> dict:
    key = jax.random.key(seed)
    keys = list(jax.random.split(key, 32))

    def nrm(shape, scale):
        return scale * jax.random.normal(keys.pop(), shape, jnp.float32)

    def gain(shape):
        return 1.0 + nrm(shape, 0.02)

    L, D = DEPTH, D_MODEL
    inp = {}
    inp["x_prompt"] = nrm((BATCH, SEQ, D), 1.0)
    inp["x_sample"] = nrm((DEC_BATCH, DEC_SEQ, D), 1.0)
    inp["cache_ckv"] = nrm((DEC_BATCH, DEPTH, PAST_LEN, KV_RANK), 1.0)
    inp["cache_kpe"] = nrm((DEC_BATCH, DEPTH, PAST_LEN, ROPE_DIM), 1.0)
    inp["c"] = nrm((DEC_BATCH, D), 1.0)
    inp["c_ctx"] = nrm((D,), 1.0)
    inp["w_mod"] = nrm((L, D, 6 * D), 0.5 * D ** -0.5)
    inp["b_mod"] = nrm((L, 6 * D), 0.01)
    inp["norm1_g"] = gain((L, D))
    inp["norm2_g"] = gain((L, D))
    inp["w_in"] = nrm((L, D, IN_WIDTH), D ** -0.5)
    inp["q_a_g"] = gain((L, Q_RANK))
    inp["kv_a_g"] = gain((L, KV_RANK))
    inp["w_q_b"] = nrm((L, Q_RANK, N_HEADS * QK_HEAD), Q_RANK ** -0.5)
    inp["w_kv_b"] = nrm((L, KV_RANK, N_HEADS * (QK_NOPE + V_HEAD)), KV_RANK ** -0.5)
    inp["q_nope_g"] = gain((L, QK_NOPE))
    inp["q_rope_g"] = gain((L, ROPE_DIM))
    inp["k_nope_g"] = gain((L, QK_NOPE))
    inp["k_rope_g"] = gain((L, ROPE_DIM))
    inp["w_attn_o"] = nrm((L, N_HEADS * V_HEAD, D), (N_HEADS * V_HEAD) ** -0.5)
    inp["w_fnet_o"] = nrm((L, FNET_WIDTH, D), FNET_WIDTH ** -0.5)
    inp["w_pool_grp"] = nrm((L, len(POOL_WINDOWS), POOL_GC, POOL_GC), POOL_GC ** -0.5)
    inp["pool_scale"] = gain((L, POOL_WIDTH))
    inp["w_pool_o"] = nrm((L, POOL_WIDTH, D), POOL_WIDTH ** -0.5)
    inp["w_out"] = nrm((L, D, D), D ** -0.5)
    inp["w_router"] = nrm((L, D, N_EXPERTS), D ** -0.5)
    inp["b_router"] = nrm((L, N_EXPERTS), 0.01)
    inp["w_gu"] = nrm((L, N_EXPERTS, D, 2 * D_FF), D ** -0.5)
    inp["b_gu"] = nrm((L, N_EXPERTS, 2 * D_FF), 0.01)
    inp["w_down"] = nrm((L, N_EXPERTS, D_FF, D), D_FF ** -0.5)
    inp["b_down"] = nrm((L, N_EXPERTS, D), 0.01)
    return inp


def reference(x_prompt, x_sample, cache_ckv, cache_kpe, c, c_ctx,
              w_mod, b_mod, norm1_g, norm2_g, w_in, q_a_g, kv_a_g, w_q_b, w_kv_b,
              q_nope_g, q_rope_g, k_nope_g, k_rope_g, w_attn_o, w_fnet_o,
              w_pool_grp, pool_scale, w_pool_o, w_out,
              w_router, b_router, w_gu, b_gu, w_down, b_down):
    rope = axial_rope_tables(x_sample.shape[1])
    cond_ctx = c_ctx[None, :]
    xp, xs = x_prompt, x_sample
    ckv_list, kpe_list = [], []
    for l in range(DEPTH):
        lp = {
            "w_mod": w_mod[l], "b_mod": b_mod[l], "norm1_g": norm1_g[l], "norm2_g": norm2_g[l],
            "w_in": w_in[l], "q_a_g": q_a_g[l], "kv_a_g": kv_a_g[l],
            "w_q_b": w_q_b[l], "w_kv_b": w_kv_b[l],
            "q_nope_g": q_nope_g[l], "q_rope_g": q_rope_g[l],
            "k_nope_g": k_nope_g[l], "k_rope_g": k_rope_g[l],
            "w_attn_o": w_attn_o[l], "w_fnet_o": w_fnet_o[l],
            "w_pool_grp": w_pool_grp[l], "pool_scale": pool_scale[l], "w_pool_o": w_pool_o[l],
            "w_out": w_out[l], "w_router": w_router[l], "b_router": b_router[l],
            "w_gu": w_gu[l], "b_gu": b_gu[l], "w_down": w_down[l], "b_down": b_down[l],
        }
        xp, ckv, kpe = trunk_layer(xp, cond_ctx, lp, None, None)
        ckv_list.append(ckv)
        kpe_list.append(kpe)
        xs, _, _ = trunk_layer(xs, c, lp, rope, (cache_ckv[:, l], cache_kpe[:, l]))
    new_ckv = jnp.stack(ckv_list, axis=1)
    new_kpe = jnp.stack(kpe_list, axis=1)
    return (xp, xs, new_ckv, new_kpe)
```

```python
import functools
import math

import numpy as np
import jax
import jax.numpy as jnp
from jax import lax
from jax.experimental import pallas as pl
from jax.experimental.pallas import tpu as pltpu

D_MODEL = 1024
GRID_W = 64
N_HEADS = 8
QK_NOPE = 64
ROPE_DIM = 32
V_HEAD = 64
QK_HEAD = QK_NOPE + ROPE_DIM
Q_RANK = 384
KV_RANK = 256
ROPE_THETA = 10000.0
FNET_GROUPS = 4
FNET_GC = 128
FNET_WIDTH = FNET_GROUPS * FNET_GC
POOL_WINDOWS = (2, 4, 8, 16)
POOL_GC = 128
POOL_WIDTH = len(POOL_WINDOWS) * POOL_GC
N_BRANCHES = 3
N_EXPERTS = 32
TOP_K = 4
D_FF = D_MODEL
SWIGLU_LIMIT = 7.0
SWIGLU_ALPHA = 1.702
RMS_EPS = 1e-6

LANES = 128
HEAD_PAD = LANES
QK_PAD = N_HEADS * HEAD_PAD
V_WIDTH = N_HEADS * V_HEAD
ROPE_HALF = ROPE_DIM // 2
KPE_PAD = LANES
COL_KV = Q_RANK
COL_KPE = Q_RANK + KV_RANK
COL_F = COL_KPE + KPE_PAD
COL_P = COL_F + FNET_WIDTH
COL_G = COL_P + POOL_WIDTH
IN_PAD_WIDTH = COL_G + N_BRANCHES * D_MODEL
ROUTER_PAD = LANES
NEG_BIG = -1e30

TOKEN_TILE = 256
ATTN_Q_TILE = 256
FNET_ROW_TILE = 512
POOL_CHUNK = 256
POOL_HALO = 16
MOE_BLOCK = 256
COMBINE_TILE = 128
MOD_COL_TILE = 1536
VMEM_LIMIT = 56 * 1024 * 1024

F32 = jnp.float32
BF16 = jnp.bfloat16


def _cparams(*sem):
    return pltpu.CompilerParams(dimension_semantics=sem, vmem_limit_bytes=VMEM_LIMIT)


def _dot(a, b):
    return jnp.dot(a, b, preferred_element_type=F32)


def _rms(x):
    return x * lax.rsqrt(jnp.mean(x * x, axis=-1, keepdims=True) + RMS_EPS)


def _seg_rms_scale(x, m_ref, mt2_ref, inv_cnt):
    ss = _dot((x * x).astype(BF16), m_ref[...])
    r = lax.rsqrt(ss * inv_cnt + RMS_EPS)
    r_hi = r.astype(BF16)
    r_lo = (r - r_hi.astype(F32)).astype(BF16)
    return _dot(jnp.concatenate([r_hi, r_lo], axis=1), mt2_ref[...])


def _rope_chunk(xc, cosf, sinf, first_half):
    partner = jnp.where(first_half, pltpu.roll(xc, HEAD_PAD - ROPE_HALF, 1), pltpu.roll(xc, ROPE_HALF, 1))
    return xc * cosf + partner * sinf


def _write_keys(ckv, kpe_chunk, wk_ref, wv_ref, gk_ref, gkpe_ref, mk_ref, mkt2_ref, rope, k_out, v_out):
    cb = ckv.astype(BF16)
    kn = _dot(cb, wk_ref[...])
    kn = kn * _seg_rms_scale(kn, mk_ref, mkt2_ref, 1.0 / QK_NOPE) * gk_ref[...]
    v_out[...] = _dot(cb, wv_ref[...]).astype(v_out.dtype)
    ms = jnp.sum(kpe_chunk * kpe_chunk, axis=-1, keepdims=True) * (1.0 / ROPE_DIM)
    kp = kpe_chunk * lax.rsqrt(ms + RMS_EPS) * gkpe_ref[...]
    if rope is not None:
        kp = _rope_chunk(kp, *rope)
    for h in range(N_HEADS):
        sl = slice(h * HEAD_PAD, (h + 1) * HEAD_PAD)
        k_out[:, sl] = (kn[:, sl] + kp).astype(k_out.dtype)


def _mod_kernel(c_ref, w_ref, b_ref, o_ref):
    c = c_ref[...]
    s = c * jax.nn.sigmoid(c)
    o_ref[...] = _dot(s.astype(BF16), w_ref[...].astype(BF16)) + b_ref[...]


def _modulation(cond8, w_mod, b_mod):
    depth = w_mod.shape[0]
    n = w_mod.shape[2]
    return pl.pallas_call(
        _mod_kernel,
        out_shape=jax.ShapeDtypeStruct((depth, 8, n), F32),
        grid=(depth, n // MOD_COL_TILE),
        in_specs=[
            pl.BlockSpec((8, D_MODEL), lambda l, j: (0, 0)),
            pl.BlockSpec((None, D_MODEL, MOD_COL_TILE), lambda l, j: (l, 0, j)),
            pl.BlockSpec((None, 1, MOD_COL_TILE), lambda l, j: (l, 0, j)),
        ],
        out_specs=pl.BlockSpec((None, 8, MOD_COL_TILE), lambda l, j: (l, 0, j)),
        compiler_params=_cparams("arbitrary", "arbitrary"),
        name="adaln_modulation",
    )(cond8, w_mod, b_mod.reshape(depth, 1, n))


def _in_kernel(x_ref, mod_ref, g1_ref, w_ref, qag_ref, kvag_ref, wq_ref, wk_ref, wv_ref,
               gq_ref, gk_ref, gkpe_ref, cos_ref, sin_ref, mq_ref, mqt2_ref, icq_ref, mk_ref, mkt2_ref,
               q_out, k_out, v_out, ckv_out, kpe_out, f_out, p_out, g_out):
    x = x_ref[...]
    shift = mod_ref[0:1, :]
    scale = mod_ref[1:2, :]
    h = _rms(x) * g1_ref[...] * (1.0 + scale) + shift
    hb = h.astype(BF16)

    za = _dot(hb, w_ref[:, 0:COL_F])
    q_lat = za[:, 0:COL_KV]
    kv_lat = za[:, COL_KV:COL_KPE]
    kpe_grp = za[:, COL_KPE:COL_F]
    ckv = _rms(kv_lat) * kvag_ref[...]
    ckv_out[...] = ckv
    kpe_out[...] = kpe_grp[:, 0:ROPE_DIM]

    cosf = cos_ref[...]
    sinf = sin_ref[...]
    lane = lax.broadcasted_iota(jnp.int32, cosf.shape, 1)
    rope = (cosf, sinf, lane < QK_NOPE + ROPE_HALF)

    qn = (_rms(q_lat) * qag_ref[...]).astype(BF16)
    q = _dot(qn, wq_ref[...])
    q = q * _seg_rms_scale(q, mq_ref, mqt2_ref, icq_ref[...]) * gq_ref[...]
    sm_scale = 1.0 / math.sqrt(QK_HEAD)
    for hd in range(N_HEADS):
        sl = slice(hd * HEAD_PAD, (hd + 1) * HEAD_PAD)
        q_out[:, sl] = (_rope_chunk(q[:, sl], *rope) * sm_scale).astype(q_out.dtype)

    kpe_chunk = pltpu.roll(kpe_grp, QK_NOPE, 1)
    _write_keys(ckv, kpe_chunk, wk_ref, wv_ref, gk_ref, gkpe_ref, mk_ref, mkt2_ref, rope, k_out, v_out)

    zb = _dot(hb, w_ref[:, COL_F:COL_G])
    f_out[...] = zb[:, 0:FNET_WIDTH].astype(f_out.dtype)
    p_out[...] = zb[:, FNET_WIDTH:]
    zc = _dot(hb, w_ref[:, COL_G:])
    g_out[...] = jax.nn.sigmoid(zc).astype(g_out.dtype)


def _const_spec(shape):
    nd = len(shape)
    return pl.BlockSpec(shape, lambda i, _n=nd: (0,) * _n)


def _layer_spec(shape, l):
    nd = len(shape)
    return pl.BlockSpec((None,) + tuple(shape), lambda i, _l=l, _n=nd: (_l,) + (0,) * _n)


def _in_projection(x, mods, l, wts, consts, tok):
    t = x.shape[0]
    tm = TOKEN_TILE
    n_tiles = t // tm
    p_tiles = tok["t_prompt"] // tm
    s_tiles = tok["dec_seq"] // tm
    rope_blocks = tok["dec_seq"] // tm

    def mod_idx(i):
        row = jnp.where(i < p_tiles, 0, 1 + (i - p_tiles) // s_tiles)
        return (l, row, 0, 0)

    def rope_idx(i):
        return (jnp.where(i < p_tiles, rope_blocks, (i - p_tiles) % s_tiles), 0)

    row = lambda w: pl.BlockSpec((tm, w), lambda i: (i, 0))
    in_specs = [
        row(D_MODEL),
        pl.BlockSpec((None, None, 6, D_MODEL), mod_idx),
        _layer_spec((1, D_MODEL), l),
        _layer_spec((D_MODEL, IN_PAD_WIDTH), l),
        _layer_spec((1, Q_RANK), l),
        _layer_spec((1, KV_RANK), l),
        _layer_spec((Q_RANK, QK_PAD), l),
        _layer_spec((KV_RANK, QK_PAD), l),
        _layer_spec((KV_RANK, V_WIDTH), l),
        _layer_spec((1, QK_PAD), l),
        _layer_spec((1, QK_PAD), l),
        _layer_spec((1, HEAD_PAD), l),
        pl.BlockSpec((tm, HEAD_PAD), rope_idx),
        pl.BlockSpec((tm, HEAD_PAD), rope_idx),
        _const_spec((QK_PAD, LANES)),
        _const_spec((2 * LANES, QK_PAD)),
        _const_spec((1, LANES)),
        _const_spec((QK_PAD, LANES)),
        _const_spec((2 * LANES, QK_PAD)),
    ]
    out_shape = [
        jax.ShapeDtypeStruct((t, QK_PAD), BF16),
        jax.ShapeDtypeStruct((t, QK_PAD), BF16),
        jax.ShapeDtypeStruct((t, V_WIDTH), BF16),
        jax.ShapeDtypeStruct((t, KV_RANK), F32),
        jax.ShapeDtypeStruct((t, ROPE_DIM), F32),
        jax.ShapeDtypeStruct((t, FNET_WIDTH), BF16),
        jax.ShapeDtypeStruct((t, POOL_WIDTH), F32),
        jax.ShapeDtypeStruct((t, N_BRANCHES * D_MODEL), BF16),
    ]
    out_specs = [row(s.shape[1]) for s in out_shape]
    return pl.pallas_call(
        _in_kernel,
        out_shape=out_shape,
        grid=(n_tiles,),
        in_specs=in_specs,
        out_specs=out_specs,
        compiler_params=_cparams("arbitrary"),
        name="in_projection",
    )(x, mods, wts["norm1_g"], wts["w_in"], wts["q_a_g"], wts["kv_a_g"], wts["w_q"], wts["w_k"], wts["w_v"],
      wts["g_q"], wts["g_k"], wts["g_kpe"], consts["cosf"], consts["sinf"],
      consts["m_q"], consts["mt2_q"], consts["inv_cnt_q"], consts["m_k"], consts["mt2_k"])


def _ctx_keys_kernel(ckv_ref, kpe_ref, wk_ref, wv_ref, gk_ref, gkpe_ref, mk_ref, mkt2_ref, k_out, v_out):
    _write_keys(ckv_ref[...], kpe_ref[...], wk_ref, wv_ref, gk_ref, gkpe_ref, mk_ref, mkt2_ref, None, k_out, v_out)


def _ctx_keys(cache_ckv, cache_kpe_pad, wts, consts):
    nb, depth, past, _ = cache_ckv.shape
    lw = lambda shape: pl.BlockSpec((None,) + shape, lambda l, b: (l,) + (0,) * len(shape))
    cs = lambda shape: pl.BlockSpec(shape, lambda l, b: (0,) * len(shape))
    return pl.pallas_call(
        _ctx_keys_kernel,
        out_shape=[jax.ShapeDtypeStruct((depth, nb * past, QK_PAD), BF16),
                   jax.ShapeDtypeStruct((depth, nb * past, V_WIDTH), BF16)],
        grid=(depth, nb),
        in_specs=[
            pl.BlockSpec((None, None, past, KV_RANK), lambda l, b: (b, l, 0, 0)),
            pl.BlockSpec((None, None, past, HEAD_PAD), lambda l, b: (b, l, 0, 0)),
            lw((KV_RANK, QK_PAD)), lw((KV_RANK, V_WIDTH)), lw((1, QK_PAD)), lw((1, HEAD_PAD)),
            cs((QK_PAD, LANES)), cs((2 * LANES, QK_PAD)),
        ],
        out_specs=[pl.BlockSpec((None, past, QK_PAD), lambda l, b: (l, b, 0)),
                   pl.BlockSpec((None, past, V_WIDTH), lambda l, b: (l, b, 0))],
        compiler_params=_cparams("arbitrary", "arbitrary"),
        name="context_keys",
    )(cache_ckv, cache_kpe_pad, wts["w_k"], wts["w_v"], wts["g_k"], wts["g_kpe"], consts["m_k"], consts["mt2_k"])


def _attn_kernel(*refs, n_parts):
    q_ref = refs[0]
    k_refs = refs[1:1 + 2 * n_parts:2]
    v_refs = refs[2:2 + 2 * n_parts:2]
    o_ref = refs[1 + 2 * n_parts]
    lane = lax.broadcasted_iota(jnp.int32, (q_ref.shape[0], 2 * V_HEAD), 1)
    for pair in range(N_HEADS // 2):
        vsl = slice(pair * 2 * V_HEAD, (pair + 1) * 2 * V_HEAD)
        outs = []
        for hd in (2 * pair, 2 * pair + 1):
            sl = slice(hd * HEAD_PAD, (hd + 1) * HEAD_PAD)
            qh = q_ref[:, sl]
            ss = [lax.dot_general(qh, k[:, sl], (((1,), (1,)), ((), ())), preferred_element_type=F32)
                  for k in k_refs]
            m = functools.reduce(jnp.maximum, [jnp.max(s, axis=-1, keepdims=True) for s in ss])
            es = [jnp.exp(s - m) for s in ss]
            den = functools.reduce(jnp.add, [jnp.sum(e, axis=-1, keepdims=True) for e in es])
            acc = functools.reduce(jnp.add, [_dot(e.astype(BF16), v[:, vsl]) for e, v in zip(es, v_refs)])
            outs.append(acc / den)
        o_ref[:, vsl] = jnp.where(lane < V_HEAD, outs[0], outs[1]).astype(o_ref.dtype)


def _attention(q, kv_parts, n_seq, seq_len, q_row0):
    tq = min(ATTN_Q_TILE, seq_len)
    nq = seq_len // tq
    qb0 = q_row0 // tq
    in_specs = [pl.BlockSpec((tq, QK_PAD), lambda b, i: (qb0 + b * nq + i, 0))]
    args = [q]
    for k, v, blk0, rows in kv_parts:
        in_specs.append(pl.BlockSpec((rows, QK_PAD), lambda b, i, _o=blk0: (_o + b, 0)))
        in_specs.append(pl.BlockSpec((rows, V_WIDTH), lambda b, i, _o=blk0: (_o + b, 0)))
        args += [k, v]
    return pl.pallas_call(
        functools.partial(_attn_kernel, n_parts=len(kv_parts)),
        out_shape=jax.ShapeDtypeStruct((n_seq * seq_len, V_WIDTH), BF16),
        grid=(n_seq, nq),
        in_specs=in_specs,
        out_specs=pl.BlockSpec((tq, V_WIDTH), lambda b, i: (b * nq + i, 0)),
        compiler_params=_cparams("arbitrary", "arbitrary"),
        name="attention",
    )(*args)


def _fnet_kernel(f_ref, cs_ref, cl_ref, sl_ref, o_ref, top_ref, bot_ref, *, norm):
    @pl.when(pl.program_id(1) == 0)
    def _():
        for g in range(FNET_GROUPS):
            sl = slice(g * FNET_GC, (g + 1) * FNET_GC)
            a = _dot(f_ref[:, sl], cs_ref[...])
            top_ref[:, sl] = a[:, :FNET_GC].astype(BF16)
            bot_ref[:, sl] = a[:, FNET_GC:].astype(BF16)

    y = _dot(cl_ref[...], top_ref[...]) - _dot(sl_ref[...], bot_ref[...])
    o_ref[...] = (y * norm).astype(o_ref.dtype)


def _fourier(f_in, n_seq, seq_len, row0, tabs):
    tr = min(FNET_ROW_TILE, seq_len)
    nj = seq_len // tr
    sb0 = row0 // seq_len
    return pl.pallas_call(
        functools.partial(_fnet_kernel, norm=1.0 / math.sqrt(seq_len * FNET_GC)),
        out_shape=jax.ShapeDtypeStruct((n_seq * seq_len, FNET_WIDTH), BF16),
        grid=(n_seq, nj),
        in_specs=[
            pl.BlockSpec((seq_len, FNET_WIDTH), lambda b, j: (sb0 + b, 0)),
            pl.BlockSpec((FNET_GC, 2 * FNET_GC), lambda b, j: (0, 0)),
            pl.BlockSpec((tr, seq_len), lambda b, j: (j, 0)),
            pl.BlockSpec((tr, seq_len), lambda b, j: (j, 0)),
        ],
        out_specs=pl.BlockSpec((tr, FNET_WIDTH), lambda b, j: (b * nj + j, 0)),
        scratch_shapes=[pltpu.VMEM((seq_len, FNET_WIDTH), BF16), pltpu.VMEM((seq_len, FNET_WIDTH), BF16)],
        compiler_params=_cparams("arbitrary", "arbitrary"),
        name="fourier_mix",
    )(f_in, tabs["chan"], tabs["cos"], tabs["sin"])


def _pool_kernel(p_ref, wg_ref, ps_ref, o_ref, pad_ref):
    seq_len = p_ref.shape[0]
    zeros = jnp.zeros((POOL_HALO, POOL_WIDTH), F32)
    pad_ref[0:POOL_HALO, :] = zeros
    pad_ref[POOL_HALO + seq_len:, :] = zeros
    pad_ref[POOL_HALO:POOL_HALO + seq_len, :] = p_ref[...]
    ch = min(POOL_CHUNK, seq_len)
    for c in range(seq_len // ch):
        t = lax.broadcasted_iota(jnp.int32, (ch, 1), 0) + c * ch
        for g, w in enumerate(POOL_WINDOWS):
            half = w // 2
            sl = slice(g * POOL_GC, (g + 1) * POOL_GC)
            acc = None
            for j in range(-half, half):
                r0 = POOL_HALO + c * ch + j
                part = pad_ref[r0:r0 + ch, sl]
                acc = part if acc is None else acc + part
            cnt = (jnp.minimum(t + half, seq_len) - jnp.maximum(t - half, 0)).astype(F32)
            pooled = acc / cnt - p_ref[c * ch:(c + 1) * ch, sl]
            mixed = _dot(pooled.astype(BF16), wg_ref[g]) * ps_ref[:, sl]
            o_ref[c * ch:(c + 1) * ch, sl] = mixed.astype(o_ref.dtype)


def _pooling(p_in, n_seq, seq_len, row0, l, wts):
    sb0 = row0 // seq_len
    g = len(POOL_WINDOWS)
    return pl.pallas_call(
        _pool_kernel,
        out_shape=jax.ShapeDtypeStruct((n_seq * seq_len, POOL_WIDTH), BF16),
        grid=(n_seq,),
        in_specs=[
            pl.BlockSpec((seq_len, POOL_WIDTH), lambda b: (sb0 + b, 0)),
            pl.BlockSpec((None, g, POOL_GC, POOL_GC), lambda b: (l, 0, 0, 0)),
            pl.BlockSpec((None, 1, POOL_WIDTH), lambda b: (l, 0, 0)),
        ],
        out_specs=pl.BlockSpec((seq_len, POOL_WIDTH), lambda b: (b, 0)),
        scratch_shapes=[pltpu.VMEM((seq_len + 2 * POOL_HALO, POOL_WIDTH), F32)],
        compiler_params=_cparams("arbitrary"),
        name="pool_mix",
    )(p_in, wts["w_pool_grp"], wts["pool_scale"])


def _merge_kernel(x_ref, mod_ref, a_ref, f_ref, p_ref, g_ref, wa_ref, wf_ref, wp_ref, wo_ref,
                  g2_ref, wr_ref, br_ref, x1_out, h2_out, te_out, tw_out):
    a = _dot(a_ref[...], wa_ref[...])
    f = _dot(f_ref[...], wf_ref[...])
    p = _dot(p_ref[...], wp_ref[...])
    merged = (g_ref[:, 0:D_MODEL].astype(F32) * a
              + g_ref[:, D_MODEL:2 * D_MODEL].astype(F32) * f
              + g_ref[:, 2 * D_MODEL:].astype(F32) * p)
    mix = _dot(merged.astype(BF16), wo_ref[...])
    gate1 = mod_ref[2:3, :]
    shift2 = mod_ref[3:4, :]
    scale2 = mod_ref[4:5, :]
    x1 = x_ref[...] + gate1 * mix
    x1_out[...] = x1
    h2 = _rms(x1) * g2_ref[...] * (1.0 + scale2) + shift2
    h2_out[...] = h2

    h_hi = h2.astype(BF16)
    h_lo = (h2 - h_hi.astype(F32)).astype(BF16)
    lhs = jnp.concatenate([h_hi, h_lo, h_hi], axis=1)
    logits = _dot(lhs, wr_ref[...]) + br_ref[...]

    lane = lax.broadcasted_iota(jnp.int32, logits.shape, 1)
    work = logits
    vals, idxs = [], []
    for _ in range(TOP_K):
        m = jnp.max(work, axis=-1, keepdims=True)
        idx = jnp.min(jnp.where(work == m, lane, ROUTER_PAD), axis=-1, keepdims=True)
        vals.append(m)
        idxs.append(idx)
        work = jnp.where(lane == idx, -jnp.inf, work)
    es = [jnp.exp(v - vals[0]) for v in vals]
    den = functools.reduce(jnp.add, es)
    te = jnp.zeros(logits.shape, jnp.int32)
    tw = jnp.zeros(logits.shape, F32)
    for k in range(TOP_K):
        te = jnp.where(lane == k, idxs[k], te)
        tw = jnp.where(lane == k, es[k] / den, tw)
    te_out[...] = te
    tw_out[...] = tw


def _merge(x, mods, attn, fnet, pool, gates, l, wts, tok):
    t = x.shape[0]
    tm = TOKEN_TILE
    p_tiles = tok["t_prompt"] // tm
    s_tiles = tok["dec_seq"] // tm

    def mod_idx(i):
        row = jnp.where(i < p_tiles, 0, 1 + (i - p_tiles) // s_tiles)
        return (l, row, 0, 0)

    row = lambda w: pl.BlockSpec((tm, w), lambda i: (i, 0))
    out_shape = [
        jax.ShapeDtypeStruct((t, D_MODEL), F32),
        jax.ShapeDtypeStruct((t, D_MODEL), F32),
        jax.ShapeDtypeStruct((t, ROUTER_PAD), jnp.int32),
        jax.ShapeDtypeStruct((t, ROUTER_PAD), F32),
    ]
    return pl.pallas_call(
        _merge_kernel,
        out_shape=out_shape,
        grid=(t // tm,),
        in_specs=[
            row(D_MODEL),
            pl.BlockSpec((None, None, 6, D_MODEL), mod_idx),
            row(V_WIDTH), row(FNET_WIDTH), row(POOL_WIDTH), row(N_BRANCHES * D_MODEL),
            _layer_spec((V_WIDTH, D_MODEL), l),
            _layer_spec((FNET_WIDTH, D_MODEL), l),
            _layer_spec((POOL_WIDTH, D_MODEL), l),
            _layer_spec((D_MODEL, D_MODEL), l),
            _layer_spec((1, D_MODEL), l),
            _layer_spec((3 * D_MODEL, ROUTER_PAD), l),
            _layer_spec((1, ROUTER_PAD), l),
        ],
        out_specs=[row(s.shape[1]) for s in out_shape],
        compiler_params=_cparams("arbitrary"),
        name="merge_router",
    )(x, mods, attn, fnet, pool, gates, wts["w_attn_o"], wts["w_fnet_o"], wts["w_pool_o"], wts["w_out"],
      wts["norm2_g"], wts["w_router3"], wts["b_router"])


def _row_copy(src, src_row, dst, dst_row, sem):
    return pltpu.make_async_copy(src.at[pl.ds(src_row, 1), :], dst.at[pl.ds(dst_row, 1), :], sem)


def _dispatch_kernel(slot_ref, h_ref, init_ref, xs_ref, sem):
    del init_ref
    rows = h_ref.shape[0]

    def issue(r, carry):
        for k in range(TOP_K):
            _row_copy(h_ref, r, xs_ref, slot_ref[0, 0, r * TOP_K + k], sem).start()
        return carry

    def drain(r, carry):
        for k in range(TOP_K):
            _row_copy(h_ref, 0, xs_ref, 0, sem).wait()
        return carry

    lax.fori_loop(0, rows, issue, 0)
    lax.fori_loop(0, rows, drain, 0)


def _dispatch(h2, slot, n_slots):
    t = h2.shape[0]
    tm = TOKEN_TILE
    slot3 = slot.reshape(t // tm, 1, tm * TOP_K)
    return pl.pallas_call(
        _dispatch_kernel,
        out_shape=jax.ShapeDtypeStruct((n_slots, D_MODEL), F32),
        grid=(t // tm,),
        in_specs=[
            pl.BlockSpec((1, 1, tm * TOP_K), lambda i: (i, 0, 0), memory_space=pltpu.SMEM),
            pl.BlockSpec((tm, D_MODEL), lambda i: (i, 0)),
            pl.BlockSpec(memory_space=pl.ANY),
        ],
        out_specs=pl.BlockSpec(memory_space=pl.ANY),
        scratch_shapes=[pltpu.SemaphoreType.DMA(())],
        input_output_aliases={2: 0},
        compiler_params=_cparams("arbitrary"),
        name="moe_dispatch",
    )(slot3, h2, jnp.zeros((n_slots, D_MODEL), F32))


def _expert_kernel(be_ref, nb_ref, xs_ref, wgu_ref, bgu_ref, wd_ref, bd_ref, y_ref, wgu_bf, wd_bf):
    b = pl.program_id(0)

    @pl.when(b < nb_ref[0])
    def _():
        prev = be_ref[jnp.maximum(b - 1, 0)]

        @pl.when((b == 0) | (be_ref[b] != prev))
        def _():
            wgu_bf[...] = wgu_ref[...].astype(BF16)
            wd_bf[...] = wd_ref[...].astype(BF16)

        gu = _dot(xs_ref[...].astype(BF16), wgu_bf[...]) + bgu_ref[...]
        glu = jnp.minimum(gu[:, :D_FF], SWIGLU_LIMIT)
        lin = jnp.clip(gu[:, D_FF:], -SWIGLU_LIMIT, SWIGLU_LIMIT)
        act = glu * jax.nn.sigmoid(SWIGLU_ALPHA * glu) * (lin + 1.0)
        y_ref[...] = _dot(act.astype(BF16), wd_bf[...]) + bd_ref[...]

    @pl.when(b >= nb_ref[0])
    def _():
        y_ref[...] = jnp.zeros(y_ref.shape, y_ref.dtype)


def _experts(xs, block_exp, n_used, l, w_gu, b_gu4, w_down, b_down4):
    n_slots = xs.shape[0]
    bm = MOE_BLOCK
    n_blocks = n_slots // bm

    def blk(b, be, nb):
        return jnp.minimum(b, nb[0] - 1)

    grid_spec = pltpu.PrefetchScalarGridSpec(
        num_scalar_prefetch=2,
        grid=(n_blocks,),
        in_specs=[
            pl.BlockSpec((bm, D_MODEL), lambda b, be, nb: (blk(b, be, nb), 0)),
            pl.BlockSpec((None, None, D_MODEL, 2 * D_FF), lambda b, be, nb: (l, be[blk(b, be, nb)], 0, 0)),
            pl.BlockSpec((None, None, 1, 2 * D_FF), lambda b, be, nb: (l, be[blk(b, be, nb)], 0, 0)),
            pl.BlockSpec((None, None, D_FF, D_MODEL), lambda b, be, nb: (l, be[blk(b, be, nb)], 0, 0)),
            pl.BlockSpec((None, None, 1, D_MODEL), lambda b, be, nb: (l, be[blk(b, be, nb)], 0, 0)),
        ],
        out_specs=pl.BlockSpec((bm, D_MODEL), lambda b, be, nb: (b, 0)),
        scratch_shapes=[pltpu.VMEM((D_MODEL, 2 * D_FF), BF16), pltpu.VMEM((D_FF, D_MODEL), BF16)],
    )
    return pl.pallas_call(
        _expert_kernel,
        out_shape=jax.ShapeDtypeStruct((n_slots, D_MODEL), F32),
        grid_spec=grid_spec,
        compiler_params=_cparams("arbitrary"),
        name="moe_experts",
    )(block_exp, n_used, xs, w_gu, b_gu4, w_down, b_down4)


def _combine_kernel(slot_ref, tw_ref, x_ref, mod_ref, y_ref, o_ref, buf, sem):
    rows = x_ref.shape[0]

    def issue(r, carry):
        for k in range(TOP_K):
            _row_copy(y_ref, slot_ref[0, 0, r * TOP_K + k], buf.at[k], r, sem).start()
        return carry

    def drain(r, carry):
        for k in range(TOP_K):
            _row_copy(y_ref, 0, buf.at[k], 0, sem).wait()
        return carry

    lax.fori_loop(0, rows, issue, 0)
    lax.fori_loop(0, rows, drain, 0)
    acc = tw_ref[:, 0:1] * buf[0]
    for k in range(1, TOP_K):
        acc = acc + tw_ref[:, k:k + 1] * buf[k]
    gate2 = mod_ref[5:6, :]
    o_ref[...] = x_ref[...] + gate2 * acc


def _combine(y, slot, tw, x1, mods, l, tok):
    t = x1.shape[0]
    tc = COMBINE_TILE
    p_tiles = tok["t_prompt"] // tc
    s_tiles = tok["dec_seq"] // tc

    def mod_idx(i):
        row = jnp.where(i < p_tiles, 0, 1 + (i - p_tiles) // s_tiles)
        return (l, row, 0, 0)

    slot3 = slot.reshape(t // tc, 1, tc * TOP_K)
    return pl.pallas_call(
        _combine_kernel,
        out_shape=jax.ShapeDtypeStruct((t, D_MODEL), F32),
        grid=(t // tc,),
        in_specs=[
            pl.BlockSpec((1, 1, tc * TOP_K), lambda i: (i, 0, 0), memory_space=pltpu.SMEM),
            pl.BlockSpec((tc, ROUTER_PAD), lambda i: (i, 0)),
            pl.BlockSpec((tc, D_MODEL), lambda i: (i, 0)),
            pl.BlockSpec((None, None, 6, D_MODEL), mod_idx),
            pl.BlockSpec(memory_space=pl.ANY),
        ],
        out_specs=pl.BlockSpec((tc, D_MODEL), lambda i: (i, 0)),
        scratch_shapes=[pltpu.VMEM((TOP_K, tc, D_MODEL), F32), pltpu.SemaphoreType.DMA(())],
        compiler_params=_cparams("arbitrary"),
        name="moe_combine",
    )(slot3, tw, x1, mods, y)


def _route(te_pad, n_slots):
    t = te_pad.shape[0]
    flat_e = te_pad[:, :TOP_K].reshape(t * TOP_K)
    onehot = (flat_e[:, None] == jnp.arange(N_EXPERTS, dtype=jnp.int32)[None, :]).astype(jnp.int32)
    csum = jnp.cumsum(onehot, axis=0)
    rank = jnp.sum(csum * onehot, axis=1) - 1
    counts = csum[-1]
    padded = (counts + MOE_BLOCK - 1) // MOE_BLOCK * MOE_BLOCK
    padded_end = jnp.cumsum(padded)
    padded_start = padded_end - padded
    slot = jnp.sum(onehot * padded_start[None, :], axis=1) + rank
    n_blocks = n_slots // MOE_BLOCK
    block_exp = jnp.minimum(
        jnp.sum((padded_end[None, :] <= (jnp.arange(n_blocks, dtype=jnp.int32) * MOE_BLOCK)[:, None]).astype(jnp.int32),
                axis=1),
        N_EXPERTS - 1).astype(jnp.int32)
    n_used = (padded_end[-1:] // MOE_BLOCK).astype(jnp.int32)
    return slot.astype(jnp.int32).reshape(t, TOP_K), block_exp, n_used


def _segment_matrices():
    m_q = np.zeros((QK_PAD, LANES), np.float32)
    m_k = np.zeros((QK_PAD, LANES), np.float32)
    inv_cnt_q = np.ones((1, LANES), np.float32)
    for h in range(N_HEADS):
        m_q[h * HEAD_PAD:h * HEAD_PAD + QK_NOPE, 2 * h] = 1.0
        m_q[h * HEAD_PAD + QK_NOPE:h * HEAD_PAD + QK_HEAD, 2 * h + 1] = 1.0
        inv_cnt_q[0, 2 * h] = 1.0 / QK_NOPE
        inv_cnt_q[0, 2 * h + 1] = 1.0 / ROPE_DIM
        m_k[h * HEAD_PAD:h * HEAD_PAD + QK_NOPE, h] = 1.0
    dup = lambda m: np.concatenate([m.T, m.T], axis=0)
    return dict(m_q=jnp.asarray(m_q, BF16), mt2_q=jnp.asarray(dup(m_q), BF16), inv_cnt_q=jnp.asarray(inv_cnt_q),
                m_k=jnp.asarray(m_k, BF16), mt2_k=jnp.asarray(dup(m_k), BF16))


def _rope_lane_tables(n_tokens, ident_rows):
    rows = n_tokens // GRID_W
    row = jnp.broadcast_to(jnp.arange(rows)[:, None], (rows, GRID_W)).reshape(n_tokens)
    col = jnp.broadcast_to(jnp.arange(GRID_W)[None, :], (rows, GRID_W)).reshape(n_tokens)
    n_freq = ROPE_DIM // 4
    inv = 1.0 / (ROPE_THETA ** (jnp.arange(n_freq, dtype=F32) / n_freq))
    ang = jnp.concatenate([row[:, None].astype(F32) * inv, col[:, None].astype(F32) * inv], axis=-1)
    cos, sin = jnp.cos(ang), jnp.sin(ang)
    ones = jnp.ones((n_tokens, QK_NOPE), F32)
    tail = HEAD_PAD - QK_HEAD
    cosf = jnp.concatenate([ones, cos, cos, jnp.ones((n_tokens, tail), F32)], axis=1)
    sinf = jnp.concatenate([0.0 * ones, -sin, sin, jnp.zeros((n_tokens, tail), F32)], axis=1)
    cosf = jnp.concatenate([cosf, jnp.ones((ident_rows, HEAD_PAD), F32)], axis=0)
    sinf = jnp.concatenate([sinf, jnp.zeros((ident_rows, HEAD_PAD), F32)], axis=0)
    return cosf, sinf


def _dft_tables(seq_len):
    def cs(n):
        k = np.arange(n, dtype=np.int64)
        ang = 2.0 * np.pi * ((k[:, None] * k[None, :]) % n).astype(np.float64) / n
        return np.cos(ang).astype(np.float32), np.sin(ang).astype(np.float32)

    cl, sl = cs(seq_len)
    cc, sc = cs(FNET_GC)
    return dict(cos=jnp.asarray(cl).astype(BF16), sin=jnp.asarray(sl).astype(BF16),
                chan=jnp.asarray(np.concatenate([cc, sc], axis=1)).astype(BF16))


def _layout_weights(w_in, w_q_b, w_kv_b, q_nope_g, q_rope_g, k_nope_g, k_rope_g, w_router, b_router):
    depth = w_in.shape[0]
    w_in_p = jnp.concatenate(
        [w_in[:, :, :COL_KPE + ROPE_DIM], jnp.zeros((depth, D_MODEL, KPE_PAD - ROPE_DIM), w_in.dtype),
         w_in[:, :, COL_KPE + ROPE_DIM:]], axis=2).astype(BF16)
    w_q = jnp.pad(w_q_b.reshape(depth, Q_RANK, N_HEADS, QK_HEAD),
                  ((0, 0), (0, 0), (0, 0), (0, HEAD_PAD - QK_HEAD))).reshape(depth, Q_RANK, QK_PAD).astype(BF16)
    kv = w_kv_b.reshape(depth, KV_RANK, N_HEADS, QK_NOPE + V_HEAD)
    w_k = jnp.pad(kv[..., :QK_NOPE], ((0, 0), (0, 0), (0, 0), (0, HEAD_PAD - QK_NOPE))
                  ).reshape(depth, KV_RANK, QK_PAD).astype(BF16)
    w_v = kv[..., QK_NOPE:].reshape(depth, KV_RANK, V_WIDTH).astype(BF16)
    zq = jnp.zeros((depth, HEAD_PAD - QK_HEAD), F32)
    g_q = jnp.tile(jnp.concatenate([q_nope_g, q_rope_g, zq], axis=1), (1, N_HEADS))[:, None, :]
    g_k = jnp.tile(jnp.concatenate([k_nope_g, jnp.zeros((depth, HEAD_PAD - QK_NOPE), F32)], axis=1),
                   (1, N_HEADS))[:, None, :]
    g_kpe = jnp.concatenate([jnp.zeros((depth, QK_NOPE), F32), k_rope_g, zq], axis=1)[:, None, :]
    w_hi = w_router.astype(BF16)
    w_lo = (w_router - w_hi.astype(F32)).astype(BF16)
    w_r3 = jnp.pad(jnp.concatenate([w_hi, w_hi, w_lo], axis=1), ((0, 0), (0, 0), (0, ROUTER_PAD - N_EXPERTS)))
    b_r = jnp.pad(b_router, ((0, 0), (0, ROUTER_PAD - N_EXPERTS)), constant_values=NEG_BIG)[:, None, :]
    return dict(w_in=w_in_p, w_q=w_q, w_k=w_k, w_v=w_v, g_q=g_q, g_k=g_k, g_kpe=g_kpe, w_router3=w_r3, b_router=b_r)


def kernel(x_prompt, x_sample, cache_ckv, cache_kpe, c, c_ctx, w_mod, b_mod, norm1_g, norm2_g, w_in, q_a_g, kv_a_g, w_q_b, w_kv_b, q_nope_g, q_rope_g, k_nope_g, k_rope_g, w_attn_o, w_fnet_o, w_pool_grp, pool_scale, w_pool_o, w_out, w_router, b_router, w_gu, b_gu, w_down, b_down):
    batch, seq, d = x_prompt.shape
    dec_batch, dec_seq, _ = x_sample.shape
    depth = w_mod.shape[0]
    past = cache_ckv.shape[2]
    t_prompt = batch * seq
    t_sample = dec_batch * dec_seq
    t = t_prompt + t_sample
    tok = dict(t_prompt=t_prompt, dec_seq=dec_seq)
    assert d == D_MODEL and dec_batch + 1 <= 8
    assert t_prompt % TOKEN_TILE == 0 and dec_seq % TOKEN_TILE == 0 and seq % COMBINE_TILE == 0
    assert past == seq and dec_seq % GRID_W == 0

    consts = _segment_matrices()
    consts["cosf"], consts["sinf"] = _rope_lane_tables(dec_seq, TOKEN_TILE)
    dft_p = _dft_tables(seq)
    dft_s = _dft_tables(dec_seq)

    wts = _layout_weights(w_in, w_q_b, w_kv_b, q_nope_g, q_rope_g, k_nope_g, k_rope_g, w_router, b_router)
    row3 = lambda a: a[:, None, :]
    wts.update(norm1_g=row3(norm1_g), norm2_g=row3(norm2_g), q_a_g=row3(q_a_g), kv_a_g=row3(kv_a_g),
               w_attn_o=w_attn_o.astype(BF16), w_fnet_o=w_fnet_o.astype(BF16), w_pool_o=w_pool_o.astype(BF16),
               w_out=w_out.astype(BF16), w_pool_grp=w_pool_grp.astype(BF16), pool_scale=row3(pool_scale))
    b_gu4 = b_gu[:, :, None, :]
    b_down4 = b_down[:, :, None, :]

    cond8 = jnp.concatenate([c_ctx[None, :], c, jnp.zeros((8 - 1 - dec_batch, d), F32)], axis=0)
    mods = _modulation(cond8, w_mod, b_mod).reshape(depth, 8, 6, D_MODEL)

    kpe_pad = jnp.pad(cache_kpe, ((0, 0), (0, 0), (0, 0), (QK_NOPE, HEAD_PAD - QK_HEAD)))
    k_ctx, v_ctx = _ctx_keys(cache_ckv, kpe_pad, wts, consts)

    n_assign = t * TOP_K
    n_slots = (n_assign + N_EXPERTS * (MOE_BLOCK - 1) + MOE_BLOCK - 1) // MOE_BLOCK * MOE_BLOCK

    x = jnp.concatenate([x_prompt.reshape(t_prompt, d), x_sample.reshape(t_sample, d)], axis=0)
    ckv_list, kpe_list = [], []
    for l in range(depth):
        q, k, v, ckv, kpe, f_in, p_in, gates = _in_projection(x, mods, l, wts, consts, tok)
        ckv_list.append(ckv[:t_prompt].reshape(batch, seq, KV_RANK))
        kpe_list.append(kpe[:t_prompt].reshape(batch, seq, ROPE_DIM))

        attn_p = _attention(q, [(k, v, 0, seq)], batch, seq, 0)
        attn_s = _attention(q, [(k, v, t_prompt // dec_seq, dec_seq), (k_ctx[l], v_ctx[l], 0, past)],
                            dec_batch, dec_seq, t_prompt)
        fnet = jnp.concatenate([_fourier(f_in, batch, seq, 0, dft_p),
                                _fourier(f_in, dec_batch, dec_seq, t_prompt, dft_s)], axis=0)
        pool = jnp.concatenate([_pooling(p_in, batch, seq, 0, l, wts),
                                _pooling(p_in, dec_batch, dec_seq, t_prompt, l, wts)], axis=0)
        attn = jnp.concatenate([attn_p, attn_s], axis=0)

        x1, h2, te_pad, tw_pad = _merge(x, mods, attn, fnet, pool, gates, l, wts, tok)
        slot, block_exp, n_used = _route(te_pad, n_slots)
        xs = _dispatch(h2, slot, n_slots)
        y = _experts(xs, block_exp, n_used, l, w_gu, b_gu4, w_down, b_down4)
        x = _combine(y, slot, tw_pad, x1, mods, l, tok)

    y_prompt = x[:t_prompt].reshape(batch, seq, d)
    y_sample = x[t_prompt:].reshape(dec_batch, dec_seq, d)
    return (y_prompt, y_sample, jnp.stack(ckv_list, axis=1), jnp.stack(kpe_list, axis=1))
```

```python
import functools
import math

import numpy as np
import jax
import jax.numpy as jnp
from jax import lax
from jax.experimental import pallas as pl
from jax.experimental.pallas import tpu as pltpu
from jax.experimental.pallas import tpu_sc as plsc

D_MODEL = 1024
GRID_W = 64
N_HEADS = 8
QK_NOPE = 64
ROPE_DIM = 32
V_HEAD = 64
QK_HEAD = QK_NOPE + ROPE_DIM
Q_RANK = 384
KV_RANK = 256
ROPE_THETA = 10000.0
FNET_GROUPS = 4
FNET_GC = 128
FNET_WIDTH = FNET_GROUPS * FNET_GC
POOL_WINDOWS = (2, 4, 8, 16)
POOL_GC = 128
POOL_WIDTH = len(POOL_WINDOWS) * POOL_GC
N_BRANCHES = 3
N_EXPERTS = 32
TOP_K = 4
D_FF = D_MODEL
SWIGLU_LIMIT = 7.0
SWIGLU_ALPHA = 1.702
RMS_EPS = 1e-6

LANES = 128
HEAD_PAD = LANES
QK_PAD = N_HEADS * HEAD_PAD
V_WIDTH = N_HEADS * V_HEAD
ROPE_HALF = ROPE_DIM // 2
KPE_PAD = LANES
COL_KV = Q_RANK
COL_KPE = Q_RANK + KV_RANK
COL_F = COL_KPE + KPE_PAD
COL_P = COL_F + FNET_WIDTH
COL_G = COL_P + POOL_WIDTH
IN_PAD_WIDTH = COL_G + N_BRANCHES * D_MODEL
ROUTER_PAD = LANES
NEG_BIG = -1e30

TOKEN_TILE = 512
ATTN_Q_TILE = 512
FNET_ROW_TILE = 512
POOL_CHUNK = 256
POOL_HALO = 16
MOE_BLOCK = 512
MOE_HALF = MOE_BLOCK // 2
COMBINE_TILE = 256
SC_GATHER_ROWS = 64
MOD_COL_TILE = 1536
VMEM_LIMIT = 56 * 1024 * 1024

F32 = jnp.float32
BF16 = jnp.bfloat16


def _cparams(*sem):
    return pltpu.CompilerParams(dimension_semantics=sem, vmem_limit_bytes=VMEM_LIMIT)


def _dot(a, b):
    return jnp.dot(a, b, preferred_element_type=F32)


def _rms(x):
    return x * lax.rsqrt(jnp.mean(x * x, axis=-1, keepdims=True) + RMS_EPS)


def _seg_rms_scale(x, m_ref, mt2_ref, inv_cnt):
    ss = _dot((x * x).astype(BF16), m_ref[...])
    r = lax.rsqrt(ss * inv_cnt + RMS_EPS)
    r_hi = r.astype(BF16)
    r_lo = (r - r_hi.astype(F32)).astype(BF16)
    return _dot(jnp.concatenate([r_hi, r_lo], axis=1), mt2_ref[...])


def _rope_chunk(xc, cosf, sinf, first_half):
    partner = jnp.where(first_half, pltpu.roll(xc, HEAD_PAD - ROPE_HALF, 1), pltpu.roll(xc, ROPE_HALF, 1))
    return xc * cosf + partner * sinf


def _write_keys(ckv, kpe_chunk, wk_ref, wv_ref, gk_ref, gkpe_ref, mk_ref, mkt2_ref, rope, k_out, v_out):
    cb = ckv.astype(BF16)
    kn = _dot(cb, wk_ref[...])
    kn = kn * _seg_rms_scale(kn, mk_ref, mkt2_ref, 1.0 / QK_NOPE) * gk_ref[...]
    v_out[...] = _dot(cb, wv_ref[...]).astype(v_out.dtype)
    ms = jnp.sum(kpe_chunk * kpe_chunk, axis=-1, keepdims=True) * (1.0 / ROPE_DIM)
    kp = kpe_chunk * lax.rsqrt(ms + RMS_EPS) * gkpe_ref[...]
    if rope is not None:
        kp = _rope_chunk(kp, *rope)
    for h in range(N_HEADS):
        sl = slice(h * HEAD_PAD, (h + 1) * HEAD_PAD)
        k_out[:, sl] = (kn[:, sl] + kp).astype(k_out.dtype)


def _mod_kernel(c_ref, w_ref, b_ref, o_ref):
    c = c_ref[...]
    s = c * jax.nn.sigmoid(c)
    o_ref[...] = _dot(s.astype(BF16), w_ref[...].astype(BF16)) + b_ref[...]


def _modulation(cond8, w_mod, b_mod):
    depth = w_mod.shape[0]
    n = w_mod.shape[2]
    return pl.pallas_call(
        _mod_kernel,
        out_shape=jax.ShapeDtypeStruct((depth, 8, n), F32),
        grid=(depth, n // MOD_COL_TILE),
        in_specs=[
            pl.BlockSpec((8, D_MODEL), lambda l, j: (0, 0)),
            pl.BlockSpec((None, D_MODEL, MOD_COL_TILE), lambda l, j: (l, 0, j)),
            pl.BlockSpec((None, 1, MOD_COL_TILE), lambda l, j: (l, 0, j)),
        ],
        out_specs=pl.BlockSpec((None, 8, MOD_COL_TILE), lambda l, j: (l, 0, j)),
        compiler_params=_cparams("arbitrary", "arbitrary"),
        name="adaln_modulation",
    )(cond8, w_mod, b_mod.reshape(depth, 1, n))


def _in_kernel(x_ref, mod_ref, g1_ref, w_ref, qag_ref, kvag_ref, wq_ref, wk_ref, wv_ref,
               gq_ref, gk_ref, gkpe_ref, cos_ref, sin_ref, mq_ref, mqt2_ref, icq_ref, mk_ref, mkt2_ref,
               q_out, k_out, v_out, ckv_out, kpe_out, f_out, p_out, g_out):
    x = x_ref[...]
    shift = mod_ref[0:1, :]
    scale = mod_ref[1:2, :]
    h = _rms(x) * g1_ref[...] * (1.0 + scale) + shift
    hb = h.astype(BF16)

    za = _dot(hb, w_ref[:, 0:COL_F])
    q_lat = za[:, 0:COL_KV]
    kv_lat = za[:, COL_KV:COL_KPE]
    kpe_grp = za[:, COL_KPE:COL_F]
    ckv = _rms(kv_lat) * kvag_ref[...]
    ckv_out[...] = ckv
    kpe_out[...] = kpe_grp[:, 0:ROPE_DIM]

    cosf = cos_ref[...]
    sinf = sin_ref[...]
    lane = lax.broadcasted_iota(jnp.int32, cosf.shape, 1)
    rope = (cosf, sinf, lane < QK_NOPE + ROPE_HALF)

    qn = (_rms(q_lat) * qag_ref[...]).astype(BF16)
    q = _dot(qn, wq_ref[...])
    q = q * _seg_rms_scale(q, mq_ref, mqt2_ref, icq_ref[...]) * gq_ref[...]
    sm_scale = 1.0 / math.sqrt(QK_HEAD)
    for hd in range(N_HEADS):
        sl = slice(hd * HEAD_PAD, (hd + 1) * HEAD_PAD)
        q_out[:, sl] = (_rope_chunk(q[:, sl], *rope) * sm_scale).astype(q_out.dtype)

    kpe_chunk = pltpu.roll(kpe_grp, QK_NOPE, 1)
    _write_keys(ckv, kpe_chunk, wk_ref, wv_ref, gk_ref, gkpe_ref, mk_ref, mkt2_ref, rope, k_out, v_out)

    zb = _dot(hb, w_ref[:, COL_F:COL_G])
    f_out[...] = zb[:, 0:FNET_WIDTH].astype(f_out.dtype)
    p_out[...] = zb[:, FNET_WIDTH:]
    zc = _dot(hb, w_ref[:, COL_G:])
    g_out[...] = jax.nn.sigmoid(zc).astype(g_out.dtype)


def _const_spec(shape):
    nd = len(shape)
    return pl.BlockSpec(shape, lambda i, _n=nd: (0,) * _n, pipeline_mode=pl.Buffered(1))


def _layer_spec(shape, l):
    nd = len(shape)
    return pl.BlockSpec((None,) + tuple(shape), lambda i, _l=l, _n=nd: (_l,) + (0,) * _n,
                        pipeline_mode=pl.Buffered(1))


def _in_projection(x, mods, l, wts, consts, tok):
    t = x.shape[0]
    tm = TOKEN_TILE
    n_tiles = t // tm
    p_tiles = tok["t_prompt"] // tm
    s_tiles = tok["dec_seq"] // tm
    rope_blocks = tok["dec_seq"] // tm

    def mod_idx(i):
        row = jnp.where(i < p_tiles, 0, 1 + (i - p_tiles) // s_tiles)
        return (l, row, 0, 0)

    def rope_idx(i):
        return (jnp.where(i < p_tiles, rope_blocks, (i - p_tiles) % s_tiles), 0)

    row = lambda w: pl.BlockSpec((tm, w), lambda i: (i, 0))
    in_specs = [
        row(D_MODEL),
        pl.BlockSpec((None, None, 6, D_MODEL), mod_idx),
        _layer_spec((1, D_MODEL), l),
        _layer_spec((D_MODEL, IN_PAD_WIDTH), l),
        _layer_spec((1, Q_RANK), l),
        _layer_spec((1, KV_RANK), l),
        _layer_spec((Q_RANK, QK_PAD), l),
        _layer_spec((KV_RANK, QK_PAD), l),
        _layer_spec((KV_RANK, V_WIDTH), l),
        _layer_spec((1, QK_PAD), l),
        _layer_spec((1, QK_PAD), l),
        _layer_spec((1, HEAD_PAD), l),
        pl.BlockSpec((tm, HEAD_PAD), rope_idx),
        pl.BlockSpec((tm, HEAD_PAD), rope_idx),
        _const_spec((QK_PAD, LANES)),
        _const_spec((2 * LANES, QK_PAD)),
        _const_spec((1, LANES)),
        _const_spec((QK_PAD, LANES)),
        _const_spec((2 * LANES, QK_PAD)),
    ]
    out_shape = [
        jax.ShapeDtypeStruct((t, QK_PAD), BF16),
        jax.ShapeDtypeStruct((t, QK_PAD), BF16),
        jax.ShapeDtypeStruct((t, V_WIDTH), BF16),
        jax.ShapeDtypeStruct((t, KV_RANK), F32),
        jax.ShapeDtypeStruct((t, ROPE_DIM), F32),
        jax.ShapeDtypeStruct((t, FNET_WIDTH), BF16),
        jax.ShapeDtypeStruct((t, POOL_WIDTH), F32),
        jax.ShapeDtypeStruct((t, N_BRANCHES * D_MODEL), BF16),
    ]
    out_specs = [row(s.shape[1]) for s in out_shape]
    return pl.pallas_call(
        _in_kernel,
        out_shape=out_shape,
        grid=(n_tiles,),
        in_specs=in_specs,
        out_specs=out_specs,
        compiler_params=_cparams("arbitrary"),
        name="in_projection",
    )(x, mods, wts["norm1_g"], wts["w_in"], wts["q_a_g"], wts["kv_a_g"], wts["w_q"], wts["w_k"], wts["w_v"],
      wts["g_q"], wts["g_k"], wts["g_kpe"], consts["cosf"], consts["sinf"],
      consts["m_q"], consts["mt2_q"], consts["inv_cnt_q"], consts["m_k"], consts["mt2_k"])


def _ctx_keys_kernel(ckv_ref, kpe_ref, wk_ref, wv_ref, gk_ref, gkpe_ref, mk_ref, mkt2_ref, k_out, v_out):
    _write_keys(ckv_ref[...], kpe_ref[...], wk_ref, wv_ref, gk_ref, gkpe_ref, mk_ref, mkt2_ref, None, k_out, v_out)


def _ctx_keys(cache_ckv, cache_kpe_pad, wts, consts):
    nb, depth, past, _ = cache_ckv.shape
    lw = lambda shape: pl.BlockSpec((None,) + shape, lambda l, b: (l,) + (0,) * len(shape))
    cs = lambda shape: pl.BlockSpec(shape, lambda l, b: (0,) * len(shape))
    return pl.pallas_call(
        _ctx_keys_kernel,
        out_shape=[jax.ShapeDtypeStruct((depth, nb * past, QK_PAD), BF16),
                   jax.ShapeDtypeStruct((depth, nb * past, V_WIDTH), BF16)],
        grid=(depth, nb),
        in_specs=[
            pl.BlockSpec((None, None, past, KV_RANK), lambda l, b: (b, l, 0, 0)),
            pl.BlockSpec((None, None, past, HEAD_PAD), lambda l, b: (b, l, 0, 0)),
            lw((KV_RANK, QK_PAD)), lw((KV_RANK, V_WIDTH)), lw((1, QK_PAD)), lw((1, HEAD_PAD)),
            cs((QK_PAD, LANES)), cs((2 * LANES, QK_PAD)),
        ],
        out_specs=[pl.BlockSpec((None, past, QK_PAD), lambda l, b: (l, b, 0)),
                   pl.BlockSpec((None, past, V_WIDTH), lambda l, b: (l, b, 0))],
        compiler_params=_cparams("arbitrary", "arbitrary"),
        name="context_keys",
    )(cache_ckv, cache_kpe_pad, wts["w_k"], wts["w_v"], wts["g_k"], wts["g_kpe"], consts["m_k"], consts["mt2_k"])


def _attn_kernel(*refs, n_parts):
    q_ref = refs[0]
    k_refs = refs[1:1 + 2 * n_parts:2]
    v_refs = refs[2:2 + 2 * n_parts:2]
    o_ref = refs[1 + 2 * n_parts]
    lane = lax.broadcasted_iota(jnp.int32, (q_ref.shape[0], 2 * V_HEAD), 1)
    for pair in range(N_HEADS // 2):
        vsl = slice(pair * 2 * V_HEAD, (pair + 1) * 2 * V_HEAD)
        outs = []
        for hd in (2 * pair, 2 * pair + 1):
            sl = slice(hd * HEAD_PAD, (hd + 1) * HEAD_PAD)
            qh = q_ref[:, sl]
            ss = [lax.dot_general(qh, k[:, sl], (((1,), (1,)), ((), ())), preferred_element_type=F32)
                  for k in k_refs]
            m = functools.reduce(jnp.maximum, [jnp.max(s, axis=-1, keepdims=True) for s in ss])
            es = [jnp.exp(s - m) for s in ss]
            den = functools.reduce(jnp.add, [jnp.sum(e, axis=-1, keepdims=True) for e in es])
            acc = functools.reduce(jnp.add, [_dot(e.astype(BF16), v[:, vsl]) for e, v in zip(es, v_refs)])
            outs.append(acc / den)
        o_ref[:, vsl] = jnp.where(lane < V_HEAD, outs[0], outs[1]).astype(o_ref.dtype)


def _attention(q, kv_parts, n_seq, seq_len, q_row0):
    tq = min(ATTN_Q_TILE, seq_len)
    nq = seq_len // tq
    qb0 = q_row0 // tq
    in_specs = [pl.BlockSpec((tq, QK_PAD), lambda b, i: (qb0 + b * nq + i, 0))]
    args = [q]
    for k, v, blk0, rows in kv_parts:
        in_specs.append(pl.BlockSpec((rows, QK_PAD), lambda b, i, _o=blk0: (_o + b, 0)))
        in_specs.append(pl.BlockSpec((rows, V_WIDTH), lambda b, i, _o=blk0: (_o + b, 0)))
        args += [k, v]
    return pl.pallas_call(
        functools.partial(_attn_kernel, n_parts=len(kv_parts)),
        out_shape=jax.ShapeDtypeStruct((n_seq * seq_len, V_WIDTH), BF16),
        grid=(n_seq, nq),
        in_specs=in_specs,
        out_specs=pl.BlockSpec((tq, V_WIDTH), lambda b, i: (b * nq + i, 0)),
        compiler_params=_cparams("arbitrary", "arbitrary"),
        name="attention",
    )(*args)


def _fnet_kernel(f_ref, cs_ref, cl_ref, sl_ref, o_ref, top_ref, bot_ref, *, norm):
    @pl.when(pl.program_id(1) == 0)
    def _():
        for g in range(FNET_GROUPS):
            sl = slice(g * FNET_GC, (g + 1) * FNET_GC)
            a = _dot(f_ref[:, sl], cs_ref[...])
            top_ref[:, sl] = a[:, :FNET_GC].astype(BF16)
            bot_ref[:, sl] = a[:, FNET_GC:].astype(BF16)

    y = _dot(cl_ref[...], top_ref[...]) - _dot(sl_ref[...], bot_ref[...])
    o_ref[...] = (y * norm).astype(o_ref.dtype)


def _fourier(f_in, n_seq, seq_len, row0, tabs):
    tr = min(FNET_ROW_TILE, seq_len)
    nj = seq_len // tr
    sb0 = row0 // seq_len
    return pl.pallas_call(
        functools.partial(_fnet_kernel, norm=1.0 / math.sqrt(seq_len * FNET_GC)),
        out_shape=jax.ShapeDtypeStruct((n_seq * seq_len, FNET_WIDTH), BF16),
        grid=(n_seq, nj),
        in_specs=[
            pl.BlockSpec((seq_len, FNET_WIDTH), lambda b, j: (sb0 + b, 0)),
            pl.BlockSpec((FNET_GC, 2 * FNET_GC), lambda b, j: (0, 0)),
            pl.BlockSpec((tr, seq_len), lambda b, j: (j, 0)),
            pl.BlockSpec((tr, seq_len), lambda b, j: (j, 0)),
        ],
        out_specs=pl.BlockSpec((tr, FNET_WIDTH), lambda b, j: (b * nj + j, 0)),
        scratch_shapes=[pltpu.VMEM((seq_len, FNET_WIDTH), BF16), pltpu.VMEM((seq_len, FNET_WIDTH), BF16)],
        compiler_params=_cparams("arbitrary", "arbitrary"),
        name="fourier_mix",
    )(f_in, tabs["chan"], tabs["cos"], tabs["sin"])


def _pool_kernel(p_ref, wg_ref, ps_ref, o_ref, pad_ref):
    seq_len = p_ref.shape[0]
    zeros = jnp.zeros((POOL_HALO, POOL_WIDTH), F32)
    pad_ref[0:POOL_HALO, :] = zeros
    pad_ref[POOL_HALO + seq_len:, :] = zeros
    pad_ref[POOL_HALO:POOL_HALO + seq_len, :] = p_ref[...]
    ch = min(POOL_CHUNK, seq_len)
    for c in range(seq_len // ch):
        t = lax.broadcasted_iota(jnp.int32, (ch, 1), 0) + c * ch
        for g, w in enumerate(POOL_WINDOWS):
            half = w // 2
            sl = slice(g * POOL_GC, (g + 1) * POOL_GC)
            acc = None
            for j in range(-half, half):
                r0 = POOL_HALO + c * ch + j
                part = pad_ref[r0:r0 + ch, sl]
                acc = part if acc is None else acc + part
            cnt = (jnp.minimum(t + half, seq_len) - jnp.maximum(t - half, 0)).astype(F32)
            pooled = acc / cnt - p_ref[c * ch:(c + 1) * ch, sl]
            mixed = _dot(pooled.astype(BF16), wg_ref[g]) * ps_ref[:, sl]
            o_ref[c * ch:(c + 1) * ch, sl] = mixed.astype(o_ref.dtype)


def _pooling(p_in, n_seq, seq_len, row0, l, wts):
    sb0 = row0 // seq_len
    g = len(POOL_WINDOWS)
    return pl.pallas_call(
        _pool_kernel,
        out_shape=jax.ShapeDtypeStruct((n_seq * seq_len, POOL_WIDTH), BF16),
        grid=(n_seq,),
        in_specs=[
            pl.BlockSpec((seq_len, POOL_WIDTH), lambda b: (sb0 + b, 0)),
            pl.BlockSpec((None, g, POOL_GC, POOL_GC), lambda b: (l, 0, 0, 0)),
            pl.BlockSpec((None, 1, POOL_WIDTH), lambda b: (l, 0, 0)),
        ],
        out_specs=pl.BlockSpec((seq_len, POOL_WIDTH), lambda b: (b, 0)),
        scratch_shapes=[pltpu.VMEM((seq_len + 2 * POOL_HALO, POOL_WIDTH), F32)],
        compiler_params=_cparams("arbitrary"),
        name="pool_mix",
    )(p_in, wts["w_pool_grp"], wts["pool_scale"])


def _merge_kernel(x_ref, mod_ref, a_ref, f_ref, p_ref, g_ref, wa_ref, wf_ref, wp_ref, wo_ref,
                  g2_ref, wr_ref, br_ref, x1_out, h2_out, te_out, tw_out):
    a = _dot(a_ref[...], wa_ref[...])
    f = _dot(f_ref[...], wf_ref[...])
    p = _dot(p_ref[...], wp_ref[...])
    merged = (g_ref[:, 0:D_MODEL].astype(F32) * a
              + g_ref[:, D_MODEL:2 * D_MODEL].astype(F32) * f
              + g_ref[:, 2 * D_MODEL:].astype(F32) * p)
    mix = _dot(merged.astype(BF16), wo_ref[...])
    gate1 = mod_ref[2:3, :]
    shift2 = mod_ref[3:4, :]
    scale2 = mod_ref[4:5, :]
    x1 = x_ref[...] + gate1 * mix
    x1_out[...] = x1
    h2 = _rms(x1) * g2_ref[...] * (1.0 + scale2) + shift2
    h2_out[...] = h2

    h_hi = h2.astype(BF16)
    h_lo = (h2 - h_hi.astype(F32)).astype(BF16)
    lhs = jnp.concatenate([h_hi, h_lo, h_hi], axis=1)
    logits = _dot(lhs, wr_ref[...]) + br_ref[...]

    lane = lax.broadcasted_iota(jnp.int32, logits.shape, 1)
    work = logits
    vals, idxs = [], []
    for _ in range(TOP_K):
        m = jnp.max(work, axis=-1, keepdims=True)
        idx = jnp.min(jnp.where(work == m, lane, ROUTER_PAD), axis=-1, keepdims=True)
        vals.append(m)
        idxs.append(idx)
        work = jnp.where(lane == idx, -jnp.inf, work)
    es = [jnp.exp(v - vals[0]) for v in vals]
    den = functools.reduce(jnp.add, es)
    te = jnp.zeros(logits.shape, jnp.int32)
    tw = jnp.zeros(logits.shape, F32)
    for k in range(TOP_K):
        te = jnp.where(lane == k, idxs[k], te)
        tw = jnp.where(lane == k, es[k] / den, tw)
    te_out[...] = te
    tw_out[...] = tw


def _merge(x, mods, attn, fnet, pool, gates, l, wts, tok):
    t = x.shape[0]
    tm = TOKEN_TILE
    p_tiles = tok["t_prompt"] // tm
    s_tiles = tok["dec_seq"] // tm

    def mod_idx(i):
        row = jnp.where(i < p_tiles, 0, 1 + (i - p_tiles) // s_tiles)
        return (l, row, 0, 0)

    row = lambda w: pl.BlockSpec((tm, w), lambda i: (i, 0))
    out_shape = [
        jax.ShapeDtypeStruct((t, D_MODEL), F32),
        jax.ShapeDtypeStruct((t, D_MODEL), F32),
        jax.ShapeDtypeStruct((t, ROUTER_PAD), jnp.int32),
        jax.ShapeDtypeStruct((t, ROUTER_PAD), F32),
    ]
    return pl.pallas_call(
        _merge_kernel,
        out_shape=out_shape,
        grid=(t // tm,),
        in_specs=[
            row(D_MODEL),
            pl.BlockSpec((None, None, 6, D_MODEL), mod_idx),
            row(V_WIDTH), row(FNET_WIDTH), row(POOL_WIDTH), row(N_BRANCHES * D_MODEL),
            _layer_spec((V_WIDTH, D_MODEL), l),
            _layer_spec((FNET_WIDTH, D_MODEL), l),
            _layer_spec((POOL_WIDTH, D_MODEL), l),
            _layer_spec((D_MODEL, D_MODEL), l),
            _layer_spec((1, D_MODEL), l),
            _layer_spec((3 * D_MODEL, ROUTER_PAD), l),
            _layer_spec((1, ROUTER_PAD), l),
        ],
        out_specs=[row(s.shape[1]) for s in out_shape],
        compiler_params=_cparams("arbitrary"),
        name="merge_router",
    )(x, mods, attn, fnet, pool, gates, wts["w_attn_o"], wts["w_fnet_o"], wts["w_pool_o"], wts["w_out"],
      wts["norm2_g"], wts["w_router3"], wts["b_router"])


def _sc_row_gather(table, idx):
    n = idx.shape[0]
    d = table.shape[1]
    info = plsc.get_sparse_core_info()
    n_cores = info.num_cores
    n_workers = n_cores * info.num_subcores
    per_worker = n // n_workers
    n_chunks = per_worker // SC_GATHER_ROWS
    assert n_chunks * SC_GATHER_ROWS * n_workers == n
    mesh = plsc.VectorSubcoreMesh(core_axis_name="core", subcore_axis_name="subcore")

    @functools.partial(
        pl.kernel, mesh=mesh, out_type=jax.ShapeDtypeStruct((n, d), table.dtype),
        scratch_types=[pltpu.VMEM((SC_GATHER_ROWS,), jnp.int32), pltpu.VMEM((SC_GATHER_ROWS, d), table.dtype),
                       pltpu.SemaphoreType.DMA],
        name="sc_row_gather")
    def gather(table_hbm, idx_hbm, out_hbm, idx_v, rows_v, sem):
        worker = lax.axis_index("subcore") * n_cores + lax.axis_index("core")

        @pl.loop(0, n_chunks)
        def _(c):
            base = worker * per_worker + c * SC_GATHER_ROWS
            pltpu.sync_copy(idx_hbm.at[pl.ds(base, SC_GATHER_ROWS)], idx_v)
            pltpu.async_copy(table_hbm.at[idx_v], rows_v, sem).wait()
            pltpu.sync_copy(rows_v, out_hbm.at[pl.ds(base, SC_GATHER_ROWS)])

    return gather(table, idx)


def _expert_kernel(be_ref, nv_ref, nb_ref, xs_ref, wgu_ref, bgu_ref, wd_ref, bd_ref, y_ref, wgu_bf, wd_bf):
    b = pl.program_id(0)

    def ffn(rows):
        gu = _dot(xs_ref[0:rows, :].astype(BF16), wgu_bf[...]) + bgu_ref[...]
        glu = jnp.minimum(gu[:, :D_FF], SWIGLU_LIMIT)
        lin = jnp.clip(gu[:, D_FF:], -SWIGLU_LIMIT, SWIGLU_LIMIT)
        act = glu * jax.nn.sigmoid(SWIGLU_ALPHA * glu) * (lin + 1.0)
        y_ref[0:rows, :] = _dot(act.astype(BF16), wd_bf[...]) + bd_ref[...]

    @pl.when(b < nb_ref[0])
    def _():
        prev = be_ref[jnp.maximum(b - 1, 0)]

        @pl.when((b == 0) | (be_ref[b] != prev))
        def _():
            wgu_bf[...] = wgu_ref[...].astype(BF16)
            wd_bf[...] = wd_ref[...].astype(BF16)

        @pl.when(nv_ref[b] > MOE_HALF)
        def _():
            ffn(MOE_BLOCK)

        @pl.when(nv_ref[b] <= MOE_HALF)
        def _():
            ffn(MOE_HALF)
            y_ref[MOE_HALF:, :] = jnp.zeros((MOE_BLOCK - MOE_HALF, D_MODEL), y_ref.dtype)

    @pl.when(b >= nb_ref[0])
    def _():
        y_ref[...] = jnp.zeros(y_ref.shape, y_ref.dtype)


def _experts(xs, block_exp, block_valid, n_used, l, w_gu, b_gu4, w_down, b_down4):
    n_slots = xs.shape[0]
    bm = MOE_BLOCK
    n_blocks = n_slots // bm

    def blk(b, nb):
        return jnp.minimum(b, nb[0] - 1)

    def wspec(shape):
        return pl.BlockSpec((None, None) + shape, lambda b, be, nv, nb: (l, be[blk(b, nb)], 0, 0))

    grid_spec = pltpu.PrefetchScalarGridSpec(
        num_scalar_prefetch=3,
        grid=(n_blocks,),
        in_specs=[
            pl.BlockSpec((bm, D_MODEL), lambda b, be, nv, nb: (blk(b, nb), 0)),
            wspec((D_MODEL, 2 * D_FF)), wspec((1, 2 * D_FF)), wspec((D_FF, D_MODEL)), wspec((1, D_MODEL)),
        ],
        out_specs=pl.BlockSpec((bm, D_MODEL), lambda b, be, nv, nb: (b, 0)),
        scratch_shapes=[pltpu.VMEM((D_MODEL, 2 * D_FF), BF16), pltpu.VMEM((D_FF, D_MODEL), BF16)],
    )
    return pl.pallas_call(
        _expert_kernel,
        out_shape=jax.ShapeDtypeStruct((n_slots, D_MODEL), F32),
        grid_spec=grid_spec,
        compiler_params=_cparams("arbitrary"),
        name="moe_experts",
    )(block_exp, block_valid, n_used, xs, w_gu, b_gu4, w_down, b_down4)


def _combine_kernel(tw_ref, x_ref, mod_ref, yg_ref, o_ref):
    acc = tw_ref[:, 0:1] * yg_ref[0]
    for k in range(1, TOP_K):
        acc = acc + tw_ref[:, k:k + 1] * yg_ref[k]
    gate2 = mod_ref[5:6, :]
    o_ref[...] = x_ref[...] + gate2 * acc


def _combine(yg, tw, x1, mods, l, tok):
    t = x1.shape[0]
    tc = COMBINE_TILE
    p_tiles = tok["t_prompt"] // tc
    s_tiles = tok["dec_seq"] // tc

    def mod_idx(i):
        row = jnp.where(i < p_tiles, 0, 1 + (i - p_tiles) // s_tiles)
        return (l, row, 0, 0)

    return pl.pallas_call(
        _combine_kernel,
        out_shape=jax.ShapeDtypeStruct((t, D_MODEL), F32),
        grid=(t // tc,),
        in_specs=[
            pl.BlockSpec((tc, ROUTER_PAD), lambda i: (i, 0)),
            pl.BlockSpec((tc, D_MODEL), lambda i: (i, 0)),
            pl.BlockSpec((None, None, 6, D_MODEL), mod_idx),
            pl.BlockSpec((TOP_K, tc, D_MODEL), lambda i: (0, i, 0)),
        ],
        out_specs=pl.BlockSpec((tc, D_MODEL), lambda i: (i, 0)),
        compiler_params=_cparams("arbitrary"),
        name="moe_combine",
    )(tw, x1, mods, yg)


def _slot_tokens(slot, counts, padded_start, padded_end, n_slots, t):
    n = slot.shape[0]
    j = jnp.arange(MOE_BLOCK, dtype=jnp.int32)
    pad_key = (padded_start + counts)[:, None] + j[None, :]
    pad_key = jnp.where(pad_key < padded_end[:, None], pad_key, n_slots).reshape(N_EXPERTS * MOE_BLOCK)
    pad_val = jnp.arange(N_EXPERTS * MOE_BLOCK, dtype=jnp.int32) % t
    keys = jnp.concatenate([slot, pad_key])
    vals = jnp.concatenate([jnp.arange(n, dtype=jnp.int32) // TOP_K, pad_val])
    _, sorted_vals = lax.sort((keys, vals), num_keys=1)
    return sorted_vals[:n_slots]


def _route(te_pad, n_slots):
    t = te_pad.shape[0]
    flat_e = te_pad[:, :TOP_K].reshape(t * TOP_K)
    onehot = (flat_e[:, None] == jnp.arange(N_EXPERTS, dtype=jnp.int32)[None, :]).astype(jnp.int32)
    csum = jnp.cumsum(onehot, axis=0)
    rank = jnp.sum(csum * onehot, axis=1) - 1
    counts = csum[-1]
    padded = (counts + MOE_BLOCK - 1) // MOE_BLOCK * MOE_BLOCK
    padded_end = jnp.cumsum(padded)
    padded_start = padded_end - padded
    slot = jnp.sum(onehot * padded_start[None, :], axis=1) + rank
    n_blocks = n_slots // MOE_BLOCK
    block_row0 = jnp.arange(n_blocks, dtype=jnp.int32) * MOE_BLOCK
    block_exp = jnp.minimum(
        jnp.sum((padded_end[None, :] <= block_row0[:, None]).astype(jnp.int32), axis=1),
        N_EXPERTS - 1).astype(jnp.int32)
    exp_hot = (block_exp[:, None] == jnp.arange(N_EXPERTS, dtype=jnp.int32)[None, :]).astype(jnp.int32)
    rows_end = jnp.sum(exp_hot * (padded_start + counts)[None, :], axis=1)
    block_valid = jnp.clip(rows_end - block_row0, 0, MOE_BLOCK).astype(jnp.int32)
    n_used = (padded_end[-1:] // MOE_BLOCK).astype(jnp.int32)
    slot = slot.astype(jnp.int32)
    slot_tok = _slot_tokens(slot, counts, padded_start, padded_end, n_slots, t)
    slot_by_choice = slot.reshape(t, TOP_K).T.reshape(TOP_K * t)
    return slot_by_choice, slot_tok, block_exp, block_valid, n_used


def _segment_matrices():
    m_q = np.zeros((QK_PAD, LANES), np.float32)
    m_k = np.zeros((QK_PAD, LANES), np.float32)
    inv_cnt_q = np.ones((1, LANES), np.float32)
    for h in range(N_HEADS):
        m_q[h * HEAD_PAD:h * HEAD_PAD + QK_NOPE, 2 * h] = 1.0
        m_q[h * HEAD_PAD + QK_NOPE:h * HEAD_PAD + QK_HEAD, 2 * h + 1] = 1.0
        inv_cnt_q[0, 2 * h] = 1.0 / QK_NOPE
        inv_cnt_q[0, 2 * h + 1] = 1.0 / ROPE_DIM
        m_k[h * HEAD_PAD:h * HEAD_PAD + QK_NOPE, h] = 1.0
    dup = lambda m: np.concatenate([m.T, m.T], axis=0)
    return dict(m_q=jnp.asarray(m_q, BF16), mt2_q=jnp.asarray(dup(m_q), BF16), inv_cnt_q=jnp.asarray(inv_cnt_q),
                m_k=jnp.asarray(m_k, BF16), mt2_k=jnp.asarray(dup(m_k), BF16))


def _rope_lane_tables(n_tokens, ident_rows):
    rows = n_tokens // GRID_W
    row = jnp.broadcast_to(jnp.arange(rows)[:, None], (rows, GRID_W)).reshape(n_tokens)
    col = jnp.broadcast_to(jnp.arange(GRID_W)[None, :], (rows, GRID_W)).reshape(n_tokens)
    n_freq = ROPE_DIM // 4
    inv = 1.0 / (ROPE_THETA ** (jnp.arange(n_freq, dtype=F32) / n_freq))
    ang = jnp.concatenate([row[:, None].astype(F32) * inv, col[:, None].astype(F32) * inv], axis=-1)
    cos, sin = jnp.cos(ang), jnp.sin(ang)
    ones = jnp.ones((n_tokens, QK_NOPE), F32)
    tail = HEAD_PAD - QK_HEAD
    cosf = jnp.concatenate([ones, cos, cos, jnp.ones((n_tokens, tail), F32)], axis=1)
    sinf = jnp.concatenate([0.0 * ones, -sin, sin, jnp.zeros((n_tokens, tail), F32)], axis=1)
    cosf = jnp.concatenate([cosf, jnp.ones((ident_rows, HEAD_PAD), F32)], axis=0)
    sinf = jnp.concatenate([sinf, jnp.zeros((ident_rows, HEAD_PAD), F32)], axis=0)
    return cosf, sinf


def _dft_tables(seq_len):
    def cs(n):
        k = np.arange(n, dtype=np.int64)
        ang = 2.0 * np.pi * ((k[:, None] * k[None, :]) % n).astype(np.float64) / n
        return np.cos(ang).astype(np.float32), np.sin(ang).astype(np.float32)

    cl, sl = cs(seq_len)
    cc, sc = cs(FNET_GC)
    return dict(cos=jnp.asarray(cl).astype(BF16), sin=jnp.asarray(sl).astype(BF16),
                chan=jnp.asarray(np.concatenate([cc, sc], axis=1)).astype(BF16))


def _layout_weights(w_in, w_q_b, w_kv_b, q_nope_g, q_rope_g, k_nope_g, k_rope_g, w_router, b_router):
    depth = w_in.shape[0]
    w_in_p = jnp.concatenate(
        [w_in[:, :, :COL_KPE + ROPE_DIM], jnp.zeros((depth, D_MODEL, KPE_PAD - ROPE_DIM), w_in.dtype),
         w_in[:, :, COL_KPE + ROPE_DIM:]], axis=2).astype(BF16)
    w_q = jnp.pad(w_q_b.reshape(depth, Q_RANK, N_HEADS, QK_HEAD),
                  ((0, 0), (0, 0), (0, 0), (0, HEAD_PAD - QK_HEAD))).reshape(depth, Q_RANK, QK_PAD).astype(BF16)
    kv = w_kv_b.reshape(depth, KV_RANK, N_HEADS, QK_NOPE + V_HEAD)
    w_k = jnp.pad(kv[..., :QK_NOPE], ((0, 0), (0, 0), (0, 0), (0, HEAD_PAD - QK_NOPE))
                  ).reshape(depth, KV_RANK, QK_PAD).astype(BF16)
    w_v = kv[..., QK_NOPE:].reshape(depth, KV_RANK, V_WIDTH).astype(BF16)
    zq = jnp.zeros((depth, HEAD_PAD - QK_HEAD), F32)
    g_q = jnp.tile(jnp.concatenate([q_nope_g, q_rope_g, zq], axis=1), (1, N_HEADS))[:, None, :]
    g_k = jnp.tile(jnp.concatenate([k_nope_g, jnp.zeros((depth, HEAD_PAD - QK_NOPE), F32)], axis=1),
                   (1, N_HEADS))[:, None, :]
    g_kpe = jnp.concatenate([jnp.zeros((depth, QK_NOPE), F32), k_rope_g, zq], axis=1)[:, None, :]
    w_hi = w_router.astype(BF16)
    w_lo = (w_router - w_hi.astype(F32)).astype(BF16)
    w_r3 = jnp.pad(jnp.concatenate([w_hi, w_hi, w_lo], axis=1), ((0, 0), (0, 0), (0, ROUTER_PAD - N_EXPERTS)))
    b_r = jnp.pad(b_router, ((0, 0), (0, ROUTER_PAD - N_EXPERTS)), constant_values=NEG_BIG)[:, None, :]
    return dict(w_in=w_in_p, w_q=w_q, w_k=w_k, w_v=w_v, g_q=g_q, g_k=g_k, g_kpe=g_kpe, w_router3=w_r3, b_router=b_r)


def kernel(x_prompt, x_sample, cache_ckv, cache_kpe, c, c_ctx, w_mod, b_mod, norm1_g, norm2_g, w_in, q_a_g, kv_a_g, w_q_b, w_kv_b, q_nope_g, q_rope_g, k_nope_g, k_rope_g, w_attn_o, w_fnet_o, w_pool_grp, pool_scale, w_pool_o, w_out, w_router, b_router, w_gu, b_gu, w_down, b_down):
    batch, seq, d = x_prompt.shape
    dec_batch, dec_seq, _ = x_sample.shape
    depth = w_mod.shape[0]
    past = cache_ckv.shape[2]
    t_prompt = batch * seq
    t_sample = dec_batch * dec_seq
    t = t_prompt + t_sample
    tok = dict(t_prompt=t_prompt, dec_seq=dec_seq)
    assert d == D_MODEL and dec_batch + 1 <= 8
    assert t_prompt % TOKEN_TILE == 0 and dec_seq % TOKEN_TILE == 0 and seq % COMBINE_TILE == 0
    assert past == seq and dec_seq % GRID_W == 0

    consts = _segment_matrices()
    consts["cosf"], consts["sinf"] = _rope_lane_tables(dec_seq, TOKEN_TILE)
    dft_p = _dft_tables(seq)
    dft_s = _dft_tables(dec_seq)

    wts = _layout_weights(w_in, w_q_b, w_kv_b, q_nope_g, q_rope_g, k_nope_g, k_rope_g, w_router, b_router)
    row3 = lambda a: a[:, None, :]
    wts.update(norm1_g=row3(norm1_g), norm2_g=row3(norm2_g), q_a_g=row3(q_a_g), kv_a_g=row3(kv_a_g),
               w_attn_o=w_attn_o.astype(BF16), w_fnet_o=w_fnet_o.astype(BF16), w_pool_o=w_pool_o.astype(BF16),
               w_out=w_out.astype(BF16), w_pool_grp=w_pool_grp.astype(BF16), pool_scale=row3(pool_scale))
    b_gu4 = b_gu[:, :, None, :]
    b_down4 = b_down[:, :, None, :]

    cond8 = jnp.concatenate([c_ctx[None, :], c, jnp.zeros((8 - 1 - dec_batch, d), F32)], axis=0)
    mods = _modulation(cond8, w_mod, b_mod).reshape(depth, 8, 6, D_MODEL)

    kpe_pad = jnp.pad(cache_kpe, ((0, 0), (0, 0), (0, 0), (QK_NOPE, HEAD_PAD - QK_HEAD)))
    k_ctx, v_ctx = _ctx_keys(cache_ckv, kpe_pad, wts, consts)

    n_assign = t * TOP_K
    n_slots = (n_assign + N_EXPERTS * (MOE_BLOCK - 1) + MOE_BLOCK - 1) // MOE_BLOCK * MOE_BLOCK

    x = jnp.concatenate([x_prompt.reshape(t_prompt, d), x_sample.reshape(t_sample, d)], axis=0)
    ckv_list, kpe_list = [], []
    for l in range(depth):
        q, k, v, ckv, kpe, f_in, p_in, gates = _in_projection(x, mods, l, wts, consts, tok)
        ckv_list.append(ckv[:t_prompt].reshape(batch, seq, KV_RANK))
        kpe_list.append(kpe[:t_prompt].reshape(batch, seq, ROPE_DIM))

        attn_p = _attention(q, [(k, v, 0, seq)], batch, seq, 0)
        attn_s = _attention(q, [(k, v, t_prompt // dec_seq, dec_seq), (k_ctx[l], v_ctx[l], 0, past)],
                            dec_batch, dec_seq, t_prompt)
        fnet = jnp.concatenate([_fourier(f_in, batch, seq, 0, dft_p),
                                _fourier(f_in, dec_batch, dec_seq, t_prompt, dft_s)], axis=0)
        pool = jnp.concatenate([_pooling(p_in, batch, seq, 0, l, wts),
                                _pooling(p_in, dec_batch, dec_seq, t_prompt, l, wts)], axis=0)
        attn = jnp.concatenate([attn_p, attn_s], axis=0)

        x1, h2, te_pad, tw_pad = _merge(x, mods, attn, fnet, pool, gates, l, wts, tok)
        slot_by_choice, slot_tok, block_exp, block_valid, n_used = _route(te_pad, n_slots)
        xs = _sc_row_gather(h2, slot_tok)
        y = _experts(xs, block_exp, block_valid, n_used, l, w_gu, b_gu4, w_down, b_down4)
        yg = _sc_row_gather(y, slot_by_choice).reshape(TOP_K, t, D_MODEL)
        x = _combine(yg, tw_pad, x1, mods, l, tok)

    y_prompt = x[:t_prompt].reshape(batch, seq, d)
    y_sample = x[t_prompt:].reshape(dec_batch, dec_seq, d)
    return (y_prompt, y_sample, jnp.stack(ckv_list, axis=1), jnp.stack(kpe_list, axis=1))
```

```python
import functools
import math

import numpy as np
import jax
import jax.numpy as jnp
from jax import lax
from jax.experimental import pallas as pl
from jax.experimental.pallas import tpu as pltpu
from jax.experimental.pallas import tpu_sc as plsc

D_MODEL = 1024
GRID_W = 64
N_HEADS = 8
QK_NOPE = 64
ROPE_DIM = 32
V_HEAD = 64
QK_HEAD = QK_NOPE + ROPE_DIM
Q_RANK = 384
KV_RANK = 256
ROPE_THETA = 10000.0
FNET_GROUPS = 4
FNET_GC = 128
FNET_WIDTH = FNET_GROUPS * FNET_GC
POOL_WINDOWS = (2, 4, 8, 16)
POOL_GC = 128
POOL_WIDTH = len(POOL_WINDOWS) * POOL_GC
N_BRANCHES = 3
N_EXPERTS = 32
TOP_K = 4
D_FF = D_MODEL
SWIGLU_LIMIT = 7.0
SWIGLU_ALPHA = 1.702
RMS_EPS = 1e-6

LANES = 128
HEAD_PAD = LANES
QK_PAD = N_HEADS * HEAD_PAD
V_WIDTH = N_HEADS * V_HEAD
ROPE_HALF = ROPE_DIM // 2
KPE_PAD = LANES
COL_KV = Q_RANK
COL_KPE = Q_RANK + KV_RANK
COL_F = COL_KPE + KPE_PAD
COL_P = COL_F + FNET_WIDTH
COL_G = COL_P + POOL_WIDTH
IN_PAD_WIDTH = COL_G + N_BRANCHES * D_MODEL
ROUTER_PAD = LANES
NEG_BIG = -1e30

TOKEN_TILE = 512
ATTN_Q_TILE = 512
FNET_ROW_TILE = 512
POOL_CHUNK = 256
POOL_HALO = 16
MOE_BLOCK = 512
MOE_HALF = MOE_BLOCK // 2
GATE_COL_CHUNK = 512
MERGE_COL_CHUNK = 256
FF_CHUNK = 256
COMBINE_TILE = 256
SC_GATHER_ROWS = 64
MOD_COL_TILE = 1536
VMEM_LIMIT = 56 * 1024 * 1024

F32 = jnp.float32
BF16 = jnp.bfloat16


def _cparams(*sem):
    return pltpu.CompilerParams(dimension_semantics=sem, vmem_limit_bytes=VMEM_LIMIT)


def _dot(a, b):
    return jnp.dot(a, b, preferred_element_type=F32)


def _rms(x):
    return x * lax.rsqrt(jnp.mean(x * x, axis=-1, keepdims=True) + RMS_EPS)


def _seg_rms_scale(x, m_ref, mt2_ref, inv_cnt):
    ss = _dot((x * x).astype(BF16), m_ref[...])
    r = lax.rsqrt(ss * inv_cnt + RMS_EPS)
    r_hi = r.astype(BF16)
    r_lo = (r - r_hi.astype(F32)).astype(BF16)
    return _dot(jnp.concatenate([r_hi, r_lo], axis=1), mt2_ref[...])


def _rope_chunk(xc, cosf, sinf, first_half):
    partner = jnp.where(first_half, pltpu.roll(xc, HEAD_PAD - ROPE_HALF, 1), pltpu.roll(xc, ROPE_HALF, 1))
    return xc * cosf + partner * sinf


def _write_keys(ckv, kpe_chunk, wk_ref, wv_ref, gk_ref, gkpe_ref, mk_ref, mkt2_ref, rope, k_out, v_out):
    cb = ckv.astype(BF16)
    kn = _dot(cb, wk_ref[...])
    kn = kn * _seg_rms_scale(kn, mk_ref, mkt2_ref, 1.0 / QK_NOPE) * gk_ref[...]
    v_out[...] = _dot(cb, wv_ref[...]).astype(v_out.dtype)
    ms = jnp.sum(kpe_chunk * kpe_chunk, axis=-1, keepdims=True) * (1.0 / ROPE_DIM)
    kp = kpe_chunk * lax.rsqrt(ms + RMS_EPS) * gkpe_ref[...]
    if rope is not None:
        kp = _rope_chunk(kp, *rope)
    for h in range(N_HEADS):
        sl = slice(h * HEAD_PAD, (h + 1) * HEAD_PAD)
        k_out[:, sl] = (kn[:, sl] + kp).astype(k_out.dtype)


def _mod_kernel(c_ref, w_ref, b_ref, o_ref):
    c = c_ref[...]
    s = c * jax.nn.sigmoid(c)
    o_ref[...] = _dot(s.astype(BF16), w_ref[...].astype(BF16)) + b_ref[...]


def _modulation(cond8, w_mod, b_mod):
    depth = w_mod.shape[0]
    n = w_mod.shape[2]
    return pl.pallas_call(
        _mod_kernel,
        out_shape=jax.ShapeDtypeStruct((depth, 8, n), F32),
        grid=(depth, n // MOD_COL_TILE),
        in_specs=[
            pl.BlockSpec((8, D_MODEL), lambda l, j: (0, 0)),
            pl.BlockSpec((None, D_MODEL, MOD_COL_TILE), lambda l, j: (l, 0, j)),
            pl.BlockSpec((None, 1, MOD_COL_TILE), lambda l, j: (l, 0, j)),
        ],
        out_specs=pl.BlockSpec((None, 8, MOD_COL_TILE), lambda l, j: (l, 0, j)),
        compiler_params=_cparams("arbitrary", "arbitrary"),
        name="adaln_modulation",
    )(cond8, w_mod, b_mod.reshape(depth, 1, n))


def _in_kernel(x_ref, mod_ref, g1_ref, w_ref, qag_ref, kvag_ref, wq_ref, wk_ref, wv_ref,
               gq_ref, gk_ref, gkpe_ref, cos_ref, sin_ref, mq_ref, mqt2_ref, icq_ref, mk_ref, mkt2_ref,
               q_out, k_out, v_out, ckv_out, kpe_out, f_out, p_out, g_out):
    x = x_ref[...]
    shift = mod_ref[0:1, :]
    scale = mod_ref[1:2, :]
    h = _rms(x) * g1_ref[...] * (1.0 + scale) + shift
    hb = h.astype(BF16)

    za = _dot(hb, w_ref[:, 0:COL_F])
    q_lat = za[:, 0:COL_KV]
    kv_lat = za[:, COL_KV:COL_KPE]
    kpe_grp = za[:, COL_KPE:COL_F]
    ckv = _rms(kv_lat) * kvag_ref[...]
    ckv_out[...] = ckv
    kpe_out[...] = kpe_grp[:, 0:ROPE_DIM]

    cosf = cos_ref[...]
    sinf = sin_ref[...]
    lane = lax.broadcasted_iota(jnp.int32, cosf.shape, 1)
    rope = (cosf, sinf, lane < QK_NOPE + ROPE_HALF)

    qn = (_rms(q_lat) * qag_ref[...]).astype(BF16)
    q = _dot(qn, wq_ref[...])
    q = q * _seg_rms_scale(q, mq_ref, mqt2_ref, icq_ref[...]) * gq_ref[...]
    sm_scale = 1.0 / math.sqrt(QK_HEAD)
    for hd in range(N_HEADS):
        sl = slice(hd * HEAD_PAD, (hd + 1) * HEAD_PAD)
        q_out[:, sl] = (_rope_chunk(q[:, sl], *rope) * sm_scale).astype(q_out.dtype)

    kpe_chunk = pltpu.roll(kpe_grp, QK_NOPE, 1)
    _write_keys(ckv, kpe_chunk, wk_ref, wv_ref, gk_ref, gkpe_ref, mk_ref, mkt2_ref, rope, k_out, v_out)

    zb = _dot(hb, w_ref[:, COL_F:COL_G])
    f_out[...] = zb[:, 0:FNET_WIDTH].astype(f_out.dtype)
    p_out[...] = zb[:, FNET_WIDTH:]
    for c0 in range(0, N_BRANCHES * D_MODEL, GATE_COL_CHUNK):
        zc = _dot(hb, w_ref[:, COL_G + c0:COL_G + c0 + GATE_COL_CHUNK])
        g_out[:, c0:c0 + GATE_COL_CHUNK] = jax.nn.sigmoid(zc).astype(g_out.dtype)


def _const_spec(shape):
    nd = len(shape)
    return pl.BlockSpec(shape, lambda i, _n=nd: (0,) * _n, pipeline_mode=pl.Buffered(1))


def _layer_spec(shape, l):
    nd = len(shape)
    return pl.BlockSpec((None,) + tuple(shape), lambda i, _l=l, _n=nd: (_l,) + (0,) * _n,
                        pipeline_mode=pl.Buffered(1))


def _in_projection(x, mods, l, wts, consts, tok):
    t = x.shape[0]
    tm = TOKEN_TILE
    n_tiles = t // tm
    p_tiles = tok["t_prompt"] // tm
    s_tiles = tok["dec_seq"] // tm
    rope_blocks = tok["dec_seq"] // tm

    def mod_idx(i):
        row = jnp.where(i < p_tiles, 0, 1 + (i - p_tiles) // s_tiles)
        return (l, row, 0, 0)

    def rope_idx(i):
        return (jnp.where(i < p_tiles, rope_blocks, (i - p_tiles) % s_tiles), 0)

    row = lambda w: pl.BlockSpec((tm, w), lambda i: (i, 0))
    in_specs = [
        row(D_MODEL),
        pl.BlockSpec((None, None, 6, D_MODEL), mod_idx),
        _layer_spec((1, D_MODEL), l),
        _layer_spec((D_MODEL, IN_PAD_WIDTH), l),
        _layer_spec((1, Q_RANK), l),
        _layer_spec((1, KV_RANK), l),
        _layer_spec((Q_RANK, QK_PAD), l),
        _layer_spec((KV_RANK, QK_PAD), l),
        _layer_spec((KV_RANK, V_WIDTH), l),
        _layer_spec((1, QK_PAD), l),
        _layer_spec((1, QK_PAD), l),
        _layer_spec((1, HEAD_PAD), l),
        pl.BlockSpec((tm, HEAD_PAD), rope_idx),
        pl.BlockSpec((tm, HEAD_PAD), rope_idx),
        _const_spec((QK_PAD, LANES)),
        _const_spec((2 * LANES, QK_PAD)),
        _const_spec((1, LANES)),
        _const_spec((QK_PAD, LANES)),
        _const_spec((2 * LANES, QK_PAD)),
    ]
    out_shape = [
        jax.ShapeDtypeStruct((t, QK_PAD), BF16),
        jax.ShapeDtypeStruct((t, QK_PAD), BF16),
        jax.ShapeDtypeStruct((t, V_WIDTH), BF16),
        jax.ShapeDtypeStruct((t, KV_RANK), F32),
        jax.ShapeDtypeStruct((t, ROPE_DIM), F32),
        jax.ShapeDtypeStruct((t, FNET_WIDTH), BF16),
        jax.ShapeDtypeStruct((t, POOL_WIDTH), F32),
        jax.ShapeDtypeStruct((t, N_BRANCHES * D_MODEL), BF16),
    ]
    out_specs = [row(s.shape[1]) for s in out_shape]
    return pl.pallas_call(
        _in_kernel,
        out_shape=out_shape,
        grid=(n_tiles,),
        in_specs=in_specs,
        out_specs=out_specs,
        compiler_params=_cparams("arbitrary"),
        name="in_projection",
    )(x, mods, wts["norm1_g"], wts["w_in"], wts["q_a_g"], wts["kv_a_g"], wts["w_q"], wts["w_k"], wts["w_v"],
      wts["g_q"], wts["g_k"], wts["g_kpe"], consts["cosf"], consts["sinf"],
      consts["m_q"], consts["mt2_q"], consts["inv_cnt_q"], consts["m_k"], consts["mt2_k"])


def _ctx_keys_kernel(ckv_ref, kpe_ref, wk_ref, wv_ref, gk_ref, gkpe_ref, mk_ref, mkt2_ref, k_out, v_out):
    _write_keys(ckv_ref[...], kpe_ref[...], wk_ref, wv_ref, gk_ref, gkpe_ref, mk_ref, mkt2_ref, None, k_out, v_out)


def _ctx_keys(cache_ckv, cache_kpe_pad, wts, consts):
    nb, depth, past, _ = cache_ckv.shape
    lw = lambda shape: pl.BlockSpec((None,) + shape, lambda l, b: (l,) + (0,) * len(shape))
    cs = lambda shape: pl.BlockSpec(shape, lambda l, b: (0,) * len(shape))
    return pl.pallas_call(
        _ctx_keys_kernel,
        out_shape=[jax.ShapeDtypeStruct((depth, nb * past, QK_PAD), BF16),
                   jax.ShapeDtypeStruct((depth, nb * past, V_WIDTH), BF16)],
        grid=(depth, nb),
        in_specs=[
            pl.BlockSpec((None, None, past, KV_RANK), lambda l, b: (b, l, 0, 0)),
            pl.BlockSpec((None, None, past, HEAD_PAD), lambda l, b: (b, l, 0, 0)),
            lw((KV_RANK, QK_PAD)), lw((KV_RANK, V_WIDTH)), lw((1, QK_PAD)), lw((1, HEAD_PAD)),
            cs((QK_PAD, LANES)), cs((2 * LANES, QK_PAD)),
        ],
        out_specs=[pl.BlockSpec((None, past, QK_PAD), lambda l, b: (l, b, 0)),
                   pl.BlockSpec((None, past, V_WIDTH), lambda l, b: (l, b, 0))],
        compiler_params=_cparams("arbitrary", "arbitrary"),
        name="context_keys",
    )(cache_ckv, cache_kpe_pad, wts["w_k"], wts["w_v"], wts["g_k"], wts["g_kpe"], consts["m_k"], consts["mt2_k"])


def _attn_kernel(*refs, n_parts):
    q_ref = refs[0]
    k_refs = refs[1:1 + 2 * n_parts:2]
    v_refs = refs[2:2 + 2 * n_parts:2]
    o_ref = refs[1 + 2 * n_parts]
    lane = lax.broadcasted_iota(jnp.int32, (q_ref.shape[0], 2 * V_HEAD), 1)
    for pair in range(N_HEADS // 2):
        vsl = slice(pair * 2 * V_HEAD, (pair + 1) * 2 * V_HEAD)
        outs = []
        for hd in (2 * pair, 2 * pair + 1):
            sl = slice(hd * HEAD_PAD, (hd + 1) * HEAD_PAD)
            qh = q_ref[:, sl]
            ss = [lax.dot_general(qh, k[:, sl], (((1,), (1,)), ((), ())), preferred_element_type=F32)
                  for k in k_refs]
            m = functools.reduce(jnp.maximum, [jnp.max(s, axis=-1, keepdims=True) for s in ss])
            es = [jnp.exp(s - m) for s in ss]
            den = functools.reduce(jnp.add, [jnp.sum(e, axis=-1, keepdims=True) for e in es])
            acc = functools.reduce(jnp.add, [_dot(e.astype(BF16), v[:, vsl]) for e, v in zip(es, v_refs)])
            outs.append(acc / den)
        o_ref[:, vsl] = jnp.where(lane < V_HEAD, outs[0], outs[1]).astype(o_ref.dtype)


def _attention(q, kv_parts, n_seq, seq_len, q_row0):
    tq = min(ATTN_Q_TILE, seq_len)
    nq = seq_len // tq
    qb0 = q_row0 // tq
    in_specs = [pl.BlockSpec((tq, QK_PAD), lambda b, i: (qb0 + b * nq + i, 0))]
    args = [q]
    for k, v, blk0, rows in kv_parts:
        in_specs.append(pl.BlockSpec((rows, QK_PAD), lambda b, i, _o=blk0: (_o + b, 0)))
        in_specs.append(pl.BlockSpec((rows, V_WIDTH), lambda b, i, _o=blk0: (_o + b, 0)))
        args += [k, v]
    return pl.pallas_call(
        functools.partial(_attn_kernel, n_parts=len(kv_parts)),
        out_shape=jax.ShapeDtypeStruct((n_seq * seq_len, V_WIDTH), BF16),
        grid=(n_seq, nq),
        in_specs=in_specs,
        out_specs=pl.BlockSpec((tq, V_WIDTH), lambda b, i: (b * nq + i, 0)),
        compiler_params=_cparams("arbitrary", "arbitrary"),
        name="attention",
    )(*args)


def _fnet_kernel(f_ref, cs_ref, cl_ref, sl_ref, o_ref, top_ref, bot_ref, *, norm):
    @pl.when(pl.program_id(1) == 0)
    def _():
        for g in range(FNET_GROUPS):
            sl = slice(g * FNET_GC, (g + 1) * FNET_GC)
            a = _dot(f_ref[:, sl], cs_ref[...])
            top_ref[:, sl] = a[:, :FNET_GC].astype(BF16)
            bot_ref[:, sl] = a[:, FNET_GC:].astype(BF16)

    y = _dot(cl_ref[...], top_ref[...]) - _dot(sl_ref[...], bot_ref[...])
    o_ref[...] = (y * norm).astype(o_ref.dtype)


def _fourier(f_in, n_seq, seq_len, row0, tabs):
    tr = min(FNET_ROW_TILE, seq_len)
    nj = seq_len // tr
    sb0 = row0 // seq_len
    return pl.pallas_call(
        functools.partial(_fnet_kernel, norm=1.0 / math.sqrt(seq_len * FNET_GC)),
        out_shape=jax.ShapeDtypeStruct((n_seq * seq_len, FNET_WIDTH), BF16),
        grid=(n_seq, nj),
        in_specs=[
            pl.BlockSpec((seq_len, FNET_WIDTH), lambda b, j: (sb0 + b, 0)),
            pl.BlockSpec((FNET_GC, 2 * FNET_GC), lambda b, j: (0, 0)),
            pl.BlockSpec((tr, seq_len), lambda b, j: (j, 0)),
            pl.BlockSpec((tr, seq_len), lambda b, j: (j, 0)),
        ],
        out_specs=pl.BlockSpec((tr, FNET_WIDTH), lambda b, j: (b * nj + j, 0)),
        scratch_shapes=[pltpu.VMEM((seq_len, FNET_WIDTH), BF16), pltpu.VMEM((seq_len, FNET_WIDTH), BF16)],
        compiler_params=_cparams("arbitrary", "arbitrary"),
        name="fourier_mix",
    )(f_in, tabs["chan"], tabs["cos"], tabs["sin"])


def _pool_kernel(p_ref, wg_ref, ps_ref, o_ref, pad_ref):
    seq_len = p_ref.shape[0]
    zeros = jnp.zeros((POOL_HALO, POOL_WIDTH), F32)
    pad_ref[0:POOL_HALO, :] = zeros
    pad_ref[POOL_HALO + seq_len:, :] = zeros
    pad_ref[POOL_HALO:POOL_HALO + seq_len, :] = p_ref[...]
    ch = min(POOL_CHUNK, seq_len)
    for c in range(seq_len // ch):
        t = lax.broadcasted_iota(jnp.int32, (ch, 1), 0) + c * ch
        for g, w in enumerate(POOL_WINDOWS):
            half = w // 2
            sl = slice(g * POOL_GC, (g + 1) * POOL_GC)
            acc = None
            for j in range(-half, half):
                r0 = POOL_HALO + c * ch + j
                part = pad_ref[r0:r0 + ch, sl]
                acc = part if acc is None else acc + part
            cnt = (jnp.minimum(t + half, seq_len) - jnp.maximum(t - half, 0)).astype(F32)
            pooled = acc / cnt - p_ref[c * ch:(c + 1) * ch, sl]
            mixed = _dot(pooled.astype(BF16), wg_ref[g]) * ps_ref[:, sl]
            o_ref[c * ch:(c + 1) * ch, sl] = mixed.astype(o_ref.dtype)


def _pooling(p_in, n_seq, seq_len, row0, l, wts):
    sb0 = row0 // seq_len
    g = len(POOL_WINDOWS)
    return pl.pallas_call(
        _pool_kernel,
        out_shape=jax.ShapeDtypeStruct((n_seq * seq_len, POOL_WIDTH), BF16),
        grid=(n_seq,),
        in_specs=[
            pl.BlockSpec((seq_len, POOL_WIDTH), lambda b: (sb0 + b, 0)),
            pl.BlockSpec((None, g, POOL_GC, POOL_GC), lambda b: (l, 0, 0, 0)),
            pl.BlockSpec((None, 1, POOL_WIDTH), lambda b: (l, 0, 0)),
        ],
        out_specs=pl.BlockSpec((seq_len, POOL_WIDTH), lambda b: (b, 0)),
        scratch_shapes=[pltpu.VMEM((seq_len + 2 * POOL_HALO, POOL_WIDTH), F32)],
        compiler_params=_cparams("arbitrary"),
        name="pool_mix",
    )(p_in, wts["w_pool_grp"], wts["pool_scale"])


def _merge_kernel(x_ref, mod_ref, a_ref, f_ref, p_ref, g_ref, wa_ref, wf_ref, wp_ref, wo_ref,
                  g2_ref, wr_ref, br_ref, tri_ref, x1_out, hp_out, te_out, tw_out, rk_out, cnt_out, carry_ref):
    chunks = []
    for c0 in range(0, D_MODEL, MERGE_COL_CHUNK):
        cs = slice(c0, c0 + MERGE_COL_CHUNK)
        a = _dot(a_ref[...], wa_ref[:, cs])
        f = _dot(f_ref[...], wf_ref[:, cs])
        p = _dot(p_ref[...], wp_ref[:, cs])
        chunks.append((g_ref[:, c0:c0 + MERGE_COL_CHUNK].astype(F32) * a
                       + g_ref[:, D_MODEL + c0:D_MODEL + c0 + MERGE_COL_CHUNK].astype(F32) * f
                       + g_ref[:, 2 * D_MODEL + c0:2 * D_MODEL + c0 + MERGE_COL_CHUNK].astype(F32) * p).astype(BF16))
    mix = _dot(jnp.concatenate(chunks, axis=1), wo_ref[...])
    gate1 = mod_ref[2:3, :]
    shift2 = mod_ref[3:4, :]
    scale2 = mod_ref[4:5, :]
    x1 = x_ref[...] + gate1 * mix
    x1_out[...] = x1
    h2 = _rms(x1) * g2_ref[...] * (1.0 + scale2) + shift2

    h_hi = h2.astype(BF16)
    bits = pltpu.bitcast(h_hi.astype(F32), jnp.int32)
    hp_out[...] = bits[:, :D_MODEL // 2] | lax.shift_right_logical(bits[:, D_MODEL // 2:], 16)

    h_lo = (h2 - h_hi.astype(F32)).astype(BF16)
    lhs = jnp.concatenate([h_hi, h_lo, h_hi], axis=1)
    logits = _dot(lhs, wr_ref[...]) + br_ref[...]

    lane = lax.broadcasted_iota(jnp.int32, logits.shape, 1)
    work = logits
    vals, idxs = [], []
    for _ in range(TOP_K):
        m = jnp.max(work, axis=-1, keepdims=True)
        idx = jnp.min(jnp.where(work == m, lane, ROUTER_PAD), axis=-1, keepdims=True)
        vals.append(m)
        idxs.append(idx)
        work = jnp.where(lane == idx, -jnp.inf, work)
    es = [jnp.exp(v - vals[0]) for v in vals]
    den = functools.reduce(jnp.add, es)

    @pl.when(pl.program_id(0) == 0)
    def _():
        carry_ref[...] = jnp.zeros(carry_ref.shape, F32)

    chosen = functools.reduce(jnp.logical_or, [lane == idx for idx in idxs])
    hot = jnp.where(chosen, 1.0, 0.0)
    before = _dot(tri_ref[...], hot.astype(BF16)) + carry_ref[...]
    carry_ref[...] = carry_ref[...] + jnp.sum(hot, axis=0, keepdims=True)
    cnt_out[...] = carry_ref[...].astype(jnp.int32)

    te = jnp.zeros(logits.shape, jnp.int32)
    tw = jnp.zeros(logits.shape, F32)
    rk = jnp.zeros(logits.shape, jnp.int32)
    for k in range(TOP_K):
        rank_k = jnp.sum(jnp.where(lane == idxs[k], before, 0.0), axis=-1, keepdims=True).astype(jnp.int32)
        te = jnp.where(lane == k, idxs[k], te)
        tw = jnp.where(lane == k, es[k] / den, tw)
        rk = jnp.where(lane == k, rank_k, rk)
    te_out[...] = te
    tw_out[...] = tw
    rk_out[...] = rk


def _merge(x, mods, attn, fnet, pool, gates, l, wts, consts, tok):
    t = x.shape[0]
    tm = TOKEN_TILE
    p_tiles = tok["t_prompt"] // tm
    s_tiles = tok["dec_seq"] // tm

    def mod_idx(i):
        row = jnp.where(i < p_tiles, 0, 1 + (i - p_tiles) // s_tiles)
        return (l, row, 0, 0)

    row = lambda w: pl.BlockSpec((tm, w), lambda i: (i, 0))
    out_shape = [
        jax.ShapeDtypeStruct((t, D_MODEL), F32),
        jax.ShapeDtypeStruct((t, D_MODEL // 2), jnp.int32),
        jax.ShapeDtypeStruct((t, ROUTER_PAD), jnp.int32),
        jax.ShapeDtypeStruct((t, ROUTER_PAD), F32),
        jax.ShapeDtypeStruct((t, ROUTER_PAD), jnp.int32),
    ]
    return pl.pallas_call(
        _merge_kernel,
        out_shape=out_shape + [jax.ShapeDtypeStruct((1, ROUTER_PAD), jnp.int32)],
        grid=(t // tm,),
        in_specs=[
            row(D_MODEL),
            pl.BlockSpec((None, None, 6, D_MODEL), mod_idx),
            row(V_WIDTH), row(FNET_WIDTH), row(POOL_WIDTH), row(N_BRANCHES * D_MODEL),
            _layer_spec((V_WIDTH, D_MODEL), l),
            _layer_spec((FNET_WIDTH, D_MODEL), l),
            _layer_spec((POOL_WIDTH, D_MODEL), l),
            _layer_spec((D_MODEL, D_MODEL), l),
            _layer_spec((1, D_MODEL), l),
            _layer_spec((3 * D_MODEL, ROUTER_PAD), l),
            _layer_spec((1, ROUTER_PAD), l),
            _const_spec((tm, tm)),
        ],
        out_specs=[row(s.shape[1]) for s in out_shape] + [pl.BlockSpec((1, ROUTER_PAD), lambda i: (0, 0))],
        scratch_shapes=[pltpu.VMEM((1, ROUTER_PAD), F32)],
        compiler_params=_cparams("arbitrary"),
        name="merge_router",
    )(x, mods, attn, fnet, pool, gates, wts["w_attn_o"], wts["w_fnet_o"], wts["w_pool_o"], wts["w_out"],
      wts["norm2_g"], wts["w_router3"], wts["b_router"], consts["tri"])


def _sc_row_gather(table, idx):
    n = idx.shape[0]
    d = table.shape[1]
    info = plsc.get_sparse_core_info()
    n_cores = info.num_cores
    n_workers = n_cores * info.num_subcores
    per_worker = n // n_workers
    n_chunks = per_worker // SC_GATHER_ROWS
    assert n_chunks * SC_GATHER_ROWS * n_workers == n
    mesh = plsc.VectorSubcoreMesh(core_axis_name="core", subcore_axis_name="subcore")

    @functools.partial(
        pl.kernel, mesh=mesh, out_type=jax.ShapeDtypeStruct((n, d), table.dtype),
        scratch_types=[pltpu.VMEM((SC_GATHER_ROWS,), jnp.int32), pltpu.VMEM((SC_GATHER_ROWS, d), table.dtype),
                       pltpu.SemaphoreType.DMA],
        name="sc_row_gather")
    def gather(table_hbm, idx_hbm, out_hbm, idx_v, rows_v, sem):
        worker = lax.axis_index("subcore") * n_cores + lax.axis_index("core")

        @pl.loop(0, n_chunks)
        def _(c):
            base = worker * per_worker + c * SC_GATHER_ROWS
            pltpu.sync_copy(idx_hbm.at[pl.ds(base, SC_GATHER_ROWS)], idx_v)
            pltpu.async_copy(table_hbm.at[idx_v], rows_v, sem).wait()
            pltpu.sync_copy(rows_v, out_hbm.at[pl.ds(base, SC_GATHER_ROWS)])

    return gather(table, idx)


def _expert_kernel(be_ref, nv_ref, nb_ref, xs_ref, wgu_ref, bgu_ref, wd_ref, bd_ref, y_ref, wgu_bf, wd_bf):
    b = pl.program_id(0)

    def ffn(rows):
        w = xs_ref[0:rows, :]
        x = jnp.concatenate([pltpu.bitcast(w & jnp.int32(-65536), F32),
                             pltpu.bitcast(lax.shift_left(w, 16), F32)], axis=1).astype(BF16)
        y = None
        for c0 in range(0, D_FF, FF_CHUNK):
            glu = _dot(x, wgu_bf[:, c0:c0 + FF_CHUNK]) + bgu_ref[:, c0:c0 + FF_CHUNK]
            lin = _dot(x, wgu_bf[:, D_FF + c0:D_FF + c0 + FF_CHUNK]) + bgu_ref[:, D_FF + c0:D_FF + c0 + FF_CHUNK]
            glu = jnp.minimum(glu, SWIGLU_LIMIT)
            lin = jnp.clip(lin, -SWIGLU_LIMIT, SWIGLU_LIMIT)
            act = glu * jax.nn.sigmoid(SWIGLU_ALPHA * glu) * (lin + 1.0)
            part = _dot(act.astype(BF16), wd_bf[c0:c0 + FF_CHUNK, :])
            y = part if y is None else y + part
        y_ref[0:rows, :] = y + bd_ref[...]

    @pl.when(b < nb_ref[0])
    def _():
        prev = be_ref[jnp.maximum(b - 1, 0)]

        @pl.when((b == 0) | (be_ref[b] != prev))
        def _():
            wgu_bf[...] = wgu_ref[...].astype(BF16)
            wd_bf[...] = wd_ref[...].astype(BF16)

        @pl.when(nv_ref[b] > MOE_HALF)
        def _():
            ffn(MOE_BLOCK)

        @pl.when(nv_ref[b] <= MOE_HALF)
        def _():
            ffn(MOE_HALF)
            y_ref[MOE_HALF:, :] = jnp.zeros((MOE_BLOCK - MOE_HALF, D_MODEL), y_ref.dtype)

    @pl.when(b >= nb_ref[0])
    def _():
        y_ref[...] = jnp.zeros(y_ref.shape, y_ref.dtype)


def _experts(xs, block_exp, block_valid, n_used, l, w_gu, b_gu4, w_down, b_down4):
    n_slots = xs.shape[0]
    bm = MOE_BLOCK
    n_blocks = n_slots // bm

    def blk(b, nb):
        return jnp.minimum(b, nb[0] - 1)

    def wspec(shape):
        return pl.BlockSpec((None, None) + shape, lambda b, be, nv, nb: (l, be[blk(b, nb)], 0, 0))

    grid_spec = pltpu.PrefetchScalarGridSpec(
        num_scalar_prefetch=3,
        grid=(n_blocks,),
        in_specs=[
            pl.BlockSpec((bm, D_MODEL // 2), lambda b, be, nv, nb: (blk(b, nb), 0)),
            wspec((D_MODEL, 2 * D_FF)), wspec((1, 2 * D_FF)), wspec((D_FF, D_MODEL)), wspec((1, D_MODEL)),
        ],
        out_specs=pl.BlockSpec((bm, D_MODEL), lambda b, be, nv, nb: (b, 0)),
        scratch_shapes=[pltpu.VMEM((D_MODEL, 2 * D_FF), BF16), pltpu.VMEM((D_FF, D_MODEL), BF16)],
    )
    return pl.pallas_call(
        _expert_kernel,
        out_shape=jax.ShapeDtypeStruct((n_slots, D_MODEL), F32),
        grid_spec=grid_spec,
        compiler_params=_cparams("arbitrary"),
        name="moe_experts",
    )(block_exp, block_valid, n_used, xs, w_gu, b_gu4, w_down, b_down4)


def _combine_kernel(tw_ref, x_ref, mod_ref, yg_ref, o_ref):
    acc = tw_ref[:, 0:1] * yg_ref[0]
    for k in range(1, TOP_K):
        acc = acc + tw_ref[:, k:k + 1] * yg_ref[k]
    gate2 = mod_ref[5:6, :]
    o_ref[...] = x_ref[...] + gate2 * acc


def _combine(yg, tw, x1, mods, l, tok):
    t = x1.shape[0]
    tc = COMBINE_TILE
    p_tiles = tok["t_prompt"] // tc
    s_tiles = tok["dec_seq"] // tc

    def mod_idx(i):
        row = jnp.where(i < p_tiles, 0, 1 + (i - p_tiles) // s_tiles)
        return (l, row, 0, 0)

    return pl.pallas_call(
        _combine_kernel,
        out_shape=jax.ShapeDtypeStruct((t, D_MODEL), F32),
        grid=(t // tc,),
        in_specs=[
            pl.BlockSpec((tc, ROUTER_PAD), lambda i: (i, 0)),
            pl.BlockSpec((tc, D_MODEL), lambda i: (i, 0)),
            pl.BlockSpec((None, None, 6, D_MODEL), mod_idx),
            pl.BlockSpec((TOP_K, tc, D_MODEL), lambda i: (0, i, 0)),
        ],
        out_specs=pl.BlockSpec((tc, D_MODEL), lambda i: (i, 0)),
        compiler_params=_cparams("arbitrary"),
        name="moe_combine",
    )(tw, x1, mods, yg)


def _slot_tokens(slot, counts, padded_start, padded_end, n_slots, t):
    n = slot.shape[0]
    j = jnp.arange(MOE_BLOCK, dtype=jnp.int32)
    pad_key = (padded_start + counts)[:, None] + j[None, :]
    pad_key = jnp.where(pad_key < padded_end[:, None], pad_key, n_slots).reshape(N_EXPERTS * MOE_BLOCK)
    pad_val = jnp.arange(N_EXPERTS * MOE_BLOCK, dtype=jnp.int32) % t
    keys = jnp.concatenate([slot, pad_key])
    vals = jnp.concatenate([jnp.arange(n, dtype=jnp.int32) // TOP_K, pad_val])
    tok_bits = max(1, (t - 1).bit_length())
    assert (n_slots + 1) << tok_bits < 2 ** 31
    packed = jnp.sort(lax.shift_left(keys, tok_bits) | vals)
    return packed[:n_slots] & ((1 << tok_bits) - 1)


def _route(te_pad, rank_pad, counts_pad, n_slots):
    t = te_pad.shape[0]
    flat_e = te_pad[:, :TOP_K].reshape(t * TOP_K)
    onehot = (flat_e[:, None] == jnp.arange(N_EXPERTS, dtype=jnp.int32)[None, :]).astype(jnp.int32)
    rank = rank_pad[:, :TOP_K].reshape(t * TOP_K)
    counts = counts_pad[0, :N_EXPERTS]
    padded = (counts + MOE_BLOCK - 1) // MOE_BLOCK * MOE_BLOCK
    padded_end = jnp.cumsum(padded)
    padded_start = padded_end - padded
    slot = jnp.sum(onehot * padded_start[None, :], axis=1) + rank
    n_blocks = n_slots // MOE_BLOCK
    block_row0 = jnp.arange(n_blocks, dtype=jnp.int32) * MOE_BLOCK
    block_exp = jnp.minimum(
        jnp.sum((padded_end[None, :] <= block_row0[:, None]).astype(jnp.int32), axis=1),
        N_EXPERTS - 1).astype(jnp.int32)
    exp_hot = (block_exp[:, None] == jnp.arange(N_EXPERTS, dtype=jnp.int32)[None, :]).astype(jnp.int32)
    rows_end = jnp.sum(exp_hot * (padded_start + counts)[None, :], axis=1)
    block_valid = jnp.clip(rows_end - block_row0, 0, MOE_BLOCK).astype(jnp.int32)
    n_used = (padded_end[-1:] // MOE_BLOCK).astype(jnp.int32)
    slot = slot.astype(jnp.int32)
    slot_tok = _slot_tokens(slot, counts, padded_start, padded_end, n_slots, t)
    slot_by_choice = slot.reshape(t, TOP_K).T.reshape(TOP_K * t)
    return slot_by_choice, slot_tok, block_exp, block_valid, n_used


def _segment_matrices():
    m_q = np.zeros((QK_PAD, LANES), np.float32)
    m_k = np.zeros((QK_PAD, LANES), np.float32)
    inv_cnt_q = np.ones((1, LANES), np.float32)
    for h in range(N_HEADS):
        m_q[h * HEAD_PAD:h * HEAD_PAD + QK_NOPE, 2 * h] = 1.0
        m_q[h * HEAD_PAD + QK_NOPE:h * HEAD_PAD + QK_HEAD, 2 * h + 1] = 1.0
        inv_cnt_q[0, 2 * h] = 1.0 / QK_NOPE
        inv_cnt_q[0, 2 * h + 1] = 1.0 / ROPE_DIM
        m_k[h * HEAD_PAD:h * HEAD_PAD + QK_NOPE, h] = 1.0
    dup = lambda m: np.concatenate([m.T, m.T], axis=0)
    return dict(m_q=jnp.asarray(m_q, BF16), mt2_q=jnp.asarray(dup(m_q), BF16), inv_cnt_q=jnp.asarray(inv_cnt_q),
                m_k=jnp.asarray(m_k, BF16), mt2_k=jnp.asarray(dup(m_k), BF16))


def _rope_lane_tables(n_tokens, ident_rows):
    rows = n_tokens // GRID_W
    row = jnp.broadcast_to(jnp.arange(rows)[:, None], (rows, GRID_W)).reshape(n_tokens)
    col = jnp.broadcast_to(jnp.arange(GRID_W)[None, :], (rows, GRID_W)).reshape(n_tokens)
    n_freq = ROPE_DIM // 4
    inv = 1.0 / (ROPE_THETA ** (jnp.arange(n_freq, dtype=F32) / n_freq))
    ang = jnp.concatenate([row[:, None].astype(F32) * inv, col[:, None].astype(F32) * inv], axis=-1)
    cos, sin = jnp.cos(ang), jnp.sin(ang)
    ones = jnp.ones((n_tokens, QK_NOPE), F32)
    tail = HEAD_PAD - QK_HEAD
    cosf = jnp.concatenate([ones, cos, cos, jnp.ones((n_tokens, tail), F32)], axis=1)
    sinf = jnp.concatenate([0.0 * ones, -sin, sin, jnp.zeros((n_tokens, tail), F32)], axis=1)
    cosf = jnp.concatenate([cosf, jnp.ones((ident_rows, HEAD_PAD), F32)], axis=0)
    sinf = jnp.concatenate([sinf, jnp.zeros((ident_rows, HEAD_PAD), F32)], axis=0)
    return cosf, sinf


def _dft_tables(seq_len):
    def cs(n):
        k = np.arange(n, dtype=np.int64)
        ang = 2.0 * np.pi * ((k[:, None] * k[None, :]) % n).astype(np.float64) / n
        return np.cos(ang).astype(np.float32), np.sin(ang).astype(np.float32)

    cl, sl = cs(seq_len)
    cc, sc = cs(FNET_GC)
    return dict(cos=jnp.asarray(cl).astype(BF16), sin=jnp.asarray(sl).astype(BF16),
                chan=jnp.asarray(np.concatenate([cc, sc], axis=1)).astype(BF16))


def _layout_weights(w_in, w_q_b, w_kv_b, q_nope_g, q_rope_g, k_nope_g, k_rope_g, w_router, b_router):
    depth = w_in.shape[0]
    w_in_p = jnp.concatenate(
        [w_in[:, :, :COL_KPE + ROPE_DIM], jnp.zeros((depth, D_MODEL, KPE_PAD - ROPE_DIM), w_in.dtype),
         w_in[:, :, COL_KPE + ROPE_DIM:]], axis=2).astype(BF16)
    w_q = jnp.pad(w_q_b.reshape(depth, Q_RANK, N_HEADS, QK_HEAD),
                  ((0, 0), (0, 0), (0, 0), (0, HEAD_PAD - QK_HEAD))).reshape(depth, Q_RANK, QK_PAD).astype(BF16)
    kv = w_kv_b.reshape(depth, KV_RANK, N_HEADS, QK_NOPE + V_HEAD)
    w_k = jnp.pad(kv[..., :QK_NOPE], ((0, 0), (0, 0), (0, 0), (0, HEAD_PAD - QK_NOPE))
                  ).reshape(depth, KV_RANK, QK_PAD).astype(BF16)
    w_v = kv[..., QK_NOPE:].reshape(depth, KV_RANK, V_WIDTH).astype(BF16)
    zq = jnp.zeros((depth, HEAD_PAD - QK_HEAD), F32)
    g_q = jnp.tile(jnp.concatenate([q_nope_g, q_rope_g, zq], axis=1), (1, N_HEADS))[:, None, :]
    g_k = jnp.tile(jnp.concatenate([k_nope_g, jnp.zeros((depth, HEAD_PAD - QK_NOPE), F32)], axis=1),
                   (1, N_HEADS))[:, None, :]
    g_kpe = jnp.concatenate([jnp.zeros((depth, QK_NOPE), F32), k_rope_g, zq], axis=1)[:, None, :]
    w_hi = w_router.astype(BF16)
    w_lo = (w_router - w_hi.astype(F32)).astype(BF16)
    w_r3 = jnp.pad(jnp.concatenate([w_hi, w_hi, w_lo], axis=1), ((0, 0), (0, 0), (0, ROUTER_PAD - N_EXPERTS)))
    b_r = jnp.pad(b_router, ((0, 0), (0, ROUTER_PAD - N_EXPERTS)), constant_values=NEG_BIG)[:, None, :]
    return dict(w_in=w_in_p, w_q=w_q, w_k=w_k, w_v=w_v, g_q=g_q, g_k=g_k, g_kpe=g_kpe, w_router3=w_r3, b_router=b_r)


def kernel(x_prompt, x_sample, cache_ckv, cache_kpe, c, c_ctx, w_mod, b_mod, norm1_g, norm2_g, w_in, q_a_g, kv_a_g, w_q_b, w_kv_b, q_nope_g, q_rope_g, k_nope_g, k_rope_g, w_attn_o, w_fnet_o, w_pool_grp, pool_scale, w_pool_o, w_out, w_router, b_router, w_gu, b_gu, w_down, b_down):
    batch, seq, d = x_prompt.shape
    dec_batch, dec_seq, _ = x_sample.shape
    depth = w_mod.shape[0]
    past = cache_ckv.shape[2]
    t_prompt = batch * seq
    t_sample = dec_batch * dec_seq
    t = t_prompt + t_sample
    tok = dict(t_prompt=t_prompt, dec_seq=dec_seq)
    assert d == D_MODEL and dec_batch + 1 <= 8
    assert t_prompt % TOKEN_TILE == 0 and dec_seq % TOKEN_TILE == 0 and seq % COMBINE_TILE == 0
    assert past == seq and dec_seq % GRID_W == 0

    consts = _segment_matrices()
    consts["cosf"], consts["sinf"] = _rope_lane_tables(dec_seq, TOKEN_TILE)
    consts["tri"] = jnp.asarray(np.tril(np.ones((TOKEN_TILE, TOKEN_TILE), np.float32), -1), BF16)
    dft_p = _dft_tables(seq)
    dft_s = _dft_tables(dec_seq)

    wts = _layout_weights(w_in, w_q_b, w_kv_b, q_nope_g, q_rope_g, k_nope_g, k_rope_g, w_router, b_router)
    row3 = lambda a: a[:, None, :]
    wts.update(norm1_g=row3(norm1_g), norm2_g=row3(norm2_g), q_a_g=row3(q_a_g), kv_a_g=row3(kv_a_g),
               w_attn_o=w_attn_o.astype(BF16), w_fnet_o=w_fnet_o.astype(BF16), w_pool_o=w_pool_o.astype(BF16),
               w_out=w_out.astype(BF16), w_pool_grp=w_pool_grp.astype(BF16), pool_scale=row3(pool_scale))
    b_gu4 = b_gu[:, :, None, :]
    b_down4 = b_down[:, :, None, :]

    cond8 = jnp.concatenate([c_ctx[None, :], c, jnp.zeros((8 - 1 - dec_batch, d), F32)], axis=0)
    mods = _modulation(cond8, w_mod, b_mod).reshape(depth, 8, 6, D_MODEL)

    kpe_pad = jnp.pad(cache_kpe, ((0, 0), (0, 0), (0, 0), (QK_NOPE, HEAD_PAD - QK_HEAD)))
    k_ctx, v_ctx = _ctx_keys(cache_ckv, kpe_pad, wts, consts)

    n_assign = t * TOP_K
    n_slots = (n_assign + N_EXPERTS * (MOE_BLOCK - 1) + MOE_BLOCK - 1) // MOE_BLOCK * MOE_BLOCK

    x = jnp.concatenate([x_prompt.reshape(t_prompt, d), x_sample.reshape(t_sample, d)], axis=0)
    ckv_list, kpe_list = [], []
    for l in range(depth):
        q, k, v, ckv, kpe, f_in, p_in, gates = _in_projection(x, mods, l, wts, consts, tok)
        ckv_list.append(ckv[:t_prompt].reshape(batch, seq, KV_RANK))
        kpe_list.append(kpe[:t_prompt].reshape(batch, seq, ROPE_DIM))

        attn_p = _attention(q, [(k, v, 0, seq)], batch, seq, 0)
        attn_s = _attention(q, [(k, v, t_prompt // dec_seq, dec_seq), (k_ctx[l], v_ctx[l], 0, past)],
                            dec_batch, dec_seq, t_prompt)
        fnet = jnp.concatenate([_fourier(f_in, batch, seq, 0, dft_p),
                                _fourier(f_in, dec_batch, dec_seq, t_prompt, dft_s)], axis=0)
        pool = jnp.concatenate([_pooling(p_in, batch, seq, 0, l, wts),
                                _pooling(p_in, dec_batch, dec_seq, t_prompt, l, wts)], axis=0)
        attn = jnp.concatenate([attn_p, attn_s], axis=0)

        x1, h2p, te_pad, tw_pad, rank_pad, counts_pad = _merge(x, mods, attn, fnet, pool, gates, l, wts, consts, tok)
        slot_by_choice, slot_tok, block_exp, block_valid, n_used = _route(te_pad, rank_pad, counts_pad, n_slots)
        xs = _sc_row_gather(h2p, slot_tok)
        y = _experts(xs, block_exp, block_valid, n_used, l, w_gu, b_gu4, w_down, b_down4)
        yg = _sc_row_gather(y, slot_by_choice).reshape(TOP_K, t, D_MODEL)
        x = _combine(yg, tw_pad, x1, mods, l, tok)

    y_prompt = x[:t_prompt].reshape(batch, seq, d)
    y_sample = x[t_prompt:].reshape(dec_batch, dec_seq, d)
    return (y_prompt, y_sample, jnp.stack(ckv_list, axis=1), jnp.stack(kpe_list, axis=1))
```

```python
import functools
import math

import numpy as np
import jax
import jax.numpy as jnp
from jax import lax
from jax.experimental import pallas as pl
from jax.experimental.pallas import tpu as pltpu
from jax.experimental.pallas import tpu_sc as plsc

D_MODEL = 1024
GRID_W = 64
N_HEADS = 8
QK_NOPE = 64
ROPE_DIM = 32
V_HEAD = 64
QK_HEAD = QK_NOPE + ROPE_DIM
Q_RANK = 384
KV_RANK = 256
ROPE_THETA = 10000.0
FNET_GROUPS = 4
FNET_GC = 128
FNET_WIDTH = FNET_GROUPS * FNET_GC
POOL_WINDOWS = (2, 4, 8, 16)
POOL_GC = 128
POOL_WIDTH = len(POOL_WINDOWS) * POOL_GC
N_BRANCHES = 3
N_EXPERTS = 32
TOP_K = 4
D_FF = D_MODEL
SWIGLU_LIMIT = 7.0
SWIGLU_ALPHA = 1.702
RMS_EPS = 1e-6

LANES = 128
HEAD_PAD = LANES
QK_PAD = N_HEADS * HEAD_PAD
V_WIDTH = N_HEADS * V_HEAD
ROPE_HALF = ROPE_DIM // 2
KPE_PAD = LANES
COL_KV = Q_RANK
COL_KPE = Q_RANK + KV_RANK
COL_F = COL_KPE + KPE_PAD
COL_P = COL_F + FNET_WIDTH
COL_G = COL_P + POOL_WIDTH
IN_PAD_WIDTH = COL_G + N_BRANCHES * D_MODEL
ROUTER_PAD = LANES
NEG_BIG = -1e30

TOKEN_TILE = 512
IN_TILE = 256
ATTN_Q_TILE = 512
FNET_ROW_TILE = 512
POOL_CHUNK = 256
POOL_HALO = 16
MOE_BLOCK = 512
MOE_STEP = 128
GATE_COL_CHUNK = 512
MERGE_COL_CHUNK = 256
FF_CHUNK = 256
COMBINE_TILE = 256
SC_GATHER_ROWS = 64
MOD_COL_TILE = 1536
VMEM_LIMIT = 56 * 1024 * 1024

F32 = jnp.float32
BF16 = jnp.bfloat16


def _cparams(*sem):
    return pltpu.CompilerParams(dimension_semantics=sem, vmem_limit_bytes=VMEM_LIMIT)


def _dot(a, b):
    return jnp.dot(a, b, preferred_element_type=F32)


def _rms(x):
    return x * lax.rsqrt(jnp.mean(x * x, axis=-1, keepdims=True) + RMS_EPS)


def _seg_rms_scale(x, m_ref, mt2_ref, inv_cnt):
    ss = _dot((x * x).astype(BF16), m_ref[...])
    r = lax.rsqrt(ss * inv_cnt + RMS_EPS)
    r_hi = r.astype(BF16)
    r_lo = (r - r_hi.astype(F32)).astype(BF16)
    return _dot(jnp.concatenate([r_hi, r_lo], axis=1), mt2_ref[...])


def _rope_chunk(xc, cosf, sinf, first_half):
    partner = jnp.where(first_half, pltpu.roll(xc, HEAD_PAD - ROPE_HALF, 1), pltpu.roll(xc, ROPE_HALF, 1))
    return xc * cosf + partner * sinf


def _write_keys(ckv, kpe_chunk, wk_ref, wv_ref, gk_ref, gkpe_ref, mk_ref, mkt2_ref, rope, k_out, v_out):
    cb = ckv.astype(BF16)
    kn = _dot(cb, wk_ref[...])
    kn = kn * _seg_rms_scale(kn, mk_ref, mkt2_ref, 1.0 / QK_NOPE) * gk_ref[...]
    v_out[...] = _dot(cb, wv_ref[...]).astype(v_out.dtype)
    ms = jnp.sum(kpe_chunk * kpe_chunk, axis=-1, keepdims=True) * (1.0 / ROPE_DIM)
    kp = kpe_chunk * lax.rsqrt(ms + RMS_EPS) * gkpe_ref[...]
    if rope is not None:
        kp = _rope_chunk(kp, *rope)
    for h in range(N_HEADS):
        sl = slice(h * HEAD_PAD, (h + 1) * HEAD_PAD)
        k_out[:, sl] = (kn[:, sl] + kp).astype(k_out.dtype)


def _mod_kernel(c_ref, w_ref, b_ref, o_ref):
    c = c_ref[...]
    s = c * jax.nn.sigmoid(c)
    o_ref[...] = _dot(s.astype(BF16), w_ref[...].astype(BF16)) + b_ref[...]


def _modulation(cond8, w_mod, b_mod):
    depth = w_mod.shape[0]
    n = w_mod.shape[2]
    return pl.pallas_call(
        _mod_kernel,
        out_shape=jax.ShapeDtypeStruct((depth, 8, n), F32),
        grid=(depth, n // MOD_COL_TILE),
        in_specs=[
            pl.BlockSpec((8, D_MODEL), lambda l, j: (0, 0)),
            pl.BlockSpec((None, D_MODEL, MOD_COL_TILE), lambda l, j: (l, 0, j)),
            pl.BlockSpec((None, 1, MOD_COL_TILE), lambda l, j: (l, 0, j)),
        ],
        out_specs=pl.BlockSpec((None, 8, MOD_COL_TILE), lambda l, j: (l, 0, j)),
        compiler_params=_cparams("arbitrary", "arbitrary"),
        name="adaln_modulation",
    )(cond8, w_mod, b_mod.reshape(depth, 1, n))


def _moe_residual(tw_ref, x1_ref, mod_ref, yg_ref):
    acc = tw_ref[:, 0:1] * yg_ref[0]
    for k in range(1, TOP_K):
        acc = acc + tw_ref[:, k:k + 1] * yg_ref[k]
    return x1_ref[...] + mod_ref[5:6, :] * acc


def _in_kernel(*refs, after_moe):
    if after_moe:
        x = _moe_residual(*refs[:4])
        refs = refs[4:]
        refs[-1][...] = x
        refs = refs[:-1]
    else:
        x = refs[0][...]
        refs = refs[1:]
    (mod_ref, g1_ref, w_ref, qag_ref, kvag_ref, wq_ref, wk_ref, wv_ref,
     gq_ref, gk_ref, gkpe_ref, cos_ref, sin_ref, mq_ref, mqt2_ref, icq_ref, mk_ref, mkt2_ref,
     q_out, k_out, v_out, ckv_out, kpe_out, f_out, p_out, g_out) = refs
    shift = mod_ref[0:1, :]
    scale = mod_ref[1:2, :]
    h = _rms(x) * g1_ref[...] * (1.0 + scale) + shift
    hb = h.astype(BF16)

    za = _dot(hb, w_ref[:, 0:COL_F])
    q_lat = za[:, 0:COL_KV]
    kv_lat = za[:, COL_KV:COL_KPE]
    kpe_grp = za[:, COL_KPE:COL_F]
    ckv = _rms(kv_lat) * kvag_ref[...]
    ckv_out[...] = ckv
    kpe_out[...] = kpe_grp[:, 0:ROPE_DIM]

    cosf = cos_ref[...]
    sinf = sin_ref[...]
    lane = lax.broadcasted_iota(jnp.int32, cosf.shape, 1)
    rope = (cosf, sinf, lane < QK_NOPE + ROPE_HALF)

    qn = (_rms(q_lat) * qag_ref[...]).astype(BF16)
    q = _dot(qn, wq_ref[...])
    q = q * _seg_rms_scale(q, mq_ref, mqt2_ref, icq_ref[...]) * gq_ref[...]
    sm_scale = 1.0 / math.sqrt(QK_HEAD)
    for hd in range(N_HEADS):
        sl = slice(hd * HEAD_PAD, (hd + 1) * HEAD_PAD)
        q_out[:, sl] = (_rope_chunk(q[:, sl], *rope) * sm_scale).astype(q_out.dtype)

    kpe_chunk = pltpu.roll(kpe_grp, QK_NOPE, 1)
    _write_keys(ckv, kpe_chunk, wk_ref, wv_ref, gk_ref, gkpe_ref, mk_ref, mkt2_ref, rope, k_out, v_out)

    zb = _dot(hb, w_ref[:, COL_F:COL_G])
    f_out[...] = zb[:, 0:FNET_WIDTH].astype(f_out.dtype)
    p_out[...] = zb[:, FNET_WIDTH:]
    for c0 in range(0, N_BRANCHES * D_MODEL, GATE_COL_CHUNK):
        zc = _dot(hb, w_ref[:, COL_G + c0:COL_G + c0 + GATE_COL_CHUNK])
        g_out[:, c0:c0 + GATE_COL_CHUNK] = jax.nn.sigmoid(zc).astype(g_out.dtype)


def _const_spec(shape):
    nd = len(shape)
    return pl.BlockSpec(shape, lambda i, _n=nd: (0,) * _n, pipeline_mode=pl.Buffered(1))


def _layer_spec(shape, l):
    nd = len(shape)
    return pl.BlockSpec((None,) + tuple(shape), lambda i, _l=l, _n=nd: (_l,) + (0,) * _n,
                        pipeline_mode=pl.Buffered(1))


def _in_projection(x, moe, mods, l, wts, consts, tok):
    after_moe = x is None
    t = moe[0].shape[0] if after_moe else x.shape[0]
    tm = IN_TILE
    n_tiles = t // tm
    p_tiles = tok["t_prompt"] // tm
    s_tiles = tok["dec_seq"] // tm
    rope_blocks = tok["dec_seq"] // tm

    def mod_row(i):
        return jnp.where(i < p_tiles, 0, 1 + (i - p_tiles) // s_tiles)

    def rope_idx(i):
        return (jnp.where(i < p_tiles, rope_blocks, (i - p_tiles) % s_tiles), 0)

    row = lambda w: pl.BlockSpec((tm, w), lambda i: (i, 0))
    if after_moe:
        x1, tw, yg = moe
        lead_specs = [row(ROUTER_PAD), row(D_MODEL),
                      pl.BlockSpec((None, None, 6, D_MODEL), lambda i: (l - 1, mod_row(i), 0, 0)),
                      pl.BlockSpec((TOP_K, tm, D_MODEL), lambda i: (0, i, 0))]
        lead_args = [tw, x1, mods, yg]
    else:
        lead_specs = [row(D_MODEL)]
        lead_args = [x]
    in_specs = lead_specs + [
        pl.BlockSpec((None, None, 6, D_MODEL), lambda i: (l, mod_row(i), 0, 0)),
        _layer_spec((1, D_MODEL), l),
        _layer_spec((D_MODEL, IN_PAD_WIDTH), l),
        _layer_spec((1, Q_RANK), l),
        _layer_spec((1, KV_RANK), l),
        _layer_spec((Q_RANK, QK_PAD), l),
        _layer_spec((KV_RANK, QK_PAD), l),
        _layer_spec((KV_RANK, V_WIDTH), l),
        _layer_spec((1, QK_PAD), l),
        _layer_spec((1, QK_PAD), l),
        _layer_spec((1, HEAD_PAD), l),
        pl.BlockSpec((tm, HEAD_PAD), rope_idx),
        pl.BlockSpec((tm, HEAD_PAD), rope_idx),
        _const_spec((QK_PAD, LANES)),
        _const_spec((2 * LANES, QK_PAD)),
        _const_spec((1, LANES)),
        _const_spec((QK_PAD, LANES)),
        _const_spec((2 * LANES, QK_PAD)),
    ]
    out_shape = [
        jax.ShapeDtypeStruct((t, QK_PAD), BF16),
        jax.ShapeDtypeStruct((t, QK_PAD), BF16),
        jax.ShapeDtypeStruct((t, V_WIDTH), BF16),
        jax.ShapeDtypeStruct((t, KV_RANK), F32),
        jax.ShapeDtypeStruct((t, ROPE_DIM), F32),
        jax.ShapeDtypeStruct((t, FNET_WIDTH), BF16),
        jax.ShapeDtypeStruct((t, POOL_WIDTH), F32),
        jax.ShapeDtypeStruct((t, N_BRANCHES * D_MODEL), BF16),
    ]
    if after_moe:
        out_shape.append(jax.ShapeDtypeStruct((t, D_MODEL), F32))
    out_specs = [row(s.shape[1]) for s in out_shape]
    outs = pl.pallas_call(
        functools.partial(_in_kernel, after_moe=after_moe),
        out_shape=out_shape,
        grid=(n_tiles,),
        in_specs=in_specs,
        out_specs=out_specs,
        compiler_params=_cparams("arbitrary"),
        name="in_projection",
    )(*lead_args, mods, wts["norm1_g"], wts["w_in"], wts["q_a_g"], wts["kv_a_g"], wts["w_q"], wts["w_k"], wts["w_v"],
      wts["g_q"], wts["g_k"], wts["g_kpe"], consts["cosf"], consts["sinf"],
      consts["m_q"], consts["mt2_q"], consts["inv_cnt_q"], consts["m_k"], consts["mt2_k"])
    return (outs[:8], outs[8]) if after_moe else (outs, x)


def _ctx_keys_kernel(ckv_ref, kpe_ref, wk_ref, wv_ref, gk_ref, gkpe_ref, mk_ref, mkt2_ref, k_out, v_out):
    _write_keys(ckv_ref[...], kpe_ref[...], wk_ref, wv_ref, gk_ref, gkpe_ref, mk_ref, mkt2_ref, None, k_out, v_out)


def _ctx_keys(cache_ckv, cache_kpe_pad, wts, consts):
    nb, depth, past, _ = cache_ckv.shape
    lw = lambda shape: pl.BlockSpec((None,) + shape, lambda l, b: (l,) + (0,) * len(shape))
    cs = lambda shape: pl.BlockSpec(shape, lambda l, b: (0,) * len(shape))
    return pl.pallas_call(
        _ctx_keys_kernel,
        out_shape=[jax.ShapeDtypeStruct((depth, nb * past, QK_PAD), BF16),
                   jax.ShapeDtypeStruct((depth, nb * past, V_WIDTH), BF16)],
        grid=(depth, nb),
        in_specs=[
            pl.BlockSpec((None, None, past, KV_RANK), lambda l, b: (b, l, 0, 0)),
            pl.BlockSpec((None, None, past, HEAD_PAD), lambda l, b: (b, l, 0, 0)),
            lw((KV_RANK, QK_PAD)), lw((KV_RANK, V_WIDTH)), lw((1, QK_PAD)), lw((1, HEAD_PAD)),
            cs((QK_PAD, LANES)), cs((2 * LANES, QK_PAD)),
        ],
        out_specs=[pl.BlockSpec((None, past, QK_PAD), lambda l, b: (l, b, 0)),
                   pl.BlockSpec((None, past, V_WIDTH), lambda l, b: (l, b, 0))],
        compiler_params=_cparams("arbitrary", "arbitrary"),
        name="context_keys",
    )(cache_ckv, cache_kpe_pad, wts["w_k"], wts["w_v"], wts["g_k"], wts["g_kpe"], consts["m_k"], consts["mt2_k"])


def _attn_kernel(*refs, n_parts):
    q_ref = refs[0]
    k_refs = refs[1:1 + 2 * n_parts:2]
    v_refs = refs[2:2 + 2 * n_parts:2]
    o_ref = refs[1 + 2 * n_parts]
    lane = lax.broadcasted_iota(jnp.int32, (q_ref.shape[0], 2 * V_HEAD), 1)
    for pair in range(N_HEADS // 2):
        vsl = slice(pair * 2 * V_HEAD, (pair + 1) * 2 * V_HEAD)
        outs = []
        for hd in (2 * pair, 2 * pair + 1):
            sl = slice(hd * HEAD_PAD, (hd + 1) * HEAD_PAD)
            qh = q_ref[:, sl]
            ss = [lax.dot_general(qh, k[:, sl], (((1,), (1,)), ((), ())), preferred_element_type=F32)
                  for k in k_refs]
            m = functools.reduce(jnp.maximum, [jnp.max(s, axis=-1, keepdims=True) for s in ss])
            es = [jnp.exp(s - m) for s in ss]
            den = functools.reduce(jnp.add, [jnp.sum(e, axis=-1, keepdims=True) for e in es])
            acc = functools.reduce(jnp.add, [_dot(e.astype(BF16), v[:, vsl]) for e, v in zip(es, v_refs)])
            outs.append(acc / den)
        o_ref[:, vsl] = jnp.where(lane < V_HEAD, outs[0], outs[1]).astype(o_ref.dtype)


def _attention(q, kv_parts, n_seq, seq_len, q_row0):
    tq = min(ATTN_Q_TILE, seq_len)
    nq = seq_len // tq
    qb0 = q_row0 // tq
    in_specs = [pl.BlockSpec((tq, QK_PAD), lambda b, i: (qb0 + b * nq + i, 0))]
    args = [q]
    for k, v, blk0, rows in kv_parts:
        in_specs.append(pl.BlockSpec((rows, QK_PAD), lambda b, i, _o=blk0: (_o + b, 0)))
        in_specs.append(pl.BlockSpec((rows, V_WIDTH), lambda b, i, _o=blk0: (_o + b, 0)))
        args += [k, v]
    return pl.pallas_call(
        functools.partial(_attn_kernel, n_parts=len(kv_parts)),
        out_shape=jax.ShapeDtypeStruct((n_seq * seq_len, V_WIDTH), BF16),
        grid=(n_seq, nq),
        in_specs=in_specs,
        out_specs=pl.BlockSpec((tq, V_WIDTH), lambda b, i: (b * nq + i, 0)),
        compiler_params=_cparams("arbitrary", "arbitrary"),
        name="attention",
    )(*args)


def _fnet_kernel(f_ref, cs_ref, cl_ref, sl_ref, o_ref, top_ref, bot_ref, *, norm):
    @pl.when(pl.program_id(1) == 0)
    def _():
        for g in range(FNET_GROUPS):
            sl = slice(g * FNET_GC, (g + 1) * FNET_GC)
            a = _dot(f_ref[:, sl], cs_ref[...])
            top_ref[:, sl] = a[:, :FNET_GC].astype(BF16)
            bot_ref[:, sl] = a[:, FNET_GC:].astype(BF16)

    y = _dot(cl_ref[...], top_ref[...]) - _dot(sl_ref[...], bot_ref[...])
    o_ref[...] = (y * norm).astype(o_ref.dtype)


def _fourier(f_in, n_seq, seq_len, row0, tabs):
    tr = min(FNET_ROW_TILE, seq_len)
    nj = seq_len // tr
    sb0 = row0 // seq_len
    return pl.pallas_call(
        functools.partial(_fnet_kernel, norm=1.0 / math.sqrt(seq_len * FNET_GC)),
        out_shape=jax.ShapeDtypeStruct((n_seq * seq_len, FNET_WIDTH), BF16),
        grid=(n_seq, nj),
        in_specs=[
            pl.BlockSpec((seq_len, FNET_WIDTH), lambda b, j: (sb0 + b, 0)),
            pl.BlockSpec((FNET_GC, 2 * FNET_GC), lambda b, j: (0, 0)),
            pl.BlockSpec((tr, seq_len), lambda b, j: (j, 0)),
            pl.BlockSpec((tr, seq_len), lambda b, j: (j, 0)),
        ],
        out_specs=pl.BlockSpec((tr, FNET_WIDTH), lambda b, j: (b * nj + j, 0)),
        scratch_shapes=[pltpu.VMEM((seq_len, FNET_WIDTH), BF16), pltpu.VMEM((seq_len, FNET_WIDTH), BF16)],
        compiler_params=_cparams("arbitrary", "arbitrary"),
        name="fourier_mix",
    )(f_in, tabs["chan"], tabs["cos"], tabs["sin"])


def _pool_kernel(p_ref, wg_ref, ps_ref, o_ref, pad_ref):
    seq_len = p_ref.shape[0]
    zeros = jnp.zeros((POOL_HALO, POOL_WIDTH), F32)
    pad_ref[0:POOL_HALO, :] = zeros
    pad_ref[POOL_HALO + seq_len:, :] = zeros
    pad_ref[POOL_HALO:POOL_HALO + seq_len, :] = p_ref[...]
    ch = min(POOL_CHUNK, seq_len)
    for c in range(seq_len // ch):
        t = lax.broadcasted_iota(jnp.int32, (ch, 1), 0) + c * ch
        for g, w in enumerate(POOL_WINDOWS):
            half = w // 2
            sl = slice(g * POOL_GC, (g + 1) * POOL_GC)
            acc = None
            for j in range(-half, half):
                r0 = POOL_HALO + c * ch + j
                part = pad_ref[r0:r0 + ch, sl]
                acc = part if acc is None else acc + part
            cnt = (jnp.minimum(t + half, seq_len) - jnp.maximum(t - half, 0)).astype(F32)
            pooled = acc / cnt - p_ref[c * ch:(c + 1) * ch, sl]
            mixed = _dot(pooled.astype(BF16), wg_ref[g]) * ps_ref[:, sl]
            o_ref[c * ch:(c + 1) * ch, sl] = mixed.astype(o_ref.dtype)


def _pooling(p_in, n_seq, seq_len, row0, l, wts):
    sb0 = row0 // seq_len
    g = len(POOL_WINDOWS)
    return pl.pallas_call(
        _pool_kernel,
        out_shape=jax.ShapeDtypeStruct((n_seq * seq_len, POOL_WIDTH), BF16),
        grid=(n_seq,),
        in_specs=[
            pl.BlockSpec((seq_len, POOL_WIDTH), lambda b: (sb0 + b, 0)),
            pl.BlockSpec((None, g, POOL_GC, POOL_GC), lambda b: (l, 0, 0, 0)),
            pl.BlockSpec((None, 1, POOL_WIDTH), lambda b: (l, 0, 0)),
        ],
        out_specs=pl.BlockSpec((seq_len, POOL_WIDTH), lambda b: (b, 0)),
        scratch_shapes=[pltpu.VMEM((seq_len + 2 * POOL_HALO, POOL_WIDTH), F32)],
        compiler_params=_cparams("arbitrary"),
        name="pool_mix",
    )(p_in, wts["w_pool_grp"], wts["pool_scale"])


def _merge_kernel(x_ref, mod_ref, ap_ref, as_ref, fp_ref, fs_ref, pp_ref, ps_ref, g_ref,
                  wa_ref, wf_ref, wp_ref, wo_ref, g2_ref, wr_ref, br_ref, tri_ref,
                  x1_out, hp_out, te_out, tw_out, rk_out, cnt_out, carry_ref, *, prompt_tiles):
    is_prompt = pl.program_id(0) < prompt_tiles
    a_in = jnp.where(is_prompt, ap_ref[...], as_ref[...])
    f_in = jnp.where(is_prompt, fp_ref[...], fs_ref[...])
    p_in = jnp.where(is_prompt, pp_ref[...], ps_ref[...])
    chunks = []
    for c0 in range(0, D_MODEL, MERGE_COL_CHUNK):
        cs = slice(c0, c0 + MERGE_COL_CHUNK)
        a = _dot(a_in, wa_ref[:, cs])
        f = _dot(f_in, wf_ref[:, cs])
        p = _dot(p_in, wp_ref[:, cs])
        chunks.append((g_ref[:, c0:c0 + MERGE_COL_CHUNK].astype(F32) * a
                       + g_ref[:, D_MODEL + c0:D_MODEL + c0 + MERGE_COL_CHUNK].astype(F32) * f
                       + g_ref[:, 2 * D_MODEL + c0:2 * D_MODEL + c0 + MERGE_COL_CHUNK].astype(F32) * p).astype(BF16))
    mix = _dot(jnp.concatenate(chunks, axis=1), wo_ref[...])
    gate1 = mod_ref[2:3, :]
    shift2 = mod_ref[3:4, :]
    scale2 = mod_ref[4:5, :]
    x1 = x_ref[...] + gate1 * mix
    x1_out[...] = x1
    h2 = _rms(x1) * g2_ref[...] * (1.0 + scale2) + shift2

    h_hi = h2.astype(BF16)
    bits = pltpu.bitcast(h_hi.astype(F32), jnp.int32)
    hp_out[...] = bits[:, :D_MODEL // 2] | lax.shift_right_logical(bits[:, D_MODEL // 2:], 16)

    h_lo = (h2 - h_hi.astype(F32)).astype(BF16)
    lhs = jnp.concatenate([h_hi, h_lo, h_hi], axis=1)
    logits = _dot(lhs, wr_ref[...]) + br_ref[...]

    lane = lax.broadcasted_iota(jnp.int32, logits.shape, 1)
    work = logits
    vals, idxs = [], []
    for _ in range(TOP_K):
        m = jnp.max(work, axis=-1, keepdims=True)
        idx = jnp.min(jnp.where(work == m, lane, ROUTER_PAD), axis=-1, keepdims=True)
        vals.append(m)
        idxs.append(idx)
        work = jnp.where(lane == idx, -jnp.inf, work)
    es = [jnp.exp(v - vals[0]) for v in vals]
    den = functools.reduce(jnp.add, es)

    @pl.when(pl.program_id(0) == 0)
    def _():
        carry_ref[...] = jnp.zeros(carry_ref.shape, F32)

    chosen = functools.reduce(jnp.logical_or, [lane == idx for idx in idxs])
    hot = jnp.where(chosen, 1.0, 0.0)
    before = _dot(tri_ref[...], hot.astype(BF16)) + carry_ref[...]
    carry_ref[...] = carry_ref[...] + jnp.sum(hot, axis=0, keepdims=True)
    cnt_out[...] = carry_ref[...].astype(jnp.int32)

    te = jnp.zeros(logits.shape, jnp.int32)
    tw = jnp.zeros(logits.shape, F32)
    rk = jnp.zeros(logits.shape, jnp.int32)
    for k in range(TOP_K):
        rank_k = jnp.sum(jnp.where(lane == idxs[k], before, 0.0), axis=-1, keepdims=True).astype(jnp.int32)
        te = jnp.where(lane == k, idxs[k], te)
        tw = jnp.where(lane == k, es[k] / den, tw)
        rk = jnp.where(lane == k, rank_k, rk)
    te_out[...] = te
    tw_out[...] = tw
    rk_out[...] = rk


def _merge(x, mods, attn, fnet, pool, gates, l, wts, consts, tok):
    t = x.shape[0]
    tm = TOKEN_TILE
    p_tiles = tok["t_prompt"] // tm
    s_tiles = tok["dec_seq"] // tm

    def mod_idx(i):
        row = jnp.where(i < p_tiles, 0, 1 + (i - p_tiles) // s_tiles)
        return (l, row, 0, 0)

    row = lambda w: pl.BlockSpec((tm, w), lambda i: (i, 0))
    prow = lambda w: pl.BlockSpec((tm, w), lambda i: (jnp.minimum(i, p_tiles - 1), 0))
    srow = lambda w: pl.BlockSpec((tm, w), lambda i: (jnp.maximum(i - p_tiles, 0), 0))
    out_shape = [
        jax.ShapeDtypeStruct((t, D_MODEL), F32),
        jax.ShapeDtypeStruct((t, D_MODEL // 2), jnp.int32),
        jax.ShapeDtypeStruct((t, ROUTER_PAD), jnp.int32),
        jax.ShapeDtypeStruct((t, ROUTER_PAD), F32),
        jax.ShapeDtypeStruct((t, ROUTER_PAD), jnp.int32),
    ]
    return pl.pallas_call(
        functools.partial(_merge_kernel, prompt_tiles=p_tiles),
        out_shape=out_shape + [jax.ShapeDtypeStruct((1, ROUTER_PAD), jnp.int32)],
        grid=(t // tm,),
        in_specs=[
            row(D_MODEL),
            pl.BlockSpec((None, None, 6, D_MODEL), mod_idx),
            prow(V_WIDTH), srow(V_WIDTH), prow(FNET_WIDTH), srow(FNET_WIDTH), prow(POOL_WIDTH), srow(POOL_WIDTH),
            row(N_BRANCHES * D_MODEL),
            _layer_spec((V_WIDTH, D_MODEL), l),
            _layer_spec((FNET_WIDTH, D_MODEL), l),
            _layer_spec((POOL_WIDTH, D_MODEL), l),
            _layer_spec((D_MODEL, D_MODEL), l),
            _layer_spec((1, D_MODEL), l),
            _layer_spec((3 * D_MODEL, ROUTER_PAD), l),
            _layer_spec((1, ROUTER_PAD), l),
            _const_spec((tm, tm)),
        ],
        out_specs=[row(s.shape[1]) for s in out_shape] + [pl.BlockSpec((1, ROUTER_PAD), lambda i: (0, 0))],
        scratch_shapes=[pltpu.VMEM((1, ROUTER_PAD), F32)],
        compiler_params=_cparams("arbitrary"),
        name="merge_router",
    )(x, mods, *attn, *fnet, *pool, gates, wts["w_attn_o"], wts["w_fnet_o"], wts["w_pool_o"], wts["w_out"],
      wts["norm2_g"], wts["w_router3"], wts["b_router"], consts["tri"])


def _sc_row_gather(table, idx):
    n = idx.shape[0]
    d = table.shape[1]
    info = plsc.get_sparse_core_info()
    n_cores = info.num_cores
    n_workers = n_cores * info.num_subcores
    per_worker = n // n_workers
    n_chunks = per_worker // SC_GATHER_ROWS
    assert n_chunks * SC_GATHER_ROWS * n_workers == n
    mesh = plsc.VectorSubcoreMesh(core_axis_name="core", subcore_axis_name="subcore")

    @functools.partial(
        pl.kernel, mesh=mesh, out_type=jax.ShapeDtypeStruct((n, d), table.dtype),
        scratch_types=[pltpu.VMEM((SC_GATHER_ROWS,), jnp.int32), pltpu.VMEM((SC_GATHER_ROWS, d), table.dtype),
                       pltpu.SemaphoreType.DMA],
        name="sc_row_gather")
    def gather(table_hbm, idx_hbm, out_hbm, idx_v, rows_v, sem):
        worker = lax.axis_index("subcore") * n_cores + lax.axis_index("core")

        @pl.loop(0, n_chunks)
        def _(c):
            base = worker * per_worker + c * SC_GATHER_ROWS
            pltpu.sync_copy(idx_hbm.at[pl.ds(base, SC_GATHER_ROWS)], idx_v)
            pltpu.async_copy(table_hbm.at[idx_v], rows_v, sem).wait()
            pltpu.sync_copy(rows_v, out_hbm.at[pl.ds(base, SC_GATHER_ROWS)])

    return gather(table, idx)


def _expert_kernel(be_ref, nv_ref, nb_ref, xs_ref, wgu_ref, bgu_ref, wd_ref, bd_ref, y_ref, wgu_bf, wd_bf):
    b = pl.program_id(0)

    def ffn(rows):
        w = xs_ref[0:rows, :]
        x = jnp.concatenate([pltpu.bitcast(w & jnp.int32(-65536), F32),
                             pltpu.bitcast(lax.shift_left(w, 16), F32)], axis=1).astype(BF16)
        gu = _dot(x, wgu_bf[...]) + bgu_ref[...]
        glu = jnp.minimum(gu[:, :D_FF], SWIGLU_LIMIT)
        lin = jnp.clip(gu[:, D_FF:], -SWIGLU_LIMIT, SWIGLU_LIMIT)
        act = glu * jax.nn.sigmoid(SWIGLU_ALPHA * glu) * (lin + 1.0)
        y_ref[0:rows, :] = _dot(act.astype(BF16), wd_bf[...]) + bd_ref[...]
        if rows < MOE_BLOCK:
            y_ref[rows:, :] = jnp.zeros((MOE_BLOCK - rows, D_MODEL), y_ref.dtype)

    @pl.when(b < nb_ref[0])
    def _():
        prev = be_ref[jnp.maximum(b - 1, 0)]

        @pl.when((b == 0) | (be_ref[b] != prev))
        def _():
            wgu_bf[...] = wgu_ref[...].astype(BF16)
            wd_bf[...] = wd_ref[...].astype(BF16)

        steps = (nv_ref[b] + (MOE_STEP - 1)) // MOE_STEP
        for rows in range(MOE_STEP, MOE_BLOCK + 1, MOE_STEP):
            want = steps <= 1 if rows == MOE_STEP else steps == rows // MOE_STEP
            pl.when(want)(functools.partial(ffn, rows))

    @pl.when(b >= nb_ref[0])
    def _():
        y_ref[...] = jnp.zeros(y_ref.shape, y_ref.dtype)


def _experts(xs, block_exp, block_valid, n_used, l, w_gu, b_gu4, w_down, b_down4):
    n_slots = xs.shape[0]
    bm = MOE_BLOCK
    n_blocks = n_slots // bm

    def blk(b, nb):
        return jnp.minimum(b, nb[0] - 1)

    def wspec(shape):
        return pl.BlockSpec((None, None) + shape, lambda b, be, nv, nb: (l, be[blk(b, nb)], 0, 0))

    grid_spec = pltpu.PrefetchScalarGridSpec(
        num_scalar_prefetch=3,
        grid=(n_blocks,),
        in_specs=[
            pl.BlockSpec((bm, D_MODEL // 2), lambda b, be, nv, nb: (blk(b, nb), 0)),
            wspec((D_MODEL, 2 * D_FF)), wspec((1, 2 * D_FF)), wspec((D_FF, D_MODEL)), wspec((1, D_MODEL)),
        ],
        out_specs=pl.BlockSpec((bm, D_MODEL), lambda b, be, nv, nb: (b, 0)),
        scratch_shapes=[pltpu.VMEM((D_MODEL, 2 * D_FF), BF16), pltpu.VMEM((D_FF, D_MODEL), BF16)],
    )
    return pl.pallas_call(
        _expert_kernel,
        out_shape=jax.ShapeDtypeStruct((n_slots, D_MODEL), F32),
        grid_spec=grid_spec,
        compiler_params=_cparams("arbitrary"),
        name="moe_experts",
    )(block_exp, block_valid, n_used, xs, w_gu, b_gu4, w_down, b_down4)


def _combine_kernel(tw_ref, x_ref, mod_ref, yg_ref, o_ref):
    o_ref[...] = _moe_residual(tw_ref, x_ref, mod_ref, yg_ref)


def _combine(yg, tw, x1, mods, l, tok):
    t = x1.shape[0]
    tc = COMBINE_TILE
    p_tiles = tok["t_prompt"] // tc
    s_tiles = tok["dec_seq"] // tc

    def mod_idx(i):
        row = jnp.where(i < p_tiles, 0, 1 + (i - p_tiles) // s_tiles)
        return (l, row, 0, 0)

    return pl.pallas_call(
        _combine_kernel,
        out_shape=jax.ShapeDtypeStruct((t, D_MODEL), F32),
        grid=(t // tc,),
        in_specs=[
            pl.BlockSpec((tc, ROUTER_PAD), lambda i: (i, 0)),
            pl.BlockSpec((tc, D_MODEL), lambda i: (i, 0)),
            pl.BlockSpec((None, None, 6, D_MODEL), mod_idx),
            pl.BlockSpec((TOP_K, tc, D_MODEL), lambda i: (0, i, 0)),
        ],
        out_specs=pl.BlockSpec((tc, D_MODEL), lambda i: (i, 0)),
        compiler_params=_cparams("arbitrary"),
        name="moe_combine",
    )(tw, x1, mods, yg)


def _slot_tokens(slot, counts, padded_start, padded_end, n_slots, t):
    n = slot.shape[0]
    j = jnp.arange(MOE_BLOCK, dtype=jnp.int32)
    pad_key = (padded_start + counts)[:, None] + j[None, :]
    pad_key = jnp.where(pad_key < padded_end[:, None], pad_key, n_slots).reshape(N_EXPERTS * MOE_BLOCK)
    pad_val = jnp.arange(N_EXPERTS * MOE_BLOCK, dtype=jnp.int32) % t
    keys = jnp.concatenate([slot, pad_key])
    vals = jnp.concatenate([jnp.arange(n, dtype=jnp.int32) // TOP_K, pad_val])
    tok_bits = max(1, (t - 1).bit_length())
    assert (n_slots + 1) << tok_bits < 2 ** 31
    packed = jnp.sort(lax.shift_left(keys, tok_bits) | vals)
    return packed[:n_slots] & ((1 << tok_bits) - 1)


def _route(te_pad, rank_pad, counts_pad, n_slots):
    t = te_pad.shape[0]
    flat_e = te_pad[:, :TOP_K].reshape(t * TOP_K)
    onehot = (flat_e[:, None] == jnp.arange(N_EXPERTS, dtype=jnp.int32)[None, :]).astype(jnp.int32)
    rank = rank_pad[:, :TOP_K].reshape(t * TOP_K)
    counts = counts_pad[0, :N_EXPERTS]
    padded = (counts + MOE_BLOCK - 1) // MOE_BLOCK * MOE_BLOCK
    padded_end = jnp.cumsum(padded)
    padded_start = padded_end - padded
    slot = jnp.sum(onehot * padded_start[None, :], axis=1) + rank
    n_blocks = n_slots // MOE_BLOCK
    block_row0 = jnp.arange(n_blocks, dtype=jnp.int32) * MOE_BLOCK
    block_exp = jnp.minimum(
        jnp.sum((padded_end[None, :] <= block_row0[:, None]).astype(jnp.int32), axis=1),
        N_EXPERTS - 1).astype(jnp.int32)
    exp_hot = (block_exp[:, None] == jnp.arange(N_EXPERTS, dtype=jnp.int32)[None, :]).astype(jnp.int32)
    rows_end = jnp.sum(exp_hot * (padded_start + counts)[None, :], axis=1)
    block_valid = jnp.clip(rows_end - block_row0, 0, MOE_BLOCK).astype(jnp.int32)
    n_used = (padded_end[-1:] // MOE_BLOCK).astype(jnp.int32)
    slot = slot.astype(jnp.int32)
    slot_tok = _slot_tokens(slot, counts, padded_start, padded_end, n_slots, t)
    slot_by_choice = slot.reshape(t, TOP_K).T.reshape(TOP_K * t)
    return slot_by_choice, slot_tok, block_exp, block_valid, n_used


def _segment_matrices():
    m_q = np.zeros((QK_PAD, LANES), np.float32)
    m_k = np.zeros((QK_PAD, LANES), np.float32)
    inv_cnt_q = np.ones((1, LANES), np.float32)
    for h in range(N_HEADS):
        m_q[h * HEAD_PAD:h * HEAD_PAD + QK_NOPE, 2 * h] = 1.0
        m_q[h * HEAD_PAD + QK_NOPE:h * HEAD_PAD + QK_HEAD, 2 * h + 1] = 1.0
        inv_cnt_q[0, 2 * h] = 1.0 / QK_NOPE
        inv_cnt_q[0, 2 * h + 1] = 1.0 / ROPE_DIM
        m_k[h * HEAD_PAD:h * HEAD_PAD + QK_NOPE, h] = 1.0
    dup = lambda m: np.concatenate([m.T, m.T], axis=0)
    return dict(m_q=jnp.asarray(m_q, BF16), mt2_q=jnp.asarray(dup(m_q), BF16), inv_cnt_q=jnp.asarray(inv_cnt_q),
                m_k=jnp.asarray(m_k, BF16), mt2_k=jnp.asarray(dup(m_k), BF16))


def _rope_lane_tables(n_tokens, ident_rows):
    rows = n_tokens // GRID_W
    row = jnp.broadcast_to(jnp.arange(rows)[:, None], (rows, GRID_W)).reshape(n_tokens)
    col = jnp.broadcast_to(jnp.arange(GRID_W)[None, :], (rows, GRID_W)).reshape(n_tokens)
    n_freq = ROPE_DIM // 4
    inv = 1.0 / (ROPE_THETA ** (jnp.arange(n_freq, dtype=F32) / n_freq))
    ang = jnp.concatenate([row[:, None].astype(F32) * inv, col[:, None].astype(F32) * inv], axis=-1)
    cos, sin = jnp.cos(ang), jnp.sin(ang)
    ones = jnp.ones((n_tokens, QK_NOPE), F32)
    tail = HEAD_PAD - QK_HEAD
    cosf = jnp.concatenate([ones, cos, cos, jnp.ones((n_tokens, tail), F32)], axis=1)
    sinf = jnp.concatenate([0.0 * ones, -sin, sin, jnp.zeros((n_tokens, tail), F32)], axis=1)
    cosf = jnp.concatenate([cosf, jnp.ones((ident_rows, HEAD_PAD), F32)], axis=0)
    sinf = jnp.concatenate([sinf, jnp.zeros((ident_rows, HEAD_PAD), F32)], axis=0)
    return cosf, sinf


def _dft_tables(seq_len):
    def cs(n):
        k = np.arange(n, dtype=np.int64)
        ang = 2.0 * np.pi * ((k[:, None] * k[None, :]) % n).astype(np.float64) / n
        return np.cos(ang).astype(np.float32), np.sin(ang).astype(np.float32)

    cl, sl = cs(seq_len)
    cc, sc = cs(FNET_GC)
    return dict(cos=jnp.asarray(cl).astype(BF16), sin=jnp.asarray(sl).astype(BF16),
                chan=jnp.asarray(np.concatenate([cc, sc], axis=1)).astype(BF16))


def _layout_weights(w_in, w_q_b, w_kv_b, q_nope_g, q_rope_g, k_nope_g, k_rope_g, w_router, b_router):
    depth = w_in.shape[0]
    w_in_p = jnp.concatenate(
        [w_in[:, :, :COL_KPE + ROPE_DIM], jnp.zeros((depth, D_MODEL, KPE_PAD - ROPE_DIM), w_in.dtype),
         w_in[:, :, COL_KPE + ROPE_DIM:]], axis=2).astype(BF16)
    w_q = jnp.pad(w_q_b.reshape(depth, Q_RANK, N_HEADS, QK_HEAD),
                  ((0, 0), (0, 0), (0, 0), (0, HEAD_PAD - QK_HEAD))).reshape(depth, Q_RANK, QK_PAD).astype(BF16)
    kv = w_kv_b.reshape(depth, KV_RANK, N_HEADS, QK_NOPE + V_HEAD)
    w_k = jnp.pad(kv[..., :QK_NOPE], ((0, 0), (0, 0), (0, 0), (0, HEAD_PAD - QK_NOPE))
                  ).reshape(depth, KV_RANK, QK_PAD).astype(BF16)
    w_v = kv[..., QK_NOPE:].reshape(depth, KV_RANK, V_WIDTH).astype(BF16)
    zq = jnp.zeros((depth, HEAD_PAD - QK_HEAD), F32)
    g_q = jnp.tile(jnp.concatenate([q_nope_g, q_rope_g, zq], axis=1), (1, N_HEADS))[:, None, :]
    g_k = jnp.tile(jnp.concatenate([k_nope_g, jnp.zeros((depth, HEAD_PAD - QK_NOPE), F32)], axis=1),
                   (1, N_HEADS))[:, None, :]
    g_kpe = jnp.concatenate([jnp.zeros((depth, QK_NOPE), F32), k_rope_g, zq], axis=1)[:, None, :]
    w_hi = w_router.astype(BF16)
    w_lo = (w_router - w_hi.astype(F32)).astype(BF16)
    w_r3 = jnp.pad(jnp.concatenate([w_hi, w_hi, w_lo], axis=1), ((0, 0), (0, 0), (0, ROUTER_PAD - N_EXPERTS)))
    b_r = jnp.pad(b_router, ((0, 0), (0, ROUTER_PAD - N_EXPERTS)), constant_values=NEG_BIG)[:, None, :]
    return dict(w_in=w_in_p, w_q=w_q, w_k=w_k, w_v=w_v, g_q=g_q, g_k=g_k, g_kpe=g_kpe, w_router3=w_r3, b_router=b_r)


def kernel(x_prompt, x_sample, cache_ckv, cache_kpe, c, c_ctx, w_mod, b_mod, norm1_g, norm2_g, w_in, q_a_g, kv_a_g, w_q_b, w_kv_b, q_nope_g, q_rope_g, k_nope_g, k_rope_g, w_attn_o, w_fnet_o, w_pool_grp, pool_scale, w_pool_o, w_out, w_router, b_router, w_gu, b_gu, w_down, b_down):
    batch, seq, d = x_prompt.shape
    dec_batch, dec_seq, _ = x_sample.shape
    depth = w_mod.shape[0]
    past = cache_ckv.shape[2]
    t_prompt = batch * seq
    t_sample = dec_batch * dec_seq
    t = t_prompt + t_sample
    tok = dict(t_prompt=t_prompt, dec_seq=dec_seq)
    assert d == D_MODEL and dec_batch + 1 <= 8
    assert t_prompt % TOKEN_TILE == 0 and dec_seq % TOKEN_TILE == 0 and seq % COMBINE_TILE == 0
    assert past == seq and dec_seq % GRID_W == 0

    consts = _segment_matrices()
    consts["cosf"], consts["sinf"] = _rope_lane_tables(dec_seq, IN_TILE)
    consts["tri"] = jnp.asarray(np.tril(np.ones((TOKEN_TILE, TOKEN_TILE), np.float32), -1), BF16)
    dft_p = _dft_tables(seq)
    dft_s = _dft_tables(dec_seq)

    wts = _layout_weights(w_in, w_q_b, w_kv_b, q_nope_g, q_rope_g, k_nope_g, k_rope_g, w_router, b_router)
    row3 = lambda a: a[:, None, :]
    wts.update(norm1_g=row3(norm1_g), norm2_g=row3(norm2_g), q_a_g=row3(q_a_g), kv_a_g=row3(kv_a_g),
               w_attn_o=w_attn_o.astype(BF16), w_fnet_o=w_fnet_o.astype(BF16), w_pool_o=w_pool_o.astype(BF16),
               w_out=w_out.astype(BF16), w_pool_grp=w_pool_grp.astype(BF16), pool_scale=row3(pool_scale))
    b_gu4 = b_gu[:, :, None, :]
    b_down4 = b_down[:, :, None, :]

    cond8 = jnp.concatenate([c_ctx[None, :], c, jnp.zeros((8 - 1 - dec_batch, d), F32)], axis=0)
    mods = _modulation(cond8, w_mod, b_mod).reshape(depth, 8, 6, D_MODEL)

    kpe_pad = jnp.pad(cache_kpe, ((0, 0), (0, 0), (0, 0), (QK_NOPE, HEAD_PAD - QK_HEAD)))
    k_ctx, v_ctx = _ctx_keys(cache_ckv, kpe_pad, wts, consts)

    n_assign = t * TOP_K
    n_slots = (n_assign + N_EXPERTS * (MOE_BLOCK - 1) + MOE_BLOCK - 1) // MOE_BLOCK * MOE_BLOCK

    x = jnp.concatenate([x_prompt.reshape(t_prompt, d), x_sample.reshape(t_sample, d)], axis=0)
    ckv_list, kpe_list = [], []
    moe = None
    for l in range(depth):
        (q, k, v, ckv, kpe, f_in, p_in, gates), x = _in_projection(x, moe, mods, l, wts, consts, tok)
        ckv_list.append(ckv[:t_prompt].reshape(batch, seq, KV_RANK))
        kpe_list.append(kpe[:t_prompt].reshape(batch, seq, ROPE_DIM))

        attn = (_attention(q, [(k, v, 0, seq)], batch, seq, 0),
                _attention(q, [(k, v, t_prompt // dec_seq, dec_seq), (k_ctx[l], v_ctx[l], 0, past)],
                           dec_batch, dec_seq, t_prompt))
        fnet = (_fourier(f_in, batch, seq, 0, dft_p), _fourier(f_in, dec_batch, dec_seq, t_prompt, dft_s))
        pool = (_pooling(p_in, batch, seq, 0, l, wts), _pooling(p_in, dec_batch, dec_seq, t_prompt, l, wts))

        x1, h2p, te_pad, tw_pad, rank_pad, counts_pad = _merge(x, mods, attn, fnet, pool, gates, l, wts, consts, tok)
        slot_by_choice, slot_tok, block_exp, block_valid, n_used = _route(te_pad, rank_pad, counts_pad, n_slots)
        xs = _sc_row_gather(h2p, slot_tok)
        y = _experts(xs, block_exp, block_valid, n_used, l, w_gu, b_gu4, w_down, b_down4)
        yg = _sc_row_gather(y, slot_by_choice).reshape(TOP_K, t, D_MODEL)
        moe = (x1, tw_pad, yg)
        x = None
    x = _combine(yg, tw_pad, x1, mods, depth - 1, tok)

    y_prompt = x[:t_prompt].reshape(batch, seq, d)
    y_sample = x[t_prompt:].reshape(dec_batch, dec_seq, d)
    return (y_prompt, y_sample, jnp.stack(ckv_list, axis=1), jnp.stack(kpe_list, axis=1))
```

```python
import functools
import math

import numpy as np
import jax
import jax.numpy as jnp
from jax import lax
from jax.experimental import pallas as pl
from jax.experimental.pallas import tpu as pltpu
from jax.experimental.pallas import tpu_sc as plsc

D_MODEL = 1024
GRID_W = 64
N_HEADS = 8
QK_NOPE = 64
ROPE_DIM = 32
V_HEAD = 64
QK_HEAD = QK_NOPE + ROPE_DIM
Q_RANK = 384
KV_RANK = 256
ROPE_THETA = 10000.0
FNET_GROUPS = 4
FNET_GC = 128
FNET_WIDTH = FNET_GROUPS * FNET_GC
POOL_WINDOWS = (2, 4, 8, 16)
POOL_GC = 128
POOL_WIDTH = len(POOL_WINDOWS) * POOL_GC
N_BRANCHES = 3
N_EXPERTS = 32
TOP_K = 4
D_FF = D_MODEL
SWIGLU_LIMIT = 7.0
SWIGLU_ALPHA = 1.702
RMS_EPS = 1e-6

LANES = 128
HEAD_PAD = LANES
QK_PAD = N_HEADS * HEAD_PAD
V_WIDTH = N_HEADS * V_HEAD
ROPE_HALF = ROPE_DIM // 2
KPE_PAD = LANES
COL_KV = Q_RANK
COL_KPE = Q_RANK + KV_RANK
COL_F = COL_KPE + KPE_PAD
COL_P = COL_F + FNET_WIDTH
COL_G = COL_P + POOL_WIDTH
IN_PAD_WIDTH = COL_G + N_BRANCHES * D_MODEL
ROUTER_PAD = LANES
NEG_BIG = -1e30

TOKEN_TILE = 512
IN_TILE = 256
ATTN_Q_TILE = 512
FNET_ROW_TILE = 512
POOL_CHUNK = 256
POOL_HALO = 16
MOE_BLOCK = 512
MOE_STEP = 128
GATE_COL_CHUNK = 512
MERGE_COL_CHUNK = 256
FF_CHUNK = 256
COMBINE_TILE = 256
SC_GATHER_ROWS = 64
MOD_COL_TILE = 1536
VMEM_LIMIT = 56 * 1024 * 1024

F32 = jnp.float32
BF16 = jnp.bfloat16


def _cparams(*sem):
    return pltpu.CompilerParams(dimension_semantics=sem, vmem_limit_bytes=VMEM_LIMIT)


def _dot(a, b):
    return jnp.dot(a, b, preferred_element_type=F32)


def _rms(x):
    return x * lax.rsqrt(jnp.mean(x * x, axis=-1, keepdims=True) + RMS_EPS)


def _seg_rms_scale(x, m_ref, mt2_ref, inv_cnt):
    ss = _dot((x * x).astype(BF16), m_ref[...])
    r = lax.rsqrt(ss * inv_cnt + RMS_EPS)
    r_hi = r.astype(BF16)
    r_lo = (r - r_hi.astype(F32)).astype(BF16)
    return _dot(jnp.concatenate([r_hi, r_lo], axis=1), mt2_ref[...])


def _rope_chunk(xc, cosf, sinf, first_half):
    partner = jnp.where(first_half, pltpu.roll(xc, HEAD_PAD - ROPE_HALF, 1), pltpu.roll(xc, ROPE_HALF, 1))
    return xc * cosf + partner * sinf


def _write_keys(ckv, kpe_chunk, wk_ref, wv_ref, gk_ref, gkpe_ref, mk_ref, mkt2_ref, rope, k_out, v_out):
    cb = ckv.astype(BF16)
    kn = _dot(cb, wk_ref[...])
    kn = kn * _seg_rms_scale(kn, mk_ref, mkt2_ref, 1.0 / QK_NOPE) * gk_ref[...]
    v_out[...] = _dot(cb, wv_ref[...]).astype(v_out.dtype)
    ms = jnp.sum(kpe_chunk * kpe_chunk, axis=-1, keepdims=True) * (1.0 / ROPE_DIM)
    kp = kpe_chunk * lax.rsqrt(ms + RMS_EPS) * gkpe_ref[...]
    if rope is not None:
        kp = _rope_chunk(kp, *rope)
    for h in range(N_HEADS):
        sl = slice(h * HEAD_PAD, (h + 1) * HEAD_PAD)
        k_out[:, sl] = (kn[:, sl] + kp).astype(k_out.dtype)


def _mod_kernel(c_ref, w_ref, b_ref, o_ref):
    c = c_ref[...]
    s = c * jax.nn.sigmoid(c)
    o_ref[...] = _dot(s.astype(BF16), w_ref[...].astype(BF16)) + b_ref[...]


def _modulation(cond8, w_mod, b_mod):
    depth = w_mod.shape[0]
    n = w_mod.shape[2]
    return pl.pallas_call(
        _mod_kernel,
        out_shape=jax.ShapeDtypeStruct((depth, 8, n), F32),
        grid=(depth, n // MOD_COL_TILE),
        in_specs=[
            pl.BlockSpec((8, D_MODEL), lambda l, j: (0, 0)),
            pl.BlockSpec((None, D_MODEL, MOD_COL_TILE), lambda l, j: (l, 0, j)),
            pl.BlockSpec((None, 1, MOD_COL_TILE), lambda l, j: (l, 0, j)),
        ],
        out_specs=pl.BlockSpec((None, 8, MOD_COL_TILE), lambda l, j: (l, 0, j)),
        compiler_params=_cparams("arbitrary", "arbitrary"),
        name="adaln_modulation",
    )(cond8, w_mod, b_mod.reshape(depth, 1, n))


def _moe_residual(tw_ref, x1_ref, mod_ref, yg_ref):
    acc = tw_ref[:, 0:1] * yg_ref[0]
    for k in range(1, TOP_K):
        acc = acc + tw_ref[:, k:k + 1] * yg_ref[k]
    return x1_ref[...] + mod_ref[5:6, :] * acc


def _in_kernel(*refs, after_moe):
    if after_moe:
        x = _moe_residual(*refs[:4])
        refs = refs[4:]
        refs[-1][...] = x
        refs = refs[:-1]
    else:
        x = refs[0][...]
        refs = refs[1:]
    (mod_ref, g1_ref, w_ref, qag_ref, kvag_ref, wq_ref, wk_ref, wv_ref,
     gq_ref, gk_ref, gkpe_ref, cos_ref, sin_ref, mq_ref, mqt2_ref, icq_ref, mk_ref, mkt2_ref,
     q_out, k_out, v_out, ckv_out, kpe_out, f_out, p_out, g_out) = refs
    shift = mod_ref[0:1, :]
    scale = mod_ref[1:2, :]
    h = _rms(x) * g1_ref[...] * (1.0 + scale) + shift
    hb = h.astype(BF16)

    za = _dot(hb, w_ref[:, 0:COL_F])
    q_lat = za[:, 0:COL_KV]
    kv_lat = za[:, COL_KV:COL_KPE]
    kpe_grp = za[:, COL_KPE:COL_F]
    ckv = _rms(kv_lat) * kvag_ref[...]
    ckv_out[...] = ckv
    kpe_out[...] = kpe_grp[:, 0:ROPE_DIM]

    cosf = cos_ref[...]
    sinf = sin_ref[...]
    lane = lax.broadcasted_iota(jnp.int32, cosf.shape, 1)
    rope = (cosf, sinf, lane < QK_NOPE + ROPE_HALF)

    qn = (_rms(q_lat) * qag_ref[...]).astype(BF16)
    q = _dot(qn, wq_ref[...])
    q = q * _seg_rms_scale(q, mq_ref, mqt2_ref, icq_ref[...]) * gq_ref[...]
    sm_scale = 1.0 / math.sqrt(QK_HEAD)
    for hd in range(N_HEADS):
        sl = slice(hd * HEAD_PAD, (hd + 1) * HEAD_PAD)
        q_out[:, sl] = (_rope_chunk(q[:, sl], *rope) * sm_scale).astype(q_out.dtype)

    kpe_chunk = pltpu.roll(kpe_grp, QK_NOPE, 1)
    _write_keys(ckv, kpe_chunk, wk_ref, wv_ref, gk_ref, gkpe_ref, mk_ref, mkt2_ref, rope, k_out, v_out)

    zb = _dot(hb, w_ref[:, COL_F:COL_G])
    f_out[...] = zb[:, 0:FNET_WIDTH].astype(f_out.dtype)
    p_out[...] = zb[:, FNET_WIDTH:]
    for c0 in range(0, N_BRANCHES * D_MODEL, GATE_COL_CHUNK):
        zc = _dot(hb, w_ref[:, COL_G + c0:COL_G + c0 + GATE_COL_CHUNK])
        g_out[:, c0:c0 + GATE_COL_CHUNK] = jax.nn.sigmoid(zc).astype(g_out.dtype)


def _const_spec(shape):
    nd = len(shape)
    return pl.BlockSpec(shape, lambda i, _n=nd: (0,) * _n, pipeline_mode=pl.Buffered(1))


def _layer_spec(shape, l):
    nd = len(shape)
    return pl.BlockSpec((None,) + tuple(shape), lambda i, _l=l, _n=nd: (_l,) + (0,) * _n,
                        pipeline_mode=pl.Buffered(1))


def _in_projection(x, moe, mods, l, wts, consts, tok):
    after_moe = x is None
    t = moe[0].shape[0] if after_moe else x.shape[0]
    tm = IN_TILE
    n_tiles = t // tm
    p_tiles = tok["t_prompt"] // tm
    s_tiles = tok["dec_seq"] // tm
    rope_blocks = tok["dec_seq"] // tm

    def mod_row(i):
        return jnp.where(i < p_tiles, 0, 1 + (i - p_tiles) // s_tiles)

    def rope_idx(i):
        return (jnp.where(i < p_tiles, rope_blocks, (i - p_tiles) % s_tiles), 0)

    row = lambda w: pl.BlockSpec((tm, w), lambda i: (i, 0))
    if after_moe:
        x1, tw, yg = moe
        lead_specs = [row(ROUTER_PAD), row(D_MODEL),
                      pl.BlockSpec((None, None, 6, D_MODEL), lambda i: (l - 1, mod_row(i), 0, 0)),
                      pl.BlockSpec((TOP_K, tm, D_MODEL), lambda i: (0, i, 0))]
        lead_args = [tw, x1, mods, yg]
    else:
        lead_specs = [row(D_MODEL)]
        lead_args = [x]
    in_specs = lead_specs + [
        pl.BlockSpec((None, None, 6, D_MODEL), lambda i: (l, mod_row(i), 0, 0)),
        _layer_spec((1, D_MODEL), l),
        _layer_spec((D_MODEL, IN_PAD_WIDTH), l),
        _layer_spec((1, Q_RANK), l),
        _layer_spec((1, KV_RANK), l),
        _layer_spec((Q_RANK, QK_PAD), l),
        _layer_spec((KV_RANK, QK_PAD), l),
        _layer_spec((KV_RANK, V_WIDTH), l),
        _layer_spec((1, QK_PAD), l),
        _layer_spec((1, QK_PAD), l),
        _layer_spec((1, HEAD_PAD), l),
        pl.BlockSpec((tm, HEAD_PAD), rope_idx),
        pl.BlockSpec((tm, HEAD_PAD), rope_idx),
        _const_spec((QK_PAD, LANES)),
        _const_spec((2 * LANES, QK_PAD)),
        _const_spec((1, LANES)),
        _const_spec((QK_PAD, LANES)),
        _const_spec((2 * LANES, QK_PAD)),
    ]
    out_shape = [
        jax.ShapeDtypeStruct((t, QK_PAD), BF16),
        jax.ShapeDtypeStruct((t, QK_PAD), BF16),
        jax.ShapeDtypeStruct((t, V_WIDTH), BF16),
        jax.ShapeDtypeStruct((t, KV_RANK), F32),
        jax.ShapeDtypeStruct((t, ROPE_DIM), F32),
        jax.ShapeDtypeStruct((t, FNET_WIDTH), BF16),
        jax.ShapeDtypeStruct((t, POOL_WIDTH), F32),
        jax.ShapeDtypeStruct((t, N_BRANCHES * D_MODEL), BF16),
    ]
    if after_moe:
        out_shape.append(jax.ShapeDtypeStruct((t, D_MODEL), F32))
    out_specs = [row(s.shape[1]) for s in out_shape]
    outs = pl.pallas_call(
        functools.partial(_in_kernel, after_moe=after_moe),
        out_shape=out_shape,
        grid=(n_tiles,),
        in_specs=in_specs,
        out_specs=out_specs,
        compiler_params=_cparams("arbitrary"),
        name="in_projection",
    )(*lead_args, mods, wts["norm1_g"], wts["w_in"], wts["q_a_g"], wts["kv_a_g"], wts["w_q"], wts["w_k"], wts["w_v"],
      wts["g_q"], wts["g_k"], wts["g_kpe"], consts["cosf"], consts["sinf"],
      consts["m_q"], consts["mt2_q"], consts["inv_cnt_q"], consts["m_k"], consts["mt2_k"])
    return (outs[:8], outs[8]) if after_moe else (outs, x)


def _ctx_keys_kernel(ckv_ref, kpe_ref, wk_ref, wv_ref, gk_ref, gkpe_ref, mk_ref, mkt2_ref, k_out, v_out):
    _write_keys(ckv_ref[...], kpe_ref[...], wk_ref, wv_ref, gk_ref, gkpe_ref, mk_ref, mkt2_ref, None, k_out, v_out)


def _ctx_keys(cache_ckv, cache_kpe_pad, wts, consts):
    nb, depth, past, _ = cache_ckv.shape
    lw = lambda shape: pl.BlockSpec((None,) + shape, lambda l, b: (l,) + (0,) * len(shape))
    cs = lambda shape: pl.BlockSpec(shape, lambda l, b: (0,) * len(shape))
    return pl.pallas_call(
        _ctx_keys_kernel,
        out_shape=[jax.ShapeDtypeStruct((depth, nb * past, QK_PAD), BF16),
                   jax.ShapeDtypeStruct((depth, nb * past, V_WIDTH), BF16)],
        grid=(depth, nb),
        in_specs=[
            pl.BlockSpec((None, None, past, KV_RANK), lambda l, b: (b, l, 0, 0)),
            pl.BlockSpec((None, None, past, HEAD_PAD), lambda l, b: (b, l, 0, 0)),
            lw((KV_RANK, QK_PAD)), lw((KV_RANK, V_WIDTH)), lw((1, QK_PAD)), lw((1, HEAD_PAD)),
            cs((QK_PAD, LANES)), cs((2 * LANES, QK_PAD)),
        ],
        out_specs=[pl.BlockSpec((None, past, QK_PAD), lambda l, b: (l, b, 0)),
                   pl.BlockSpec((None, past, V_WIDTH), lambda l, b: (l, b, 0))],
        compiler_params=_cparams("arbitrary", "arbitrary"),
        name="context_keys",
    )(cache_ckv, cache_kpe_pad, wts["w_k"], wts["w_v"], wts["g_k"], wts["g_kpe"], consts["m_k"], consts["mt2_k"])


def _attn_kernel(*refs, n_parts):
    q_ref = refs[0]
    k_refs = refs[1:1 + 2 * n_parts:2]
    v_refs = refs[2:2 + 2 * n_parts:2]
    o_ref = refs[1 + 2 * n_parts]
    lane = lax.broadcasted_iota(jnp.int32, (q_ref.shape[0], 2 * V_HEAD), 1)
    for pair in range(N_HEADS // 2):
        vsl = slice(pair * 2 * V_HEAD, (pair + 1) * 2 * V_HEAD)
        outs = []
        for hd in (2 * pair, 2 * pair + 1):
            sl = slice(hd * HEAD_PAD, (hd + 1) * HEAD_PAD)
            qh = q_ref[:, sl]
            ss = [lax.dot_general(qh, k[:, sl], (((1,), (1,)), ((), ())), preferred_element_type=F32)
                  for k in k_refs]
            m = functools.reduce(jnp.maximum, [jnp.max(s, axis=-1, keepdims=True) for s in ss])
            es = [jnp.exp(s - m) for s in ss]
            den = functools.reduce(jnp.add, [jnp.sum(e, axis=-1, keepdims=True) for e in es])
            acc = functools.reduce(jnp.add, [_dot(e.astype(BF16), v[:, vsl]) for e, v in zip(es, v_refs)])
            outs.append(acc / den)
        o_ref[:, vsl] = jnp.where(lane < V_HEAD, outs[0], outs[1]).astype(o_ref.dtype)


def _attention(q, kv_parts, n_seq, seq_len):
    tq = min(ATTN_Q_TILE, seq_len)
    nq = seq_len // tq
    in_specs = [pl.BlockSpec((tq, QK_PAD), lambda b, i: (b * nq + i, 0))]
    args = [q]
    for k, v, layer, rows in kv_parts:
        for arr, width in ((k, QK_PAD), (v, V_WIDTH)):
            if layer is None:
                in_specs.append(pl.BlockSpec((rows, width), lambda b, i: (b, 0)))
            else:
                in_specs.append(pl.BlockSpec((None, rows, width), lambda b, i, _l=layer: (_l, b, 0)))
            args.append(arr)
    return pl.pallas_call(
        functools.partial(_attn_kernel, n_parts=len(kv_parts)),
        out_shape=jax.ShapeDtypeStruct((n_seq * seq_len, V_WIDTH), BF16),
        grid=(n_seq, nq),
        in_specs=in_specs,
        out_specs=pl.BlockSpec((tq, V_WIDTH), lambda b, i: (b * nq + i, 0)),
        compiler_params=_cparams("arbitrary", "arbitrary"),
        name="attention",
    )(*args)


def _fnet_kernel(f_ref, cs_ref, cl_ref, sl_ref, o_ref, top_ref, bot_ref, *, norm):
    @pl.when(pl.program_id(1) == 0)
    def _():
        for g in range(FNET_GROUPS):
            sl = slice(g * FNET_GC, (g + 1) * FNET_GC)
            a = _dot(f_ref[:, sl], cs_ref[...])
            top_ref[:, sl] = a[:, :FNET_GC].astype(BF16)
            bot_ref[:, sl] = a[:, FNET_GC:].astype(BF16)

    y = _dot(cl_ref[...], top_ref[...]) - _dot(sl_ref[...], bot_ref[...])
    o_ref[...] = (y * norm).astype(o_ref.dtype)


def _fourier(f_in, n_seq, seq_len, row0, tabs):
    tr = min(FNET_ROW_TILE, seq_len)
    nj = seq_len // tr
    sb0 = row0 // seq_len
    return pl.pallas_call(
        functools.partial(_fnet_kernel, norm=1.0 / math.sqrt(seq_len * FNET_GC)),
        out_shape=jax.ShapeDtypeStruct((n_seq * seq_len, FNET_WIDTH), BF16),
        grid=(n_seq, nj),
        in_specs=[
            pl.BlockSpec((seq_len, FNET_WIDTH), lambda b, j: (sb0 + b, 0)),
            pl.BlockSpec((FNET_GC, 2 * FNET_GC), lambda b, j: (0, 0)),
            pl.BlockSpec((tr, seq_len), lambda b, j: (j, 0)),
            pl.BlockSpec((tr, seq_len), lambda b, j: (j, 0)),
        ],
        out_specs=pl.BlockSpec((tr, FNET_WIDTH), lambda b, j: (b * nj + j, 0)),
        scratch_shapes=[pltpu.VMEM((seq_len, FNET_WIDTH), BF16), pltpu.VMEM((seq_len, FNET_WIDTH), BF16)],
        compiler_params=_cparams("arbitrary", "arbitrary"),
        name="fourier_mix",
    )(f_in, tabs["chan"], tabs["cos"], tabs["sin"])


def _pool_kernel(p_ref, wg_ref, ps_ref, o_ref, pad_ref):
    seq_len = p_ref.shape[0]
    zeros = jnp.zeros((POOL_HALO, POOL_WIDTH), F32)
    pad_ref[0:POOL_HALO, :] = zeros
    pad_ref[POOL_HALO + seq_len:, :] = zeros
    pad_ref[POOL_HALO:POOL_HALO + seq_len, :] = p_ref[...]
    ch = min(POOL_CHUNK, seq_len)
    for c in range(seq_len // ch):
        t = lax.broadcasted_iota(jnp.int32, (ch, 1), 0) + c * ch
        for g, w in enumerate(POOL_WINDOWS):
            half = w // 2
            sl = slice(g * POOL_GC, (g + 1) * POOL_GC)
            acc = None
            for j in range(-half, half):
                r0 = POOL_HALO + c * ch + j
                part = pad_ref[r0:r0 + ch, sl]
                acc = part if acc is None else acc + part
            cnt = (jnp.minimum(t + half, seq_len) - jnp.maximum(t - half, 0)).astype(F32)
            pooled = acc / cnt - p_ref[c * ch:(c + 1) * ch, sl]
            mixed = _dot(pooled.astype(BF16), wg_ref[g]) * ps_ref[:, sl]
            o_ref[c * ch:(c + 1) * ch, sl] = mixed.astype(o_ref.dtype)


def _pooling(p_in, n_seq, seq_len, row0, l, wts):
    sb0 = row0 // seq_len
    g = len(POOL_WINDOWS)
    return pl.pallas_call(
        _pool_kernel,
        out_shape=jax.ShapeDtypeStruct((n_seq * seq_len, POOL_WIDTH), BF16),
        grid=(n_seq,),
        in_specs=[
            pl.BlockSpec((seq_len, POOL_WIDTH), lambda b: (sb0 + b, 0)),
            pl.BlockSpec((None, g, POOL_GC, POOL_GC), lambda b: (l, 0, 0, 0)),
            pl.BlockSpec((None, 1, POOL_WIDTH), lambda b: (l, 0, 0)),
        ],
        out_specs=pl.BlockSpec((seq_len, POOL_WIDTH), lambda b: (b, 0)),
        scratch_shapes=[pltpu.VMEM((seq_len + 2 * POOL_HALO, POOL_WIDTH), F32)],
        compiler_params=_cparams("arbitrary"),
        name="pool_mix",
    )(p_in, wts["w_pool_grp"], wts["pool_scale"])


def _merge_kernel(x_ref, mod_ref, a_ref, f_ref, p_ref, g_ref,
                  wa_ref, wf_ref, wp_ref, wo_ref, g2_ref, wr_ref, br_ref, tri_ref,
                  x1_out, hp_out, te_out, tw_out, rk_out, cnt_out, carry_ref):
    a_in = a_ref[...]
    f_in = f_ref[...]
    p_in = p_ref[...]
    chunks = []
    for c0 in range(0, D_MODEL, MERGE_COL_CHUNK):
        cs = slice(c0, c0 + MERGE_COL_CHUNK)
        a = _dot(a_in, wa_ref[:, cs])
        f = _dot(f_in, wf_ref[:, cs])
        p = _dot(p_in, wp_ref[:, cs])
        chunks.append((g_ref[:, c0:c0 + MERGE_COL_CHUNK].astype(F32) * a
                       + g_ref[:, D_MODEL + c0:D_MODEL + c0 + MERGE_COL_CHUNK].astype(F32) * f
                       + g_ref[:, 2 * D_MODEL + c0:2 * D_MODEL + c0 + MERGE_COL_CHUNK].astype(F32) * p).astype(BF16))
    mix = _dot(jnp.concatenate(chunks, axis=1), wo_ref[...])
    gate1 = mod_ref[2:3, :]
    shift2 = mod_ref[3:4, :]
    scale2 = mod_ref[4:5, :]
    x1 = x_ref[...] + gate1 * mix
    x1_out[...] = x1
    h2 = _rms(x1) * g2_ref[...] * (1.0 + scale2) + shift2

    h_hi = h2.astype(BF16)
    bits = pltpu.bitcast(h_hi.astype(F32), jnp.int32)
    hp_out[...] = bits[:, :D_MODEL // 2] | lax.shift_right_logical(bits[:, D_MODEL // 2:], 16)

    h_lo = (h2 - h_hi.astype(F32)).astype(BF16)
    lhs = jnp.concatenate([h_hi, h_lo, h_hi], axis=1)
    logits = _dot(lhs, wr_ref[...]) + br_ref[...]

    lane = lax.broadcasted_iota(jnp.int32, logits.shape, 1)
    work = logits
    vals, idxs = [], []
    for _ in range(TOP_K):
        m = jnp.max(work, axis=-1, keepdims=True)
        idx = jnp.min(jnp.where(work == m, lane, ROUTER_PAD), axis=-1, keepdims=True)
        vals.append(m)
        idxs.append(idx)
        work = jnp.where(lane == idx, -jnp.inf, work)
    es = [jnp.exp(v - vals[0]) for v in vals]
    den = functools.reduce(jnp.add, es)

    @pl.when(pl.program_id(0) == 0)
    def _():
        carry_ref[...] = jnp.zeros(carry_ref.shape, F32)

    chosen = functools.reduce(jnp.logical_or, [lane == idx for idx in idxs])
    hot = jnp.where(chosen, 1.0, 0.0)
    before = _dot(tri_ref[...], hot.astype(BF16)) + carry_ref[...]
    carry_ref[...] = carry_ref[...] + jnp.sum(hot, axis=0, keepdims=True)
    cnt_out[...] = carry_ref[...].astype(jnp.int32)

    te = jnp.zeros(logits.shape, jnp.int32)
    tw = jnp.zeros(logits.shape, F32)
    rk = jnp.zeros(logits.shape, jnp.int32)
    for k in range(TOP_K):
        rank_k = jnp.sum(jnp.where(lane == idxs[k], before, 0.0), axis=-1, keepdims=True).astype(jnp.int32)
        te = jnp.where(lane == k, idxs[k], te)
        tw = jnp.where(lane == k, es[k] / den, tw)
        rk = jnp.where(lane == k, rank_k, rk)
    te_out[...] = te
    tw_out[...] = tw
    rk_out[...] = rk


def _merge(x, mods, attn, fnet, pool, gates, l, wts, consts, tok):
    t = x.shape[0]
    tm = TOKEN_TILE
    p_tiles = tok["t_prompt"] // tm
    s_tiles = tok["dec_seq"] // tm

    def mod_idx(i):
        row = jnp.where(i < p_tiles, 0, 1 + (i - p_tiles) // s_tiles)
        return (l, row, 0, 0)

    row = lambda w: pl.BlockSpec((tm, w), lambda i: (i, 0))
    out_shape = [
        jax.ShapeDtypeStruct((t, D_MODEL), F32),
        jax.ShapeDtypeStruct((t, D_MODEL // 2), jnp.int32),
        jax.ShapeDtypeStruct((t, ROUTER_PAD), jnp.int32),
        jax.ShapeDtypeStruct((t, ROUTER_PAD), F32),
        jax.ShapeDtypeStruct((t, ROUTER_PAD), jnp.int32),
    ]
    return pl.pallas_call(
        _merge_kernel,
        out_shape=out_shape + [jax.ShapeDtypeStruct((1, ROUTER_PAD), jnp.int32)],
        grid=(t // tm,),
        in_specs=[
            row(D_MODEL),
            pl.BlockSpec((None, None, 6, D_MODEL), mod_idx),
            row(V_WIDTH), row(FNET_WIDTH), row(POOL_WIDTH), row(N_BRANCHES * D_MODEL),
            _layer_spec((V_WIDTH, D_MODEL), l),
            _layer_spec((FNET_WIDTH, D_MODEL), l),
            _layer_spec((POOL_WIDTH, D_MODEL), l),
            _layer_spec((D_MODEL, D_MODEL), l),
            _layer_spec((1, D_MODEL), l),
            _layer_spec((3 * D_MODEL, ROUTER_PAD), l),
            _layer_spec((1, ROUTER_PAD), l),
            _const_spec((tm, tm)),
        ],
        out_specs=[row(s.shape[1]) for s in out_shape] + [pl.BlockSpec((1, ROUTER_PAD), lambda i: (0, 0))],
        scratch_shapes=[pltpu.VMEM((1, ROUTER_PAD), F32)],
        compiler_params=_cparams("arbitrary"),
        name="merge_router",
    )(x, mods, attn, fnet, pool, gates, wts["w_attn_o"], wts["w_fnet_o"], wts["w_pool_o"], wts["w_out"],
      wts["norm2_g"], wts["w_router3"], wts["b_router"], consts["tri"])


def _sc_row_gather(table, idx):
    n = idx.shape[0]
    d = table.shape[1]
    info = plsc.get_sparse_core_info()
    n_cores = info.num_cores
    n_workers = n_cores * info.num_subcores
    per_worker = n // n_workers
    n_chunks = per_worker // SC_GATHER_ROWS
    assert n_chunks * SC_GATHER_ROWS * n_workers == n
    mesh = plsc.VectorSubcoreMesh(core_axis_name="core", subcore_axis_name="subcore")

    @functools.partial(
        pl.kernel, mesh=mesh, out_type=jax.ShapeDtypeStruct((n, d), table.dtype),
        scratch_types=[pltpu.VMEM((SC_GATHER_ROWS,), jnp.int32), pltpu.VMEM((SC_GATHER_ROWS, d), table.dtype),
                       pltpu.SemaphoreType.DMA],
        name="sc_row_gather")
    def gather(table_hbm, idx_hbm, out_hbm, idx_v, rows_v, sem):
        worker = lax.axis_index("subcore") * n_cores + lax.axis_index("core")

        @pl.loop(0, n_chunks)
        def _(c):
            base = worker * per_worker + c * SC_GATHER_ROWS
            pltpu.sync_copy(idx_hbm.at[pl.ds(base, SC_GATHER_ROWS)], idx_v)
            pltpu.async_copy(table_hbm.at[idx_v], rows_v, sem).wait()
            pltpu.sync_copy(rows_v, out_hbm.at[pl.ds(base, SC_GATHER_ROWS)])

    return gather(table, idx)


def _expert_kernel(be_ref, nv_ref, nb_ref, xs_ref, wgu_ref, bgu_ref, wd_ref, bd_ref, y_ref, wgu_bf, wd_bf):
    b = pl.program_id(0)

    def ffn(rows):
        w = xs_ref[0:rows, :]
        x = jnp.concatenate([pltpu.bitcast(w & jnp.int32(-65536), F32),
                             pltpu.bitcast(lax.shift_left(w, 16), F32)], axis=1).astype(BF16)
        gu = _dot(x, wgu_bf[...]) + bgu_ref[...]
        glu = jnp.minimum(gu[:, :D_FF], SWIGLU_LIMIT)
        lin = jnp.clip(gu[:, D_FF:], -SWIGLU_LIMIT, SWIGLU_LIMIT)
        act = glu * jax.nn.sigmoid(SWIGLU_ALPHA * glu) * (lin + 1.0)
        y_ref[0:rows, :] = _dot(act.astype(BF16), wd_bf[...]) + bd_ref[...]
        if rows < MOE_BLOCK:
            y_ref[rows:, :] = jnp.zeros((MOE_BLOCK - rows, D_MODEL), y_ref.dtype)

    @pl.when(b < nb_ref[0])
    def _():
        prev = be_ref[jnp.maximum(b - 1, 0)]

        @pl.when((b == 0) | (be_ref[b] != prev))
        def _():
            wgu_bf[...] = wgu_ref[...].astype(BF16)
            wd_bf[...] = wd_ref[...].astype(BF16)

        steps = (nv_ref[b] + (MOE_STEP - 1)) // MOE_STEP
        for rows in range(MOE_STEP, MOE_BLOCK + 1, MOE_STEP):
            want = steps <= 1 if rows == MOE_STEP else steps == rows // MOE_STEP
            pl.when(want)(functools.partial(ffn, rows))

    @pl.when(b >= nb_ref[0])
    def _():
        y_ref[...] = jnp.zeros(y_ref.shape, y_ref.dtype)


def _experts(xs, block_exp, block_valid, n_used, l, w_gu, b_gu4, w_down, b_down4):
    n_slots = xs.shape[0]
    bm = MOE_BLOCK
    n_blocks = n_slots // bm

    def blk(b, nb):
        return jnp.minimum(b, nb[0] - 1)

    def wspec(shape):
        return pl.BlockSpec((None, None) + shape, lambda b, be, nv, nb: (l, be[blk(b, nb)], 0, 0))

    grid_spec = pltpu.PrefetchScalarGridSpec(
        num_scalar_prefetch=3,
        grid=(n_blocks,),
        in_specs=[
            pl.BlockSpec((bm, D_MODEL // 2), lambda b, be, nv, nb: (blk(b, nb), 0)),
            wspec((D_MODEL, 2 * D_FF)), wspec((1, 2 * D_FF)), wspec((D_FF, D_MODEL)), wspec((1, D_MODEL)),
        ],
        out_specs=pl.BlockSpec((bm, D_MODEL), lambda b, be, nv, nb: (b, 0)),
        scratch_shapes=[pltpu.VMEM((D_MODEL, 2 * D_FF), BF16), pltpu.VMEM((D_FF, D_MODEL), BF16)],
    )
    return pl.pallas_call(
        _expert_kernel,
        out_shape=jax.ShapeDtypeStruct((n_slots, D_MODEL), F32),
        grid_spec=grid_spec,
        compiler_params=_cparams("arbitrary"),
        name="moe_experts",
    )(block_exp, block_valid, n_used, xs, w_gu, b_gu4, w_down, b_down4)


def _combine_kernel(tw_ref, x_ref, mod_ref, yg_ref, o_ref):
    o_ref[...] = _moe_residual(tw_ref, x_ref, mod_ref, yg_ref)


def _combine(yg, tw, x1, mods, l, tok):
    t = x1.shape[0]
    tc = COMBINE_TILE
    p_tiles = tok["t_prompt"] // tc
    s_tiles = tok["dec_seq"] // tc

    def mod_idx(i):
        row = jnp.where(i < p_tiles, 0, 1 + (i - p_tiles) // s_tiles)
        return (l, row, 0, 0)

    return pl.pallas_call(
        _combine_kernel,
        out_shape=jax.ShapeDtypeStruct((t, D_MODEL), F32),
        grid=(t // tc,),
        in_specs=[
            pl.BlockSpec((tc, ROUTER_PAD), lambda i: (i, 0)),
            pl.BlockSpec((tc, D_MODEL), lambda i: (i, 0)),
            pl.BlockSpec((None, None, 6, D_MODEL), mod_idx),
            pl.BlockSpec((TOP_K, tc, D_MODEL), lambda i: (0, i, 0)),
        ],
        out_specs=pl.BlockSpec((tc, D_MODEL), lambda i: (i, 0)),
        compiler_params=_cparams("arbitrary"),
        name="moe_combine",
    )(tw, x1, mods, yg)


def _slot_tokens(slot, counts, padded_start, padded_end, n_slots, t):
    n = slot.shape[0]
    j = jnp.arange(MOE_BLOCK, dtype=jnp.int32)
    pad_key = (padded_start + counts)[:, None] + j[None, :]
    pad_key = jnp.where(pad_key < padded_end[:, None], pad_key, n_slots).reshape(N_EXPERTS * MOE_BLOCK)
    pad_val = jnp.arange(N_EXPERTS * MOE_BLOCK, dtype=jnp.int32) % t
    keys = jnp.concatenate([slot, pad_key])
    vals = jnp.concatenate([jnp.arange(n, dtype=jnp.int32) // TOP_K, pad_val])
    tok_bits = max(1, (t - 1).bit_length())
    assert (n_slots + 1) << tok_bits < 2 ** 31
    packed = jnp.sort(lax.shift_left(keys, tok_bits) | vals)
    return packed[:n_slots] & ((1 << tok_bits) - 1)


def _route(te_pad, rank_pad, counts_pad, n_slots):
    t = te_pad.shape[0]
    flat_e = te_pad[:, :TOP_K].reshape(t * TOP_K)
    onehot = (flat_e[:, None] == jnp.arange(N_EXPERTS, dtype=jnp.int32)[None, :]).astype(jnp.int32)
    rank = rank_pad[:, :TOP_K].reshape(t * TOP_K)
    counts = counts_pad[0, :N_EXPERTS]
    padded = (counts + MOE_BLOCK - 1) // MOE_BLOCK * MOE_BLOCK
    padded_end = jnp.cumsum(padded)
    padded_start = padded_end - padded
    slot = jnp.sum(onehot * padded_start[None, :], axis=1) + rank
    n_blocks = n_slots // MOE_BLOCK
    block_row0 = jnp.arange(n_blocks, dtype=jnp.int32) * MOE_BLOCK
    block_exp = jnp.minimum(
        jnp.sum((padded_end[None, :] <= block_row0[:, None]).astype(jnp.int32), axis=1),
        N_EXPERTS - 1).astype(jnp.int32)
    exp_hot = (block_exp[:, None] == jnp.arange(N_EXPERTS, dtype=jnp.int32)[None, :]).astype(jnp.int32)
    rows_end = jnp.sum(exp_hot * (padded_start + counts)[None, :], axis=1)
    block_valid = jnp.clip(rows_end - block_row0, 0, MOE_BLOCK).astype(jnp.int32)
    n_used = (padded_end[-1:] // MOE_BLOCK).astype(jnp.int32)
    slot = slot.astype(jnp.int32)
    slot_tok = _slot_tokens(slot, counts, padded_start, padded_end, n_slots, t)
    slot_by_choice = slot.reshape(t, TOP_K).T.reshape(TOP_K * t)
    return slot_by_choice, slot_tok, block_exp, block_valid, n_used


def _segment_matrices():
    m_q = np.zeros((QK_PAD, LANES), np.float32)
    m_k = np.zeros((QK_PAD, LANES), np.float32)
    inv_cnt_q = np.ones((1, LANES), np.float32)
    for h in range(N_HEADS):
        m_q[h * HEAD_PAD:h * HEAD_PAD + QK_NOPE, 2 * h] = 1.0
        m_q[h * HEAD_PAD + QK_NOPE:h * HEAD_PAD + QK_HEAD, 2 * h + 1] = 1.0
        inv_cnt_q[0, 2 * h] = 1.0 / QK_NOPE
        inv_cnt_q[0, 2 * h + 1] = 1.0 / ROPE_DIM
        m_k[h * HEAD_PAD:h * HEAD_PAD + QK_NOPE, h] = 1.0
    dup = lambda m: np.concatenate([m.T, m.T], axis=0)
    return dict(m_q=jnp.asarray(m_q, BF16), mt2_q=jnp.asarray(dup(m_q), BF16), inv_cnt_q=jnp.asarray(inv_cnt_q),
                m_k=jnp.asarray(m_k, BF16), mt2_k=jnp.asarray(dup(m_k), BF16))


def _rope_lane_tables(n_tokens, ident_rows):
    rows = n_tokens // GRID_W
    row = jnp.broadcast_to(jnp.arange(rows)[:, None], (rows, GRID_W)).reshape(n_tokens)
    col = jnp.broadcast_to(jnp.arange(GRID_W)[None, :], (rows, GRID_W)).reshape(n_tokens)
    n_freq = ROPE_DIM // 4
    inv = 1.0 / (ROPE_THETA ** (jnp.arange(n_freq, dtype=F32) / n_freq))
    ang = jnp.concatenate([row[:, None].astype(F32) * inv, col[:, None].astype(F32) * inv], axis=-1)
    cos, sin = jnp.cos(ang), jnp.sin(ang)
    ones = jnp.ones((n_tokens, QK_NOPE), F32)
    tail = HEAD_PAD - QK_HEAD
    cosf = jnp.concatenate([ones, cos, cos, jnp.ones((n_tokens, tail), F32)], axis=1)
    sinf = jnp.concatenate([0.0 * ones, -sin, sin, jnp.zeros((n_tokens, tail), F32)], axis=1)
    cosf = jnp.concatenate([cosf, jnp.ones((ident_rows, HEAD_PAD), F32)], axis=0)
    sinf = jnp.concatenate([sinf, jnp.zeros((ident_rows, HEAD_PAD), F32)], axis=0)
    return cosf, sinf


def _dft_tables(seq_len):
    def cs(n):
        k = np.arange(n, dtype=np.int64)
        ang = 2.0 * np.pi * ((k[:, None] * k[None, :]) % n).astype(np.float64) / n
        return np.cos(ang).astype(np.float32), np.sin(ang).astype(np.float32)

    cl, sl = cs(seq_len)
    cc, sc = cs(FNET_GC)
    return dict(cos=jnp.asarray(cl).astype(BF16), sin=jnp.asarray(sl).astype(BF16),
                chan=jnp.asarray(np.concatenate([cc, sc], axis=1)).astype(BF16))


def _layout_weights(w_in, w_q_b, w_kv_b, q_nope_g, q_rope_g, k_nope_g, k_rope_g, w_router, b_router):
    depth = w_in.shape[0]
    w_in_p = jnp.concatenate(
        [w_in[:, :, :COL_KPE + ROPE_DIM], jnp.zeros((depth, D_MODEL, KPE_PAD - ROPE_DIM), w_in.dtype),
         w_in[:, :, COL_KPE + ROPE_DIM:]], axis=2).astype(BF16)
    w_q = jnp.pad(w_q_b.reshape(depth, Q_RANK, N_HEADS, QK_HEAD),
                  ((0, 0), (0, 0), (0, 0), (0, HEAD_PAD - QK_HEAD))).reshape(depth, Q_RANK, QK_PAD).astype(BF16)
    kv = w_kv_b.reshape(depth, KV_RANK, N_HEADS, QK_NOPE + V_HEAD)
    w_k = jnp.pad(kv[..., :QK_NOPE], ((0, 0), (0, 0), (0, 0), (0, HEAD_PAD - QK_NOPE))
                  ).reshape(depth, KV_RANK, QK_PAD).astype(BF16)
    w_v = kv[..., QK_NOPE:].reshape(depth, KV_RANK, V_WIDTH).astype(BF16)
    zq = jnp.zeros((depth, HEAD_PAD - QK_HEAD), F32)
    g_q = jnp.tile(jnp.concatenate([q_nope_g, q_rope_g, zq], axis=1), (1, N_HEADS))[:, None, :]
    g_k = jnp.tile(jnp.concatenate([k_nope_g, jnp.zeros((depth, HEAD_PAD - QK_NOPE), F32)], axis=1),
                   (1, N_HEADS))[:, None, :]
    g_kpe = jnp.concatenate([jnp.zeros((depth, QK_NOPE), F32), k_rope_g, zq], axis=1)[:, None, :]
    w_hi = w_router.astype(BF16)
    w_lo = (w_router - w_hi.astype(F32)).astype(BF16)
    w_r3 = jnp.pad(jnp.concatenate([w_hi, w_hi, w_lo], axis=1), ((0, 0), (0, 0), (0, ROUTER_PAD - N_EXPERTS)))
    b_r = jnp.pad(b_router, ((0, 0), (0, ROUTER_PAD - N_EXPERTS)), constant_values=NEG_BIG)[:, None, :]
    return dict(w_in=w_in_p, w_q=w_q, w_k=w_k, w_v=w_v, g_q=g_q, g_k=g_k, g_kpe=g_kpe, w_router3=w_r3, b_router=b_r)


def kernel(x_prompt, x_sample, cache_ckv, cache_kpe, c, c_ctx, w_mod, b_mod, norm1_g, norm2_g, w_in, q_a_g, kv_a_g, w_q_b, w_kv_b, q_nope_g, q_rope_g, k_nope_g, k_rope_g, w_attn_o, w_fnet_o, w_pool_grp, pool_scale, w_pool_o, w_out, w_router, b_router, w_gu, b_gu, w_down, b_down):
    batch, seq, d = x_prompt.shape
    dec_batch, dec_seq, _ = x_sample.shape
    depth = w_mod.shape[0]
    past = cache_ckv.shape[2]
    t_prompt = batch * seq
    t_sample = dec_batch * dec_seq
    assert d == D_MODEL and dec_batch + 1 <= 8
    assert t_prompt % TOKEN_TILE == 0 and dec_seq % TOKEN_TILE == 0 and seq % COMBINE_TILE == 0
    assert dec_seq % GRID_W == 0

    consts = _segment_matrices()
    consts["cosf"], consts["sinf"] = _rope_lane_tables(dec_seq, IN_TILE)
    consts["tri"] = jnp.asarray(np.tril(np.ones((TOKEN_TILE, TOKEN_TILE), np.float32), -1), BF16)

    wts = _layout_weights(w_in, w_q_b, w_kv_b, q_nope_g, q_rope_g, k_nope_g, k_rope_g, w_router, b_router)
    row3 = lambda a: a[:, None, :]
    wts.update(norm1_g=row3(norm1_g), norm2_g=row3(norm2_g), q_a_g=row3(q_a_g), kv_a_g=row3(kv_a_g),
               w_attn_o=w_attn_o.astype(BF16), w_fnet_o=w_fnet_o.astype(BF16), w_pool_o=w_pool_o.astype(BF16),
               w_out=w_out.astype(BF16), w_pool_grp=w_pool_grp.astype(BF16), pool_scale=row3(pool_scale))
    b_gu4 = b_gu[:, :, None, :]
    b_down4 = b_down[:, :, None, :]

    cond8 = jnp.concatenate([c_ctx[None, :], c, jnp.zeros((8 - 1 - dec_batch, d), F32)], axis=0)
    mods = _modulation(cond8, w_mod, b_mod).reshape(depth, 8, 6, D_MODEL)

    kpe_pad = jnp.pad(cache_kpe, ((0, 0), (0, 0), (0, 0), (QK_NOPE, HEAD_PAD - QK_HEAD)))
    k_ctx, v_ctx = _ctx_keys(cache_ckv, kpe_pad, wts, consts)

    streams = [
        dict(x=x_prompt.reshape(t_prompt, d), n_seq=batch, seq_len=seq, tok=dict(t_prompt=t_prompt, dec_seq=dec_seq),
             dft=_dft_tables(seq), ctx=None, moe=None),
        dict(x=x_sample.reshape(t_sample, d), n_seq=dec_batch, seq_len=dec_seq, tok=dict(t_prompt=0, dec_seq=dec_seq),
             dft=_dft_tables(dec_seq), ctx=(k_ctx, v_ctx, past), moe=None),
    ]
    ckv_list, kpe_list = [], []
    for l in range(depth):
        for st in streams:
            (q, k, v, ckv, kpe, f_in, p_in, st["gates"]), st["x"] = _in_projection(
                st["x"], st["moe"], mods, l, wts, consts, st["tok"])
            if st["ctx"] is None:
                ckv_list.append(ckv.reshape(batch, seq, KV_RANK))
                kpe_list.append(kpe.reshape(batch, seq, ROPE_DIM))
            parts = [(k, v, None, st["seq_len"])]
            if st["ctx"] is not None:
                parts.append((st["ctx"][0], st["ctx"][1], l, st["ctx"][2]))
            st["attn"] = _attention(q, parts, st["n_seq"], st["seq_len"])
            st["fnet"] = _fourier(f_in, st["n_seq"], st["seq_len"], 0, st["dft"])
            st["pool"] = _pooling(p_in, st["n_seq"], st["seq_len"], 0, l, wts)
        for st in streams:
            t = st["n_seq"] * st["seq_len"]
            n_slots = (t * TOP_K + N_EXPERTS * (MOE_BLOCK - 1) + MOE_BLOCK - 1) // MOE_BLOCK * MOE_BLOCK
            x1, h2p, te_pad, tw_pad, rank_pad, counts_pad = _merge(
                st["x"], mods, st["attn"], st["fnet"], st["pool"], st["gates"], l, wts, consts, st["tok"])
            slot_by_choice, slot_tok, block_exp, block_valid, n_used = _route(te_pad, rank_pad, counts_pad, n_slots)
            st["routed"] = (x1, tw_pad, slot_by_choice, block_exp, block_valid, n_used)
            st["xs"] = _sc_row_gather(h2p, slot_tok)
        for st in streams:
            t = st["n_seq"] * st["seq_len"]
            x1, tw_pad, slot_by_choice, block_exp, block_valid, n_used = st["routed"]
            y = _experts(st["xs"], block_exp, block_valid, n_used, l, w_gu, b_gu4, w_down, b_down4)
            yg = _sc_row_gather(y, slot_by_choice).reshape(TOP_K, t, D_MODEL)
            st["moe"] = (x1, tw_pad, yg)
            st["x"] = None
    outs = []
    for st in streams:
        x1, tw_pad, yg = st["moe"]
        outs.append(_combine(yg, tw_pad, x1, mods, depth - 1, st["tok"]))
    y_prompt = outs[0].reshape(batch, seq, d)
    y_sample = outs[1].reshape(dec_batch, dec_seq, d)
    return (y_prompt, y_sample, jnp.stack(ckv_list, axis=1), jnp.stack(kpe_list, axis=1))
```

```python
import functools
import math

import numpy as np
import jax
import jax.numpy as jnp
from jax import lax
from jax.experimental import pallas as pl
from jax.experimental.pallas import tpu as pltpu
from jax.experimental.pallas import tpu_sc as plsc

D_MODEL = 1024
GRID_W = 64
N_HEADS = 8
QK_NOPE = 64
ROPE_DIM = 32
V_HEAD = 64
QK_HEAD = QK_NOPE + ROPE_DIM
Q_RANK = 384
KV_RANK = 256
ROPE_THETA = 10000.0
FNET_GROUPS = 4
FNET_GC = 128
FNET_WIDTH = FNET_GROUPS * FNET_GC
POOL_WINDOWS = (2, 4, 8, 16)
POOL_GC = 128
POOL_WIDTH = len(POOL_WINDOWS) * POOL_GC
N_BRANCHES = 3
N_EXPERTS = 32
TOP_K = 4
D_FF = D_MODEL
SWIGLU_LIMIT = 7.0
SWIGLU_ALPHA = 1.702
RMS_EPS = 1e-6

LANES = 128
HEAD_PAD = LANES
QK_PAD = N_HEADS * HEAD_PAD
V_WIDTH = N_HEADS * V_HEAD
ROPE_HALF = ROPE_DIM // 2
KPE_PAD = LANES
COL_KV = Q_RANK
COL_KPE = Q_RANK + KV_RANK
COL_F = COL_KPE + KPE_PAD
COL_P = COL_F + FNET_WIDTH
COL_G = COL_P + POOL_WIDTH
IN_PAD_WIDTH = COL_G + N_BRANCHES * D_MODEL
ROUTER_PAD = LANES
NEG_BIG = -1e30

TOKEN_TILE = 512
IN_TILE = 256
ATTN_Q_TILE = 512
FNET_ROW_TILE = 512
POOL_CHUNK = 256
POOL_HALO = 16
MOE_BLOCK = 512
MOE_STEP = 128
GATE_COL_CHUNK = 512
MERGE_COL_CHUNK = 256
FF_CHUNK = 256
COMBINE_TILE = 256
SC_GATHER_ROWS = 64
MOD_COL_TILE = 1536
VMEM_LIMIT = 56 * 1024 * 1024

F32 = jnp.float32
BF16 = jnp.bfloat16


def _cparams(*sem):
    return pltpu.CompilerParams(dimension_semantics=sem, vmem_limit_bytes=VMEM_LIMIT)


def _dot(a, b):
    return jnp.dot(a, b, preferred_element_type=F32)


def _rms(x):
    return x * lax.rsqrt(jnp.mean(x * x, axis=-1, keepdims=True) + RMS_EPS)


def _seg_rms_scale(x, m_ref, mt2_ref, inv_cnt):
    ss = _dot((x * x).astype(BF16), m_ref[...])
    r = lax.rsqrt(ss * inv_cnt + RMS_EPS)
    r_hi = r.astype(BF16)
    r_lo = (r - r_hi.astype(F32)).astype(BF16)
    return _dot(jnp.concatenate([r_hi, r_lo], axis=1), mt2_ref[...])


def _rope_chunk(xc, cosf, sinf, first_half):
    partner = jnp.where(first_half, pltpu.roll(xc, HEAD_PAD - ROPE_HALF, 1), pltpu.roll(xc, ROPE_HALF, 1))
    return xc * cosf + partner * sinf


def _write_keys(ckv, kpe_chunk, wk_ref, wv_ref, gk_ref, gkpe_ref, mk_ref, mkt2_ref, rope, k_out, v_out):
    cb = ckv.astype(BF16)
    kn = _dot(cb, wk_ref[...])
    kn = kn * _seg_rms_scale(kn, mk_ref, mkt2_ref, 1.0 / QK_NOPE) * gk_ref[...]
    v_out[...] = _dot(cb, wv_ref[...]).astype(v_out.dtype)
    ms = jnp.sum(kpe_chunk * kpe_chunk, axis=-1, keepdims=True) * (1.0 / ROPE_DIM)
    kp = kpe_chunk * lax.rsqrt(ms + RMS_EPS) * gkpe_ref[...]
    if rope is not None:
        kp = _rope_chunk(kp, *rope)
    for h in range(N_HEADS):
        sl = slice(h * HEAD_PAD, (h + 1) * HEAD_PAD)
        k_out[:, sl] = (kn[:, sl] + kp).astype(k_out.dtype)


def _mod_kernel(c_ref, w_ref, b_ref, o_ref):
    c = c_ref[...]
    s = c * jax.nn.sigmoid(c)
    o_ref[...] = _dot(s.astype(BF16), w_ref[...].astype(BF16)) + b_ref[...]


def _modulation(cond8, w_mod, b_mod):
    depth = w_mod.shape[0]
    n = w_mod.shape[2]
    return pl.pallas_call(
        _mod_kernel,
        out_shape=jax.ShapeDtypeStruct((depth, 8, n), F32),
        grid=(depth, n // MOD_COL_TILE),
        in_specs=[
            pl.BlockSpec((8, D_MODEL), lambda l, j: (0, 0)),
            pl.BlockSpec((None, D_MODEL, MOD_COL_TILE), lambda l, j: (l, 0, j)),
            pl.BlockSpec((None, 1, MOD_COL_TILE), lambda l, j: (l, 0, j)),
        ],
        out_specs=pl.BlockSpec((None, 8, MOD_COL_TILE), lambda l, j: (l, 0, j)),
        compiler_params=_cparams("arbitrary", "arbitrary"),
        name="adaln_modulation",
    )(cond8, w_mod, b_mod.reshape(depth, 1, n))


def _moe_residual(tw_ref, x1_ref, mod_ref, yg_ref):
    acc = tw_ref[:, 0:1] * yg_ref[0]
    for k in range(1, TOP_K):
        acc = acc + tw_ref[:, k:k + 1] * yg_ref[k]
    return x1_ref[...] + mod_ref[5:6, :] * acc


def _in_kernel(*refs, after_moe):
    if after_moe:
        x = _moe_residual(*refs[:4])
        refs = refs[4:]
        refs[-1][...] = x
        refs = refs[:-1]
    else:
        x = refs[0][...]
        refs = refs[1:]
    (mod_ref, g1_ref, w_ref, qag_ref, kvag_ref, wq_ref, wk_ref, wv_ref,
     gq_ref, gk_ref, gkpe_ref, cos_ref, sin_ref, mq_ref, mqt2_ref, icq_ref, mk_ref, mkt2_ref,
     q_out, k_out, v_out, ckv_out, kpe_out, f_out, p_out, g_out) = refs
    shift = mod_ref[0:1, :]
    scale = mod_ref[1:2, :]
    h = _rms(x) * g1_ref[...] * (1.0 + scale) + shift
    hb = h.astype(BF16)

    za = _dot(hb, w_ref[:, 0:COL_F])
    q_lat = za[:, 0:COL_KV]
    kv_lat = za[:, COL_KV:COL_KPE]
    kpe_grp = za[:, COL_KPE:COL_F]
    ckv = _rms(kv_lat) * kvag_ref[...]
    ckv_out[...] = ckv
    kpe_out[...] = kpe_grp[:, 0:ROPE_DIM]

    cosf = cos_ref[...]
    sinf = sin_ref[...]
    lane = lax.broadcasted_iota(jnp.int32, cosf.shape, 1)
    rope = (cosf, sinf, lane < QK_NOPE + ROPE_HALF)

    qn = (_rms(q_lat) * qag_ref[...]).astype(BF16)
    q = _dot(qn, wq_ref[...])
    q = q * _seg_rms_scale(q, mq_ref, mqt2_ref, icq_ref[...]) * gq_ref[...]
    sm_scale = 1.0 / math.sqrt(QK_HEAD)
    for hd in range(N_HEADS):
        sl = slice(hd * HEAD_PAD, (hd + 1) * HEAD_PAD)
        q_out[:, sl] = (_rope_chunk(q[:, sl], *rope) * sm_scale).astype(q_out.dtype)

    kpe_chunk = pltpu.roll(kpe_grp, QK_NOPE, 1)
    _write_keys(ckv, kpe_chunk, wk_ref, wv_ref, gk_ref, gkpe_ref, mk_ref, mkt2_ref, rope, k_out, v_out)

    zb = _dot(hb, w_ref[:, COL_F:COL_G])
    f_out[...] = zb[:, 0:FNET_WIDTH].astype(f_out.dtype)
    p_out[...] = zb[:, FNET_WIDTH:]
    for c0 in range(0, N_BRANCHES * D_MODEL, GATE_COL_CHUNK):
        zc = _dot(hb, w_ref[:, COL_G + c0:COL_G + c0 + GATE_COL_CHUNK])
        g_out[:, c0:c0 + GATE_COL_CHUNK] = jax.nn.sigmoid(zc).astype(g_out.dtype)


def _const_spec(shape):
    nd = len(shape)
    return pl.BlockSpec(shape, lambda i, _n=nd: (0,) * _n, pipeline_mode=pl.Buffered(1))


def _layer_spec(shape, l):
    nd = len(shape)
    return pl.BlockSpec((None,) + tuple(shape), lambda i, _l=l, _n=nd: (_l,) + (0,) * _n,
                        pipeline_mode=pl.Buffered(1))


def _in_projection(x, moe, mods, l, wts, consts, tok):
    after_moe = x is None
    t = moe[0].shape[0] if after_moe else x.shape[0]
    tm = IN_TILE
    n_tiles = t // tm
    p_tiles = tok["t_prompt"] // tm
    s_tiles = tok["dec_seq"] // tm
    rope_blocks = tok["dec_seq"] // tm

    def mod_row(i):
        return jnp.where(i < p_tiles, 0, 1 + (i - p_tiles) // s_tiles)

    def rope_idx(i):
        return (jnp.where(i < p_tiles, rope_blocks, (i - p_tiles) % s_tiles), 0)

    row = lambda w: pl.BlockSpec((tm, w), lambda i: (i, 0))
    if after_moe:
        x1, tw, yg = moe
        lead_specs = [row(ROUTER_PAD), row(D_MODEL),
                      pl.BlockSpec((None, None, 6, D_MODEL), lambda i: (l - 1, mod_row(i), 0, 0)),
                      pl.BlockSpec((TOP_K, tm, D_MODEL), lambda i: (0, i, 0))]
        lead_args = [tw, x1, mods, yg]
    else:
        lead_specs = [row(D_MODEL)]
        lead_args = [x]
    in_specs = lead_specs + [
        pl.BlockSpec((None, None, 6, D_MODEL), lambda i: (l, mod_row(i), 0, 0)),
        _layer_spec((1, D_MODEL), l),
        _layer_spec((D_MODEL, IN_PAD_WIDTH), l),
        _layer_spec((1, Q_RANK), l),
        _layer_spec((1, KV_RANK), l),
        _layer_spec((Q_RANK, QK_PAD), l),
        _layer_spec((KV_RANK, QK_PAD), l),
        _layer_spec((KV_RANK, V_WIDTH), l),
        _layer_spec((1, QK_PAD), l),
        _layer_spec((1, QK_PAD), l),
        _layer_spec((1, HEAD_PAD), l),
        pl.BlockSpec((tm, HEAD_PAD), rope_idx),
        pl.BlockSpec((tm, HEAD_PAD), rope_idx),
        _const_spec((QK_PAD, LANES)),
        _const_spec((2 * LANES, QK_PAD)),
        _const_spec((1, LANES)),
        _const_spec((QK_PAD, LANES)),
        _const_spec((2 * LANES, QK_PAD)),
    ]
    out_shape = [
        jax.ShapeDtypeStruct((t, QK_PAD), BF16),
        jax.ShapeDtypeStruct((t, QK_PAD), BF16),
        jax.ShapeDtypeStruct((t, V_WIDTH), BF16),
        jax.ShapeDtypeStruct((t, KV_RANK), F32),
        jax.ShapeDtypeStruct((t, ROPE_DIM), F32),
        jax.ShapeDtypeStruct((t, FNET_WIDTH), BF16),
        jax.ShapeDtypeStruct((t, POOL_WIDTH), F32),
        jax.ShapeDtypeStruct((t, N_BRANCHES * D_MODEL), BF16),
    ]
    if after_moe:
        out_shape.append(jax.ShapeDtypeStruct((t, D_MODEL), F32))
    out_specs = [row(s.shape[1]) for s in out_shape]
    outs = pl.pallas_call(
        functools.partial(_in_kernel, after_moe=after_moe),
        out_shape=out_shape,
        grid=(n_tiles,),
        in_specs=in_specs,
        out_specs=out_specs,
        compiler_params=_cparams("arbitrary"),
        name="in_projection",
    )(*lead_args, mods, wts["norm1_g"], wts["w_in"], wts["q_a_g"], wts["kv_a_g"], wts["w_q"], wts["w_k"], wts["w_v"],
      wts["g_q"], wts["g_k"], wts["g_kpe"], consts["cosf"], consts["sinf"],
      consts["m_q"], consts["mt2_q"], consts["inv_cnt_q"], consts["m_k"], consts["mt2_k"])
    return (outs[:8], outs[8]) if after_moe else (outs, x)


def _ctx_keys_kernel(ckv_ref, kpe_ref, wk_ref, wv_ref, gk_ref, gkpe_ref, mk_ref, mkt2_ref, k_out, v_out):
    _write_keys(ckv_ref[...], kpe_ref[...], wk_ref, wv_ref, gk_ref, gkpe_ref, mk_ref, mkt2_ref, None, k_out, v_out)


def _ctx_keys(cache_ckv, cache_kpe_pad, wts, consts):
    nb, depth, past, _ = cache_ckv.shape
    lw = lambda shape: pl.BlockSpec((None,) + shape, lambda l, b: (l,) + (0,) * len(shape))
    cs = lambda shape: pl.BlockSpec(shape, lambda l, b: (0,) * len(shape))
    return pl.pallas_call(
        _ctx_keys_kernel,
        out_shape=[jax.ShapeDtypeStruct((depth, nb * past, QK_PAD), BF16),
                   jax.ShapeDtypeStruct((depth, nb * past, V_WIDTH), BF16)],
        grid=(depth, nb),
        in_specs=[
            pl.BlockSpec((None, None, past, KV_RANK), lambda l, b: (b, l, 0, 0)),
            pl.BlockSpec((None, None, past, HEAD_PAD), lambda l, b: (b, l, 0, 0)),
            lw((KV_RANK, QK_PAD)), lw((KV_RANK, V_WIDTH)), lw((1, QK_PAD)), lw((1, HEAD_PAD)),
            cs((QK_PAD, LANES)), cs((2 * LANES, QK_PAD)),
        ],
        out_specs=[pl.BlockSpec((None, past, QK_PAD), lambda l, b: (l, b, 0)),
                   pl.BlockSpec((None, past, V_WIDTH), lambda l, b: (l, b, 0))],
        compiler_params=_cparams("arbitrary", "arbitrary"),
        name="context_keys",
    )(cache_ckv, cache_kpe_pad, wts["w_k"], wts["w_v"], wts["g_k"], wts["g_kpe"], consts["m_k"], consts["mt2_k"])


def _attn_kernel(*refs, n_parts):
    q_ref = refs[0]
    k_refs = refs[1:1 + 2 * n_parts:2]
    v_refs = refs[2:2 + 2 * n_parts:2]
    o_ref = refs[1 + 2 * n_parts]
    lane = lax.broadcasted_iota(jnp.int32, (q_ref.shape[0], 2 * V_HEAD), 1)
    for pair in range(N_HEADS // 2):
        vsl = slice(pair * 2 * V_HEAD, (pair + 1) * 2 * V_HEAD)
        outs = []
        for hd in (2 * pair, 2 * pair + 1):
            sl = slice(hd * HEAD_PAD, (hd + 1) * HEAD_PAD)
            qh = q_ref[:, sl]
            ss = [lax.dot_general(qh, k[:, sl], (((1,), (1,)), ((), ())), preferred_element_type=F32)
                  for k in k_refs]
            m = functools.reduce(jnp.maximum, [jnp.max(s, axis=-1, keepdims=True) for s in ss])
            es = [jnp.exp(s - m) for s in ss]
            den = functools.reduce(jnp.add, [jnp.sum(e, axis=-1, keepdims=True) for e in es])
            acc = functools.reduce(jnp.add, [_dot(e.astype(BF16), v[:, vsl]) for e, v in zip(es, v_refs)])
            outs.append(acc / den)
        o_ref[:, vsl] = jnp.where(lane < V_HEAD, outs[0], outs[1]).astype(o_ref.dtype)


def _attention(q, kv_parts, n_seq, seq_len):
    tq = min(ATTN_Q_TILE, seq_len)
    nq = seq_len // tq
    in_specs = [pl.BlockSpec((tq, QK_PAD), lambda b, i: (b * nq + i, 0))]
    args = [q]
    for k, v, layer, rows in kv_parts:
        for arr, width in ((k, QK_PAD), (v, V_WIDTH)):
            if layer is None:
                in_specs.append(pl.BlockSpec((rows, width), lambda b, i: (b, 0)))
            else:
                in_specs.append(pl.BlockSpec((None, rows, width), lambda b, i, _l=layer: (_l, b, 0)))
            args.append(arr)
    return pl.pallas_call(
        functools.partial(_attn_kernel, n_parts=len(kv_parts)),
        out_shape=jax.ShapeDtypeStruct((n_seq * seq_len, V_WIDTH), BF16),
        grid=(n_seq, nq),
        in_specs=in_specs,
        out_specs=pl.BlockSpec((tq, V_WIDTH), lambda b, i: (b * nq + i, 0)),
        compiler_params=_cparams("arbitrary", "arbitrary"),
        name="attention",
    )(*args)


def _fnet_kernel(f_ref, cs_ref, cl_ref, sl_ref, o_ref, top_ref, bot_ref, *, norm):
    @pl.when(pl.program_id(1) == 0)
    def _():
        for g in range(FNET_GROUPS):
            sl = slice(g * FNET_GC, (g + 1) * FNET_GC)
            a = _dot(f_ref[:, sl], cs_ref[...])
            top_ref[:, sl] = a[:, :FNET_GC].astype(BF16)
            bot_ref[:, sl] = a[:, FNET_GC:].astype(BF16)

    y = _dot(cl_ref[...], top_ref[...]) - _dot(sl_ref[...], bot_ref[...])
    o_ref[...] = (y * norm).astype(o_ref.dtype)


def _fourier(f_in, n_seq, seq_len, row0, tabs):
    tr = min(FNET_ROW_TILE, seq_len)
    nj = seq_len // tr
    sb0 = row0 // seq_len
    return pl.pallas_call(
        functools.partial(_fnet_kernel, norm=1.0 / math.sqrt(seq_len * FNET_GC)),
        out_shape=jax.ShapeDtypeStruct((n_seq * seq_len, FNET_WIDTH), BF16),
        grid=(n_seq, nj),
        in_specs=[
            pl.BlockSpec((seq_len, FNET_WIDTH), lambda b, j: (sb0 + b, 0)),
            pl.BlockSpec((FNET_GC, 2 * FNET_GC), lambda b, j: (0, 0)),
            pl.BlockSpec((tr, seq_len), lambda b, j: (j, 0)),
            pl.BlockSpec((tr, seq_len), lambda b, j: (j, 0)),
        ],
        out_specs=pl.BlockSpec((tr, FNET_WIDTH), lambda b, j: (b * nj + j, 0)),
        scratch_shapes=[pltpu.VMEM((seq_len, FNET_WIDTH), BF16), pltpu.VMEM((seq_len, FNET_WIDTH), BF16)],
        compiler_params=_cparams("arbitrary", "arbitrary"),
        name="fourier_mix",
    )(f_in, tabs["chan"], tabs["cos"], tabs["sin"])


def _pool_kernel(p_ref, wg_ref, ps_ref, o_ref, pad_ref):
    seq_len = p_ref.shape[0]
    zeros = jnp.zeros((POOL_HALO, POOL_WIDTH), F32)
    pad_ref[0:POOL_HALO, :] = zeros
    pad_ref[POOL_HALO + seq_len:, :] = zeros
    pad_ref[POOL_HALO:POOL_HALO + seq_len, :] = p_ref[...]
    ch = min(POOL_CHUNK, seq_len)
    for c in range(seq_len // ch):
        t = lax.broadcasted_iota(jnp.int32, (ch, 1), 0) + c * ch
        for g, w in enumerate(POOL_WINDOWS):
            half = w // 2
            sl = slice(g * POOL_GC, (g + 1) * POOL_GC)
            acc = None
            for j in range(-half, half):
                r0 = POOL_HALO + c * ch + j
                part = pad_ref[r0:r0 + ch, sl]
                acc = part if acc is None else acc + part
            cnt = (jnp.minimum(t + half, seq_len) - jnp.maximum(t - half, 0)).astype(F32)
            pooled = acc / cnt - p_ref[c * ch:(c + 1) * ch, sl]
            mixed = _dot(pooled.astype(BF16), wg_ref[g]) * ps_ref[:, sl]
            o_ref[c * ch:(c + 1) * ch, sl] = mixed.astype(o_ref.dtype)


def _pooling(p_in, n_seq, seq_len, row0, l, wts):
    sb0 = row0 // seq_len
    g = len(POOL_WINDOWS)
    return pl.pallas_call(
        _pool_kernel,
        out_shape=jax.ShapeDtypeStruct((n_seq * seq_len, POOL_WIDTH), BF16),
        grid=(n_seq,),
        in_specs=[
            pl.BlockSpec((seq_len, POOL_WIDTH), lambda b: (sb0 + b, 0)),
            pl.BlockSpec((None, g, POOL_GC, POOL_GC), lambda b: (l, 0, 0, 0)),
            pl.BlockSpec((None, 1, POOL_WIDTH), lambda b: (l, 0, 0)),
        ],
        out_specs=pl.BlockSpec((seq_len, POOL_WIDTH), lambda b: (b, 0)),
        scratch_shapes=[pltpu.VMEM((seq_len + 2 * POOL_HALO, POOL_WIDTH), F32)],
        compiler_params=_cparams("arbitrary"),
        name="pool_mix",
    )(p_in, wts["w_pool_grp"], wts["pool_scale"])


def _merge_kernel(x_ref, mod_ref, a_ref, f_ref, p_ref, g_ref,
                  wa_ref, wf_ref, wp_ref, wo_ref, g2_ref, wr_ref, br_ref, tri_ref,
                  x1_out, hp_out, te_out, tw_out, rk_out, cnt_out, carry_ref):
    a_in = a_ref[...]
    f_in = f_ref[...]
    p_in = p_ref[...]
    chunks = []
    for c0 in range(0, D_MODEL, MERGE_COL_CHUNK):
        cs = slice(c0, c0 + MERGE_COL_CHUNK)
        a = _dot(a_in, wa_ref[:, cs])
        f = _dot(f_in, wf_ref[:, cs])
        p = _dot(p_in, wp_ref[:, cs])
        chunks.append((g_ref[:, c0:c0 + MERGE_COL_CHUNK].astype(F32) * a
                       + g_ref[:, D_MODEL + c0:D_MODEL + c0 + MERGE_COL_CHUNK].astype(F32) * f
                       + g_ref[:, 2 * D_MODEL + c0:2 * D_MODEL + c0 + MERGE_COL_CHUNK].astype(F32) * p).astype(BF16))
    mix = _dot(jnp.concatenate(chunks, axis=1), wo_ref[...])
    gate1 = mod_ref[2:3, :]
    shift2 = mod_ref[3:4, :]
    scale2 = mod_ref[4:5, :]
    x1 = x_ref[...] + gate1 * mix
    x1_out[...] = x1
    h2 = _rms(x1) * g2_ref[...] * (1.0 + scale2) + shift2

    h_hi = h2.astype(BF16)
    bits = pltpu.bitcast(h_hi.astype(F32), jnp.int32)
    hp_out[...] = bits[:, :D_MODEL // 2] | lax.shift_right_logical(bits[:, D_MODEL // 2:], 16)

    h_lo = (h2 - h_hi.astype(F32)).astype(BF16)
    lhs = jnp.concatenate([h_hi, h_lo, h_hi], axis=1)
    logits = _dot(lhs, wr_ref[...]) + br_ref[...]

    lane = lax.broadcasted_iota(jnp.int32, logits.shape, 1)
    work = logits
    vals, idxs = [], []
    for _ in range(TOP_K):
        m = jnp.max(work, axis=-1, keepdims=True)
        idx = jnp.min(jnp.where(work == m, lane, ROUTER_PAD), axis=-1, keepdims=True)
        vals.append(m)
        idxs.append(idx)
        work = jnp.where(lane == idx, -jnp.inf, work)
    es = [jnp.exp(v - vals[0]) for v in vals]
    den = functools.reduce(jnp.add, es)

    @pl.when(pl.program_id(0) == 0)
    def _():
        carry_ref[...] = jnp.zeros(carry_ref.shape, F32)

    chosen = functools.reduce(jnp.logical_or, [lane == idx for idx in idxs])
    hot = jnp.where(chosen, 1.0, 0.0)
    before = _dot(tri_ref[...], hot.astype(BF16)) + carry_ref[...]
    carry_ref[...] = carry_ref[...] + jnp.sum(hot, axis=0, keepdims=True)
    cnt_out[...] = carry_ref[...].astype(jnp.int32)

    te = jnp.zeros(logits.shape, jnp.int32)
    tw = jnp.zeros(logits.shape, F32)
    rk = jnp.zeros(logits.shape, jnp.int32)
    for k in range(TOP_K):
        rank_k = jnp.sum(jnp.where(lane == idxs[k], before, 0.0), axis=-1, keepdims=True).astype(jnp.int32)
        te = jnp.where(lane == k, idxs[k], te)
        tw = jnp.where(lane == k, es[k] / den, tw)
        rk = jnp.where(lane == k, rank_k, rk)
    te_out[...] = te
    tw_out[...] = tw
    rk_out[...] = rk


def _merge(x, mods, attn, fnet, pool, gates, l, wts, consts, tok):
    t = x.shape[0]
    tm = TOKEN_TILE
    p_tiles = tok["t_prompt"] // tm
    s_tiles = tok["dec_seq"] // tm

    def mod_idx(i):
        row = jnp.where(i < p_tiles, 0, 1 + (i - p_tiles) // s_tiles)
        return (l, row, 0, 0)

    row = lambda w: pl.BlockSpec((tm, w), lambda i: (i, 0))
    out_shape = [
        jax.ShapeDtypeStruct((t, D_MODEL), F32),
        jax.ShapeDtypeStruct((t, D_MODEL // 2), jnp.int32),
        jax.ShapeDtypeStruct((t, ROUTER_PAD), jnp.int32),
        jax.ShapeDtypeStruct((t, ROUTER_PAD), F32),
        jax.ShapeDtypeStruct((t, ROUTER_PAD), jnp.int32),
    ]
    return pl.pallas_call(
        _merge_kernel,
        out_shape=out_shape + [jax.ShapeDtypeStruct((1, ROUTER_PAD), jnp.int32)],
        grid=(t // tm,),
        in_specs=[
            row(D_MODEL),
            pl.BlockSpec((None, None, 6, D_MODEL), mod_idx),
            row(V_WIDTH), row(FNET_WIDTH), row(POOL_WIDTH), row(N_BRANCHES * D_MODEL),
            _layer_spec((V_WIDTH, D_MODEL), l),
            _layer_spec((FNET_WIDTH, D_MODEL), l),
            _layer_spec((POOL_WIDTH, D_MODEL), l),
            _layer_spec((D_MODEL, D_MODEL), l),
            _layer_spec((1, D_MODEL), l),
            _layer_spec((3 * D_MODEL, ROUTER_PAD), l),
            _layer_spec((1, ROUTER_PAD), l),
            _const_spec((tm, tm)),
        ],
        out_specs=[row(s.shape[1]) for s in out_shape] + [pl.BlockSpec((1, ROUTER_PAD), lambda i: (0, 0))],
        scratch_shapes=[pltpu.VMEM((1, ROUTER_PAD), F32)],
        compiler_params=_cparams("arbitrary"),
        name="merge_router",
    )(x, mods, attn, fnet, pool, gates, wts["w_attn_o"], wts["w_fnet_o"], wts["w_pool_o"], wts["w_out"],
      wts["norm2_g"], wts["w_router3"], wts["b_router"], consts["tri"])


def _sc_row_gather(table, idx):
    n = idx.shape[0]
    d = table.shape[1]
    info = plsc.get_sparse_core_info()
    n_cores = info.num_cores
    n_workers = n_cores * info.num_subcores
    per_worker = n // n_workers
    n_chunks = per_worker // SC_GATHER_ROWS
    assert n_chunks * SC_GATHER_ROWS * n_workers == n
    mesh = plsc.VectorSubcoreMesh(core_axis_name="core", subcore_axis_name="subcore")

    @functools.partial(
        pl.kernel, mesh=mesh, out_type=jax.ShapeDtypeStruct((n, d), table.dtype),
        scratch_types=[pltpu.VMEM((SC_GATHER_ROWS,), jnp.int32), pltpu.VMEM((SC_GATHER_ROWS, d), table.dtype),
                       pltpu.SemaphoreType.DMA],
        name="sc_row_gather")
    def gather(table_hbm, idx_hbm, out_hbm, idx_v, rows_v, sem):
        worker = lax.axis_index("subcore") * n_cores + lax.axis_index("core")

        @pl.loop(0, n_chunks)
        def _(c):
            base = worker * per_worker + c * SC_GATHER_ROWS
            pltpu.sync_copy(idx_hbm.at[pl.ds(base, SC_GATHER_ROWS)], idx_v)
            pltpu.async_copy(table_hbm.at[idx_v], rows_v, sem).wait()
            pltpu.sync_copy(rows_v, out_hbm.at[pl.ds(base, SC_GATHER_ROWS)])

    return gather(table, idx)


PLAN_EXPERT, PLAN_VALID, PLAN_FIRST, PLAN_SLOT, PLAN_NEXT, PLAN_USED = range(6)


def _expert_kernel(plan_ref, xs_ref, bgu_ref, bd_ref, wgu_hbm, wd_hbm, y_ref,
                   wgu_f32, wd_f32, wgu_bf, wd_bf, sems, *, layer):
    b = pl.program_id(0)

    def weight_copies(expert, slot):
        return (pltpu.make_async_copy(wgu_hbm.at[layer, expert], wgu_f32.at[slot], sems.at[0, slot]),
                pltpu.make_async_copy(wd_hbm.at[layer, expert], wd_f32.at[slot], sems.at[1, slot]))

    def ffn(rows):
        w = xs_ref[0:rows, :]
        x = jnp.concatenate([pltpu.bitcast(w & jnp.int32(-65536), F32),
                             pltpu.bitcast(lax.shift_left(w, 16), F32)], axis=1).astype(BF16)
        gu = _dot(x, wgu_bf[...]) + bgu_ref[...]
        glu = jnp.minimum(gu[:, :D_FF], SWIGLU_LIMIT)
        lin = jnp.clip(gu[:, D_FF:], -SWIGLU_LIMIT, SWIGLU_LIMIT)
        act = glu * jax.nn.sigmoid(SWIGLU_ALPHA * glu) * (lin + 1.0)
        y_ref[0:rows, :] = _dot(act.astype(BF16), wd_bf[...]) + bd_ref[...]
        if rows < MOE_BLOCK:
            y_ref[rows:, :] = jnp.zeros((MOE_BLOCK - rows, D_MODEL), y_ref.dtype)

    n_used = plan_ref[PLAN_USED, 0]

    @pl.when(b < n_used)
    def _():
        slot = plan_ref[PLAN_SLOT, b]

        @pl.when(b == 0)
        def _():
            for cp in weight_copies(plan_ref[PLAN_EXPERT, 0], 0):
                cp.start()

        @pl.when(plan_ref[PLAN_FIRST, b] == 1)
        def _():
            for cp in weight_copies(plan_ref[PLAN_EXPERT, b], slot):
                cp.wait()
            wgu_bf[...] = wgu_f32[slot].astype(BF16)
            wd_bf[...] = wd_f32[slot].astype(BF16)

            @pl.when(plan_ref[PLAN_NEXT, b] >= 0)
            def _():
                for cp in weight_copies(plan_ref[PLAN_NEXT, b], 1 - slot):
                    cp.start()

        steps = (plan_ref[PLAN_VALID, b] + (MOE_STEP - 1)) // MOE_STEP
        for rows in range(MOE_STEP, MOE_BLOCK + 1, MOE_STEP):
            want = steps <= 1 if rows == MOE_STEP else steps == rows // MOE_STEP
            pl.when(want)(functools.partial(ffn, rows))

    @pl.when(b >= n_used)
    def _():
        y_ref[...] = jnp.zeros(y_ref.shape, y_ref.dtype)


def _experts(xs, plan, l, w_gu, b_gu4, w_down, b_down4):
    n_slots = xs.shape[0]
    bm = MOE_BLOCK
    n_blocks = n_slots // bm

    def blk(b, plan):
        return jnp.minimum(b, plan[PLAN_USED, 0] - 1)

    def bspec(width):
        return pl.BlockSpec((None, None, 1, width), lambda b, plan: (l, plan[PLAN_EXPERT, blk(b, plan)], 0, 0))

    grid_spec = pltpu.PrefetchScalarGridSpec(
        num_scalar_prefetch=1,
        grid=(n_blocks,),
        in_specs=[
            pl.BlockSpec((bm, D_MODEL // 2), lambda b, plan: (blk(b, plan), 0)),
            bspec(2 * D_FF), bspec(D_MODEL),
            pl.BlockSpec(memory_space=pl.ANY), pl.BlockSpec(memory_space=pl.ANY),
        ],
        out_specs=pl.BlockSpec((bm, D_MODEL), lambda b, plan: (b, 0)),
        scratch_shapes=[pltpu.VMEM((2, D_MODEL, 2 * D_FF), F32), pltpu.VMEM((2, D_FF, D_MODEL), F32),
                        pltpu.VMEM((D_MODEL, 2 * D_FF), BF16), pltpu.VMEM((D_FF, D_MODEL), BF16),
                        pltpu.SemaphoreType.DMA((2, 2))],
    )
    return pl.pallas_call(
        functools.partial(_expert_kernel, layer=l),
        out_shape=jax.ShapeDtypeStruct((n_slots, D_MODEL), F32),
        grid_spec=grid_spec,
        compiler_params=_cparams("arbitrary"),
        name="moe_experts",
    )(plan, xs, b_gu4, b_down4, w_gu, w_down)


def _combine_kernel(tw_ref, x_ref, mod_ref, yg_ref, o_ref):
    o_ref[...] = _moe_residual(tw_ref, x_ref, mod_ref, yg_ref)


def _combine(yg, tw, x1, mods, l, tok):
    t = x1.shape[0]
    tc = COMBINE_TILE
    p_tiles = tok["t_prompt"] // tc
    s_tiles = tok["dec_seq"] // tc

    def mod_idx(i):
        row = jnp.where(i < p_tiles, 0, 1 + (i - p_tiles) // s_tiles)
        return (l, row, 0, 0)

    return pl.pallas_call(
        _combine_kernel,
        out_shape=jax.ShapeDtypeStruct((t, D_MODEL), F32),
        grid=(t // tc,),
        in_specs=[
            pl.BlockSpec((tc, ROUTER_PAD), lambda i: (i, 0)),
            pl.BlockSpec((tc, D_MODEL), lambda i: (i, 0)),
            pl.BlockSpec((None, None, 6, D_MODEL), mod_idx),
            pl.BlockSpec((TOP_K, tc, D_MODEL), lambda i: (0, i, 0)),
        ],
        out_specs=pl.BlockSpec((tc, D_MODEL), lambda i: (i, 0)),
        compiler_params=_cparams("arbitrary"),
        name="moe_combine",
    )(tw, x1, mods, yg)


def _slot_tokens(slot, counts, padded_start, padded_end, n_slots, t):
    n = slot.shape[0]
    j = jnp.arange(MOE_BLOCK, dtype=jnp.int32)
    pad_key = (padded_start + counts)[:, None] + j[None, :]
    pad_key = jnp.where(pad_key < padded_end[:, None], pad_key, n_slots).reshape(N_EXPERTS * MOE_BLOCK)
    pad_val = jnp.arange(N_EXPERTS * MOE_BLOCK, dtype=jnp.int32) % t
    keys = jnp.concatenate([slot, pad_key])
    vals = jnp.concatenate([jnp.arange(n, dtype=jnp.int32) // TOP_K, pad_val])
    tok_bits = max(1, (t - 1).bit_length())
    assert (n_slots + 1) << tok_bits < 2 ** 31
    packed = jnp.sort(lax.shift_left(keys, tok_bits) | vals)
    return packed[:n_slots] & ((1 << tok_bits) - 1)


def _route(te_pad, rank_pad, counts_pad, n_slots):
    t = te_pad.shape[0]
    flat_e = te_pad[:, :TOP_K].reshape(t * TOP_K)
    onehot = (flat_e[:, None] == jnp.arange(N_EXPERTS, dtype=jnp.int32)[None, :]).astype(jnp.int32)
    rank = rank_pad[:, :TOP_K].reshape(t * TOP_K)
    counts = counts_pad[0, :N_EXPERTS]
    padded = (counts + MOE_BLOCK - 1) // MOE_BLOCK * MOE_BLOCK
    padded_end = jnp.cumsum(padded)
    padded_start = padded_end - padded
    slot = jnp.sum(onehot * padded_start[None, :], axis=1) + rank
    n_blocks = n_slots // MOE_BLOCK
    block_row0 = jnp.arange(n_blocks, dtype=jnp.int32) * MOE_BLOCK
    block_exp = jnp.minimum(
        jnp.sum((padded_end[None, :] <= block_row0[:, None]).astype(jnp.int32), axis=1),
        N_EXPERTS - 1).astype(jnp.int32)
    exp_hot = (block_exp[:, None] == jnp.arange(N_EXPERTS, dtype=jnp.int32)[None, :]).astype(jnp.int32)
    rows_end = jnp.sum(exp_hot * (padded_start + counts)[None, :], axis=1)
    block_valid = jnp.clip(rows_end - block_row0, 0, MOE_BLOCK).astype(jnp.int32)
    n_used = (padded_end[-1] // MOE_BLOCK).astype(jnp.int32)
    block_ids = jnp.arange(n_blocks, dtype=jnp.int32)
    prev_exp = jnp.concatenate([jnp.full((1,), -1, jnp.int32), block_exp[:-1]])
    first = ((block_exp != prev_exp) & (block_ids < n_used)).astype(jnp.int32)
    buf_slot = (jnp.cumsum(first) - 1) % 2
    experts = jnp.arange(N_EXPERTS, dtype=jnp.int32)
    holder = jnp.where(counts > 0, experts, N_EXPERTS)
    later = lax.cummin(jnp.concatenate([holder[1:], jnp.full((1,), N_EXPERTS, jnp.int32)]), reverse=True)
    next_exp = jnp.where(later < N_EXPERTS, later, -1)
    block_next = jnp.sum(exp_hot * next_exp[None, :], axis=1)
    plan = jnp.stack([block_exp, block_valid, first, buf_slot, block_next,
                      jnp.full((n_blocks,), n_used, jnp.int32)]).astype(jnp.int32)
    slot = slot.astype(jnp.int32)
    slot_tok = _slot_tokens(slot, counts, padded_start, padded_end, n_slots, t)
    slot_by_choice = slot.reshape(t, TOP_K).T.reshape(TOP_K * t)
    return slot_by_choice, slot_tok, plan


def _segment_matrices():
    m_q = np.zeros((QK_PAD, LANES), np.float32)
    m_k = np.zeros((QK_PAD, LANES), np.float32)
    inv_cnt_q = np.ones((1, LANES), np.float32)
    for h in range(N_HEADS):
        m_q[h * HEAD_PAD:h * HEAD_PAD + QK_NOPE, 2 * h] = 1.0
        m_q[h * HEAD_PAD + QK_NOPE:h * HEAD_PAD + QK_HEAD, 2 * h + 1] = 1.0
        inv_cnt_q[0, 2 * h] = 1.0 / QK_NOPE
        inv_cnt_q[0, 2 * h + 1] = 1.0 / ROPE_DIM
        m_k[h * HEAD_PAD:h * HEAD_PAD + QK_NOPE, h] = 1.0
    dup = lambda m: np.concatenate([m.T, m.T], axis=0)
    return dict(m_q=jnp.asarray(m_q, BF16), mt2_q=jnp.asarray(dup(m_q), BF16), inv_cnt_q=jnp.asarray(inv_cnt_q),
                m_k=jnp.asarray(m_k, BF16), mt2_k=jnp.asarray(dup(m_k), BF16))


def _rope_lane_tables(n_tokens, ident_rows):
    rows = n_tokens // GRID_W
    row = jnp.broadcast_to(jnp.arange(rows)[:, None], (rows, GRID_W)).reshape(n_tokens)
    col = jnp.broadcast_to(jnp.arange(GRID_W)[None, :], (rows, GRID_W)).reshape(n_tokens)
    n_freq = ROPE_DIM // 4
    inv = 1.0 / (ROPE_THETA ** (jnp.arange(n_freq, dtype=F32) / n_freq))
    ang = jnp.concatenate([row[:, None].astype(F32) * inv, col[:, None].astype(F32) * inv], axis=-1)
    cos, sin = jnp.cos(ang), jnp.sin(ang)
    ones = jnp.ones((n_tokens, QK_NOPE), F32)
    tail = HEAD_PAD - QK_HEAD
    cosf = jnp.concatenate([ones, cos, cos, jnp.ones((n_tokens, tail), F32)], axis=1)
    sinf = jnp.concatenate([0.0 * ones, -sin, sin, jnp.zeros((n_tokens, tail), F32)], axis=1)
    cosf = jnp.concatenate([cosf, jnp.ones((ident_rows, HEAD_PAD), F32)], axis=0)
    sinf = jnp.concatenate([sinf, jnp.zeros((ident_rows, HEAD_PAD), F32)], axis=0)
    return cosf, sinf


def _dft_tables(seq_len):
    def cs(n):
        k = np.arange(n, dtype=np.int64)
        ang = 2.0 * np.pi * ((k[:, None] * k[None, :]) % n).astype(np.float64) / n
        return np.cos(ang).astype(np.float32), np.sin(ang).astype(np.float32)

    cl, sl = cs(seq_len)
    cc, sc = cs(FNET_GC)
    return dict(cos=jnp.asarray(cl).astype(BF16), sin=jnp.asarray(sl).astype(BF16),
                chan=jnp.asarray(np.concatenate([cc, sc], axis=1)).astype(BF16))


def _layout_weights(w_in, w_q_b, w_kv_b, q_nope_g, q_rope_g, k_nope_g, k_rope_g, w_router, b_router):
    depth = w_in.shape[0]
    w_in_p = jnp.concatenate(
        [w_in[:, :, :COL_KPE + ROPE_DIM], jnp.zeros((depth, D_MODEL, KPE_PAD - ROPE_DIM), w_in.dtype),
         w_in[:, :, COL_KPE + ROPE_DIM:]], axis=2).astype(BF16)
    w_q = jnp.pad(w_q_b.reshape(depth, Q_RANK, N_HEADS, QK_HEAD),
                  ((0, 0), (0, 0), (0, 0), (0, HEAD_PAD - QK_HEAD))).reshape(depth, Q_RANK, QK_PAD).astype(BF16)
    kv = w_kv_b.reshape(depth, KV_RANK, N_HEADS, QK_NOPE + V_HEAD)
    w_k = jnp.pad(kv[..., :QK_NOPE], ((0, 0), (0, 0), (0, 0), (0, HEAD_PAD - QK_NOPE))
                  ).reshape(depth, KV_RANK, QK_PAD).astype(BF16)
    w_v = kv[..., QK_NOPE:].reshape(depth, KV_RANK, V_WIDTH).astype(BF16)
    zq = jnp.zeros((depth, HEAD_PAD - QK_HEAD), F32)
    g_q = jnp.tile(jnp.concatenate([q_nope_g, q_rope_g, zq], axis=1), (1, N_HEADS))[:, None, :]
    g_k = jnp.tile(jnp.concatenate([k_nope_g, jnp.zeros((depth, HEAD_PAD - QK_NOPE), F32)], axis=1),
                   (1, N_HEADS))[:, None, :]
    g_kpe = jnp.concatenate([jnp.zeros((depth, QK_NOPE), F32), k_rope_g, zq], axis=1)[:, None, :]
    w_hi = w_router.astype(BF16)
    w_lo = (w_router - w_hi.astype(F32)).astype(BF16)
    w_r3 = jnp.pad(jnp.concatenate([w_hi, w_hi, w_lo], axis=1), ((0, 0), (0, 0), (0, ROUTER_PAD - N_EXPERTS)))
    b_r = jnp.pad(b_router, ((0, 0), (0, ROUTER_PAD - N_EXPERTS)), constant_values=NEG_BIG)[:, None, :]
    return dict(w_in=w_in_p, w_q=w_q, w_k=w_k, w_v=w_v, g_q=g_q, g_k=g_k, g_kpe=g_kpe, w_router3=w_r3, b_router=b_r)


def kernel(x_prompt, x_sample, cache_ckv, cache_kpe, c, c_ctx, w_mod, b_mod, norm1_g, norm2_g, w_in, q_a_g, kv_a_g, w_q_b, w_kv_b, q_nope_g, q_rope_g, k_nope_g, k_rope_g, w_attn_o, w_fnet_o, w_pool_grp, pool_scale, w_pool_o, w_out, w_router, b_router, w_gu, b_gu, w_down, b_down):
    batch, seq, d = x_prompt.shape
    dec_batch, dec_seq, _ = x_sample.shape
    depth = w_mod.shape[0]
    past = cache_ckv.shape[2]
    t_prompt = batch * seq
    t_sample = dec_batch * dec_seq
    assert d == D_MODEL and dec_batch + 1 <= 8
    assert t_prompt % TOKEN_TILE == 0 and dec_seq % TOKEN_TILE == 0 and seq % COMBINE_TILE == 0
    assert dec_seq % GRID_W == 0

    consts = _segment_matrices()
    consts["cosf"], consts["sinf"] = _rope_lane_tables(dec_seq, IN_TILE)
    consts["tri"] = jnp.asarray(np.tril(np.ones((TOKEN_TILE, TOKEN_TILE), np.float32), -1), BF16)

    wts = _layout_weights(w_in, w_q_b, w_kv_b, q_nope_g, q_rope_g, k_nope_g, k_rope_g, w_router, b_router)
    row3 = lambda a: a[:, None, :]
    wts.update(norm1_g=row3(norm1_g), norm2_g=row3(norm2_g), q_a_g=row3(q_a_g), kv_a_g=row3(kv_a_g),
               w_attn_o=w_attn_o.astype(BF16), w_fnet_o=w_fnet_o.astype(BF16), w_pool_o=w_pool_o.astype(BF16),
               w_out=w_out.astype(BF16), w_pool_grp=w_pool_grp.astype(BF16), pool_scale=row3(pool_scale))
    b_gu4 = b_gu[:, :, None, :]
    b_down4 = b_down[:, :, None, :]

    cond8 = jnp.concatenate([c_ctx[None, :], c, jnp.zeros((8 - 1 - dec_batch, d), F32)], axis=0)
    mods = _modulation(cond8, w_mod, b_mod).reshape(depth, 8, 6, D_MODEL)

    kpe_pad = jnp.pad(cache_kpe, ((0, 0), (0, 0), (0, 0), (QK_NOPE, HEAD_PAD - QK_HEAD)))
    k_ctx, v_ctx = _ctx_keys(cache_ckv, kpe_pad, wts, consts)

    streams = [
        dict(x=x_prompt.reshape(t_prompt, d), n_seq=batch, seq_len=seq, tok=dict(t_prompt=t_prompt, dec_seq=dec_seq),
             dft=_dft_tables(seq), ctx=None, moe=None),
        dict(x=x_sample.reshape(t_sample, d), n_seq=dec_batch, seq_len=dec_seq, tok=dict(t_prompt=0, dec_seq=dec_seq),
             dft=_dft_tables(dec_seq), ctx=(k_ctx, v_ctx, past), moe=None),
    ]
    ckv_list, kpe_list = [], []
    for l in range(depth):
        for st in streams:
            (q, k, v, ckv, kpe, f_in, p_in, st["gates"]), st["x"] = _in_projection(
                st["x"], st["moe"], mods, l, wts, consts, st["tok"])
            if st["ctx"] is None:
                ckv_list.append(ckv.reshape(batch, seq, KV_RANK))
                kpe_list.append(kpe.reshape(batch, seq, ROPE_DIM))
            parts = [(k, v, None, st["seq_len"])]
            if st["ctx"] is not None:
                parts.append((st["ctx"][0], st["ctx"][1], l, st["ctx"][2]))
            st["attn"] = _attention(q, parts, st["n_seq"], st["seq_len"])
            st["fnet"] = _fourier(f_in, st["n_seq"], st["seq_len"], 0, st["dft"])
            st["pool"] = _pooling(p_in, st["n_seq"], st["seq_len"], 0, l, wts)
        for st in streams:
            t = st["n_seq"] * st["seq_len"]
            n_slots = (t * TOP_K + N_EXPERTS * (MOE_BLOCK - 1) + MOE_BLOCK - 1) // MOE_BLOCK * MOE_BLOCK
            x1, h2p, te_pad, tw_pad, rank_pad, counts_pad = _merge(
                st["x"], mods, st["attn"], st["fnet"], st["pool"], st["gates"], l, wts, consts, st["tok"])
            slot_by_choice, slot_tok, plan = _route(te_pad, rank_pad, counts_pad, n_slots)
            st["routed"] = (x1, tw_pad, slot_by_choice, plan)
            st["xs"] = _sc_row_gather(h2p, slot_tok)
        for st in streams:
            t = st["n_seq"] * st["seq_len"]
            x1, tw_pad, slot_by_choice, plan = st["routed"]
            y = _experts(st["xs"], plan, l, w_gu, b_gu4, w_down, b_down4)
            yg = _sc_row_gather(y, slot_by_choice).reshape(TOP_K, t, D_MODEL)
            st["moe"] = (x1, tw_pad, yg)
            st["x"] = None
    outs = []
    for st in streams:
        x1, tw_pad, yg = st["moe"]
        outs.append(_combine(yg, tw_pad, x1, mods, depth - 1, st["tok"]))
    y_prompt = outs[0].reshape(batch, seq, d)
    y_sample = outs[1].reshape(dec_batch, dec_seq, d)
    return (y_prompt, y_sample, jnp.stack(ckv_list, axis=1), jnp.stack(kpe_list, axis=1))
```

```python
import functools
import math

import numpy as np
import jax
import jax.numpy as jnp
from jax import lax
from jax.experimental import pallas as pl
from jax.experimental.pallas import tpu as pltpu
from jax.experimental.pallas import tpu_sc as plsc

D_MODEL = 1024
GRID_W = 64
N_HEADS = 8
QK_NOPE = 64
ROPE_DIM = 32
V_HEAD = 64
QK_HEAD = QK_NOPE + ROPE_DIM
Q_RANK = 384
KV_RANK = 256
ROPE_THETA = 10000.0
FNET_GROUPS = 4
FNET_GC = 128
FNET_WIDTH = FNET_GROUPS * FNET_GC
POOL_WINDOWS = (2, 4, 8, 16)
POOL_GC = 128
POOL_WIDTH = len(POOL_WINDOWS) * POOL_GC
N_BRANCHES = 3
N_EXPERTS = 32
TOP_K = 4
D_FF = D_MODEL
SWIGLU_LIMIT = 7.0
SWIGLU_ALPHA = 1.702
RMS_EPS = 1e-6

LANES = 128
HEAD_PAD = LANES
QK_PAD = N_HEADS * HEAD_PAD
V_WIDTH = N_HEADS * V_HEAD
ROPE_HALF = ROPE_DIM // 2
KPE_PAD = LANES
COL_KV = Q_RANK
COL_KPE = Q_RANK + KV_RANK
COL_F = COL_KPE + KPE_PAD
COL_P = COL_F + FNET_WIDTH
COL_G = COL_P + POOL_WIDTH
IN_PAD_WIDTH = COL_G + N_BRANCHES * D_MODEL
ROUTER_PAD = LANES
NEG_BIG = -1e30

TOKEN_TILE = 512
IN_TILE = 256
ATTN_Q_TILE = 512
FNET_ROW_TILE = 512
POOL_CHUNK = 256
POOL_HALO = 16
MOE_BLOCK = 512
MOE_STEP = 128
GATE_COL_CHUNK = 512
MERGE_COL_CHUNK = 256
FF_CHUNK = 256
COMBINE_TILE = 256
SC_GATHER_ROWS = 64
MOD_COL_TILE = 1536
VMEM_LIMIT = 56 * 1024 * 1024

F32 = jnp.float32
BF16 = jnp.bfloat16


def _cparams(*sem):
    return pltpu.CompilerParams(dimension_semantics=sem, vmem_limit_bytes=VMEM_LIMIT)


def _dot(a, b):
    return jnp.dot(a, b, preferred_element_type=F32)


def _rms(x):
    return x * lax.rsqrt(jnp.mean(x * x, axis=-1, keepdims=True) + RMS_EPS)


def _pack_bf16_pairs(x):
    half = x.shape[1] // 2
    bits = pltpu.bitcast(x.astype(BF16).astype(F32), jnp.int32)
    return bits[:, :half] | lax.shift_right_logical(bits[:, half:], 16)


def _unpack_bf16_pairs(w):
    return pltpu.bitcast(w & jnp.int32(-65536), F32), pltpu.bitcast(lax.shift_left(w, 16), F32)


def _seg_rms_scale(x, m_ref, mt2_ref, inv_cnt):
    ss = _dot((x * x).astype(BF16), m_ref[...])
    r = lax.rsqrt(ss * inv_cnt + RMS_EPS)
    r_hi = r.astype(BF16)
    r_lo = (r - r_hi.astype(F32)).astype(BF16)
    return _dot(jnp.concatenate([r_hi, r_lo], axis=1), mt2_ref[...])


def _rope_chunk(xc, cosf, sinf, first_half):
    partner = jnp.where(first_half, pltpu.roll(xc, HEAD_PAD - ROPE_HALF, 1), pltpu.roll(xc, ROPE_HALF, 1))
    return xc * cosf + partner * sinf


def _write_keys(ckv, kpe_chunk, wk_ref, wv_ref, gk_ref, gkpe_ref, mk_ref, mkt2_ref, rope, k_out, v_out):
    cb = ckv.astype(BF16)
    kn = _dot(cb, wk_ref[...])
    kn = kn * _seg_rms_scale(kn, mk_ref, mkt2_ref, 1.0 / QK_NOPE) * gk_ref[...]
    v_out[...] = _dot(cb, wv_ref[...]).astype(v_out.dtype)
    ms = jnp.sum(kpe_chunk * kpe_chunk, axis=-1, keepdims=True) * (1.0 / ROPE_DIM)
    kp = kpe_chunk * lax.rsqrt(ms + RMS_EPS) * gkpe_ref[...]
    if rope is not None:
        kp = _rope_chunk(kp, *rope)
    for h in range(N_HEADS):
        sl = slice(h * HEAD_PAD, (h + 1) * HEAD_PAD)
        k_out[:, sl] = (kn[:, sl] + kp).astype(k_out.dtype)


def _mod_kernel(c_ref, w_ref, b_ref, o_ref):
    c = c_ref[...]
    s = c * jax.nn.sigmoid(c)
    o_ref[...] = _dot(s.astype(BF16), w_ref[...].astype(BF16)) + b_ref[...]


def _modulation(cond8, w_mod, b_mod):
    depth = w_mod.shape[0]
    n = w_mod.shape[2]
    return pl.pallas_call(
        _mod_kernel,
        out_shape=jax.ShapeDtypeStruct((depth, 8, n), F32),
        grid=(depth, n // MOD_COL_TILE),
        in_specs=[
            pl.BlockSpec((8, D_MODEL), lambda l, j: (0, 0)),
            pl.BlockSpec((None, D_MODEL, MOD_COL_TILE), lambda l, j: (l, 0, j)),
            pl.BlockSpec((None, 1, MOD_COL_TILE), lambda l, j: (l, 0, j)),
        ],
        out_specs=pl.BlockSpec((None, 8, MOD_COL_TILE), lambda l, j: (l, 0, j)),
        compiler_params=_cparams("arbitrary", "arbitrary"),
        name="adaln_modulation",
    )(cond8, w_mod, b_mod.reshape(depth, 1, n))


def _moe_residual(tw_ref, x1_ref, mod_ref, yg_ref):
    acc = None
    for k in range(TOP_K):
        w = tw_ref[:, k:k + 1]
        part = [w * half for half in _unpack_bf16_pairs(yg_ref[k])]
        acc = part if acc is None else [a + p for a, p in zip(acc, part)]
    return x1_ref[...] + mod_ref[5:6, :] * jnp.concatenate(acc, axis=1)


def _in_kernel(*refs, after_moe):
    if after_moe:
        x = _moe_residual(*refs[:4])
        refs = refs[4:]
        refs[-1][...] = x
        refs = refs[:-1]
    else:
        x = refs[0][...]
        refs = refs[1:]
    (mod_ref, g1_ref, w_ref, qag_ref, kvag_ref, wq_ref, wk_ref, wv_ref,
     gq_ref, gk_ref, gkpe_ref, cos_ref, sin_ref, mq_ref, mqt2_ref, icq_ref, mk_ref, mkt2_ref,
     q_out, k_out, v_out, ckv_out, kpe_out, f_out, p_out, g_out) = refs
    shift = mod_ref[0:1, :]
    scale = mod_ref[1:2, :]
    h = _rms(x) * g1_ref[...] * (1.0 + scale) + shift
    hb = h.astype(BF16)

    za = _dot(hb, w_ref[:, 0:COL_F])
    q_lat = za[:, 0:COL_KV]
    kv_lat = za[:, COL_KV:COL_KPE]
    kpe_grp = za[:, COL_KPE:COL_F]
    ckv = _rms(kv_lat) * kvag_ref[...]
    ckv_out[...] = ckv
    kpe_out[...] = kpe_grp[:, 0:ROPE_DIM]

    cosf = cos_ref[...]
    sinf = sin_ref[...]
    lane = lax.broadcasted_iota(jnp.int32, cosf.shape, 1)
    rope = (cosf, sinf, lane < QK_NOPE + ROPE_HALF)

    qn = (_rms(q_lat) * qag_ref[...]).astype(BF16)
    q = _dot(qn, wq_ref[...])
    q = q * _seg_rms_scale(q, mq_ref, mqt2_ref, icq_ref[...]) * gq_ref[...]
    sm_scale = 1.0 / math.sqrt(QK_HEAD)
    for hd in range(N_HEADS):
        sl = slice(hd * HEAD_PAD, (hd + 1) * HEAD_PAD)
        q_out[:, sl] = (_rope_chunk(q[:, sl], *rope) * sm_scale).astype(q_out.dtype)

    kpe_chunk = pltpu.roll(kpe_grp, QK_NOPE, 1)
    _write_keys(ckv, kpe_chunk, wk_ref, wv_ref, gk_ref, gkpe_ref, mk_ref, mkt2_ref, rope, k_out, v_out)

    zb = _dot(hb, w_ref[:, COL_F:COL_G])
    f_out[...] = zb[:, 0:FNET_WIDTH].astype(f_out.dtype)
    p_out[...] = zb[:, FNET_WIDTH:]
    for c0 in range(0, N_BRANCHES * D_MODEL, GATE_COL_CHUNK):
        zc = _dot(hb, w_ref[:, COL_G + c0:COL_G + c0 + GATE_COL_CHUNK])
        g_out[:, c0:c0 + GATE_COL_CHUNK] = jax.nn.sigmoid(zc).astype(g_out.dtype)


def _const_spec(shape):
    nd = len(shape)
    return pl.BlockSpec(shape, lambda i, _n=nd: (0,) * _n, pipeline_mode=pl.Buffered(1))


def _layer_spec(shape, l):
    nd = len(shape)
    return pl.BlockSpec((None,) + tuple(shape), lambda i, _l=l, _n=nd: (_l,) + (0,) * _n,
                        pipeline_mode=pl.Buffered(1))


def _in_projection(x, moe, mods, l, wts, consts, tok):
    after_moe = x is None
    t = moe[0].shape[0] if after_moe else x.shape[0]
    tm = IN_TILE
    n_tiles = t // tm
    p_tiles = tok["t_prompt"] // tm
    s_tiles = tok["dec_seq"] // tm
    rope_blocks = tok["dec_seq"] // tm

    def mod_row(i):
        return jnp.where(i < p_tiles, 0, 1 + (i - p_tiles) // s_tiles)

    def rope_idx(i):
        return (jnp.where(i < p_tiles, rope_blocks, (i - p_tiles) % s_tiles), 0)

    row = lambda w: pl.BlockSpec((tm, w), lambda i: (i, 0))
    if after_moe:
        x1, tw, yg = moe
        lead_specs = [row(ROUTER_PAD), row(D_MODEL),
                      pl.BlockSpec((None, None, 6, D_MODEL), lambda i: (l - 1, mod_row(i), 0, 0)),
                      pl.BlockSpec((TOP_K, tm, D_MODEL // 2), lambda i: (0, i, 0))]
        lead_args = [tw, x1, mods, yg]
    else:
        lead_specs = [row(D_MODEL)]
        lead_args = [x]
    in_specs = lead_specs + [
        pl.BlockSpec((None, None, 6, D_MODEL), lambda i: (l, mod_row(i), 0, 0)),
        _layer_spec((1, D_MODEL), l),
        _layer_spec((D_MODEL, IN_PAD_WIDTH), l),
        _layer_spec((1, Q_RANK), l),
        _layer_spec((1, KV_RANK), l),
        _layer_spec((Q_RANK, QK_PAD), l),
        _layer_spec((KV_RANK, QK_PAD), l),
        _layer_spec((KV_RANK, V_WIDTH), l),
        _layer_spec((1, QK_PAD), l),
        _layer_spec((1, QK_PAD), l),
        _layer_spec((1, HEAD_PAD), l),
        pl.BlockSpec((tm, HEAD_PAD), rope_idx),
        pl.BlockSpec((tm, HEAD_PAD), rope_idx),
        _const_spec((QK_PAD, LANES)),
        _const_spec((2 * LANES, QK_PAD)),
        _const_spec((1, LANES)),
        _const_spec((QK_PAD, LANES)),
        _const_spec((2 * LANES, QK_PAD)),
    ]
    out_shape = [
        jax.ShapeDtypeStruct((t, QK_PAD), BF16),
        jax.ShapeDtypeStruct((t, QK_PAD), BF16),
        jax.ShapeDtypeStruct((t, V_WIDTH), BF16),
        jax.ShapeDtypeStruct((t, KV_RANK), F32),
        jax.ShapeDtypeStruct((t, ROPE_DIM), F32),
        jax.ShapeDtypeStruct((t, FNET_WIDTH), BF16),
        jax.ShapeDtypeStruct((t, POOL_WIDTH), F32),
        jax.ShapeDtypeStruct((t, N_BRANCHES * D_MODEL), BF16),
    ]
    if after_moe:
        out_shape.append(jax.ShapeDtypeStruct((t, D_MODEL), F32))
    out_specs = [row(s.shape[1]) for s in out_shape]
    outs = pl.pallas_call(
        functools.partial(_in_kernel, after_moe=after_moe),
        out_shape=out_shape,
        grid=(n_tiles,),
        in_specs=in_specs,
        out_specs=out_specs,
        compiler_params=_cparams("arbitrary"),
        cost_estimate=pl.CostEstimate(
            flops=2 * t * (D_MODEL * IN_PAD_WIDTH + Q_RANK * QK_PAD + KV_RANK * (QK_PAD + V_WIDTH)
                           + 2 * (QK_PAD * LANES + 2 * LANES * QK_PAD)),
            transcendentals=t * N_BRANCHES * D_MODEL,
            bytes_accessed=2 * D_MODEL * IN_PAD_WIDTH + sum(
                math.prod(s.shape) * s.dtype.itemsize for s in out_shape)
            + 4 * t * D_MODEL * (1 + (TOP_K // 2 if after_moe else 0))),
        name="in_projection",
    )(*lead_args, mods, wts["norm1_g"], wts["w_in"], wts["q_a_g"], wts["kv_a_g"], wts["w_q"], wts["w_k"], wts["w_v"],
      wts["g_q"], wts["g_k"], wts["g_kpe"], consts["cosf"], consts["sinf"],
      consts["m_q"], consts["mt2_q"], consts["inv_cnt_q"], consts["m_k"], consts["mt2_k"])
    return (outs[:8], outs[8]) if after_moe else (outs, x)


def _ctx_keys_kernel(ckv_ref, kpe_ref, wk_ref, wv_ref, gk_ref, gkpe_ref, mk_ref, mkt2_ref, k_out, v_out):
    _write_keys(ckv_ref[...], kpe_ref[...], wk_ref, wv_ref, gk_ref, gkpe_ref, mk_ref, mkt2_ref, None, k_out, v_out)


def _ctx_keys(cache_ckv, cache_kpe_pad, wts, consts):
    nb, depth, past, _ = cache_ckv.shape
    lw = lambda shape: pl.BlockSpec((None,) + shape, lambda l, b: (l,) + (0,) * len(shape))
    cs = lambda shape: pl.BlockSpec(shape, lambda l, b: (0,) * len(shape))
    return pl.pallas_call(
        _ctx_keys_kernel,
        out_shape=[jax.ShapeDtypeStruct((depth, nb * past, QK_PAD), BF16),
                   jax.ShapeDtypeStruct((depth, nb * past, V_WIDTH), BF16)],
        grid=(depth, nb),
        in_specs=[
            pl.BlockSpec((None, None, past, KV_RANK), lambda l, b: (b, l, 0, 0)),
            pl.BlockSpec((None, None, past, HEAD_PAD), lambda l, b: (b, l, 0, 0)),
            lw((KV_RANK, QK_PAD)), lw((KV_RANK, V_WIDTH)), lw((1, QK_PAD)), lw((1, HEAD_PAD)),
            cs((QK_PAD, LANES)), cs((2 * LANES, QK_PAD)),
        ],
        out_specs=[pl.BlockSpec((None, past, QK_PAD), lambda l, b: (l, b, 0)),
                   pl.BlockSpec((None, past, V_WIDTH), lambda l, b: (l, b, 0))],
        compiler_params=_cparams("arbitrary", "arbitrary"),
        name="context_keys",
    )(cache_ckv, cache_kpe_pad, wts["w_k"], wts["w_v"], wts["g_k"], wts["g_kpe"], consts["m_k"], consts["mt2_k"])


def _attn_kernel(*refs, n_parts):
    q_ref = refs[0]
    k_refs = refs[1:1 + 2 * n_parts:2]
    v_refs = refs[2:2 + 2 * n_parts:2]
    o_ref = refs[1 + 2 * n_parts]
    lane = lax.broadcasted_iota(jnp.int32, (q_ref.shape[0], 2 * V_HEAD), 1)
    for pair in range(N_HEADS // 2):
        vsl = slice(pair * 2 * V_HEAD, (pair + 1) * 2 * V_HEAD)
        outs = []
        for hd in (2 * pair, 2 * pair + 1):
            sl = slice(hd * HEAD_PAD, (hd + 1) * HEAD_PAD)
            qh = q_ref[:, sl]
            ss = [lax.dot_general(qh, k[:, sl], (((1,), (1,)), ((), ())), preferred_element_type=F32)
                  for k in k_refs]
            m = functools.reduce(jnp.maximum, [jnp.max(s, axis=-1, keepdims=True) for s in ss])
            es = [jnp.exp(s - m) for s in ss]
            den = functools.reduce(jnp.add, [jnp.sum(e, axis=-1, keepdims=True) for e in es])
            acc = functools.reduce(jnp.add, [_dot(e.astype(BF16), v[:, vsl]) for e, v in zip(es, v_refs)])
            outs.append(acc / den)
        o_ref[:, vsl] = jnp.where(lane < V_HEAD, outs[0], outs[1]).astype(o_ref.dtype)


def _attention(q, kv_parts, n_seq, seq_len):
    tq = min(ATTN_Q_TILE, seq_len)
    nq = seq_len // tq
    n_keys = sum(rows for _, _, _, rows in kv_parts)
    in_specs = [pl.BlockSpec((tq, QK_PAD), lambda b, i: (b * nq + i, 0))]
    args = [q]
    for k, v, layer, rows in kv_parts:
        for arr, width in ((k, QK_PAD), (v, V_WIDTH)):
            if layer is None:
                in_specs.append(pl.BlockSpec((rows, width), lambda b, i: (b, 0)))
            else:
                in_specs.append(pl.BlockSpec((None, rows, width), lambda b, i, _l=layer: (_l, b, 0)))
            args.append(arr)
    return pl.pallas_call(
        functools.partial(_attn_kernel, n_parts=len(kv_parts)),
        out_shape=jax.ShapeDtypeStruct((n_seq * seq_len, V_WIDTH), BF16),
        grid=(n_seq, nq),
        in_specs=in_specs,
        out_specs=pl.BlockSpec((tq, V_WIDTH), lambda b, i: (b * nq + i, 0)),
        compiler_params=_cparams("arbitrary", "arbitrary"),
        cost_estimate=pl.CostEstimate(
            flops=2 * n_seq * seq_len * n_keys * N_HEADS * (HEAD_PAD + 2 * V_HEAD),
            transcendentals=n_seq * seq_len * n_keys * N_HEADS,
            bytes_accessed=2 * n_seq * (seq_len * (QK_PAD + V_WIDTH) + n_keys * (QK_PAD + V_WIDTH))),
        name="attention",
    )(*args)


def _fnet_kernel(f_ref, cs_ref, cl_ref, sl_ref, o_ref, top_ref, bot_ref, *, norm):
    @pl.when(pl.program_id(1) == 0)
    def _():
        for g in range(FNET_GROUPS):
            sl = slice(g * FNET_GC, (g + 1) * FNET_GC)
            a = _dot(f_ref[:, sl], cs_ref[...])
            top_ref[:, sl] = a[:, :FNET_GC].astype(BF16)
            bot_ref[:, sl] = a[:, FNET_GC:].astype(BF16)

    y = _dot(cl_ref[...], top_ref[...]) - _dot(sl_ref[...], bot_ref[...])
    o_ref[...] = (y * norm).astype(o_ref.dtype)


def _fourier(f_in, n_seq, seq_len, row0, tabs):
    tr = min(FNET_ROW_TILE, seq_len)
    nj = seq_len // tr
    sb0 = row0 // seq_len
    return pl.pallas_call(
        functools.partial(_fnet_kernel, norm=1.0 / math.sqrt(seq_len * FNET_GC)),
        out_shape=jax.ShapeDtypeStruct((n_seq * seq_len, FNET_WIDTH), BF16),
        grid=(n_seq, nj),
        in_specs=[
            pl.BlockSpec((seq_len, FNET_WIDTH), lambda b, j: (sb0 + b, 0)),
            pl.BlockSpec((FNET_GC, 2 * FNET_GC), lambda b, j: (0, 0)),
            pl.BlockSpec((tr, seq_len), lambda b, j: (j, 0)),
            pl.BlockSpec((tr, seq_len), lambda b, j: (j, 0)),
        ],
        out_specs=pl.BlockSpec((tr, FNET_WIDTH), lambda b, j: (b * nj + j, 0)),
        scratch_shapes=[pltpu.VMEM((seq_len, FNET_WIDTH), BF16), pltpu.VMEM((seq_len, FNET_WIDTH), BF16)],
        compiler_params=_cparams("arbitrary", "arbitrary"),
        name="fourier_mix",
    )(f_in, tabs["chan"], tabs["cos"], tabs["sin"])


def _pool_kernel(p_ref, wg_ref, ps_ref, o_ref, pad_ref):
    seq_len = p_ref.shape[0]
    zeros = jnp.zeros((POOL_HALO, POOL_WIDTH), F32)
    pad_ref[0:POOL_HALO, :] = zeros
    pad_ref[POOL_HALO + seq_len:, :] = zeros
    pad_ref[POOL_HALO:POOL_HALO + seq_len, :] = p_ref[...]
    ch = min(POOL_CHUNK, seq_len)
    for c in range(seq_len // ch):
        t = lax.broadcasted_iota(jnp.int32, (ch, 1), 0) + c * ch
        for g, w in enumerate(POOL_WINDOWS):
            half = w // 2
            sl = slice(g * POOL_GC, (g + 1) * POOL_GC)
            acc = None
            for j in range(-half, half):
                r0 = POOL_HALO + c * ch + j
                part = pad_ref[r0:r0 + ch, sl]
                acc = part if acc is None else acc + part
            cnt = (jnp.minimum(t + half, seq_len) - jnp.maximum(t - half, 0)).astype(F32)
            pooled = acc / cnt - p_ref[c * ch:(c + 1) * ch, sl]
            mixed = _dot(pooled.astype(BF16), wg_ref[g]) * ps_ref[:, sl]
            o_ref[c * ch:(c + 1) * ch, sl] = mixed.astype(o_ref.dtype)


def _pooling(p_in, n_seq, seq_len, row0, l, wts):
    sb0 = row0 // seq_len
    g = len(POOL_WINDOWS)
    return pl.pallas_call(
        _pool_kernel,
        out_shape=jax.ShapeDtypeStruct((n_seq * seq_len, POOL_WIDTH), BF16),
        grid=(n_seq,),
        in_specs=[
            pl.BlockSpec((seq_len, POOL_WIDTH), lambda b: (sb0 + b, 0)),
            pl.BlockSpec((None, g, POOL_GC, POOL_GC), lambda b: (l, 0, 0, 0)),
            pl.BlockSpec((None, 1, POOL_WIDTH), lambda b: (l, 0, 0)),
        ],
        out_specs=pl.BlockSpec((seq_len, POOL_WIDTH), lambda b: (b, 0)),
        scratch_shapes=[pltpu.VMEM((seq_len + 2 * POOL_HALO, POOL_WIDTH), F32)],
        compiler_params=_cparams("arbitrary"),
        name="pool_mix",
    )(p_in, wts["w_pool_grp"], wts["pool_scale"])


def _merge_kernel(x_ref, mod_ref, a_ref, f_ref, p_ref, g_ref,
                  wa_ref, wf_ref, wp_ref, wo_ref, g2_ref, wr_ref, br_ref, tri_ref,
                  x1_out, hp_out, te_out, tw_out, rk_out, cnt_out, carry_ref):
    a_in = a_ref[...]
    f_in = f_ref[...]
    p_in = p_ref[...]
    chunks = []
    for c0 in range(0, D_MODEL, MERGE_COL_CHUNK):
        cs = slice(c0, c0 + MERGE_COL_CHUNK)
        a = _dot(a_in, wa_ref[:, cs])
        f = _dot(f_in, wf_ref[:, cs])
        p = _dot(p_in, wp_ref[:, cs])
        chunks.append((g_ref[:, c0:c0 + MERGE_COL_CHUNK].astype(F32) * a
                       + g_ref[:, D_MODEL + c0:D_MODEL + c0 + MERGE_COL_CHUNK].astype(F32) * f
                       + g_ref[:, 2 * D_MODEL + c0:2 * D_MODEL + c0 + MERGE_COL_CHUNK].astype(F32) * p).astype(BF16))
    mix = _dot(jnp.concatenate(chunks, axis=1), wo_ref[...])
    gate1 = mod_ref[2:3, :]
    shift2 = mod_ref[3:4, :]
    scale2 = mod_ref[4:5, :]
    x1 = x_ref[...] + gate1 * mix
    x1_out[...] = x1
    h2 = _rms(x1) * g2_ref[...] * (1.0 + scale2) + shift2

    hp_out[...] = _pack_bf16_pairs(h2)
    h_hi = h2.astype(BF16)

    h_lo = (h2 - h_hi.astype(F32)).astype(BF16)
    lhs = jnp.concatenate([h_hi, h_lo, h_hi], axis=1)
    logits = _dot(lhs, wr_ref[...]) + br_ref[...]

    lane = lax.broadcasted_iota(jnp.int32, logits.shape, 1)
    work = logits
    vals, idxs = [], []
    for _ in range(TOP_K):
        m = jnp.max(work, axis=-1, keepdims=True)
        idx = jnp.min(jnp.where(work == m, lane, ROUTER_PAD), axis=-1, keepdims=True)
        vals.append(m)
        idxs.append(idx)
        work = jnp.where(lane == idx, -jnp.inf, work)
    es = [jnp.exp(v - vals[0]) for v in vals]
    den = functools.reduce(jnp.add, es)

    @pl.when(pl.program_id(0) == 0)
    def _():
        carry_ref[...] = jnp.zeros(carry_ref.shape, F32)

    chosen = functools.reduce(jnp.logical_or, [lane == idx for idx in idxs])
    hot = jnp.where(chosen, 1.0, 0.0)
    before = _dot(tri_ref[...], hot.astype(BF16)) + carry_ref[...]
    carry_ref[...] = carry_ref[...] + jnp.sum(hot, axis=0, keepdims=True)
    cnt_out[...] = carry_ref[...].astype(jnp.int32)

    te = jnp.zeros(logits.shape, jnp.int32)
    tw = jnp.zeros(logits.shape, F32)
    rk = jnp.zeros(logits.shape, jnp.int32)
    for k in range(TOP_K):
        rank_k = jnp.sum(jnp.where(lane == idxs[k], before, 0.0), axis=-1, keepdims=True).astype(jnp.int32)
        te = jnp.where(lane == k, idxs[k], te)
        tw = jnp.where(lane == k, es[k] / den, tw)
        rk = jnp.where(lane == k, rank_k, rk)
    te_out[...] = te
    tw_out[...] = tw
    rk_out[...] = rk


def _merge(x, mods, attn, fnet, pool, gates, l, wts, consts, tok):
    t = x.shape[0]
    tm = TOKEN_TILE
    p_tiles = tok["t_prompt"] // tm
    s_tiles = tok["dec_seq"] // tm

    def mod_idx(i):
        row = jnp.where(i < p_tiles, 0, 1 + (i - p_tiles) // s_tiles)
        return (l, row, 0, 0)

    row = lambda w: pl.BlockSpec((tm, w), lambda i: (i, 0))
    out_shape = [
        jax.ShapeDtypeStruct((t, D_MODEL), F32),
        jax.ShapeDtypeStruct((t, D_MODEL // 2), jnp.int32),
        jax.ShapeDtypeStruct((t, ROUTER_PAD), jnp.int32),
        jax.ShapeDtypeStruct((t, ROUTER_PAD), F32),
        jax.ShapeDtypeStruct((t, ROUTER_PAD), jnp.int32),
    ]
    return pl.pallas_call(
        _merge_kernel,
        out_shape=out_shape + [jax.ShapeDtypeStruct((1, ROUTER_PAD), jnp.int32)],
        grid=(t // tm,),
        in_specs=[
            row(D_MODEL),
            pl.BlockSpec((None, None, 6, D_MODEL), mod_idx),
            row(V_WIDTH), row(FNET_WIDTH), row(POOL_WIDTH), row(N_BRANCHES * D_MODEL),
            _layer_spec((V_WIDTH, D_MODEL), l),
            _layer_spec((FNET_WIDTH, D_MODEL), l),
            _layer_spec((POOL_WIDTH, D_MODEL), l),
            _layer_spec((D_MODEL, D_MODEL), l),
            _layer_spec((1, D_MODEL), l),
            _layer_spec((3 * D_MODEL, ROUTER_PAD), l),
            _layer_spec((1, ROUTER_PAD), l),
            _const_spec((tm, tm)),
        ],
        out_specs=[row(s.shape[1]) for s in out_shape] + [pl.BlockSpec((1, ROUTER_PAD), lambda i: (0, 0))],
        scratch_shapes=[pltpu.VMEM((1, ROUTER_PAD), F32)],
        compiler_params=_cparams("arbitrary"),
        cost_estimate=pl.CostEstimate(
            flops=2 * t * (D_MODEL * (V_WIDTH + FNET_WIDTH + POOL_WIDTH + D_MODEL) + 3 * D_MODEL * ROUTER_PAD
                           + tm * ROUTER_PAD),
            transcendentals=t * TOP_K,
            bytes_accessed=t * (2 * (V_WIDTH + FNET_WIDTH + POOL_WIDTH + N_BRANCHES * D_MODEL) + 4 * 2 * D_MODEL
                                + 4 * (D_MODEL // 2) + 3 * 4 * ROUTER_PAD)
            + 2 * D_MODEL * (V_WIDTH + FNET_WIDTH + POOL_WIDTH + D_MODEL + 3 * ROUTER_PAD)),
        name="merge_router",
    )(x, mods, attn, fnet, pool, gates, wts["w_attn_o"], wts["w_fnet_o"], wts["w_pool_o"], wts["w_out"],
      wts["norm2_g"], wts["w_router3"], wts["b_router"], consts["tri"])


def _sc_row_gather(table, idx):
    n = idx.shape[0]
    d = table.shape[1]
    info = plsc.get_sparse_core_info()
    n_cores = info.num_cores
    n_workers = n_cores * info.num_subcores
    per_worker = n // n_workers
    n_chunks = per_worker // SC_GATHER_ROWS
    assert n_chunks * SC_GATHER_ROWS * n_workers == n
    mesh = plsc.VectorSubcoreMesh(core_axis_name="core", subcore_axis_name="subcore")

    @functools.partial(
        pl.kernel, mesh=mesh, out_type=jax.ShapeDtypeStruct((n, d), table.dtype),
        scratch_types=[pltpu.VMEM((SC_GATHER_ROWS,), jnp.int32), pltpu.VMEM((SC_GATHER_ROWS, d), table.dtype),
                       pltpu.SemaphoreType.DMA],
        cost_estimate=pl.CostEstimate(flops=0, transcendentals=0,
                                      bytes_accessed=2 * n * d * table.dtype.itemsize + 4 * n),
        name="sc_row_gather")
    def gather(table_hbm, idx_hbm, out_hbm, idx_v, rows_v, sem):
        worker = lax.axis_index("subcore") * n_cores + lax.axis_index("core")

        @pl.loop(0, n_chunks)
        def _(c):
            base = worker * per_worker + c * SC_GATHER_ROWS
            pltpu.sync_copy(idx_hbm.at[pl.ds(base, SC_GATHER_ROWS)], idx_v)
            pltpu.async_copy(table_hbm.at[idx_v], rows_v, sem).wait()
            pltpu.sync_copy(rows_v, out_hbm.at[pl.ds(base, SC_GATHER_ROWS)])

    return gather(table, idx)


PLAN_EXPERT, PLAN_VALID, PLAN_FIRST, PLAN_SLOT, PLAN_NEXT, PLAN_USED = range(6)


def _expert_kernel(plan_ref, xs_ref, bgu_ref, bd_ref, wgu_hbm, wd_hbm, y_ref,
                   wgu_f32, wd_f32, wgu_bf, wd_bf, sems, *, layer):
    b = pl.program_id(0)

    def weight_copies(expert, slot):
        return (pltpu.make_async_copy(wgu_hbm.at[layer, expert], wgu_f32.at[slot], sems.at[0, slot]),
                pltpu.make_async_copy(wd_hbm.at[layer, expert], wd_f32.at[slot], sems.at[1, slot]))

    def ffn(rows):
        x = jnp.concatenate(_unpack_bf16_pairs(xs_ref[0:rows, :]), axis=1).astype(BF16)
        gu = _dot(x, wgu_bf[...]) + bgu_ref[...]
        glu = jnp.minimum(gu[:, :D_FF], SWIGLU_LIMIT)
        lin = jnp.clip(gu[:, D_FF:], -SWIGLU_LIMIT, SWIGLU_LIMIT)
        act = glu * jax.nn.sigmoid(SWIGLU_ALPHA * glu) * (lin + 1.0)
        y_ref[0:rows, :] = _pack_bf16_pairs(_dot(act.astype(BF16), wd_bf[...]) + bd_ref[...])
        if rows < MOE_BLOCK:
            y_ref[rows:, :] = jnp.zeros((MOE_BLOCK - rows, D_MODEL // 2), y_ref.dtype)

    n_used = plan_ref[PLAN_USED, 0]

    @pl.when(b < n_used)
    def _():
        slot = plan_ref[PLAN_SLOT, b]

        @pl.when(b == 0)
        def _():
            for cp in weight_copies(plan_ref[PLAN_EXPERT, 0], 0):
                cp.start()

        @pl.when(plan_ref[PLAN_FIRST, b] == 1)
        def _():
            for cp in weight_copies(plan_ref[PLAN_EXPERT, b], slot):
                cp.wait()
            wgu_bf[...] = wgu_f32[slot].astype(BF16)
            wd_bf[...] = wd_f32[slot].astype(BF16)

            @pl.when(plan_ref[PLAN_NEXT, b] >= 0)
            def _():
                for cp in weight_copies(plan_ref[PLAN_NEXT, b], 1 - slot):
                    cp.start()

        steps = (plan_ref[PLAN_VALID, b] + (MOE_STEP - 1)) // MOE_STEP
        for rows in range(MOE_STEP, MOE_BLOCK + 1, MOE_STEP):
            want = steps <= 1 if rows == MOE_STEP else steps == rows // MOE_STEP
            pl.when(want)(functools.partial(ffn, rows))

    @pl.when(b >= n_used)
    def _():
        y_ref[...] = jnp.zeros(y_ref.shape, y_ref.dtype)


def _experts(xs, plan, l, w_gu, b_gu4, w_down, b_down4):
    n_slots = xs.shape[0]
    bm = MOE_BLOCK
    n_blocks = n_slots // bm

    def blk(b, plan):
        return jnp.minimum(b, plan[PLAN_USED, 0] - 1)

    def bspec(width):
        return pl.BlockSpec((None, None, 1, width), lambda b, plan: (l, plan[PLAN_EXPERT, blk(b, plan)], 0, 0))

    grid_spec = pltpu.PrefetchScalarGridSpec(
        num_scalar_prefetch=1,
        grid=(n_blocks,),
        in_specs=[
            pl.BlockSpec((bm, D_MODEL // 2), lambda b, plan: (blk(b, plan), 0)),
            bspec(2 * D_FF), bspec(D_MODEL),
            pl.BlockSpec(memory_space=pl.ANY), pl.BlockSpec(memory_space=pl.ANY),
        ],
        out_specs=pl.BlockSpec((bm, D_MODEL // 2), lambda b, plan: (b, 0)),
        scratch_shapes=[pltpu.VMEM((2, D_MODEL, 2 * D_FF), F32), pltpu.VMEM((2, D_FF, D_MODEL), F32),
                        pltpu.VMEM((D_MODEL, 2 * D_FF), BF16), pltpu.VMEM((D_FF, D_MODEL), BF16),
                        pltpu.SemaphoreType.DMA((2, 2))],
    )
    return pl.pallas_call(
        functools.partial(_expert_kernel, layer=l),
        out_shape=jax.ShapeDtypeStruct((n_slots, D_MODEL // 2), jnp.int32),
        grid_spec=grid_spec,
        compiler_params=_cparams("arbitrary"),
        cost_estimate=pl.CostEstimate(
            flops=2 * n_slots * 3 * D_MODEL * D_FF, transcendentals=n_slots * D_FF,
            bytes_accessed=4 * N_EXPERTS * 3 * D_MODEL * D_FF + 2 * 4 * n_slots * (D_MODEL // 2)),
        name="moe_experts",
    )(plan, xs, b_gu4, b_down4, w_gu, w_down)


def _combine_kernel(tw_ref, x_ref, mod_ref, yg_ref, o_ref):
    o_ref[...] = _moe_residual(tw_ref, x_ref, mod_ref, yg_ref)


def _combine(yg, tw, x1, mods, l, tok):
    t = x1.shape[0]
    tc = COMBINE_TILE
    p_tiles = tok["t_prompt"] // tc
    s_tiles = tok["dec_seq"] // tc

    def mod_idx(i):
        row = jnp.where(i < p_tiles, 0, 1 + (i - p_tiles) // s_tiles)
        return (l, row, 0, 0)

    return pl.pallas_call(
        _combine_kernel,
        out_shape=jax.ShapeDtypeStruct((t, D_MODEL), F32),
        grid=(t // tc,),
        in_specs=[
            pl.BlockSpec((tc, ROUTER_PAD), lambda i: (i, 0)),
            pl.BlockSpec((tc, D_MODEL), lambda i: (i, 0)),
            pl.BlockSpec((None, None, 6, D_MODEL), mod_idx),
            pl.BlockSpec((TOP_K, tc, D_MODEL // 2), lambda i: (0, i, 0)),
        ],
        out_specs=pl.BlockSpec((tc, D_MODEL), lambda i: (i, 0)),
        compiler_params=_cparams("arbitrary"),
        name="moe_combine",
    )(tw, x1, mods, yg)


def _slot_tokens(slot, counts, padded_start, padded_end, n_slots, t):
    n = slot.shape[0]
    j = jnp.arange(MOE_BLOCK, dtype=jnp.int32)
    pad_key = (padded_start + counts)[:, None] + j[None, :]
    pad_key = jnp.where(pad_key < padded_end[:, None], pad_key, n_slots).reshape(N_EXPERTS * MOE_BLOCK)
    pad_val = jnp.arange(N_EXPERTS * MOE_BLOCK, dtype=jnp.int32) % t
    keys = jnp.concatenate([slot, pad_key])
    vals = jnp.concatenate([jnp.arange(n, dtype=jnp.int32) // TOP_K, pad_val])
    tok_bits = max(1, (t - 1).bit_length())
    assert (n_slots + 1) << tok_bits < 2 ** 31
    packed = jnp.sort(lax.shift_left(keys, tok_bits) | vals)
    return packed[:n_slots] & ((1 << tok_bits) - 1)


def _route(te_pad, rank_pad, counts_pad, n_slots):
    t = te_pad.shape[0]
    flat_e = te_pad[:, :TOP_K].reshape(t * TOP_K)
    onehot = (flat_e[:, None] == jnp.arange(N_EXPERTS, dtype=jnp.int32)[None, :]).astype(jnp.int32)
    rank = rank_pad[:, :TOP_K].reshape(t * TOP_K)
    counts = counts_pad[0, :N_EXPERTS]
    padded = (counts + MOE_BLOCK - 1) // MOE_BLOCK * MOE_BLOCK
    padded_end = jnp.cumsum(padded)
    padded_start = padded_end - padded
    slot = jnp.sum(onehot * padded_start[None, :], axis=1) + rank
    n_blocks = n_slots // MOE_BLOCK
    block_row0 = jnp.arange(n_blocks, dtype=jnp.int32) * MOE_BLOCK
    block_exp = jnp.minimum(
        jnp.sum((padded_end[None, :] <= block_row0[:, None]).astype(jnp.int32), axis=1),
        N_EXPERTS - 1).astype(jnp.int32)
    exp_hot = (block_exp[:, None] == jnp.arange(N_EXPERTS, dtype=jnp.int32)[None, :]).astype(jnp.int32)
    rows_end = jnp.sum(exp_hot * (padded_start + counts)[None, :], axis=1)
    block_valid = jnp.clip(rows_end - block_row0, 0, MOE_BLOCK).astype(jnp.int32)
    n_used = (padded_end[-1] // MOE_BLOCK).astype(jnp.int32)
    block_ids = jnp.arange(n_blocks, dtype=jnp.int32)
    prev_exp = jnp.concatenate([jnp.full((1,), -1, jnp.int32), block_exp[:-1]])
    first = ((block_exp != prev_exp) & (block_ids < n_used)).astype(jnp.int32)
    buf_slot = (jnp.cumsum(first) - 1) % 2
    experts = jnp.arange(N_EXPERTS, dtype=jnp.int32)
    holder = jnp.where(counts > 0, experts, N_EXPERTS)
    later = lax.cummin(jnp.concatenate([holder[1:], jnp.full((1,), N_EXPERTS, jnp.int32)]), reverse=True)
    next_exp = jnp.where(later < N_EXPERTS, later, -1)
    block_next = jnp.sum(exp_hot * next_exp[None, :], axis=1)
    plan = jnp.stack([block_exp, block_valid, first, buf_slot, block_next,
                      jnp.full((n_blocks,), n_used, jnp.int32)]).astype(jnp.int32)
    slot = slot.astype(jnp.int32)
    slot_tok = _slot_tokens(slot, counts, padded_start, padded_end, n_slots, t)
    slot_by_choice = slot.reshape(t, TOP_K).T.reshape(TOP_K * t)
    return slot_by_choice, slot_tok, plan


def _segment_matrices():
    m_q = np.zeros((QK_PAD, LANES), np.float32)
    m_k = np.zeros((QK_PAD, LANES), np.float32)
    inv_cnt_q = np.ones((1, LANES), np.float32)
    for h in range(N_HEADS):
        m_q[h * HEAD_PAD:h * HEAD_PAD + QK_NOPE, 2 * h] = 1.0
        m_q[h * HEAD_PAD + QK_NOPE:h * HEAD_PAD + QK_HEAD, 2 * h + 1] = 1.0
        inv_cnt_q[0, 2 * h] = 1.0 / QK_NOPE
        inv_cnt_q[0, 2 * h + 1] = 1.0 / ROPE_DIM
        m_k[h * HEAD_PAD:h * HEAD_PAD + QK_NOPE, h] = 1.0
    dup = lambda m: np.concatenate([m.T, m.T], axis=0)
    return dict(m_q=jnp.asarray(m_q, BF16), mt2_q=jnp.asarray(dup(m_q), BF16), inv_cnt_q=jnp.asarray(inv_cnt_q),
                m_k=jnp.asarray(m_k, BF16), mt2_k=jnp.asarray(dup(m_k), BF16))


def _rope_lane_tables(n_tokens, ident_rows):
    rows = n_tokens // GRID_W
    row = jnp.broadcast_to(jnp.arange(rows)[:, None], (rows, GRID_W)).reshape(n_tokens)
    col = jnp.broadcast_to(jnp.arange(GRID_W)[None, :], (rows, GRID_W)).reshape(n_tokens)
    n_freq = ROPE_DIM // 4
    inv = 1.0 / (ROPE_THETA ** (jnp.arange(n_freq, dtype=F32) / n_freq))
    ang = jnp.concatenate([row[:, None].astype(F32) * inv, col[:, None].astype(F32) * inv], axis=-1)
    cos, sin = jnp.cos(ang), jnp.sin(ang)
    ones = jnp.ones((n_tokens, QK_NOPE), F32)
    tail = HEAD_PAD - QK_HEAD
    cosf = jnp.concatenate([ones, cos, cos, jnp.ones((n_tokens, tail), F32)], axis=1)
    sinf = jnp.concatenate([0.0 * ones, -sin, sin, jnp.zeros((n_tokens, tail), F32)], axis=1)
    cosf = jnp.concatenate([cosf, jnp.ones((ident_rows, HEAD_PAD), F32)], axis=0)
    sinf = jnp.concatenate([sinf, jnp.zeros((ident_rows, HEAD_PAD), F32)], axis=0)
    return cosf, sinf


def _dft_tables(seq_len):
    def cs(n):
        k = np.arange(n, dtype=np.int64)
        ang = 2.0 * np.pi * ((k[:, None] * k[None, :]) % n).astype(np.float64) / n
        return np.cos(ang).astype(np.float32), np.sin(ang).astype(np.float32)

    cl, sl = cs(seq_len)
    cc, sc = cs(FNET_GC)
    return dict(cos=jnp.asarray(cl).astype(BF16), sin=jnp.asarray(sl).astype(BF16),
                chan=jnp.asarray(np.concatenate([cc, sc], axis=1)).astype(BF16))


def _layout_weights(w_in, w_q_b, w_kv_b, q_nope_g, q_rope_g, k_nope_g, k_rope_g, w_router, b_router):
    depth = w_in.shape[0]
    w_in_p = jnp.concatenate(
        [w_in[:, :, :COL_KPE + ROPE_DIM], jnp.zeros((depth, D_MODEL, KPE_PAD - ROPE_DIM), w_in.dtype),
         w_in[:, :, COL_KPE + ROPE_DIM:]], axis=2).astype(BF16)
    w_q = jnp.pad(w_q_b.reshape(depth, Q_RANK, N_HEADS, QK_HEAD),
                  ((0, 0), (0, 0), (0, 0), (0, HEAD_PAD - QK_HEAD))).reshape(depth, Q_RANK, QK_PAD).astype(BF16)
    kv = w_kv_b.reshape(depth, KV_RANK, N_HEADS, QK_NOPE + V_HEAD)
    w_k = jnp.pad(kv[..., :QK_NOPE], ((0, 0), (0, 0), (0, 0), (0, HEAD_PAD - QK_NOPE))
                  ).reshape(depth, KV_RANK, QK_PAD).astype(BF16)
    w_v = kv[..., QK_NOPE:].reshape(depth, KV_RANK, V_WIDTH).astype(BF16)
    zq = jnp.zeros((depth, HEAD_PAD - QK_HEAD), F32)
    g_q = jnp.tile(jnp.concatenate([q_nope_g, q_rope_g, zq], axis=1), (1, N_HEADS))[:, None, :]
    g_k = jnp.tile(jnp.concatenate([k_nope_g, jnp.zeros((depth, HEAD_PAD - QK_NOPE), F32)], axis=1),
                   (1, N_HEADS))[:, None, :]
    g_kpe = jnp.concatenate([jnp.zeros((depth, QK_NOPE), F32), k_rope_g, zq], axis=1)[:, None, :]
    w_hi = w_router.astype(BF16)
    w_lo = (w_router - w_hi.astype(F32)).astype(BF16)
    w_r3 = jnp.pad(jnp.concatenate([w_hi, w_hi, w_lo], axis=1), ((0, 0), (0, 0), (0, ROUTER_PAD - N_EXPERTS)))
    b_r = jnp.pad(b_router, ((0, 0), (0, ROUTER_PAD - N_EXPERTS)), constant_values=NEG_BIG)[:, None, :]
    return dict(w_in=w_in_p, w_q=w_q, w_k=w_k, w_v=w_v, g_q=g_q, g_k=g_k, g_kpe=g_kpe, w_router3=w_r3, b_router=b_r)


def kernel(x_prompt, x_sample, cache_ckv, cache_kpe, c, c_ctx, w_mod, b_mod, norm1_g, norm2_g, w_in, q_a_g, kv_a_g, w_q_b, w_kv_b, q_nope_g, q_rope_g, k_nope_g, k_rope_g, w_attn_o, w_fnet_o, w_pool_grp, pool_scale, w_pool_o, w_out, w_router, b_router, w_gu, b_gu, w_down, b_down):
    batch, seq, d = x_prompt.shape
    dec_batch, dec_seq, _ = x_sample.shape
    depth = w_mod.shape[0]
    past = cache_ckv.shape[2]
    t_prompt = batch * seq
    t_sample = dec_batch * dec_seq
    assert d == D_MODEL and dec_batch + 1 <= 8
    assert t_prompt % TOKEN_TILE == 0 and dec_seq % TOKEN_TILE == 0 and seq % COMBINE_TILE == 0
    assert dec_seq % GRID_W == 0

    consts = _segment_matrices()
    consts["cosf"], consts["sinf"] = _rope_lane_tables(dec_seq, IN_TILE)
    consts["tri"] = jnp.asarray(np.tril(np.ones((TOKEN_TILE, TOKEN_TILE), np.float32), -1), BF16)

    wts = _layout_weights(w_in, w_q_b, w_kv_b, q_nope_g, q_rope_g, k_nope_g, k_rope_g, w_router, b_router)
    row3 = lambda a: a[:, None, :]
    wts.update(norm1_g=row3(norm1_g), norm2_g=row3(norm2_g), q_a_g=row3(q_a_g), kv_a_g=row3(kv_a_g),
               w_attn_o=w_attn_o.astype(BF16), w_fnet_o=w_fnet_o.astype(BF16), w_pool_o=w_pool_o.astype(BF16),
               w_out=w_out.astype(BF16), w_pool_grp=w_pool_grp.astype(BF16), pool_scale=row3(pool_scale))
    b_gu4 = b_gu[:, :, None, :]
    b_down4 = b_down[:, :, None, :]

    cond8 = jnp.concatenate([c_ctx[None, :], c, jnp.zeros((8 - 1 - dec_batch, d), F32)], axis=0)
    mods = _modulation(cond8, w_mod, b_mod).reshape(depth, 8, 6, D_MODEL)

    kpe_pad = jnp.pad(cache_kpe, ((0, 0), (0, 0), (0, 0), (QK_NOPE, HEAD_PAD - QK_HEAD)))
    k_ctx, v_ctx = _ctx_keys(cache_ckv, kpe_pad, wts, consts)

    streams = [
        dict(x=x_prompt.reshape(t_prompt, d), n_seq=batch, seq_len=seq, tok=dict(t_prompt=t_prompt, dec_seq=dec_seq),
             dft=_dft_tables(seq), ctx=None, moe=None),
        dict(x=x_sample.reshape(t_sample, d), n_seq=dec_batch, seq_len=dec_seq, tok=dict(t_prompt=0, dec_seq=dec_seq),
             dft=_dft_tables(dec_seq), ctx=(k_ctx, v_ctx, past), moe=None),
    ]
    ckv_list, kpe_list = [], []
    for l in range(depth):
        for st in streams:
            (q, k, v, ckv, kpe, f_in, p_in, st["gates"]), st["x"] = _in_projection(
                st["x"], st["moe"], mods, l, wts, consts, st["tok"])
            if st["ctx"] is None:
                ckv_list.append(ckv.reshape(batch, seq, KV_RANK))
                kpe_list.append(kpe.reshape(batch, seq, ROPE_DIM))
            parts = [(k, v, None, st["seq_len"])]
            if st["ctx"] is not None:
                parts.append((st["ctx"][0], st["ctx"][1], l, st["ctx"][2]))
            st["attn"] = _attention(q, parts, st["n_seq"], st["seq_len"])
            st["fnet"] = _fourier(f_in, st["n_seq"], st["seq_len"], 0, st["dft"])
            st["pool"] = _pooling(p_in, st["n_seq"], st["seq_len"], 0, l, wts)
        for st in streams:
            t = st["n_seq"] * st["seq_len"]
            n_slots = (t * TOP_K + N_EXPERTS * (MOE_BLOCK - 1) + MOE_BLOCK - 1) // MOE_BLOCK * MOE_BLOCK
            x1, h2p, te_pad, tw_pad, rank_pad, counts_pad = _merge(
                st["x"], mods, st["attn"], st["fnet"], st["pool"], st["gates"], l, wts, consts, st["tok"])
            slot_by_choice, slot_tok, plan = _route(te_pad, rank_pad, counts_pad, n_slots)
            st["routed"] = (x1, tw_pad, slot_by_choice, plan)
            st["xs"] = _sc_row_gather(h2p, slot_tok)
        for st in streams:
            t = st["n_seq"] * st["seq_len"]
            x1, tw_pad, slot_by_choice, plan = st["routed"]
            y = _experts(st["xs"], plan, l, w_gu, b_gu4, w_down, b_down4)
            yg = _sc_row_gather(y, slot_by_choice).reshape(TOP_K, t, D_MODEL // 2)
            st["moe"] = (x1, tw_pad, yg)
            st["x"] = None
    outs = []
    for st in streams:
        x1, tw_pad, yg = st["moe"]
        outs.append(_combine(yg, tw_pad, x1, mods, depth - 1, st["tok"]))
    y_prompt = outs[0].reshape(batch, seq, d)
    y_sample = outs[1].reshape(dec_batch, dec_seq, d)
    return (y_prompt, y_sample, jnp.stack(ckv_list, axis=1), jnp.stack(kpe_list, axis=1))
```

```python
import functools
import math

import numpy as np
import jax
import jax.numpy as jnp
from jax import lax
from jax.experimental import pallas as pl
from jax.experimental.pallas import tpu as pltpu
from jax.experimental.pallas import tpu_sc as plsc

D_MODEL = 1024
GRID_W = 64
N_HEADS = 8
QK_NOPE = 64
ROPE_DIM = 32
V_HEAD = 64
QK_HEAD = QK_NOPE + ROPE_DIM
Q_RANK = 384
KV_RANK = 256
ROPE_THETA = 10000.0
FNET_GROUPS = 4
FNET_GC = 128
FNET_WIDTH = FNET_GROUPS * FNET_GC
POOL_WINDOWS = (2, 4, 8, 16)
POOL_GC = 128
POOL_WIDTH = len(POOL_WINDOWS) * POOL_GC
N_BRANCHES = 3
N_EXPERTS = 32
TOP_K = 4
D_FF = D_MODEL
SWIGLU_LIMIT = 7.0
SWIGLU_ALPHA = 1.702
RMS_EPS = 1e-6

LANES = 128
HEAD_PAD = LANES
QK_PAD = N_HEADS * HEAD_PAD
V_WIDTH = N_HEADS * V_HEAD
ROPE_HALF = ROPE_DIM // 2
KPE_PAD = LANES
COL_KV = Q_RANK
COL_KPE = Q_RANK + KV_RANK
COL_F = COL_KPE + KPE_PAD
COL_P = COL_F + FNET_WIDTH
COL_G = COL_P + POOL_WIDTH
IN_PAD_WIDTH = COL_G + N_BRANCHES * D_MODEL
ROUTER_PAD = LANES
NEG_BIG = -1e30

TOKEN_TILE = 512
IN_TILE = 256
ATTN_Q_TILE = 512
FNET_ROW_TILE = 512
POOL_CHUNK = 256
POOL_HALO = 16
MOE_BLOCK = 512
MOE_STEP = 128
GATE_COL_CHUNK = 512
MERGE_COL_CHUNK = 256
FF_CHUNK = 256
COMBINE_TILE = 256
SC_GATHER_ROWS = 64
MOD_COL_TILE = 1536
VMEM_LIMIT = 56 * 1024 * 1024

F32 = jnp.float32
BF16 = jnp.bfloat16


def _cparams(*sem):
    return pltpu.CompilerParams(dimension_semantics=sem, vmem_limit_bytes=VMEM_LIMIT)


def _dot(a, b):
    return jnp.dot(a, b, preferred_element_type=F32)


def _rms(x):
    return x * lax.rsqrt(jnp.mean(x * x, axis=-1, keepdims=True) + RMS_EPS)


def _pack_bf16_pairs(x):
    half = x.shape[1] // 2
    bits = pltpu.bitcast(x.astype(BF16).astype(F32), jnp.int32)
    return bits[:, :half] | lax.shift_right_logical(bits[:, half:], 16)


def _unpack_bf16_pairs(w):
    return pltpu.bitcast(w & jnp.int32(-65536), F32), pltpu.bitcast(lax.shift_left(w, 16), F32)


def _seg_rms_scale(x, m_ref, mt2_ref, inv_cnt):
    ss = _dot((x * x).astype(BF16), m_ref[...])
    r = lax.rsqrt(ss * inv_cnt + RMS_EPS)
    r_hi = r.astype(BF16)
    r_lo = (r - r_hi.astype(F32)).astype(BF16)
    return _dot(jnp.concatenate([r_hi, r_lo], axis=1), mt2_ref[...])


def _rope_chunk(xc, cosf, sinf, first_half):
    partner = jnp.where(first_half, pltpu.roll(xc, HEAD_PAD - ROPE_HALF, 1), pltpu.roll(xc, ROPE_HALF, 1))
    return xc * cosf + partner * sinf


def _write_keys(ckv, kpe_chunk, wk_ref, wv_ref, gk_ref, gkpe_ref, mk_ref, mkt2_ref, rope, k_out, v_out):
    cb = ckv.astype(BF16)
    kn = _dot(cb, wk_ref[...])
    kn = kn * _seg_rms_scale(kn, mk_ref, mkt2_ref, 1.0 / QK_NOPE) * gk_ref[...]
    v_out[...] = _dot(cb, wv_ref[...]).astype(v_out.dtype)
    ms = jnp.sum(kpe_chunk * kpe_chunk, axis=-1, keepdims=True) * (1.0 / ROPE_DIM)
    kp = kpe_chunk * lax.rsqrt(ms + RMS_EPS) * gkpe_ref[...]
    if rope is not None:
        kp = _rope_chunk(kp, *rope)
    for h in range(N_HEADS):
        sl = slice(h * HEAD_PAD, (h + 1) * HEAD_PAD)
        k_out[:, sl] = (kn[:, sl] + kp).astype(k_out.dtype)


def _mod_kernel(c_ref, w_ref, b_ref, o_ref):
    c = c_ref[...]
    s = c * jax.nn.sigmoid(c)
    o_ref[...] = _dot(s.astype(BF16), w_ref[...].astype(BF16)) + b_ref[...]


def _modulation(cond8, w_mod, b_mod):
    depth = w_mod.shape[0]
    n = w_mod.shape[2]
    return pl.pallas_call(
        _mod_kernel,
        out_shape=jax.ShapeDtypeStruct((depth, 8, n), F32),
        grid=(depth, n // MOD_COL_TILE),
        in_specs=[
            pl.BlockSpec((8, D_MODEL), lambda l, j: (0, 0)),
            pl.BlockSpec((None, D_MODEL, MOD_COL_TILE), lambda l, j: (l, 0, j)),
            pl.BlockSpec((None, 1, MOD_COL_TILE), lambda l, j: (l, 0, j)),
        ],
        out_specs=pl.BlockSpec((None, 8, MOD_COL_TILE), lambda l, j: (l, 0, j)),
        compiler_params=_cparams("arbitrary", "arbitrary"),
        name="adaln_modulation",
    )(cond8, w_mod, b_mod.reshape(depth, 1, n))


def _moe_residual(tw_ref, x1_ref, mod_ref, yg_ref):
    acc = None
    for k in range(TOP_K):
        w = tw_ref[:, k:k + 1]
        part = [w * half for half in _unpack_bf16_pairs(yg_ref[k])]
        acc = part if acc is None else [a + p for a, p in zip(acc, part)]
    return x1_ref[...] + mod_ref[5:6, :] * jnp.concatenate(acc, axis=1)


def _in_kernel(*refs, after_moe):
    if after_moe:
        x = _moe_residual(*refs[:4])
        refs = refs[4:]
        refs[-1][...] = x
        refs = refs[:-1]
    else:
        x = refs[0][...]
        refs = refs[1:]
    (mod_ref, g1_ref, w_ref, qag_ref, kvag_ref, wq_ref, wk_ref, wv_ref,
     gq_ref, gk_ref, gkpe_ref, cos_ref, sin_ref, mq_ref, mqt2_ref, icq_ref, mk_ref, mkt2_ref,
     q_out, k_out, v_out, ckv_out, kpe_out, f_out, p_out, g_out) = refs
    shift = mod_ref[0:1, :]
    scale = mod_ref[1:2, :]
    h = _rms(x) * g1_ref[...] * (1.0 + scale) + shift
    hb = h.astype(BF16)

    za = _dot(hb, w_ref[:, 0:COL_F])
    q_lat = za[:, 0:COL_KV]
    kv_lat = za[:, COL_KV:COL_KPE]
    kpe_grp = za[:, COL_KPE:COL_F]
    ckv = _rms(kv_lat) * kvag_ref[...]
    ckv_out[...] = ckv
    kpe_out[...] = kpe_grp[:, 0:ROPE_DIM]

    cosf = cos_ref[...]
    sinf = sin_ref[...]
    lane = lax.broadcasted_iota(jnp.int32, cosf.shape, 1)
    rope = (cosf, sinf, lane < QK_NOPE + ROPE_HALF)

    qn = (_rms(q_lat) * qag_ref[...]).astype(BF16)
    q = _dot(qn, wq_ref[...])
    q = q * _seg_rms_scale(q, mq_ref, mqt2_ref, icq_ref[...]) * gq_ref[...]
    sm_scale = 1.0 / math.sqrt(QK_HEAD)
    for hd in range(N_HEADS):
        sl = slice(hd * HEAD_PAD, (hd + 1) * HEAD_PAD)
        q_out[:, sl] = (_rope_chunk(q[:, sl], *rope) * sm_scale).astype(q_out.dtype)

    kpe_chunk = pltpu.roll(kpe_grp, QK_NOPE, 1)
    _write_keys(ckv, kpe_chunk, wk_ref, wv_ref, gk_ref, gkpe_ref, mk_ref, mkt2_ref, rope, k_out, v_out)

    zb = _dot(hb, w_ref[:, COL_F:COL_G])
    f_out[...] = zb[:, 0:FNET_WIDTH].astype(f_out.dtype)
    p_out[...] = zb[:, FNET_WIDTH:]
    for c0 in range(0, N_BRANCHES * D_MODEL, GATE_COL_CHUNK):
        zc = _dot(hb, w_ref[:, COL_G + c0:COL_G + c0 + GATE_COL_CHUNK])
        g_out[:, c0:c0 + GATE_COL_CHUNK] = jax.nn.sigmoid(zc).astype(g_out.dtype)


def _const_spec(shape):
    nd = len(shape)
    return pl.BlockSpec(shape, lambda i, _n=nd: (0,) * _n, pipeline_mode=pl.Buffered(1))


def _layer_spec(shape, l):
    nd = len(shape)
    return pl.BlockSpec((None,) + tuple(shape), lambda i, _l=l, _n=nd: (_l,) + (0,) * _n,
                        pipeline_mode=pl.Buffered(1))


def _in_projection(x, moe, mods, l, wts, consts, tok):
    after_moe = x is None
    t = moe[0].shape[0] if after_moe else x.shape[0]
    tm = IN_TILE
    n_tiles = t // tm
    p_tiles = tok["t_prompt"] // tm
    s_tiles = tok["dec_seq"] // tm
    rope_blocks = tok["dec_seq"] // tm

    def mod_row(i):
        return jnp.where(i < p_tiles, 0, 1 + (i - p_tiles) // s_tiles)

    def rope_idx(i):
        return (jnp.where(i < p_tiles, rope_blocks, (i - p_tiles) % s_tiles), 0)

    row = lambda w: pl.BlockSpec((tm, w), lambda i: (i, 0))
    if after_moe:
        x1, tw, yg = moe
        lead_specs = [row(ROUTER_PAD), row(D_MODEL),
                      pl.BlockSpec((None, None, 6, D_MODEL), lambda i: (l - 1, mod_row(i), 0, 0)),
                      pl.BlockSpec((TOP_K, tm, D_MODEL // 2), lambda i: (0, i, 0))]
        lead_args = [tw, x1, mods, yg]
    else:
        lead_specs = [row(D_MODEL)]
        lead_args = [x]
    in_specs = lead_specs + [
        pl.BlockSpec((None, None, 6, D_MODEL), lambda i: (l, mod_row(i), 0, 0)),
        _layer_spec((1, D_MODEL), l),
        _layer_spec((D_MODEL, IN_PAD_WIDTH), l),
        _layer_spec((1, Q_RANK), l),
        _layer_spec((1, KV_RANK), l),
        _layer_spec((Q_RANK, QK_PAD), l),
        _layer_spec((KV_RANK, QK_PAD), l),
        _layer_spec((KV_RANK, V_WIDTH), l),
        _layer_spec((1, QK_PAD), l),
        _layer_spec((1, QK_PAD), l),
        _layer_spec((1, HEAD_PAD), l),
        pl.BlockSpec((tm, HEAD_PAD), rope_idx),
        pl.BlockSpec((tm, HEAD_PAD), rope_idx),
        _const_spec((QK_PAD, LANES)),
        _const_spec((2 * LANES, QK_PAD)),
        _const_spec((1, LANES)),
        _const_spec((QK_PAD, LANES)),
        _const_spec((2 * LANES, QK_PAD)),
    ]
    out_shape = [
        jax.ShapeDtypeStruct((t, QK_PAD), BF16),
        jax.ShapeDtypeStruct((t, QK_PAD), BF16),
        jax.ShapeDtypeStruct((t, V_WIDTH), BF16),
        jax.ShapeDtypeStruct((t, KV_RANK), F32),
        jax.ShapeDtypeStruct((t, ROPE_DIM), F32),
        jax.ShapeDtypeStruct((t, FNET_WIDTH), BF16),
        jax.ShapeDtypeStruct((t, POOL_WIDTH), F32),
        jax.ShapeDtypeStruct((t, N_BRANCHES * D_MODEL), BF16),
    ]
    if after_moe:
        out_shape.append(jax.ShapeDtypeStruct((t, D_MODEL), F32))
    out_specs = [row(s.shape[1]) for s in out_shape]
    outs = pl.pallas_call(
        functools.partial(_in_kernel, after_moe=after_moe),
        out_shape=out_shape,
        grid=(n_tiles,),
        in_specs=in_specs,
        out_specs=out_specs,
        compiler_params=_cparams("arbitrary"),
        cost_estimate=pl.CostEstimate(
            flops=2 * t * (D_MODEL * IN_PAD_WIDTH + Q_RANK * QK_PAD + KV_RANK * (QK_PAD + V_WIDTH)
                           + 2 * (QK_PAD * LANES + 2 * LANES * QK_PAD)),
            transcendentals=t * N_BRANCHES * D_MODEL,
            bytes_accessed=2 * D_MODEL * IN_PAD_WIDTH + sum(
                math.prod(s.shape) * s.dtype.itemsize for s in out_shape)
            + 4 * t * D_MODEL * (1 + (TOP_K // 2 if after_moe else 0))),
        name="in_projection",
    )(*lead_args, mods, wts["norm1_g"], wts["w_in"], wts["q_a_g"], wts["kv_a_g"], wts["w_q"], wts["w_k"], wts["w_v"],
      wts["g_q"], wts["g_k"], wts["g_kpe"], consts["cosf"], consts["sinf"],
      consts["m_q"], consts["mt2_q"], consts["inv_cnt_q"], consts["m_k"], consts["mt2_k"])
    return (outs[:8], outs[8]) if after_moe else (outs, x)


def _ctx_keys_kernel(ckv_ref, kpe_ref, wk_ref, wv_ref, gk_ref, gkpe_ref, mk_ref, mkt2_ref, k_out, v_out):
    _write_keys(ckv_ref[...], kpe_ref[...], wk_ref, wv_ref, gk_ref, gkpe_ref, mk_ref, mkt2_ref, None, k_out, v_out)


def _ctx_keys(cache_ckv, cache_kpe_pad, wts, consts):
    nb, depth, past, _ = cache_ckv.shape
    lw = lambda shape: pl.BlockSpec((None,) + shape, lambda l, b: (l,) + (0,) * len(shape))
    cs = lambda shape: pl.BlockSpec(shape, lambda l, b: (0,) * len(shape))
    return pl.pallas_call(
        _ctx_keys_kernel,
        out_shape=[jax.ShapeDtypeStruct((depth, nb * past, QK_PAD), BF16),
                   jax.ShapeDtypeStruct((depth, nb * past, V_WIDTH), BF16)],
        grid=(depth, nb),
        in_specs=[
            pl.BlockSpec((None, None, past, KV_RANK), lambda l, b: (b, l, 0, 0)),
            pl.BlockSpec((None, None, past, HEAD_PAD), lambda l, b: (b, l, 0, 0)),
            lw((KV_RANK, QK_PAD)), lw((KV_RANK, V_WIDTH)), lw((1, QK_PAD)), lw((1, HEAD_PAD)),
            cs((QK_PAD, LANES)), cs((2 * LANES, QK_PAD)),
        ],
        out_specs=[pl.BlockSpec((None, past, QK_PAD), lambda l, b: (l, b, 0)),
                   pl.BlockSpec((None, past, V_WIDTH), lambda l, b: (l, b, 0))],
        compiler_params=_cparams("arbitrary", "arbitrary"),
        name="context_keys",
    )(cache_ckv, cache_kpe_pad, wts["w_k"], wts["w_v"], wts["g_k"], wts["g_kpe"], consts["m_k"], consts["mt2_k"])


def _attn_kernel(*refs, n_parts):
    q_ref = refs[0]
    k_refs = refs[1:1 + 2 * n_parts:2]
    v_refs = refs[2:2 + 2 * n_parts:2]
    o_ref = refs[1 + 2 * n_parts]
    lane = lax.broadcasted_iota(jnp.int32, (q_ref.shape[0], 2 * V_HEAD), 1)
    for pair in range(N_HEADS // 2):
        vsl = slice(pair * 2 * V_HEAD, (pair + 1) * 2 * V_HEAD)
        outs = []
        for hd in (2 * pair, 2 * pair + 1):
            sl = slice(hd * HEAD_PAD, (hd + 1) * HEAD_PAD)
            qh = q_ref[:, sl]
            ss = [lax.dot_general(qh, k[:, sl], (((1,), (1,)), ((), ())), preferred_element_type=F32)
                  for k in k_refs]
            m = functools.reduce(jnp.maximum, [jnp.max(s, axis=-1, keepdims=True) for s in ss])
            es = [jnp.exp(s - m) for s in ss]
            den = functools.reduce(jnp.add, [jnp.sum(e, axis=-1, keepdims=True) for e in es])
            acc = functools.reduce(jnp.add, [_dot(e.astype(BF16), v[:, vsl]) for e, v in zip(es, v_refs)])
            outs.append(acc / den)
        o_ref[:, vsl] = jnp.where(lane < V_HEAD, outs[0], outs[1]).astype(o_ref.dtype)


def _attention(q, kv_parts, n_seq, seq_len):
    tq = min(ATTN_Q_TILE, seq_len)
    nq = seq_len // tq
    n_keys = sum(rows for _, _, _, rows in kv_parts)
    in_specs = [pl.BlockSpec((tq, QK_PAD), lambda b, i: (b * nq + i, 0))]
    args = [q]
    for k, v, layer, rows in kv_parts:
        for arr, width in ((k, QK_PAD), (v, V_WIDTH)):
            if layer is None:
                in_specs.append(pl.BlockSpec((rows, width), lambda b, i: (b, 0)))
            else:
                in_specs.append(pl.BlockSpec((None, rows, width), lambda b, i, _l=layer: (_l, b, 0)))
            args.append(arr)
    return pl.pallas_call(
        functools.partial(_attn_kernel, n_parts=len(kv_parts)),
        out_shape=jax.ShapeDtypeStruct((n_seq * seq_len, V_WIDTH), BF16),
        grid=(n_seq, nq),
        in_specs=in_specs,
        out_specs=pl.BlockSpec((tq, V_WIDTH), lambda b, i: (b * nq + i, 0)),
        compiler_params=_cparams("arbitrary", "arbitrary"),
        cost_estimate=pl.CostEstimate(
            flops=2 * n_seq * seq_len * n_keys * N_HEADS * (HEAD_PAD + 2 * V_HEAD),
            transcendentals=n_seq * seq_len * n_keys * N_HEADS,
            bytes_accessed=2 * n_seq * (seq_len * (QK_PAD + V_WIDTH) + n_keys * (QK_PAD + V_WIDTH))),
        name="attention",
    )(*args)


def _fnet_kernel(f_ref, cs_ref, cl_ref, sl_ref, o_ref, top_ref, bot_ref, *, norm):
    @pl.when(pl.program_id(1) == 0)
    def _():
        for g in range(FNET_GROUPS):
            sl = slice(g * FNET_GC, (g + 1) * FNET_GC)
            a = _dot(f_ref[:, sl], cs_ref[...])
            top_ref[:, sl] = a[:, :FNET_GC].astype(BF16)
            bot_ref[:, sl] = a[:, FNET_GC:].astype(BF16)

    y = _dot(cl_ref[...], top_ref[...]) - _dot(sl_ref[...], bot_ref[...])
    o_ref[...] = (y * norm).astype(o_ref.dtype)


def _fourier(f_in, n_seq, seq_len, row0, tabs):
    tr = min(FNET_ROW_TILE, seq_len)
    nj = seq_len // tr
    sb0 = row0 // seq_len
    return pl.pallas_call(
        functools.partial(_fnet_kernel, norm=1.0 / math.sqrt(seq_len * FNET_GC)),
        out_shape=jax.ShapeDtypeStruct((n_seq * seq_len, FNET_WIDTH), BF16),
        grid=(n_seq, nj),
        in_specs=[
            pl.BlockSpec((seq_len, FNET_WIDTH), lambda b, j: (sb0 + b, 0)),
            pl.BlockSpec((FNET_GC, 2 * FNET_GC), lambda b, j: (0, 0)),
            pl.BlockSpec((tr, seq_len), lambda b, j: (j, 0)),
            pl.BlockSpec((tr, seq_len), lambda b, j: (j, 0)),
        ],
        out_specs=pl.BlockSpec((tr, FNET_WIDTH), lambda b, j: (b * nj + j, 0)),
        scratch_shapes=[pltpu.VMEM((seq_len, FNET_WIDTH), BF16), pltpu.VMEM((seq_len, FNET_WIDTH), BF16)],
        compiler_params=_cparams("arbitrary", "arbitrary"),
        name="fourier_mix",
    )(f_in, tabs["chan"], tabs["cos"], tabs["sin"])


def _pool_kernel(p_ref, wg_ref, ps_ref, o_ref, pad_ref):
    seq_len = p_ref.shape[0]
    zeros = jnp.zeros((POOL_HALO, POOL_WIDTH), F32)
    pad_ref[0:POOL_HALO, :] = zeros
    pad_ref[POOL_HALO + seq_len:, :] = zeros
    pad_ref[POOL_HALO:POOL_HALO + seq_len, :] = p_ref[...]
    ch = min(POOL_CHUNK, seq_len)
    for c in range(seq_len // ch):
        t = lax.broadcasted_iota(jnp.int32, (ch, 1), 0) + c * ch
        for g, w in enumerate(POOL_WINDOWS):
            half = w // 2
            sl = slice(g * POOL_GC, (g + 1) * POOL_GC)
            acc = None
            for j in range(-half, half):
                r0 = POOL_HALO + c * ch + j
                part = pad_ref[r0:r0 + ch, sl]
                acc = part if acc is None else acc + part
            cnt = (jnp.minimum(t + half, seq_len) - jnp.maximum(t - half, 0)).astype(F32)
            pooled = acc / cnt - p_ref[c * ch:(c + 1) * ch, sl]
            mixed = _dot(pooled.astype(BF16), wg_ref[g]) * ps_ref[:, sl]
            o_ref[c * ch:(c + 1) * ch, sl] = mixed.astype(o_ref.dtype)


def _pooling(p_in, n_seq, seq_len, row0, l, wts):
    sb0 = row0 // seq_len
    g = len(POOL_WINDOWS)
    return pl.pallas_call(
        _pool_kernel,
        out_shape=jax.ShapeDtypeStruct((n_seq * seq_len, POOL_WIDTH), BF16),
        grid=(n_seq,),
        in_specs=[
            pl.BlockSpec((seq_len, POOL_WIDTH), lambda b: (sb0 + b, 0)),
            pl.BlockSpec((None, g, POOL_GC, POOL_GC), lambda b: (l, 0, 0, 0)),
            pl.BlockSpec((None, 1, POOL_WIDTH), lambda b: (l, 0, 0)),
        ],
        out_specs=pl.BlockSpec((seq_len, POOL_WIDTH), lambda b: (b, 0)),
        scratch_shapes=[pltpu.VMEM((seq_len + 2 * POOL_HALO, POOL_WIDTH), F32)],
        compiler_params=_cparams("arbitrary"),
        name="pool_mix",
    )(p_in, wts["w_pool_grp"], wts["pool_scale"])


def _merge_kernel(x_ref, mod_ref, a_ref, f_ref, p_ref, g_ref,
                  wa_ref, wf_ref, wp_ref, wo_ref, g2_ref, wr_ref, br_ref, tri_ref,
                  x1_out, hp_out, te_out, tw_out, rk_out, cnt_out, carry_ref):
    a_in = a_ref[...]
    f_in = f_ref[...]
    p_in = p_ref[...]
    chunks = []
    for c0 in range(0, D_MODEL, MERGE_COL_CHUNK):
        cs = slice(c0, c0 + MERGE_COL_CHUNK)
        a = _dot(a_in, wa_ref[:, cs])
        f = _dot(f_in, wf_ref[:, cs])
        p = _dot(p_in, wp_ref[:, cs])
        chunks.append((g_ref[:, c0:c0 + MERGE_COL_CHUNK].astype(F32) * a
                       + g_ref[:, D_MODEL + c0:D_MODEL + c0 + MERGE_COL_CHUNK].astype(F32) * f
                       + g_ref[:, 2 * D_MODEL + c0:2 * D_MODEL + c0 + MERGE_COL_CHUNK].astype(F32) * p).astype(BF16))
    mix = _dot(jnp.concatenate(chunks, axis=1), wo_ref[...])
    gate1 = mod_ref[2:3, :]
    shift2 = mod_ref[3:4, :]
    scale2 = mod_ref[4:5, :]
    x1 = x_ref[...] + gate1 * mix
    x1_out[...] = x1
    h2 = _rms(x1) * g2_ref[...] * (1.0 + scale2) + shift2

    hp_out[...] = _pack_bf16_pairs(h2)
    h_hi = h2.astype(BF16)

    h_lo = (h2 - h_hi.astype(F32)).astype(BF16)
    lhs = jnp.concatenate([h_hi, h_lo, h_hi], axis=1)
    logits = _dot(lhs, wr_ref[...]) + br_ref[...]

    lane = lax.broadcasted_iota(jnp.int32, logits.shape, 1)
    work = logits
    vals, idxs = [], []
    for _ in range(TOP_K):
        m = jnp.max(work, axis=-1, keepdims=True)
        idx = jnp.min(jnp.where(work == m, lane, ROUTER_PAD), axis=-1, keepdims=True)
        vals.append(m)
        idxs.append(idx)
        work = jnp.where(lane == idx, -jnp.inf, work)
    es = [jnp.exp(v - vals[0]) for v in vals]
    den = functools.reduce(jnp.add, es)

    @pl.when(pl.program_id(0) == 0)
    def _():
        carry_ref[...] = jnp.zeros(carry_ref.shape, F32)

    chosen = functools.reduce(jnp.logical_or, [lane == idx for idx in idxs])
    hot = jnp.where(chosen, 1.0, 0.0)
    before = _dot(tri_ref[...], hot.astype(BF16)) + carry_ref[...]
    carry_ref[...] = carry_ref[...] + jnp.sum(hot, axis=0, keepdims=True)
    cnt_out[...] = carry_ref[...].astype(jnp.int32)

    te = jnp.zeros(logits.shape, jnp.int32)
    tw = jnp.zeros(logits.shape, F32)
    rk = jnp.zeros(logits.shape, jnp.int32)
    for k in range(TOP_K):
        rank_k = jnp.sum(jnp.where(lane == idxs[k], before, 0.0), axis=-1, keepdims=True).astype(jnp.int32)
        te = jnp.where(lane == k, idxs[k], te)
        tw = jnp.where(lane == k, es[k] / den, tw)
        rk = jnp.where(lane == k, rank_k, rk)
    te_out[...] = te
    tw_out[...] = tw
    rk_out[...] = rk


def _merge(x, mods, attn, fnet, pool, gates, l, wts, consts, tok):
    t = x.shape[0]
    tm = TOKEN_TILE
    p_tiles = tok["t_prompt"] // tm
    s_tiles = tok["dec_seq"] // tm

    def mod_idx(i):
        row = jnp.where(i < p_tiles, 0, 1 + (i - p_tiles) // s_tiles)
        return (l, row, 0, 0)

    row = lambda w: pl.BlockSpec((tm, w), lambda i: (i, 0))
    out_shape = [
        jax.ShapeDtypeStruct((t, D_MODEL), F32),
        jax.ShapeDtypeStruct((t, D_MODEL // 2), jnp.int32),
        jax.ShapeDtypeStruct((t, ROUTER_PAD), jnp.int32),
        jax.ShapeDtypeStruct((t, ROUTER_PAD), F32),
        jax.ShapeDtypeStruct((t, ROUTER_PAD), jnp.int32),
    ]
    return pl.pallas_call(
        _merge_kernel,
        out_shape=out_shape + [jax.ShapeDtypeStruct((1, ROUTER_PAD), jnp.int32)],
        grid=(t // tm,),
        in_specs=[
            row(D_MODEL),
            pl.BlockSpec((None, None, 6, D_MODEL), mod_idx),
            row(V_WIDTH), row(FNET_WIDTH), row(POOL_WIDTH), row(N_BRANCHES * D_MODEL),
            _layer_spec((V_WIDTH, D_MODEL), l),
            _layer_spec((FNET_WIDTH, D_MODEL), l),
            _layer_spec((POOL_WIDTH, D_MODEL), l),
            _layer_spec((D_MODEL, D_MODEL), l),
            _layer_spec((1, D_MODEL), l),
            _layer_spec((3 * D_MODEL, ROUTER_PAD), l),
            _layer_spec((1, ROUTER_PAD), l),
            _const_spec((tm, tm)),
        ],
        out_specs=[row(s.shape[1]) for s in out_shape] + [pl.BlockSpec((1, ROUTER_PAD), lambda i: (0, 0))],
        scratch_shapes=[pltpu.VMEM((1, ROUTER_PAD), F32)],
        compiler_params=_cparams("arbitrary"),
        cost_estimate=pl.CostEstimate(
            flops=2 * t * (D_MODEL * (V_WIDTH + FNET_WIDTH + POOL_WIDTH + D_MODEL) + 3 * D_MODEL * ROUTER_PAD
                           + tm * ROUTER_PAD),
            transcendentals=t * TOP_K,
            bytes_accessed=t * (2 * (V_WIDTH + FNET_WIDTH + POOL_WIDTH + N_BRANCHES * D_MODEL) + 4 * 2 * D_MODEL
                                + 4 * (D_MODEL // 2) + 3 * 4 * ROUTER_PAD)
            + 2 * D_MODEL * (V_WIDTH + FNET_WIDTH + POOL_WIDTH + D_MODEL + 3 * ROUTER_PAD)),
        name="merge_router",
    )(x, mods, attn, fnet, pool, gates, wts["w_attn_o"], wts["w_fnet_o"], wts["w_pool_o"], wts["w_out"],
      wts["norm2_g"], wts["w_router3"], wts["b_router"], consts["tri"])


def _sc_row_gather(table, idx):
    n = idx.shape[0]
    d = table.shape[1]
    info = plsc.get_sparse_core_info()
    n_cores = info.num_cores
    n_workers = n_cores * info.num_subcores
    per_worker = n // n_workers
    n_chunks = per_worker // SC_GATHER_ROWS
    assert n_chunks * SC_GATHER_ROWS * n_workers == n
    mesh = plsc.VectorSubcoreMesh(core_axis_name="core", subcore_axis_name="subcore")

    @functools.partial(
        pl.kernel, mesh=mesh, out_type=jax.ShapeDtypeStruct((n, d), table.dtype),
        scratch_types=[pltpu.VMEM((SC_GATHER_ROWS,), jnp.int32), pltpu.VMEM((SC_GATHER_ROWS, d), table.dtype),
                       pltpu.SemaphoreType.DMA],
        cost_estimate=pl.CostEstimate(flops=0, transcendentals=0,
                                      bytes_accessed=2 * n * d * table.dtype.itemsize + 4 * n),
        name="sc_row_gather")
    def gather(table_hbm, idx_hbm, out_hbm, idx_v, rows_v, sem):
        worker = lax.axis_index("subcore") * n_cores + lax.axis_index("core")

        @pl.loop(0, n_chunks)
        def _(c):
            base = worker * per_worker + c * SC_GATHER_ROWS
            pltpu.sync_copy(idx_hbm.at[pl.ds(base, SC_GATHER_ROWS)], idx_v)
            pltpu.async_copy(table_hbm.at[idx_v], rows_v, sem).wait()
            pltpu.sync_copy(rows_v, out_hbm.at[pl.ds(base, SC_GATHER_ROWS)])

    return gather(table, idx)


PLAN_EXPERT, PLAN_VALID, PLAN_FIRST, PLAN_SLOT, PLAN_NEXT, PLAN_USED = range(6)


def _expert_kernel(plan_ref, xs_ref, bgu_ref, bd_ref, wgu_hbm, wd_hbm, y_ref,
                   wgu_f32, wd_f32, wgu_bf, wd_bf, sems, *, layer):
    b = pl.program_id(0)

    def weight_copies(expert, slot):
        return (pltpu.make_async_copy(wgu_hbm.at[layer, expert], wgu_f32.at[slot], sems.at[0, slot]),
                pltpu.make_async_copy(wd_hbm.at[layer, expert], wd_f32.at[slot], sems.at[1, slot]))

    def ffn(rows):
        x = jnp.concatenate(_unpack_bf16_pairs(xs_ref[0:rows, :]), axis=1).astype(BF16)
        gu = _dot(x, wgu_bf[...]) + bgu_ref[...]
        glu = jnp.minimum(gu[:, :D_FF], SWIGLU_LIMIT)
        lin = jnp.clip(gu[:, D_FF:], -SWIGLU_LIMIT, SWIGLU_LIMIT)
        act = glu * jax.nn.sigmoid(SWIGLU_ALPHA * glu) * (lin + 1.0)
        y_ref[0:rows, :] = _pack_bf16_pairs(_dot(act.astype(BF16), wd_bf[...]) + bd_ref[...])
        if rows < MOE_BLOCK:
            y_ref[rows:, :] = jnp.zeros((MOE_BLOCK - rows, D_MODEL // 2), y_ref.dtype)

    n_used = plan_ref[PLAN_USED, 0]

    @pl.when(b < n_used)
    def _():
        slot = plan_ref[PLAN_SLOT, b]

        @pl.when(b == 0)
        def _():
            for cp in weight_copies(plan_ref[PLAN_EXPERT, 0], 0):
                cp.start()

        @pl.when(plan_ref[PLAN_FIRST, b] == 1)
        def _():
            for cp in weight_copies(plan_ref[PLAN_EXPERT, b], slot):
                cp.wait()
            wgu_bf[...] = wgu_f32[slot].astype(BF16)
            wd_bf[...] = wd_f32[slot].astype(BF16)

            @pl.when(plan_ref[PLAN_NEXT, b] >= 0)
            def _():
                for cp in weight_copies(plan_ref[PLAN_NEXT, b], 1 - slot):
                    cp.start()

        steps = (plan_ref[PLAN_VALID, b] + (MOE_STEP - 1)) // MOE_STEP
        for rows in range(MOE_STEP, MOE_BLOCK + 1, MOE_STEP):
            want = steps <= 1 if rows == MOE_STEP else steps == rows // MOE_STEP
            pl.when(want)(functools.partial(ffn, rows))

    @pl.when(b >= n_used)
    def _():
        y_ref[...] = jnp.zeros(y_ref.shape, y_ref.dtype)


def _experts(xs, plan, l, w_gu, b_gu4, w_down, b_down4):
    n_slots = xs.shape[0]
    bm = MOE_BLOCK
    n_blocks = n_slots // bm

    def blk(b, plan):
        return jnp.minimum(b, plan[PLAN_USED, 0] - 1)

    def bspec(width):
        return pl.BlockSpec((None, None, 1, width), lambda b, plan: (l, plan[PLAN_EXPERT, blk(b, plan)], 0, 0))

    grid_spec = pltpu.PrefetchScalarGridSpec(
        num_scalar_prefetch=1,
        grid=(n_blocks,),
        in_specs=[
            pl.BlockSpec((bm, D_MODEL // 2), lambda b, plan: (blk(b, plan), 0)),
            bspec(2 * D_FF), bspec(D_MODEL),
            pl.BlockSpec(memory_space=pl.ANY), pl.BlockSpec(memory_space=pl.ANY),
        ],
        out_specs=pl.BlockSpec((bm, D_MODEL // 2), lambda b, plan: (b, 0)),
        scratch_shapes=[pltpu.VMEM((2, D_MODEL, 2 * D_FF), F32), pltpu.VMEM((2, D_FF, D_MODEL), F32),
                        pltpu.VMEM((D_MODEL, 2 * D_FF), BF16), pltpu.VMEM((D_FF, D_MODEL), BF16),
                        pltpu.SemaphoreType.DMA((2, 2))],
    )
    return pl.pallas_call(
        functools.partial(_expert_kernel, layer=l),
        out_shape=jax.ShapeDtypeStruct((n_slots, D_MODEL // 2), jnp.int32),
        grid_spec=grid_spec,
        compiler_params=_cparams("arbitrary"),
        cost_estimate=pl.CostEstimate(
            flops=2 * n_slots * 3 * D_MODEL * D_FF, transcendentals=n_slots * D_FF,
            bytes_accessed=4 * N_EXPERTS * 3 * D_MODEL * D_FF + 2 * 4 * n_slots * (D_MODEL // 2)),
        name="moe_experts",
    )(plan, xs, b_gu4, b_down4, w_gu, w_down)


def _combine_kernel(tw_ref, x_ref, mod_ref, yg_ref, o_ref):
    o_ref[...] = _moe_residual(tw_ref, x_ref, mod_ref, yg_ref)


def _combine(yg, tw, x1, mods, l, tok):
    t = x1.shape[0]
    tc = COMBINE_TILE
    p_tiles = tok["t_prompt"] // tc
    s_tiles = tok["dec_seq"] // tc

    def mod_idx(i):
        row = jnp.where(i < p_tiles, 0, 1 + (i - p_tiles) // s_tiles)
        return (l, row, 0, 0)

    return pl.pallas_call(
        _combine_kernel,
        out_shape=jax.ShapeDtypeStruct((t, D_MODEL), F32),
        grid=(t // tc,),
        in_specs=[
            pl.BlockSpec((tc, ROUTER_PAD), lambda i: (i, 0)),
            pl.BlockSpec((tc, D_MODEL), lambda i: (i, 0)),
            pl.BlockSpec((None, None, 6, D_MODEL), mod_idx),
            pl.BlockSpec((TOP_K, tc, D_MODEL // 2), lambda i: (0, i, 0)),
        ],
        out_specs=pl.BlockSpec((tc, D_MODEL), lambda i: (i, 0)),
        compiler_params=_cparams("arbitrary"),
        name="moe_combine",
    )(tw, x1, mods, yg)


def _slot_tokens(slot, counts, padded_start, padded_end, n_slots, t):
    n = slot.shape[0]
    j = jnp.arange(MOE_BLOCK, dtype=jnp.int32)
    pad_key = (padded_start + counts)[:, None] + j[None, :]
    pad_key = jnp.where(pad_key < padded_end[:, None], pad_key, n_slots).reshape(N_EXPERTS * MOE_BLOCK)
    pad_val = jnp.arange(N_EXPERTS * MOE_BLOCK, dtype=jnp.int32) % t
    keys = jnp.concatenate([slot, pad_key])
    vals = jnp.concatenate([jnp.arange(n, dtype=jnp.int32) // TOP_K, pad_val])
    tok_bits = max(1, (t - 1).bit_length())
    assert (n_slots + 1) << tok_bits < 2 ** 31
    packed = jnp.sort(lax.shift_left(keys, tok_bits) | vals)
    return packed[:n_slots] & ((1 << tok_bits) - 1)


def _route(te_pad, rank_pad, counts_pad, n_slots):
    t = te_pad.shape[0]
    flat_e = te_pad[:, :TOP_K].reshape(t * TOP_K)
    onehot = (flat_e[:, None] == jnp.arange(N_EXPERTS, dtype=jnp.int32)[None, :]).astype(jnp.int32)
    rank = rank_pad[:, :TOP_K].reshape(t * TOP_K)
    counts = counts_pad[0, :N_EXPERTS]
    padded = (counts + MOE_BLOCK - 1) // MOE_BLOCK * MOE_BLOCK
    padded_end = jnp.cumsum(padded)
    padded_start = padded_end - padded
    slot = jnp.sum(onehot * padded_start[None, :], axis=1) + rank
    n_blocks = n_slots // MOE_BLOCK
    block_row0 = jnp.arange(n_blocks, dtype=jnp.int32) * MOE_BLOCK
    block_exp = jnp.minimum(
        jnp.sum((padded_end[None, :] <= block_row0[:, None]).astype(jnp.int32), axis=1),
        N_EXPERTS - 1).astype(jnp.int32)
    exp_hot = (block_exp[:, None] == jnp.arange(N_EXPERTS, dtype=jnp.int32)[None, :]).astype(jnp.int32)
    rows_end = jnp.sum(exp_hot * (padded_start + counts)[None, :], axis=1)
    block_valid = jnp.clip(rows_end - block_row0, 0, MOE_BLOCK).astype(jnp.int32)
    n_used = (padded_end[-1] // MOE_BLOCK).astype(jnp.int32)
    block_ids = jnp.arange(n_blocks, dtype=jnp.int32)
    prev_exp = jnp.concatenate([jnp.full((1,), -1, jnp.int32), block_exp[:-1]])
    first = ((block_exp != prev_exp) & (block_ids < n_used)).astype(jnp.int32)
    buf_slot = (jnp.cumsum(first) - 1) % 2
    experts = jnp.arange(N_EXPERTS, dtype=jnp.int32)
    holder = jnp.where(counts > 0, experts, N_EXPERTS)
    later = lax.cummin(jnp.concatenate([holder[1:], jnp.full((1,), N_EXPERTS, jnp.int32)]), reverse=True)
    next_exp = jnp.where(later < N_EXPERTS, later, -1)
    block_next = jnp.sum(exp_hot * next_exp[None, :], axis=1)
    plan = jnp.stack([block_exp, block_valid, first, buf_slot, block_next,
                      jnp.full((n_blocks,), n_used, jnp.int32)]).astype(jnp.int32)
    slot = slot.astype(jnp.int32)
    slot_tok = _slot_tokens(slot, counts, padded_start, padded_end, n_slots, t)
    slot_by_choice = slot.reshape(t, TOP_K).T.reshape(TOP_K * t)
    return slot_by_choice, slot_tok, plan


def _segment_matrices():
    m_q = np.zeros((QK_PAD, LANES), np.float32)
    m_k = np.zeros((QK_PAD, LANES), np.float32)
    inv_cnt_q = np.ones((1, LANES), np.float32)
    for h in range(N_HEADS):
        m_q[h * HEAD_PAD:h * HEAD_PAD + QK_NOPE, 2 * h] = 1.0
        m_q[h * HEAD_PAD + QK_NOPE:h * HEAD_PAD + QK_HEAD, 2 * h + 1] = 1.0
        inv_cnt_q[0, 2 * h] = 1.0 / QK_NOPE
        inv_cnt_q[0, 2 * h + 1] = 1.0 / ROPE_DIM
        m_k[h * HEAD_PAD:h * HEAD_PAD + QK_NOPE, h] = 1.0
    dup = lambda m: np.concatenate([m.T, m.T], axis=0)
    return dict(m_q=jnp.asarray(m_q, BF16), mt2_q=jnp.asarray(dup(m_q), BF16), inv_cnt_q=jnp.asarray(inv_cnt_q),
                m_k=jnp.asarray(m_k, BF16), mt2_k=jnp.asarray(dup(m_k), BF16))


def _rope_lane_tables(n_tokens, ident_rows):
    rows = n_tokens // GRID_W
    row = jnp.broadcast_to(jnp.arange(rows)[:, None], (rows, GRID_W)).reshape(n_tokens)
    col = jnp.broadcast_to(jnp.arange(GRID_W)[None, :], (rows, GRID_W)).reshape(n_tokens)
    n_freq = ROPE_DIM // 4
    inv = 1.0 / (ROPE_THETA ** (jnp.arange(n_freq, dtype=F32) / n_freq))
    ang = jnp.concatenate([row[:, None].astype(F32) * inv, col[:, None].astype(F32) * inv], axis=-1)
    cos, sin = jnp.cos(ang), jnp.sin(ang)
    ones = jnp.ones((n_tokens, QK_NOPE), F32)
    tail = HEAD_PAD - QK_HEAD
    cosf = jnp.concatenate([ones, cos, cos, jnp.ones((n_tokens, tail), F32)], axis=1)
    sinf = jnp.concatenate([0.0 * ones, -sin, sin, jnp.zeros((n_tokens, tail), F32)], axis=1)
    cosf = jnp.concatenate([cosf, jnp.ones((ident_rows, HEAD_PAD), F32)], axis=0)
    sinf = jnp.concatenate([sinf, jnp.zeros((ident_rows, HEAD_PAD), F32)], axis=0)
    return cosf, sinf


def _dft_tables(seq_len):
    def cs(n):
        k = np.arange(n, dtype=np.int64)
        ang = 2.0 * np.pi * ((k[:, None] * k[None, :]) % n).astype(np.float64) / n
        return np.cos(ang).astype(np.float32), np.sin(ang).astype(np.float32)

    cl, sl = cs(seq_len)
    cc, sc = cs(FNET_GC)
    return dict(cos=jnp.asarray(cl).astype(BF16), sin=jnp.asarray(sl).astype(BF16),
                chan=jnp.asarray(np.concatenate([cc, sc], axis=1)).astype(BF16))


def _layout_weights(w_in, w_q_b, w_kv_b, q_nope_g, q_rope_g, k_nope_g, k_rope_g, w_router, b_router):
    depth = w_in.shape[0]
    w_in_p = jnp.concatenate(
        [w_in[:, :, :COL_KPE + ROPE_DIM], jnp.zeros((depth, D_MODEL, KPE_PAD - ROPE_DIM), w_in.dtype),
         w_in[:, :, COL_KPE + ROPE_DIM:]], axis=2).astype(BF16)
    w_q = jnp.pad(w_q_b.reshape(depth, Q_RANK, N_HEADS, QK_HEAD),
                  ((0, 0), (0, 0), (0, 0), (0, HEAD_PAD - QK_HEAD))).reshape(depth, Q_RANK, QK_PAD).astype(BF16)
    kv = w_kv_b.reshape(depth, KV_RANK, N_HEADS, QK_NOPE + V_HEAD)
    w_k = jnp.pad(kv[..., :QK_NOPE], ((0, 0), (0, 0), (0, 0), (0, HEAD_PAD - QK_NOPE))
                  ).reshape(depth, KV_RANK, QK_PAD).astype(BF16)
    w_v = kv[..., QK_NOPE:].reshape(depth, KV_RANK, V_WIDTH).astype(BF16)
    zq = jnp.zeros((depth, HEAD_PAD - QK_HEAD), F32)
    g_q = jnp.tile(jnp.concatenate([q_nope_g, q_rope_g, zq], axis=1), (1, N_HEADS))[:, None, :]
    g_k = jnp.tile(jnp.concatenate([k_nope_g, jnp.zeros((depth, HEAD_PAD - QK_NOPE), F32)], axis=1),
                   (1, N_HEADS))[:, None, :]
    g_kpe = jnp.concatenate([jnp.zeros((depth, QK_NOPE), F32), k_rope_g, zq], axis=1)[:, None, :]
    w_hi = w_router.astype(BF16)
    w_lo = (w_router - w_hi.astype(F32)).astype(BF16)
    w_r3 = jnp.pad(jnp.concatenate([w_hi, w_hi, w_lo], axis=1), ((0, 0), (0, 0), (0, ROUTER_PAD - N_EXPERTS)))
    b_r = jnp.pad(b_router, ((0, 0), (0, ROUTER_PAD - N_EXPERTS)), constant_values=NEG_BIG)[:, None, :]
    return dict(w_in=w_in_p, w_q=w_q, w_k=w_k, w_v=w_v, g_q=g_q, g_k=g_k, g_kpe=g_kpe, w_router3=w_r3, b_router=b_r)


def kernel(x_prompt, x_sample, cache_ckv, cache_kpe, c, c_ctx, w_mod, b_mod, norm1_g, norm2_g, w_in, q_a_g, kv_a_g, w_q_b, w_kv_b, q_nope_g, q_rope_g, k_nope_g, k_rope_g, w_attn_o, w_fnet_o, w_pool_grp, pool_scale, w_pool_o, w_out, w_router, b_router, w_gu, b_gu, w_down, b_down):
    batch, seq, d = x_prompt.shape
    dec_batch, dec_seq, _ = x_sample.shape
    depth = w_mod.shape[0]
    past = cache_ckv.shape[2]
    t_prompt = batch * seq
    t_sample = dec_batch * dec_seq
    assert d == D_MODEL and dec_batch + 1 <= 8
    assert t_prompt % TOKEN_TILE == 0 and dec_seq % TOKEN_TILE == 0 and seq % COMBINE_TILE == 0
    assert dec_seq % GRID_W == 0

    consts = _segment_matrices()
    consts["cosf"], consts["sinf"] = _rope_lane_tables(dec_seq, IN_TILE)
    consts["tri"] = jnp.asarray(np.tril(np.ones((TOKEN_TILE, TOKEN_TILE), np.float32), -1), BF16)

    wts = _layout_weights(w_in, w_q_b, w_kv_b, q_nope_g, q_rope_g, k_nope_g, k_rope_g, w_router, b_router)
    row3 = lambda a: a[:, None, :]
    wts.update(norm1_g=row3(norm1_g), norm2_g=row3(norm2_g), q_a_g=row3(q_a_g), kv_a_g=row3(kv_a_g),
               w_attn_o=w_attn_o.astype(BF16), w_fnet_o=w_fnet_o.astype(BF16), w_pool_o=w_pool_o.astype(BF16),
               w_out=w_out.astype(BF16), w_pool_grp=w_pool_grp.astype(BF16), pool_scale=row3(pool_scale))
    b_gu4 = b_gu[:, :, None, :]
    b_down4 = b_down[:, :, None, :]

    cond8 = jnp.concatenate([c_ctx[None, :], c, jnp.zeros((8 - 1 - dec_batch, d), F32)], axis=0)
    mods = _modulation(cond8, w_mod, b_mod).reshape(depth, 8, 6, D_MODEL)

    kpe_pad = jnp.pad(cache_kpe, ((0, 0), (0, 0), (0, 0), (QK_NOPE, HEAD_PAD - QK_HEAD)))
    k_ctx, v_ctx = _ctx_keys(cache_ckv, kpe_pad, wts, consts)

    streams = [
        dict(x=x_prompt.reshape(t_prompt, d), n_seq=batch, seq_len=seq, tok=dict(t_prompt=t_prompt, dec_seq=dec_seq),
             dft=_dft_tables(seq), ctx=None, moe=None),
        dict(x=x_sample.reshape(t_sample, d), n_seq=dec_batch, seq_len=dec_seq, tok=dict(t_prompt=0, dec_seq=dec_seq),
             dft=_dft_tables(dec_seq), ctx=(k_ctx, v_ctx, past), moe=None),
    ]
    ckv_list, kpe_list = [], []

    def mixers(st, l):
        (q, k, v, ckv, kpe, f_in, p_in, st["gates"]), st["x"] = _in_projection(
            st["x"], st["moe"], mods, l, wts, consts, st["tok"])
        if st["ctx"] is None:
            ckv_list.append(ckv.reshape(batch, seq, KV_RANK))
            kpe_list.append(kpe.reshape(batch, seq, ROPE_DIM))
        parts = [(k, v, None, st["seq_len"])]
        if st["ctx"] is not None:
            parts.append((st["ctx"][0], st["ctx"][1], l, st["ctx"][2]))
        st["attn"] = _attention(q, parts, st["n_seq"], st["seq_len"])
        st["fnet"] = _fourier(f_in, st["n_seq"], st["seq_len"], 0, st["dft"])
        st["pool"] = _pooling(p_in, st["n_seq"], st["seq_len"], 0, l, wts)

    def merge_and_route(st, l):
        t = st["n_seq"] * st["seq_len"]
        n_slots = (t * TOP_K + N_EXPERTS * (MOE_BLOCK - 1) + MOE_BLOCK - 1) // MOE_BLOCK * MOE_BLOCK
        st["x1"], st["h2p"], te_pad, st["tw"], rank_pad, counts_pad = _merge(
            st["x"], mods, st["attn"], st["fnet"], st["pool"], st["gates"], l, wts, consts, st["tok"])
        st["slot_by_choice"], st["slot_tok"], st["plan"] = _route(te_pad, rank_pad, counts_pad, n_slots)

    def gather_back(st):
        t = st["n_seq"] * st["seq_len"]
        yg = _sc_row_gather(st["y"], st["slot_by_choice"]).reshape(TOP_K, t, D_MODEL // 2)
        st["moe"] = (st["x1"], st["tw"], yg)
        st["x"] = None

    a, b = streams
    for l in range(depth):
        mixers(a, l)
        if b["moe"] is not None:
            x1, tw, yg = b["moe"]
            yg, a["attn"] = lax.optimization_barrier((yg, a["attn"]))
            b["moe"] = (x1, tw, yg)
        mixers(b, l)
        merge_and_route(a, l)
        b["x"], a["slot_tok"] = lax.optimization_barrier((b["x"], a["slot_tok"]))
        a["xs"] = _sc_row_gather(a["h2p"], a["slot_tok"])
        merge_and_route(b, l)
        a["xs"], b["slot_tok"] = lax.optimization_barrier((a["xs"], b["slot_tok"]))
        b["xs"] = _sc_row_gather(b["h2p"], b["slot_tok"])
        a["y"] = _experts(a["xs"], a["plan"], l, w_gu, b_gu4, w_down, b_down4)
        gather_back(a)
        b["y"] = _experts(b["xs"], b["plan"], l, w_gu, b_gu4, w_down, b_down4)
        x1, tw, yg = a["moe"]
        yg, b["y"] = lax.optimization_barrier((yg, b["y"]))
        a["moe"] = (x1, tw, yg)
        gather_back(b)
    x1, tw, yg = a["moe"]
    out_a = _combine(yg, tw, x1, mods, depth - 1, a["tok"])
    x1, tw, yg = b["moe"]
    yg, out_a = lax.optimization_barrier((yg, out_a))
    out_b = _combine(yg, tw, x1, mods, depth - 1, b["tok"])
    y_prompt = out_a.reshape(batch, seq, d)
    y_sample = out_b.reshape(dec_batch, dec_seq, d)
    return (y_prompt, y_sample, jnp.stack(ckv_list, axis=1), jnp.stack(kpe_list, axis=1))
```

```python
import functools
import math

import numpy as np
import jax
import jax.numpy as jnp
from jax import lax
from jax.experimental import pallas as pl
from jax.experimental.pallas import tpu as pltpu
from jax.experimental.pallas import tpu_sc as plsc

D_MODEL = 1024
GRID_W = 64
N_HEADS = 8
QK_NOPE = 64
ROPE_DIM = 32
V_HEAD = 64
QK_HEAD = QK_NOPE + ROPE_DIM
Q_RANK = 384
KV_RANK = 256
ROPE_THETA = 10000.0
FNET_GROUPS = 4
FNET_GC = 128
FNET_WIDTH = FNET_GROUPS * FNET_GC
POOL_WINDOWS = (2, 4, 8, 16)
POOL_GC = 128
POOL_WIDTH = len(POOL_WINDOWS) * POOL_GC
N_BRANCHES = 3
N_EXPERTS = 32
TOP_K = 4
D_FF = D_MODEL
SWIGLU_LIMIT = 7.0
SWIGLU_ALPHA = 1.702
RMS_EPS = 1e-6

LANES = 128
HEAD_PAD = LANES
QK_PAD = N_HEADS * HEAD_PAD
V_WIDTH = N_HEADS * V_HEAD
ROPE_HALF = ROPE_DIM // 2
KPE_PAD = LANES
COL_KV = Q_RANK
COL_KPE = Q_RANK + KV_RANK
COL_F = COL_KPE + KPE_PAD
COL_P = COL_F + FNET_WIDTH
COL_G = COL_P + POOL_WIDTH
IN_PAD_WIDTH = COL_G + N_BRANCHES * D_MODEL
MIX_WIDTH = FNET_WIDTH + POOL_WIDTH
ROUTER_PAD = LANES
NEG_BIG = -1e30

TOKEN_TILE = 512
IN_TILE = 256
ATTN_Q_TILE = 512
FNET_ROW_TILE = 512
POOL_CHUNK = 256
POOL_HALO = 16
MOE_BLOCK = 512
MOE_STEP = 128
GATE_COL_CHUNK = 512
MERGE_COL_CHUNK = 256
FF_CHUNK = 256
COMBINE_TILE = 256
SC_GATHER_ROWS = 64
MOD_COL_TILE = 1536
VMEM_LIMIT = 56 * 1024 * 1024

F32 = jnp.float32
BF16 = jnp.bfloat16


def _cparams(*sem):
    return pltpu.CompilerParams(dimension_semantics=sem, vmem_limit_bytes=VMEM_LIMIT)


def _dot(a, b):
    return jnp.dot(a, b, preferred_element_type=F32)


def _rms(x):
    return x * lax.rsqrt(jnp.mean(x * x, axis=-1, keepdims=True) + RMS_EPS)


def _pack_bf16_pairs(x):
    half = x.shape[1] // 2
    bits = pltpu.bitcast(x.astype(BF16).astype(F32), jnp.int32)
    return bits[:, :half] | lax.shift_right_logical(bits[:, half:], 16)


def _unpack_bf16_pairs(w):
    return pltpu.bitcast(w & jnp.int32(-65536), F32), pltpu.bitcast(lax.shift_left(w, 16), F32)


def _seg_sums(x, m_ref):
    return _dot((x * x).astype(BF16), m_ref[...])


def _seg_inv_rms(ss, inv_cnt, mt2_ref):
    r = lax.rsqrt(ss * inv_cnt + RMS_EPS)
    r_hi = r.astype(BF16)
    r_lo = (r - r_hi.astype(F32)).astype(BF16)
    return _dot(jnp.concatenate([r_hi, r_lo], axis=1), mt2_ref[...])


def _seg_rms_scale(x, m_ref, mt2_ref, inv_cnt):
    return _seg_inv_rms(_seg_sums(x, m_ref), inv_cnt, mt2_ref)


def _rope_key_chunk(kpe_chunk, gkpe_ref, rope):
    ms = jnp.sum(kpe_chunk * kpe_chunk, axis=-1, keepdims=True) * (1.0 / ROPE_DIM)
    kp = kpe_chunk * lax.rsqrt(ms + RMS_EPS) * gkpe_ref[...]
    return kp if rope is None else _rope_chunk(kp, *rope)


def _rope_chunk(xc, cosf, sinf, first_half):
    partner = jnp.where(first_half, pltpu.roll(xc, HEAD_PAD - ROPE_HALF, 1), pltpu.roll(xc, ROPE_HALF, 1))
    return xc * cosf + partner * sinf


def _write_keys(ckv, kpe_chunk, wk_ref, wv_ref, gk_ref, gkpe_ref, mk_ref, mkt2_ref, rope, k_out, v_out):
    cb = ckv.astype(BF16)
    kn = _dot(cb, wk_ref[...])
    kn = kn * _seg_rms_scale(kn, mk_ref, mkt2_ref, 1.0 / QK_NOPE) * gk_ref[...]
    v_out[...] = _dot(cb, wv_ref[...]).astype(v_out.dtype)
    kp = _rope_key_chunk(kpe_chunk, gkpe_ref, rope)
    for h in range(N_HEADS):
        sl = slice(h * HEAD_PAD, (h + 1) * HEAD_PAD)
        k_out[:, sl] = (kn[:, sl] + kp).astype(k_out.dtype)


def _mod_kernel(c_ref, w_ref, b_ref, o_ref):
    c = c_ref[...]
    s = c * jax.nn.sigmoid(c)
    o_ref[...] = _dot(s.astype(BF16), w_ref[...].astype(BF16)) + b_ref[...]


def _modulation(cond8, w_mod, b_mod):
    depth = w_mod.shape[0]
    n = w_mod.shape[2]
    return pl.pallas_call(
        _mod_kernel,
        out_shape=jax.ShapeDtypeStruct((depth, 8, n), F32),
        grid=(depth, n // MOD_COL_TILE),
        in_specs=[
            pl.BlockSpec((8, D_MODEL), lambda l, j: (0, 0)),
            pl.BlockSpec((None, D_MODEL, MOD_COL_TILE), lambda l, j: (l, 0, j)),
            pl.BlockSpec((None, 1, MOD_COL_TILE), lambda l, j: (l, 0, j)),
        ],
        out_specs=pl.BlockSpec((None, 8, MOD_COL_TILE), lambda l, j: (l, 0, j)),
        compiler_params=_cparams("arbitrary", "arbitrary"),
        name="adaln_modulation",
    )(cond8, w_mod, b_mod.reshape(depth, 1, n))


def _moe_residual(tw_ref, x1_ref, mod_ref, yg_ref):
    acc = None
    for k in range(TOP_K):
        w = tw_ref[:, k:k + 1]
        part = [w * half for half in _unpack_bf16_pairs(yg_ref[k])]
        acc = part if acc is None else [a + p for a, p in zip(acc, part)]
    return x1_ref[...] + mod_ref[5:6, :] * jnp.concatenate(acc, axis=1)


def _in_kernel(*refs, after_moe):
    if after_moe:
        x = _moe_residual(*refs[:4])
        refs = refs[4:]
        refs[-1][...] = x
        refs = refs[:-1]
    else:
        x = refs[0][...]
        refs = refs[1:]
    (mod_ref, g1_ref, wa_ref, wb_ref, qag_ref, kvag_ref, wq_ref, wk_ref, wv_ref,
     gq_ref, gk_ref, gkpe_ref, cos_ref, sin_ref, mq_ref, mqt2_ref, icq_ref, mk_ref, mkt2_ref,
     q_out, k_out, v_out, ckv_out, kpe_out, f_out, p_out, g_out) = refs
    shift = mod_ref[0:1, :]
    scale = mod_ref[1:2, :]
    h = _rms(x) * g1_ref[...] * (1.0 + scale) + shift
    hb = h.astype(BF16)

    gate_chunks = iter(range(0, N_BRANCHES * D_MODEL, GATE_COL_CHUNK))

    def gate_chunk():
        c0 = next(gate_chunks)
        zc = _dot(hb, wb_ref[:, MIX_WIDTH + c0:MIX_WIDTH + c0 + GATE_COL_CHUNK])
        g_out[:, c0:c0 + GATE_COL_CHUNK] = jax.nn.sigmoid(zc).astype(g_out.dtype)

    za = _dot(hb, wa_ref[...])
    zb = _dot(hb, wb_ref[:, 0:MIX_WIDTH])
    f_out[...] = zb[:, 0:FNET_WIDTH].astype(f_out.dtype)
    p_out[...] = zb[:, FNET_WIDTH:]
    q_lat = za[:, 0:COL_KV]
    kv_lat = za[:, COL_KV:COL_KPE]
    key_lane = lax.broadcasted_iota(jnp.int32, (x.shape[0], KPE_PAD), 1) < ROPE_DIM
    kpe_grp = jnp.where(key_lane, za[:, COL_KPE:COL_F], 0.0)
    ckv = _rms(kv_lat) * kvag_ref[...]
    ckv_out[...] = ckv
    kpe_out[...] = kpe_grp[:, 0:ROPE_DIM]
    qn = (_rms(q_lat) * qag_ref[...]).astype(BF16)
    cb = ckv.astype(BF16)
    gate_chunk()

    q = _dot(qn, wq_ref[...])
    kn = _dot(cb, wk_ref[...])
    v_out[...] = _dot(cb, wv_ref[...]).astype(v_out.dtype)
    gate_chunk()

    ss_q = _seg_sums(q, mq_ref)
    ss_k = _seg_sums(kn, mk_ref)
    gate_chunk()

    q = q * _seg_inv_rms(ss_q, icq_ref[...], mqt2_ref) * gq_ref[...]
    kn = kn * _seg_inv_rms(ss_k, 1.0 / QK_NOPE, mkt2_ref) * gk_ref[...]
    for _ in range(N_BRANCHES * D_MODEL // GATE_COL_CHUNK - 3):
        gate_chunk()

    cosf = cos_ref[...]
    sinf = sin_ref[...]
    lane = lax.broadcasted_iota(jnp.int32, cosf.shape, 1)
    rope = (cosf, sinf, lane < QK_NOPE + ROPE_HALF)
    sm_scale = 1.0 / math.sqrt(QK_HEAD)
    kp = _rope_key_chunk(pltpu.roll(kpe_grp, QK_NOPE, 1), gkpe_ref, rope)
    for hd in range(N_HEADS):
        sl = slice(hd * HEAD_PAD, (hd + 1) * HEAD_PAD)
        q_out[:, sl] = (_rope_chunk(q[:, sl], *rope) * sm_scale).astype(q_out.dtype)
        k_out[:, sl] = (kn[:, sl] + kp).astype(k_out.dtype)


def _const_spec(shape):
    nd = len(shape)
    return pl.BlockSpec(shape, lambda i, _n=nd: (0,) * _n, pipeline_mode=pl.Buffered(1))


def _layer_spec(shape, l):
    nd = len(shape)
    return pl.BlockSpec((None,) + tuple(shape), lambda i, _l=l, _n=nd: (_l,) + (0,) * _n,
                        pipeline_mode=pl.Buffered(1))


def _in_projection(x, moe, mods, l, wts, consts, tok):
    after_moe = x is None
    t = moe[0].shape[0] if after_moe else x.shape[0]
    tm = IN_TILE
    n_tiles = t // tm
    p_tiles = tok["t_prompt"] // tm
    s_tiles = tok["dec_seq"] // tm
    rope_blocks = tok["dec_seq"] // tm

    def mod_row(i):
        return jnp.where(i < p_tiles, 0, 1 + (i - p_tiles) // s_tiles)

    def rope_idx(i):
        return (jnp.where(i < p_tiles, rope_blocks, (i - p_tiles) % s_tiles), 0)

    row = lambda w: pl.BlockSpec((tm, w), lambda i: (i, 0))
    if after_moe:
        x1, tw, yg = moe
        lead_specs = [row(ROUTER_PAD), row(D_MODEL),
                      pl.BlockSpec((None, None, 6, D_MODEL), lambda i: (l - 1, mod_row(i), 0, 0)),
                      pl.BlockSpec((TOP_K, tm, D_MODEL // 2), lambda i: (0, i, 0))]
        lead_args = [tw, x1, mods, yg]
    else:
        lead_specs = [row(D_MODEL)]
        lead_args = [x]
    in_specs = lead_specs + [
        pl.BlockSpec((None, None, 6, D_MODEL), lambda i: (l, mod_row(i), 0, 0)),
        _layer_spec((1, D_MODEL), l),
        _layer_spec((D_MODEL, COL_F), l),
        _layer_spec((D_MODEL, MIX_WIDTH + N_BRANCHES * D_MODEL), l),
        _layer_spec((1, Q_RANK), l),
        _layer_spec((1, KV_RANK), l),
        _layer_spec((Q_RANK, QK_PAD), l),
        _layer_spec((KV_RANK, QK_PAD), l),
        _layer_spec((KV_RANK, V_WIDTH), l),
        _layer_spec((1, QK_PAD), l),
        _layer_spec((1, QK_PAD), l),
        _layer_spec((1, HEAD_PAD), l),
        pl.BlockSpec((tm, HEAD_PAD), rope_idx),
        pl.BlockSpec((tm, HEAD_PAD), rope_idx),
        _const_spec((QK_PAD, LANES)),
        _const_spec((2 * LANES, QK_PAD)),
        _const_spec((1, LANES)),
        _const_spec((QK_PAD, LANES)),
        _const_spec((2 * LANES, QK_PAD)),
    ]
    out_shape = [
        jax.ShapeDtypeStruct((t, QK_PAD), BF16),
        jax.ShapeDtypeStruct((t, QK_PAD), BF16),
        jax.ShapeDtypeStruct((t, V_WIDTH), BF16),
        jax.ShapeDtypeStruct((t, KV_RANK), F32),
        jax.ShapeDtypeStruct((t, ROPE_DIM), F32),
        jax.ShapeDtypeStruct((t, FNET_WIDTH), BF16),
        jax.ShapeDtypeStruct((t, POOL_WIDTH), F32),
        jax.ShapeDtypeStruct((t, N_BRANCHES * D_MODEL), BF16),
    ]
    if after_moe:
        out_shape.append(jax.ShapeDtypeStruct((t, D_MODEL), F32))
    out_specs = [row(s.shape[1]) for s in out_shape]
    outs = pl.pallas_call(
        functools.partial(_in_kernel, after_moe=after_moe),
        out_shape=out_shape,
        grid=(n_tiles,),
        in_specs=in_specs,
        out_specs=out_specs,
        compiler_params=_cparams("arbitrary"),
        cost_estimate=pl.CostEstimate(
            flops=2 * t * (D_MODEL * IN_PAD_WIDTH + Q_RANK * QK_PAD + KV_RANK * (QK_PAD + V_WIDTH)
                           + 2 * (QK_PAD * LANES + 2 * LANES * QK_PAD)),
            transcendentals=t * N_BRANCHES * D_MODEL,
            bytes_accessed=2 * D_MODEL * IN_PAD_WIDTH + sum(
                math.prod(s.shape) * s.dtype.itemsize for s in out_shape)
            + 4 * t * D_MODEL * (1 + (TOP_K // 2 if after_moe else 0))),
        name="in_projection",
    )(*lead_args, mods, wts["norm1_g"], wts["w_in_a"], wts["w_in_b"], wts["q_a_g"], wts["kv_a_g"], wts["w_q"], wts["w_k"], wts["w_v"],
      wts["g_q"], wts["g_k"], wts["g_kpe"], consts["cosf"], consts["sinf"],
      consts["m_q"], consts["mt2_q"], consts["inv_cnt_q"], consts["m_k"], consts["mt2_k"])
    return (outs[:8], outs[8]) if after_moe else (outs, x)


def _ctx_keys_kernel(ckv_ref, kpe_ref, wk_ref, wv_ref, gk_ref, gkpe_ref, mk_ref, mkt2_ref, k_out, v_out):
    _write_keys(ckv_ref[...], kpe_ref[...], wk_ref, wv_ref, gk_ref, gkpe_ref, mk_ref, mkt2_ref, None, k_out, v_out)


def _ctx_keys(cache_ckv, cache_kpe_pad, wts, consts):
    nb, depth, past, _ = cache_ckv.shape
    lw = lambda shape: pl.BlockSpec((None,) + shape, lambda l, b: (l,) + (0,) * len(shape))
    cs = lambda shape: pl.BlockSpec(shape, lambda l, b: (0,) * len(shape))
    return pl.pallas_call(
        _ctx_keys_kernel,
        out_shape=[jax.ShapeDtypeStruct((depth, nb * past, QK_PAD), BF16),
                   jax.ShapeDtypeStruct((depth, nb * past, V_WIDTH), BF16)],
        grid=(depth, nb),
        in_specs=[
            pl.BlockSpec((None, None, past, KV_RANK), lambda l, b: (b, l, 0, 0)),
            pl.BlockSpec((None, None, past, HEAD_PAD), lambda l, b: (b, l, 0, 0)),
            lw((KV_RANK, QK_PAD)), lw((KV_RANK, V_WIDTH)), lw((1, QK_PAD)), lw((1, HEAD_PAD)),
            cs((QK_PAD, LANES)), cs((2 * LANES, QK_PAD)),
        ],
        out_specs=[pl.BlockSpec((None, past, QK_PAD), lambda l, b: (l, b, 0)),
                   pl.BlockSpec((None, past, V_WIDTH), lambda l, b: (l, b, 0))],
        compiler_params=_cparams("arbitrary", "arbitrary"),
        name="context_keys",
    )(cache_ckv, cache_kpe_pad, wts["w_k"], wts["w_v"], wts["g_k"], wts["g_kpe"], consts["m_k"], consts["mt2_k"])


def _attn_kernel(*refs, n_parts):
    q_ref = refs[0]
    k_refs = refs[1:1 + 2 * n_parts:2]
    v_refs = refs[2:2 + 2 * n_parts:2]
    o_ref = refs[1 + 2 * n_parts]
    lane = lax.broadcasted_iota(jnp.int32, (q_ref.shape[0], 2 * V_HEAD), 1)
    for pair in range(N_HEADS // 2):
        vsl = slice(pair * 2 * V_HEAD, (pair + 1) * 2 * V_HEAD)
        outs = []
        for hd in (2 * pair, 2 * pair + 1):
            sl = slice(hd * HEAD_PAD, (hd + 1) * HEAD_PAD)
            qh = q_ref[:, sl]
            ss = [lax.dot_general(qh, k[:, sl], (((1,), (1,)), ((), ())), preferred_element_type=F32)
                  for k in k_refs]
            m = functools.reduce(jnp.maximum, [jnp.max(s, axis=-1, keepdims=True) for s in ss])
            es = [jnp.exp(s - m) for s in ss]
            den = functools.reduce(jnp.add, [jnp.sum(e, axis=-1, keepdims=True) for e in es])
            acc = functools.reduce(jnp.add, [_dot(e.astype(BF16), v[:, vsl]) for e, v in zip(es, v_refs)])
            outs.append(acc / den)
        o_ref[:, vsl] = jnp.where(lane < V_HEAD, outs[0], outs[1]).astype(o_ref.dtype)


def _attention(q, kv_parts, n_seq, seq_len):
    tq = min(ATTN_Q_TILE, seq_len)
    nq = seq_len // tq
    n_keys = sum(rows for _, _, _, rows in kv_parts)
    in_specs = [pl.BlockSpec((tq, QK_PAD), lambda b, i: (b * nq + i, 0))]
    args = [q]
    for k, v, layer, rows in kv_parts:
        for arr, width in ((k, QK_PAD), (v, V_WIDTH)):
            if layer is None:
                in_specs.append(pl.BlockSpec((rows, width), lambda b, i: (b, 0)))
            else:
                in_specs.append(pl.BlockSpec((None, rows, width), lambda b, i, _l=layer: (_l, b, 0)))
            args.append(arr)
    return pl.pallas_call(
        functools.partial(_attn_kernel, n_parts=len(kv_parts)),
        out_shape=jax.ShapeDtypeStruct((n_seq * seq_len, V_WIDTH), BF16),
        grid=(n_seq, nq),
        in_specs=in_specs,
        out_specs=pl.BlockSpec((tq, V_WIDTH), lambda b, i: (b * nq + i, 0)),
        compiler_params=_cparams("arbitrary", "arbitrary"),
        cost_estimate=pl.CostEstimate(
            flops=2 * n_seq * seq_len * n_keys * N_HEADS * (HEAD_PAD + 2 * V_HEAD),
            transcendentals=n_seq * seq_len * n_keys * N_HEADS,
            bytes_accessed=2 * n_seq * (seq_len * (QK_PAD + V_WIDTH) + n_keys * (QK_PAD + V_WIDTH))),
        name="attention",
    )(*args)


def _fnet_kernel(f_ref, cs_ref, cl_ref, sl_ref, o_ref, top_ref, bot_ref, *, norm):
    @pl.when(pl.program_id(1) == 0)
    def _():
        for g in range(FNET_GROUPS):
            sl = slice(g * FNET_GC, (g + 1) * FNET_GC)
            a = _dot(f_ref[:, sl], cs_ref[...])
            top_ref[:, sl] = a[:, :FNET_GC].astype(BF16)
            bot_ref[:, sl] = a[:, FNET_GC:].astype(BF16)

    y = _dot(cl_ref[...], top_ref[...]) - _dot(sl_ref[...], bot_ref[...])
    o_ref[...] = (y * norm).astype(o_ref.dtype)


def _fourier(f_in, n_seq, seq_len, row0, tabs):
    tr = min(FNET_ROW_TILE, seq_len)
    nj = seq_len // tr
    sb0 = row0 // seq_len
    return pl.pallas_call(
        functools.partial(_fnet_kernel, norm=1.0 / math.sqrt(seq_len * FNET_GC)),
        out_shape=jax.ShapeDtypeStruct((n_seq * seq_len, FNET_WIDTH), BF16),
        grid=(n_seq, nj),
        in_specs=[
            pl.BlockSpec((seq_len, FNET_WIDTH), lambda b, j: (sb0 + b, 0)),
            pl.BlockSpec((FNET_GC, 2 * FNET_GC), lambda b, j: (0, 0)),
            pl.BlockSpec((tr, seq_len), lambda b, j: (j, 0)),
            pl.BlockSpec((tr, seq_len), lambda b, j: (j, 0)),
        ],
        out_specs=pl.BlockSpec((tr, FNET_WIDTH), lambda b, j: (b * nj + j, 0)),
        scratch_shapes=[pltpu.VMEM((seq_len, FNET_WIDTH), BF16), pltpu.VMEM((seq_len, FNET_WIDTH), BF16)],
        compiler_params=_cparams("arbitrary", "arbitrary"),
        name="fourier_mix",
    )(f_in, tabs["chan"], tabs["cos"], tabs["sin"])


def _pool_kernel(p_ref, wg_ref, ps_ref, o_ref, pad_ref):
    seq_len = p_ref.shape[0]
    zeros = jnp.zeros((POOL_HALO, POOL_WIDTH), F32)
    pad_ref[0:POOL_HALO, :] = zeros
    pad_ref[POOL_HALO + seq_len:, :] = zeros
    pad_ref[POOL_HALO:POOL_HALO + seq_len, :] = p_ref[...]
    ch = min(POOL_CHUNK, seq_len)
    for c in range(seq_len // ch):
        t = lax.broadcasted_iota(jnp.int32, (ch, 1), 0) + c * ch
        for g, w in enumerate(POOL_WINDOWS):
            half = w // 2
            sl = slice(g * POOL_GC, (g + 1) * POOL_GC)
            acc = None
            for j in range(-half, half):
                r0 = POOL_HALO + c * ch + j
                part = pad_ref[r0:r0 + ch, sl]
                acc = part if acc is None else acc + part
            cnt = (jnp.minimum(t + half, seq_len) - jnp.maximum(t - half, 0)).astype(F32)
            pooled = acc / cnt - p_ref[c * ch:(c + 1) * ch, sl]
            mixed = _dot(pooled.astype(BF16), wg_ref[g]) * ps_ref[:, sl]
            o_ref[c * ch:(c + 1) * ch, sl] = mixed.astype(o_ref.dtype)


def _pooling(p_in, n_seq, seq_len, row0, l, wts):
    sb0 = row0 // seq_len
    g = len(POOL_WINDOWS)
    return pl.pallas_call(
        _pool_kernel,
        out_shape=jax.ShapeDtypeStruct((n_seq * seq_len, POOL_WIDTH), BF16),
        grid=(n_seq,),
        in_specs=[
            pl.BlockSpec((seq_len, POOL_WIDTH), lambda b: (sb0 + b, 0)),
            pl.BlockSpec((None, g, POOL_GC, POOL_GC), lambda b: (l, 0, 0, 0)),
            pl.BlockSpec((None, 1, POOL_WIDTH), lambda b: (l, 0, 0)),
        ],
        out_specs=pl.BlockSpec((seq_len, POOL_WIDTH), lambda b: (b, 0)),
        scratch_shapes=[pltpu.VMEM((seq_len + 2 * POOL_HALO, POOL_WIDTH), F32)],
        compiler_params=_cparams("arbitrary"),
        name="pool_mix",
    )(p_in, wts["w_pool_grp"], wts["pool_scale"])


def _merge_kernel(x_ref, mod_ref, a_ref, f_ref, p_ref, g_ref,
                  wa_ref, wf_ref, wp_ref, wo_ref, g2_ref, wr_ref, br_ref, tri_ref,
                  x1_out, hp_out, te_out, tw_out, rk_out, cnt_out, mix_scr, carry_ref):
    j = pl.program_id(0)

    @pl.when(j == 0)
    def _():
        mix_scr[1] = jnp.zeros(mix_scr.shape[1:], F32)

    @pl.when(j <= 1)
    def _():
        carry_ref[...] = jnp.zeros(carry_ref.shape, F32)

    def branch_chunks(lo, hi):
        out = []
        for c0 in range(lo, hi, MERGE_COL_CHUNK):
            cs = slice(c0, c0 + MERGE_COL_CHUNK)
            a = _dot(a_ref[...], wa_ref[:, cs])
            f = _dot(f_ref[...], wf_ref[:, cs])
            p = _dot(p_ref[...], wp_ref[:, cs])
            out.append((g_ref[:, c0:c0 + MERGE_COL_CHUNK].astype(F32) * a
                        + g_ref[:, D_MODEL + c0:D_MODEL + c0 + MERGE_COL_CHUNK].astype(F32) * f
                        + g_ref[:, 2 * D_MODEL + c0:2 * D_MODEL + c0 + MERGE_COL_CHUNK].astype(F32) * p
                        ).astype(BF16))
        return out

    def step(cur, prev):
        chunks = branch_chunks(0, D_MODEL // 2)

        gate1 = mod_ref[2:3, :]
        shift2 = mod_ref[3:4, :]
        scale2 = mod_ref[4:5, :]
        x1 = x_ref[...] + gate1 * mix_scr[prev]
        x1_out[...] = x1
        h2 = _rms(x1) * g2_ref[...] * (1.0 + scale2) + shift2
        hp_out[...] = _pack_bf16_pairs(h2)
        h_hi = h2.astype(BF16)
        h_lo = (h2 - h_hi.astype(F32)).astype(BF16)
        logits = _dot(jnp.concatenate([h_hi, h_lo, h_hi], axis=1), wr_ref[...]) + br_ref[...]

        chunks += branch_chunks(D_MODEL // 2, D_MODEL)
        mix_scr[cur] = _dot(jnp.concatenate(chunks, axis=1), wo_ref[...])

        lane = lax.broadcasted_iota(jnp.int32, logits.shape, 1)
        work = logits
        vals, idxs = [], []
        for _ in range(TOP_K):
            m = jnp.max(work, axis=-1, keepdims=True)
            idx = jnp.min(jnp.where(work == m, lane, ROUTER_PAD), axis=-1, keepdims=True)
            vals.append(m)
            idxs.append(idx)
            work = jnp.where(lane == idx, -jnp.inf, work)
        es = [jnp.exp(v - vals[0]) for v in vals]
        den = functools.reduce(jnp.add, es)

        chosen = functools.reduce(jnp.logical_or, [lane == idx for idx in idxs])
        hot = jnp.where(chosen, 1.0, 0.0)
        before = _dot(tri_ref[...], hot.astype(BF16)) + carry_ref[...]
        carry_ref[...] = carry_ref[...] + jnp.sum(hot, axis=0, keepdims=True)
        cnt_out[...] = carry_ref[...].astype(jnp.int32)

        te = jnp.zeros(logits.shape, jnp.int32)
        tw = jnp.zeros(logits.shape, F32)
        rk = jnp.zeros(logits.shape, jnp.int32)
        for k in range(TOP_K):
            rank_k = jnp.sum(jnp.where(lane == idxs[k], before, 0.0), axis=-1, keepdims=True).astype(jnp.int32)
            te = jnp.where(lane == k, idxs[k], te)
            tw = jnp.where(lane == k, es[k] / den, tw)
            rk = jnp.where(lane == k, rank_k, rk)
        te_out[...] = te
        tw_out[...] = tw
        rk_out[...] = rk

    for parity in (0, 1):
        pl.when(j % 2 == parity)(functools.partial(step, parity, 1 - parity))


def _merge(x, mods, attn, fnet, pool, gates, l, wts, consts, tok):
    t = x.shape[0]
    tm = TOKEN_TILE
    p_tiles = tok["t_prompt"] // tm
    s_tiles = tok["dec_seq"] // tm

    n_tiles = t // tm
    routed = lambda j: jnp.maximum(j - 1, 0)
    mixed = lambda j: jnp.minimum(j, n_tiles - 1)

    def mod_idx(j):
        i = routed(j)
        row = jnp.where(i < p_tiles, 0, 1 + (i - p_tiles) // s_tiles)
        return (l, row, 0, 0)

    row = lambda w: pl.BlockSpec((tm, w), lambda j: (routed(j), 0))
    head = lambda w: pl.BlockSpec((tm, w), lambda j: (mixed(j), 0))
    out_shape = [
        jax.ShapeDtypeStruct((t, D_MODEL), F32),
        jax.ShapeDtypeStruct((t, D_MODEL // 2), jnp.int32),
        jax.ShapeDtypeStruct((t, ROUTER_PAD), jnp.int32),
        jax.ShapeDtypeStruct((t, ROUTER_PAD), F32),
        jax.ShapeDtypeStruct((t, ROUTER_PAD), jnp.int32),
    ]
    return pl.pallas_call(
        _merge_kernel,
        out_shape=out_shape + [jax.ShapeDtypeStruct((1, ROUTER_PAD), jnp.int32)],
        grid=(n_tiles + 1,),
        in_specs=[
            row(D_MODEL),
            pl.BlockSpec((None, None, 6, D_MODEL), mod_idx),
            head(V_WIDTH), head(FNET_WIDTH), head(POOL_WIDTH), head(N_BRANCHES * D_MODEL),
            _layer_spec((V_WIDTH, D_MODEL), l),
            _layer_spec((FNET_WIDTH, D_MODEL), l),
            _layer_spec((POOL_WIDTH, D_MODEL), l),
            _layer_spec((D_MODEL, D_MODEL), l),
            _layer_spec((1, D_MODEL), l),
            _layer_spec((3 * D_MODEL, ROUTER_PAD), l),
            _layer_spec((1, ROUTER_PAD), l),
            _const_spec((tm, tm)),
        ],
        out_specs=[row(s.shape[1]) for s in out_shape] + [pl.BlockSpec((1, ROUTER_PAD), lambda j: (0, 0))],
        scratch_shapes=[pltpu.VMEM((2, tm, D_MODEL), F32), pltpu.VMEM((1, ROUTER_PAD), F32)],
        compiler_params=_cparams("arbitrary"),
        cost_estimate=pl.CostEstimate(
            flops=2 * t * (D_MODEL * (V_WIDTH + FNET_WIDTH + POOL_WIDTH + D_MODEL) + 3 * D_MODEL * ROUTER_PAD
                           + tm * ROUTER_PAD),
            transcendentals=t * TOP_K,
            bytes_accessed=t * (2 * (V_WIDTH + FNET_WIDTH + POOL_WIDTH + N_BRANCHES * D_MODEL) + 4 * 2 * D_MODEL
                                + 4 * (D_MODEL // 2) + 3 * 4 * ROUTER_PAD)
            + 2 * D_MODEL * (V_WIDTH + FNET_WIDTH + POOL_WIDTH + D_MODEL + 3 * ROUTER_PAD)),
        name="merge_router",
    )(x, mods, attn, fnet, pool, gates, wts["w_attn_o"], wts["w_fnet_o"], wts["w_pool_o"], wts["w_out"],
      wts["norm2_g"], wts["w_router3"], wts["b_router"], consts["tri"])


def _sc_row_gather(table, idx):
    n = idx.shape[0]
    d = table.shape[1]
    info = plsc.get_sparse_core_info()
    n_cores = info.num_cores
    n_workers = n_cores * info.num_subcores
    per_worker = n // n_workers
    n_chunks = per_worker // SC_GATHER_ROWS
    assert n_chunks * SC_GATHER_ROWS * n_workers == n
    mesh = plsc.VectorSubcoreMesh(core_axis_name="core", subcore_axis_name="subcore")

    @functools.partial(
        pl.kernel, mesh=mesh, out_type=jax.ShapeDtypeStruct((n, d), table.dtype),
        scratch_types=[pltpu.VMEM((SC_GATHER_ROWS,), jnp.int32), pltpu.VMEM((SC_GATHER_ROWS, d), table.dtype),
                       pltpu.SemaphoreType.DMA],
        cost_estimate=pl.CostEstimate(flops=0, transcendentals=0,
                                      bytes_accessed=2 * n * d * table.dtype.itemsize + 4 * n),
        name="sc_row_gather")
    def gather(table_hbm, idx_hbm, out_hbm, idx_v, rows_v, sem):
        worker = lax.axis_index("subcore") * n_cores + lax.axis_index("core")

        @pl.loop(0, n_chunks)
        def _(c):
            base = worker * per_worker + c * SC_GATHER_ROWS
            pltpu.sync_copy(idx_hbm.at[pl.ds(base, SC_GATHER_ROWS)], idx_v)
            pltpu.async_copy(table_hbm.at[idx_v], rows_v, sem).wait()
            pltpu.sync_copy(rows_v, out_hbm.at[pl.ds(base, SC_GATHER_ROWS)])

    return gather(table, idx)


PLAN_EXPERT, PLAN_VALID, PLAN_FIRST, PLAN_SLOT, PLAN_NEXT, PLAN_USED = range(6)


def _expert_kernel(plan_ref, xs_ref, bgu_ref, bd_ref, wgu_hbm, wd_hbm, y_ref,
                   wgu_f32, wd_f32, wgu_bf, wd_bf, sems, *, layer):
    b = pl.program_id(0)

    def weight_copies(expert, slot):
        return (pltpu.make_async_copy(wgu_hbm.at[layer, expert], wgu_f32.at[slot], sems.at[0, slot]),
                pltpu.make_async_copy(wd_hbm.at[layer, expert], wd_f32.at[slot], sems.at[1, slot]))

    def ffn(rows):
        x = jnp.concatenate(_unpack_bf16_pairs(xs_ref[0:rows, :]), axis=1).astype(BF16)
        gu = _dot(x, wgu_bf[...]) + bgu_ref[...]
        glu = jnp.minimum(gu[:, :D_FF], SWIGLU_LIMIT)
        lin = jnp.clip(gu[:, D_FF:], -SWIGLU_LIMIT, SWIGLU_LIMIT)
        act = glu * jax.nn.sigmoid(SWIGLU_ALPHA * glu) * (lin + 1.0)
        y_ref[0:rows, :] = _pack_bf16_pairs(_dot(act.astype(BF16), wd_bf[...]) + bd_ref[...])
        if rows < MOE_BLOCK:
            y_ref[rows:, :] = jnp.zeros((MOE_BLOCK - rows, D_MODEL // 2), y_ref.dtype)

    n_used = plan_ref[PLAN_USED, 0]

    @pl.when(b < n_used)
    def _():
        slot = plan_ref[PLAN_SLOT, b]

        @pl.when(b == 0)
        def _():
            for cp in weight_copies(plan_ref[PLAN_EXPERT, 0], 0):
                cp.start()

        @pl.when(plan_ref[PLAN_FIRST, b] == 1)
        def _():
            for cp in weight_copies(plan_ref[PLAN_EXPERT, b], slot):
                cp.wait()
            wgu_bf[...] = wgu_f32[slot].astype(BF16)
            wd_bf[...] = wd_f32[slot].astype(BF16)

            @pl.when(plan_ref[PLAN_NEXT, b] >= 0)
            def _():
                for cp in weight_copies(plan_ref[PLAN_NEXT, b], 1 - slot):
                    cp.start()

        steps = (plan_ref[PLAN_VALID, b] + (MOE_STEP - 1)) // MOE_STEP
        for rows in range(MOE_STEP, MOE_BLOCK + 1, MOE_STEP):
            want = steps <= 1 if rows == MOE_STEP else steps == rows // MOE_STEP
            pl.when(want)(functools.partial(ffn, rows))

    @pl.when(b >= n_used)
    def _():
        y_ref[...] = jnp.zeros(y_ref.shape, y_ref.dtype)


def _experts(xs, plan, l, w_gu, b_gu4, w_down, b_down4):
    n_slots = xs.shape[0]
    bm = MOE_BLOCK
    n_blocks = n_slots // bm

    def blk(b, plan):
        return jnp.minimum(b, plan[PLAN_USED, 0] - 1)

    def bspec(width):
        return pl.BlockSpec((None, None, 1, width), lambda b, plan: (l, plan[PLAN_EXPERT, blk(b, plan)], 0, 0))

    grid_spec = pltpu.PrefetchScalarGridSpec(
        num_scalar_prefetch=1,
        grid=(n_blocks,),
        in_specs=[
            pl.BlockSpec((bm, D_MODEL // 2), lambda b, plan: (blk(b, plan), 0)),
            bspec(2 * D_FF), bspec(D_MODEL),
            pl.BlockSpec(memory_space=pl.ANY), pl.BlockSpec(memory_space=pl.ANY),
        ],
        out_specs=pl.BlockSpec((bm, D_MODEL // 2), lambda b, plan: (b, 0)),
        scratch_shapes=[pltpu.VMEM((2, D_MODEL, 2 * D_FF), F32), pltpu.VMEM((2, D_FF, D_MODEL), F32),
                        pltpu.VMEM((D_MODEL, 2 * D_FF), BF16), pltpu.VMEM((D_FF, D_MODEL), BF16),
                        pltpu.SemaphoreType.DMA((2, 2))],
    )
    return pl.pallas_call(
        functools.partial(_expert_kernel, layer=l),
        out_shape=jax.ShapeDtypeStruct((n_slots, D_MODEL // 2), jnp.int32),
        grid_spec=grid_spec,
        compiler_params=_cparams("arbitrary"),
        cost_estimate=pl.CostEstimate(
            flops=2 * n_slots * 3 * D_MODEL * D_FF, transcendentals=n_slots * D_FF,
            bytes_accessed=4 * N_EXPERTS * 3 * D_MODEL * D_FF + 2 * 4 * n_slots * (D_MODEL // 2)),
        name="moe_experts",
    )(plan, xs, b_gu4, b_down4, w_gu, w_down)


def _combine_kernel(tw_ref, x_ref, mod_ref, yg_ref, o_ref):
    o_ref[...] = _moe_residual(tw_ref, x_ref, mod_ref, yg_ref)


def _combine(yg, tw, x1, mods, l, tok):
    t = x1.shape[0]
    tc = COMBINE_TILE
    p_tiles = tok["t_prompt"] // tc
    s_tiles = tok["dec_seq"] // tc

    def mod_idx(i):
        row = jnp.where(i < p_tiles, 0, 1 + (i - p_tiles) // s_tiles)
        return (l, row, 0, 0)

    return pl.pallas_call(
        _combine_kernel,
        out_shape=jax.ShapeDtypeStruct((t, D_MODEL), F32),
        grid=(t // tc,),
        in_specs=[
            pl.BlockSpec((tc, ROUTER_PAD), lambda i: (i, 0)),
            pl.BlockSpec((tc, D_MODEL), lambda i: (i, 0)),
            pl.BlockSpec((None, None, 6, D_MODEL), mod_idx),
            pl.BlockSpec((TOP_K, tc, D_MODEL // 2), lambda i: (0, i, 0)),
        ],
        out_specs=pl.BlockSpec((tc, D_MODEL), lambda i: (i, 0)),
        compiler_params=_cparams("arbitrary"),
        name="moe_combine",
    )(tw, x1, mods, yg)


def _slot_tokens(slot, counts, padded_start, padded_end, n_slots, t):
    n = slot.shape[0]
    j = jnp.arange(MOE_BLOCK, dtype=jnp.int32)
    pad_key = (padded_start + counts)[:, None] + j[None, :]
    pad_key = jnp.where(pad_key < padded_end[:, None], pad_key, n_slots).reshape(N_EXPERTS * MOE_BLOCK)
    pad_val = jnp.arange(N_EXPERTS * MOE_BLOCK, dtype=jnp.int32) % t
    keys = jnp.concatenate([slot, pad_key])
    vals = jnp.concatenate([jnp.arange(n, dtype=jnp.int32) // TOP_K, pad_val])
    tok_bits = max(1, (t - 1).bit_length())
    assert (n_slots + 1) << tok_bits < 2 ** 31
    packed = jnp.sort(lax.shift_left(keys, tok_bits) | vals)
    return packed[:n_slots] & ((1 << tok_bits) - 1)


def _route(te_pad, rank_pad, counts_pad, n_slots):
    t = te_pad.shape[0]
    flat_e = te_pad[:, :TOP_K].reshape(t * TOP_K)
    onehot = (flat_e[:, None] == jnp.arange(N_EXPERTS, dtype=jnp.int32)[None, :]).astype(jnp.int32)
    rank = rank_pad[:, :TOP_K].reshape(t * TOP_K)
    counts = counts_pad[0, :N_EXPERTS]
    padded = (counts + MOE_BLOCK - 1) // MOE_BLOCK * MOE_BLOCK
    padded_end = jnp.cumsum(padded)
    padded_start = padded_end - padded
    slot = jnp.sum(onehot * padded_start[None, :], axis=1) + rank
    n_blocks = n_slots // MOE_BLOCK
    block_row0 = jnp.arange(n_blocks, dtype=jnp.int32) * MOE_BLOCK
    block_exp = jnp.minimum(
        jnp.sum((padded_end[None, :] <= block_row0[:, None]).astype(jnp.int32), axis=1),
        N_EXPERTS - 1).astype(jnp.int32)
    exp_hot = (block_exp[:, None] == jnp.arange(N_EXPERTS, dtype=jnp.int32)[None, :]).astype(jnp.int32)
    rows_end = jnp.sum(exp_hot * (padded_start + counts)[None, :], axis=1)
    block_valid = jnp.clip(rows_end - block_row0, 0, MOE_BLOCK).astype(jnp.int32)
    n_used = (padded_end[-1] // MOE_BLOCK).astype(jnp.int32)
    block_ids = jnp.arange(n_blocks, dtype=jnp.int32)
    prev_exp = jnp.concatenate([jnp.full((1,), -1, jnp.int32), block_exp[:-1]])
    first = ((block_exp != prev_exp) & (block_ids < n_used)).astype(jnp.int32)
    buf_slot = (jnp.cumsum(first) - 1) % 2
    experts = jnp.arange(N_EXPERTS, dtype=jnp.int32)
    holder = jnp.where(counts > 0, experts, N_EXPERTS)
    later = lax.cummin(jnp.concatenate([holder[1:], jnp.full((1,), N_EXPERTS, jnp.int32)]), reverse=True)
    next_exp = jnp.where(later < N_EXPERTS, later, -1)
    block_next = jnp.sum(exp_hot * next_exp[None, :], axis=1)
    plan = jnp.stack([block_exp, block_valid, first, buf_slot, block_next,
                      jnp.full((n_blocks,), n_used, jnp.int32)]).astype(jnp.int32)
    slot = slot.astype(jnp.int32)
    slot_tok = _slot_tokens(slot, counts, padded_start, padded_end, n_slots, t)
    slot_by_choice = slot.reshape(t, TOP_K).T.reshape(TOP_K * t)
    return slot_by_choice, slot_tok, plan


def _segment_matrices():
    m_q = np.zeros((QK_PAD, LANES), np.float32)
    m_k = np.zeros((QK_PAD, LANES), np.float32)
    inv_cnt_q = np.ones((1, LANES), np.float32)
    for h in range(N_HEADS):
        m_q[h * HEAD_PAD:h * HEAD_PAD + QK_NOPE, 2 * h] = 1.0
        m_q[h * HEAD_PAD + QK_NOPE:h * HEAD_PAD + QK_HEAD, 2 * h + 1] = 1.0
        inv_cnt_q[0, 2 * h] = 1.0 / QK_NOPE
        inv_cnt_q[0, 2 * h + 1] = 1.0 / ROPE_DIM
        m_k[h * HEAD_PAD:h * HEAD_PAD + QK_NOPE, h] = 1.0
    dup = lambda m: np.concatenate([m.T, m.T], axis=0)
    return dict(m_q=jnp.asarray(m_q, BF16), mt2_q=jnp.asarray(dup(m_q), BF16), inv_cnt_q=jnp.asarray(inv_cnt_q),
                m_k=jnp.asarray(m_k, BF16), mt2_k=jnp.asarray(dup(m_k), BF16))


def _rope_lane_tables(n_tokens, ident_rows):
    rows = n_tokens // GRID_W
    row = jnp.broadcast_to(jnp.arange(rows)[:, None], (rows, GRID_W)).reshape(n_tokens)
    col = jnp.broadcast_to(jnp.arange(GRID_W)[None, :], (rows, GRID_W)).reshape(n_tokens)
    n_freq = ROPE_DIM // 4
    inv = 1.0 / (ROPE_THETA ** (jnp.arange(n_freq, dtype=F32) / n_freq))
    ang = jnp.concatenate([row[:, None].astype(F32) * inv, col[:, None].astype(F32) * inv], axis=-1)
    cos, sin = jnp.cos(ang), jnp.sin(ang)
    ones = jnp.ones((n_tokens, QK_NOPE), F32)
    tail = HEAD_PAD - QK_HEAD
    cosf = jnp.concatenate([ones, cos, cos, jnp.ones((n_tokens, tail), F32)], axis=1)
    sinf = jnp.concatenate([0.0 * ones, -sin, sin, jnp.zeros((n_tokens, tail), F32)], axis=1)
    cosf = jnp.concatenate([cosf, jnp.ones((ident_rows, HEAD_PAD), F32)], axis=0)
    sinf = jnp.concatenate([sinf, jnp.zeros((ident_rows, HEAD_PAD), F32)], axis=0)
    return cosf, sinf


def _dft_tables(seq_len):
    def cs(n):
        k = np.arange(n, dtype=np.int64)
        ang = 2.0 * np.pi * ((k[:, None] * k[None, :]) % n).astype(np.float64) / n
        return np.cos(ang).astype(np.float32), np.sin(ang).astype(np.float32)

    cl, sl = cs(seq_len)
    cc, sc = cs(FNET_GC)
    return dict(cos=jnp.asarray(cl).astype(BF16), sin=jnp.asarray(sl).astype(BF16),
                chan=jnp.asarray(np.concatenate([cc, sc], axis=1)).astype(BF16))


def _layout_weights(w_in, w_q_b, w_kv_b, q_nope_g, q_rope_g, k_nope_g, k_rope_g, w_router, b_router):
    depth = w_in.shape[0]
    w_in_a = w_in[:, :, :COL_F].astype(BF16)
    w_in_b = w_in[:, :, COL_KPE + ROPE_DIM:].astype(BF16)
    w_q = jnp.pad(w_q_b.reshape(depth, Q_RANK, N_HEADS, QK_HEAD),
                  ((0, 0), (0, 0), (0, 0), (0, HEAD_PAD - QK_HEAD))).reshape(depth, Q_RANK, QK_PAD).astype(BF16)
    kv = w_kv_b.reshape(depth, KV_RANK, N_HEADS, QK_NOPE + V_HEAD)
    w_k = jnp.pad(kv[..., :QK_NOPE], ((0, 0), (0, 0), (0, 0), (0, HEAD_PAD - QK_NOPE))
                  ).reshape(depth, KV_RANK, QK_PAD).astype(BF16)
    w_v = kv[..., QK_NOPE:].reshape(depth, KV_RANK, V_WIDTH).astype(BF16)
    zq = jnp.zeros((depth, HEAD_PAD - QK_HEAD), F32)
    g_q = jnp.tile(jnp.concatenate([q_nope_g, q_rope_g, zq], axis=1), (1, N_HEADS))[:, None, :]
    g_k = jnp.tile(jnp.concatenate([k_nope_g, jnp.zeros((depth, HEAD_PAD - QK_NOPE), F32)], axis=1),
                   (1, N_HEADS))[:, None, :]
    g_kpe = jnp.concatenate([jnp.zeros((depth, QK_NOPE), F32), k_rope_g, zq], axis=1)[:, None, :]
    w_hi = w_router.astype(BF16)
    w_lo = (w_router - w_hi.astype(F32)).astype(BF16)
    w_r3 = jnp.pad(jnp.concatenate([w_hi, w_hi, w_lo], axis=1), ((0, 0), (0, 0), (0, ROUTER_PAD - N_EXPERTS)))
    b_r = jnp.pad(b_router, ((0, 0), (0, ROUTER_PAD - N_EXPERTS)), constant_values=NEG_BIG)[:, None, :]
    return dict(w_in_a=w_in_a, w_in_b=w_in_b, w_q=w_q, w_k=w_k, w_v=w_v, g_q=g_q, g_k=g_k, g_kpe=g_kpe, w_router3=w_r3, b_router=b_r)


def kernel(x_prompt, x_sample, cache_ckv, cache_kpe, c, c_ctx, w_mod, b_mod, norm1_g, norm2_g, w_in, q_a_g, kv_a_g, w_q_b, w_kv_b, q_nope_g, q_rope_g, k_nope_g, k_rope_g, w_attn_o, w_fnet_o, w_pool_grp, pool_scale, w_pool_o, w_out, w_router, b_router, w_gu, b_gu, w_down, b_down):
    batch, seq, d = x_prompt.shape
    dec_batch, dec_seq, _ = x_sample.shape
    depth = w_mod.shape[0]
    past = cache_ckv.shape[2]
    t_prompt = batch * seq
    t_sample = dec_batch * dec_seq
    assert d == D_MODEL and dec_batch + 1 <= 8
    assert t_prompt % TOKEN_TILE == 0 and dec_seq % TOKEN_TILE == 0 and seq % COMBINE_TILE == 0
    assert dec_seq % GRID_W == 0

    consts = _segment_matrices()
    consts["cosf"], consts["sinf"] = _rope_lane_tables(dec_seq, IN_TILE)
    consts["tri"] = jnp.asarray(np.tril(np.ones((TOKEN_TILE, TOKEN_TILE), np.float32), -1), BF16)

    wts = _layout_weights(w_in, w_q_b, w_kv_b, q_nope_g, q_rope_g, k_nope_g, k_rope_g, w_router, b_router)
    row3 = lambda a: a[:, None, :]
    wts.update(norm1_g=row3(norm1_g), norm2_g=row3(norm2_g), q_a_g=row3(q_a_g), kv_a_g=row3(kv_a_g),
               w_attn_o=w_attn_o.astype(BF16), w_fnet_o=w_fnet_o.astype(BF16), w_pool_o=w_pool_o.astype(BF16),
               w_out=w_out.astype(BF16), w_pool_grp=w_pool_grp.astype(BF16), pool_scale=row3(pool_scale))
    b_gu4 = b_gu[:, :, None, :]
    b_down4 = b_down[:, :, None, :]

    cond8 = jnp.concatenate([c_ctx[None, :], c, jnp.zeros((8 - 1 - dec_batch, d), F32)], axis=0)
    mods = _modulation(cond8, w_mod, b_mod).reshape(depth, 8, 6, D_MODEL)

    kpe_pad = jnp.pad(cache_kpe, ((0, 0), (0, 0), (0, 0), (QK_NOPE, HEAD_PAD - QK_HEAD)))
    k_ctx, v_ctx = _ctx_keys(cache_ckv, kpe_pad, wts, consts)

    streams = [
        dict(x=x_prompt.reshape(t_prompt, d), n_seq=batch, seq_len=seq, tok=dict(t_prompt=t_prompt, dec_seq=dec_seq),
             dft=_dft_tables(seq), ctx=None, moe=None),
        dict(x=x_sample.reshape(t_sample, d), n_seq=dec_batch, seq_len=dec_seq, tok=dict(t_prompt=0, dec_seq=dec_seq),
             dft=_dft_tables(dec_seq), ctx=(k_ctx, v_ctx, past), moe=None),
    ]
    ckv_list, kpe_list = [], []

    def mixers(st, l):
        (q, k, v, ckv, kpe, f_in, p_in, st["gates"]), st["x"] = _in_projection(
            st["x"], st["moe"], mods, l, wts, consts, st["tok"])
        if st["ctx"] is None:
            ckv_list.append(ckv.reshape(batch, seq, KV_RANK))
            kpe_list.append(kpe.reshape(batch, seq, ROPE_DIM))
        parts = [(k, v, None, st["seq_len"])]
        if st["ctx"] is not None:
            parts.append((st["ctx"][0], st["ctx"][1], l, st["ctx"][2]))
        st["attn"] = _attention(q, parts, st["n_seq"], st["seq_len"])
        st["fnet"] = _fourier(f_in, st["n_seq"], st["seq_len"], 0, st["dft"])
        st["pool"] = _pooling(p_in, st["n_seq"], st["seq_len"], 0, l, wts)

    def merge_and_route(st, l):
        t = st["n_seq"] * st["seq_len"]
        n_slots = (t * TOP_K + N_EXPERTS * (MOE_BLOCK - 1) + MOE_BLOCK - 1) // MOE_BLOCK * MOE_BLOCK
        st["x1"], st["h2p"], te_pad, st["tw"], rank_pad, counts_pad = _merge(
            st["x"], mods, st["attn"], st["fnet"], st["pool"], st["gates"], l, wts, consts, st["tok"])
        st["slot_by_choice"], st["slot_tok"], st["plan"] = _route(te_pad, rank_pad, counts_pad, n_slots)

    def gather_back(st):
        t = st["n_seq"] * st["seq_len"]
        yg = _sc_row_gather(st["y"], st["slot_by_choice"]).reshape(TOP_K, t, D_MODEL // 2)
        st["moe"] = (st["x1"], st["tw"], yg)
        st["x"] = None

    a, b = streams
    for l in range(depth):
        mixers(a, l)
        if b["moe"] is not None:
            x1, tw, yg = b["moe"]
            yg, a["attn"] = lax.optimization_barrier((yg, a["attn"]))
            b["moe"] = (x1, tw, yg)
        mixers(b, l)
        merge_and_route(a, l)
        b["x"], a["slot_tok"] = lax.optimization_barrier((b["x"], a["slot_tok"]))
        a["xs"] = _sc_row_gather(a["h2p"], a["slot_tok"])
        merge_and_route(b, l)
        a["xs"], b["slot_tok"] = lax.optimization_barrier((a["xs"], b["slot_tok"]))
        b["xs"] = _sc_row_gather(b["h2p"], b["slot_tok"])
        a["y"] = _experts(a["xs"], a["plan"], l, w_gu, b_gu4, w_down, b_down4)
        gather_back(a)
        b["y"] = _experts(b["xs"], b["plan"], l, w_gu, b_gu4, w_down, b_down4)
        x1, tw, yg = a["moe"]
        yg, b["y"] = lax.optimization_barrier((yg, b["y"]))
        a["moe"] = (x1, tw, yg)
        gather_back(b)
    x1, tw, yg = a["moe"]
    out_a = _combine(yg, tw, x1, mods, depth - 1, a["tok"])
    x1, tw, yg = b["moe"]
    yg, out_a = lax.optimization_barrier((yg, out_a))
    out_b = _combine(yg, tw, x1, mods, depth - 1, b["tok"])
    y_prompt = out_a.reshape(batch, seq, d)
    y_sample = out_b.reshape(dec_batch, dec_seq, d)
    return (y_prompt, y_sample, jnp.stack(ckv_list, axis=1), jnp.stack(kpe_list, axis=1))
```

```python
import functools
import math

import numpy as np
import jax
import jax.numpy as jnp
from jax import lax
from jax.experimental import pallas as pl
from jax.experimental.pallas import tpu as pltpu
from jax.experimental.pallas import tpu_sc as plsc

D_MODEL = 1024
GRID_W = 64
N_HEADS = 8
QK_NOPE = 64
ROPE_DIM = 32
V_HEAD = 64
QK_HEAD = QK_NOPE + ROPE_DIM
Q_RANK = 384
KV_RANK = 256
ROPE_THETA = 10000.0
FNET_GROUPS = 4
FNET_GC = 128
FNET_WIDTH = FNET_GROUPS * FNET_GC
POOL_WINDOWS = (2, 4, 8, 16)
POOL_GC = 128
POOL_WIDTH = len(POOL_WINDOWS) * POOL_GC
N_BRANCHES = 3
N_EXPERTS = 32
TOP_K = 4
D_FF = D_MODEL
SWIGLU_LIMIT = 7.0
SWIGLU_ALPHA = 1.702
RMS_EPS = 1e-6

LANES = 128
HEAD_PAD = LANES
QK_PAD = N_HEADS * HEAD_PAD
V_WIDTH = N_HEADS * V_HEAD
ROPE_HALF = ROPE_DIM // 2
KPE_PAD = LANES
COL_KV = Q_RANK
COL_KPE = Q_RANK + KV_RANK
COL_F = COL_KPE + KPE_PAD
COL_P = COL_F + FNET_WIDTH
COL_G = COL_P + POOL_WIDTH
IN_PAD_WIDTH = COL_G + N_BRANCHES * D_MODEL
MIX_WIDTH = FNET_WIDTH + POOL_WIDTH
ROUTER_PAD = LANES
NEG_BIG = -1e30

TOKEN_TILE = 512
IN_TILE = 256
ATTN_Q_TILE = 512
ATTN_Q_SUB = 512
FNET_ROW_TILE = 512
POOL_CHUNK = 256
POOL_HALO = 16
MOE_BLOCK = 512
MOE_STEP = 128
GATE_COL_CHUNK = 512
MERGE_COL_CHUNK = 256
FF_CHUNK = 256
COMBINE_TILE = 256
SC_GATHER_ROWS = 64
MOD_COL_TILE = 1536
VMEM_LIMIT = 56 * 1024 * 1024

F32 = jnp.float32
BF16 = jnp.bfloat16


def _cparams(*sem):
    return pltpu.CompilerParams(dimension_semantics=sem, vmem_limit_bytes=VMEM_LIMIT)


def _dot(a, b):
    return jnp.dot(a, b, preferred_element_type=F32)


def _rms(x):
    return x * lax.rsqrt(jnp.mean(x * x, axis=-1, keepdims=True) + RMS_EPS)


def _pack_bf16_pairs(x):
    half = x.shape[1] // 2
    bits = pltpu.bitcast(x.astype(BF16).astype(F32), jnp.int32)
    return bits[:, :half] | lax.shift_right_logical(bits[:, half:], 16)


def _unpack_bf16_pairs(w):
    return pltpu.bitcast(w & jnp.int32(-65536), F32), pltpu.bitcast(lax.shift_left(w, 16), F32)


def _seg_sums(x, m_ref):
    return _dot((x * x).astype(BF16), m_ref[...])


def _seg_inv_rms(ss, inv_cnt, mt2_ref):
    r = lax.rsqrt(ss * inv_cnt + RMS_EPS)
    r_hi = r.astype(BF16)
    r_lo = (r - r_hi.astype(F32)).astype(BF16)
    return _dot(jnp.concatenate([r_hi, r_lo], axis=1), mt2_ref[...])


def _seg_rms_scale(x, m_ref, mt2_ref, inv_cnt):
    return _seg_inv_rms(_seg_sums(x, m_ref), inv_cnt, mt2_ref)


def _rope_key_chunk(kpe_chunk, gkpe_ref, rope):
    ms = jnp.sum(kpe_chunk * kpe_chunk, axis=-1, keepdims=True) * (1.0 / ROPE_DIM)
    kp = kpe_chunk * lax.rsqrt(ms + RMS_EPS) * gkpe_ref[...]
    return kp if rope is None else _rope_chunk(kp, *rope)


def _rope_chunk(xc, cosf, sinf, first_half):
    partner = jnp.where(first_half, pltpu.roll(xc, HEAD_PAD - ROPE_HALF, 1), pltpu.roll(xc, ROPE_HALF, 1))
    return xc * cosf + partner * sinf


def _write_keys(ckv, kpe_chunk, wk_ref, wv_ref, gk_ref, gkpe_ref, mk_ref, mkt2_ref, rope, k_out, v_out):
    cb = ckv.astype(BF16)
    kn = _dot(cb, wk_ref[...])
    kn = kn * _seg_rms_scale(kn, mk_ref, mkt2_ref, 1.0 / QK_NOPE) * gk_ref[...]
    v_out[...] = _dot(cb, wv_ref[...]).astype(v_out.dtype)
    kp = _rope_key_chunk(kpe_chunk, gkpe_ref, rope)
    for h in range(N_HEADS):
        sl = slice(h * HEAD_PAD, (h + 1) * HEAD_PAD)
        k_out[:, sl] = (kn[:, sl] + kp).astype(k_out.dtype)


def _mod_kernel(c_ref, w_ref, b_ref, o_ref):
    c = c_ref[...]
    s = c * jax.nn.sigmoid(c)
    o_ref[...] = _dot(s.astype(BF16), w_ref[...].astype(BF16)) + b_ref[...]


def _modulation(cond8, w_mod, b_mod):
    depth = w_mod.shape[0]
    n = w_mod.shape[2]
    return pl.pallas_call(
        _mod_kernel,
        out_shape=jax.ShapeDtypeStruct((depth, 8, n), F32),
        grid=(depth, n // MOD_COL_TILE),
        in_specs=[
            pl.BlockSpec((8, D_MODEL), lambda l, j: (0, 0)),
            pl.BlockSpec((None, D_MODEL, MOD_COL_TILE), lambda l, j: (l, 0, j)),
            pl.BlockSpec((None, 1, MOD_COL_TILE), lambda l, j: (l, 0, j)),
        ],
        out_specs=pl.BlockSpec((None, 8, MOD_COL_TILE), lambda l, j: (l, 0, j)),
        compiler_params=_cparams("arbitrary", "arbitrary"),
        name="adaln_modulation",
    )(cond8, w_mod, b_mod.reshape(depth, 1, n))


def _moe_residual(tw_ref, x1_ref, mod_ref, yg_ref):
    acc = None
    for k in range(TOP_K):
        w = tw_ref[:, k:k + 1]
        part = [w * half for half in _unpack_bf16_pairs(yg_ref[k])]
        acc = part if acc is None else [a + p for a, p in zip(acc, part)]
    return x1_ref[...] + mod_ref[5:6, :] * jnp.concatenate(acc, axis=1)


def _in_kernel(*refs, after_moe):
    n_lead = 4 if after_moe else 1
    lead, refs = refs[:n_lead], refs[n_lead:]
    hb_scr, refs = refs[-1], refs[:-1]
    x_out = None
    if after_moe:
        x_out, refs = refs[-1], refs[:-1]
    mod_ref, g1_ref = refs[0], refs[1]
    j = pl.program_id(0)

    @pl.when(j == 0)
    def _():
        hb_scr[1] = jnp.zeros(hb_scr.shape[1:], hb_scr.dtype)

    def normalise(cur):
        x = _moe_residual(*lead) if after_moe else lead[0][...]
        if after_moe:
            x_out[...] = x
        h = _rms(x) * g1_ref[...] * (1.0 + mod_ref[1:2, :]) + mod_ref[0:1, :]
        hb_scr[cur] = h.astype(BF16)

    for parity in (0, 1):
        pl.when(j % 2 == parity)(functools.partial(_in_project, hb_scr, parity, normalise, refs[2:]))


def _in_project(hb_scr, cur, normalise, refs):
    (wa_ref, wb_ref, qag_ref, kvag_ref, wq_ref, wk_ref, wv_ref,
     gq_ref, gk_ref, gkpe_ref, cos_ref, sin_ref, mq_ref, mqt2_ref, icq_ref, mk_ref, mkt2_ref,
     q_out, k_out, v_out, ckv_out, kpe_out, f_out, p_out, g_out) = refs
    hb = hb_scr[1 - cur]

    gate_chunks = iter(range(0, N_BRANCHES * D_MODEL, GATE_COL_CHUNK))

    def gate_chunk():
        c0 = next(gate_chunks)
        zc = _dot(hb, wb_ref[:, MIX_WIDTH + c0:MIX_WIDTH + c0 + GATE_COL_CHUNK])
        g_out[:, c0:c0 + GATE_COL_CHUNK] = jax.nn.sigmoid(zc).astype(g_out.dtype)

    za = _dot(hb, wa_ref[...])
    zb = _dot(hb, wb_ref[:, 0:MIX_WIDTH])
    normalise(cur)
    f_out[...] = zb[:, 0:FNET_WIDTH].astype(f_out.dtype)
    p_out[...] = zb[:, FNET_WIDTH:]
    q_lat = za[:, 0:COL_KV]
    kv_lat = za[:, COL_KV:COL_KPE]
    key_lane = lax.broadcasted_iota(jnp.int32, (hb.shape[0], KPE_PAD), 1) < ROPE_DIM
    kpe_grp = jnp.where(key_lane, za[:, COL_KPE:COL_F], 0.0)
    ckv = _rms(kv_lat) * kvag_ref[...]
    ckv_out[...] = ckv
    kpe_out[...] = kpe_grp[:, 0:ROPE_DIM]
    qn = (_rms(q_lat) * qag_ref[...]).astype(BF16)
    cb = ckv.astype(BF16)
    gate_chunk()

    q = _dot(qn, wq_ref[...])
    kn = _dot(cb, wk_ref[...])
    v_out[...] = _dot(cb, wv_ref[...]).astype(v_out.dtype)
    gate_chunk()

    ss_q = _seg_sums(q, mq_ref)
    ss_k = _seg_sums(kn, mk_ref)
    gate_chunk()

    q = q * _seg_inv_rms(ss_q, icq_ref[...], mqt2_ref) * gq_ref[...]
    kn = kn * _seg_inv_rms(ss_k, 1.0 / QK_NOPE, mkt2_ref) * gk_ref[...]
    for _ in range(N_BRANCHES * D_MODEL // GATE_COL_CHUNK - 3):
        gate_chunk()

    cosf = cos_ref[...]
    sinf = sin_ref[...]
    lane = lax.broadcasted_iota(jnp.int32, cosf.shape, 1)
    rope = (cosf, sinf, lane < QK_NOPE + ROPE_HALF)
    sm_scale = 1.0 / math.sqrt(QK_HEAD)
    kp = _rope_key_chunk(pltpu.roll(kpe_grp, QK_NOPE, 1), gkpe_ref, rope)
    for hd in range(N_HEADS):
        sl = slice(hd * HEAD_PAD, (hd + 1) * HEAD_PAD)
        q_out[:, sl] = (_rope_chunk(q[:, sl], *rope) * sm_scale).astype(q_out.dtype)
        k_out[:, sl] = (kn[:, sl] + kp).astype(k_out.dtype)


def _const_spec(shape):
    nd = len(shape)
    return pl.BlockSpec(shape, lambda i, _n=nd: (0,) * _n, pipeline_mode=pl.Buffered(1))


def _layer_spec(shape, l):
    nd = len(shape)
    return pl.BlockSpec((None,) + tuple(shape), lambda i, _l=l, _n=nd: (_l,) + (0,) * _n,
                        pipeline_mode=pl.Buffered(1))


def _in_projection(x, moe, mods, l, wts, consts, tok):
    after_moe = x is None
    t = moe[0].shape[0] if after_moe else x.shape[0]
    tm = IN_TILE
    n_tiles = t // tm
    p_tiles = tok["t_prompt"] // tm
    s_tiles = tok["dec_seq"] // tm
    rope_blocks = tok["dec_seq"] // tm

    normed = lambda j: jnp.minimum(j, n_tiles - 1)
    projected = lambda j: jnp.maximum(j - 1, 0)

    def mod_row(i):
        return jnp.where(i < p_tiles, 0, 1 + (i - p_tiles) // s_tiles)

    def rope_idx(j):
        i = projected(j)
        return (jnp.where(i < p_tiles, rope_blocks, (i - p_tiles) % s_tiles), 0)

    row = lambda w: pl.BlockSpec((tm, w), lambda j: (projected(j), 0))
    nrow = lambda w: pl.BlockSpec((tm, w), lambda j: (normed(j), 0))
    if after_moe:
        x1, tw, yg = moe
        lead_specs = [nrow(ROUTER_PAD), nrow(D_MODEL),
                      pl.BlockSpec((None, None, 6, D_MODEL), lambda j: (l - 1, mod_row(normed(j)), 0, 0)),
                      pl.BlockSpec((TOP_K, tm, D_MODEL // 2), lambda j: (0, normed(j), 0))]
        lead_args = [tw, x1, mods, yg]
    else:
        lead_specs = [nrow(D_MODEL)]
        lead_args = [x]
    in_specs = lead_specs + [
        pl.BlockSpec((None, None, 6, D_MODEL), lambda j: (l, mod_row(normed(j)), 0, 0)),
        _layer_spec((1, D_MODEL), l),
        _layer_spec((D_MODEL, COL_F), l),
        _layer_spec((D_MODEL, MIX_WIDTH + N_BRANCHES * D_MODEL), l),
        _layer_spec((1, Q_RANK), l),
        _layer_spec((1, KV_RANK), l),
        _layer_spec((Q_RANK, QK_PAD), l),
        _layer_spec((KV_RANK, QK_PAD), l),
        _layer_spec((KV_RANK, V_WIDTH), l),
        _layer_spec((1, QK_PAD), l),
        _layer_spec((1, QK_PAD), l),
        _layer_spec((1, HEAD_PAD), l),
        pl.BlockSpec((tm, HEAD_PAD), rope_idx),
        pl.BlockSpec((tm, HEAD_PAD), rope_idx),
        _const_spec((QK_PAD, LANES)),
        _const_spec((2 * LANES, QK_PAD)),
        _const_spec((1, LANES)),
        _const_spec((QK_PAD, LANES)),
        _const_spec((2 * LANES, QK_PAD)),
    ]
    out_shape = [
        jax.ShapeDtypeStruct((t, QK_PAD), BF16),
        jax.ShapeDtypeStruct((t, QK_PAD), BF16),
        jax.ShapeDtypeStruct((t, V_WIDTH), BF16),
        jax.ShapeDtypeStruct((t, KV_RANK), F32),
        jax.ShapeDtypeStruct((t, ROPE_DIM), F32),
        jax.ShapeDtypeStruct((t, FNET_WIDTH), BF16),
        jax.ShapeDtypeStruct((t, POOL_WIDTH), F32),
        jax.ShapeDtypeStruct((t, N_BRANCHES * D_MODEL), BF16),
    ]
    out_specs = [row(s.shape[1]) for s in out_shape]
    if after_moe:
        out_shape.append(jax.ShapeDtypeStruct((t, D_MODEL), F32))
        out_specs.append(nrow(D_MODEL))
    outs = pl.pallas_call(
        functools.partial(_in_kernel, after_moe=after_moe),
        out_shape=out_shape,
        grid=(n_tiles + 1,),
        in_specs=in_specs,
        out_specs=out_specs,
        scratch_shapes=[pltpu.VMEM((2, tm, D_MODEL), BF16)],
        compiler_params=_cparams("arbitrary"),
        cost_estimate=pl.CostEstimate(
            flops=2 * t * (D_MODEL * IN_PAD_WIDTH + Q_RANK * QK_PAD + KV_RANK * (QK_PAD + V_WIDTH)
                           + 2 * (QK_PAD * LANES + 2 * LANES * QK_PAD)),
            transcendentals=t * N_BRANCHES * D_MODEL,
            bytes_accessed=2 * D_MODEL * IN_PAD_WIDTH + sum(
                math.prod(s.shape) * s.dtype.itemsize for s in out_shape)
            + 4 * t * D_MODEL * (1 + (TOP_K // 2 if after_moe else 0))),
        name="in_projection",
    )(*lead_args, mods, wts["norm1_g"], wts["w_in_a"], wts["w_in_b"], wts["q_a_g"], wts["kv_a_g"], wts["w_q"], wts["w_k"], wts["w_v"],
      wts["g_q"], wts["g_k"], wts["g_kpe"], consts["cosf"], consts["sinf"],
      consts["m_q"], consts["mt2_q"], consts["inv_cnt_q"], consts["m_k"], consts["mt2_k"])
    return (outs[:8], outs[8]) if after_moe else (outs, x)


def _ctx_keys_kernel(ckv_ref, kpe_ref, wk_ref, wv_ref, gk_ref, gkpe_ref, mk_ref, mkt2_ref, k_out, v_out):
    _write_keys(ckv_ref[...], kpe_ref[...], wk_ref, wv_ref, gk_ref, gkpe_ref, mk_ref, mkt2_ref, None, k_out, v_out)


def _ctx_keys(cache_ckv, cache_kpe_pad, wts, consts):
    nb, depth, past, _ = cache_ckv.shape
    lw = lambda shape: pl.BlockSpec((None,) + shape, lambda l, b: (l,) + (0,) * len(shape))
    cs = lambda shape: pl.BlockSpec(shape, lambda l, b: (0,) * len(shape))
    return pl.pallas_call(
        _ctx_keys_kernel,
        out_shape=[jax.ShapeDtypeStruct((depth, nb * past, QK_PAD), BF16),
                   jax.ShapeDtypeStruct((depth, nb * past, V_WIDTH), BF16)],
        grid=(depth, nb),
        in_specs=[
            pl.BlockSpec((None, None, past, KV_RANK), lambda l, b: (b, l, 0, 0)),
            pl.BlockSpec((None, None, past, HEAD_PAD), lambda l, b: (b, l, 0, 0)),
            lw((KV_RANK, QK_PAD)), lw((KV_RANK, V_WIDTH)), lw((1, QK_PAD)), lw((1, HEAD_PAD)),
            cs((QK_PAD, LANES)), cs((2 * LANES, QK_PAD)),
        ],
        out_specs=[pl.BlockSpec((None, past, QK_PAD), lambda l, b: (l, b, 0)),
                   pl.BlockSpec((None, past, V_WIDTH), lambda l, b: (l, b, 0))],
        compiler_params=_cparams("arbitrary", "arbitrary"),
        name="context_keys",
    )(cache_ckv, cache_kpe_pad, wts["w_k"], wts["w_v"], wts["g_k"], wts["g_kpe"], consts["m_k"], consts["mt2_k"])


def _attn_kernel(*refs, n_parts):
    q_ref = refs[0]
    k_refs = refs[1:1 + 2 * n_parts:2]
    v_refs = refs[2:2 + 2 * n_parts:2]
    o_ref = refs[1 + 2 * n_parts]
    sub = min(ATTN_Q_SUB, q_ref.shape[0])
    lane = lax.broadcasted_iota(jnp.int32, (sub, 2 * V_HEAD), 1)
    for r0 in range(0, q_ref.shape[0], sub):
        rows = slice(r0, r0 + sub)
        for pair in range(N_HEADS // 2):
            vsl = slice(pair * 2 * V_HEAD, (pair + 1) * 2 * V_HEAD)
            outs = []
            for hd in (2 * pair, 2 * pair + 1):
                sl = slice(hd * HEAD_PAD, (hd + 1) * HEAD_PAD)
                qh = q_ref[rows, sl]
                ss = [lax.dot_general(qh, k[:, sl], (((1,), (1,)), ((), ())), preferred_element_type=F32)
                      for k in k_refs]
                m = functools.reduce(jnp.maximum, [jnp.max(s, axis=-1, keepdims=True) for s in ss])
                es = [jnp.exp(s - m) for s in ss]
                den = functools.reduce(jnp.add, [jnp.sum(e, axis=-1, keepdims=True) for e in es])
                acc = functools.reduce(jnp.add, [_dot(e.astype(BF16), v[:, vsl]) for e, v in zip(es, v_refs)])
                outs.append(acc / den)
            o_ref[rows, vsl] = jnp.where(lane < V_HEAD, outs[0], outs[1]).astype(o_ref.dtype)


def _attention(q, kv_parts, n_seq, seq_len):
    tq = min(ATTN_Q_TILE, seq_len)
    nq = seq_len // tq
    n_keys = sum(rows for _, _, _, rows in kv_parts)
    in_specs = [pl.BlockSpec((tq, QK_PAD), lambda b, i: (b * nq + i, 0))]
    args = [q]
    for k, v, layer, rows in kv_parts:
        for arr, width in ((k, QK_PAD), (v, V_WIDTH)):
            if layer is None:
                in_specs.append(pl.BlockSpec((rows, width), lambda b, i: (b, 0)))
            else:
                in_specs.append(pl.BlockSpec((None, rows, width), lambda b, i, _l=layer: (_l, b, 0)))
            args.append(arr)
    return pl.pallas_call(
        functools.partial(_attn_kernel, n_parts=len(kv_parts)),
        out_shape=jax.ShapeDtypeStruct((n_seq * seq_len, V_WIDTH), BF16),
        grid=(n_seq, nq),
        in_specs=in_specs,
        out_specs=pl.BlockSpec((tq, V_WIDTH), lambda b, i: (b * nq + i, 0)),
        compiler_params=_cparams("arbitrary", "arbitrary"),
        cost_estimate=pl.CostEstimate(
            flops=2 * n_seq * seq_len * n_keys * N_HEADS * (HEAD_PAD + 2 * V_HEAD),
            transcendentals=n_seq * seq_len * n_keys * N_HEADS,
            bytes_accessed=2 * n_seq * (seq_len * (QK_PAD + V_WIDTH) + n_keys * (QK_PAD + V_WIDTH))),
        name="attention",
    )(*args)


def _fnet_kernel(f_ref, cs_ref, cl_ref, sl_ref, o_ref, top_ref, bot_ref, *, norm):
    @pl.when(pl.program_id(1) == 0)
    def _():
        for g in range(FNET_GROUPS):
            sl = slice(g * FNET_GC, (g + 1) * FNET_GC)
            a = _dot(f_ref[:, sl], cs_ref[...])
            top_ref[:, sl] = a[:, :FNET_GC].astype(BF16)
            bot_ref[:, sl] = a[:, FNET_GC:].astype(BF16)

    y = _dot(cl_ref[...], top_ref[...]) - _dot(sl_ref[...], bot_ref[...])
    o_ref[...] = (y * norm).astype(o_ref.dtype)


def _fourier(f_in, n_seq, seq_len, row0, tabs):
    tr = min(FNET_ROW_TILE, seq_len)
    nj = seq_len // tr
    sb0 = row0 // seq_len
    return pl.pallas_call(
        functools.partial(_fnet_kernel, norm=1.0 / math.sqrt(seq_len * FNET_GC)),
        out_shape=jax.ShapeDtypeStruct((n_seq * seq_len, FNET_WIDTH), BF16),
        grid=(n_seq, nj),
        in_specs=[
            pl.BlockSpec((seq_len, FNET_WIDTH), lambda b, j: (sb0 + b, 0)),
            pl.BlockSpec((FNET_GC, 2 * FNET_GC), lambda b, j: (0, 0)),
            pl.BlockSpec((tr, seq_len), lambda b, j: (j, 0)),
            pl.BlockSpec((tr, seq_len), lambda b, j: (j, 0)),
        ],
        out_specs=pl.BlockSpec((tr, FNET_WIDTH), lambda b, j: (b * nj + j, 0)),
        scratch_shapes=[pltpu.VMEM((seq_len, FNET_WIDTH), BF16), pltpu.VMEM((seq_len, FNET_WIDTH), BF16)],
        compiler_params=_cparams("arbitrary", "arbitrary"),
        name="fourier_mix",
    )(f_in, tabs["chan"], tabs["cos"], tabs["sin"])


def _pool_kernel(p_ref, wg_ref, ps_ref, o_ref, pad_ref):
    seq_len = p_ref.shape[0]
    zeros = jnp.zeros((POOL_HALO, POOL_WIDTH), F32)
    pad_ref[0:POOL_HALO, :] = zeros
    pad_ref[POOL_HALO + seq_len:, :] = zeros
    pad_ref[POOL_HALO:POOL_HALO + seq_len, :] = p_ref[...]
    ch = min(POOL_CHUNK, seq_len)
    for c in range(seq_len // ch):
        t = lax.broadcasted_iota(jnp.int32, (ch, 1), 0) + c * ch
        for g, w in enumerate(POOL_WINDOWS):
            half = w // 2
            sl = slice(g * POOL_GC, (g + 1) * POOL_GC)
            acc = None
            for j in range(-half, half):
                r0 = POOL_HALO + c * ch + j
                part = pad_ref[r0:r0 + ch, sl]
                acc = part if acc is None else acc + part
            cnt = (jnp.minimum(t + half, seq_len) - jnp.maximum(t - half, 0)).astype(F32)
            pooled = acc / cnt - p_ref[c * ch:(c + 1) * ch, sl]
            mixed = _dot(pooled.astype(BF16), wg_ref[g]) * ps_ref[:, sl]
            o_ref[c * ch:(c + 1) * ch, sl] = mixed.astype(o_ref.dtype)


def _pooling(p_in, n_seq, seq_len, row0, l, wts):
    sb0 = row0 // seq_len
    g = len(POOL_WINDOWS)
    return pl.pallas_call(
        _pool_kernel,
        out_shape=jax.ShapeDtypeStruct((n_seq * seq_len, POOL_WIDTH), BF16),
        grid=(n_seq,),
        in_specs=[
            pl.BlockSpec((seq_len, POOL_WIDTH), lambda b: (sb0 + b, 0)),
            pl.BlockSpec((None, g, POOL_GC, POOL_GC), lambda b: (l, 0, 0, 0)),
            pl.BlockSpec((None, 1, POOL_WIDTH), lambda b: (l, 0, 0)),
        ],
        out_specs=pl.BlockSpec((seq_len, POOL_WIDTH), lambda b: (b, 0)),
        scratch_shapes=[pltpu.VMEM((seq_len + 2 * POOL_HALO, POOL_WIDTH), F32)],
        compiler_params=_cparams("arbitrary"),
        name="pool_mix",
    )(p_in, wts["w_pool_grp"], wts["pool_scale"])


def _merge_kernel(x_ref, mod_ref, a_ref, f_ref, p_ref, g_ref,
                  wa_ref, wf_ref, wp_ref, wo_ref, g2_ref, wr_ref, br_ref, tri_ref,
                  x1_out, hp_out, te_out, tw_out, rk_out, cnt_out, mix_scr, carry_ref):
    j = pl.program_id(0)

    @pl.when(j == 0)
    def _():
        mix_scr[1] = jnp.zeros(mix_scr.shape[1:], F32)

    @pl.when(j <= 1)
    def _():
        carry_ref[...] = jnp.zeros(carry_ref.shape, F32)

    def branch_chunks(lo, hi):
        out = []
        for c0 in range(lo, hi, MERGE_COL_CHUNK):
            cs = slice(c0, c0 + MERGE_COL_CHUNK)
            a = _dot(a_ref[...], wa_ref[:, cs])
            f = _dot(f_ref[...], wf_ref[:, cs])
            p = _dot(p_ref[...], wp_ref[:, cs])
            out.append((g_ref[:, c0:c0 + MERGE_COL_CHUNK].astype(F32) * a
                        + g_ref[:, D_MODEL + c0:D_MODEL + c0 + MERGE_COL_CHUNK].astype(F32) * f
                        + g_ref[:, 2 * D_MODEL + c0:2 * D_MODEL + c0 + MERGE_COL_CHUNK].astype(F32) * p
                        ).astype(BF16))
        return out

    def step(cur, prev):
        chunks = branch_chunks(0, D_MODEL // 2)

        gate1 = mod_ref[2:3, :]
        shift2 = mod_ref[3:4, :]
        scale2 = mod_ref[4:5, :]
        x1 = x_ref[...] + gate1 * mix_scr[prev]
        x1_out[...] = x1
        h2 = _rms(x1) * g2_ref[...] * (1.0 + scale2) + shift2
        hp_out[...] = _pack_bf16_pairs(h2)
        h_hi = h2.astype(BF16)
        h_lo = (h2 - h_hi.astype(F32)).astype(BF16)
        logits = _dot(jnp.concatenate([h_hi, h_lo, h_hi], axis=1), wr_ref[...]) + br_ref[...]

        chunks += branch_chunks(D_MODEL // 2, D_MODEL)
        mix_scr[cur] = _dot(jnp.concatenate(chunks, axis=1), wo_ref[...])

        lane = lax.broadcasted_iota(jnp.int32, logits.shape, 1)
        work = logits
        vals, idxs = [], []
        for _ in range(TOP_K):
            m = jnp.max(work, axis=-1, keepdims=True)
            idx = jnp.min(jnp.where(work == m, lane, ROUTER_PAD), axis=-1, keepdims=True)
            vals.append(m)
            idxs.append(idx)
            work = jnp.where(lane == idx, -jnp.inf, work)
        es = [jnp.exp(v - vals[0]) for v in vals]
        den = functools.reduce(jnp.add, es)

        chosen = functools.reduce(jnp.logical_or, [lane == idx for idx in idxs])
        hot = jnp.where(chosen, 1.0, 0.0)
        before = _dot(tri_ref[...], hot.astype(BF16)) + carry_ref[...]
        carry_ref[...] = carry_ref[...] + jnp.sum(hot, axis=0, keepdims=True)
        cnt_out[...] = carry_ref[...].astype(jnp.int32)

        te = jnp.zeros(logits.shape, jnp.int32)
        tw = jnp.zeros(logits.shape, F32)
        rk = jnp.zeros(logits.shape, jnp.int32)
        for k in range(TOP_K):
            rank_k = jnp.sum(jnp.where(lane == idxs[k], before, 0.0), axis=-1, keepdims=True).astype(jnp.int32)
            te = jnp.where(lane == k, idxs[k], te)
            tw = jnp.where(lane == k, es[k] / den, tw)
            rk = jnp.where(lane == k, rank_k, rk)
        te_out[...] = te
        tw_out[...] = tw
        rk_out[...] = rk

    for parity in (0, 1):
        pl.when(j % 2 == parity)(functools.partial(step, parity, 1 - parity))


def _merge(x, mods, attn, fnet, pool, gates, l, wts, consts, tok):
    t = x.shape[0]
    tm = TOKEN_TILE
    p_tiles = tok["t_prompt"] // tm
    s_tiles = tok["dec_seq"] // tm

    n_tiles = t // tm
    routed = lambda j: jnp.maximum(j - 1, 0)
    mixed = lambda j: jnp.minimum(j, n_tiles - 1)

    def mod_idx(j):
        i = routed(j)
        row = jnp.where(i < p_tiles, 0, 1 + (i - p_tiles) // s_tiles)
        return (l, row, 0, 0)

    row = lambda w: pl.BlockSpec((tm, w), lambda j: (routed(j), 0))
    head = lambda w: pl.BlockSpec((tm, w), lambda j: (mixed(j), 0))
    out_shape = [
        jax.ShapeDtypeStruct((t, D_MODEL), F32),
        jax.ShapeDtypeStruct((t, D_MODEL // 2), jnp.int32),
        jax.ShapeDtypeStruct((t, ROUTER_PAD), jnp.int32),
        jax.ShapeDtypeStruct((t, ROUTER_PAD), F32),
        jax.ShapeDtypeStruct((t, ROUTER_PAD), jnp.int32),
    ]
    return pl.pallas_call(
        _merge_kernel,
        out_shape=out_shape + [jax.ShapeDtypeStruct((1, ROUTER_PAD), jnp.int32)],
        grid=(n_tiles + 1,),
        in_specs=[
            row(D_MODEL),
            pl.BlockSpec((None, None, 6, D_MODEL), mod_idx),
            head(V_WIDTH), head(FNET_WIDTH), head(POOL_WIDTH), head(N_BRANCHES * D_MODEL),
            _layer_spec((V_WIDTH, D_MODEL), l),
            _layer_spec((FNET_WIDTH, D_MODEL), l),
            _layer_spec((POOL_WIDTH, D_MODEL), l),
            _layer_spec((D_MODEL, D_MODEL), l),
            _layer_spec((1, D_MODEL), l),
            _layer_spec((3 * D_MODEL, ROUTER_PAD), l),
            _layer_spec((1, ROUTER_PAD), l),
            _const_spec((tm, tm)),
        ],
        out_specs=[row(s.shape[1]) for s in out_shape] + [pl.BlockSpec((1, ROUTER_PAD), lambda j: (0, 0))],
        scratch_shapes=[pltpu.VMEM((2, tm, D_MODEL), F32), pltpu.VMEM((1, ROUTER_PAD), F32)],
        compiler_params=_cparams("arbitrary"),
        cost_estimate=pl.CostEstimate(
            flops=2 * t * (D_MODEL * (V_WIDTH + FNET_WIDTH + POOL_WIDTH + D_MODEL) + 3 * D_MODEL * ROUTER_PAD
                           + tm * ROUTER_PAD),
            transcendentals=t * TOP_K,
            bytes_accessed=t * (2 * (V_WIDTH + FNET_WIDTH + POOL_WIDTH + N_BRANCHES * D_MODEL) + 4 * 2 * D_MODEL
                                + 4 * (D_MODEL // 2) + 3 * 4 * ROUTER_PAD)
            + 2 * D_MODEL * (V_WIDTH + FNET_WIDTH + POOL_WIDTH + D_MODEL + 3 * ROUTER_PAD)),
        name="merge_router",
    )(x, mods, attn, fnet, pool, gates, wts["w_attn_o"], wts["w_fnet_o"], wts["w_pool_o"], wts["w_out"],
      wts["norm2_g"], wts["w_router3"], wts["b_router"], consts["tri"])


def _sc_row_gather(table, idx):
    n = idx.shape[0]
    d = table.shape[1]
    info = plsc.get_sparse_core_info()
    n_cores = info.num_cores
    n_workers = n_cores * info.num_subcores
    per_worker = n // n_workers
    n_chunks = per_worker // SC_GATHER_ROWS
    assert n_chunks * SC_GATHER_ROWS * n_workers == n
    mesh = plsc.VectorSubcoreMesh(core_axis_name="core", subcore_axis_name="subcore")

    @functools.partial(
        pl.kernel, mesh=mesh, out_type=jax.ShapeDtypeStruct((n, d), table.dtype),
        scratch_types=[pltpu.VMEM((SC_GATHER_ROWS,), jnp.int32), pltpu.VMEM((SC_GATHER_ROWS, d), table.dtype),
                       pltpu.SemaphoreType.DMA],
        cost_estimate=pl.CostEstimate(flops=0, transcendentals=0,
                                      bytes_accessed=2 * n * d * table.dtype.itemsize + 4 * n),
        name="sc_row_gather")
    def gather(table_hbm, idx_hbm, out_hbm, idx_v, rows_v, sem):
        worker = lax.axis_index("subcore") * n_cores + lax.axis_index("core")

        @pl.loop(0, n_chunks)
        def _(c):
            base = worker * per_worker + c * SC_GATHER_ROWS
            pltpu.sync_copy(idx_hbm.at[pl.ds(base, SC_GATHER_ROWS)], idx_v)
            pltpu.async_copy(table_hbm.at[idx_v], rows_v, sem).wait()
            pltpu.sync_copy(rows_v, out_hbm.at[pl.ds(base, SC_GATHER_ROWS)])

    return gather(table, idx)


PLAN_EXPERT, PLAN_VALID, PLAN_FIRST, PLAN_SLOT, PLAN_NEXT, PLAN_USED = range(6)


def _expert_kernel(plan_ref, xs_ref, bgu_ref, bd_ref, wgu_hbm, wd_hbm, y_ref,
                   wgu_f32, wd_f32, wgu_bf, wd_bf, sems, *, layer):
    b = pl.program_id(0)

    def weight_copies(expert, slot):
        return (pltpu.make_async_copy(wgu_hbm.at[layer, expert], wgu_f32.at[slot], sems.at[0, slot]),
                pltpu.make_async_copy(wd_hbm.at[layer, expert], wd_f32.at[slot], sems.at[1, slot]))

    def ffn(rows):
        x = jnp.concatenate(_unpack_bf16_pairs(xs_ref[0:rows, :]), axis=1).astype(BF16)
        gu = _dot(x, wgu_bf[...]) + bgu_ref[...]
        glu = jnp.minimum(gu[:, :D_FF], SWIGLU_LIMIT)
        lin = jnp.clip(gu[:, D_FF:], -SWIGLU_LIMIT, SWIGLU_LIMIT)
        act = glu * jax.nn.sigmoid(SWIGLU_ALPHA * glu) * (lin + 1.0)
        y_ref[0:rows, :] = _pack_bf16_pairs(_dot(act.astype(BF16), wd_bf[...]) + bd_ref[...])
        if rows < MOE_BLOCK:
            y_ref[rows:, :] = jnp.zeros((MOE_BLOCK - rows, D_MODEL // 2), y_ref.dtype)

    n_used = plan_ref[PLAN_USED, 0]

    @pl.when(b < n_used)
    def _():
        slot = plan_ref[PLAN_SLOT, b]

        @pl.when(b == 0)
        def _():
            for cp in weight_copies(plan_ref[PLAN_EXPERT, 0], 0):
                cp.start()

        @pl.when(plan_ref[PLAN_FIRST, b] == 1)
        def _():
            for cp in weight_copies(plan_ref[PLAN_EXPERT, b], slot):
                cp.wait()
            wgu_bf[...] = wgu_f32[slot].astype(BF16)
            wd_bf[...] = wd_f32[slot].astype(BF16)

            @pl.when(plan_ref[PLAN_NEXT, b] >= 0)
            def _():
                for cp in weight_copies(plan_ref[PLAN_NEXT, b], 1 - slot):
                    cp.start()

        steps = (plan_ref[PLAN_VALID, b] + (MOE_STEP - 1)) // MOE_STEP
        for rows in range(MOE_STEP, MOE_BLOCK + 1, MOE_STEP):
            want = steps <= 1 if rows == MOE_STEP else steps == rows // MOE_STEP
            pl.when(want)(functools.partial(ffn, rows))

    @pl.when(b >= n_used)
    def _():
        y_ref[...] = jnp.zeros(y_ref.shape, y_ref.dtype)


def _experts(xs, plan, l, w_gu, b_gu4, w_down, b_down4):
    n_slots = xs.shape[0]
    bm = MOE_BLOCK
    n_blocks = n_slots // bm

    def blk(b, plan):
        return jnp.minimum(b, plan[PLAN_USED, 0] - 1)

    def bspec(width):
        return pl.BlockSpec((None, None, 1, width), lambda b, plan: (l, plan[PLAN_EXPERT, blk(b, plan)], 0, 0))

    grid_spec = pltpu.PrefetchScalarGridSpec(
        num_scalar_prefetch=1,
        grid=(n_blocks,),
        in_specs=[
            pl.BlockSpec((bm, D_MODEL // 2), lambda b, plan: (blk(b, plan), 0)),
            bspec(2 * D_FF), bspec(D_MODEL),
            pl.BlockSpec(memory_space=pl.ANY), pl.BlockSpec(memory_space=pl.ANY),
        ],
        out_specs=pl.BlockSpec((bm, D_MODEL // 2), lambda b, plan: (b, 0)),
        scratch_shapes=[pltpu.VMEM((2, D_MODEL, 2 * D_FF), F32), pltpu.VMEM((2, D_FF, D_MODEL), F32),
                        pltpu.VMEM((D_MODEL, 2 * D_FF), BF16), pltpu.VMEM((D_FF, D_MODEL), BF16),
                        pltpu.SemaphoreType.DMA((2, 2))],
    )
    return pl.pallas_call(
        functools.partial(_expert_kernel, layer=l),
        out_shape=jax.ShapeDtypeStruct((n_slots, D_MODEL // 2), jnp.int32),
        grid_spec=grid_spec,
        compiler_params=_cparams("arbitrary"),
        cost_estimate=pl.CostEstimate(
            flops=2 * n_slots * 3 * D_MODEL * D_FF, transcendentals=n_slots * D_FF,
            bytes_accessed=4 * N_EXPERTS * 3 * D_MODEL * D_FF + 2 * 4 * n_slots * (D_MODEL // 2)),
        name="moe_experts",
    )(plan, xs, b_gu4, b_down4, w_gu, w_down)


def _combine_kernel(tw_ref, x_ref, mod_ref, yg_ref, o_ref):
    o_ref[...] = _moe_residual(tw_ref, x_ref, mod_ref, yg_ref)


def _combine(yg, tw, x1, mods, l, tok):
    t = x1.shape[0]
    tc = COMBINE_TILE
    p_tiles = tok["t_prompt"] // tc
    s_tiles = tok["dec_seq"] // tc

    def mod_idx(i):
        row = jnp.where(i < p_tiles, 0, 1 + (i - p_tiles) // s_tiles)
        return (l, row, 0, 0)

    return pl.pallas_call(
        _combine_kernel,
        out_shape=jax.ShapeDtypeStruct((t, D_MODEL), F32),
        grid=(t // tc,),
        in_specs=[
            pl.BlockSpec((tc, ROUTER_PAD), lambda i: (i, 0)),
            pl.BlockSpec((tc, D_MODEL), lambda i: (i, 0)),
            pl.BlockSpec((None, None, 6, D_MODEL), mod_idx),
            pl.BlockSpec((TOP_K, tc, D_MODEL // 2), lambda i: (0, i, 0)),
        ],
        out_specs=pl.BlockSpec((tc, D_MODEL), lambda i: (i, 0)),
        compiler_params=_cparams("arbitrary"),
        name="moe_combine",
    )(tw, x1, mods, yg)


def _slot_tokens(slot, counts, padded_start, padded_end, n_slots, t):
    n = slot.shape[0]
    j = jnp.arange(MOE_BLOCK, dtype=jnp.int32)
    pad_key = (padded_start + counts)[:, None] + j[None, :]
    pad_key = jnp.where(pad_key < padded_end[:, None], pad_key, n_slots).reshape(N_EXPERTS * MOE_BLOCK)
    pad_val = jnp.arange(N_EXPERTS * MOE_BLOCK, dtype=jnp.int32) % t
    keys = jnp.concatenate([slot, pad_key])
    vals = jnp.concatenate([jnp.arange(n, dtype=jnp.int32) // TOP_K, pad_val])
    tok_bits = max(1, (t - 1).bit_length())
    assert (n_slots + 1) << tok_bits < 2 ** 31
    packed = jnp.sort(lax.shift_left(keys, tok_bits) | vals)
    return packed[:n_slots] & ((1 << tok_bits) - 1)


def _route(te_pad, rank_pad, counts_pad, n_slots):
    t = te_pad.shape[0]
    flat_e = te_pad[:, :TOP_K].reshape(t * TOP_K)
    onehot = (flat_e[:, None] == jnp.arange(N_EXPERTS, dtype=jnp.int32)[None, :]).astype(jnp.int32)
    rank = rank_pad[:, :TOP_K].reshape(t * TOP_K)
    counts = counts_pad[0, :N_EXPERTS]
    padded = (counts + MOE_BLOCK - 1) // MOE_BLOCK * MOE_BLOCK
    padded_end = jnp.cumsum(padded)
    padded_start = padded_end - padded
    slot = jnp.sum(onehot * padded_start[None, :], axis=1) + rank
    n_blocks = n_slots // MOE_BLOCK
    block_row0 = jnp.arange(n_blocks, dtype=jnp.int32) * MOE_BLOCK
    block_exp = jnp.minimum(
        jnp.sum((padded_end[None, :] <= block_row0[:, None]).astype(jnp.int32), axis=1),
        N_EXPERTS - 1).astype(jnp.int32)
    exp_hot = (block_exp[:, None] == jnp.arange(N_EXPERTS, dtype=jnp.int32)[None, :]).astype(jnp.int32)
    rows_end = jnp.sum(exp_hot * (padded_start + counts)[None, :], axis=1)
    block_valid = jnp.clip(rows_end - block_row0, 0, MOE_BLOCK).astype(jnp.int32)
    n_used = (padded_end[-1] // MOE_BLOCK).astype(jnp.int32)
    block_ids = jnp.arange(n_blocks, dtype=jnp.int32)
    prev_exp = jnp.concatenate([jnp.full((1,), -1, jnp.int32), block_exp[:-1]])
    first = ((block_exp != prev_exp) & (block_ids < n_used)).astype(jnp.int32)
    buf_slot = (jnp.cumsum(first) - 1) % 2
    experts = jnp.arange(N_EXPERTS, dtype=jnp.int32)
    holder = jnp.where(counts > 0, experts, N_EXPERTS)
    later = lax.cummin(jnp.concatenate([holder[1:], jnp.full((1,), N_EXPERTS, jnp.int32)]), reverse=True)
    next_exp = jnp.where(later < N_EXPERTS, later, -1)
    block_next = jnp.sum(exp_hot * next_exp[None, :], axis=1)
    plan = jnp.stack([block_exp, block_valid, first, buf_slot, block_next,
                      jnp.full((n_blocks,), n_used, jnp.int32)]).astype(jnp.int32)
    slot = slot.astype(jnp.int32)
    slot_tok = _slot_tokens(slot, counts, padded_start, padded_end, n_slots, t)
    slot_by_choice = slot.reshape(t, TOP_K).T.reshape(TOP_K * t)
    return slot_by_choice, slot_tok, plan


def _segment_matrices():
    m_q = np.zeros((QK_PAD, LANES), np.float32)
    m_k = np.zeros((QK_PAD, LANES), np.float32)
    inv_cnt_q = np.ones((1, LANES), np.float32)
    for h in range(N_HEADS):
        m_q[h * HEAD_PAD:h * HEAD_PAD + QK_NOPE, 2 * h] = 1.0
        m_q[h * HEAD_PAD + QK_NOPE:h * HEAD_PAD + QK_HEAD, 2 * h + 1] = 1.0
        inv_cnt_q[0, 2 * h] = 1.0 / QK_NOPE
        inv_cnt_q[0, 2 * h + 1] = 1.0 / ROPE_DIM
        m_k[h * HEAD_PAD:h * HEAD_PAD + QK_NOPE, h] = 1.0
    dup = lambda m: np.concatenate([m.T, m.T], axis=0)
    return dict(m_q=jnp.asarray(m_q, BF16), mt2_q=jnp.asarray(dup(m_q), BF16), inv_cnt_q=jnp.asarray(inv_cnt_q),
                m_k=jnp.asarray(m_k, BF16), mt2_k=jnp.asarray(dup(m_k), BF16))


def _rope_lane_tables(n_tokens, ident_rows):
    rows = n_tokens // GRID_W
    row = jnp.broadcast_to(jnp.arange(rows)[:, None], (rows, GRID_W)).reshape(n_tokens)
    col = jnp.broadcast_to(jnp.arange(GRID_W)[None, :], (rows, GRID_W)).reshape(n_tokens)
    n_freq = ROPE_DIM // 4
    inv = 1.0 / (ROPE_THETA ** (jnp.arange(n_freq, dtype=F32) / n_freq))
    ang = jnp.concatenate([row[:, None].astype(F32) * inv, col[:, None].astype(F32) * inv], axis=-1)
    cos, sin = jnp.cos(ang), jnp.sin(ang)
    ones = jnp.ones((n_tokens, QK_NOPE), F32)
    tail = HEAD_PAD - QK_HEAD
    cosf = jnp.concatenate([ones, cos, cos, jnp.ones((n_tokens, tail), F32)], axis=1)
    sinf = jnp.concatenate([0.0 * ones, -sin, sin, jnp.zeros((n_tokens, tail), F32)], axis=1)
    cosf = jnp.concatenate([cosf, jnp.ones((ident_rows, HEAD_PAD), F32)], axis=0)
    sinf = jnp.concatenate([sinf, jnp.zeros((ident_rows, HEAD_PAD), F32)], axis=0)
    return cosf, sinf


def _dft_tables(seq_len):
    def cs(n):
        k = np.arange(n, dtype=np.int64)
        ang = 2.0 * np.pi * ((k[:, None] * k[None, :]) % n).astype(np.float64) / n
        return np.cos(ang).astype(np.float32), np.sin(ang).astype(np.float32)

    cl, sl = cs(seq_len)
    cc, sc = cs(FNET_GC)
    return dict(cos=jnp.asarray(cl).astype(BF16), sin=jnp.asarray(sl).astype(BF16),
                chan=jnp.asarray(np.concatenate([cc, sc], axis=1)).astype(BF16))


def _layout_weights(w_in, w_q_b, w_kv_b, q_nope_g, q_rope_g, k_nope_g, k_rope_g, w_router, b_router):
    depth = w_in.shape[0]
    w_in_a = w_in[:, :, :COL_F].astype(BF16)
    w_in_b = w_in[:, :, COL_KPE + ROPE_DIM:].astype(BF16)
    w_q = jnp.pad(w_q_b.reshape(depth, Q_RANK, N_HEADS, QK_HEAD),
                  ((0, 0), (0, 0), (0, 0), (0, HEAD_PAD - QK_HEAD))).reshape(depth, Q_RANK, QK_PAD).astype(BF16)
    kv = w_kv_b.reshape(depth, KV_RANK, N_HEADS, QK_NOPE + V_HEAD)
    w_k = jnp.pad(kv[..., :QK_NOPE], ((0, 0), (0, 0), (0, 0), (0, HEAD_PAD - QK_NOPE))
                  ).reshape(depth, KV_RANK, QK_PAD).astype(BF16)
    w_v = kv[..., QK_NOPE:].reshape(depth, KV_RANK, V_WIDTH).astype(BF16)
    zq = jnp.zeros((depth, HEAD_PAD - QK_HEAD), F32)
    g_q = jnp.tile(jnp.concatenate([q_nope_g, q_rope_g, zq], axis=1), (1, N_HEADS))[:, None, :]
    g_k = jnp.tile(jnp.concatenate([k_nope_g, jnp.zeros((depth, HEAD_PAD - QK_NOPE), F32)], axis=1),
                   (1, N_HEADS))[:, None, :]
    g_kpe = jnp.concatenate([jnp.zeros((depth, QK_NOPE), F32), k_rope_g, zq], axis=1)[:, None, :]
    w_hi = w_router.astype(BF16)
    w_lo = (w_router - w_hi.astype(F32)).astype(BF16)
    w_r3 = jnp.pad(jnp.concatenate([w_hi, w_hi, w_lo], axis=1), ((0, 0), (0, 0), (0, ROUTER_PAD - N_EXPERTS)))
    b_r = jnp.pad(b_router, ((0, 0), (0, ROUTER_PAD - N_EXPERTS)), constant_values=NEG_BIG)[:, None, :]
    return dict(w_in_a=w_in_a, w_in_b=w_in_b, w_q=w_q, w_k=w_k, w_v=w_v, g_q=g_q, g_k=g_k, g_kpe=g_kpe, w_router3=w_r3, b_router=b_r)


def kernel(x_prompt, x_sample, cache_ckv, cache_kpe, c, c_ctx, w_mod, b_mod, norm1_g, norm2_g, w_in, q_a_g, kv_a_g, w_q_b, w_kv_b, q_nope_g, q_rope_g, k_nope_g, k_rope_g, w_attn_o, w_fnet_o, w_pool_grp, pool_scale, w_pool_o, w_out, w_router, b_router, w_gu, b_gu, w_down, b_down):
    batch, seq, d = x_prompt.shape
    dec_batch, dec_seq, _ = x_sample.shape
    depth = w_mod.shape[0]
    past = cache_ckv.shape[2]
    t_prompt = batch * seq
    t_sample = dec_batch * dec_seq
    assert d == D_MODEL and dec_batch + 1 <= 8
    assert t_prompt % TOKEN_TILE == 0 and dec_seq % TOKEN_TILE == 0 and seq % COMBINE_TILE == 0
    assert dec_seq % GRID_W == 0

    consts = _segment_matrices()
    consts["cosf"], consts["sinf"] = _rope_lane_tables(dec_seq, IN_TILE)
    consts["tri"] = jnp.asarray(np.tril(np.ones((TOKEN_TILE, TOKEN_TILE), np.float32), -1), BF16)

    wts = _layout_weights(w_in, w_q_b, w_kv_b, q_nope_g, q_rope_g, k_nope_g, k_rope_g, w_router, b_router)
    row3 = lambda a: a[:, None, :]
    wts.update(norm1_g=row3(norm1_g), norm2_g=row3(norm2_g), q_a_g=row3(q_a_g), kv_a_g=row3(kv_a_g),
               w_attn_o=w_attn_o.astype(BF16), w_fnet_o=w_fnet_o.astype(BF16), w_pool_o=w_pool_o.astype(BF16),
               w_out=w_out.astype(BF16), w_pool_grp=w_pool_grp.astype(BF16), pool_scale=row3(pool_scale))
    b_gu4 = b_gu[:, :, None, :]
    b_down4 = b_down[:, :, None, :]

    cond8 = jnp.concatenate([c_ctx[None, :], c, jnp.zeros((8 - 1 - dec_batch, d), F32)], axis=0)
    mods = _modulation(cond8, w_mod, b_mod).reshape(depth, 8, 6, D_MODEL)

    kpe_pad = jnp.pad(cache_kpe, ((0, 0), (0, 0), (0, 0), (QK_NOPE, HEAD_PAD - QK_HEAD)))
    k_ctx, v_ctx = _ctx_keys(cache_ckv, kpe_pad, wts, consts)

    streams = [
        dict(x=x_prompt.reshape(t_prompt, d), n_seq=batch, seq_len=seq, tok=dict(t_prompt=t_prompt, dec_seq=dec_seq),
             dft=_dft_tables(seq), ctx=None, moe=None),
        dict(x=x_sample.reshape(t_sample, d), n_seq=dec_batch, seq_len=dec_seq, tok=dict(t_prompt=0, dec_seq=dec_seq),
             dft=_dft_tables(dec_seq), ctx=(k_ctx, v_ctx, past), moe=None),
    ]
    ckv_list, kpe_list = [], []

    def mixers(st, l):
        (q, k, v, ckv, kpe, f_in, p_in, st["gates"]), st["x"] = _in_projection(
            st["x"], st["moe"], mods, l, wts, consts, st["tok"])
        if st["ctx"] is None:
            ckv_list.append(ckv.reshape(batch, seq, KV_RANK))
            kpe_list.append(kpe.reshape(batch, seq, ROPE_DIM))
        parts = [(k, v, None, st["seq_len"])]
        if st["ctx"] is not None:
            parts.append((st["ctx"][0], st["ctx"][1], l, st["ctx"][2]))
        st["attn"] = _attention(q, parts, st["n_seq"], st["seq_len"])
        st["fnet"] = _fourier(f_in, st["n_seq"], st["seq_len"], 0, st["dft"])
        st["pool"] = _pooling(p_in, st["n_seq"], st["seq_len"], 0, l, wts)

    def merge_and_route(st, l):
        t = st["n_seq"] * st["seq_len"]
        n_slots = (t * TOP_K + N_EXPERTS * (MOE_BLOCK - 1) + MOE_BLOCK - 1) // MOE_BLOCK * MOE_BLOCK
        st["x1"], st["h2p"], te_pad, st["tw"], rank_pad, counts_pad = _merge(
            st["x"], mods, st["attn"], st["fnet"], st["pool"], st["gates"], l, wts, consts, st["tok"])
        st["slot_by_choice"], st["slot_tok"], st["plan"] = _route(te_pad, rank_pad, counts_pad, n_slots)

    def gather_back(st):
        t = st["n_seq"] * st["seq_len"]
        yg = _sc_row_gather(st["y"], st["slot_by_choice"]).reshape(TOP_K, t, D_MODEL // 2)
        st["moe"] = (st["x1"], st["tw"], yg)
        st["x"] = None

    a, b = streams
    for l in range(depth):
        mixers(a, l)
        if b["moe"] is not None:
            x1, tw, yg = b["moe"]
            yg, a["attn"] = lax.optimization_barrier((yg, a["attn"]))
            b["moe"] = (x1, tw, yg)
        mixers(b, l)
        merge_and_route(a, l)
        b["x"], a["slot_tok"] = lax.optimization_barrier((b["x"], a["slot_tok"]))
        a["xs"] = _sc_row_gather(a["h2p"], a["slot_tok"])
        merge_and_route(b, l)
        a["xs"], b["slot_tok"] = lax.optimization_barrier((a["xs"], b["slot_tok"]))
        b["xs"] = _sc_row_gather(b["h2p"], b["slot_tok"])
        a["y"] = _experts(a["xs"], a["plan"], l, w_gu, b_gu4, w_down, b_down4)
        gather_back(a)
        b["y"] = _experts(b["xs"], b["plan"], l, w_gu, b_gu4, w_down, b_down4)
        x1, tw, yg = a["moe"]
        yg, b["y"] = lax.optimization_barrier((yg, b["y"]))
        a["moe"] = (x1, tw, yg)
        gather_back(b)
    x1, tw, yg = a["moe"]
    out_a = _combine(yg, tw, x1, mods, depth - 1, a["tok"])
    x1, tw, yg = b["moe"]
    yg, out_a = lax.optimization_barrier((yg, out_a))
    out_b = _combine(yg, tw, x1, mods, depth - 1, b["tok"])
    y_prompt = out_a.reshape(batch, seq, d)
    y_sample = out_b.reshape(dec_batch, dec_seq, d)
    return (y_prompt, y_sample, jnp.stack(ckv_list, axis=1), jnp.stack(kpe_list, axis=1))
```

```python
import functools
import math

import numpy as np
import jax
import jax.numpy as jnp
from jax import lax
from jax.experimental import pallas as pl
from jax.experimental.pallas import tpu as pltpu
from jax.experimental.pallas import tpu_sc as plsc

D_MODEL = 1024
GRID_W = 64
N_HEADS = 8
QK_NOPE = 64
ROPE_DIM = 32
V_HEAD = 64
QK_HEAD = QK_NOPE + ROPE_DIM
Q_RANK = 384
KV_RANK = 256
ROPE_THETA = 10000.0
FNET_GROUPS = 4
FNET_GC = 128
FNET_WIDTH = FNET_GROUPS * FNET_GC
POOL_WINDOWS = (2, 4, 8, 16)
POOL_GC = 128
POOL_WIDTH = len(POOL_WINDOWS) * POOL_GC
N_BRANCHES = 3
N_EXPERTS = 32
TOP_K = 4
D_FF = D_MODEL
SWIGLU_LIMIT = 7.0
SWIGLU_ALPHA = 1.702
RMS_EPS = 1e-6

LANES = 128
HEAD_PAD = LANES
QK_PAD = N_HEADS * HEAD_PAD
V_WIDTH = N_HEADS * V_HEAD
ROPE_HALF = ROPE_DIM // 2
KPE_PAD = LANES
COL_KV = Q_RANK
COL_KPE = Q_RANK + KV_RANK
COL_F = COL_KPE + KPE_PAD
COL_P = COL_F + FNET_WIDTH
COL_G = COL_P + POOL_WIDTH
IN_PAD_WIDTH = COL_G + N_BRANCHES * D_MODEL
MIX_WIDTH = FNET_WIDTH + POOL_WIDTH
ROUTER_PAD = LANES
NEG_BIG = -1e30

TOKEN_TILE = 512
IN_TILE = 256
ATTN_Q_TILE = 512
ATTN_Q_SUB = 512
FNET_ROW_TILE = 512
POOL_CHUNK = 256
POOL_HALO = 16
MOE_BLOCK = 512
MOE_STEP = 128
GATE_COL_CHUNK = 512
MERGE_COL_CHUNK = 256
FF_CHUNK = 256
COMBINE_TILE = 256
SC_GATHER_ROWS = 64
MOD_COL_TILE = 1536
VMEM_LIMIT = 56 * 1024 * 1024

F32 = jnp.float32
BF16 = jnp.bfloat16


def _cparams(*sem):
    return pltpu.CompilerParams(dimension_semantics=sem, vmem_limit_bytes=VMEM_LIMIT)


def _dot(a, b):
    return jnp.dot(a, b, preferred_element_type=F32)


def _rms(x):
    return x * lax.rsqrt(jnp.mean(x * x, axis=-1, keepdims=True) + RMS_EPS)


def _pack_bf16_pairs(x):
    half = x.shape[1] // 2
    bits = pltpu.bitcast(x.astype(BF16).astype(F32), jnp.int32)
    return bits[:, :half] | lax.shift_right_logical(bits[:, half:], 16)


def _unpack_bf16_pairs(w):
    return pltpu.bitcast(w & jnp.int32(-65536), F32), pltpu.bitcast(lax.shift_left(w, 16), F32)


def _seg_sums(x, m_ref):
    return _dot((x * x).astype(BF16), m_ref[...])


def _seg_inv_rms(ss, inv_cnt, mt2_ref):
    r = lax.rsqrt(ss * inv_cnt + RMS_EPS)
    r_hi = r.astype(BF16)
    r_lo = (r - r_hi.astype(F32)).astype(BF16)
    return _dot(jnp.concatenate([r_hi, r_lo], axis=1), mt2_ref[...])


def _seg_rms_scale(x, m_ref, mt2_ref, inv_cnt):
    return _seg_inv_rms(_seg_sums(x, m_ref), inv_cnt, mt2_ref)


def _rope_key_chunk(kpe_chunk, gkpe_ref, rope):
    ms = jnp.sum(kpe_chunk * kpe_chunk, axis=-1, keepdims=True) * (1.0 / ROPE_DIM)
    kp = kpe_chunk * lax.rsqrt(ms + RMS_EPS) * gkpe_ref[...]
    return kp if rope is None else _rope_chunk(kp, *rope)


def _rope_chunk(xc, cosf, sinf, first_half):
    partner = jnp.where(first_half, pltpu.roll(xc, HEAD_PAD - ROPE_HALF, 1), pltpu.roll(xc, ROPE_HALF, 1))
    return xc * cosf + partner * sinf


def _write_keys(ckv, kpe_chunk, wk_ref, wv_ref, gk_ref, gkpe_ref, mk_ref, mkt2_ref, rope, k_out, v_out):
    cb = ckv.astype(BF16)
    kn = _dot(cb, wk_ref[...])
    kn = kn * _seg_rms_scale(kn, mk_ref, mkt2_ref, 1.0 / QK_NOPE) * gk_ref[...]
    v_out[...] = _dot(cb, wv_ref[...]).astype(v_out.dtype)
    kp = _rope_key_chunk(kpe_chunk, gkpe_ref, rope)
    for h in range(N_HEADS):
        sl = slice(h * HEAD_PAD, (h + 1) * HEAD_PAD)
        k_out[:, sl] = (kn[:, sl] + kp).astype(k_out.dtype)


def _mod_kernel(c_ref, w_ref, b_ref, o_ref):
    c = c_ref[...]
    s = c * jax.nn.sigmoid(c)
    o_ref[...] = _dot(s.astype(BF16), w_ref[...].astype(BF16)) + b_ref[...]


def _modulation(cond8, w_mod, b_mod):
    depth = w_mod.shape[0]
    n = w_mod.shape[2]
    return pl.pallas_call(
        _mod_kernel,
        out_shape=jax.ShapeDtypeStruct((depth, 8, n), F32),
        grid=(depth, n // MOD_COL_TILE),
        in_specs=[
            pl.BlockSpec((8, D_MODEL), lambda l, j: (0, 0)),
            pl.BlockSpec((None, D_MODEL, MOD_COL_TILE), lambda l, j: (l, 0, j)),
            pl.BlockSpec((None, 1, MOD_COL_TILE), lambda l, j: (l, 0, j)),
        ],
        out_specs=pl.BlockSpec((None, 8, MOD_COL_TILE), lambda l, j: (l, 0, j)),
        compiler_params=_cparams("arbitrary", "arbitrary"),
        name="adaln_modulation",
    )(cond8, w_mod, b_mod.reshape(depth, 1, n))


def _moe_residual(tw_ref, x1_ref, mod_ref, yg_ref):
    acc = None
    for k in range(TOP_K):
        w = tw_ref[:, k:k + 1]
        part = [w * half for half in _unpack_bf16_pairs(yg_ref[k])]
        acc = part if acc is None else [a + p for a, p in zip(acc, part)]
    return x1_ref[...] + mod_ref[5:6, :] * jnp.concatenate(acc, axis=1)


def _in_kernel(*refs, after_moe):
    if after_moe:
        x = _moe_residual(*refs[:4])
        refs = refs[4:]
        refs[-1][...] = x
        refs = refs[:-1]
    else:
        x = refs[0][...]
        refs = refs[1:]
    (mod_ref, g1_ref, wa_ref, wb_ref, qag_ref, kvag_ref, wq_ref, wk_ref, wv_ref,
     gq_ref, gk_ref, gkpe_ref, cos_ref, sin_ref, mq_ref, mqt2_ref, icq_ref, mk_ref, mkt2_ref,
     q_out, k_out, v_out, ckv_out, kpe_out, f_out, p_out, g_out) = refs
    shift = mod_ref[0:1, :]
    scale = mod_ref[1:2, :]
    h = _rms(x) * g1_ref[...] * (1.0 + scale) + shift
    hb = h.astype(BF16)

    gate_chunks = iter(range(0, N_BRANCHES * D_MODEL, GATE_COL_CHUNK))

    def gate_chunk():
        c0 = next(gate_chunks)
        zc = _dot(hb, wb_ref[:, MIX_WIDTH + c0:MIX_WIDTH + c0 + GATE_COL_CHUNK])
        g_out[:, c0:c0 + GATE_COL_CHUNK] = jax.nn.sigmoid(zc).astype(g_out.dtype)

    za = _dot(hb, wa_ref[...])
    zb = _dot(hb, wb_ref[:, 0:MIX_WIDTH])
    f_out[...] = zb[:, 0:FNET_WIDTH].astype(f_out.dtype)
    p_out[...] = zb[:, FNET_WIDTH:]
    q_lat = za[:, 0:COL_KV]
    kv_lat = za[:, COL_KV:COL_KPE]
    key_lane = lax.broadcasted_iota(jnp.int32, (hb.shape[0], KPE_PAD), 1) < ROPE_DIM
    kpe_grp = jnp.where(key_lane, za[:, COL_KPE:COL_F], 0.0)
    ckv = _rms(kv_lat) * kvag_ref[...]
    ckv_out[...] = ckv
    kpe_out[...] = kpe_grp[:, 0:ROPE_DIM]
    qn = (_rms(q_lat) * qag_ref[...]).astype(BF16)
    cb = ckv.astype(BF16)
    gate_chunk()

    q = _dot(qn, wq_ref[...])
    kn = _dot(cb, wk_ref[...])
    v_out[...] = _dot(cb, wv_ref[...]).astype(v_out.dtype)
    gate_chunk()

    ss_q = _seg_sums(q, mq_ref)
    ss_k = _seg_sums(kn, mk_ref)
    gate_chunk()

    q = q * _seg_inv_rms(ss_q, icq_ref[...], mqt2_ref) * gq_ref[...]
    kn = kn * _seg_inv_rms(ss_k, 1.0 / QK_NOPE, mkt2_ref) * gk_ref[...]
    for _ in range(N_BRANCHES * D_MODEL // GATE_COL_CHUNK - 3):
        gate_chunk()

    cosf = cos_ref[...]
    sinf = sin_ref[...]
    lane = lax.broadcasted_iota(jnp.int32, cosf.shape, 1)
    rope = (cosf, sinf, lane < QK_NOPE + ROPE_HALF)
    sm_scale = 1.0 / math.sqrt(QK_HEAD)
    kp = _rope_key_chunk(pltpu.roll(kpe_grp, QK_NOPE, 1), gkpe_ref, rope)
    for hd in range(N_HEADS):
        sl = slice(hd * HEAD_PAD, (hd + 1) * HEAD_PAD)
        q_out[:, sl] = (_rope_chunk(q[:, sl], *rope) * sm_scale).astype(q_out.dtype)
        k_out[:, sl] = (kn[:, sl] + kp).astype(k_out.dtype)


def _const_spec(shape):
    nd = len(shape)
    return pl.BlockSpec(shape, lambda i, _n=nd: (0,) * _n, pipeline_mode=pl.Buffered(1))


def _layer_spec(shape, l):
    nd = len(shape)
    return pl.BlockSpec((None,) + tuple(shape), lambda i, _l=l, _n=nd: (_l,) + (0,) * _n,
                        pipeline_mode=pl.Buffered(1))


def _in_projection(x, moe, mods, l, wts, consts, tok):
    after_moe = x is None
    t = moe[0].shape[0] if after_moe else x.shape[0]
    tm = IN_TILE
    n_tiles = t // tm
    p_tiles = tok["t_prompt"] // tm
    s_tiles = tok["dec_seq"] // tm
    rope_blocks = tok["dec_seq"] // tm

    def mod_row(i):
        return jnp.where(i < p_tiles, 0, 1 + (i - p_tiles) // s_tiles)

    def rope_idx(i):
        return (jnp.where(i < p_tiles, rope_blocks, (i - p_tiles) % s_tiles), 0)

    row = lambda w: pl.BlockSpec((tm, w), lambda i: (i, 0))
    if after_moe:
        x1, tw, yg = moe
        lead_specs = [row(ROUTER_PAD), row(D_MODEL),
                      pl.BlockSpec((None, None, 6, D_MODEL), lambda i: (l - 1, mod_row(i), 0, 0)),
                      pl.BlockSpec((TOP_K, tm, D_MODEL // 2), lambda i: (0, i, 0))]
        lead_args = [tw, x1, mods, yg]
    else:
        lead_specs = [row(D_MODEL)]
        lead_args = [x]
    in_specs = lead_specs + [
        pl.BlockSpec((None, None, 6, D_MODEL), lambda i: (l, mod_row(i), 0, 0)),
        _layer_spec((1, D_MODEL), l),
        _layer_spec((D_MODEL, COL_F), l),
        _layer_spec((D_MODEL, MIX_WIDTH + N_BRANCHES * D_MODEL), l),
        _layer_spec((1, Q_RANK), l),
        _layer_spec((1, KV_RANK), l),
        _layer_spec((Q_RANK, QK_PAD), l),
        _layer_spec((KV_RANK, QK_PAD), l),
        _layer_spec((KV_RANK, V_WIDTH), l),
        _layer_spec((1, QK_PAD), l),
        _layer_spec((1, QK_PAD), l),
        _layer_spec((1, HEAD_PAD), l),
        pl.BlockSpec((tm, HEAD_PAD), rope_idx),
        pl.BlockSpec((tm, HEAD_PAD), rope_idx),
        _const_spec((QK_PAD, LANES)),
        _const_spec((2 * LANES, QK_PAD)),
        _const_spec((1, LANES)),
        _const_spec((QK_PAD, LANES)),
        _const_spec((2 * LANES, QK_PAD)),
    ]
    out_shape = [
        jax.ShapeDtypeStruct((t, QK_PAD), BF16),
        jax.ShapeDtypeStruct((t, QK_PAD), BF16),
        jax.ShapeDtypeStruct((t, V_WIDTH), BF16),
        jax.ShapeDtypeStruct((t, KV_RANK), F32),
        jax.ShapeDtypeStruct((t, ROPE_DIM), F32),
        jax.ShapeDtypeStruct((t, FNET_WIDTH), BF16),
        jax.ShapeDtypeStruct((t, POOL_WIDTH), F32),
        jax.ShapeDtypeStruct((t, N_BRANCHES * D_MODEL), BF16),
    ]
    if after_moe:
        out_shape.append(jax.ShapeDtypeStruct((t, D_MODEL), F32))
    out_specs = [row(s.shape[1]) for s in out_shape]
    outs = pl.pallas_call(
        functools.partial(_in_kernel, after_moe=after_moe),
        out_shape=out_shape,
        grid=(n_tiles,),
        in_specs=in_specs,
        out_specs=out_specs,
        compiler_params=_cparams("arbitrary"),
        cost_estimate=pl.CostEstimate(
            flops=2 * t * (D_MODEL * IN_PAD_WIDTH + Q_RANK * QK_PAD + KV_RANK * (QK_PAD + V_WIDTH)
                           + 2 * (QK_PAD * LANES + 2 * LANES * QK_PAD)),
            transcendentals=t * N_BRANCHES * D_MODEL,
            bytes_accessed=2 * D_MODEL * IN_PAD_WIDTH + sum(
                math.prod(s.shape) * s.dtype.itemsize for s in out_shape)
            + 4 * t * D_MODEL * (1 + (TOP_K // 2 if after_moe else 0))),
        name="in_projection",
    )(*lead_args, mods, wts["norm1_g"], wts["w_in_a"], wts["w_in_b"], wts["q_a_g"], wts["kv_a_g"], wts["w_q"], wts["w_k"], wts["w_v"],
      wts["g_q"], wts["g_k"], wts["g_kpe"], consts["cosf"], consts["sinf"],
      consts["m_q"], consts["mt2_q"], consts["inv_cnt_q"], consts["m_k"], consts["mt2_k"])
    return (outs[:8], outs[8]) if after_moe else (outs, x)


def _ctx_keys_kernel(ckv_ref, kpe_ref, wk_ref, wv_ref, gk_ref, gkpe_ref, mk_ref, mkt2_ref, k_out, v_out):
    _write_keys(ckv_ref[...], kpe_ref[...], wk_ref, wv_ref, gk_ref, gkpe_ref, mk_ref, mkt2_ref, None, k_out, v_out)


def _ctx_keys(cache_ckv, cache_kpe_pad, wts, consts):
    nb, depth, past, _ = cache_ckv.shape
    lw = lambda shape: pl.BlockSpec((None,) + shape, lambda l, b: (l,) + (0,) * len(shape))
    cs = lambda shape: pl.BlockSpec(shape, lambda l, b: (0,) * len(shape))
    return pl.pallas_call(
        _ctx_keys_kernel,
        out_shape=[jax.ShapeDtypeStruct((depth, nb * past, QK_PAD), BF16),
                   jax.ShapeDtypeStruct((depth, nb * past, V_WIDTH), BF16)],
        grid=(depth, nb),
        in_specs=[
            pl.BlockSpec((None, None, past, KV_RANK), lambda l, b: (b, l, 0, 0)),
            pl.BlockSpec((None, None, past, HEAD_PAD), lambda l, b: (b, l, 0, 0)),
            lw((KV_RANK, QK_PAD)), lw((KV_RANK, V_WIDTH)), lw((1, QK_PAD)), lw((1, HEAD_PAD)),
            cs((QK_PAD, LANES)), cs((2 * LANES, QK_PAD)),
        ],
        out_specs=[pl.BlockSpec((None, past, QK_PAD), lambda l, b: (l, b, 0)),
                   pl.BlockSpec((None, past, V_WIDTH), lambda l, b: (l, b, 0))],
        compiler_params=_cparams("arbitrary", "arbitrary"),
        name="context_keys",
    )(cache_ckv, cache_kpe_pad, wts["w_k"], wts["w_v"], wts["g_k"], wts["g_kpe"], consts["m_k"], consts["mt2_k"])


def _attn_kernel(*refs, n_parts):
    q_ref = refs[0]
    k_refs = refs[1:1 + 2 * n_parts:2]
    v_refs = refs[2:2 + 2 * n_parts:2]
    o_ref = refs[1 + 2 * n_parts]
    sub = min(ATTN_Q_SUB, q_ref.shape[0])
    lane = lax.broadcasted_iota(jnp.int32, (sub, 2 * V_HEAD), 1)
    for r0 in range(0, q_ref.shape[0], sub):
        rows = slice(r0, r0 + sub)
        for pair in range(N_HEADS // 2):
            vsl = slice(pair * 2 * V_HEAD, (pair + 1) * 2 * V_HEAD)
            outs = []
            for hd in (2 * pair, 2 * pair + 1):
                sl = slice(hd * HEAD_PAD, (hd + 1) * HEAD_PAD)
                qh = q_ref[rows, sl]
                ss = [lax.dot_general(qh, k[:, sl], (((1,), (1,)), ((), ())), preferred_element_type=F32)
                      for k in k_refs]
                m = functools.reduce(jnp.maximum, [jnp.max(s, axis=-1, keepdims=True) for s in ss])
                es = [jnp.exp(s - m) for s in ss]
                den = functools.reduce(jnp.add, [jnp.sum(e, axis=-1, keepdims=True) for e in es])
                acc = functools.reduce(jnp.add, [_dot(e.astype(BF16), v[:, vsl]) for e, v in zip(es, v_refs)])
                outs.append(acc / den)
            o_ref[rows, vsl] = jnp.where(lane < V_HEAD, outs[0], outs[1]).astype(o_ref.dtype)


def _attention(q, kv_parts, n_seq, seq_len):
    tq = min(ATTN_Q_TILE, seq_len)
    nq = seq_len // tq
    n_keys = sum(rows for _, _, _, rows in kv_parts)
    in_specs = [pl.BlockSpec((tq, QK_PAD), lambda b, i: (b * nq + i, 0))]
    args = [q]
    for k, v, layer, rows in kv_parts:
        for arr, width in ((k, QK_PAD), (v, V_WIDTH)):
            if layer is None:
                in_specs.append(pl.BlockSpec((rows, width), lambda b, i: (b, 0)))
            else:
                in_specs.append(pl.BlockSpec((None, rows, width), lambda b, i, _l=layer: (_l, b, 0)))
            args.append(arr)
    return pl.pallas_call(
        functools.partial(_attn_kernel, n_parts=len(kv_parts)),
        out_shape=jax.ShapeDtypeStruct((n_seq * seq_len, V_WIDTH), BF16),
        grid=(n_seq, nq),
        in_specs=in_specs,
        out_specs=pl.BlockSpec((tq, V_WIDTH), lambda b, i: (b * nq + i, 0)),
        compiler_params=_cparams("arbitrary", "arbitrary"),
        cost_estimate=pl.CostEstimate(
            flops=2 * n_seq * seq_len * n_keys * N_HEADS * (HEAD_PAD + 2 * V_HEAD),
            transcendentals=n_seq * seq_len * n_keys * N_HEADS,
            bytes_accessed=2 * n_seq * (seq_len * (QK_PAD + V_WIDTH) + n_keys * (QK_PAD + V_WIDTH))),
        name="attention",
    )(*args)


def _fnet_kernel(f_ref, cs_ref, cl_ref, sl_ref, o_ref, top_ref, bot_ref, *, norm):
    @pl.when(pl.program_id(1) == 0)
    def _():
        for g in range(FNET_GROUPS):
            sl = slice(g * FNET_GC, (g + 1) * FNET_GC)
            a = _dot(f_ref[:, sl], cs_ref[...])
            top_ref[:, sl] = a[:, :FNET_GC].astype(BF16)
            bot_ref[:, sl] = a[:, FNET_GC:].astype(BF16)

    y = _dot(cl_ref[...], top_ref[...]) - _dot(sl_ref[...], bot_ref[...])
    o_ref[...] = (y * norm).astype(o_ref.dtype)


def _fourier(f_in, n_seq, seq_len, row0, tabs):
    tr = min(FNET_ROW_TILE, seq_len)
    nj = seq_len // tr
    sb0 = row0 // seq_len
    return pl.pallas_call(
        functools.partial(_fnet_kernel, norm=1.0 / math.sqrt(seq_len * FNET_GC)),
        out_shape=jax.ShapeDtypeStruct((n_seq * seq_len, FNET_WIDTH), BF16),
        grid=(n_seq, nj),
        in_specs=[
            pl.BlockSpec((seq_len, FNET_WIDTH), lambda b, j: (sb0 + b, 0)),
            pl.BlockSpec((FNET_GC, 2 * FNET_GC), lambda b, j: (0, 0)),
            pl.BlockSpec((tr, seq_len), lambda b, j: (j, 0)),
            pl.BlockSpec((tr, seq_len), lambda b, j: (j, 0)),
        ],
        out_specs=pl.BlockSpec((tr, FNET_WIDTH), lambda b, j: (b * nj + j, 0)),
        scratch_shapes=[pltpu.VMEM((seq_len, FNET_WIDTH), BF16), pltpu.VMEM((seq_len, FNET_WIDTH), BF16)],
        compiler_params=_cparams("arbitrary", "arbitrary"),
        name="fourier_mix",
    )(f_in, tabs["chan"], tabs["cos"], tabs["sin"])


def _pool_kernel(p_ref, wg_ref, ps_ref, o_ref, pad_ref):
    seq_len = p_ref.shape[0]
    zeros = jnp.zeros((POOL_HALO, POOL_WIDTH), F32)
    pad_ref[0:POOL_HALO, :] = zeros
    pad_ref[POOL_HALO + seq_len:, :] = zeros
    pad_ref[POOL_HALO:POOL_HALO + seq_len, :] = p_ref[...]
    ch = min(POOL_CHUNK, seq_len)
    for c in range(seq_len // ch):
        t = lax.broadcasted_iota(jnp.int32, (ch, 1), 0) + c * ch
        for g, w in enumerate(POOL_WINDOWS):
            half = w // 2
            sl = slice(g * POOL_GC, (g + 1) * POOL_GC)
            acc = None
            for j in range(-half, half):
                r0 = POOL_HALO + c * ch + j
                part = pad_ref[r0:r0 + ch, sl]
                acc = part if acc is None else acc + part
            cnt = (jnp.minimum(t + half, seq_len) - jnp.maximum(t - half, 0)).astype(F32)
            pooled = acc / cnt - p_ref[c * ch:(c + 1) * ch, sl]
            mixed = _dot(pooled.astype(BF16), wg_ref[g]) * ps_ref[:, sl]
            o_ref[c * ch:(c + 1) * ch, sl] = mixed.astype(o_ref.dtype)


def _pooling(p_in, n_seq, seq_len, row0, l, wts):
    sb0 = row0 // seq_len
    g = len(POOL_WINDOWS)
    return pl.pallas_call(
        _pool_kernel,
        out_shape=jax.ShapeDtypeStruct((n_seq * seq_len, POOL_WIDTH), BF16),
        grid=(n_seq,),
        in_specs=[
            pl.BlockSpec((seq_len, POOL_WIDTH), lambda b: (sb0 + b, 0)),
            pl.BlockSpec((None, g, POOL_GC, POOL_GC), lambda b: (l, 0, 0, 0)),
            pl.BlockSpec((None, 1, POOL_WIDTH), lambda b: (l, 0, 0)),
        ],
        out_specs=pl.BlockSpec((seq_len, POOL_WIDTH), lambda b: (b, 0)),
        scratch_shapes=[pltpu.VMEM((seq_len + 2 * POOL_HALO, POOL_WIDTH), F32)],
        compiler_params=_cparams("arbitrary"),
        name="pool_mix",
    )(p_in, wts["w_pool_grp"], wts["pool_scale"])


def _merge_kernel(x_ref, mod_ref, a_ref, f_ref, p_ref, g_ref,
                  wa_ref, wf_ref, wp_ref, wo_ref, g2_ref, wr_ref, br_ref, tri_ref,
                  x1_out, hp_out, te_out, tw_out, rk_out, cnt_out, mix_scr, carry_ref):
    j = pl.program_id(0)

    @pl.when(j == 0)
    def _():
        mix_scr[1] = jnp.zeros(mix_scr.shape[1:], F32)

    @pl.when(j <= 1)
    def _():
        carry_ref[...] = jnp.zeros(carry_ref.shape, F32)

    def branch_chunks(lo, hi):
        out = []
        for c0 in range(lo, hi, MERGE_COL_CHUNK):
            cs = slice(c0, c0 + MERGE_COL_CHUNK)
            a = _dot(a_ref[...], wa_ref[:, cs])
            f = _dot(f_ref[...], wf_ref[:, cs])
            p = _dot(p_ref[...], wp_ref[:, cs])
            out.append((g_ref[:, c0:c0 + MERGE_COL_CHUNK].astype(F32) * a
                        + g_ref[:, D_MODEL + c0:D_MODEL + c0 + MERGE_COL_CHUNK].astype(F32) * f
                        + g_ref[:, 2 * D_MODEL + c0:2 * D_MODEL + c0 + MERGE_COL_CHUNK].astype(F32) * p
                        ).astype(BF16))
        return out

    def step(cur, prev):
        chunks = branch_chunks(0, D_MODEL // 2)

        gate1 = mod_ref[2:3, :]
        shift2 = mod_ref[3:4, :]
        scale2 = mod_ref[4:5, :]
        x1 = x_ref[...] + gate1 * mix_scr[prev]
        x1_out[...] = x1
        h2 = _rms(x1) * g2_ref[...] * (1.0 + scale2) + shift2
        hp_out[...] = _pack_bf16_pairs(h2)
        h_hi = h2.astype(BF16)
        h_lo = (h2 - h_hi.astype(F32)).astype(BF16)
        logits = _dot(jnp.concatenate([h_hi, h_lo, h_hi], axis=1), wr_ref[...]) + br_ref[...]

        chunks += branch_chunks(D_MODEL // 2, D_MODEL)
        mix_scr[cur] = _dot(jnp.concatenate(chunks, axis=1), wo_ref[...])

        lane = lax.broadcasted_iota(jnp.int32, logits.shape, 1)
        work = logits
        vals, idxs = [], []
        for _ in range(TOP_K):
            m = jnp.max(work, axis=-1, keepdims=True)
            idx = jnp.min(jnp.where(work == m, lane, ROUTER_PAD), axis=-1, keepdims=True)
            vals.append(m)
            idxs.append(idx)
            work = jnp.where(lane == idx, -jnp.inf, work)
        es = [jnp.exp(v - vals[0]) for v in vals]
        den = functools.reduce(jnp.add, es)

        chosen = functools.reduce(jnp.logical_or, [lane == idx for idx in idxs])
        hot = jnp.where(chosen, 1.0, 0.0)
        before = _dot(tri_ref[...], hot.astype(BF16)) + carry_ref[...]
        carry_ref[...] = carry_ref[...] + jnp.sum(hot, axis=0, keepdims=True)
        cnt_out[...] = carry_ref[...].astype(jnp.int32)

        te = jnp.zeros(logits.shape, jnp.int32)
        tw = jnp.zeros(logits.shape, F32)
        rk = jnp.zeros(logits.shape, jnp.int32)
        for k in range(TOP_K):
            rank_k = jnp.sum(jnp.where(lane == idxs[k], before, 0.0), axis=-1, keepdims=True).astype(jnp.int32)
            te = jnp.where(lane == k, idxs[k], te)
            tw = jnp.where(lane == k, es[k] / den, tw)
            rk = jnp.where(lane == k, rank_k, rk)
        te_out[...] = te
        tw_out[...] = tw
        rk_out[...] = rk

    for parity in (0, 1):
        pl.when(j % 2 == parity)(functools.partial(step, parity, 1 - parity))


def _merge(x, mods, attn, fnet, pool, gates, l, wts, consts, tok):
    t = x.shape[0]
    tm = TOKEN_TILE
    p_tiles = tok["t_prompt"] // tm
    s_tiles = tok["dec_seq"] // tm

    n_tiles = t // tm
    routed = lambda j: jnp.maximum(j - 1, 0)
    mixed = lambda j: jnp.minimum(j, n_tiles - 1)

    def mod_idx(j):
        i = routed(j)
        row = jnp.where(i < p_tiles, 0, 1 + (i - p_tiles) // s_tiles)
        return (l, row, 0, 0)

    row = lambda w: pl.BlockSpec((tm, w), lambda j: (routed(j), 0))
    head = lambda w: pl.BlockSpec((tm, w), lambda j: (mixed(j), 0))
    out_shape = [
        jax.ShapeDtypeStruct((t, D_MODEL), F32),
        jax.ShapeDtypeStruct((t, D_MODEL // 2), jnp.int32),
        jax.ShapeDtypeStruct((t, ROUTER_PAD), jnp.int32),
        jax.ShapeDtypeStruct((t, ROUTER_PAD), F32),
        jax.ShapeDtypeStruct((t, ROUTER_PAD), jnp.int32),
    ]
    return pl.pallas_call(
        _merge_kernel,
        out_shape=out_shape + [jax.ShapeDtypeStruct((1, ROUTER_PAD), jnp.int32)],
        grid=(n_tiles + 1,),
        in_specs=[
            row(D_MODEL),
            pl.BlockSpec((None, None, 6, D_MODEL), mod_idx),
            head(V_WIDTH), head(FNET_WIDTH), head(POOL_WIDTH), head(N_BRANCHES * D_MODEL),
            _layer_spec((V_WIDTH, D_MODEL), l),
            _layer_spec((FNET_WIDTH, D_MODEL), l),
            _layer_spec((POOL_WIDTH, D_MODEL), l),
            _layer_spec((D_MODEL, D_MODEL), l),
            _layer_spec((1, D_MODEL), l),
            _layer_spec((3 * D_MODEL, ROUTER_PAD), l),
            _layer_spec((1, ROUTER_PAD), l),
            _const_spec((tm, tm)),
        ],
        out_specs=[row(s.shape[1]) for s in out_shape] + [pl.BlockSpec((1, ROUTER_PAD), lambda j: (0, 0))],
        scratch_shapes=[pltpu.VMEM((2, tm, D_MODEL), F32), pltpu.VMEM((1, ROUTER_PAD), F32)],
        compiler_params=_cparams("arbitrary"),
        cost_estimate=pl.CostEstimate(
            flops=2 * t * (D_MODEL * (V_WIDTH + FNET_WIDTH + POOL_WIDTH + D_MODEL) + 3 * D_MODEL * ROUTER_PAD
                           + tm * ROUTER_PAD),
            transcendentals=t * TOP_K,
            bytes_accessed=t * (2 * (V_WIDTH + FNET_WIDTH + POOL_WIDTH + N_BRANCHES * D_MODEL) + 4 * 2 * D_MODEL
                                + 4 * (D_MODEL // 2) + 3 * 4 * ROUTER_PAD)
            + 2 * D_MODEL * (V_WIDTH + FNET_WIDTH + POOL_WIDTH + D_MODEL + 3 * ROUTER_PAD)),
        name="merge_router",
    )(x, mods, attn, fnet, pool, gates, wts["w_attn_o"], wts["w_fnet_o"], wts["w_pool_o"], wts["w_out"],
      wts["norm2_g"], wts["w_router3"], wts["b_router"], consts["tri"])


def _sc_row_gather(table, idx):
    n = idx.shape[0]
    d = table.shape[1]
    info = plsc.get_sparse_core_info()
    n_cores = info.num_cores
    n_workers = n_cores * info.num_subcores
    per_worker = n // n_workers
    n_chunks = per_worker // SC_GATHER_ROWS
    assert n_chunks * SC_GATHER_ROWS * n_workers == n
    mesh = plsc.VectorSubcoreMesh(core_axis_name="core", subcore_axis_name="subcore")

    @functools.partial(
        pl.kernel, mesh=mesh, out_type=jax.ShapeDtypeStruct((n, d), table.dtype),
        scratch_types=[pltpu.VMEM((SC_GATHER_ROWS,), jnp.int32), pltpu.VMEM((SC_GATHER_ROWS, d), table.dtype),
                       pltpu.SemaphoreType.DMA],
        cost_estimate=pl.CostEstimate(flops=0, transcendentals=0,
                                      bytes_accessed=2 * n * d * table.dtype.itemsize + 4 * n),
        name="sc_row_gather")
    def gather(table_hbm, idx_hbm, out_hbm, idx_v, rows_v, sem):
        worker = lax.axis_index("subcore") * n_cores + lax.axis_index("core")

        @pl.loop(0, n_chunks)
        def _(c):
            base = worker * per_worker + c * SC_GATHER_ROWS
            pltpu.sync_copy(idx_hbm.at[pl.ds(base, SC_GATHER_ROWS)], idx_v)
            pltpu.async_copy(table_hbm.at[idx_v], rows_v, sem).wait()
            pltpu.sync_copy(rows_v, out_hbm.at[pl.ds(base, SC_GATHER_ROWS)])

    return gather(table, idx)


PLAN_EXPERT, PLAN_VALID, PLAN_FIRST, PLAN_SLOT, PLAN_NEXT, PLAN_USED = range(6)


def _expert_kernel(plan_ref, xs_ref, bgu_ref, bd_ref, wgu_hbm, wd_hbm, y_ref,
                   wgu_f32, wd_f32, wgu_bf, wd_bf, sems, *, layer):
    b = pl.program_id(0)

    def weight_copies(expert, slot):
        return (pltpu.make_async_copy(wgu_hbm.at[layer, expert], wgu_f32.at[slot], sems.at[0, slot]),
                pltpu.make_async_copy(wd_hbm.at[layer, expert], wd_f32.at[slot], sems.at[1, slot]))

    def ffn(rows):
        x = jnp.concatenate(_unpack_bf16_pairs(xs_ref[0:rows, :]), axis=1).astype(BF16)
        gu = _dot(x, wgu_bf[...]) + bgu_ref[...]
        glu = jnp.minimum(gu[:, :D_FF], SWIGLU_LIMIT)
        lin = jnp.clip(gu[:, D_FF:], -SWIGLU_LIMIT, SWIGLU_LIMIT)
        act = glu * jax.nn.sigmoid(SWIGLU_ALPHA * glu) * (lin + 1.0)
        y_ref[0:rows, :] = _pack_bf16_pairs(_dot(act.astype(BF16), wd_bf[...]) + bd_ref[...])
        if rows < MOE_BLOCK:
            y_ref[rows:, :] = jnp.zeros((MOE_BLOCK - rows, D_MODEL // 2), y_ref.dtype)

    n_used = plan_ref[PLAN_USED, 0]

    @pl.when(b < n_used)
    def _():
        slot = plan_ref[PLAN_SLOT, b]

        @pl.when(b == 0)
        def _():
            for cp in weight_copies(plan_ref[PLAN_EXPERT, 0], 0):
                cp.start()

        @pl.when(plan_ref[PLAN_FIRST, b] == 1)
        def _():
            for cp in weight_copies(plan_ref[PLAN_EXPERT, b], slot):
                cp.wait()
            wgu_bf[...] = wgu_f32[slot].astype(BF16)
            wd_bf[...] = wd_f32[slot].astype(BF16)

            @pl.when(plan_ref[PLAN_NEXT, b] >= 0)
            def _():
                for cp in weight_copies(plan_ref[PLAN_NEXT, b], 1 - slot):
                    cp.start()

        steps = (plan_ref[PLAN_VALID, b] + (MOE_STEP - 1)) // MOE_STEP
        for rows in range(MOE_STEP, MOE_BLOCK + 1, MOE_STEP):
            want = steps <= 1 if rows == MOE_STEP else steps == rows // MOE_STEP
            pl.when(want)(functools.partial(ffn, rows))

    @pl.when(b >= n_used)
    def _():
        y_ref[...] = jnp.zeros(y_ref.shape, y_ref.dtype)


def _experts(xs, plan, l, w_gu, b_gu4, w_down, b_down4):
    n_slots = xs.shape[0]
    bm = MOE_BLOCK
    n_blocks = n_slots // bm

    def blk(b, plan):
        return jnp.minimum(b, plan[PLAN_USED, 0] - 1)

    def bspec(width):
        return pl.BlockSpec((None, None, 1, width), lambda b, plan: (l, plan[PLAN_EXPERT, blk(b, plan)], 0, 0))

    grid_spec = pltpu.PrefetchScalarGridSpec(
        num_scalar_prefetch=1,
        grid=(n_blocks,),
        in_specs=[
            pl.BlockSpec((bm, D_MODEL // 2), lambda b, plan: (blk(b, plan), 0)),
            bspec(2 * D_FF), bspec(D_MODEL),
            pl.BlockSpec(memory_space=pl.ANY), pl.BlockSpec(memory_space=pl.ANY),
        ],
        out_specs=pl.BlockSpec((bm, D_MODEL // 2), lambda b, plan: (b, 0)),
        scratch_shapes=[pltpu.VMEM((2, D_MODEL, 2 * D_FF), F32), pltpu.VMEM((2, D_FF, D_MODEL), F32),
                        pltpu.VMEM((D_MODEL, 2 * D_FF), BF16), pltpu.VMEM((D_FF, D_MODEL), BF16),
                        pltpu.SemaphoreType.DMA((2, 2))],
    )
    return pl.pallas_call(
        functools.partial(_expert_kernel, layer=l),
        out_shape=jax.ShapeDtypeStruct((n_slots, D_MODEL // 2), jnp.int32),
        grid_spec=grid_spec,
        compiler_params=_cparams("arbitrary"),
        cost_estimate=pl.CostEstimate(
            flops=2 * n_slots * 3 * D_MODEL * D_FF, transcendentals=n_slots * D_FF,
            bytes_accessed=4 * N_EXPERTS * 3 * D_MODEL * D_FF + 2 * 4 * n_slots * (D_MODEL // 2)),
        name="moe_experts",
    )(plan, xs, b_gu4, b_down4, w_gu, w_down)


def _combine_kernel(tw_ref, x_ref, mod_ref, yg_ref, o_ref):
    o_ref[...] = _moe_residual(tw_ref, x_ref, mod_ref, yg_ref)


def _combine(yg, tw, x1, mods, l, tok):
    t = x1.shape[0]
    tc = COMBINE_TILE
    p_tiles = tok["t_prompt"] // tc
    s_tiles = tok["dec_seq"] // tc

    def mod_idx(i):
        row = jnp.where(i < p_tiles, 0, 1 + (i - p_tiles) // s_tiles)
        return (l, row, 0, 0)

    return pl.pallas_call(
        _combine_kernel,
        out_shape=jax.ShapeDtypeStruct((t, D_MODEL), F32),
        grid=(t // tc,),
        in_specs=[
            pl.BlockSpec((tc, ROUTER_PAD), lambda i: (i, 0)),
            pl.BlockSpec((tc, D_MODEL), lambda i: (i, 0)),
            pl.BlockSpec((None, None, 6, D_MODEL), mod_idx),
            pl.BlockSpec((TOP_K, tc, D_MODEL // 2), lambda i: (0, i, 0)),
        ],
        out_specs=pl.BlockSpec((tc, D_MODEL), lambda i: (i, 0)),
        compiler_params=_cparams("arbitrary"),
        name="moe_combine",
    )(tw, x1, mods, yg)


def _slot_tokens(slot, counts, padded_start, padded_end, n_slots, t):
    n = slot.shape[0]
    j = jnp.arange(MOE_BLOCK, dtype=jnp.int32)
    pad_key = (padded_start + counts)[:, None] + j[None, :]
    pad_key = jnp.where(pad_key < padded_end[:, None], pad_key, n_slots).reshape(N_EXPERTS * MOE_BLOCK)
    pad_val = jnp.arange(N_EXPERTS * MOE_BLOCK, dtype=jnp.int32) % t
    keys = jnp.concatenate([slot, pad_key])
    vals = jnp.concatenate([jnp.arange(n, dtype=jnp.int32) // TOP_K, pad_val])
    tok_bits = max(1, (t - 1).bit_length())
    assert (n_slots + 1) << tok_bits < 2 ** 31
    packed = jnp.sort(lax.shift_left(keys, tok_bits) | vals)
    return packed[:n_slots] & ((1 << tok_bits) - 1)


def _route(te_pad, rank_pad, counts_pad, n_slots):
    t = te_pad.shape[0]
    flat_e = te_pad[:, :TOP_K].reshape(t * TOP_K)
    onehot = (flat_e[:, None] == jnp.arange(N_EXPERTS, dtype=jnp.int32)[None, :]).astype(jnp.int32)
    rank = rank_pad[:, :TOP_K].reshape(t * TOP_K)
    counts = counts_pad[0, :N_EXPERTS]
    padded = (counts + MOE_BLOCK - 1) // MOE_BLOCK * MOE_BLOCK
    padded_end = jnp.cumsum(padded)
    padded_start = padded_end - padded
    slot = jnp.sum(onehot * padded_start[None, :], axis=1) + rank
    n_blocks = n_slots // MOE_BLOCK
    block_row0 = jnp.arange(n_blocks, dtype=jnp.int32) * MOE_BLOCK
    block_exp = jnp.minimum(
        jnp.sum((padded_end[None, :] <= block_row0[:, None]).astype(jnp.int32), axis=1),
        N_EXPERTS - 1).astype(jnp.int32)
    exp_hot = (block_exp[:, None] == jnp.arange(N_EXPERTS, dtype=jnp.int32)[None, :]).astype(jnp.int32)
    rows_end = jnp.sum(exp_hot * (padded_start + counts)[None, :], axis=1)
    block_valid = jnp.clip(rows_end - block_row0, 0, MOE_BLOCK).astype(jnp.int32)
    n_used = (padded_end[-1] // MOE_BLOCK).astype(jnp.int32)
    block_ids = jnp.arange(n_blocks, dtype=jnp.int32)
    prev_exp = jnp.concatenate([jnp.full((1,), -1, jnp.int32), block_exp[:-1]])
    first = ((block_exp != prev_exp) & (block_ids < n_used)).astype(jnp.int32)
    buf_slot = (jnp.cumsum(first) - 1) % 2
    experts = jnp.arange(N_EXPERTS, dtype=jnp.int32)
    holder = jnp.where(counts > 0, experts, N_EXPERTS)
    later = lax.cummin(jnp.concatenate([holder[1:], jnp.full((1,), N_EXPERTS, jnp.int32)]), reverse=True)
    next_exp = jnp.where(later < N_EXPERTS, later, -1)
    block_next = jnp.sum(exp_hot * next_exp[None, :], axis=1)
    plan = jnp.stack([block_exp, block_valid, first, buf_slot, block_next,
                      jnp.full((n_blocks,), n_used, jnp.int32)]).astype(jnp.int32)
    slot = slot.astype(jnp.int32)
    slot_tok = _slot_tokens(slot, counts, padded_start, padded_end, n_slots, t)
    slot_by_choice = slot.reshape(t, TOP_K).T.reshape(TOP_K * t)
    return slot_by_choice, slot_tok, plan


def _segment_matrices():
    m_q = np.zeros((QK_PAD, LANES), np.float32)
    m_k = np.zeros((QK_PAD, LANES), np.float32)
    inv_cnt_q = np.ones((1, LANES), np.float32)
    for h in range(N_HEADS):
        m_q[h * HEAD_PAD:h * HEAD_PAD + QK_NOPE, 2 * h] = 1.0
        m_q[h * HEAD_PAD + QK_NOPE:h * HEAD_PAD + QK_HEAD, 2 * h + 1] = 1.0
        inv_cnt_q[0, 2 * h] = 1.0 / QK_NOPE
        inv_cnt_q[0, 2 * h + 1] = 1.0 / ROPE_DIM
        m_k[h * HEAD_PAD:h * HEAD_PAD + QK_NOPE, h] = 1.0
    dup = lambda m: np.concatenate([m.T, m.T], axis=0)
    return dict(m_q=jnp.asarray(m_q, BF16), mt2_q=jnp.asarray(dup(m_q), BF16), inv_cnt_q=jnp.asarray(inv_cnt_q),
                m_k=jnp.asarray(m_k, BF16), mt2_k=jnp.asarray(dup(m_k), BF16))


def _rope_lane_tables(n_tokens, ident_rows):
    rows = n_tokens // GRID_W
    row = jnp.broadcast_to(jnp.arange(rows)[:, None], (rows, GRID_W)).reshape(n_tokens)
    col = jnp.broadcast_to(jnp.arange(GRID_W)[None, :], (rows, GRID_W)).reshape(n_tokens)
    n_freq = ROPE_DIM // 4
    inv = 1.0 / (ROPE_THETA ** (jnp.arange(n_freq, dtype=F32) / n_freq))
    ang = jnp.concatenate([row[:, None].astype(F32) * inv, col[:, None].astype(F32) * inv], axis=-1)
    cos, sin = jnp.cos(ang), jnp.sin(ang)
    ones = jnp.ones((n_tokens, QK_NOPE), F32)
    tail = HEAD_PAD - QK_HEAD
    cosf = jnp.concatenate([ones, cos, cos, jnp.ones((n_tokens, tail), F32)], axis=1)
    sinf = jnp.concatenate([0.0 * ones, -sin, sin, jnp.zeros((n_tokens, tail), F32)], axis=1)
    cosf = jnp.concatenate([cosf, jnp.ones((ident_rows, HEAD_PAD), F32)], axis=0)
    sinf = jnp.concatenate([sinf, jnp.zeros((ident_rows, HEAD_PAD), F32)], axis=0)
    return cosf, sinf


def _dft_tables(seq_len):
    def cs(n):
        k = np.arange(n, dtype=np.int64)
        ang = 2.0 * np.pi * ((k[:, None] * k[None, :]) % n).astype(np.float64) / n
        return np.cos(ang).astype(np.float32), np.sin(ang).astype(np.float32)

    cl, sl = cs(seq_len)
    cc, sc = cs(FNET_GC)
    return dict(cos=jnp.asarray(cl).astype(BF16), sin=jnp.asarray(sl).astype(BF16),
                chan=jnp.asarray(np.concatenate([cc, sc], axis=1)).astype(BF16))


def _layout_weights(w_in, w_q_b, w_kv_b, q_nope_g, q_rope_g, k_nope_g, k_rope_g, w_router, b_router):
    depth = w_in.shape[0]
    w_in_a = w_in[:, :, :COL_F].astype(BF16)
    w_in_b = w_in[:, :, COL_KPE + ROPE_DIM:].astype(BF16)
    w_q = jnp.pad(w_q_b.reshape(depth, Q_RANK, N_HEADS, QK_HEAD),
                  ((0, 0), (0, 0), (0, 0), (0, HEAD_PAD - QK_HEAD))).reshape(depth, Q_RANK, QK_PAD).astype(BF16)
    kv = w_kv_b.reshape(depth, KV_RANK, N_HEADS, QK_NOPE + V_HEAD)
    w_k = jnp.pad(kv[..., :QK_NOPE], ((0, 0), (0, 0), (0, 0), (0, HEAD_PAD - QK_NOPE))
                  ).reshape(depth, KV_RANK, QK_PAD).astype(BF16)
    w_v = kv[..., QK_NOPE:].reshape(depth, KV_RANK, V_WIDTH).astype(BF16)
    zq = jnp.zeros((depth, HEAD_PAD - QK_HEAD), F32)
    g_q = jnp.tile(jnp.concatenate([q_nope_g, q_rope_g, zq], axis=1), (1, N_HEADS))[:, None, :]
    g_k = jnp.tile(jnp.concatenate([k_nope_g, jnp.zeros((depth, HEAD_PAD - QK_NOPE), F32)], axis=1),
                   (1, N_HEADS))[:, None, :]
    g_kpe = jnp.concatenate([jnp.zeros((depth, QK_NOPE), F32), k_rope_g, zq], axis=1)[:, None, :]
    w_hi = w_router.astype(BF16)
    w_lo = (w_router - w_hi.astype(F32)).astype(BF16)
    w_r3 = jnp.pad(jnp.concatenate([w_hi, w_hi, w_lo], axis=1), ((0, 0), (0, 0), (0, ROUTER_PAD - N_EXPERTS)))
    b_r = jnp.pad(b_router, ((0, 0), (0, ROUTER_PAD - N_EXPERTS)), constant_values=NEG_BIG)[:, None, :]
    return dict(w_in_a=w_in_a, w_in_b=w_in_b, w_q=w_q, w_k=w_k, w_v=w_v, g_q=g_q, g_k=g_k, g_kpe=g_kpe, w_router3=w_r3, b_router=b_r)


def kernel(x_prompt, x_sample, cache_ckv, cache_kpe, c, c_ctx, w_mod, b_mod, norm1_g, norm2_g, w_in, q_a_g, kv_a_g, w_q_b, w_kv_b, q_nope_g, q_rope_g, k_nope_g, k_rope_g, w_attn_o, w_fnet_o, w_pool_grp, pool_scale, w_pool_o, w_out, w_router, b_router, w_gu, b_gu, w_down, b_down):
    batch, seq, d = x_prompt.shape
    dec_batch, dec_seq, _ = x_sample.shape
    depth = w_mod.shape[0]
    past = cache_ckv.shape[2]
    t_prompt = batch * seq
    t_sample = dec_batch * dec_seq
    assert d == D_MODEL and dec_batch + 1 <= 8
    assert t_prompt % TOKEN_TILE == 0 and dec_seq % TOKEN_TILE == 0 and seq % COMBINE_TILE == 0
    assert dec_seq % GRID_W == 0

    consts = _segment_matrices()
    consts["cosf"], consts["sinf"] = _rope_lane_tables(dec_seq, IN_TILE)
    consts["tri"] = jnp.asarray(np.tril(np.ones((TOKEN_TILE, TOKEN_TILE), np.float32), -1), BF16)

    wts = _layout_weights(w_in, w_q_b, w_kv_b, q_nope_g, q_rope_g, k_nope_g, k_rope_g, w_router, b_router)
    row3 = lambda a: a[:, None, :]
    wts.update(norm1_g=row3(norm1_g), norm2_g=row3(norm2_g), q_a_g=row3(q_a_g), kv_a_g=row3(kv_a_g),
               w_attn_o=w_attn_o.astype(BF16), w_fnet_o=w_fnet_o.astype(BF16), w_pool_o=w_pool_o.astype(BF16),
               w_out=w_out.astype(BF16), w_pool_grp=w_pool_grp.astype(BF16), pool_scale=row3(pool_scale))
    b_gu4 = b_gu[:, :, None, :]
    b_down4 = b_down[:, :, None, :]

    cond8 = jnp.concatenate([c_ctx[None, :], c, jnp.zeros((8 - 1 - dec_batch, d), F32)], axis=0)
    mods = _modulation(cond8, w_mod, b_mod).reshape(depth, 8, 6, D_MODEL)

    kpe_pad = jnp.pad(cache_kpe, ((0, 0), (0, 0), (0, 0), (QK_NOPE, HEAD_PAD - QK_HEAD)))
    k_ctx, v_ctx = _ctx_keys(cache_ckv, kpe_pad, wts, consts)

    streams = [
        dict(x=x_prompt.reshape(t_prompt, d), n_seq=batch, seq_len=seq, tok=dict(t_prompt=t_prompt, dec_seq=dec_seq),
             dft=_dft_tables(seq), ctx=None, moe=None),
        dict(x=x_sample.reshape(t_sample, d), n_seq=dec_batch, seq_len=dec_seq, tok=dict(t_prompt=0, dec_seq=dec_seq),
             dft=_dft_tables(dec_seq), ctx=(k_ctx, v_ctx, past), moe=None),
    ]
    ckv_list, kpe_list = [], []

    def mixers(st, l):
        moe = None if st["x"] is not None else (st["x1"], st["tw"], st["yg"])
        (q, k, v, ckv, kpe, f_in, p_in, st["gates"]), st["x"] = _in_projection(
            st["x"], moe, mods, l, wts, consts, st["tok"])
        if st["ctx"] is None:
            ckv_list.append(ckv.reshape(batch, seq, KV_RANK))
            kpe_list.append(kpe.reshape(batch, seq, ROPE_DIM))
        parts = [(k, v, None, st["seq_len"])]
        if st["ctx"] is not None:
            parts.append((st["ctx"][0], st["ctx"][1], l, st["ctx"][2]))
        st["attn"] = _attention(q, parts, st["n_seq"], st["seq_len"])
        st["fnet"] = _fourier(f_in, st["n_seq"], st["seq_len"], 0, st["dft"])
        st["pool"] = _pooling(p_in, st["n_seq"], st["seq_len"], 0, l, wts)

    def merge_and_route(st, l):
        t = st["n_seq"] * st["seq_len"]
        n_slots = (t * TOP_K + N_EXPERTS * (MOE_BLOCK - 1) + MOE_BLOCK - 1) // MOE_BLOCK * MOE_BLOCK
        st["x1"], st["h2p"], te_pad, st["tw"], rank_pad, counts_pad = _merge(
            st["x"], mods, st["attn"], st["fnet"], st["pool"], st["gates"], l, wts, consts, st["tok"])
        st["slot_by_choice"], st["slot_tok"], st["plan"] = _route(te_pad, rank_pad, counts_pad, n_slots)

    def dispatch(st):
        st["xs"] = _sc_row_gather(st["h2p"], st["slot_tok"])

    def experts(st, l):
        st["y"] = _experts(st["xs"], st["plan"], l, w_gu, b_gu4, w_down, b_down4)

    def gather_back(st):
        t = st["n_seq"] * st["seq_len"]
        st["yg"] = _sc_row_gather(st["y"], st["slot_by_choice"]).reshape(TOP_K, t, D_MODEL // 2)
        st["x"] = None

    def mixer_input(st):
        return "x" if st["x"] is not None else "yg"

    def tie(sa, ka, sb, kb):
        sa[ka], sb[kb] = lax.optimization_barrier((sa[ka], sb[kb]))

    a, b = streams
    mixers(a, 0)
    merge_and_route(a, 0)
    tie(b, mixer_input(b), a, "slot_tok")
    dispatch(a)
    mixers(b, 0)
    tie(a, "xs", b, "attn")
    experts(a, 0)
    for l in range(depth):
        tie(b, "x", a, "y")
        gather_back(a)
        merge_and_route(b, l)
        tie(a, "yg", b, "slot_tok")
        dispatch(b)
        if l + 1 == depth:
            break
        mixers(a, l + 1)
        tie(b, "xs", a, "attn")
        experts(b, l)
        tie(a, "x", b, "y")
        gather_back(b)
        merge_and_route(a, l + 1)
        tie(b, "yg", a, "slot_tok")
        dispatch(a)
        mixers(b, l + 1)
        tie(a, "xs", b, "attn")
        experts(a, l + 1)
    a["out"] = _combine(a["yg"], a["tw"], a["x1"], mods, depth - 1, a["tok"])
    tie(b, "xs", a, "out")
    experts(b, depth - 1)
    gather_back(b)
    out_a = a["out"]
    out_b = _combine(b["yg"], b["tw"], b["x1"], mods, depth - 1, b["tok"])
    y_prompt = out_a.reshape(batch, seq, d)
    y_sample = out_b.reshape(dec_batch, dec_seq, d)
    return (y_prompt, y_sample, jnp.stack(ckv_list, axis=1), jnp.stack(kpe_list, axis=1))
```

```python
import functools
import math

import numpy as np
import jax
import jax.numpy as jnp
from jax import lax
from jax.experimental import pallas as pl
from jax.experimental.pallas import tpu as pltpu
from jax.experimental.pallas import tpu_sc as plsc

D_MODEL = 1024
GRID_W = 64
N_HEADS = 8
QK_NOPE = 64
ROPE_DIM = 32
V_HEAD = 64
QK_HEAD = QK_NOPE + ROPE_DIM
Q_RANK = 384
KV_RANK = 256
ROPE_THETA = 10000.0
FNET_GROUPS = 4
FNET_GC = 128
FNET_WIDTH = FNET_GROUPS * FNET_GC
POOL_WINDOWS = (2, 4, 8, 16)
POOL_GC = 128
POOL_WIDTH = len(POOL_WINDOWS) * POOL_GC
N_BRANCHES = 3
N_EXPERTS = 32
TOP_K = 4
D_FF = D_MODEL
SWIGLU_LIMIT = 7.0
SWIGLU_ALPHA = 1.702
RMS_EPS = 1e-6

LANES = 128
HEAD_PAD = LANES
QK_PAD = N_HEADS * HEAD_PAD
V_WIDTH = N_HEADS * V_HEAD
ROPE_HALF = ROPE_DIM // 2
KPE_PAD = LANES
COL_KV = Q_RANK
COL_KPE = Q_RANK + KV_RANK
COL_F = COL_KPE + KPE_PAD
COL_P = COL_F + FNET_WIDTH
COL_G = COL_P + POOL_WIDTH
IN_PAD_WIDTH = COL_G + N_BRANCHES * D_MODEL
MIX_WIDTH = FNET_WIDTH + POOL_WIDTH
ROUTER_PAD = LANES
NEG_BIG = -1e30

TOKEN_TILE = 512
IN_TILE = 256
ATTN_Q_TILE = 512
SHORT_SEQ_ROWS = 1024
FNET_ROW_TILE = 512
POOL_CHUNK = 256
POOL_HALO = 16
MOE_BLOCK = 512
MOE_STEP = 64
GATE_COL_CHUNK = 512
MERGE_COL_CHUNK = 256
FF_CHUNK = 256
COMBINE_TILE = 256
SC_GATHER_ROWS = 64
MOD_COL_TILE = 1536
VMEM_LIMIT = 56 * 1024 * 1024

F32 = jnp.float32
BF16 = jnp.bfloat16


def _cparams(*sem):
    return pltpu.CompilerParams(dimension_semantics=sem, vmem_limit_bytes=VMEM_LIMIT)


def _dot(a, b):
    return jnp.dot(a, b, preferred_element_type=F32)


def _rms(x):
    return x * lax.rsqrt(jnp.mean(x * x, axis=-1, keepdims=True) + RMS_EPS)


def _pack_bf16_pairs(x):
    half = x.shape[1] // 2
    bits = pltpu.bitcast(x.astype(BF16).astype(F32), jnp.int32)
    return bits[:, :half] | lax.shift_right_logical(bits[:, half:], 16)


def _unpack_bf16_pairs(w):
    return pltpu.bitcast(w & jnp.int32(-65536), F32), pltpu.bitcast(lax.shift_left(w, 16), F32)


def _seg_sums(x, m_ref):
    return _dot((x * x).astype(BF16), m_ref[...])


def _seg_inv_rms(ss, inv_cnt, mt2_ref):
    r = lax.rsqrt(ss * inv_cnt + RMS_EPS)
    r_hi = r.astype(BF16)
    r_lo = (r - r_hi.astype(F32)).astype(BF16)
    return _dot(jnp.concatenate([r_hi, r_lo], axis=1), mt2_ref[...])


def _seg_rms_scale(x, m_ref, mt2_ref, inv_cnt):
    return _seg_inv_rms(_seg_sums(x, m_ref), inv_cnt, mt2_ref)


def _rope_key_chunk(kpe_chunk, gkpe_ref, rope):
    ms = jnp.sum(kpe_chunk * kpe_chunk, axis=-1, keepdims=True) * (1.0 / ROPE_DIM)
    kp = kpe_chunk * lax.rsqrt(ms + RMS_EPS) * gkpe_ref[...]
    return kp if rope is None else _rope_chunk(kp, *rope)


def _rope_chunk(xc, cosf, sinf, first_half):
    partner = jnp.where(first_half, pltpu.roll(xc, HEAD_PAD - ROPE_HALF, 1), pltpu.roll(xc, ROPE_HALF, 1))
    return xc * cosf + partner * sinf


def _write_keys(ckv, kpe_chunk, wk_ref, wv_ref, gk_ref, gkpe_ref, mk_ref, mkt2_ref, rope, k_out, v_out):
    cb = ckv.astype(BF16)
    kn = _dot(cb, wk_ref[...])
    kn = kn * _seg_rms_scale(kn, mk_ref, mkt2_ref, 1.0 / QK_NOPE) * gk_ref[...]
    v_out[...] = _dot(cb, wv_ref[...]).astype(v_out.dtype)
    kp = _rope_key_chunk(kpe_chunk, gkpe_ref, rope)
    for h in range(N_HEADS):
        sl = slice(h * HEAD_PAD, (h + 1) * HEAD_PAD)
        k_out[:, sl] = (kn[:, sl] + kp).astype(k_out.dtype)


def _mod_kernel(c_ref, w_ref, b_ref, o_ref):
    c = c_ref[...]
    s = c * jax.nn.sigmoid(c)
    o_ref[...] = _dot(s.astype(BF16), w_ref[...].astype(BF16)) + b_ref[...]


def _modulation(cond8, w_mod, b_mod):
    depth = w_mod.shape[0]
    n = w_mod.shape[2]
    return pl.pallas_call(
        _mod_kernel,
        out_shape=jax.ShapeDtypeStruct((depth, 8, n), F32),
        grid=(depth, n // MOD_COL_TILE),
        in_specs=[
            pl.BlockSpec((8, D_MODEL), lambda l, j: (0, 0)),
            pl.BlockSpec((None, D_MODEL, MOD_COL_TILE), lambda l, j: (l, 0, j)),
            pl.BlockSpec((None, 1, MOD_COL_TILE), lambda l, j: (l, 0, j)),
        ],
        out_specs=pl.BlockSpec((None, 8, MOD_COL_TILE), lambda l, j: (l, 0, j)),
        compiler_params=_cparams("arbitrary", "arbitrary"),
        name="adaln_modulation",
    )(cond8, w_mod, b_mod.reshape(depth, 1, n))


def _moe_residual(tw_ref, x1_ref, mod_ref, yg_ref):
    acc = None
    for k in range(TOP_K):
        w = tw_ref[:, k:k + 1]
        part = [w * half for half in _unpack_bf16_pairs(yg_ref[k])]
        acc = part if acc is None else [a + p for a, p in zip(acc, part)]
    return x1_ref[...] + mod_ref[5:6, :] * jnp.concatenate(acc, axis=1)


def _in_kernel(*refs, after_moe):
    if after_moe:
        x = _moe_residual(*refs[:4])
        refs = refs[4:]
        refs[-1][...] = x
        refs = refs[:-1]
    else:
        x = refs[0][...]
        refs = refs[1:]
    (mod_ref, g1_ref, wa_ref, wb_ref, qag_ref, kvag_ref, wq_ref, wk_ref, wv_ref,
     gq_ref, gk_ref, gkpe_ref, cos_ref, sin_ref, mq_ref, mqt2_ref, icq_ref, mk_ref, mkt2_ref,
     q_out, k_out, v_out, ckv_out, kpe_out, f_out, p_out, g_out) = refs
    shift = mod_ref[0:1, :]
    scale = mod_ref[1:2, :]
    h = _rms(x) * g1_ref[...] * (1.0 + scale) + shift
    hb = h.astype(BF16)

    gate_chunks = iter(range(0, N_BRANCHES * D_MODEL, GATE_COL_CHUNK))

    def gate_chunk():
        c0 = next(gate_chunks)
        zc = _dot(hb, wb_ref[:, MIX_WIDTH + c0:MIX_WIDTH + c0 + GATE_COL_CHUNK])
        g_out[:, c0:c0 + GATE_COL_CHUNK] = jax.nn.sigmoid(zc).astype(g_out.dtype)

    za = _dot(hb, wa_ref[...])
    zb = _dot(hb, wb_ref[:, 0:MIX_WIDTH])
    f_out[...] = zb[:, 0:FNET_WIDTH].astype(f_out.dtype)
    p_out[...] = zb[:, FNET_WIDTH:]
    q_lat = za[:, 0:COL_KV]
    kv_lat = za[:, COL_KV:COL_KPE]
    key_lane = lax.broadcasted_iota(jnp.int32, (hb.shape[0], KPE_PAD), 1) < ROPE_DIM
    kpe_grp = jnp.where(key_lane, za[:, COL_KPE:COL_F], 0.0)
    ckv = _rms(kv_lat) * kvag_ref[...]
    ckv_out[...] = ckv
    kpe_out[...] = kpe_grp[:, 0:ROPE_DIM]
    qn = (_rms(q_lat) * qag_ref[...]).astype(BF16)
    cb = ckv.astype(BF16)
    gate_chunk()

    q = _dot(qn, wq_ref[...])
    kn = _dot(cb, wk_ref[...])
    v_out[...] = _dot(cb, wv_ref[...]).astype(v_out.dtype)
    gate_chunk()

    ss_q = _seg_sums(q, mq_ref)
    ss_k = _seg_sums(kn, mk_ref)
    gate_chunk()

    q = q * _seg_inv_rms(ss_q, icq_ref[...], mqt2_ref) * gq_ref[...]
    kn = kn * _seg_inv_rms(ss_k, 1.0 / QK_NOPE, mkt2_ref) * gk_ref[...]
    for _ in range(N_BRANCHES * D_MODEL // GATE_COL_CHUNK - 3):
        gate_chunk()

    cosf = cos_ref[...]
    sinf = sin_ref[...]
    lane = lax.broadcasted_iota(jnp.int32, cosf.shape, 1)
    rope = (cosf, sinf, lane < QK_NOPE + ROPE_HALF)
    sm_scale = 1.0 / math.sqrt(QK_HEAD)
    kp = _rope_key_chunk(pltpu.roll(kpe_grp, QK_NOPE, 1), gkpe_ref, rope)
    for hd in range(N_HEADS):
        sl = slice(hd * HEAD_PAD, (hd + 1) * HEAD_PAD)
        q_out[:, sl] = (_rope_chunk(q[:, sl], *rope) * sm_scale).astype(q_out.dtype)
        k_out[:, sl] = (kn[:, sl] + kp).astype(k_out.dtype)


def _const_spec(shape):
    nd = len(shape)
    return pl.BlockSpec(shape, lambda i, _n=nd: (0,) * _n, pipeline_mode=pl.Buffered(1))


def _layer_spec(shape, l):
    nd = len(shape)
    return pl.BlockSpec((None,) + tuple(shape), lambda i, _l=l, _n=nd: (_l,) + (0,) * _n,
                        pipeline_mode=pl.Buffered(1))


def _in_projection(x, moe, mods, l, wts, consts, tok):
    after_moe = x is None
    t = moe[0].shape[0] if after_moe else x.shape[0]
    tm = IN_TILE
    n_tiles = t // tm
    p_tiles = tok["t_prompt"] // tm
    s_tiles = tok["dec_seq"] // tm
    rope_blocks = tok["dec_seq"] // tm

    def mod_row(i):
        return jnp.where(i < p_tiles, 0, 1 + (i - p_tiles) // s_tiles)

    def rope_idx(i):
        return (jnp.where(i < p_tiles, rope_blocks, (i - p_tiles) % s_tiles), 0)

    row = lambda w: pl.BlockSpec((tm, w), lambda i: (i, 0))
    if after_moe:
        x1, tw, yg = moe
        lead_specs = [row(ROUTER_PAD), row(D_MODEL),
                      pl.BlockSpec((None, None, 6, D_MODEL), lambda i: (l - 1, mod_row(i), 0, 0)),
                      pl.BlockSpec((TOP_K, tm, D_MODEL // 2), lambda i: (0, i, 0))]
        lead_args = [tw, x1, mods, yg]
    else:
        lead_specs = [row(D_MODEL)]
        lead_args = [x]
    in_specs = lead_specs + [
        pl.BlockSpec((None, None, 6, D_MODEL), lambda i: (l, mod_row(i), 0, 0)),
        _layer_spec((1, D_MODEL), l),
        _layer_spec((D_MODEL, COL_F), l),
        _layer_spec((D_MODEL, MIX_WIDTH + N_BRANCHES * D_MODEL), l),
        _layer_spec((1, Q_RANK), l),
        _layer_spec((1, KV_RANK), l),
        _layer_spec((Q_RANK, QK_PAD), l),
        _layer_spec((KV_RANK, QK_PAD), l),
        _layer_spec((KV_RANK, V_WIDTH), l),
        _layer_spec((1, QK_PAD), l),
        _layer_spec((1, QK_PAD), l),
        _layer_spec((1, HEAD_PAD), l),
        pl.BlockSpec((tm, HEAD_PAD), rope_idx),
        pl.BlockSpec((tm, HEAD_PAD), rope_idx),
        _const_spec((QK_PAD, LANES)),
        _const_spec((2 * LANES, QK_PAD)),
        _const_spec((1, LANES)),
        _const_spec((QK_PAD, LANES)),
        _const_spec((2 * LANES, QK_PAD)),
    ]
    out_shape = [
        jax.ShapeDtypeStruct((t, QK_PAD), BF16),
        jax.ShapeDtypeStruct((t, QK_PAD), BF16),
        jax.ShapeDtypeStruct((t, V_WIDTH), BF16),
        jax.ShapeDtypeStruct((t, KV_RANK), F32),
        jax.ShapeDtypeStruct((t, ROPE_DIM), F32),
        jax.ShapeDtypeStruct((t, FNET_WIDTH), BF16),
        jax.ShapeDtypeStruct((t, POOL_WIDTH), F32),
        jax.ShapeDtypeStruct((t, N_BRANCHES * D_MODEL), BF16),
    ]
    if after_moe:
        out_shape.append(jax.ShapeDtypeStruct((t, D_MODEL), F32))
    out_specs = [row(s.shape[1]) for s in out_shape]
    outs = pl.pallas_call(
        functools.partial(_in_kernel, after_moe=after_moe),
        out_shape=out_shape,
        grid=(n_tiles,),
        in_specs=in_specs,
        out_specs=out_specs,
        compiler_params=_cparams("arbitrary"),
        cost_estimate=pl.CostEstimate(
            flops=2 * t * (D_MODEL * IN_PAD_WIDTH + Q_RANK * QK_PAD + KV_RANK * (QK_PAD + V_WIDTH)
                           + 2 * (QK_PAD * LANES + 2 * LANES * QK_PAD)),
            transcendentals=t * N_BRANCHES * D_MODEL,
            bytes_accessed=2 * D_MODEL * IN_PAD_WIDTH + sum(
                math.prod(s.shape) * s.dtype.itemsize for s in out_shape)
            + 4 * t * D_MODEL * (1 + (TOP_K // 2 if after_moe else 0))),
        name="in_projection",
    )(*lead_args, mods, wts["norm1_g"], wts["w_in_a"], wts["w_in_b"], wts["q_a_g"], wts["kv_a_g"], wts["w_q"], wts["w_k"], wts["w_v"],
      wts["g_q"], wts["g_k"], wts["g_kpe"], consts["cosf"], consts["sinf"],
      consts["m_q"], consts["mt2_q"], consts["inv_cnt_q"], consts["m_k"], consts["mt2_k"])
    return (outs[:8], outs[8]) if after_moe else (outs, x)


def _ctx_keys_kernel(ckv_ref, kpe_ref, wk_ref, wv_ref, gk_ref, gkpe_ref, mk_ref, mkt2_ref, k_out, v_out):
    _write_keys(ckv_ref[...], kpe_ref[...], wk_ref, wv_ref, gk_ref, gkpe_ref, mk_ref, mkt2_ref, None, k_out, v_out)


def _ctx_keys(cache_ckv, cache_kpe_pad, wts, consts):
    nb, depth, past, _ = cache_ckv.shape
    lw = lambda shape: pl.BlockSpec((None,) + shape, lambda l, b: (l,) + (0,) * len(shape))
    cs = lambda shape: pl.BlockSpec(shape, lambda l, b: (0,) * len(shape))
    return pl.pallas_call(
        _ctx_keys_kernel,
        out_shape=[jax.ShapeDtypeStruct((depth, nb * past, QK_PAD), BF16),
                   jax.ShapeDtypeStruct((depth, nb * past, V_WIDTH), BF16)],
        grid=(depth, nb),
        in_specs=[
            pl.BlockSpec((None, None, past, KV_RANK), lambda l, b: (b, l, 0, 0)),
            pl.BlockSpec((None, None, past, HEAD_PAD), lambda l, b: (b, l, 0, 0)),
            lw((KV_RANK, QK_PAD)), lw((KV_RANK, V_WIDTH)), lw((1, QK_PAD)), lw((1, HEAD_PAD)),
            cs((QK_PAD, LANES)), cs((2 * LANES, QK_PAD)),
        ],
        out_specs=[pl.BlockSpec((None, past, QK_PAD), lambda l, b: (l, b, 0)),
                   pl.BlockSpec((None, past, V_WIDTH), lambda l, b: (l, b, 0))],
        compiler_params=_cparams("arbitrary", "arbitrary"),
        name="context_keys",
    )(cache_ckv, cache_kpe_pad, wts["w_k"], wts["w_v"], wts["g_k"], wts["g_kpe"], consts["m_k"], consts["mt2_k"])


def _attn_kernel(*refs, n_parts, seqs):
    q_ref = refs[0]
    k_refs = refs[1:1 + 2 * n_parts:2]
    v_refs = refs[2:2 + 2 * n_parts:2]
    o_ref = refs[1 + 2 * n_parts]
    q_rows = q_ref.shape[0] // seqs
    lane = lax.broadcasted_iota(jnp.int32, (q_rows, 2 * V_HEAD), 1)
    for s in range(seqs):
        rows = slice(s * q_rows, (s + 1) * q_rows)
        keys = [slice(s * (k.shape[0] // seqs), (s + 1) * (k.shape[0] // seqs)) for k in k_refs]
        for pair in range(N_HEADS // 2):
            vsl = slice(pair * 2 * V_HEAD, (pair + 1) * 2 * V_HEAD)
            outs = []
            for hd in (2 * pair, 2 * pair + 1):
                sl = slice(hd * HEAD_PAD, (hd + 1) * HEAD_PAD)
                qh = q_ref[rows, sl]
                ss = [lax.dot_general(qh, k[ks, sl], (((1,), (1,)), ((), ())), preferred_element_type=F32)
                      for k, ks in zip(k_refs, keys)]
                m = functools.reduce(jnp.maximum, [jnp.max(sc, axis=-1, keepdims=True) for sc in ss])
                es = [jnp.exp(sc - m) for sc in ss]
                den = functools.reduce(jnp.add, [jnp.sum(e, axis=-1, keepdims=True) for e in es])
                acc = functools.reduce(jnp.add, [_dot(e.astype(BF16), v[ks, vsl])
                                                 for e, v, ks in zip(es, v_refs, keys)])
                outs.append(acc / den)
            o_ref[rows, vsl] = jnp.where(lane < V_HEAD, outs[0], outs[1]).astype(o_ref.dtype)


def _attention(q, kv_parts, n_seq, seq_len):
    tq = min(ATTN_Q_TILE, seq_len)
    nq = seq_len // tq
    n_keys = sum(rows for _, _, _, rows in kv_parts)
    seqs = math.gcd(n_seq, max(1, SHORT_SEQ_ROWS // seq_len)) if nq == 1 and len(kv_parts) == 1 else 1
    in_specs = [pl.BlockSpec((seqs * tq, QK_PAD), lambda b, i: (b * nq + i, 0))]
    args = [q]
    for k, v, layer, rows in kv_parts:
        for arr, width in ((k, QK_PAD), (v, V_WIDTH)):
            if layer is None:
                in_specs.append(pl.BlockSpec((seqs * rows, width), lambda b, i: (b, 0)))
            else:
                in_specs.append(pl.BlockSpec((None, rows, width), lambda b, i, _l=layer: (_l, b, 0)))
            args.append(arr)
    return pl.pallas_call(
        functools.partial(_attn_kernel, n_parts=len(kv_parts), seqs=seqs),
        out_shape=jax.ShapeDtypeStruct((n_seq * seq_len, V_WIDTH), BF16),
        grid=(n_seq // seqs, nq),
        in_specs=in_specs,
        out_specs=pl.BlockSpec((seqs * tq, V_WIDTH), lambda b, i: (b * nq + i, 0)),
        compiler_params=_cparams("arbitrary", "arbitrary"),
        cost_estimate=pl.CostEstimate(
            flops=2 * n_seq * seq_len * n_keys * N_HEADS * (HEAD_PAD + 2 * V_HEAD),
            transcendentals=n_seq * seq_len * n_keys * N_HEADS,
            bytes_accessed=2 * n_seq * (seq_len * (QK_PAD + V_WIDTH) + n_keys * (QK_PAD + V_WIDTH))),
        name="attention",
    )(*args)


def _fnet_kernel(f_ref, cs_ref, cl_ref, sl_ref, o_ref, top_ref, bot_ref, *, norm, seqs):
    @pl.when(pl.program_id(1) == 0)
    def _():
        for g in range(FNET_GROUPS):
            sl = slice(g * FNET_GC, (g + 1) * FNET_GC)
            a = _dot(f_ref[:, sl], cs_ref[...])
            top_ref[:, sl] = a[:, :FNET_GC].astype(BF16)
            bot_ref[:, sl] = a[:, FNET_GC:].astype(BF16)

    seq_len = f_ref.shape[0] // seqs
    rows = o_ref.shape[0] // seqs
    for s in range(seqs):
        src = slice(s * seq_len, (s + 1) * seq_len)
        y = _dot(cl_ref[...], top_ref[src, :]) - _dot(sl_ref[...], bot_ref[src, :])
        o_ref[s * rows:(s + 1) * rows, :] = (y * norm).astype(o_ref.dtype)


def _fourier(f_in, n_seq, seq_len, row0, tabs):
    tr = min(FNET_ROW_TILE, seq_len)
    nj = seq_len // tr
    seqs = math.gcd(n_seq, max(1, SHORT_SEQ_ROWS // seq_len)) if nj == 1 else 1
    sb0 = row0 // (seqs * seq_len)
    return pl.pallas_call(
        functools.partial(_fnet_kernel, norm=1.0 / math.sqrt(seq_len * FNET_GC), seqs=seqs),
        out_shape=jax.ShapeDtypeStruct((n_seq * seq_len, FNET_WIDTH), BF16),
        grid=(n_seq // seqs, nj),
        in_specs=[
            pl.BlockSpec((seqs * seq_len, FNET_WIDTH), lambda b, j: (sb0 + b, 0)),
            pl.BlockSpec((FNET_GC, 2 * FNET_GC), lambda b, j: (0, 0)),
            pl.BlockSpec((tr, seq_len), lambda b, j: (j, 0)),
            pl.BlockSpec((tr, seq_len), lambda b, j: (j, 0)),
        ],
        out_specs=pl.BlockSpec((seqs * tr, FNET_WIDTH), lambda b, j: (b * nj + j, 0)),
        scratch_shapes=[pltpu.VMEM((seqs * seq_len, FNET_WIDTH), BF16),
                        pltpu.VMEM((seqs * seq_len, FNET_WIDTH), BF16)],
        compiler_params=_cparams("arbitrary", "arbitrary"),
        name="fourier_mix",
    )(f_in, tabs["chan"], tabs["cos"], tabs["sin"])


def _pool_kernel(p_ref, wg_ref, ps_ref, o_ref, pad_ref, *, seqs):
    seq_len = p_ref.shape[0] // seqs
    zeros = jnp.zeros((POOL_HALO, POOL_WIDTH), F32)
    ch = min(POOL_CHUNK, seq_len)
    for s in range(seqs):
        s0 = s * seq_len
        pad_ref[s, 0:POOL_HALO, :] = zeros
        pad_ref[s, POOL_HALO + seq_len:, :] = zeros
        pad_ref[s, POOL_HALO:POOL_HALO + seq_len, :] = p_ref[s0:s0 + seq_len, :]
        for c in range(seq_len // ch):
            t = lax.broadcasted_iota(jnp.int32, (ch, 1), 0) + c * ch
            for g, w in enumerate(POOL_WINDOWS):
                half = w // 2
                sl = slice(g * POOL_GC, (g + 1) * POOL_GC)
                acc = None
                for j in range(-half, half):
                    r0 = POOL_HALO + c * ch + j
                    part = pad_ref[s, r0:r0 + ch, sl]
                    acc = part if acc is None else acc + part
                cnt = (jnp.minimum(t + half, seq_len) - jnp.maximum(t - half, 0)).astype(F32)
                pooled = acc / cnt - p_ref[s0 + c * ch:s0 + (c + 1) * ch, sl]
                mixed = _dot(pooled.astype(BF16), wg_ref[g]) * ps_ref[:, sl]
                o_ref[s0 + c * ch:s0 + (c + 1) * ch, sl] = mixed.astype(o_ref.dtype)


def _pooling(p_in, n_seq, seq_len, row0, l, wts):
    seqs = math.gcd(n_seq, max(1, SHORT_SEQ_ROWS // seq_len))
    sb0 = row0 // (seqs * seq_len)
    g = len(POOL_WINDOWS)
    return pl.pallas_call(
        functools.partial(_pool_kernel, seqs=seqs),
        out_shape=jax.ShapeDtypeStruct((n_seq * seq_len, POOL_WIDTH), BF16),
        grid=(n_seq // seqs,),
        in_specs=[
            pl.BlockSpec((seqs * seq_len, POOL_WIDTH), lambda b: (sb0 + b, 0)),
            pl.BlockSpec((None, g, POOL_GC, POOL_GC), lambda b: (l, 0, 0, 0)),
            pl.BlockSpec((None, 1, POOL_WIDTH), lambda b: (l, 0, 0)),
        ],
        out_specs=pl.BlockSpec((seqs * seq_len, POOL_WIDTH), lambda b: (b, 0)),
        scratch_shapes=[pltpu.VMEM((seqs, seq_len + 2 * POOL_HALO, POOL_WIDTH), F32)],
        compiler_params=_cparams("arbitrary"),
        name="pool_mix",
    )(p_in, wts["w_pool_grp"], wts["pool_scale"])


def _merge_kernel(x_ref, mod_ref, a_ref, f_ref, p_ref, g_ref,
                  wa_ref, wf_ref, wp_ref, wo_ref, g2_ref, wr_ref, br_ref, tri_ref,
                  x1_out, hp_out, te_out, tw_out, rk_out, cnt_out, mix_scr, carry_ref):
    j = pl.program_id(0)

    @pl.when(j == 0)
    def _():
        mix_scr[1] = jnp.zeros(mix_scr.shape[1:], F32)

    @pl.when(j <= 1)
    def _():
        carry_ref[...] = jnp.zeros(carry_ref.shape, F32)

    def branch_chunks(lo, hi):
        out = []
        for c0 in range(lo, hi, MERGE_COL_CHUNK):
            cs = slice(c0, c0 + MERGE_COL_CHUNK)
            a = _dot(a_ref[...], wa_ref[:, cs])
            f = _dot(f_ref[...], wf_ref[:, cs])
            p = _dot(p_ref[...], wp_ref[:, cs])
            out.append((g_ref[:, c0:c0 + MERGE_COL_CHUNK].astype(F32) * a
                        + g_ref[:, D_MODEL + c0:D_MODEL + c0 + MERGE_COL_CHUNK].astype(F32) * f
                        + g_ref[:, 2 * D_MODEL + c0:2 * D_MODEL + c0 + MERGE_COL_CHUNK].astype(F32) * p
                        ).astype(BF16))
        return out

    def step(cur, prev):
        chunks = branch_chunks(0, D_MODEL // 2)

        gate1 = mod_ref[2:3, :]
        shift2 = mod_ref[3:4, :]
        scale2 = mod_ref[4:5, :]
        x1 = x_ref[...] + gate1 * mix_scr[prev]
        x1_out[...] = x1
        h2 = _rms(x1) * g2_ref[...] * (1.0 + scale2) + shift2
        hp_out[...] = _pack_bf16_pairs(h2)
        h_hi = h2.astype(BF16)
        h_lo = (h2 - h_hi.astype(F32)).astype(BF16)
        logits = _dot(jnp.concatenate([h_hi, h_lo, h_hi], axis=1), wr_ref[...]) + br_ref[...]

        chunks += branch_chunks(D_MODEL // 2, D_MODEL)
        mix_scr[cur] = _dot(jnp.concatenate(chunks, axis=1), wo_ref[...])

        lane = lax.broadcasted_iota(jnp.int32, logits.shape, 1)
        work = logits
        vals, idxs = [], []
        for _ in range(TOP_K):
            m = jnp.max(work, axis=-1, keepdims=True)
            idx = jnp.min(jnp.where(work == m, lane, ROUTER_PAD), axis=-1, keepdims=True)
            vals.append(m)
            idxs.append(idx)
            work = jnp.where(lane == idx, -jnp.inf, work)
        es = [jnp.exp(v - vals[0]) for v in vals]
        den = functools.reduce(jnp.add, es)

        chosen = functools.reduce(jnp.logical_or, [lane == idx for idx in idxs])
        hot = jnp.where(chosen, 1.0, 0.0)
        before = _dot(tri_ref[...], hot.astype(BF16)) + carry_ref[...]
        carry_ref[...] = carry_ref[...] + jnp.sum(hot, axis=0, keepdims=True)
        cnt_out[...] = carry_ref[...].astype(jnp.int32)

        te = jnp.zeros(logits.shape, jnp.int32)
        tw = jnp.zeros(logits.shape, F32)
        rk = jnp.zeros(logits.shape, jnp.int32)
        for k in range(TOP_K):
            rank_k = jnp.sum(jnp.where(lane == idxs[k], before, 0.0), axis=-1, keepdims=True).astype(jnp.int32)
            te = jnp.where(lane == k, idxs[k], te)
            tw = jnp.where(lane == k, es[k] / den, tw)
            rk = jnp.where(lane == k, rank_k, rk)
        te_out[...] = te
        tw_out[...] = tw
        rk_out[...] = rk

    for parity in (0, 1):
        pl.when(j % 2 == parity)(functools.partial(step, parity, 1 - parity))


def _merge(x, mods, attn, fnet, pool, gates, l, wts, consts, tok):
    t = x.shape[0]
    tm = TOKEN_TILE
    p_tiles = tok["t_prompt"] // tm
    s_tiles = tok["dec_seq"] // tm

    n_tiles = t // tm
    routed = lambda j: jnp.maximum(j - 1, 0)
    mixed = lambda j: jnp.minimum(j, n_tiles - 1)

    def mod_idx(j):
        i = routed(j)
        row = jnp.where(i < p_tiles, 0, 1 + (i - p_tiles) // s_tiles)
        return (l, row, 0, 0)

    row = lambda w: pl.BlockSpec((tm, w), lambda j: (routed(j), 0))
    head = lambda w: pl.BlockSpec((tm, w), lambda j: (mixed(j), 0))
    out_shape = [
        jax.ShapeDtypeStruct((t, D_MODEL), F32),
        jax.ShapeDtypeStruct((t, D_MODEL // 2), jnp.int32),
        jax.ShapeDtypeStruct((t, ROUTER_PAD), jnp.int32),
        jax.ShapeDtypeStruct((t, ROUTER_PAD), F32),
        jax.ShapeDtypeStruct((t, ROUTER_PAD), jnp.int32),
    ]
    return pl.pallas_call(
        _merge_kernel,
        out_shape=out_shape + [jax.ShapeDtypeStruct((1, ROUTER_PAD), jnp.int32)],
        grid=(n_tiles + 1,),
        in_specs=[
            row(D_MODEL),
            pl.BlockSpec((None, None, 6, D_MODEL), mod_idx),
            head(V_WIDTH), head(FNET_WIDTH), head(POOL_WIDTH), head(N_BRANCHES * D_MODEL),
            _layer_spec((V_WIDTH, D_MODEL), l),
            _layer_spec((FNET_WIDTH, D_MODEL), l),
            _layer_spec((POOL_WIDTH, D_MODEL), l),
            _layer_spec((D_MODEL, D_MODEL), l),
            _layer_spec((1, D_MODEL), l),
            _layer_spec((3 * D_MODEL, ROUTER_PAD), l),
            _layer_spec((1, ROUTER_PAD), l),
            _const_spec((tm, tm)),
        ],
        out_specs=[row(s.shape[1]) for s in out_shape] + [pl.BlockSpec((1, ROUTER_PAD), lambda j: (0, 0))],
        scratch_shapes=[pltpu.VMEM((2, tm, D_MODEL), F32), pltpu.VMEM((1, ROUTER_PAD), F32)],
        compiler_params=_cparams("arbitrary"),
        cost_estimate=pl.CostEstimate(
            flops=2 * t * (D_MODEL * (V_WIDTH + FNET_WIDTH + POOL_WIDTH + D_MODEL) + 3 * D_MODEL * ROUTER_PAD
                           + tm * ROUTER_PAD),
            transcendentals=t * TOP_K,
            bytes_accessed=t * (2 * (V_WIDTH + FNET_WIDTH + POOL_WIDTH + N_BRANCHES * D_MODEL) + 4 * 2 * D_MODEL
                                + 4 * (D_MODEL // 2) + 3 * 4 * ROUTER_PAD)
            + 2 * D_MODEL * (V_WIDTH + FNET_WIDTH + POOL_WIDTH + D_MODEL + 3 * ROUTER_PAD)),
        name="merge_router",
    )(x, mods, attn, fnet, pool, gates, wts["w_attn_o"], wts["w_fnet_o"], wts["w_pool_o"], wts["w_out"],
      wts["norm2_g"], wts["w_router3"], wts["b_router"], consts["tri"])


def _sc_row_gather(table, idx):
    n = idx.shape[0]
    d = table.shape[1]
    info = plsc.get_sparse_core_info()
    n_cores = info.num_cores
    n_workers = n_cores * info.num_subcores
    per_worker = n // n_workers
    n_chunks = per_worker // SC_GATHER_ROWS
    assert n_chunks * SC_GATHER_ROWS * n_workers == n
    mesh = plsc.VectorSubcoreMesh(core_axis_name="core", subcore_axis_name="subcore")

    @functools.partial(
        pl.kernel, mesh=mesh, out_type=jax.ShapeDtypeStruct((n, d), table.dtype),
        scratch_types=[pltpu.VMEM((SC_GATHER_ROWS,), jnp.int32), pltpu.VMEM((SC_GATHER_ROWS, d), table.dtype),
                       pltpu.SemaphoreType.DMA],
        cost_estimate=pl.CostEstimate(flops=0, transcendentals=0,
                                      bytes_accessed=2 * n * d * table.dtype.itemsize + 4 * n),
        name="sc_row_gather")
    def gather(table_hbm, idx_hbm, out_hbm, idx_v, rows_v, sem):
        worker = lax.axis_index("subcore") * n_cores + lax.axis_index("core")

        @pl.loop(0, n_chunks)
        def _(c):
            base = worker * per_worker + c * SC_GATHER_ROWS
            pltpu.sync_copy(idx_hbm.at[pl.ds(base, SC_GATHER_ROWS)], idx_v)
            pltpu.async_copy(table_hbm.at[idx_v], rows_v, sem).wait()
            pltpu.sync_copy(rows_v, out_hbm.at[pl.ds(base, SC_GATHER_ROWS)])

    return gather(table, idx)


PLAN_EXPERT, PLAN_VALID, PLAN_FIRST, PLAN_SLOT, PLAN_NEXT, PLAN_USED = range(6)


def _expert_kernel(plan_ref, xs_ref, bgu_ref, bd_ref, wgu_hbm, wd_hbm, y_ref,
                   wgu_f32, wd_f32, wgu_bf, wd_bf, sems, *, layer):
    b = pl.program_id(0)

    def weight_copies(expert, slot):
        return (pltpu.make_async_copy(wgu_hbm.at[layer, expert], wgu_f32.at[slot], sems.at[0, slot]),
                pltpu.make_async_copy(wd_hbm.at[layer, expert], wd_f32.at[slot], sems.at[1, slot]))

    def ffn(rows):
        x = jnp.concatenate(_unpack_bf16_pairs(xs_ref[0:rows, :]), axis=1).astype(BF16)
        gu = _dot(x, wgu_bf[...]) + bgu_ref[...]
        glu = jnp.minimum(gu[:, :D_FF], SWIGLU_LIMIT)
        lin = jnp.clip(gu[:, D_FF:], -SWIGLU_LIMIT, SWIGLU_LIMIT)
        act = glu * jax.nn.sigmoid(SWIGLU_ALPHA * glu) * (lin + 1.0)
        y_ref[0:rows, :] = _pack_bf16_pairs(_dot(act.astype(BF16), wd_bf[...]) + bd_ref[...])
        if rows < MOE_BLOCK:
            y_ref[rows:, :] = jnp.zeros((MOE_BLOCK - rows, D_MODEL // 2), y_ref.dtype)

    n_used = plan_ref[PLAN_USED, 0]

    @pl.when(b < n_used)
    def _():
        slot = plan_ref[PLAN_SLOT, b]

        @pl.when(b == 0)
        def _():
            for cp in weight_copies(plan_ref[PLAN_EXPERT, 0], 0):
                cp.start()

        @pl.when(plan_ref[PLAN_FIRST, b] == 1)
        def _():
            for cp in weight_copies(plan_ref[PLAN_EXPERT, b], slot):
                cp.wait()
            wgu_bf[...] = wgu_f32[slot].astype(BF16)
            wd_bf[...] = wd_f32[slot].astype(BF16)

            @pl.when(plan_ref[PLAN_NEXT, b] >= 0)
            def _():
                for cp in weight_copies(plan_ref[PLAN_NEXT, b], 1 - slot):
                    cp.start()

        steps = (plan_ref[PLAN_VALID, b] + (MOE_STEP - 1)) // MOE_STEP
        for rows in range(MOE_STEP, MOE_BLOCK + 1, MOE_STEP):
            want = steps <= 1 if rows == MOE_STEP else steps == rows // MOE_STEP
            pl.when(want)(functools.partial(ffn, rows))

    @pl.when(b >= n_used)
    def _():
        y_ref[...] = jnp.zeros(y_ref.shape, y_ref.dtype)


def _experts(xs, plan, l, w_gu, b_gu4, w_down, b_down4):
    n_slots = xs.shape[0]
    bm = MOE_BLOCK
    n_blocks = n_slots // bm

    def blk(b, plan):
        return jnp.minimum(b, plan[PLAN_USED, 0] - 1)

    def bspec(width):
        return pl.BlockSpec((None, None, 1, width), lambda b, plan: (l, plan[PLAN_EXPERT, blk(b, plan)], 0, 0))

    grid_spec = pltpu.PrefetchScalarGridSpec(
        num_scalar_prefetch=1,
        grid=(n_blocks,),
        in_specs=[
            pl.BlockSpec((bm, D_MODEL // 2), lambda b, plan: (blk(b, plan), 0)),
            bspec(2 * D_FF), bspec(D_MODEL),
            pl.BlockSpec(memory_space=pl.ANY), pl.BlockSpec(memory_space=pl.ANY),
        ],
        out_specs=pl.BlockSpec((bm, D_MODEL // 2), lambda b, plan: (b, 0)),
        scratch_shapes=[pltpu.VMEM((2, D_MODEL, 2 * D_FF), F32), pltpu.VMEM((2, D_FF, D_MODEL), F32),
                        pltpu.VMEM((D_MODEL, 2 * D_FF), BF16), pltpu.VMEM((D_FF, D_MODEL), BF16),
                        pltpu.SemaphoreType.DMA((2, 2))],
    )
    return pl.pallas_call(
        functools.partial(_expert_kernel, layer=l),
        out_shape=jax.ShapeDtypeStruct((n_slots, D_MODEL // 2), jnp.int32),
        grid_spec=grid_spec,
        compiler_params=_cparams("arbitrary"),
        cost_estimate=pl.CostEstimate(
            flops=2 * n_slots * 3 * D_MODEL * D_FF, transcendentals=n_slots * D_FF,
            bytes_accessed=4 * N_EXPERTS * 3 * D_MODEL * D_FF + 2 * 4 * n_slots * (D_MODEL // 2)),
        name="moe_experts",
    )(plan, xs, b_gu4, b_down4, w_gu, w_down)


def _combine_kernel(tw_ref, x_ref, mod_ref, yg_ref, o_ref):
    o_ref[...] = _moe_residual(tw_ref, x_ref, mod_ref, yg_ref)


def _combine(yg, tw, x1, mods, l, tok):
    t = x1.shape[0]
    tc = COMBINE_TILE
    p_tiles = tok["t_prompt"] // tc
    s_tiles = tok["dec_seq"] // tc

    def mod_idx(i):
        row = jnp.where(i < p_tiles, 0, 1 + (i - p_tiles) // s_tiles)
        return (l, row, 0, 0)

    return pl.pallas_call(
        _combine_kernel,
        out_shape=jax.ShapeDtypeStruct((t, D_MODEL), F32),
        grid=(t // tc,),
        in_specs=[
            pl.BlockSpec((tc, ROUTER_PAD), lambda i: (i, 0)),
            pl.BlockSpec((tc, D_MODEL), lambda i: (i, 0)),
            pl.BlockSpec((None, None, 6, D_MODEL), mod_idx),
            pl.BlockSpec((TOP_K, tc, D_MODEL // 2), lambda i: (0, i, 0)),
        ],
        out_specs=pl.BlockSpec((tc, D_MODEL), lambda i: (i, 0)),
        compiler_params=_cparams("arbitrary"),
        name="moe_combine",
    )(tw, x1, mods, yg)


def _slot_tokens(slot, counts, padded_start, padded_end, n_slots, t):
    n = slot.shape[0]
    j = jnp.arange(MOE_BLOCK, dtype=jnp.int32)
    pad_key = (padded_start + counts)[:, None] + j[None, :]
    pad_key = jnp.where(pad_key < padded_end[:, None], pad_key, n_slots).reshape(N_EXPERTS * MOE_BLOCK)
    pad_val = jnp.arange(N_EXPERTS * MOE_BLOCK, dtype=jnp.int32) % t
    keys = jnp.concatenate([slot, pad_key])
    vals = jnp.concatenate([jnp.arange(n, dtype=jnp.int32) // TOP_K, pad_val])
    tok_bits = max(1, (t - 1).bit_length())
    assert (n_slots + 1) << tok_bits < 2 ** 31
    packed = jnp.sort(lax.shift_left(keys, tok_bits) | vals)
    return packed[:n_slots] & ((1 << tok_bits) - 1)


def _route(te_pad, rank_pad, counts_pad, n_slots):
    t = te_pad.shape[0]
    flat_e = te_pad[:, :TOP_K].reshape(t * TOP_K)
    onehot = (flat_e[:, None] == jnp.arange(N_EXPERTS, dtype=jnp.int32)[None, :]).astype(jnp.int32)
    rank = rank_pad[:, :TOP_K].reshape(t * TOP_K)
    counts = counts_pad[0, :N_EXPERTS]
    padded = (counts + MOE_BLOCK - 1) // MOE_BLOCK * MOE_BLOCK
    padded_end = jnp.cumsum(padded)
    padded_start = padded_end - padded
    slot = jnp.sum(onehot * padded_start[None, :], axis=1) + rank
    n_blocks = n_slots // MOE_BLOCK
    block_row0 = jnp.arange(n_blocks, dtype=jnp.int32) * MOE_BLOCK
    block_exp = jnp.minimum(
        jnp.sum((padded_end[None, :] <= block_row0[:, None]).astype(jnp.int32), axis=1),
        N_EXPERTS - 1).astype(jnp.int32)
    exp_hot = (block_exp[:, None] == jnp.arange(N_EXPERTS, dtype=jnp.int32)[None, :]).astype(jnp.int32)
    rows_end = jnp.sum(exp_hot * (padded_start + counts)[None, :], axis=1)
    block_valid = jnp.clip(rows_end - block_row0, 0, MOE_BLOCK).astype(jnp.int32)
    n_used = (padded_end[-1] // MOE_BLOCK).astype(jnp.int32)
    block_ids = jnp.arange(n_blocks, dtype=jnp.int32)
    prev_exp = jnp.concatenate([jnp.full((1,), -1, jnp.int32), block_exp[:-1]])
    first = ((block_exp != prev_exp) & (block_ids < n_used)).astype(jnp.int32)
    buf_slot = (jnp.cumsum(first) - 1) % 2
    experts = jnp.arange(N_EXPERTS, dtype=jnp.int32)
    holder = jnp.where(counts > 0, experts, N_EXPERTS)
    later = lax.cummin(jnp.concatenate([holder[1:], jnp.full((1,), N_EXPERTS, jnp.int32)]), reverse=True)
    next_exp = jnp.where(later < N_EXPERTS, later, -1)
    block_next = jnp.sum(exp_hot * next_exp[None, :], axis=1)
    plan = jnp.stack([block_exp, block_valid, first, buf_slot, block_next,
                      jnp.full((n_blocks,), n_used, jnp.int32)]).astype(jnp.int32)
    slot = slot.astype(jnp.int32)
    slot_tok = _slot_tokens(slot, counts, padded_start, padded_end, n_slots, t)
    slot_by_choice = slot.reshape(t, TOP_K).T.reshape(TOP_K * t)
    return slot_by_choice, slot_tok, plan


def _segment_matrices():
    m_q = np.zeros((QK_PAD, LANES), np.float32)
    m_k = np.zeros((QK_PAD, LANES), np.float32)
    inv_cnt_q = np.ones((1, LANES), np.float32)
    for h in range(N_HEADS):
        m_q[h * HEAD_PAD:h * HEAD_PAD + QK_NOPE, 2 * h] = 1.0
        m_q[h * HEAD_PAD + QK_NOPE:h * HEAD_PAD + QK_HEAD, 2 * h + 1] = 1.0
        inv_cnt_q[0, 2 * h] = 1.0 / QK_NOPE
        inv_cnt_q[0, 2 * h + 1] = 1.0 / ROPE_DIM
        m_k[h * HEAD_PAD:h * HEAD_PAD + QK_NOPE, h] = 1.0
    dup = lambda m: np.concatenate([m.T, m.T], axis=0)
    return dict(m_q=jnp.asarray(m_q, BF16), mt2_q=jnp.asarray(dup(m_q), BF16), inv_cnt_q=jnp.asarray(inv_cnt_q),
                m_k=jnp.asarray(m_k, BF16), mt2_k=jnp.asarray(dup(m_k), BF16))


def _rope_lane_tables(n_tokens, ident_rows):
    rows = n_tokens // GRID_W
    row = jnp.broadcast_to(jnp.arange(rows)[:, None], (rows, GRID_W)).reshape(n_tokens)
    col = jnp.broadcast_to(jnp.arange(GRID_W)[None, :], (rows, GRID_W)).reshape(n_tokens)
    n_freq = ROPE_DIM // 4
    inv = 1.0 / (ROPE_THETA ** (jnp.arange(n_freq, dtype=F32) / n_freq))
    ang = jnp.concatenate([row[:, None].astype(F32) * inv, col[:, None].astype(F32) * inv], axis=-1)
    cos, sin = jnp.cos(ang), jnp.sin(ang)
    ones = jnp.ones((n_tokens, QK_NOPE), F32)
    tail = HEAD_PAD - QK_HEAD
    cosf = jnp.concatenate([ones, cos, cos, jnp.ones((n_tokens, tail), F32)], axis=1)
    sinf = jnp.concatenate([0.0 * ones, -sin, sin, jnp.zeros((n_tokens, tail), F32)], axis=1)
    cosf = jnp.concatenate([cosf, jnp.ones((ident_rows, HEAD_PAD), F32)], axis=0)
    sinf = jnp.concatenate([sinf, jnp.zeros((ident_rows, HEAD_PAD), F32)], axis=0)
    return cosf, sinf


def _dft_tables(seq_len):
    def cs(n):
        k = np.arange(n, dtype=np.int64)
        ang = 2.0 * np.pi * ((k[:, None] * k[None, :]) % n).astype(np.float64) / n
        return np.cos(ang).astype(np.float32), np.sin(ang).astype(np.float32)

    cl, sl = cs(seq_len)
    cc, sc = cs(FNET_GC)
    return dict(cos=jnp.asarray(cl).astype(BF16), sin=jnp.asarray(sl).astype(BF16),
                chan=jnp.asarray(np.concatenate([cc, sc], axis=1)).astype(BF16))


def _layout_weights(w_in, w_q_b, w_kv_b, q_nope_g, q_rope_g, k_nope_g, k_rope_g, w_router, b_router):
    depth = w_in.shape[0]
    w_in_a = w_in[:, :, :COL_F].astype(BF16)
    w_in_b = w_in[:, :, COL_KPE + ROPE_DIM:].astype(BF16)
    w_q = jnp.pad(w_q_b.reshape(depth, Q_RANK, N_HEADS, QK_HEAD),
                  ((0, 0), (0, 0), (0, 0), (0, HEAD_PAD - QK_HEAD))).reshape(depth, Q_RANK, QK_PAD).astype(BF16)
    kv = w_kv_b.reshape(depth, KV_RANK, N_HEADS, QK_NOPE + V_HEAD)
    w_k = jnp.pad(kv[..., :QK_NOPE], ((0, 0), (0, 0), (0, 0), (0, HEAD_PAD - QK_NOPE))
                  ).reshape(depth, KV_RANK, QK_PAD).astype(BF16)
    w_v = kv[..., QK_NOPE:].reshape(depth, KV_RANK, V_WIDTH).astype(BF16)
    zq = jnp.zeros((depth, HEAD_PAD - QK_HEAD), F32)
    g_q = jnp.tile(jnp.concatenate([q_nope_g, q_rope_g, zq], axis=1), (1, N_HEADS))[:, None, :]
    g_k = jnp.tile(jnp.concatenate([k_nope_g, jnp.zeros((depth, HEAD_PAD - QK_NOPE), F32)], axis=1),
                   (1, N_HEADS))[:, None, :]
    g_kpe = jnp.concatenate([jnp.zeros((depth, QK_NOPE), F32), k_rope_g, zq], axis=1)[:, None, :]
    w_hi = w_router.astype(BF16)
    w_lo = (w_router - w_hi.astype(F32)).astype(BF16)
    w_r3 = jnp.pad(jnp.concatenate([w_hi, w_hi, w_lo], axis=1), ((0, 0), (0, 0), (0, ROUTER_PAD - N_EXPERTS)))
    b_r = jnp.pad(b_router, ((0, 0), (0, ROUTER_PAD - N_EXPERTS)), constant_values=NEG_BIG)[:, None, :]
    return dict(w_in_a=w_in_a, w_in_b=w_in_b, w_q=w_q, w_k=w_k, w_v=w_v, g_q=g_q, g_k=g_k, g_kpe=g_kpe, w_router3=w_r3, b_router=b_r)


def kernel(x_prompt, x_sample, cache_ckv, cache_kpe, c, c_ctx, w_mod, b_mod, norm1_g, norm2_g, w_in, q_a_g, kv_a_g, w_q_b, w_kv_b, q_nope_g, q_rope_g, k_nope_g, k_rope_g, w_attn_o, w_fnet_o, w_pool_grp, pool_scale, w_pool_o, w_out, w_router, b_router, w_gu, b_gu, w_down, b_down):
    batch, seq, d = x_prompt.shape
    dec_batch, dec_seq, _ = x_sample.shape
    depth = w_mod.shape[0]
    past = cache_ckv.shape[2]
    t_prompt = batch * seq
    t_sample = dec_batch * dec_seq
    assert d == D_MODEL and dec_batch + 1 <= 8
    assert t_prompt % TOKEN_TILE == 0 and dec_seq % TOKEN_TILE == 0 and seq % COMBINE_TILE == 0
    assert dec_seq % GRID_W == 0

    consts = _segment_matrices()
    consts["cosf"], consts["sinf"] = _rope_lane_tables(dec_seq, IN_TILE)
    consts["tri"] = jnp.asarray(np.tril(np.ones((TOKEN_TILE, TOKEN_TILE), np.float32), -1), BF16)

    wts = _layout_weights(w_in, w_q_b, w_kv_b, q_nope_g, q_rope_g, k_nope_g, k_rope_g, w_router, b_router)
    row3 = lambda a: a[:, None, :]
    wts.update(norm1_g=row3(norm1_g), norm2_g=row3(norm2_g), q_a_g=row3(q_a_g), kv_a_g=row3(kv_a_g),
               w_attn_o=w_attn_o.astype(BF16), w_fnet_o=w_fnet_o.astype(BF16), w_pool_o=w_pool_o.astype(BF16),
               w_out=w_out.astype(BF16), w_pool_grp=w_pool_grp.astype(BF16), pool_scale=row3(pool_scale))
    b_gu4 = b_gu[:, :, None, :]
    b_down4 = b_down[:, :, None, :]

    cond8 = jnp.concatenate([c_ctx[None, :], c, jnp.zeros((8 - 1 - dec_batch, d), F32)], axis=0)
    mods = _modulation(cond8, w_mod, b_mod).reshape(depth, 8, 6, D_MODEL)

    kpe_pad = jnp.pad(cache_kpe, ((0, 0), (0, 0), (0, 0), (QK_NOPE, HEAD_PAD - QK_HEAD)))
    k_ctx, v_ctx = _ctx_keys(cache_ckv, kpe_pad, wts, consts)

    streams = [
        dict(x=x_prompt.reshape(t_prompt, d), n_seq=batch, seq_len=seq, tok=dict(t_prompt=t_prompt, dec_seq=dec_seq),
             dft=_dft_tables(seq), ctx=None, moe=None),
        dict(x=x_sample.reshape(t_sample, d), n_seq=dec_batch, seq_len=dec_seq, tok=dict(t_prompt=0, dec_seq=dec_seq),
             dft=_dft_tables(dec_seq), ctx=(k_ctx, v_ctx, past), moe=None),
    ]
    ckv_list, kpe_list = [], []

    def mixers(st, l):
        moe = None if st["x"] is not None else (st["x1"], st["tw"], st["yg"])
        (q, k, v, ckv, kpe, f_in, p_in, st["gates"]), st["x"] = _in_projection(
            st["x"], moe, mods, l, wts, consts, st["tok"])
        if st["ctx"] is None:
            ckv_list.append(ckv.reshape(batch, seq, KV_RANK))
            kpe_list.append(kpe.reshape(batch, seq, ROPE_DIM))
        parts = [(k, v, None, st["seq_len"])]
        if st["ctx"] is not None:
            parts.append((st["ctx"][0], st["ctx"][1], l, st["ctx"][2]))
        st["attn"] = _attention(q, parts, st["n_seq"], st["seq_len"])
        st["fnet"] = _fourier(f_in, st["n_seq"], st["seq_len"], 0, st["dft"])
        st["pool"] = _pooling(p_in, st["n_seq"], st["seq_len"], 0, l, wts)

    def merge_and_route(st, l):
        t = st["n_seq"] * st["seq_len"]
        n_slots = (t * TOP_K + N_EXPERTS * (MOE_BLOCK - 1) + MOE_BLOCK - 1) // MOE_BLOCK * MOE_BLOCK
        st["x1"], st["h2p"], te_pad, st["tw"], rank_pad, counts_pad = _merge(
            st["x"], mods, st["attn"], st["fnet"], st["pool"], st["gates"], l, wts, consts, st["tok"])
        st["slot_by_choice"], st["slot_tok"], st["plan"] = _route(te_pad, rank_pad, counts_pad, n_slots)

    def dispatch(st):
        st["xs"] = _sc_row_gather(st["h2p"], st["slot_tok"])

    def experts(st, l):
        st["y"] = _experts(st["xs"], st["plan"], l, w_gu, b_gu4, w_down, b_down4)

    def gather_back(st):
        t = st["n_seq"] * st["seq_len"]
        st["yg"] = _sc_row_gather(st["y"], st["slot_by_choice"]).reshape(TOP_K, t, D_MODEL // 2)
        st["x"] = None

    def mixer_input(st):
        return "x" if st["x"] is not None else "yg"

    def tie(sa, ka, sb, kb):
        sa[ka], sb[kb] = lax.optimization_barrier((sa[ka], sb[kb]))

    a, b = streams
    mixers(a, 0)
    merge_and_route(a, 0)
    tie(b, mixer_input(b), a, "slot_tok")
    dispatch(a)
    mixers(b, 0)
    tie(a, "xs", b, "attn")
    experts(a, 0)
    for l in range(depth):
        tie(b, "x", a, "y")
        gather_back(a)
        merge_and_route(b, l)
        tie(a, "yg", b, "slot_tok")
        dispatch(b)
        if l + 1 == depth:
            break
        mixers(a, l + 1)
        tie(b, "xs", a, "attn")
        experts(b, l)
        tie(a, "x", b, "y")
        gather_back(b)
        merge_and_route(a, l + 1)
        tie(b, "yg", a, "slot_tok")
        dispatch(a)
        mixers(b, l + 1)
        tie(a, "xs", b, "attn")
        experts(a, l + 1)
    a["out"] = _combine(a["yg"], a["tw"], a["x1"], mods, depth - 1, a["tok"])
    tie(b, "xs", a, "out")
    experts(b, depth - 1)
    gather_back(b)
    out_a = a["out"]
    out_b = _combine(b["yg"], b["tw"], b["x1"], mods, depth - 1, b["tok"])
    y_prompt = out_a.reshape(batch, seq, d)
    y_sample = out_b.reshape(dec_batch, dec_seq, d)
    return (y_prompt, y_sample, jnp.stack(ckv_list, axis=1), jnp.stack(kpe_list, axis=1))
```

```python
import functools
import math

import numpy as np
import jax
import jax.numpy as jnp
from jax import lax
from jax.experimental import pallas as pl
from jax.experimental.pallas import tpu as pltpu
from jax.experimental.pallas import tpu_sc as plsc

D_MODEL = 1024
GRID_W = 64
N_HEADS = 8
QK_NOPE = 64
ROPE_DIM = 32
V_HEAD = 64
QK_HEAD = QK_NOPE + ROPE_DIM
Q_RANK = 384
KV_RANK = 256
ROPE_THETA = 10000.0
FNET_GROUPS = 4
FNET_GC = 128
FNET_WIDTH = FNET_GROUPS * FNET_GC
POOL_WINDOWS = (2, 4, 8, 16)
POOL_GC = 128
POOL_WIDTH = len(POOL_WINDOWS) * POOL_GC
N_BRANCHES = 3
N_EXPERTS = 32
TOP_K = 4
D_FF = D_MODEL
SWIGLU_LIMIT = 7.0
SWIGLU_ALPHA = 1.702
RMS_EPS = 1e-6

LANES = 128
HEAD_PAD = LANES
QK_PAD = N_HEADS * HEAD_PAD
V_WIDTH = N_HEADS * V_HEAD
ROPE_HALF = ROPE_DIM // 2
KPE_PAD = LANES
COL_KV = Q_RANK
COL_KPE = Q_RANK + KV_RANK
COL_F = COL_KPE + KPE_PAD
COL_P = COL_F + FNET_WIDTH
COL_G = COL_P + POOL_WIDTH
IN_PAD_WIDTH = COL_G + N_BRANCHES * D_MODEL
MIX_WIDTH = FNET_WIDTH + POOL_WIDTH
ROUTER_PAD = LANES
NEG_BIG = -1e30

TOKEN_TILE = 512
IN_TILE = 256
ATTN_Q_TILE = 512
SHORT_SEQ_ROWS = 1024
FNET_ROW_TILE = 512
POOL_CHUNK = 256
POOL_HALO = 16
MOE_BLOCK = 512
MOE_STEP = 64
GATE_COL_CHUNK = 512
MERGE_COL_CHUNK = 256
FF_CHUNK = 256
COMBINE_TILE = 256
SC_GATHER_ROWS = 64
MOD_COL_TILE = 1536
VMEM_LIMIT = 56 * 1024 * 1024

F32 = jnp.float32
BF16 = jnp.bfloat16


def _cparams(*sem):
    return pltpu.CompilerParams(dimension_semantics=sem, vmem_limit_bytes=VMEM_LIMIT)


def _dot(a, b):
    return jnp.dot(a, b, preferred_element_type=F32)


def _rms(x):
    return x * lax.rsqrt(jnp.mean(x * x, axis=-1, keepdims=True) + RMS_EPS)


def _pack_bf16_pairs(x):
    half = x.shape[1] // 2
    bits = pltpu.bitcast(x.astype(BF16).astype(F32), jnp.int32)
    return bits[:, :half] | lax.shift_right_logical(bits[:, half:], 16)


def _unpack_bf16_pairs(w):
    return pltpu.bitcast(w & jnp.int32(-65536), F32), pltpu.bitcast(lax.shift_left(w, 16), F32)


def _seg_sums(x, m_ref):
    return _dot((x * x).astype(BF16), m_ref[...])


def _seg_inv_rms(ss, inv_cnt, mt2_ref):
    r = lax.rsqrt(ss * inv_cnt + RMS_EPS)
    r_hi = r.astype(BF16)
    r_lo = (r - r_hi.astype(F32)).astype(BF16)
    return _dot(jnp.concatenate([r_hi, r_lo], axis=1), mt2_ref[...])


def _seg_rms_scale(x, m_ref, mt2_ref, inv_cnt):
    return _seg_inv_rms(_seg_sums(x, m_ref), inv_cnt, mt2_ref)


def _rope_key_chunk(kpe_chunk, gkpe_ref, rope):
    ms = jnp.sum(kpe_chunk * kpe_chunk, axis=-1, keepdims=True) * (1.0 / ROPE_DIM)
    kp = kpe_chunk * lax.rsqrt(ms + RMS_EPS) * gkpe_ref[...]
    return kp if rope is None else _rope_chunk(kp, *rope)


def _rope_chunk(xc, cosf, sinf, first_half):
    partner = jnp.where(first_half, pltpu.roll(xc, HEAD_PAD - ROPE_HALF, 1), pltpu.roll(xc, ROPE_HALF, 1))
    return xc * cosf + partner * sinf


def _write_keys(ckv, kpe_chunk, wk_ref, wv_ref, gk_ref, gkpe_ref, mk_ref, mkt2_ref, rope, k_out, v_out):
    cb = ckv.astype(BF16)
    kn = _dot(cb, wk_ref[...])
    kn = kn * _seg_rms_scale(kn, mk_ref, mkt2_ref, 1.0 / QK_NOPE) * gk_ref[...]
    v_out[...] = _dot(cb, wv_ref[...]).astype(v_out.dtype)
    kp = _rope_key_chunk(kpe_chunk, gkpe_ref, rope)
    for h in range(N_HEADS):
        sl = slice(h * HEAD_PAD, (h + 1) * HEAD_PAD)
        k_out[:, sl] = (kn[:, sl] + kp).astype(k_out.dtype)


def _mod_kernel(c_ref, w_ref, b_ref, o_ref):
    c = c_ref[...]
    s = c * jax.nn.sigmoid(c)
    o_ref[...] = _dot(s.astype(BF16), w_ref[...].astype(BF16)) + b_ref[...]


def _modulation(cond8, w_mod, b_mod):
    depth = w_mod.shape[0]
    n = w_mod.shape[2]
    return pl.pallas_call(
        _mod_kernel,
        out_shape=jax.ShapeDtypeStruct((depth, 8, n), F32),
        grid=(depth, n // MOD_COL_TILE),
        in_specs=[
            pl.BlockSpec((8, D_MODEL), lambda l, j: (0, 0)),
            pl.BlockSpec((None, D_MODEL, MOD_COL_TILE), lambda l, j: (l, 0, j)),
            pl.BlockSpec((None, 1, MOD_COL_TILE), lambda l, j: (l, 0, j)),
        ],
        out_specs=pl.BlockSpec((None, 8, MOD_COL_TILE), lambda l, j: (l, 0, j)),
        compiler_params=_cparams("arbitrary", "arbitrary"),
        name="adaln_modulation",
    )(cond8, w_mod, b_mod.reshape(depth, 1, n))


def _moe_residual(tw_ref, x1_ref, mod_ref, yg_ref):
    acc = None
    for k in range(TOP_K):
        w = tw_ref[:, k:k + 1]
        part = [w * half for half in _unpack_bf16_pairs(yg_ref[k])]
        acc = part if acc is None else [a + p for a, p in zip(acc, part)]
    return x1_ref[...] + mod_ref[5:6, :] * jnp.concatenate(acc, axis=1)


def _in_kernel(*refs, after_moe):
    if after_moe:
        x = _moe_residual(*refs[:4])
        refs = refs[4:]
        refs[-1][...] = x
        refs = refs[:-1]
    else:
        x = refs[0][...]
        refs = refs[1:]
    (mod_ref, g1_ref, wa_ref, wb_ref, qag_ref, kvag_ref, wq_ref, wk_ref, wv_ref,
     gq_ref, gk_ref, gkpe_ref, cos_ref, sin_ref, mq_ref, mqt2_ref, icq_ref, mk_ref, mkt2_ref,
     q_out, k_out, v_out, ckv_out, kpe_out, f_out, p_out, g_out) = refs
    shift = mod_ref[0:1, :]
    scale = mod_ref[1:2, :]
    h = _rms(x) * g1_ref[...] * (1.0 + scale) + shift
    hb = h.astype(BF16)

    gate_chunks = iter(range(0, N_BRANCHES * D_MODEL, GATE_COL_CHUNK))

    def gate_chunk():
        c0 = next(gate_chunks)
        zc = _dot(hb, wb_ref[:, MIX_WIDTH + c0:MIX_WIDTH + c0 + GATE_COL_CHUNK])
        g_out[:, c0:c0 + GATE_COL_CHUNK] = jax.nn.sigmoid(zc).astype(g_out.dtype)

    za = _dot(hb, wa_ref[...])
    zb = _dot(hb, wb_ref[:, 0:MIX_WIDTH])
    f_out[...] = zb[:, 0:FNET_WIDTH].astype(f_out.dtype)
    p_out[...] = zb[:, FNET_WIDTH:]
    q_lat = za[:, 0:COL_KV]
    kv_lat = za[:, COL_KV:COL_KPE]
    key_lane = lax.broadcasted_iota(jnp.int32, (hb.shape[0], KPE_PAD), 1) < ROPE_DIM
    kpe_grp = jnp.where(key_lane, za[:, COL_KPE:COL_F], 0.0)
    ckv = _rms(kv_lat) * kvag_ref[...]
    ckv_out[...] = ckv
    kpe_out[...] = kpe_grp[:, 0:ROPE_DIM]
    qn = (_rms(q_lat) * qag_ref[...]).astype(BF16)
    cb = ckv.astype(BF16)
    gate_chunk()

    q = _dot(qn, wq_ref[...])
    kn = _dot(cb, wk_ref[...])
    v_out[...] = _dot(cb, wv_ref[...]).astype(v_out.dtype)
    gate_chunk()

    ss_q = _seg_sums(q, mq_ref)
    ss_k = _seg_sums(kn, mk_ref)
    gate_chunk()

    q = q * _seg_inv_rms(ss_q, icq_ref[...], mqt2_ref) * gq_ref[...]
    kn = kn * _seg_inv_rms(ss_k, 1.0 / QK_NOPE, mkt2_ref) * gk_ref[...]
    for _ in range(N_BRANCHES * D_MODEL // GATE_COL_CHUNK - 3):
        gate_chunk()

    cosf = cos_ref[...]
    sinf = sin_ref[...]
    lane = lax.broadcasted_iota(jnp.int32, cosf.shape, 1)
    rope = (cosf, sinf, lane < QK_NOPE + ROPE_HALF)
    sm_scale = 1.0 / math.sqrt(QK_HEAD)
    kp = _rope_key_chunk(pltpu.roll(kpe_grp, QK_NOPE, 1), gkpe_ref, rope)
    for hd in range(N_HEADS):
        sl = slice(hd * HEAD_PAD, (hd + 1) * HEAD_PAD)
        q_out[:, sl] = (_rope_chunk(q[:, sl], *rope) * sm_scale).astype(q_out.dtype)
        k_out[:, sl] = (kn[:, sl] + kp).astype(k_out.dtype)


def _const_spec(shape):
    nd = len(shape)
    return pl.BlockSpec(shape, lambda i, _n=nd: (0,) * _n, pipeline_mode=pl.Buffered(1))


def _layer_spec(shape, l):
    nd = len(shape)
    return pl.BlockSpec((None,) + tuple(shape), lambda i, _l=l, _n=nd: (_l,) + (0,) * _n,
                        pipeline_mode=pl.Buffered(1))


def _in_projection(x, moe, mods, l, wts, consts, tok):
    after_moe = x is None
    t = moe[0].shape[0] if after_moe else x.shape[0]
    tm = IN_TILE
    n_tiles = t // tm
    p_tiles = tok["t_prompt"] // tm
    s_tiles = tok["dec_seq"] // tm
    rope_blocks = tok["dec_seq"] // tm

    def mod_row(i):
        return jnp.where(i < p_tiles, 0, 1 + (i - p_tiles) // s_tiles)

    def rope_idx(i):
        return (jnp.where(i < p_tiles, rope_blocks, (i - p_tiles) % s_tiles), 0)

    row = lambda w: pl.BlockSpec((tm, w), lambda i: (i, 0))
    if after_moe:
        x1, tw, yg = moe
        lead_specs = [row(ROUTER_PAD), row(D_MODEL),
                      pl.BlockSpec((None, None, 6, D_MODEL), lambda i: (l - 1, mod_row(i), 0, 0)),
                      pl.BlockSpec((TOP_K, tm, D_MODEL // 2), lambda i: (0, i, 0))]
        lead_args = [tw, x1, mods, yg]
    else:
        lead_specs = [row(D_MODEL)]
        lead_args = [x]
    in_specs = lead_specs + [
        pl.BlockSpec((None, None, 6, D_MODEL), lambda i: (l, mod_row(i), 0, 0)),
        _layer_spec((1, D_MODEL), l),
        _layer_spec((D_MODEL, COL_F), l),
        _layer_spec((D_MODEL, MIX_WIDTH + N_BRANCHES * D_MODEL), l),
        _layer_spec((1, Q_RANK), l),
        _layer_spec((1, KV_RANK), l),
        _layer_spec((Q_RANK, QK_PAD), l),
        _layer_spec((KV_RANK, QK_PAD), l),
        _layer_spec((KV_RANK, V_WIDTH), l),
        _layer_spec((1, QK_PAD), l),
        _layer_spec((1, QK_PAD), l),
        _layer_spec((1, HEAD_PAD), l),
        pl.BlockSpec((tm, HEAD_PAD), rope_idx),
        pl.BlockSpec((tm, HEAD_PAD), rope_idx),
        _const_spec((QK_PAD, LANES)),
        _const_spec((2 * LANES, QK_PAD)),
        _const_spec((1, LANES)),
        _const_spec((QK_PAD, LANES)),
        _const_spec((2 * LANES, QK_PAD)),
    ]
    out_shape = [
        jax.ShapeDtypeStruct((t, QK_PAD), BF16),
        jax.ShapeDtypeStruct((t, QK_PAD), BF16),
        jax.ShapeDtypeStruct((t, V_WIDTH), BF16),
        jax.ShapeDtypeStruct((t, KV_RANK), F32),
        jax.ShapeDtypeStruct((t, ROPE_DIM), F32),
        jax.ShapeDtypeStruct((t, FNET_WIDTH), BF16),
        jax.ShapeDtypeStruct((t, POOL_WIDTH), F32),
        jax.ShapeDtypeStruct((t, N_BRANCHES * D_MODEL), BF16),
    ]
    if after_moe:
        out_shape.append(jax.ShapeDtypeStruct((t, D_MODEL), F32))
    out_specs = [row(s.shape[1]) for s in out_shape]
    outs = pl.pallas_call(
        functools.partial(_in_kernel, after_moe=after_moe),
        out_shape=out_shape,
        grid=(n_tiles,),
        in_specs=in_specs,
        out_specs=out_specs,
        compiler_params=_cparams("arbitrary"),
        cost_estimate=pl.CostEstimate(
            flops=2 * t * (D_MODEL * IN_PAD_WIDTH + Q_RANK * QK_PAD + KV_RANK * (QK_PAD + V_WIDTH)
                           + 2 * (QK_PAD * LANES + 2 * LANES * QK_PAD)),
            transcendentals=t * N_BRANCHES * D_MODEL,
            bytes_accessed=2 * D_MODEL * IN_PAD_WIDTH + sum(
                math.prod(s.shape) * s.dtype.itemsize for s in out_shape)
            + 4 * t * D_MODEL * (1 + (TOP_K // 2 if after_moe else 0))),
        name="in_projection",
    )(*lead_args, mods, wts["norm1_g"], wts["w_in_a"], wts["w_in_b"], wts["q_a_g"], wts["kv_a_g"], wts["w_q"], wts["w_k"], wts["w_v"],
      wts["g_q"], wts["g_k"], wts["g_kpe"], consts["cosf"], consts["sinf"],
      consts["m_q"], consts["mt2_q"], consts["inv_cnt_q"], consts["m_k"], consts["mt2_k"])
    return (outs[:8], outs[8]) if after_moe else (outs, x)


def _ctx_keys_kernel(ckv_ref, kpe_ref, wk_ref, wv_ref, gk_ref, gkpe_ref, mk_ref, mkt2_ref, k_out, v_out):
    _write_keys(ckv_ref[...], kpe_ref[...], wk_ref, wv_ref, gk_ref, gkpe_ref, mk_ref, mkt2_ref, None, k_out, v_out)


def _ctx_keys(cache_ckv, cache_kpe_pad, wts, consts):
    nb, depth, past, _ = cache_ckv.shape
    lw = lambda shape: pl.BlockSpec((None,) + shape, lambda l, b: (l,) + (0,) * len(shape))
    cs = lambda shape: pl.BlockSpec(shape, lambda l, b: (0,) * len(shape))
    return pl.pallas_call(
        _ctx_keys_kernel,
        out_shape=[jax.ShapeDtypeStruct((depth, nb * past, QK_PAD), BF16),
                   jax.ShapeDtypeStruct((depth, nb * past, V_WIDTH), BF16)],
        grid=(depth, nb),
        in_specs=[
            pl.BlockSpec((None, None, past, KV_RANK), lambda l, b: (b, l, 0, 0)),
            pl.BlockSpec((None, None, past, HEAD_PAD), lambda l, b: (b, l, 0, 0)),
            lw((KV_RANK, QK_PAD)), lw((KV_RANK, V_WIDTH)), lw((1, QK_PAD)), lw((1, HEAD_PAD)),
            cs((QK_PAD, LANES)), cs((2 * LANES, QK_PAD)),
        ],
        out_specs=[pl.BlockSpec((None, past, QK_PAD), lambda l, b: (l, b, 0)),
                   pl.BlockSpec((None, past, V_WIDTH), lambda l, b: (l, b, 0))],
        compiler_params=_cparams("arbitrary", "arbitrary"),
        name="context_keys",
    )(cache_ckv, cache_kpe_pad, wts["w_k"], wts["w_v"], wts["g_k"], wts["g_kpe"], consts["m_k"], consts["mt2_k"])


def _attn_kernel(*refs, n_parts, seqs):
    q_ref = refs[0]
    k_refs = refs[1:1 + 2 * n_parts:2]
    v_refs = refs[2:2 + 2 * n_parts:2]
    o_ref = refs[1 + 2 * n_parts]
    q_rows = q_ref.shape[0] // seqs
    lane = lax.broadcasted_iota(jnp.int32, (q_rows, 2 * V_HEAD), 1)
    for s in range(seqs):
        rows = slice(s * q_rows, (s + 1) * q_rows)
        keys = [slice(s * (k.shape[0] // seqs), (s + 1) * (k.shape[0] // seqs)) for k in k_refs]
        for pair in range(N_HEADS // 2):
            vsl = slice(pair * 2 * V_HEAD, (pair + 1) * 2 * V_HEAD)
            outs = []
            for hd in (2 * pair, 2 * pair + 1):
                sl = slice(hd * HEAD_PAD, (hd + 1) * HEAD_PAD)
                qh = q_ref[rows, sl]
                ss = [lax.dot_general(qh, k[ks, sl], (((1,), (1,)), ((), ())), preferred_element_type=F32)
                      for k, ks in zip(k_refs, keys)]
                m = functools.reduce(jnp.maximum, [jnp.max(sc, axis=-1, keepdims=True) for sc in ss])
                es = [jnp.exp(sc - m) for sc in ss]
                den = functools.reduce(jnp.add, [jnp.sum(e, axis=-1, keepdims=True) for e in es])
                acc = functools.reduce(jnp.add, [_dot(e.astype(BF16), v[ks, vsl])
                                                 for e, v, ks in zip(es, v_refs, keys)])
                outs.append(acc / den)
            o_ref[rows, vsl] = jnp.where(lane < V_HEAD, outs[0], outs[1]).astype(o_ref.dtype)


def _attention(q, kv_parts, n_seq, seq_len):
    tq = min(ATTN_Q_TILE, seq_len)
    nq = seq_len // tq
    n_keys = sum(rows for _, _, _, rows in kv_parts)
    seqs = math.gcd(n_seq, max(1, SHORT_SEQ_ROWS // seq_len)) if nq == 1 and len(kv_parts) == 1 else 1
    in_specs = [pl.BlockSpec((seqs * tq, QK_PAD), lambda b, i: (b * nq + i, 0))]
    args = [q]
    for k, v, layer, rows in kv_parts:
        for arr, width in ((k, QK_PAD), (v, V_WIDTH)):
            if layer is None:
                in_specs.append(pl.BlockSpec((seqs * rows, width), lambda b, i: (b, 0)))
            else:
                in_specs.append(pl.BlockSpec((None, rows, width), lambda b, i, _l=layer: (_l, b, 0)))
            args.append(arr)
    return pl.pallas_call(
        functools.partial(_attn_kernel, n_parts=len(kv_parts), seqs=seqs),
        out_shape=jax.ShapeDtypeStruct((n_seq * seq_len, V_WIDTH), BF16),
        grid=(n_seq // seqs, nq),
        in_specs=in_specs,
        out_specs=pl.BlockSpec((seqs * tq, V_WIDTH), lambda b, i: (b * nq + i, 0)),
        compiler_params=_cparams("arbitrary", "arbitrary"),
        cost_estimate=pl.CostEstimate(
            flops=2 * n_seq * seq_len * n_keys * N_HEADS * (HEAD_PAD + 2 * V_HEAD),
            transcendentals=n_seq * seq_len * n_keys * N_HEADS,
            bytes_accessed=2 * n_seq * (seq_len * (QK_PAD + V_WIDTH) + n_keys * (QK_PAD + V_WIDTH))),
        name="attention",
    )(*args)


def _fnet_kernel(f_ref, cs_ref, cl_ref, sl_ref, o_ref, top_ref, bot_ref, *, norm, seqs):
    @pl.when(pl.program_id(1) == 0)
    def _():
        for g in range(FNET_GROUPS):
            sl = slice(g * FNET_GC, (g + 1) * FNET_GC)
            a = _dot(f_ref[:, sl], cs_ref[...])
            top_ref[:, sl] = a[:, :FNET_GC].astype(BF16)
            bot_ref[:, sl] = a[:, FNET_GC:].astype(BF16)

    seq_len = f_ref.shape[0] // seqs
    rows = o_ref.shape[0] // seqs
    for s in range(seqs):
        src = slice(s * seq_len, (s + 1) * seq_len)
        y = _dot(cl_ref[...], top_ref[src, :]) - _dot(sl_ref[...], bot_ref[src, :])
        o_ref[s * rows:(s + 1) * rows, :] = (y * norm).astype(o_ref.dtype)


def _fourier(f_in, n_seq, seq_len, row0, tabs):
    tr = min(FNET_ROW_TILE, seq_len)
    nj = seq_len // tr
    seqs = math.gcd(n_seq, max(1, SHORT_SEQ_ROWS // seq_len)) if nj == 1 else 1
    sb0 = row0 // (seqs * seq_len)
    return pl.pallas_call(
        functools.partial(_fnet_kernel, norm=1.0 / math.sqrt(seq_len * FNET_GC), seqs=seqs),
        out_shape=jax.ShapeDtypeStruct((n_seq * seq_len, FNET_WIDTH), BF16),
        grid=(n_seq // seqs, nj),
        in_specs=[
            pl.BlockSpec((seqs * seq_len, FNET_WIDTH), lambda b, j: (sb0 + b, 0)),
            pl.BlockSpec((FNET_GC, 2 * FNET_GC), lambda b, j: (0, 0)),
            pl.BlockSpec((tr, seq_len), lambda b, j: (j, 0)),
            pl.BlockSpec((tr, seq_len), lambda b, j: (j, 0)),
        ],
        out_specs=pl.BlockSpec((seqs * tr, FNET_WIDTH), lambda b, j: (b * nj + j, 0)),
        scratch_shapes=[pltpu.VMEM((seqs * seq_len, FNET_WIDTH), BF16),
                        pltpu.VMEM((seqs * seq_len, FNET_WIDTH), BF16)],
        compiler_params=_cparams("arbitrary", "arbitrary"),
        name="fourier_mix",
    )(f_in, tabs["chan"], tabs["cos"], tabs["sin"])


def _pool_kernel(p_ref, wg_ref, ps_ref, o_ref, pad_ref, *, seqs):
    seq_len = p_ref.shape[0] // seqs
    zeros = jnp.zeros((POOL_HALO, POOL_WIDTH), F32)
    ch = min(POOL_CHUNK, seq_len)
    for s in range(seqs):
        s0 = s * seq_len
        pad_ref[s, 0:POOL_HALO, :] = zeros
        pad_ref[s, POOL_HALO + seq_len:, :] = zeros
        pad_ref[s, POOL_HALO:POOL_HALO + seq_len, :] = p_ref[s0:s0 + seq_len, :]
        for c in range(seq_len // ch):
            t = lax.broadcasted_iota(jnp.int32, (ch, 1), 0) + c * ch
            for g, w in enumerate(POOL_WINDOWS):
                half = w // 2
                sl = slice(g * POOL_GC, (g + 1) * POOL_GC)
                acc = None
                for j in range(-half, half):
                    r0 = POOL_HALO + c * ch + j
                    part = pad_ref[s, r0:r0 + ch, sl]
                    acc = part if acc is None else acc + part
                cnt = (jnp.minimum(t + half, seq_len) - jnp.maximum(t - half, 0)).astype(F32)
                pooled = acc / cnt - p_ref[s0 + c * ch:s0 + (c + 1) * ch, sl]
                mixed = _dot(pooled.astype(BF16), wg_ref[g]) * ps_ref[:, sl]
                o_ref[s0 + c * ch:s0 + (c + 1) * ch, sl] = mixed.astype(o_ref.dtype)


def _pooling(p_in, n_seq, seq_len, row0, l, wts):
    seqs = math.gcd(n_seq, max(1, SHORT_SEQ_ROWS // seq_len))
    sb0 = row0 // (seqs * seq_len)
    g = len(POOL_WINDOWS)
    return pl.pallas_call(
        functools.partial(_pool_kernel, seqs=seqs),
        out_shape=jax.ShapeDtypeStruct((n_seq * seq_len, POOL_WIDTH), BF16),
        grid=(n_seq // seqs,),
        in_specs=[
            pl.BlockSpec((seqs * seq_len, POOL_WIDTH), lambda b: (sb0 + b, 0)),
            pl.BlockSpec((None, g, POOL_GC, POOL_GC), lambda b: (l, 0, 0, 0)),
            pl.BlockSpec((None, 1, POOL_WIDTH), lambda b: (l, 0, 0)),
        ],
        out_specs=pl.BlockSpec((seqs * seq_len, POOL_WIDTH), lambda b: (b, 0)),
        scratch_shapes=[pltpu.VMEM((seqs, seq_len + 2 * POOL_HALO, POOL_WIDTH), F32)],
        compiler_params=_cparams("arbitrary"),
        name="pool_mix",
    )(p_in, wts["w_pool_grp"], wts["pool_scale"])


def _merge_kernel(x_ref, mod_ref, a_ref, f_ref, p_ref, g_ref,
                  wa_ref, wf_ref, wp_ref, wo_ref, g2_ref, wr_ref, br_ref, tri_ref,
                  x1_out, hp_out, te_out, tw_out, rk_out, cnt_out, mix_scr, carry_ref):
    j = pl.program_id(0)

    @pl.when(j == 0)
    def _():
        mix_scr[1] = jnp.zeros(mix_scr.shape[1:], F32)

    @pl.when(j <= 1)
    def _():
        carry_ref[...] = jnp.zeros(carry_ref.shape, F32)

    def branch_chunks(lo, hi):
        out = []
        for c0 in range(lo, hi, MERGE_COL_CHUNK):
            cs = slice(c0, c0 + MERGE_COL_CHUNK)
            a = _dot(a_ref[...], wa_ref[:, cs])
            f = _dot(f_ref[...], wf_ref[:, cs])
            p = _dot(p_ref[...], wp_ref[:, cs])
            out.append((g_ref[:, c0:c0 + MERGE_COL_CHUNK].astype(F32) * a
                        + g_ref[:, D_MODEL + c0:D_MODEL + c0 + MERGE_COL_CHUNK].astype(F32) * f
                        + g_ref[:, 2 * D_MODEL + c0:2 * D_MODEL + c0 + MERGE_COL_CHUNK].astype(F32) * p
                        ).astype(BF16))
        return out

    def step(cur, prev):
        chunks = branch_chunks(0, D_MODEL // 2)

        gate1 = mod_ref[2:3, :]
        shift2 = mod_ref[3:4, :]
        scale2 = mod_ref[4:5, :]
        x1 = x_ref[...] + gate1 * mix_scr[prev]
        x1_out[...] = x1
        h2 = _rms(x1) * g2_ref[...] * (1.0 + scale2) + shift2
        hp_out[...] = _pack_bf16_pairs(h2)
        h_hi = h2.astype(BF16)
        h_lo = (h2 - h_hi.astype(F32)).astype(BF16)
        logits = _dot(jnp.concatenate([h_hi, h_lo, h_hi], axis=1), wr_ref[...]) + br_ref[...]

        chunks += branch_chunks(D_MODEL // 2, D_MODEL)
        mix_scr[cur] = _dot(jnp.concatenate(chunks, axis=1), wo_ref[...])

        lane = lax.broadcasted_iota(jnp.int32, logits.shape, 1)
        work = logits
        vals, idxs = [], []
        for _ in range(TOP_K):
            m = jnp.max(work, axis=-1, keepdims=True)
            idx = jnp.min(jnp.where(work == m, lane, ROUTER_PAD), axis=-1, keepdims=True)
            vals.append(m)
            idxs.append(idx)
            work = jnp.where(lane == idx, -jnp.inf, work)
        es = [jnp.exp(v - vals[0]) for v in vals]
        den = functools.reduce(jnp.add, es)

        chosen = functools.reduce(jnp.logical_or, [lane == idx for idx in idxs])
        hot = jnp.where(chosen, 1.0, 0.0)
        before = _dot(tri_ref[...], hot.astype(BF16)) + carry_ref[...]
        carry_ref[...] = carry_ref[...] + jnp.sum(hot, axis=0, keepdims=True)
        cnt_out[...] = carry_ref[...].astype(jnp.int32)

        te = jnp.zeros(logits.shape, jnp.int32)
        tw = jnp.zeros(logits.shape, F32)
        rk = jnp.zeros(logits.shape, jnp.int32)
        for k in range(TOP_K):
            rank_k = jnp.sum(jnp.where(lane == idxs[k], before, 0.0), axis=-1, keepdims=True).astype(jnp.int32)
            te = jnp.where(lane == k, idxs[k], te)
            tw = jnp.where(lane == k, es[k] / den, tw)
            rk = jnp.where(lane == k, rank_k, rk)
        te_out[...] = te
        tw_out[...] = tw
        rk_out[...] = rk

    for parity in (0, 1):
        pl.when(j % 2 == parity)(functools.partial(step, parity, 1 - parity))


def _merge(x, mods, attn, fnet, pool, gates, l, wts, consts, tok):
    t = x.shape[0]
    tm = TOKEN_TILE
    p_tiles = tok["t_prompt"] // tm
    s_tiles = tok["dec_seq"] // tm

    n_tiles = t // tm
    routed = lambda j: jnp.maximum(j - 1, 0)
    mixed = lambda j: jnp.minimum(j, n_tiles - 1)

    def mod_idx(j):
        i = routed(j)
        row = jnp.where(i < p_tiles, 0, 1 + (i - p_tiles) // s_tiles)
        return (l, row, 0, 0)

    row = lambda w: pl.BlockSpec((tm, w), lambda j: (routed(j), 0))
    head = lambda w: pl.BlockSpec((tm, w), lambda j: (mixed(j), 0))
    out_shape = [
        jax.ShapeDtypeStruct((t, D_MODEL), F32),
        jax.ShapeDtypeStruct((t, D_MODEL // 2), jnp.int32),
        jax.ShapeDtypeStruct((t, ROUTER_PAD), jnp.int32),
        jax.ShapeDtypeStruct((t, ROUTER_PAD), F32),
        jax.ShapeDtypeStruct((t, ROUTER_PAD), jnp.int32),
    ]
    return pl.pallas_call(
        _merge_kernel,
        out_shape=out_shape + [jax.ShapeDtypeStruct((1, ROUTER_PAD), jnp.int32)],
        grid=(n_tiles + 1,),
        in_specs=[
            row(D_MODEL),
            pl.BlockSpec((None, None, 6, D_MODEL), mod_idx),
            head(V_WIDTH), head(FNET_WIDTH), head(POOL_WIDTH), head(N_BRANCHES * D_MODEL),
            _layer_spec((V_WIDTH, D_MODEL), l),
            _layer_spec((FNET_WIDTH, D_MODEL), l),
            _layer_spec((POOL_WIDTH, D_MODEL), l),
            _layer_spec((D_MODEL, D_MODEL), l),
            _layer_spec((1, D_MODEL), l),
            _layer_spec((3 * D_MODEL, ROUTER_PAD), l),
            _layer_spec((1, ROUTER_PAD), l),
            _const_spec((tm, tm)),
        ],
        out_specs=[row(s.shape[1]) for s in out_shape] + [pl.BlockSpec((1, ROUTER_PAD), lambda j: (0, 0))],
        scratch_shapes=[pltpu.VMEM((2, tm, D_MODEL), F32), pltpu.VMEM((1, ROUTER_PAD), F32)],
        compiler_params=_cparams("arbitrary"),
        cost_estimate=pl.CostEstimate(
            flops=2 * t * (D_MODEL * (V_WIDTH + FNET_WIDTH + POOL_WIDTH + D_MODEL) + 3 * D_MODEL * ROUTER_PAD
                           + tm * ROUTER_PAD),
            transcendentals=t * TOP_K,
            bytes_accessed=t * (2 * (V_WIDTH + FNET_WIDTH + POOL_WIDTH + N_BRANCHES * D_MODEL) + 4 * 2 * D_MODEL
                                + 4 * (D_MODEL // 2) + 3 * 4 * ROUTER_PAD)
            + 2 * D_MODEL * (V_WIDTH + FNET_WIDTH + POOL_WIDTH + D_MODEL + 3 * ROUTER_PAD)),
        name="merge_router",
    )(x, mods, attn, fnet, pool, gates, wts["w_attn_o"], wts["w_fnet_o"], wts["w_pool_o"], wts["w_out"],
      wts["norm2_g"], wts["w_router3"], wts["b_router"], consts["tri"])


def _sc_row_gather(table, idx):
    n = idx.shape[0]
    d = table.shape[1]
    info = plsc.get_sparse_core_info()
    n_cores = info.num_cores
    n_workers = n_cores * info.num_subcores
    per_worker = n // n_workers
    n_chunks = per_worker // SC_GATHER_ROWS
    assert n_chunks * SC_GATHER_ROWS * n_workers == n
    mesh = plsc.VectorSubcoreMesh(core_axis_name="core", subcore_axis_name="subcore")

    @functools.partial(
        pl.kernel, mesh=mesh, out_type=jax.ShapeDtypeStruct((n, d), table.dtype),
        scratch_types=[pltpu.VMEM((SC_GATHER_ROWS,), jnp.int32), pltpu.VMEM((SC_GATHER_ROWS, d), table.dtype),
                       pltpu.SemaphoreType.DMA],
        cost_estimate=pl.CostEstimate(flops=0, transcendentals=0,
                                      bytes_accessed=2 * n * d * table.dtype.itemsize + 4 * n),
        name="sc_row_gather")
    def gather(table_hbm, idx_hbm, out_hbm, idx_v, rows_v, sem):
        worker = lax.axis_index("subcore") * n_cores + lax.axis_index("core")

        @pl.loop(0, n_chunks)
        def _(c):
            base = worker * per_worker + c * SC_GATHER_ROWS
            pltpu.sync_copy(idx_hbm.at[pl.ds(base, SC_GATHER_ROWS)], idx_v)
            pltpu.async_copy(table_hbm.at[idx_v], rows_v, sem).wait()
            pltpu.sync_copy(rows_v, out_hbm.at[pl.ds(base, SC_GATHER_ROWS)])

    return gather(table, idx)


PLAN_EXPERT, PLAN_VALID, PLAN_FIRST, PLAN_SLOT, PLAN_NEXT, PLAN_USED = range(6)


def _expert_kernel(plan_ref, xs_ref, bgu_ref, bd_ref, wgu_hbm, wd_hbm, y_ref,
                   wgu_f32, wd_f32, wgu_bf, wd_bf, sems, *, layer):
    b = pl.program_id(0)

    def weight_copies(expert, slot):
        return (pltpu.make_async_copy(wgu_hbm.at[layer, expert], wgu_f32.at[slot], sems.at[0, slot]),
                pltpu.make_async_copy(wd_hbm.at[layer, expert], wd_f32.at[slot], sems.at[1, slot]))

    def ffn(rows):
        x = jnp.concatenate(_unpack_bf16_pairs(xs_ref[0:rows, :]), axis=1).astype(BF16)
        gu = _dot(x, wgu_bf[...]) + bgu_ref[...]
        glu = jnp.minimum(gu[:, :D_FF], SWIGLU_LIMIT)
        lin = jnp.clip(gu[:, D_FF:], -SWIGLU_LIMIT, SWIGLU_LIMIT)
        act = glu * jax.nn.sigmoid(SWIGLU_ALPHA * glu) * (lin + 1.0)
        y_ref[0:rows, :] = _pack_bf16_pairs(_dot(act.astype(BF16), wd_bf[...]) + bd_ref[...])
        if rows < MOE_BLOCK:
            y_ref[rows:, :] = jnp.zeros((MOE_BLOCK - rows, D_MODEL // 2), y_ref.dtype)

    n_used = plan_ref[PLAN_USED, 0]

    @pl.when(b < n_used)
    def _():
        slot = plan_ref[PLAN_SLOT, b]

        @pl.when(b == 0)
        def _():
            for cp in weight_copies(plan_ref[PLAN_EXPERT, 0], 0):
                cp.start()

        @pl.when(plan_ref[PLAN_FIRST, b] == 1)
        def _():
            for cp in weight_copies(plan_ref[PLAN_EXPERT, b], slot):
                cp.wait()
            wgu_bf[...] = wgu_f32[slot].astype(BF16)
            wd_bf[...] = wd_f32[slot].astype(BF16)

            @pl.when(plan_ref[PLAN_NEXT, b] >= 0)
            def _():
                for cp in weight_copies(plan_ref[PLAN_NEXT, b], 1 - slot):
                    cp.start()

        steps = (plan_ref[PLAN_VALID, b] + (MOE_STEP - 1)) // MOE_STEP
        for rows in range(MOE_STEP, MOE_BLOCK + 1, MOE_STEP):
            want = steps <= 1 if rows == MOE_STEP else steps == rows // MOE_STEP
            pl.when(want)(functools.partial(ffn, rows))

    @pl.when(b >= n_used)
    def _():
        y_ref[...] = jnp.zeros(y_ref.shape, y_ref.dtype)


def _experts(xs, plan, l, w_gu, b_gu4, w_down, b_down4):
    n_slots = xs.shape[0]
    bm = MOE_BLOCK
    n_blocks = n_slots // bm

    def blk(b, plan):
        return jnp.minimum(b, plan[PLAN_USED, 0] - 1)

    def bspec(width):
        return pl.BlockSpec((None, None, 1, width), lambda b, plan: (l, plan[PLAN_EXPERT, blk(b, plan)], 0, 0))

    grid_spec = pltpu.PrefetchScalarGridSpec(
        num_scalar_prefetch=1,
        grid=(n_blocks,),
        in_specs=[
            pl.BlockSpec((bm, D_MODEL // 2), lambda b, plan: (blk(b, plan), 0)),
            bspec(2 * D_FF), bspec(D_MODEL),
            pl.BlockSpec(memory_space=pl.ANY), pl.BlockSpec(memory_space=pl.ANY),
        ],
        out_specs=pl.BlockSpec((bm, D_MODEL // 2), lambda b, plan: (b, 0)),
        scratch_shapes=[pltpu.VMEM((2, D_MODEL, 2 * D_FF), F32), pltpu.VMEM((2, D_FF, D_MODEL), F32),
                        pltpu.VMEM((D_MODEL, 2 * D_FF), BF16), pltpu.VMEM((D_FF, D_MODEL), BF16),
                        pltpu.SemaphoreType.DMA((2, 2))],
    )
    return pl.pallas_call(
        functools.partial(_expert_kernel, layer=l),
        out_shape=jax.ShapeDtypeStruct((n_slots, D_MODEL // 2), jnp.int32),
        grid_spec=grid_spec,
        compiler_params=_cparams("arbitrary"),
        cost_estimate=pl.CostEstimate(
            flops=2 * n_slots * 3 * D_MODEL * D_FF, transcendentals=n_slots * D_FF,
            bytes_accessed=4 * N_EXPERTS * 3 * D_MODEL * D_FF + 2 * 4 * n_slots * (D_MODEL // 2)),
        name="moe_experts",
    )(plan, xs, b_gu4, b_down4, w_gu, w_down)


def _combine_kernel(tw_ref, x_ref, mod_ref, yg_ref, o_ref):
    o_ref[...] = _moe_residual(tw_ref, x_ref, mod_ref, yg_ref)


def _combine(yg, tw, x1, mods, l, tok):
    t = x1.shape[0]
    tc = COMBINE_TILE
    p_tiles = tok["t_prompt"] // tc
    s_tiles = tok["dec_seq"] // tc

    def mod_idx(i):
        row = jnp.where(i < p_tiles, 0, 1 + (i - p_tiles) // s_tiles)
        return (l, row, 0, 0)

    return pl.pallas_call(
        _combine_kernel,
        out_shape=jax.ShapeDtypeStruct((t, D_MODEL), F32),
        grid=(t // tc,),
        in_specs=[
            pl.BlockSpec((tc, ROUTER_PAD), lambda i: (i, 0)),
            pl.BlockSpec((tc, D_MODEL), lambda i: (i, 0)),
            pl.BlockSpec((None, None, 6, D_MODEL), mod_idx),
            pl.BlockSpec((TOP_K, tc, D_MODEL // 2), lambda i: (0, i, 0)),
        ],
        out_specs=pl.BlockSpec((tc, D_MODEL), lambda i: (i, 0)),
        compiler_params=_cparams("arbitrary"),
        name="moe_combine",
    )(tw, x1, mods, yg)


def _slot_tokens(slot, block_first, n_slots, t):
    n = slot.shape[0]
    tok_bits = max(1, (t - 1).bit_length())
    assert (n_slots + 1) << tok_bits < 2 ** 31
    packed = jnp.sort(lax.shift_left(slot, tok_bits) | (jnp.arange(n, dtype=jnp.int32) // TOP_K))
    spare = jnp.arange(MOE_BLOCK, dtype=jnp.int32) % t
    tokens = jnp.concatenate([packed & ((1 << tok_bits) - 1), spare])
    starts = jnp.clip(block_first, 0, n)
    blocks = jax.vmap(lambda s0: lax.dynamic_slice(tokens, (s0,), (MOE_BLOCK,)))(starts)
    return blocks.reshape(n_slots)


def _route(te_pad, rank_pad, counts_pad, n_slots):
    t = te_pad.shape[0]
    flat_e = te_pad[:, :TOP_K].reshape(t * TOP_K)
    onehot = (flat_e[:, None] == jnp.arange(N_EXPERTS, dtype=jnp.int32)[None, :]).astype(jnp.int32)
    rank = rank_pad[:, :TOP_K].reshape(t * TOP_K)
    counts = counts_pad[0, :N_EXPERTS]
    padded = (counts + MOE_BLOCK - 1) // MOE_BLOCK * MOE_BLOCK
    padded_end = jnp.cumsum(padded)
    padded_start = padded_end - padded
    slot = jnp.sum(onehot * padded_start[None, :], axis=1) + rank
    n_blocks = n_slots // MOE_BLOCK
    block_row0 = jnp.arange(n_blocks, dtype=jnp.int32) * MOE_BLOCK
    block_exp = jnp.minimum(
        jnp.sum((padded_end[None, :] <= block_row0[:, None]).astype(jnp.int32), axis=1),
        N_EXPERTS - 1).astype(jnp.int32)
    exp_hot = (block_exp[:, None] == jnp.arange(N_EXPERTS, dtype=jnp.int32)[None, :]).astype(jnp.int32)
    rows_end = jnp.sum(exp_hot * (padded_start + counts)[None, :], axis=1)
    block_valid = jnp.clip(rows_end - block_row0, 0, MOE_BLOCK).astype(jnp.int32)
    n_used = (padded_end[-1] // MOE_BLOCK).astype(jnp.int32)
    block_ids = jnp.arange(n_blocks, dtype=jnp.int32)
    prev_exp = jnp.concatenate([jnp.full((1,), -1, jnp.int32), block_exp[:-1]])
    first = ((block_exp != prev_exp) & (block_ids < n_used)).astype(jnp.int32)
    buf_slot = (jnp.cumsum(first) - 1) % 2
    experts = jnp.arange(N_EXPERTS, dtype=jnp.int32)
    holder = jnp.where(counts > 0, experts, N_EXPERTS)
    later = lax.cummin(jnp.concatenate([holder[1:], jnp.full((1,), N_EXPERTS, jnp.int32)]), reverse=True)
    next_exp = jnp.where(later < N_EXPERTS, later, -1)
    block_next = jnp.sum(exp_hot * next_exp[None, :], axis=1)
    plan = jnp.stack([block_exp, block_valid, first, buf_slot, block_next,
                      jnp.full((n_blocks,), n_used, jnp.int32)]).astype(jnp.int32)
    slot = slot.astype(jnp.int32)
    pad_before = padded_start - (jnp.cumsum(counts) - counts)
    block_first = block_row0 - jnp.sum(exp_hot * pad_before[None, :], axis=1)
    slot_tok = _slot_tokens(slot, block_first, n_slots, t)
    slot_by_choice = slot.reshape(t, TOP_K).T.reshape(TOP_K * t)
    return slot_by_choice, slot_tok, plan


def _segment_matrices():
    m_q = np.zeros((QK_PAD, LANES), np.float32)
    m_k = np.zeros((QK_PAD, LANES), np.float32)
    inv_cnt_q = np.ones((1, LANES), np.float32)
    for h in range(N_HEADS):
        m_q[h * HEAD_PAD:h * HEAD_PAD + QK_NOPE, 2 * h] = 1.0
        m_q[h * HEAD_PAD + QK_NOPE:h * HEAD_PAD + QK_HEAD, 2 * h + 1] = 1.0
        inv_cnt_q[0, 2 * h] = 1.0 / QK_NOPE
        inv_cnt_q[0, 2 * h + 1] = 1.0 / ROPE_DIM
        m_k[h * HEAD_PAD:h * HEAD_PAD + QK_NOPE, h] = 1.0
    dup = lambda m: np.concatenate([m.T, m.T], axis=0)
    return dict(m_q=jnp.asarray(m_q, BF16), mt2_q=jnp.asarray(dup(m_q), BF16), inv_cnt_q=jnp.asarray(inv_cnt_q),
                m_k=jnp.asarray(m_k, BF16), mt2_k=jnp.asarray(dup(m_k), BF16))


def _rope_lane_tables(n_tokens, ident_rows):
    rows = n_tokens // GRID_W
    row = jnp.broadcast_to(jnp.arange(rows)[:, None], (rows, GRID_W)).reshape(n_tokens)
    col = jnp.broadcast_to(jnp.arange(GRID_W)[None, :], (rows, GRID_W)).reshape(n_tokens)
    n_freq = ROPE_DIM // 4
    inv = 1.0 / (ROPE_THETA ** (jnp.arange(n_freq, dtype=F32) / n_freq))
    ang = jnp.concatenate([row[:, None].astype(F32) * inv, col[:, None].astype(F32) * inv], axis=-1)
    cos, sin = jnp.cos(ang), jnp.sin(ang)
    ones = jnp.ones((n_tokens, QK_NOPE), F32)
    tail = HEAD_PAD - QK_HEAD
    cosf = jnp.concatenate([ones, cos, cos, jnp.ones((n_tokens, tail), F32)], axis=1)
    sinf = jnp.concatenate([0.0 * ones, -sin, sin, jnp.zeros((n_tokens, tail), F32)], axis=1)
    cosf = jnp.concatenate([cosf, jnp.ones((ident_rows, HEAD_PAD), F32)], axis=0)
    sinf = jnp.concatenate([sinf, jnp.zeros((ident_rows, HEAD_PAD), F32)], axis=0)
    return cosf, sinf


def _dft_tables(seq_len):
    def cs(n):
        k = np.arange(n, dtype=np.int64)
        ang = 2.0 * np.pi * ((k[:, None] * k[None, :]) % n).astype(np.float64) / n
        return np.cos(ang).astype(np.float32), np.sin(ang).astype(np.float32)

    cl, sl = cs(seq_len)
    cc, sc = cs(FNET_GC)
    return dict(cos=jnp.asarray(cl).astype(BF16), sin=jnp.asarray(sl).astype(BF16),
                chan=jnp.asarray(np.concatenate([cc, sc], axis=1)).astype(BF16))


def _layout_weights(w_in, w_q_b, w_kv_b, q_nope_g, q_rope_g, k_nope_g, k_rope_g, w_router, b_router):
    depth = w_in.shape[0]
    w_in_a = w_in[:, :, :COL_F].astype(BF16)
    w_in_b = w_in[:, :, COL_KPE + ROPE_DIM:].astype(BF16)
    w_q = jnp.pad(w_q_b.reshape(depth, Q_RANK, N_HEADS, QK_HEAD),
                  ((0, 0), (0, 0), (0, 0), (0, HEAD_PAD - QK_HEAD))).reshape(depth, Q_RANK, QK_PAD).astype(BF16)
    kv = w_kv_b.reshape(depth, KV_RANK, N_HEADS, QK_NOPE + V_HEAD)
    w_k = jnp.pad(kv[..., :QK_NOPE], ((0, 0), (0, 0), (0, 0), (0, HEAD_PAD - QK_NOPE))
                  ).reshape(depth, KV_RANK, QK_PAD).astype(BF16)
    w_v = kv[..., QK_NOPE:].reshape(depth, KV_RANK, V_WIDTH).astype(BF16)
    zq = jnp.zeros((depth, HEAD_PAD - QK_HEAD), F32)
    g_q = jnp.tile(jnp.concatenate([q_nope_g, q_rope_g, zq], axis=1), (1, N_HEADS))[:, None, :]
    g_k = jnp.tile(jnp.concatenate([k_nope_g, jnp.zeros((depth, HEAD_PAD - QK_NOPE), F32)], axis=1),
                   (1, N_HEADS))[:, None, :]
    g_kpe = jnp.concatenate([jnp.zeros((depth, QK_NOPE), F32), k_rope_g, zq], axis=1)[:, None, :]
    w_hi = w_router.astype(BF16)
    w_lo = (w_router - w_hi.astype(F32)).astype(BF16)
    w_r3 = jnp.pad(jnp.concatenate([w_hi, w_hi, w_lo], axis=1), ((0, 0), (0, 0), (0, ROUTER_PAD - N_EXPERTS)))
    b_r = jnp.pad(b_router, ((0, 0), (0, ROUTER_PAD - N_EXPERTS)), constant_values=NEG_BIG)[:, None, :]
    return dict(w_in_a=w_in_a, w_in_b=w_in_b, w_q=w_q, w_k=w_k, w_v=w_v, g_q=g_q, g_k=g_k, g_kpe=g_kpe, w_router3=w_r3, b_router=b_r)


def kernel(x_prompt, x_sample, cache_ckv, cache_kpe, c, c_ctx, w_mod, b_mod, norm1_g, norm2_g, w_in, q_a_g, kv_a_g, w_q_b, w_kv_b, q_nope_g, q_rope_g, k_nope_g, k_rope_g, w_attn_o, w_fnet_o, w_pool_grp, pool_scale, w_pool_o, w_out, w_router, b_router, w_gu, b_gu, w_down, b_down):
    batch, seq, d = x_prompt.shape
    dec_batch, dec_seq, _ = x_sample.shape
    depth = w_mod.shape[0]
    past = cache_ckv.shape[2]
    t_prompt = batch * seq
    t_sample = dec_batch * dec_seq
    assert d == D_MODEL and dec_batch + 1 <= 8
    assert t_prompt % TOKEN_TILE == 0 and dec_seq % TOKEN_TILE == 0 and seq % COMBINE_TILE == 0
    assert dec_seq % GRID_W == 0

    consts = _segment_matrices()
    consts["cosf"], consts["sinf"] = _rope_lane_tables(dec_seq, IN_TILE)
    consts["tri"] = jnp.asarray(np.tril(np.ones((TOKEN_TILE, TOKEN_TILE), np.float32), -1), BF16)

    wts = _layout_weights(w_in, w_q_b, w_kv_b, q_nope_g, q_rope_g, k_nope_g, k_rope_g, w_router, b_router)
    row3 = lambda a: a[:, None, :]
    wts.update(norm1_g=row3(norm1_g), norm2_g=row3(norm2_g), q_a_g=row3(q_a_g), kv_a_g=row3(kv_a_g),
               w_attn_o=w_attn_o.astype(BF16), w_fnet_o=w_fnet_o.astype(BF16), w_pool_o=w_pool_o.astype(BF16),
               w_out=w_out.astype(BF16), w_pool_grp=w_pool_grp.astype(BF16), pool_scale=row3(pool_scale))
    b_gu4 = b_gu[:, :, None, :]
    b_down4 = b_down[:, :, None, :]

    cond8 = jnp.concatenate([c_ctx[None, :], c, jnp.zeros((8 - 1 - dec_batch, d), F32)], axis=0)
    mods = _modulation(cond8, w_mod, b_mod).reshape(depth, 8, 6, D_MODEL)

    kpe_pad = jnp.pad(cache_kpe, ((0, 0), (0, 0), (0, 0), (QK_NOPE, HEAD_PAD - QK_HEAD)))
    k_ctx, v_ctx = _ctx_keys(cache_ckv, kpe_pad, wts, consts)

    streams = [
        dict(x=x_prompt.reshape(t_prompt, d), n_seq=batch, seq_len=seq, tok=dict(t_prompt=t_prompt, dec_seq=dec_seq),
             dft=_dft_tables(seq), ctx=None, moe=None),
        dict(x=x_sample.reshape(t_sample, d), n_seq=dec_batch, seq_len=dec_seq, tok=dict(t_prompt=0, dec_seq=dec_seq),
             dft=_dft_tables(dec_seq), ctx=(k_ctx, v_ctx, past), moe=None),
    ]
    ckv_list, kpe_list = [], []

    def mixers(st, l):
        moe = None if st["x"] is not None else (st["x1"], st["tw"], st["yg"])
        (q, k, v, ckv, kpe, f_in, p_in, st["gates"]), st["x"] = _in_projection(
            st["x"], moe, mods, l, wts, consts, st["tok"])
        if st["ctx"] is None:
            ckv_list.append(ckv.reshape(batch, seq, KV_RANK))
            kpe_list.append(kpe.reshape(batch, seq, ROPE_DIM))
        parts = [(k, v, None, st["seq_len"])]
        if st["ctx"] is not None:
            parts.append((st["ctx"][0], st["ctx"][1], l, st["ctx"][2]))
        st["attn"] = _attention(q, parts, st["n_seq"], st["seq_len"])
        st["fnet"] = _fourier(f_in, st["n_seq"], st["seq_len"], 0, st["dft"])
        st["pool"] = _pooling(p_in, st["n_seq"], st["seq_len"], 0, l, wts)

    def merge_and_route(st, l):
        t = st["n_seq"] * st["seq_len"]
        n_slots = (t * TOP_K + N_EXPERTS * (MOE_BLOCK - 1) + MOE_BLOCK - 1) // MOE_BLOCK * MOE_BLOCK
        st["x1"], st["h2p"], te_pad, st["tw"], rank_pad, counts_pad = _merge(
            st["x"], mods, st["attn"], st["fnet"], st["pool"], st["gates"], l, wts, consts, st["tok"])
        st["slot_by_choice"], st["slot_tok"], st["plan"] = _route(te_pad, rank_pad, counts_pad, n_slots)

    def dispatch(st):
        st["xs"] = _sc_row_gather(st["h2p"], st["slot_tok"])

    def experts(st, l):
        st["y"] = _experts(st["xs"], st["plan"], l, w_gu, b_gu4, w_down, b_down4)

    def gather_back(st):
        t = st["n_seq"] * st["seq_len"]
        st["yg"] = _sc_row_gather(st["y"], st["slot_by_choice"]).reshape(TOP_K, t, D_MODEL // 2)
        st["x"] = None

    def mixer_input(st):
        return "x" if st["x"] is not None else "yg"

    def tie(sa, ka, sb, kb):
        sa[ka], sb[kb] = lax.optimization_barrier((sa[ka], sb[kb]))

    a, b = streams
    mixers(a, 0)
    merge_and_route(a, 0)
    tie(b, mixer_input(b), a, "slot_tok")
    dispatch(a)
    mixers(b, 0)
    tie(a, "xs", b, "attn")
    experts(a, 0)
    for l in range(depth):
        tie(b, "x", a, "y")
        gather_back(a)
        merge_and_route(b, l)
        tie(a, "yg", b, "slot_tok")
        dispatch(b)
        if l + 1 == depth:
            break
        mixers(a, l + 1)
        tie(b, "xs", a, "attn")
        experts(b, l)
        tie(a, "x", b, "y")
        gather_back(b)
        merge_and_route(a, l + 1)
        tie(b, "yg", a, "slot_tok")
        dispatch(a)
        mixers(b, l + 1)
        tie(a, "xs", b, "attn")
        experts(a, l + 1)
    a["out"] = _combine(a["yg"], a["tw"], a["x1"], mods, depth - 1, a["tok"])
    tie(b, "xs", a, "out")
    experts(b, depth - 1)
    gather_back(b)
    out_a = a["out"]
    out_b = _combine(b["yg"], b["tw"], b["x1"], mods, depth - 1, b["tok"])
    y_prompt = out_a.reshape(batch, seq, d)
    y_sample = out_b.reshape(dec_batch, dec_seq, d)
    return (y_prompt, y_sample, jnp.stack(ckv_list, axis=1), jnp.stack(kpe_list, axis=1))
```

```python
import functools
import math

import numpy as np
import jax
import jax.numpy as jnp
from jax import lax
from jax.experimental import pallas as pl
from jax.experimental.pallas import tpu as pltpu
from jax.experimental.pallas import tpu_sc as plsc

D_MODEL = 1024
GRID_W = 64
N_HEADS = 8
QK_NOPE = 64
ROPE_DIM = 32
V_HEAD = 64
QK_HEAD = QK_NOPE + ROPE_DIM
Q_RANK = 384
KV_RANK = 256
ROPE_THETA = 10000.0
FNET_GROUPS = 4
FNET_GC = 128
FNET_WIDTH = FNET_GROUPS * FNET_GC
POOL_WINDOWS = (2, 4, 8, 16)
POOL_GC = 128
POOL_WIDTH = len(POOL_WINDOWS) * POOL_GC
N_BRANCHES = 3
N_EXPERTS = 32
TOP_K = 4
D_FF = D_MODEL
SWIGLU_LIMIT = 7.0
SWIGLU_ALPHA = 1.702
RMS_EPS = 1e-6

LANES = 128
HEAD_PAD = LANES
QK_PAD = N_HEADS * HEAD_PAD
V_WIDTH = N_HEADS * V_HEAD
ROPE_HALF = ROPE_DIM // 2
KPE_PAD = LANES
COL_KV = Q_RANK
COL_KPE = Q_RANK + KV_RANK
COL_F = COL_KPE + KPE_PAD
COL_P = COL_F + FNET_WIDTH
COL_G = COL_P + POOL_WIDTH
IN_PAD_WIDTH = COL_G + N_BRANCHES * D_MODEL
MIX_WIDTH = FNET_WIDTH + POOL_WIDTH
ROUTER_PAD = LANES
NEG_BIG = -1e30

TOKEN_TILE = 512
IN_TILE = 256
ATTN_Q_TILE = 512
SHORT_SEQ_ROWS = 1024
FNET_ROW_TILE = 512
POOL_CHUNK = 256
POOL_HALO = 16
MOE_BLOCK = 512
MOE_STEP = 64
GATE_COL_CHUNK = 512
MERGE_COL_CHUNK = 256
FF_CHUNK = 256
COMBINE_TILE = 256
SC_GATHER_ROWS = 64
MOD_COL_TILE = 1536
VMEM_LIMIT = 56 * 1024 * 1024

F32 = jnp.float32
BF16 = jnp.bfloat16


def _cparams(*sem):
    return pltpu.CompilerParams(dimension_semantics=sem, vmem_limit_bytes=VMEM_LIMIT)


def _dot(a, b):
    return jnp.dot(a, b, preferred_element_type=F32)


def _rms(x):
    return x * lax.rsqrt(jnp.mean(x * x, axis=-1, keepdims=True) + RMS_EPS)


def _pack_bf16_pairs(x):
    half = x.shape[1] // 2
    bits = pltpu.bitcast(x.astype(BF16).astype(F32), jnp.int32)
    return bits[:, :half] | lax.shift_right_logical(bits[:, half:], 16)


def _unpack_bf16_pairs(w):
    return pltpu.bitcast(w & jnp.int32(-65536), F32), pltpu.bitcast(lax.shift_left(w, 16), F32)


def _seg_sums(x, m_ref):
    return _dot((x * x).astype(BF16), m_ref[...])


def _seg_inv_rms(ss, inv_cnt, mt2_ref):
    r = lax.rsqrt(ss * inv_cnt + RMS_EPS)
    r_hi = r.astype(BF16)
    r_lo = (r - r_hi.astype(F32)).astype(BF16)
    return _dot(jnp.concatenate([r_hi, r_lo], axis=1), mt2_ref[...])


def _seg_rms_scale(x, m_ref, mt2_ref, inv_cnt):
    return _seg_inv_rms(_seg_sums(x, m_ref), inv_cnt, mt2_ref)


def _rope_key_chunk(kpe_chunk, gkpe_ref, rope):
    ms = jnp.sum(kpe_chunk * kpe_chunk, axis=-1, keepdims=True) * (1.0 / ROPE_DIM)
    kp = kpe_chunk * lax.rsqrt(ms + RMS_EPS) * gkpe_ref[...]
    return kp if rope is None else _rope_chunk(kp, *rope)


def _rope_chunk(xc, cosf, sinf, first_half):
    partner = jnp.where(first_half, pltpu.roll(xc, HEAD_PAD - ROPE_HALF, 1), pltpu.roll(xc, ROPE_HALF, 1))
    return xc * cosf + partner * sinf


def _write_keys(ckv, kpe_chunk, wk_ref, wv_ref, gk_ref, gkpe_ref, mk_ref, mkt2_ref, rope, k_out, v_out):
    cb = ckv.astype(BF16)
    kn = _dot(cb, wk_ref[...])
    kn = kn * _seg_rms_scale(kn, mk_ref, mkt2_ref, 1.0 / QK_NOPE) * gk_ref[...]
    v_out[...] = _dot(cb, wv_ref[...]).astype(v_out.dtype)
    kp = _rope_key_chunk(kpe_chunk, gkpe_ref, rope)
    for h in range(N_HEADS):
        sl = slice(h * HEAD_PAD, (h + 1) * HEAD_PAD)
        k_out[:, sl] = (kn[:, sl] + kp).astype(k_out.dtype)


def _mod_kernel(c_ref, w_ref, b_ref, o_ref):
    c = c_ref[...]
    s = c * jax.nn.sigmoid(c)
    o_ref[...] = _dot(s.astype(BF16), w_ref[...].astype(BF16)) + b_ref[...]


def _modulation(cond8, w_mod, b_mod):
    depth = w_mod.shape[0]
    n = w_mod.shape[2]
    return pl.pallas_call(
        _mod_kernel,
        out_shape=jax.ShapeDtypeStruct((depth, 8, n), F32),
        grid=(depth, n // MOD_COL_TILE),
        in_specs=[
            pl.BlockSpec((8, D_MODEL), lambda l, j: (0, 0)),
            pl.BlockSpec((None, D_MODEL, MOD_COL_TILE), lambda l, j: (l, 0, j)),
            pl.BlockSpec((None, 1, MOD_COL_TILE), lambda l, j: (l, 0, j)),
        ],
        out_specs=pl.BlockSpec((None, 8, MOD_COL_TILE), lambda l, j: (l, 0, j)),
        compiler_params=_cparams("arbitrary", "arbitrary"),
        name="adaln_modulation",
    )(cond8, w_mod, b_mod.reshape(depth, 1, n))


def _moe_residual(tw_ref, x1_ref, mod_ref, yg_ref):
    acc = None
    for k in range(TOP_K):
        w = tw_ref[:, k:k + 1]
        part = [w * half for half in _unpack_bf16_pairs(yg_ref[k])]
        acc = part if acc is None else [a + p for a, p in zip(acc, part)]
    return x1_ref[...] + mod_ref[5:6, :] * jnp.concatenate(acc, axis=1)


def _in_kernel(*refs, after_moe):
    if after_moe:
        x = _moe_residual(*refs[:4])
        refs = refs[4:]
        refs[-1][...] = x
        refs = refs[:-1]
    else:
        x = refs[0][...]
        refs = refs[1:]
    (mod_ref, g1_ref, wa_ref, wb_ref, qag_ref, kvag_ref, wq_ref, wk_ref, wv_ref,
     gq_ref, gk_ref, gkpe_ref, cos_ref, sin_ref, mq_ref, mqt2_ref, icq_ref, mk_ref, mkt2_ref,
     q_out, k_out, v_out, ckv_out, kpe_out, f_out, p_out, g_out) = refs
    shift = mod_ref[0:1, :]
    scale = mod_ref[1:2, :]
    h = _rms(x) * g1_ref[...] * (1.0 + scale) + shift
    hb = h.astype(BF16)

    gate_chunks = iter(range(0, N_BRANCHES * D_MODEL, GATE_COL_CHUNK))

    def gate_chunk():
        c0 = next(gate_chunks)
        zc = _dot(hb, wb_ref[:, MIX_WIDTH + c0:MIX_WIDTH + c0 + GATE_COL_CHUNK])
        g_out[:, c0:c0 + GATE_COL_CHUNK] = jax.nn.sigmoid(zc).astype(g_out.dtype)

    za = _dot(hb, wa_ref[...])
    zb = _dot(hb, wb_ref[:, 0:MIX_WIDTH])
    f_out[...] = zb[:, 0:FNET_WIDTH].astype(f_out.dtype)
    p_out[...] = zb[:, FNET_WIDTH:]
    q_lat = za[:, 0:COL_KV]
    kv_lat = za[:, COL_KV:COL_KPE]
    key_lane = lax.broadcasted_iota(jnp.int32, (hb.shape[0], KPE_PAD), 1) < ROPE_DIM
    kpe_grp = jnp.where(key_lane, za[:, COL_KPE:COL_F], 0.0)
    ckv = _rms(kv_lat) * kvag_ref[...]
    ckv_out[...] = ckv
    kpe_out[...] = kpe_grp[:, 0:ROPE_DIM]
    qn = (_rms(q_lat) * qag_ref[...]).astype(BF16)
    cb = ckv.astype(BF16)
    gate_chunk()

    q = _dot(qn, wq_ref[...])
    kn = _dot(cb, wk_ref[...])
    v_out[...] = _dot(cb, wv_ref[...]).astype(v_out.dtype)
    gate_chunk()

    ss_q = _seg_sums(q, mq_ref)
    ss_k = _seg_sums(kn, mk_ref)
    gate_chunk()

    q = q * _seg_inv_rms(ss_q, icq_ref[...], mqt2_ref) * gq_ref[...]
    kn = kn * _seg_inv_rms(ss_k, 1.0 / QK_NOPE, mkt2_ref) * gk_ref[...]
    for _ in range(N_BRANCHES * D_MODEL // GATE_COL_CHUNK - 3):
        gate_chunk()

    cosf = cos_ref[...]
    sinf = sin_ref[...]
    lane = lax.broadcasted_iota(jnp.int32, cosf.shape, 1)
    rope = (cosf, sinf, lane < QK_NOPE + ROPE_HALF)
    sm_scale = 1.0 / math.sqrt(QK_HEAD)
    kp = _rope_key_chunk(pltpu.roll(kpe_grp, QK_NOPE, 1), gkpe_ref, rope)
    for hd in range(N_HEADS):
        sl = slice(hd * HEAD_PAD, (hd + 1) * HEAD_PAD)
        q_out[:, sl] = (_rope_chunk(q[:, sl], *rope) * sm_scale).astype(q_out.dtype)
        k_out[:, sl] = (kn[:, sl] + kp).astype(k_out.dtype)


def _const_spec(shape):
    nd = len(shape)
    return pl.BlockSpec(shape, lambda i, _n=nd: (0,) * _n, pipeline_mode=pl.Buffered(1))


def _layer_spec(shape, l):
    nd = len(shape)
    return pl.BlockSpec((None,) + tuple(shape), lambda i, _l=l, _n=nd: (_l,) + (0,) * _n,
                        pipeline_mode=pl.Buffered(1))


def _in_projection(x, moe, mods, l, wts, consts, tok):
    after_moe = x is None
    t = moe[0].shape[0] if after_moe else x.shape[0]
    tm = IN_TILE
    n_tiles = t // tm
    p_tiles = tok["t_prompt"] // tm
    s_tiles = tok["dec_seq"] // tm
    rope_blocks = tok["dec_seq"] // tm

    def mod_row(i):
        return jnp.where(i < p_tiles, 0, 1 + (i - p_tiles) // s_tiles)

    def rope_idx(i):
        return (jnp.where(i < p_tiles, rope_blocks, (i - p_tiles) % s_tiles), 0)

    row = lambda w: pl.BlockSpec((tm, w), lambda i: (i, 0))
    if after_moe:
        x1, tw, yg = moe
        lead_specs = [row(ROUTER_PAD), row(D_MODEL),
                      pl.BlockSpec((None, None, 6, D_MODEL), lambda i: (l - 1, mod_row(i), 0, 0)),
                      pl.BlockSpec((TOP_K, tm, D_MODEL // 2), lambda i: (0, i, 0))]
        lead_args = [tw, x1, mods, yg]
    else:
        lead_specs = [row(D_MODEL)]
        lead_args = [x]
    in_specs = lead_specs + [
        pl.BlockSpec((None, None, 6, D_MODEL), lambda i: (l, mod_row(i), 0, 0)),
        _layer_spec((1, D_MODEL), l),
        _layer_spec((D_MODEL, COL_F), l),
        _layer_spec((D_MODEL, MIX_WIDTH + N_BRANCHES * D_MODEL), l),
        _layer_spec((1, Q_RANK), l),
        _layer_spec((1, KV_RANK), l),
        _layer_spec((Q_RANK, QK_PAD), l),
        _layer_spec((KV_RANK, QK_PAD), l),
        _layer_spec((KV_RANK, V_WIDTH), l),
        _layer_spec((1, QK_PAD), l),
        _layer_spec((1, QK_PAD), l),
        _layer_spec((1, HEAD_PAD), l),
        pl.BlockSpec((tm, HEAD_PAD), rope_idx),
        pl.BlockSpec((tm, HEAD_PAD), rope_idx),
        _const_spec((QK_PAD, LANES)),
        _const_spec((2 * LANES, QK_PAD)),
        _const_spec((1, LANES)),
        _const_spec((QK_PAD, LANES)),
        _const_spec((2 * LANES, QK_PAD)),
    ]
    out_shape = [
        jax.ShapeDtypeStruct((t, QK_PAD), BF16),
        jax.ShapeDtypeStruct((t, QK_PAD), BF16),
        jax.ShapeDtypeStruct((t, V_WIDTH), BF16),
        jax.ShapeDtypeStruct((t, KV_RANK), F32),
        jax.ShapeDtypeStruct((t, ROPE_DIM), F32),
        jax.ShapeDtypeStruct((t, FNET_WIDTH), BF16),
        jax.ShapeDtypeStruct((t, POOL_WIDTH), F32),
        jax.ShapeDtypeStruct((t, N_BRANCHES * D_MODEL), BF16),
    ]
    if after_moe:
        out_shape.append(jax.ShapeDtypeStruct((t, D_MODEL), F32))
    out_specs = [row(s.shape[1]) for s in out_shape]
    outs = pl.pallas_call(
        functools.partial(_in_kernel, after_moe=after_moe),
        out_shape=out_shape,
        grid=(n_tiles,),
        in_specs=in_specs,
        out_specs=out_specs,
        compiler_params=_cparams("arbitrary"),
        cost_estimate=pl.CostEstimate(
            flops=2 * t * (D_MODEL * IN_PAD_WIDTH + Q_RANK * QK_PAD + KV_RANK * (QK_PAD + V_WIDTH)
                           + 2 * (QK_PAD * LANES + 2 * LANES * QK_PAD)),
            transcendentals=t * N_BRANCHES * D_MODEL,
            bytes_accessed=2 * D_MODEL * IN_PAD_WIDTH + sum(
                math.prod(s.shape) * s.dtype.itemsize for s in out_shape)
            + 4 * t * D_MODEL * (1 + (TOP_K // 2 if after_moe else 0))),
        name="in_projection",
    )(*lead_args, mods, wts["norm1_g"], wts["w_in_a"], wts["w_in_b"], wts["q_a_g"], wts["kv_a_g"], wts["w_q"], wts["w_k"], wts["w_v"],
      wts["g_q"], wts["g_k"], wts["g_kpe"], consts["cosf"], consts["sinf"],
      consts["m_q"], consts["mt2_q"], consts["inv_cnt_q"], consts["m_k"], consts["mt2_k"])
    return (outs[:8], outs[8]) if after_moe else (outs, x)


def _ctx_keys_kernel(ckv_ref, kpe_ref, wk_ref, wv_ref, gk_ref, gkpe_ref, mk_ref, mkt2_ref, k_out, v_out):
    _write_keys(ckv_ref[...], kpe_ref[...], wk_ref, wv_ref, gk_ref, gkpe_ref, mk_ref, mkt2_ref, None, k_out, v_out)


def _ctx_keys(cache_ckv, cache_kpe_pad, wts, consts):
    nb, depth, past, _ = cache_ckv.shape
    lw = lambda shape: pl.BlockSpec((None,) + shape, lambda l, b: (l,) + (0,) * len(shape))
    cs = lambda shape: pl.BlockSpec(shape, lambda l, b: (0,) * len(shape))
    return pl.pallas_call(
        _ctx_keys_kernel,
        out_shape=[jax.ShapeDtypeStruct((depth, nb * past, QK_PAD), BF16),
                   jax.ShapeDtypeStruct((depth, nb * past, V_WIDTH), BF16)],
        grid=(depth, nb),
        in_specs=[
            pl.BlockSpec((None, None, past, KV_RANK), lambda l, b: (b, l, 0, 0)),
            pl.BlockSpec((None, None, past, HEAD_PAD), lambda l, b: (b, l, 0, 0)),
            lw((KV_RANK, QK_PAD)), lw((KV_RANK, V_WIDTH)), lw((1, QK_PAD)), lw((1, HEAD_PAD)),
            cs((QK_PAD, LANES)), cs((2 * LANES, QK_PAD)),
        ],
        out_specs=[pl.BlockSpec((None, past, QK_PAD), lambda l, b: (l, b, 0)),
                   pl.BlockSpec((None, past, V_WIDTH), lambda l, b: (l, b, 0))],
        compiler_params=_cparams("arbitrary", "arbitrary"),
        name="context_keys",
    )(cache_ckv, cache_kpe_pad, wts["w_k"], wts["w_v"], wts["g_k"], wts["g_kpe"], consts["m_k"], consts["mt2_k"])


def _attn_kernel(*refs, n_parts, seqs):
    q_ref = refs[0]
    k_refs = refs[1:1 + 2 * n_parts:2]
    v_refs = refs[2:2 + 2 * n_parts:2]
    o_ref = refs[1 + 2 * n_parts]
    q_rows = q_ref.shape[0] // seqs
    lane = lax.broadcasted_iota(jnp.int32, (q_rows, 2 * V_HEAD), 1)
    for s in range(seqs):
        rows = slice(s * q_rows, (s + 1) * q_rows)
        keys = [slice(s * (k.shape[0] // seqs), (s + 1) * (k.shape[0] // seqs)) for k in k_refs]
        for pair in range(N_HEADS // 2):
            vsl = slice(pair * 2 * V_HEAD, (pair + 1) * 2 * V_HEAD)
            outs = []
            for hd in (2 * pair, 2 * pair + 1):
                sl = slice(hd * HEAD_PAD, (hd + 1) * HEAD_PAD)
                qh = q_ref[rows, sl]
                ss = [lax.dot_general(qh, k[ks, sl], (((1,), (1,)), ((), ())), preferred_element_type=F32)
                      for k, ks in zip(k_refs, keys)]
                m = functools.reduce(jnp.maximum, [jnp.max(sc, axis=-1, keepdims=True) for sc in ss])
                es = [jnp.exp(sc - m) for sc in ss]
                den = functools.reduce(jnp.add, [jnp.sum(e, axis=-1, keepdims=True) for e in es])
                acc = functools.reduce(jnp.add, [_dot(e.astype(BF16), v[ks, vsl])
                                                 for e, v, ks in zip(es, v_refs, keys)])
                outs.append(acc / den)
            o_ref[rows, vsl] = jnp.where(lane < V_HEAD, outs[0], outs[1]).astype(o_ref.dtype)


def _attention(q, kv_parts, n_seq, seq_len):
    tq = min(ATTN_Q_TILE, seq_len)
    nq = seq_len // tq
    n_keys = sum(rows for _, _, _, rows in kv_parts)
    seqs = math.gcd(n_seq, max(1, SHORT_SEQ_ROWS // seq_len)) if nq == 1 and len(kv_parts) == 1 else 1
    in_specs = [pl.BlockSpec((seqs * tq, QK_PAD), lambda b, i: (b * nq + i, 0))]
    args = [q]
    for k, v, layer, rows in kv_parts:
        for arr, width in ((k, QK_PAD), (v, V_WIDTH)):
            if layer is None:
                in_specs.append(pl.BlockSpec((seqs * rows, width), lambda b, i: (b, 0)))
            else:
                in_specs.append(pl.BlockSpec((None, rows, width), lambda b, i, _l=layer: (_l, b, 0)))
            args.append(arr)
    return pl.pallas_call(
        functools.partial(_attn_kernel, n_parts=len(kv_parts), seqs=seqs),
        out_shape=jax.ShapeDtypeStruct((n_seq * seq_len, V_WIDTH), BF16),
        grid=(n_seq // seqs, nq),
        in_specs=in_specs,
        out_specs=pl.BlockSpec((seqs * tq, V_WIDTH), lambda b, i: (b * nq + i, 0)),
        compiler_params=_cparams("arbitrary", "arbitrary"),
        cost_estimate=pl.CostEstimate(
            flops=2 * n_seq * seq_len * n_keys * N_HEADS * (HEAD_PAD + 2 * V_HEAD),
            transcendentals=n_seq * seq_len * n_keys * N_HEADS,
            bytes_accessed=2 * n_seq * (seq_len * (QK_PAD + V_WIDTH) + n_keys * (QK_PAD + V_WIDTH))),
        name="attention",
    )(*args)


def _fnet_kernel(f_ref, cs_ref, cl_ref, sl_ref, o_ref, top_ref, bot_ref, *, norm, seqs):
    @pl.when(pl.program_id(1) == 0)
    def _():
        for g in range(FNET_GROUPS):
            sl = slice(g * FNET_GC, (g + 1) * FNET_GC)
            a = _dot(f_ref[:, sl], cs_ref[...])
            top_ref[:, sl] = a[:, :FNET_GC].astype(BF16)
            bot_ref[:, sl] = a[:, FNET_GC:].astype(BF16)

    seq_len = f_ref.shape[0] // seqs
    rows = o_ref.shape[0] // seqs
    for s in range(seqs):
        src = slice(s * seq_len, (s + 1) * seq_len)
        y = _dot(cl_ref[...], top_ref[src, :]) - _dot(sl_ref[...], bot_ref[src, :])
        o_ref[s * rows:(s + 1) * rows, :] = (y * norm).astype(o_ref.dtype)


def _fourier(f_in, n_seq, seq_len, row0, tabs):
    tr = min(FNET_ROW_TILE, seq_len)
    nj = seq_len // tr
    seqs = math.gcd(n_seq, max(1, SHORT_SEQ_ROWS // seq_len)) if nj == 1 else 1
    sb0 = row0 // (seqs * seq_len)
    return pl.pallas_call(
        functools.partial(_fnet_kernel, norm=1.0 / math.sqrt(seq_len * FNET_GC), seqs=seqs),
        out_shape=jax.ShapeDtypeStruct((n_seq * seq_len, FNET_WIDTH), BF16),
        grid=(n_seq // seqs, nj),
        in_specs=[
            pl.BlockSpec((seqs * seq_len, FNET_WIDTH), lambda b, j: (sb0 + b, 0)),
            pl.BlockSpec((FNET_GC, 2 * FNET_GC), lambda b, j: (0, 0)),
            pl.BlockSpec((tr, seq_len), lambda b, j: (j, 0)),
            pl.BlockSpec((tr, seq_len), lambda b, j: (j, 0)),
        ],
        out_specs=pl.BlockSpec((seqs * tr, FNET_WIDTH), lambda b, j: (b * nj + j, 0)),
        scratch_shapes=[pltpu.VMEM((seqs * seq_len, FNET_WIDTH), BF16),
                        pltpu.VMEM((seqs * seq_len, FNET_WIDTH), BF16)],
        compiler_params=_cparams("arbitrary", "arbitrary"),
        name="fourier_mix",
    )(f_in, tabs["chan"], tabs["cos"], tabs["sin"])


def _pool_kernel(p_ref, wg_ref, ps_ref, o_ref, pad_ref, *, seqs):
    seq_len = p_ref.shape[0] // seqs
    zeros = jnp.zeros((POOL_HALO, POOL_WIDTH), F32)
    ch = min(POOL_CHUNK, seq_len)
    for s in range(seqs):
        s0 = s * seq_len
        pad_ref[s, 0:POOL_HALO, :] = zeros
        pad_ref[s, POOL_HALO + seq_len:, :] = zeros
        pad_ref[s, POOL_HALO:POOL_HALO + seq_len, :] = p_ref[s0:s0 + seq_len, :]
        for c in range(seq_len // ch):
            t = lax.broadcasted_iota(jnp.int32, (ch, 1), 0) + c * ch
            for g, w in enumerate(POOL_WINDOWS):
                half = w // 2
                sl = slice(g * POOL_GC, (g + 1) * POOL_GC)
                acc = None
                for j in range(-half, half):
                    r0 = POOL_HALO + c * ch + j
                    part = pad_ref[s, r0:r0 + ch, sl]
                    acc = part if acc is None else acc + part
                cnt = (jnp.minimum(t + half, seq_len) - jnp.maximum(t - half, 0)).astype(F32)
                pooled = acc / cnt - p_ref[s0 + c * ch:s0 + (c + 1) * ch, sl]
                mixed = _dot(pooled.astype(BF16), wg_ref[g]) * ps_ref[:, sl]
                o_ref[s0 + c * ch:s0 + (c + 1) * ch, sl] = mixed.astype(o_ref.dtype)


def _pooling(p_in, n_seq, seq_len, row0, l, wts):
    seqs = math.gcd(n_seq, max(1, SHORT_SEQ_ROWS // seq_len))
    sb0 = row0 // (seqs * seq_len)
    g = len(POOL_WINDOWS)
    return pl.pallas_call(
        functools.partial(_pool_kernel, seqs=seqs),
        out_shape=jax.ShapeDtypeStruct((n_seq * seq_len, POOL_WIDTH), BF16),
        grid=(n_seq // seqs,),
        in_specs=[
            pl.BlockSpec((seqs * seq_len, POOL_WIDTH), lambda b: (sb0 + b, 0)),
            pl.BlockSpec((None, g, POOL_GC, POOL_GC), lambda b: (l, 0, 0, 0)),
            pl.BlockSpec((None, 1, POOL_WIDTH), lambda b: (l, 0, 0)),
        ],
        out_specs=pl.BlockSpec((seqs * seq_len, POOL_WIDTH), lambda b: (b, 0)),
        scratch_shapes=[pltpu.VMEM((seqs, seq_len + 2 * POOL_HALO, POOL_WIDTH), F32)],
        compiler_params=_cparams("arbitrary"),
        name="pool_mix",
    )(p_in, wts["w_pool_grp"], wts["pool_scale"])


def _merge_kernel(x_ref, mod_ref, a_ref, f_ref, p_ref, g_ref,
                  wa_ref, wf_ref, wp_ref, wo_ref, g2_ref, wr_ref, br_ref, tri_ref,
                  x1_out, hp_out, tw_out, er_out, cnt_out, mix_scr, carry_ref):
    j = pl.program_id(0)

    @pl.when(j == 0)
    def _():
        mix_scr[1] = jnp.zeros(mix_scr.shape[1:], F32)

    @pl.when(j <= 1)
    def _():
        carry_ref[...] = jnp.zeros(carry_ref.shape, F32)

    def branch_chunks(lo, hi):
        out = []
        for c0 in range(lo, hi, MERGE_COL_CHUNK):
            cs = slice(c0, c0 + MERGE_COL_CHUNK)
            a = _dot(a_ref[...], wa_ref[:, cs])
            f = _dot(f_ref[...], wf_ref[:, cs])
            p = _dot(p_ref[...], wp_ref[:, cs])
            out.append((g_ref[:, c0:c0 + MERGE_COL_CHUNK].astype(F32) * a
                        + g_ref[:, D_MODEL + c0:D_MODEL + c0 + MERGE_COL_CHUNK].astype(F32) * f
                        + g_ref[:, 2 * D_MODEL + c0:2 * D_MODEL + c0 + MERGE_COL_CHUNK].astype(F32) * p
                        ).astype(BF16))
        return out

    def step(cur, prev):
        chunks = branch_chunks(0, D_MODEL // 2)

        gate1 = mod_ref[2:3, :]
        shift2 = mod_ref[3:4, :]
        scale2 = mod_ref[4:5, :]
        x1 = x_ref[...] + gate1 * mix_scr[prev]
        x1_out[...] = x1
        h2 = _rms(x1) * g2_ref[...] * (1.0 + scale2) + shift2
        hp_out[...] = _pack_bf16_pairs(h2)
        h_hi = h2.astype(BF16)
        h_lo = (h2 - h_hi.astype(F32)).astype(BF16)
        logits = _dot(jnp.concatenate([h_hi, h_lo, h_hi], axis=1), wr_ref[...]) + br_ref[...]

        chunks += branch_chunks(D_MODEL // 2, D_MODEL)
        mix_scr[cur] = _dot(jnp.concatenate(chunks, axis=1), wo_ref[...])

        lane = lax.broadcasted_iota(jnp.int32, logits.shape, 1)
        work = logits
        vals, idxs = [], []
        for _ in range(TOP_K):
            m = jnp.max(work, axis=-1, keepdims=True)
            idx = jnp.min(jnp.where(work == m, lane, ROUTER_PAD), axis=-1, keepdims=True)
            vals.append(m)
            idxs.append(idx)
            work = jnp.where(lane == idx, -jnp.inf, work)
        es = [jnp.exp(v - vals[0]) for v in vals]
        den = functools.reduce(jnp.add, es)

        chosen = functools.reduce(jnp.logical_or, [lane == idx for idx in idxs])
        hot = jnp.where(chosen, 1.0, 0.0)
        before = _dot(tri_ref[...], hot.astype(BF16)) + carry_ref[...]
        carry_ref[...] = carry_ref[...] + jnp.sum(hot, axis=0, keepdims=True)
        cnt_out[...] = carry_ref[...].astype(jnp.int32)

        er = jnp.zeros(logits.shape, jnp.int32)
        tw = jnp.zeros(logits.shape, F32)
        for k in range(TOP_K):
            rank_k = jnp.sum(jnp.where(lane == idxs[k], before, 0.0), axis=-1, keepdims=True).astype(jnp.int32)
            er = jnp.where(lane == k, idxs[k], er)
            er = jnp.where(lane == TOP_K + k, rank_k, er)
            tw = jnp.where(lane == k, es[k] / den, tw)
        tw_out[...] = tw
        er_out[...] = jnp.transpose(er)[0:2 * TOP_K, :]

    for parity in (0, 1):
        pl.when(j % 2 == parity)(functools.partial(step, parity, 1 - parity))


def _merge(x, mods, attn, fnet, pool, gates, l, wts, consts, tok):
    t = x.shape[0]
    tm = TOKEN_TILE
    p_tiles = tok["t_prompt"] // tm
    s_tiles = tok["dec_seq"] // tm

    n_tiles = t // tm
    routed = lambda j: jnp.maximum(j - 1, 0)
    mixed = lambda j: jnp.minimum(j, n_tiles - 1)

    def mod_idx(j):
        i = routed(j)
        row = jnp.where(i < p_tiles, 0, 1 + (i - p_tiles) // s_tiles)
        return (l, row, 0, 0)

    row = lambda w: pl.BlockSpec((tm, w), lambda j: (routed(j), 0))
    head = lambda w: pl.BlockSpec((tm, w), lambda j: (mixed(j), 0))
    out_shape = [
        jax.ShapeDtypeStruct((t, D_MODEL), F32),
        jax.ShapeDtypeStruct((t, D_MODEL // 2), jnp.int32),
        jax.ShapeDtypeStruct((t, ROUTER_PAD), F32),
    ]
    return pl.pallas_call(
        _merge_kernel,
        out_shape=out_shape + [jax.ShapeDtypeStruct((2 * TOP_K, t), jnp.int32),
                               jax.ShapeDtypeStruct((1, ROUTER_PAD), jnp.int32)],
        grid=(n_tiles + 1,),
        in_specs=[
            row(D_MODEL),
            pl.BlockSpec((None, None, 6, D_MODEL), mod_idx),
            head(V_WIDTH), head(FNET_WIDTH), head(POOL_WIDTH), head(N_BRANCHES * D_MODEL),
            _layer_spec((V_WIDTH, D_MODEL), l),
            _layer_spec((FNET_WIDTH, D_MODEL), l),
            _layer_spec((POOL_WIDTH, D_MODEL), l),
            _layer_spec((D_MODEL, D_MODEL), l),
            _layer_spec((1, D_MODEL), l),
            _layer_spec((3 * D_MODEL, ROUTER_PAD), l),
            _layer_spec((1, ROUTER_PAD), l),
            _const_spec((tm, tm)),
        ],
        out_specs=[row(s.shape[1]) for s in out_shape] + [
            pl.BlockSpec((2 * TOP_K, tm), lambda j: (0, routed(j))),
            pl.BlockSpec((1, ROUTER_PAD), lambda j: (0, 0))],
        scratch_shapes=[pltpu.VMEM((2, tm, D_MODEL), F32), pltpu.VMEM((1, ROUTER_PAD), F32)],
        compiler_params=_cparams("arbitrary"),
        cost_estimate=pl.CostEstimate(
            flops=2 * t * (D_MODEL * (V_WIDTH + FNET_WIDTH + POOL_WIDTH + D_MODEL) + 3 * D_MODEL * ROUTER_PAD
                           + tm * ROUTER_PAD),
            transcendentals=t * TOP_K,
            bytes_accessed=t * (2 * (V_WIDTH + FNET_WIDTH + POOL_WIDTH + N_BRANCHES * D_MODEL) + 4 * 2 * D_MODEL
                                + 4 * (D_MODEL // 2) + 3 * 4 * ROUTER_PAD)
            + 2 * D_MODEL * (V_WIDTH + FNET_WIDTH + POOL_WIDTH + D_MODEL + 3 * ROUTER_PAD)),
        name="merge_router",
    )(x, mods, attn, fnet, pool, gates, wts["w_attn_o"], wts["w_fnet_o"], wts["w_pool_o"], wts["w_out"],
      wts["norm2_g"], wts["w_router3"], wts["b_router"], consts["tri"])


def _sc_row_gather(table, idx):
    n = idx.shape[0]
    d = table.shape[1]
    info = plsc.get_sparse_core_info()
    n_cores = info.num_cores
    n_workers = n_cores * info.num_subcores
    per_worker = n // n_workers
    n_chunks = per_worker // SC_GATHER_ROWS
    assert n_chunks * SC_GATHER_ROWS * n_workers == n
    mesh = plsc.VectorSubcoreMesh(core_axis_name="core", subcore_axis_name="subcore")

    @functools.partial(
        pl.kernel, mesh=mesh, out_type=jax.ShapeDtypeStruct((n, d), table.dtype),
        scratch_types=[pltpu.VMEM((SC_GATHER_ROWS,), jnp.int32), pltpu.VMEM((SC_GATHER_ROWS, d), table.dtype),
                       pltpu.SemaphoreType.DMA],
        cost_estimate=pl.CostEstimate(flops=0, transcendentals=0,
                                      bytes_accessed=2 * n * d * table.dtype.itemsize + 4 * n),
        name="sc_row_gather")
    def gather(table_hbm, idx_hbm, out_hbm, idx_v, rows_v, sem):
        worker = lax.axis_index("subcore") * n_cores + lax.axis_index("core")

        @pl.loop(0, n_chunks)
        def _(c):
            base = worker * per_worker + c * SC_GATHER_ROWS
            pltpu.sync_copy(idx_hbm.at[pl.ds(base, SC_GATHER_ROWS)], idx_v)
            pltpu.async_copy(table_hbm.at[idx_v], rows_v, sem).wait()
            pltpu.sync_copy(rows_v, out_hbm.at[pl.ds(base, SC_GATHER_ROWS)])

    return gather(table, idx)


PLAN_EXPERT, PLAN_VALID, PLAN_FIRST, PLAN_SLOT, PLAN_NEXT, PLAN_USED = range(6)


def _expert_kernel(plan_ref, xs_ref, bgu_ref, bd_ref, wgu_hbm, wd_hbm, y_ref,
                   wgu_f32, wd_f32, wgu_bf, wd_bf, sems, *, layer):
    b = pl.program_id(0)

    def weight_copies(expert, slot):
        return (pltpu.make_async_copy(wgu_hbm.at[layer, expert], wgu_f32.at[slot], sems.at[0, slot]),
                pltpu.make_async_copy(wd_hbm.at[layer, expert], wd_f32.at[slot], sems.at[1, slot]))

    def ffn(rows):
        x = jnp.concatenate(_unpack_bf16_pairs(xs_ref[0:rows, :]), axis=1).astype(BF16)
        gu = _dot(x, wgu_bf[...]) + bgu_ref[...]
        glu = jnp.minimum(gu[:, :D_FF], SWIGLU_LIMIT)
        lin = jnp.clip(gu[:, D_FF:], -SWIGLU_LIMIT, SWIGLU_LIMIT)
        act = glu * jax.nn.sigmoid(SWIGLU_ALPHA * glu) * (lin + 1.0)
        y_ref[0:rows, :] = _pack_bf16_pairs(_dot(act.astype(BF16), wd_bf[...]) + bd_ref[...])
        if rows < MOE_BLOCK:
            y_ref[rows:, :] = jnp.zeros((MOE_BLOCK - rows, D_MODEL // 2), y_ref.dtype)

    n_used = plan_ref[PLAN_USED, 0]

    @pl.when(b < n_used)
    def _():
        slot = plan_ref[PLAN_SLOT, b]

        @pl.when(b == 0)
        def _():
            for cp in weight_copies(plan_ref[PLAN_EXPERT, 0], 0):
                cp.start()

        @pl.when(plan_ref[PLAN_FIRST, b] == 1)
        def _():
            for cp in weight_copies(plan_ref[PLAN_EXPERT, b], slot):
                cp.wait()
            wgu_bf[...] = wgu_f32[slot].astype(BF16)
            wd_bf[...] = wd_f32[slot].astype(BF16)

            @pl.when(plan_ref[PLAN_NEXT, b] >= 0)
            def _():
                for cp in weight_copies(plan_ref[PLAN_NEXT, b], 1 - slot):
                    cp.start()

        steps = (plan_ref[PLAN_VALID, b] + (MOE_STEP - 1)) // MOE_STEP
        for rows in range(MOE_STEP, MOE_BLOCK + 1, MOE_STEP):
            want = steps <= 1 if rows == MOE_STEP else steps == rows // MOE_STEP
            pl.when(want)(functools.partial(ffn, rows))

    @pl.when(b >= n_used)
    def _():
        y_ref[...] = jnp.zeros(y_ref.shape, y_ref.dtype)


def _experts(xs, plan, l, w_gu, b_gu4, w_down, b_down4):
    n_slots = xs.shape[0]
    bm = MOE_BLOCK
    n_blocks = n_slots // bm

    def blk(b, plan):
        return jnp.minimum(b, plan[PLAN_USED, 0] - 1)

    def bspec(width):
        return pl.BlockSpec((None, None, 1, width), lambda b, plan: (l, plan[PLAN_EXPERT, blk(b, plan)], 0, 0))

    grid_spec = pltpu.PrefetchScalarGridSpec(
        num_scalar_prefetch=1,
        grid=(n_blocks,),
        in_specs=[
            pl.BlockSpec((bm, D_MODEL // 2), lambda b, plan: (blk(b, plan), 0)),
            bspec(2 * D_FF), bspec(D_MODEL),
            pl.BlockSpec(memory_space=pl.ANY), pl.BlockSpec(memory_space=pl.ANY),
        ],
        out_specs=pl.BlockSpec((bm, D_MODEL // 2), lambda b, plan: (b, 0)),
        scratch_shapes=[pltpu.VMEM((2, D_MODEL, 2 * D_FF), F32), pltpu.VMEM((2, D_FF, D_MODEL), F32),
                        pltpu.VMEM((D_MODEL, 2 * D_FF), BF16), pltpu.VMEM((D_FF, D_MODEL), BF16),
                        pltpu.SemaphoreType.DMA((2, 2))],
    )
    return pl.pallas_call(
        functools.partial(_expert_kernel, layer=l),
        out_shape=jax.ShapeDtypeStruct((n_slots, D_MODEL // 2), jnp.int32),
        grid_spec=grid_spec,
        compiler_params=_cparams("arbitrary"),
        cost_estimate=pl.CostEstimate(
            flops=2 * n_slots * 3 * D_MODEL * D_FF, transcendentals=n_slots * D_FF,
            bytes_accessed=4 * N_EXPERTS * 3 * D_MODEL * D_FF + 2 * 4 * n_slots * (D_MODEL // 2)),
        name="moe_experts",
    )(plan, xs, b_gu4, b_down4, w_gu, w_down)


def _combine_kernel(tw_ref, x_ref, mod_ref, yg_ref, o_ref):
    o_ref[...] = _moe_residual(tw_ref, x_ref, mod_ref, yg_ref)


def _combine(yg, tw, x1, mods, l, tok):
    t = x1.shape[0]
    tc = COMBINE_TILE
    p_tiles = tok["t_prompt"] // tc
    s_tiles = tok["dec_seq"] // tc

    def mod_idx(i):
        row = jnp.where(i < p_tiles, 0, 1 + (i - p_tiles) // s_tiles)
        return (l, row, 0, 0)

    return pl.pallas_call(
        _combine_kernel,
        out_shape=jax.ShapeDtypeStruct((t, D_MODEL), F32),
        grid=(t // tc,),
        in_specs=[
            pl.BlockSpec((tc, ROUTER_PAD), lambda i: (i, 0)),
            pl.BlockSpec((tc, D_MODEL), lambda i: (i, 0)),
            pl.BlockSpec((None, None, 6, D_MODEL), mod_idx),
            pl.BlockSpec((TOP_K, tc, D_MODEL // 2), lambda i: (0, i, 0)),
        ],
        out_specs=pl.BlockSpec((tc, D_MODEL), lambda i: (i, 0)),
        compiler_params=_cparams("arbitrary"),
        name="moe_combine",
    )(tw, x1, mods, yg)


def _slot_tokens(slot, counts, padded_start, padded_end, n_slots, t):
    n = slot.shape[0]
    j = jnp.arange(MOE_BLOCK, dtype=jnp.int32)
    pad_key = (padded_start + counts)[:, None] + j[None, :]
    pad_key = jnp.where(pad_key < padded_end[:, None], pad_key, n_slots).reshape(N_EXPERTS * MOE_BLOCK)
    pad_val = jnp.arange(N_EXPERTS * MOE_BLOCK, dtype=jnp.int32) % t
    keys = jnp.concatenate([slot, pad_key])
    vals = jnp.concatenate([jnp.arange(n, dtype=jnp.int32) % t, pad_val])
    tok_bits = max(1, (t - 1).bit_length())
    assert (n_slots + 1) << tok_bits < 2 ** 31
    packed = jnp.sort(lax.shift_left(keys, tok_bits) | vals)
    return packed[:n_slots] & ((1 << tok_bits) - 1)


def _route(expert_rank, counts_pad, n_slots):
    t = expert_rank.shape[1]
    flat_e = expert_rank[:TOP_K].reshape(TOP_K * t)
    onehot = (flat_e[:, None] == jnp.arange(N_EXPERTS, dtype=jnp.int32)[None, :]).astype(jnp.int32)
    rank = expert_rank[TOP_K:].reshape(TOP_K * t)
    counts = counts_pad[0, :N_EXPERTS]
    padded = (counts + MOE_BLOCK - 1) // MOE_BLOCK * MOE_BLOCK
    padded_end = jnp.cumsum(padded)
    padded_start = padded_end - padded
    slot = jnp.sum(onehot * padded_start[None, :], axis=1) + rank
    n_blocks = n_slots // MOE_BLOCK
    block_row0 = jnp.arange(n_blocks, dtype=jnp.int32) * MOE_BLOCK
    block_exp = jnp.minimum(
        jnp.sum((padded_end[None, :] <= block_row0[:, None]).astype(jnp.int32), axis=1),
        N_EXPERTS - 1).astype(jnp.int32)
    exp_hot = (block_exp[:, None] == jnp.arange(N_EXPERTS, dtype=jnp.int32)[None, :]).astype(jnp.int32)
    rows_end = jnp.sum(exp_hot * (padded_start + counts)[None, :], axis=1)
    block_valid = jnp.clip(rows_end - block_row0, 0, MOE_BLOCK).astype(jnp.int32)
    n_used = (padded_end[-1] // MOE_BLOCK).astype(jnp.int32)
    block_ids = jnp.arange(n_blocks, dtype=jnp.int32)
    prev_exp = jnp.concatenate([jnp.full((1,), -1, jnp.int32), block_exp[:-1]])
    first = ((block_exp != prev_exp) & (block_ids < n_used)).astype(jnp.int32)
    buf_slot = (jnp.cumsum(first) - 1) % 2
    experts = jnp.arange(N_EXPERTS, dtype=jnp.int32)
    holder = jnp.where(counts > 0, experts, N_EXPERTS)
    later = lax.cummin(jnp.concatenate([holder[1:], jnp.full((1,), N_EXPERTS, jnp.int32)]), reverse=True)
    next_exp = jnp.where(later < N_EXPERTS, later, -1)
    block_next = jnp.sum(exp_hot * next_exp[None, :], axis=1)
    plan = jnp.stack([block_exp, block_valid, first, buf_slot, block_next,
                      jnp.full((n_blocks,), n_used, jnp.int32)]).astype(jnp.int32)
    slot = slot.astype(jnp.int32)
    slot_tok = _slot_tokens(slot, counts, padded_start, padded_end, n_slots, t)
    return slot, slot_tok, plan


def _segment_matrices():
    m_q = np.zeros((QK_PAD, LANES), np.float32)
    m_k = np.zeros((QK_PAD, LANES), np.float32)
    inv_cnt_q = np.ones((1, LANES), np.float32)
    for h in range(N_HEADS):
        m_q[h * HEAD_PAD:h * HEAD_PAD + QK_NOPE, 2 * h] = 1.0
        m_q[h * HEAD_PAD + QK_NOPE:h * HEAD_PAD + QK_HEAD, 2 * h + 1] = 1.0
        inv_cnt_q[0, 2 * h] = 1.0 / QK_NOPE
        inv_cnt_q[0, 2 * h + 1] = 1.0 / ROPE_DIM
        m_k[h * HEAD_PAD:h * HEAD_PAD + QK_NOPE, h] = 1.0
    dup = lambda m: np.concatenate([m.T, m.T], axis=0)
    return dict(m_q=jnp.asarray(m_q, BF16), mt2_q=jnp.asarray(dup(m_q), BF16), inv_cnt_q=jnp.asarray(inv_cnt_q),
                m_k=jnp.asarray(m_k, BF16), mt2_k=jnp.asarray(dup(m_k), BF16))


def _rope_lane_tables(n_tokens, ident_rows):
    rows = n_tokens // GRID_W
    row = jnp.broadcast_to(jnp.arange(rows)[:, None], (rows, GRID_W)).reshape(n_tokens)
    col = jnp.broadcast_to(jnp.arange(GRID_W)[None, :], (rows, GRID_W)).reshape(n_tokens)
    n_freq = ROPE_DIM // 4
    inv = 1.0 / (ROPE_THETA ** (jnp.arange(n_freq, dtype=F32) / n_freq))
    ang = jnp.concatenate([row[:, None].astype(F32) * inv, col[:, None].astype(F32) * inv], axis=-1)
    cos, sin = jnp.cos(ang), jnp.sin(ang)
    ones = jnp.ones((n_tokens, QK_NOPE), F32)
    tail = HEAD_PAD - QK_HEAD
    cosf = jnp.concatenate([ones, cos, cos, jnp.ones((n_tokens, tail), F32)], axis=1)
    sinf = jnp.concatenate([0.0 * ones, -sin, sin, jnp.zeros((n_tokens, tail), F32)], axis=1)
    cosf = jnp.concatenate([cosf, jnp.ones((ident_rows, HEAD_PAD), F32)], axis=0)
    sinf = jnp.concatenate([sinf, jnp.zeros((ident_rows, HEAD_PAD), F32)], axis=0)
    return cosf, sinf


def _dft_tables(seq_len):
    def cs(n):
        k = np.arange(n, dtype=np.int64)
        ang = 2.0 * np.pi * ((k[:, None] * k[None, :]) % n).astype(np.float64) / n
        return np.cos(ang).astype(np.float32), np.sin(ang).astype(np.float32)

    cl, sl = cs(seq_len)
    cc, sc = cs(FNET_GC)
    return dict(cos=jnp.asarray(cl).astype(BF16), sin=jnp.asarray(sl).astype(BF16),
                chan=jnp.asarray(np.concatenate([cc, sc], axis=1)).astype(BF16))


def _layout_weights(w_in, w_q_b, w_kv_b, q_nope_g, q_rope_g, k_nope_g, k_rope_g, w_router, b_router):
    depth = w_in.shape[0]
    w_in_a = w_in[:, :, :COL_F].astype(BF16)
    w_in_b = w_in[:, :, COL_KPE + ROPE_DIM:].astype(BF16)
    w_q = jnp.pad(w_q_b.reshape(depth, Q_RANK, N_HEADS, QK_HEAD),
                  ((0, 0), (0, 0), (0, 0), (0, HEAD_PAD - QK_HEAD))).reshape(depth, Q_RANK, QK_PAD).astype(BF16)
    kv = w_kv_b.reshape(depth, KV_RANK, N_HEADS, QK_NOPE + V_HEAD)
    w_k = jnp.pad(kv[..., :QK_NOPE], ((0, 0), (0, 0), (0, 0), (0, HEAD_PAD - QK_NOPE))
                  ).reshape(depth, KV_RANK, QK_PAD).astype(BF16)
    w_v = kv[..., QK_NOPE:].reshape(depth, KV_RANK, V_WIDTH).astype(BF16)
    zq = jnp.zeros((depth, HEAD_PAD - QK_HEAD), F32)
    g_q = jnp.tile(jnp.concatenate([q_nope_g, q_rope_g, zq], axis=1), (1, N_HEADS))[:, None, :]
    g_k = jnp.tile(jnp.concatenate([k_nope_g, jnp.zeros((depth, HEAD_PAD - QK_NOPE), F32)], axis=1),
                   (1, N_HEADS))[:, None, :]
    g_kpe = jnp.concatenate([jnp.zeros((depth, QK_NOPE), F32), k_rope_g, zq], axis=1)[:, None, :]
    w_hi = w_router.astype(BF16)
    w_lo = (w_router - w_hi.astype(F32)).astype(BF16)
    w_r3 = jnp.pad(jnp.concatenate([w_hi, w_hi, w_lo], axis=1), ((0, 0), (0, 0), (0, ROUTER_PAD - N_EXPERTS)))
    b_r = jnp.pad(b_router, ((0, 0), (0, ROUTER_PAD - N_EXPERTS)), constant_values=NEG_BIG)[:, None, :]
    return dict(w_in_a=w_in_a, w_in_b=w_in_b, w_q=w_q, w_k=w_k, w_v=w_v, g_q=g_q, g_k=g_k, g_kpe=g_kpe, w_router3=w_r3, b_router=b_r)


def kernel(x_prompt, x_sample, cache_ckv, cache_kpe, c, c_ctx, w_mod, b_mod, norm1_g, norm2_g, w_in, q_a_g, kv_a_g, w_q_b, w_kv_b, q_nope_g, q_rope_g, k_nope_g, k_rope_g, w_attn_o, w_fnet_o, w_pool_grp, pool_scale, w_pool_o, w_out, w_router, b_router, w_gu, b_gu, w_down, b_down):
    batch, seq, d = x_prompt.shape
    dec_batch, dec_seq, _ = x_sample.shape
    depth = w_mod.shape[0]
    past = cache_ckv.shape[2]
    t_prompt = batch * seq
    t_sample = dec_batch * dec_seq
    assert d == D_MODEL and dec_batch + 1 <= 8
    assert t_prompt % TOKEN_TILE == 0 and dec_seq % TOKEN_TILE == 0 and seq % COMBINE_TILE == 0
    assert dec_seq % GRID_W == 0

    consts = _segment_matrices()
    consts["cosf"], consts["sinf"] = _rope_lane_tables(dec_seq, IN_TILE)
    consts["tri"] = jnp.asarray(np.tril(np.ones((TOKEN_TILE, TOKEN_TILE), np.float32), -1), BF16)

    wts = _layout_weights(w_in, w_q_b, w_kv_b, q_nope_g, q_rope_g, k_nope_g, k_rope_g, w_router, b_router)
    row3 = lambda a: a[:, None, :]
    wts.update(norm1_g=row3(norm1_g), norm2_g=row3(norm2_g), q_a_g=row3(q_a_g), kv_a_g=row3(kv_a_g),
               w_attn_o=w_attn_o.astype(BF16), w_fnet_o=w_fnet_o.astype(BF16), w_pool_o=w_pool_o.astype(BF16),
               w_out=w_out.astype(BF16), w_pool_grp=w_pool_grp.astype(BF16), pool_scale=row3(pool_scale))
    b_gu4 = b_gu[:, :, None, :]
    b_down4 = b_down[:, :, None, :]

    cond8 = jnp.concatenate([c_ctx[None, :], c, jnp.zeros((8 - 1 - dec_batch, d), F32)], axis=0)
    mods = _modulation(cond8, w_mod, b_mod).reshape(depth, 8, 6, D_MODEL)

    kpe_pad = jnp.pad(cache_kpe, ((0, 0), (0, 0), (0, 0), (QK_NOPE, HEAD_PAD - QK_HEAD)))
    k_ctx, v_ctx = _ctx_keys(cache_ckv, kpe_pad, wts, consts)

    streams = [
        dict(x=x_prompt.reshape(t_prompt, d), n_seq=batch, seq_len=seq, tok=dict(t_prompt=t_prompt, dec_seq=dec_seq),
             dft=_dft_tables(seq), ctx=None, moe=None),
        dict(x=x_sample.reshape(t_sample, d), n_seq=dec_batch, seq_len=dec_seq, tok=dict(t_prompt=0, dec_seq=dec_seq),
             dft=_dft_tables(dec_seq), ctx=(k_ctx, v_ctx, past), moe=None),
    ]
    ckv_list, kpe_list = [], []

    def mixers(st, l):
        moe = None if st["x"] is not None else (st["x1"], st["tw"], st["yg"])
        (q, k, v, ckv, kpe, f_in, p_in, st["gates"]), st["x"] = _in_projection(
            st["x"], moe, mods, l, wts, consts, st["tok"])
        if st["ctx"] is None:
            ckv_list.append(ckv.reshape(batch, seq, KV_RANK))
            kpe_list.append(kpe.reshape(batch, seq, ROPE_DIM))
        parts = [(k, v, None, st["seq_len"])]
        if st["ctx"] is not None:
            parts.append((st["ctx"][0], st["ctx"][1], l, st["ctx"][2]))
        st["attn"] = _attention(q, parts, st["n_seq"], st["seq_len"])
        st["fnet"] = _fourier(f_in, st["n_seq"], st["seq_len"], 0, st["dft"])
        st["pool"] = _pooling(p_in, st["n_seq"], st["seq_len"], 0, l, wts)

    def merge_and_route(st, l):
        t = st["n_seq"] * st["seq_len"]
        n_slots = (t * TOP_K + N_EXPERTS * (MOE_BLOCK - 1) + MOE_BLOCK - 1) // MOE_BLOCK * MOE_BLOCK
        st["x1"], st["h2p"], st["tw"], expert_rank, counts_pad = _merge(
            st["x"], mods, st["attn"], st["fnet"], st["pool"], st["gates"], l, wts, consts, st["tok"])
        st["slot_by_choice"], st["slot_tok"], st["plan"] = _route(expert_rank, counts_pad, n_slots)

    def dispatch(st):
        st["xs"] = _sc_row_gather(st["h2p"], st["slot_tok"])

    def experts(st, l):
        st["y"] = _experts(st["xs"], st["plan"], l, w_gu, b_gu4, w_down, b_down4)

    def gather_back(st):
        t = st["n_seq"] * st["seq_len"]
        st["yg"] = _sc_row_gather(st["y"], st["slot_by_choice"]).reshape(TOP_K, t, D_MODEL // 2)
        st["x"] = None

    def mixer_input(st):
        return "x" if st["x"] is not None else "yg"

    def tie(sa, ka, sb, kb):
        sa[ka], sb[kb] = lax.optimization_barrier((sa[ka], sb[kb]))

    a, b = streams
    mixers(a, 0)
    merge_and_route(a, 0)
    tie(b, mixer_input(b), a, "slot_tok")
    dispatch(a)
    mixers(b, 0)
    tie(a, "xs", b, "attn")
    experts(a, 0)
    for l in range(depth):
        tie(b, "x", a, "y")
        gather_back(a)
        merge_and_route(b, l)
        tie(a, "yg", b, "slot_tok")
        dispatch(b)
        if l + 1 == depth:
            break
        mixers(a, l + 1)
        tie(b, "xs", a, "attn")
        experts(b, l)
        tie(a, "x", b, "y")
        gather_back(b)
        merge_and_route(a, l + 1)
        tie(b, "yg", a, "slot_tok")
        dispatch(a)
        mixers(b, l + 1)
        tie(a, "xs", b, "attn")
        experts(a, l + 1)
    a["out"] = _combine(a["yg"], a["tw"], a["x1"], mods, depth - 1, a["tok"])
    tie(b, "xs", a, "out")
    experts(b, depth - 1)
    gather_back(b)
    out_a = a["out"]
    out_b = _combine(b["yg"], b["tw"], b["x1"], mods, depth - 1, b["tok"])
    y_prompt = out_a.reshape(batch, seq, d)
    y_sample = out_b.reshape(dec_batch, dec_seq, d)
    return (y_prompt, y_sample, jnp.stack(ckv_list, axis=1), jnp.stack(kpe_list, axis=1))
```

```python
import functools
import math

import numpy as np
import jax
import jax.numpy as jnp
from jax import lax
from jax.experimental import pallas as pl
from jax.experimental.pallas import tpu as pltpu
from jax.experimental.pallas import tpu_sc as plsc

D_MODEL = 1024
GRID_W = 64
N_HEADS = 8
QK_NOPE = 64
ROPE_DIM = 32
V_HEAD = 64
QK_HEAD = QK_NOPE + ROPE_DIM
Q_RANK = 384
KV_RANK = 256
ROPE_THETA = 10000.0
FNET_GROUPS = 4
FNET_GC = 128
FNET_WIDTH = FNET_GROUPS * FNET_GC
POOL_WINDOWS = (2, 4, 8, 16)
POOL_GC = 128
POOL_WIDTH = len(POOL_WINDOWS) * POOL_GC
N_BRANCHES = 3
N_EXPERTS = 32
TOP_K = 4
D_FF = D_MODEL
SWIGLU_LIMIT = 7.0
SWIGLU_ALPHA = 1.702
RMS_EPS = 1e-6

LANES = 128
HEAD_PAD = LANES
QK_PAD = N_HEADS * HEAD_PAD
V_WIDTH = N_HEADS * V_HEAD
ROPE_HALF = ROPE_DIM // 2
KPE_PAD = LANES
COL_KV = Q_RANK
COL_KPE = Q_RANK + KV_RANK
COL_F = COL_KPE + KPE_PAD
COL_P = COL_F + FNET_WIDTH
COL_G = COL_P + POOL_WIDTH
IN_PAD_WIDTH = COL_G + N_BRANCHES * D_MODEL
MIX_WIDTH = FNET_WIDTH + POOL_WIDTH
ROUTER_PAD = LANES
NEG_BIG = -1e30

TOKEN_TILE = 512
IN_TILE = 256
ATTN_Q_TILE = 512
SHORT_SEQ_ROWS = 1024
FNET_ROW_TILE = 512
POOL_CHUNK = 256
POOL_HALO = 16
MOE_BLOCK = 512
MOE_STEP = 64
GATE_COL_CHUNK = 512
MERGE_COL_CHUNK = 256
COMBINE_TILE = 256
SC_GATHER_ROWS = 64
MOD_COL_TILE = 1536
VMEM_LIMIT = 56 * 1024 * 1024

F32 = jnp.float32
BF16 = jnp.bfloat16


def _cparams(*sem):
    return pltpu.CompilerParams(dimension_semantics=sem, vmem_limit_bytes=VMEM_LIMIT)


def _dot(a, b):
    return jnp.dot(a, b, preferred_element_type=F32)


def _rms(x):
    return x * lax.rsqrt(jnp.mean(x * x, axis=-1, keepdims=True) + RMS_EPS)


def _pack_bf16_pairs(x):
    half = x.shape[1] // 2
    bits = pltpu.bitcast(x.astype(BF16).astype(F32), jnp.int32)
    return bits[:, :half] | lax.shift_right_logical(bits[:, half:], 16)


def _unpack_bf16_pairs(w):
    return pltpu.bitcast(w & jnp.int32(-65536), F32), pltpu.bitcast(lax.shift_left(w, 16), F32)


def _seg_sums(x, m_ref):
    return _dot((x * x).astype(BF16), m_ref[...])


def _seg_inv_rms(ss, inv_cnt, mt2_ref):
    r = lax.rsqrt(ss * inv_cnt + RMS_EPS)
    r_hi = r.astype(BF16)
    r_lo = (r - r_hi.astype(F32)).astype(BF16)
    return _dot(jnp.concatenate([r_hi, r_lo], axis=1), mt2_ref[...])


def _seg_rms_scale(x, m_ref, mt2_ref, inv_cnt):
    return _seg_inv_rms(_seg_sums(x, m_ref), inv_cnt, mt2_ref)


def _rope_key_chunk(kpe_chunk, gkpe_ref, rope):
    ms = jnp.sum(kpe_chunk * kpe_chunk, axis=-1, keepdims=True) * (1.0 / ROPE_DIM)
    kp = kpe_chunk * lax.rsqrt(ms + RMS_EPS) * gkpe_ref[...]
    return kp if rope is None else _rope_chunk(kp, *rope)


def _rope_chunk(xc, cosf, sinf, first_half):
    partner = jnp.where(first_half, pltpu.roll(xc, HEAD_PAD - ROPE_HALF, 1), pltpu.roll(xc, ROPE_HALF, 1))
    return xc * cosf + partner * sinf


def _write_keys(ckv, kpe_chunk, wk_ref, wv_ref, gk_ref, gkpe_ref, mk_ref, mkt2_ref, rope, k_out, v_out):
    cb = ckv.astype(BF16)
    kn = _dot(cb, wk_ref[...])
    kn = kn * _seg_rms_scale(kn, mk_ref, mkt2_ref, 1.0 / QK_NOPE) * gk_ref[...]
    v_out[...] = _dot(cb, wv_ref[...]).astype(v_out.dtype)
    kp = _rope_key_chunk(kpe_chunk, gkpe_ref, rope)
    for h in range(N_HEADS):
        sl = slice(h * HEAD_PAD, (h + 1) * HEAD_PAD)
        k_out[:, sl] = (kn[:, sl] + kp).astype(k_out.dtype)


def _mod_kernel(c_ref, w_ref, b_ref, o_ref):
    c = c_ref[...]
    s = c * jax.nn.sigmoid(c)
    o_ref[...] = _dot(s.astype(BF16), w_ref[...].astype(BF16)) + b_ref[...]


def _modulation(cond8, w_mod, b_mod):
    depth = w_mod.shape[0]
    n = w_mod.shape[2]
    return pl.pallas_call(
        _mod_kernel,
        out_shape=jax.ShapeDtypeStruct((depth, 8, n), F32),
        grid=(depth, n // MOD_COL_TILE),
        in_specs=[
            pl.BlockSpec((8, D_MODEL), lambda l, j: (0, 0)),
            pl.BlockSpec((None, D_MODEL, MOD_COL_TILE), lambda l, j: (l, 0, j)),
            pl.BlockSpec((None, 1, MOD_COL_TILE), lambda l, j: (l, 0, j)),
        ],
        out_specs=pl.BlockSpec((None, 8, MOD_COL_TILE), lambda l, j: (l, 0, j)),
        compiler_params=_cparams("arbitrary", "arbitrary"),
        name="adaln_modulation",
    )(cond8, w_mod, b_mod.reshape(depth, 1, n))


def _moe_residual(tw_ref, x1_ref, mod_ref, yg_ref):
    acc = None
    for k in range(TOP_K):
        w = tw_ref[:, k:k + 1]
        part = [w * half for half in _unpack_bf16_pairs(yg_ref[k])]
        acc = part if acc is None else [a + p for a, p in zip(acc, part)]
    return x1_ref[...] + mod_ref[5:6, :] * jnp.concatenate(acc, axis=1)


def _in_kernel(*refs, after_moe):
    if after_moe:
        x = _moe_residual(*refs[:4])
        refs = refs[4:]
        refs[-1][...] = x
        refs = refs[:-1]
    else:
        x = refs[0][...]
        refs = refs[1:]
    (mod_ref, g1_ref, wa_ref, wb_ref, qag_ref, kvag_ref, wq_ref, wk_ref, wv_ref,
     gq_ref, gk_ref, gkpe_ref, cos_ref, sin_ref, mq_ref, mqt2_ref, icq_ref, mk_ref, mkt2_ref,
     q_out, k_out, v_out, ckv_out, kpe_out, f_out, p_out, g_out) = refs
    shift = mod_ref[0:1, :]
    scale = mod_ref[1:2, :]
    h = _rms(x) * g1_ref[...] * (1.0 + scale) + shift
    hb = h.astype(BF16)

    gate_chunks = iter(range(0, N_BRANCHES * D_MODEL, GATE_COL_CHUNK))

    def gate_chunk():
        c0 = next(gate_chunks)
        zc = _dot(hb, wb_ref[:, MIX_WIDTH + c0:MIX_WIDTH + c0 + GATE_COL_CHUNK])
        g_out[:, c0:c0 + GATE_COL_CHUNK] = jax.nn.sigmoid(zc).astype(g_out.dtype)

    za = _dot(hb, wa_ref[...])
    zb = _dot(hb, wb_ref[:, 0:MIX_WIDTH])
    f_out[...] = zb[:, 0:FNET_WIDTH].astype(f_out.dtype)
    p_out[...] = zb[:, FNET_WIDTH:]
    q_lat = za[:, 0:COL_KV]
    kv_lat = za[:, COL_KV:COL_KPE]
    key_lane = lax.broadcasted_iota(jnp.int32, (hb.shape[0], KPE_PAD), 1) < ROPE_DIM
    kpe_grp = jnp.where(key_lane, za[:, COL_KPE:COL_F], 0.0)
    ckv = _rms(kv_lat) * kvag_ref[...]
    ckv_out[...] = ckv
    kpe_out[...] = kpe_grp[:, 0:ROPE_DIM]
    qn = (_rms(q_lat) * qag_ref[...]).astype(BF16)
    cb = ckv.astype(BF16)
    gate_chunk()

    q = _dot(qn, wq_ref[...])
    kn = _dot(cb, wk_ref[...])
    v_out[...] = _dot(cb, wv_ref[...]).astype(v_out.dtype)
    gate_chunk()

    ss_q = _seg_sums(q, mq_ref)
    ss_k = _seg_sums(kn, mk_ref)
    gate_chunk()

    q = q * _seg_inv_rms(ss_q, icq_ref[...], mqt2_ref) * gq_ref[...]
    kn = kn * _seg_inv_rms(ss_k, 1.0 / QK_NOPE, mkt2_ref) * gk_ref[...]
    for _ in range(N_BRANCHES * D_MODEL // GATE_COL_CHUNK - 3):
        gate_chunk()

    cosf = cos_ref[...]
    sinf = sin_ref[...]
    lane = lax.broadcasted_iota(jnp.int32, cosf.shape, 1)
    rope = (cosf, sinf, lane < QK_NOPE + ROPE_HALF)
    sm_scale = 1.0 / math.sqrt(QK_HEAD)
    kp = _rope_key_chunk(pltpu.roll(kpe_grp, QK_NOPE, 1), gkpe_ref, rope)
    for hd in range(N_HEADS):
        sl = slice(hd * HEAD_PAD, (hd + 1) * HEAD_PAD)
        q_out[:, sl] = (_rope_chunk(q[:, sl], *rope) * sm_scale).astype(q_out.dtype)
        k_out[:, sl] = (kn[:, sl] + kp).astype(k_out.dtype)


def _const_spec(shape):
    nd = len(shape)
    return pl.BlockSpec(shape, lambda i, _n=nd: (0,) * _n, pipeline_mode=pl.Buffered(1))


def _layer_spec(shape, l):
    nd = len(shape)
    return pl.BlockSpec((None,) + tuple(shape), lambda i, _l=l, _n=nd: (_l,) + (0,) * _n,
                        pipeline_mode=pl.Buffered(1))


def _in_projection(x, moe, mods, l, wts, consts, tok):
    after_moe = x is None
    t = moe[0].shape[0] if after_moe else x.shape[0]
    tm = IN_TILE
    n_tiles = t // tm
    p_tiles = tok["t_prompt"] // tm
    s_tiles = tok["dec_seq"] // tm
    rope_blocks = tok["dec_seq"] // tm

    def mod_row(i):
        return jnp.where(i < p_tiles, 0, 1 + (i - p_tiles) // s_tiles)

    def rope_idx(i):
        return (jnp.where(i < p_tiles, rope_blocks, (i - p_tiles) % s_tiles), 0)

    row = lambda w: pl.BlockSpec((tm, w), lambda i: (i, 0))
    if after_moe:
        x1, tw, yg = moe
        lead_specs = [row(ROUTER_PAD), row(D_MODEL),
                      pl.BlockSpec((None, None, 6, D_MODEL), lambda i: (l - 1, mod_row(i), 0, 0)),
                      pl.BlockSpec((TOP_K, tm, D_MODEL // 2), lambda i: (0, i, 0))]
        lead_args = [tw, x1, mods, yg]
    else:
        lead_specs = [row(D_MODEL)]
        lead_args = [x]
    in_specs = lead_specs + [
        pl.BlockSpec((None, None, 6, D_MODEL), lambda i: (l, mod_row(i), 0, 0)),
        _layer_spec((1, D_MODEL), l),
        _layer_spec((D_MODEL, COL_F), l),
        _layer_spec((D_MODEL, MIX_WIDTH + N_BRANCHES * D_MODEL), l),
        _layer_spec((1, Q_RANK), l),
        _layer_spec((1, KV_RANK), l),
        _layer_spec((Q_RANK, QK_PAD), l),
        _layer_spec((KV_RANK, QK_PAD), l),
        _layer_spec((KV_RANK, V_WIDTH), l),
        _layer_spec((1, QK_PAD), l),
        _layer_spec((1, QK_PAD), l),
        _layer_spec((1, HEAD_PAD), l),
        pl.BlockSpec((tm, HEAD_PAD), rope_idx),
        pl.BlockSpec((tm, HEAD_PAD), rope_idx),
        _const_spec((QK_PAD, LANES)),
        _const_spec((2 * LANES, QK_PAD)),
        _const_spec((1, LANES)),
        _const_spec((QK_PAD, LANES)),
        _const_spec((2 * LANES, QK_PAD)),
    ]
    out_shape = [
        jax.ShapeDtypeStruct((t, QK_PAD), BF16),
        jax.ShapeDtypeStruct((t, QK_PAD), BF16),
        jax.ShapeDtypeStruct((t, V_WIDTH), BF16),
        jax.ShapeDtypeStruct((t, KV_RANK), F32),
        jax.ShapeDtypeStruct((t, ROPE_DIM), F32),
        jax.ShapeDtypeStruct((t, FNET_WIDTH), BF16),
        jax.ShapeDtypeStruct((t, POOL_WIDTH), F32),
        jax.ShapeDtypeStruct((t, N_BRANCHES * D_MODEL), BF16),
    ]
    if after_moe:
        out_shape.append(jax.ShapeDtypeStruct((t, D_MODEL), F32))
    out_specs = [row(s.shape[1]) for s in out_shape]
    outs = pl.pallas_call(
        functools.partial(_in_kernel, after_moe=after_moe),
        out_shape=out_shape,
        grid=(n_tiles,),
        in_specs=in_specs,
        out_specs=out_specs,
        compiler_params=_cparams("arbitrary"),
        cost_estimate=pl.CostEstimate(
            flops=2 * t * (D_MODEL * IN_PAD_WIDTH + Q_RANK * QK_PAD + KV_RANK * (QK_PAD + V_WIDTH)
                           + 2 * (QK_PAD * LANES + 2 * LANES * QK_PAD)),
            transcendentals=t * N_BRANCHES * D_MODEL,
            bytes_accessed=2 * D_MODEL * IN_PAD_WIDTH + sum(
                math.prod(s.shape) * s.dtype.itemsize for s in out_shape)
            + 4 * t * D_MODEL * (1 + (TOP_K // 2 if after_moe else 0))),
        name="in_projection",
    )(*lead_args, mods, wts["norm1_g"], wts["w_in_a"], wts["w_in_b"], wts["q_a_g"], wts["kv_a_g"], wts["w_q"], wts["w_k"], wts["w_v"],
      wts["g_q"], wts["g_k"], wts["g_kpe"], consts["cosf"], consts["sinf"],
      consts["m_q"], consts["mt2_q"], consts["inv_cnt_q"], consts["m_k"], consts["mt2_k"])
    return (outs[:8], outs[8]) if after_moe else (outs, x)


def _ctx_keys_kernel(ckv_ref, kpe_ref, wk_ref, wv_ref, gk_ref, gkpe_ref, mk_ref, mkt2_ref, k_out, v_out):
    _write_keys(ckv_ref[...], kpe_ref[...], wk_ref, wv_ref, gk_ref, gkpe_ref, mk_ref, mkt2_ref, None, k_out, v_out)


def _ctx_keys(cache_ckv, cache_kpe_pad, wts, consts):
    nb, depth, past, _ = cache_ckv.shape
    lw = lambda shape: pl.BlockSpec((None,) + shape, lambda l, b: (l,) + (0,) * len(shape))
    cs = lambda shape: pl.BlockSpec(shape, lambda l, b: (0,) * len(shape))
    return pl.pallas_call(
        _ctx_keys_kernel,
        out_shape=[jax.ShapeDtypeStruct((depth, nb * past, QK_PAD), BF16),
                   jax.ShapeDtypeStruct((depth, nb * past, V_WIDTH), BF16)],
        grid=(depth, nb),
        in_specs=[
            pl.BlockSpec((None, None, past, KV_RANK), lambda l, b: (b, l, 0, 0)),
            pl.BlockSpec((None, None, past, HEAD_PAD), lambda l, b: (b, l, 0, 0)),
            lw((KV_RANK, QK_PAD)), lw((KV_RANK, V_WIDTH)), lw((1, QK_PAD)), lw((1, HEAD_PAD)),
            cs((QK_PAD, LANES)), cs((2 * LANES, QK_PAD)),
        ],
        out_specs=[pl.BlockSpec((None, past, QK_PAD), lambda l, b: (l, b, 0)),
                   pl.BlockSpec((None, past, V_WIDTH), lambda l, b: (l, b, 0))],
        compiler_params=_cparams("arbitrary", "arbitrary"),
        name="context_keys",
    )(cache_ckv, cache_kpe_pad, wts["w_k"], wts["w_v"], wts["g_k"], wts["g_kpe"], consts["m_k"], consts["mt2_k"])


def _attn_kernel(*refs, n_parts, seqs):
    q_ref = refs[0]
    k_refs = refs[1:1 + 2 * n_parts:2]
    v_refs = refs[2:2 + 2 * n_parts:2]
    o_ref = refs[1 + 2 * n_parts]
    q_rows = q_ref.shape[0] // seqs
    lane = lax.broadcasted_iota(jnp.int32, (q_rows, 2 * V_HEAD), 1)
    for s in range(seqs):
        rows = slice(s * q_rows, (s + 1) * q_rows)
        keys = [slice(s * (k.shape[0] // seqs), (s + 1) * (k.shape[0] // seqs)) for k in k_refs]
        for pair in range(N_HEADS // 2):
            vsl = slice(pair * 2 * V_HEAD, (pair + 1) * 2 * V_HEAD)
            outs = []
            for hd in (2 * pair, 2 * pair + 1):
                sl = slice(hd * HEAD_PAD, (hd + 1) * HEAD_PAD)
                qh = q_ref[rows, sl]
                ss = [lax.dot_general(qh, k[ks, sl], (((1,), (1,)), ((), ())), preferred_element_type=F32)
                      for k, ks in zip(k_refs, keys)]
                m = functools.reduce(jnp.maximum, [jnp.max(sc, axis=-1, keepdims=True) for sc in ss])
                es = [jnp.exp(sc - m) for sc in ss]
                den = functools.reduce(jnp.add, [jnp.sum(e, axis=-1, keepdims=True) for e in es])
                acc = functools.reduce(jnp.add, [_dot(e.astype(BF16), v[ks, vsl])
                                                 for e, v, ks in zip(es, v_refs, keys)])
                outs.append(acc / den)
            o_ref[rows, vsl] = jnp.where(lane < V_HEAD, outs[0], outs[1]).astype(o_ref.dtype)


def _attention(q, kv_parts, n_seq, seq_len):
    tq = min(ATTN_Q_TILE, seq_len)
    nq = seq_len // tq
    n_keys = sum(rows for _, _, _, rows in kv_parts)
    seqs = math.gcd(n_seq, max(1, SHORT_SEQ_ROWS // seq_len)) if nq == 1 and len(kv_parts) == 1 else 1
    in_specs = [pl.BlockSpec((seqs * tq, QK_PAD), lambda b, i: (b * nq + i, 0))]
    args = [q]
    for k, v, layer, rows in kv_parts:
        for arr, width in ((k, QK_PAD), (v, V_WIDTH)):
            if layer is None:
                in_specs.append(pl.BlockSpec((seqs * rows, width), lambda b, i: (b, 0)))
            else:
                in_specs.append(pl.BlockSpec((None, rows, width), lambda b, i, _l=layer: (_l, b, 0)))
            args.append(arr)
    return pl.pallas_call(
        functools.partial(_attn_kernel, n_parts=len(kv_parts), seqs=seqs),
        out_shape=jax.ShapeDtypeStruct((n_seq * seq_len, V_WIDTH), BF16),
        grid=(n_seq // seqs, nq),
        in_specs=in_specs,
        out_specs=pl.BlockSpec((seqs * tq, V_WIDTH), lambda b, i: (b * nq + i, 0)),
        compiler_params=_cparams("arbitrary", "arbitrary"),
        cost_estimate=pl.CostEstimate(
            flops=2 * n_seq * seq_len * n_keys * N_HEADS * (HEAD_PAD + 2 * V_HEAD),
            transcendentals=n_seq * seq_len * n_keys * N_HEADS,
            bytes_accessed=2 * n_seq * (seq_len * (QK_PAD + V_WIDTH) + n_keys * (QK_PAD + V_WIDTH))),
        name="attention",
    )(*args)


def _fnet_kernel(f_ref, cs_ref, cl_ref, sl_ref, o_ref, top_ref, bot_ref, *, norm, seqs):
    @pl.when(pl.program_id(1) == 0)
    def _():
        for g in range(FNET_GROUPS):
            sl = slice(g * FNET_GC, (g + 1) * FNET_GC)
            a = _dot(f_ref[:, sl], cs_ref[...])
            top_ref[:, sl] = a[:, :FNET_GC].astype(BF16)
            bot_ref[:, sl] = a[:, FNET_GC:].astype(BF16)

    seq_len = f_ref.shape[0] // seqs
    rows = o_ref.shape[0] // seqs
    for s in range(seqs):
        src = slice(s * seq_len, (s + 1) * seq_len)
        y = _dot(cl_ref[...], top_ref[src, :]) - _dot(sl_ref[...], bot_ref[src, :])
        o_ref[s * rows:(s + 1) * rows, :] = (y * norm).astype(o_ref.dtype)


def _fourier(f_in, n_seq, seq_len, row0, tabs):
    tr = min(FNET_ROW_TILE, seq_len)
    nj = seq_len // tr
    seqs = math.gcd(n_seq, max(1, SHORT_SEQ_ROWS // seq_len)) if nj == 1 else 1
    sb0 = row0 // (seqs * seq_len)
    return pl.pallas_call(
        functools.partial(_fnet_kernel, norm=1.0 / math.sqrt(seq_len * FNET_GC), seqs=seqs),
        out_shape=jax.ShapeDtypeStruct((n_seq * seq_len, FNET_WIDTH), BF16),
        grid=(n_seq // seqs, nj),
        in_specs=[
            pl.BlockSpec((seqs * seq_len, FNET_WIDTH), lambda b, j: (sb0 + b, 0)),
            pl.BlockSpec((FNET_GC, 2 * FNET_GC), lambda b, j: (0, 0)),
            pl.BlockSpec((tr, seq_len), lambda b, j: (j, 0)),
            pl.BlockSpec((tr, seq_len), lambda b, j: (j, 0)),
        ],
        out_specs=pl.BlockSpec((seqs * tr, FNET_WIDTH), lambda b, j: (b * nj + j, 0)),
        scratch_shapes=[pltpu.VMEM((seqs * seq_len, FNET_WIDTH), BF16),
                        pltpu.VMEM((seqs * seq_len, FNET_WIDTH), BF16)],
        compiler_params=_cparams("arbitrary", "arbitrary"),
        name="fourier_mix",
    )(f_in, tabs["chan"], tabs["cos"], tabs["sin"])


def _pool_kernel(p_ref, wg_ref, ps_ref, o_ref, pad_ref, *, seqs):
    seq_len = p_ref.shape[0] // seqs
    zeros = jnp.zeros((POOL_HALO, POOL_WIDTH), F32)
    ch = min(POOL_CHUNK, seq_len)
    for s in range(seqs):
        s0 = s * seq_len
        pad_ref[s, 0:POOL_HALO, :] = zeros
        pad_ref[s, POOL_HALO + seq_len:, :] = zeros
        pad_ref[s, POOL_HALO:POOL_HALO + seq_len, :] = p_ref[s0:s0 + seq_len, :]
        for c in range(seq_len // ch):
            t = lax.broadcasted_iota(jnp.int32, (ch, 1), 0) + c * ch
            for g, w in enumerate(POOL_WINDOWS):
                half = w // 2
                sl = slice(g * POOL_GC, (g + 1) * POOL_GC)
                acc = None
                for j in range(-half, half):
                    r0 = POOL_HALO + c * ch + j
                    part = pad_ref[s, r0:r0 + ch, sl]
                    acc = part if acc is None else acc + part
                cnt = (jnp.minimum(t + half, seq_len) - jnp.maximum(t - half, 0)).astype(F32)
                pooled = acc / cnt - p_ref[s0 + c * ch:s0 + (c + 1) * ch, sl]
                mixed = _dot(pooled.astype(BF16), wg_ref[g]) * ps_ref[:, sl]
                o_ref[s0 + c * ch:s0 + (c + 1) * ch, sl] = mixed.astype(o_ref.dtype)


def _pooling(p_in, n_seq, seq_len, row0, l, wts):
    seqs = math.gcd(n_seq, max(1, SHORT_SEQ_ROWS // seq_len))
    sb0 = row0 // (seqs * seq_len)
    g = len(POOL_WINDOWS)
    return pl.pallas_call(
        functools.partial(_pool_kernel, seqs=seqs),
        out_shape=jax.ShapeDtypeStruct((n_seq * seq_len, POOL_WIDTH), BF16),
        grid=(n_seq // seqs,),
        in_specs=[
            pl.BlockSpec((seqs * seq_len, POOL_WIDTH), lambda b: (sb0 + b, 0)),
            pl.BlockSpec((None, g, POOL_GC, POOL_GC), lambda b: (l, 0, 0, 0)),
            pl.BlockSpec((None, 1, POOL_WIDTH), lambda b: (l, 0, 0)),
        ],
        out_specs=pl.BlockSpec((seqs * seq_len, POOL_WIDTH), lambda b: (b, 0)),
        scratch_shapes=[pltpu.VMEM((seqs, seq_len + 2 * POOL_HALO, POOL_WIDTH), F32)],
        compiler_params=_cparams("arbitrary"),
        name="pool_mix",
    )(p_in, wts["w_pool_grp"], wts["pool_scale"])


def _merge_kernel(x_ref, mod_ref, a_ref, f_ref, p_ref, g_ref,
                  wa_ref, wf_ref, wp_ref, wo_ref, g2_ref, wr_ref, br_ref, tri_ref,
                  x1_out, hp_out, tw_out, er_out, cnt_out, mix_scr, carry_ref):
    j = pl.program_id(0)

    @pl.when(j == 0)
    def _():
        mix_scr[1] = jnp.zeros(mix_scr.shape[1:], F32)

    @pl.when(j <= 1)
    def _():
        carry_ref[...] = jnp.zeros(carry_ref.shape, F32)

    def branch_chunks(lo, hi):
        out = []
        for c0 in range(lo, hi, MERGE_COL_CHUNK):
            cs = slice(c0, c0 + MERGE_COL_CHUNK)
            a = _dot(a_ref[...], wa_ref[:, cs])
            f = _dot(f_ref[...], wf_ref[:, cs])
            p = _dot(p_ref[...], wp_ref[:, cs])
            out.append((g_ref[:, c0:c0 + MERGE_COL_CHUNK].astype(F32) * a
                        + g_ref[:, D_MODEL + c0:D_MODEL + c0 + MERGE_COL_CHUNK].astype(F32) * f
                        + g_ref[:, 2 * D_MODEL + c0:2 * D_MODEL + c0 + MERGE_COL_CHUNK].astype(F32) * p
                        ).astype(BF16))
        return out

    def step(cur, prev):
        chunks = branch_chunks(0, D_MODEL // 2)

        gate1 = mod_ref[2:3, :]
        shift2 = mod_ref[3:4, :]
        scale2 = mod_ref[4:5, :]
        x1 = x_ref[...] + gate1 * mix_scr[prev]
        x1_out[...] = x1
        h2 = _rms(x1) * g2_ref[...] * (1.0 + scale2) + shift2
        hp_out[...] = _pack_bf16_pairs(h2)
        h_hi = h2.astype(BF16)
        h_lo = (h2 - h_hi.astype(F32)).astype(BF16)
        logits = _dot(jnp.concatenate([h_hi, h_lo, h_hi], axis=1), wr_ref[...]) + br_ref[...]

        chunks += branch_chunks(D_MODEL // 2, D_MODEL)
        mix_scr[cur] = _dot(jnp.concatenate(chunks, axis=1), wo_ref[...])

        lane = lax.broadcasted_iota(jnp.int32, logits.shape, 1)
        work = logits
        vals, idxs = [], []
        for _ in range(TOP_K):
            m = jnp.max(work, axis=-1, keepdims=True)
            idx = jnp.min(jnp.where(work == m, lane, ROUTER_PAD), axis=-1, keepdims=True)
            vals.append(m)
            idxs.append(idx)
            work = jnp.where(lane == idx, -jnp.inf, work)
        es = [jnp.exp(v - vals[0]) for v in vals]
        den = functools.reduce(jnp.add, es)

        chosen = functools.reduce(jnp.logical_or, [lane == idx for idx in idxs])
        hot = jnp.where(chosen, 1.0, 0.0)
        before = _dot(tri_ref[...], hot.astype(BF16)) + carry_ref[...]
        carry_ref[...] = carry_ref[...] + jnp.sum(hot, axis=0, keepdims=True)
        cnt_out[...] = carry_ref[...].astype(jnp.int32)

        er = jnp.zeros(logits.shape, jnp.int32)
        tw = jnp.zeros(logits.shape, F32)
        for k in range(TOP_K):
            rank_k = jnp.sum(jnp.where(lane == idxs[k], before, 0.0), axis=-1, keepdims=True).astype(jnp.int32)
            er = jnp.where(lane == k, idxs[k], er)
            er = jnp.where(lane == TOP_K + k, rank_k, er)
            tw = jnp.where(lane == k, es[k] / den, tw)
        tw_out[...] = tw
        er_out[...] = jnp.transpose(er)[0:2 * TOP_K, :]

    for parity in (0, 1):
        pl.when(j % 2 == parity)(functools.partial(step, parity, 1 - parity))


def _merge(x, mods, attn, fnet, pool, gates, l, wts, consts, tok):
    t = x.shape[0]
    tm = TOKEN_TILE
    p_tiles = tok["t_prompt"] // tm
    s_tiles = tok["dec_seq"] // tm

    n_tiles = t // tm
    routed = lambda j: jnp.maximum(j - 1, 0)
    mixed = lambda j: jnp.minimum(j, n_tiles - 1)

    def mod_idx(j):
        i = routed(j)
        row = jnp.where(i < p_tiles, 0, 1 + (i - p_tiles) // s_tiles)
        return (l, row, 0, 0)

    row = lambda w: pl.BlockSpec((tm, w), lambda j: (routed(j), 0))
    head = lambda w: pl.BlockSpec((tm, w), lambda j: (mixed(j), 0))
    out_shape = [
        jax.ShapeDtypeStruct((t, D_MODEL), F32),
        jax.ShapeDtypeStruct((t, D_MODEL // 2), jnp.int32),
        jax.ShapeDtypeStruct((t, ROUTER_PAD), F32),
    ]
    return pl.pallas_call(
        _merge_kernel,
        out_shape=out_shape + [jax.ShapeDtypeStruct((2 * TOP_K, t), jnp.int32),
                               jax.ShapeDtypeStruct((1, ROUTER_PAD), jnp.int32)],
        grid=(n_tiles + 1,),
        in_specs=[
            row(D_MODEL),
            pl.BlockSpec((None, None, 6, D_MODEL), mod_idx),
            head(V_WIDTH), head(FNET_WIDTH), head(POOL_WIDTH), head(N_BRANCHES * D_MODEL),
            _layer_spec((V_WIDTH, D_MODEL), l),
            _layer_spec((FNET_WIDTH, D_MODEL), l),
            _layer_spec((POOL_WIDTH, D_MODEL), l),
            _layer_spec((D_MODEL, D_MODEL), l),
            _layer_spec((1, D_MODEL), l),
            _layer_spec((3 * D_MODEL, ROUTER_PAD), l),
            _layer_spec((1, ROUTER_PAD), l),
            _const_spec((tm, tm)),
        ],
        out_specs=[row(s.shape[1]) for s in out_shape] + [
            pl.BlockSpec((2 * TOP_K, tm), lambda j: (0, routed(j))),
            pl.BlockSpec((1, ROUTER_PAD), lambda j: (0, 0))],
        scratch_shapes=[pltpu.VMEM((2, tm, D_MODEL), F32), pltpu.VMEM((1, ROUTER_PAD), F32)],
        compiler_params=_cparams("arbitrary"),
        cost_estimate=pl.CostEstimate(
            flops=2 * t * (D_MODEL * (V_WIDTH + FNET_WIDTH + POOL_WIDTH + D_MODEL) + 3 * D_MODEL * ROUTER_PAD
                           + tm * ROUTER_PAD),
            transcendentals=t * TOP_K,
            bytes_accessed=t * (2 * (V_WIDTH + FNET_WIDTH + POOL_WIDTH + N_BRANCHES * D_MODEL) + 4 * 2 * D_MODEL
                                + 4 * (D_MODEL // 2) + 3 * 4 * ROUTER_PAD)
            + 2 * D_MODEL * (V_WIDTH + FNET_WIDTH + POOL_WIDTH + D_MODEL + 3 * ROUTER_PAD)),
        name="merge_router",
    )(x, mods, attn, fnet, pool, gates, wts["w_attn_o"], wts["w_fnet_o"], wts["w_pool_o"], wts["w_out"],
      wts["norm2_g"], wts["w_router3"], wts["b_router"], consts["tri"])


def _sc_row_gather(table, idx):
    n = idx.shape[0]
    d = table.shape[1]
    info = plsc.get_sparse_core_info()
    n_cores = info.num_cores
    n_workers = n_cores * info.num_subcores
    per_worker = n // n_workers
    n_chunks = per_worker // SC_GATHER_ROWS
    assert n_chunks * SC_GATHER_ROWS * n_workers == n
    mesh = plsc.VectorSubcoreMesh(core_axis_name="core", subcore_axis_name="subcore")

    @functools.partial(
        pl.kernel, mesh=mesh, out_type=jax.ShapeDtypeStruct((n, d), table.dtype),
        scratch_types=[pltpu.VMEM((SC_GATHER_ROWS,), jnp.int32), pltpu.VMEM((SC_GATHER_ROWS, d), table.dtype),
                       pltpu.SemaphoreType.DMA],
        cost_estimate=pl.CostEstimate(flops=0, transcendentals=0,
                                      bytes_accessed=2 * n * d * table.dtype.itemsize + 4 * n),
        name="sc_row_gather")
    def gather(table_hbm, idx_hbm, out_hbm, idx_v, rows_v, sem):
        worker = lax.axis_index("subcore") * n_cores + lax.axis_index("core")

        @pl.loop(0, n_chunks)
        def _(c):
            base = worker * per_worker + c * SC_GATHER_ROWS
            pltpu.sync_copy(idx_hbm.at[pl.ds(base, SC_GATHER_ROWS)], idx_v)
            pltpu.async_copy(table_hbm.at[idx_v], rows_v, sem).wait()
            pltpu.sync_copy(rows_v, out_hbm.at[pl.ds(base, SC_GATHER_ROWS)])

    return gather(table, idx)


PLAN_EXPERT, PLAN_VALID, PLAN_FIRST, PLAN_SLOT, PLAN_NEXT, PLAN_USED = range(6)


def _expert_kernel(plan_ref, xs_ref, bgu_ref, bd_ref, wgu_hbm, wd_hbm, y_ref,
                   wgu_f32, wd_f32, wgu_bf, wd_bf, sems, *, layer):
    b = pl.program_id(0)

    def weight_copies(expert, slot):
        return (pltpu.make_async_copy(wgu_hbm.at[layer, expert], wgu_f32.at[slot], sems.at[0, slot]),
                pltpu.make_async_copy(wd_hbm.at[layer, expert], wd_f32.at[slot], sems.at[1, slot]))

    def ffn(rows):
        x = jnp.concatenate(_unpack_bf16_pairs(xs_ref[0:rows, :]), axis=1).astype(BF16)
        gu = _dot(x, wgu_bf[...]) + bgu_ref[...]
        glu = jnp.minimum(gu[:, :D_FF], SWIGLU_LIMIT)
        lin = jnp.clip(gu[:, D_FF:], -SWIGLU_LIMIT, SWIGLU_LIMIT)
        act = glu * jax.nn.sigmoid(SWIGLU_ALPHA * glu) * (lin + 1.0)
        y_ref[0:rows, :] = _pack_bf16_pairs(_dot(act.astype(BF16), wd_bf[...]) + bd_ref[...])
        if rows < MOE_BLOCK:
            y_ref[rows:, :] = jnp.zeros((MOE_BLOCK - rows, D_MODEL // 2), y_ref.dtype)

    n_used = plan_ref[PLAN_USED, 0]

    @pl.when(b < n_used)
    def _():
        slot = plan_ref[PLAN_SLOT, b]

        @pl.when(b == 0)
        def _():
            for cp in weight_copies(plan_ref[PLAN_EXPERT, 0], 0):
                cp.start()

        @pl.when(plan_ref[PLAN_FIRST, b] == 1)
        def _():
            for cp in weight_copies(plan_ref[PLAN_EXPERT, b], slot):
                cp.wait()
            wgu_bf[...] = wgu_f32[slot].astype(BF16)
            wd_bf[...] = wd_f32[slot].astype(BF16)

            @pl.when(plan_ref[PLAN_NEXT, b] >= 0)
            def _():
                for cp in weight_copies(plan_ref[PLAN_NEXT, b], 1 - slot):
                    cp.start()

        steps = (plan_ref[PLAN_VALID, b] + (MOE_STEP - 1)) // MOE_STEP
        for rows in range(MOE_STEP, MOE_BLOCK + 1, MOE_STEP):
            want = steps <= 1 if rows == MOE_STEP else steps == rows // MOE_STEP
            pl.when(want)(functools.partial(ffn, rows))

    @pl.when(b >= n_used)
    def _():
        y_ref[...] = jnp.zeros(y_ref.shape, y_ref.dtype)


def _experts(xs, plan, l, w_gu, b_gu4, w_down, b_down4):
    n_slots = xs.shape[0]
    bm = MOE_BLOCK
    n_blocks = n_slots // bm

    def blk(b, plan):
        return jnp.minimum(b, plan[PLAN_USED, 0] - 1)

    def bspec(width):
        return pl.BlockSpec((None, None, 1, width), lambda b, plan: (l, plan[PLAN_EXPERT, blk(b, plan)], 0, 0))

    grid_spec = pltpu.PrefetchScalarGridSpec(
        num_scalar_prefetch=1,
        grid=(n_blocks,),
        in_specs=[
            pl.BlockSpec((bm, D_MODEL // 2), lambda b, plan: (blk(b, plan), 0)),
            bspec(2 * D_FF), bspec(D_MODEL),
            pl.BlockSpec(memory_space=pl.ANY), pl.BlockSpec(memory_space=pl.ANY),
        ],
        out_specs=pl.BlockSpec((bm, D_MODEL // 2), lambda b, plan: (b, 0)),
        scratch_shapes=[pltpu.VMEM((2, D_MODEL, 2 * D_FF), F32), pltpu.VMEM((2, D_FF, D_MODEL), F32),
                        pltpu.VMEM((D_MODEL, 2 * D_FF), BF16), pltpu.VMEM((D_FF, D_MODEL), BF16),
                        pltpu.SemaphoreType.DMA((2, 2))],
    )
    return pl.pallas_call(
        functools.partial(_expert_kernel, layer=l),
        out_shape=jax.ShapeDtypeStruct((n_slots, D_MODEL // 2), jnp.int32),
        grid_spec=grid_spec,
        compiler_params=_cparams("arbitrary"),
        cost_estimate=pl.CostEstimate(
            flops=2 * n_slots * 3 * D_MODEL * D_FF, transcendentals=n_slots * D_FF,
            bytes_accessed=4 * N_EXPERTS * 3 * D_MODEL * D_FF + 2 * 4 * n_slots * (D_MODEL // 2)),
        name="moe_experts",
    )(plan, xs, b_gu4, b_down4, w_gu, w_down)


def _combine_kernel(tw_ref, x_ref, mod_ref, yg_ref, o_ref):
    o_ref[...] = _moe_residual(tw_ref, x_ref, mod_ref, yg_ref)


def _combine(yg, tw, x1, mods, l, tok):
    t = x1.shape[0]
    tc = COMBINE_TILE
    p_tiles = tok["t_prompt"] // tc
    s_tiles = tok["dec_seq"] // tc

    def mod_idx(i):
        row = jnp.where(i < p_tiles, 0, 1 + (i - p_tiles) // s_tiles)
        return (l, row, 0, 0)

    return pl.pallas_call(
        _combine_kernel,
        out_shape=jax.ShapeDtypeStruct((t, D_MODEL), F32),
        grid=(t // tc,),
        in_specs=[
            pl.BlockSpec((tc, ROUTER_PAD), lambda i: (i, 0)),
            pl.BlockSpec((tc, D_MODEL), lambda i: (i, 0)),
            pl.BlockSpec((None, None, 6, D_MODEL), mod_idx),
            pl.BlockSpec((TOP_K, tc, D_MODEL // 2), lambda i: (0, i, 0)),
        ],
        out_specs=pl.BlockSpec((tc, D_MODEL), lambda i: (i, 0)),
        compiler_params=_cparams("arbitrary"),
        name="moe_combine",
    )(tw, x1, mods, yg)


def _slot_tokens(slot, counts, padded_start, padded_end, n_slots, t):
    n = slot.shape[0]
    j = jnp.arange(MOE_BLOCK, dtype=jnp.int32)
    pad_key = (padded_start + counts)[:, None] + j[None, :]
    pad_key = jnp.where(pad_key < padded_end[:, None], pad_key, n_slots).reshape(N_EXPERTS * MOE_BLOCK)
    pad_val = jnp.arange(N_EXPERTS * MOE_BLOCK, dtype=jnp.int32) % t
    fill = pl.next_power_of_2(n + N_EXPERTS * MOE_BLOCK) - (n + N_EXPERTS * MOE_BLOCK)
    keys = jnp.concatenate([slot, pad_key, jnp.full((fill,), n_slots, jnp.int32)])
    vals = jnp.concatenate([jnp.arange(n, dtype=jnp.int32) % t, pad_val, jnp.zeros((fill,), jnp.int32)])
    tok_bits = max(1, (t - 1).bit_length())
    assert (n_slots + 1) << tok_bits < 2 ** 31
    packed = jnp.sort(lax.shift_left(keys, tok_bits) | vals)
    return packed[:n_slots] & ((1 << tok_bits) - 1)


def _route(expert_rank, counts_pad, n_slots):
    t = expert_rank.shape[1]
    flat_e = expert_rank[:TOP_K].reshape(TOP_K * t)
    onehot = (flat_e[:, None] == jnp.arange(N_EXPERTS, dtype=jnp.int32)[None, :]).astype(jnp.int32)
    rank = expert_rank[TOP_K:].reshape(TOP_K * t)
    counts = counts_pad[0, :N_EXPERTS]
    padded = (counts + MOE_BLOCK - 1) // MOE_BLOCK * MOE_BLOCK
    padded_end = jnp.cumsum(padded)
    padded_start = padded_end - padded
    slot = jnp.sum(onehot * padded_start[None, :], axis=1) + rank
    n_blocks = n_slots // MOE_BLOCK
    block_row0 = jnp.arange(n_blocks, dtype=jnp.int32) * MOE_BLOCK
    block_exp = jnp.minimum(
        jnp.sum((padded_end[None, :] <= block_row0[:, None]).astype(jnp.int32), axis=1),
        N_EXPERTS - 1).astype(jnp.int32)
    exp_hot = (block_exp[:, None] == jnp.arange(N_EXPERTS, dtype=jnp.int32)[None, :]).astype(jnp.int32)
    rows_end = jnp.sum(exp_hot * (padded_start + counts)[None, :], axis=1)
    block_valid = jnp.clip(rows_end - block_row0, 0, MOE_BLOCK).astype(jnp.int32)
    n_used = (padded_end[-1] // MOE_BLOCK).astype(jnp.int32)
    block_ids = jnp.arange(n_blocks, dtype=jnp.int32)
    prev_exp = jnp.concatenate([jnp.full((1,), -1, jnp.int32), block_exp[:-1]])
    first = ((block_exp != prev_exp) & (block_ids < n_used)).astype(jnp.int32)
    buf_slot = (jnp.cumsum(first) - 1) % 2
    experts = jnp.arange(N_EXPERTS, dtype=jnp.int32)
    holder = jnp.where(counts > 0, experts, N_EXPERTS)
    later = lax.cummin(jnp.concatenate([holder[1:], jnp.full((1,), N_EXPERTS, jnp.int32)]), reverse=True)
    next_exp = jnp.where(later < N_EXPERTS, later, -1)
    block_next = jnp.sum(exp_hot * next_exp[None, :], axis=1)
    plan = jnp.stack([block_exp, block_valid, first, buf_slot, block_next,
                      jnp.full((n_blocks,), n_used, jnp.int32)]).astype(jnp.int32)
    slot = slot.astype(jnp.int32)
    slot_tok = _slot_tokens(slot, counts, padded_start, padded_end, n_slots, t)
    return slot, slot_tok, plan


def _segment_matrices():
    m_q = np.zeros((QK_PAD, LANES), np.float32)
    m_k = np.zeros((QK_PAD, LANES), np.float32)
    inv_cnt_q = np.ones((1, LANES), np.float32)
    for h in range(N_HEADS):
        m_q[h * HEAD_PAD:h * HEAD_PAD + QK_NOPE, 2 * h] = 1.0
        m_q[h * HEAD_PAD + QK_NOPE:h * HEAD_PAD + QK_HEAD, 2 * h + 1] = 1.0
        inv_cnt_q[0, 2 * h] = 1.0 / QK_NOPE
        inv_cnt_q[0, 2 * h + 1] = 1.0 / ROPE_DIM
        m_k[h * HEAD_PAD:h * HEAD_PAD + QK_NOPE, h] = 1.0
    dup = lambda m: np.concatenate([m.T, m.T], axis=0)
    return dict(m_q=jnp.asarray(m_q, BF16), mt2_q=jnp.asarray(dup(m_q), BF16), inv_cnt_q=jnp.asarray(inv_cnt_q),
                m_k=jnp.asarray(m_k, BF16), mt2_k=jnp.asarray(dup(m_k), BF16))


def _rope_lane_tables(n_tokens, ident_rows):
    rows = n_tokens // GRID_W
    row = jnp.broadcast_to(jnp.arange(rows)[:, None], (rows, GRID_W)).reshape(n_tokens)
    col = jnp.broadcast_to(jnp.arange(GRID_W)[None, :], (rows, GRID_W)).reshape(n_tokens)
    n_freq = ROPE_DIM // 4
    inv = 1.0 / (ROPE_THETA ** (jnp.arange(n_freq, dtype=F32) / n_freq))
    ang = jnp.concatenate([row[:, None].astype(F32) * inv, col[:, None].astype(F32) * inv], axis=-1)
    cos, sin = jnp.cos(ang), jnp.sin(ang)
    ones = jnp.ones((n_tokens, QK_NOPE), F32)
    tail = HEAD_PAD - QK_HEAD
    cosf = jnp.concatenate([ones, cos, cos, jnp.ones((n_tokens, tail), F32)], axis=1)
    sinf = jnp.concatenate([0.0 * ones, -sin, sin, jnp.zeros((n_tokens, tail), F32)], axis=1)
    cosf = jnp.concatenate([cosf, jnp.ones((ident_rows, HEAD_PAD), F32)], axis=0)
    sinf = jnp.concatenate([sinf, jnp.zeros((ident_rows, HEAD_PAD), F32)], axis=0)
    return cosf, sinf


def _dft_tables(seq_len):
    def cs(n):
        k = np.arange(n, dtype=np.int64)
        ang = 2.0 * np.pi * ((k[:, None] * k[None, :]) % n).astype(np.float64) / n
        return np.cos(ang).astype(np.float32), np.sin(ang).astype(np.float32)

    cl, sl = cs(seq_len)
    cc, sc = cs(FNET_GC)
    return dict(cos=jnp.asarray(cl).astype(BF16), sin=jnp.asarray(sl).astype(BF16),
                chan=jnp.asarray(np.concatenate([cc, sc], axis=1)).astype(BF16))


def _layout_weights(w_in, w_q_b, w_kv_b, q_nope_g, q_rope_g, k_nope_g, k_rope_g, w_router, b_router):
    depth = w_in.shape[0]
    w_in_a = w_in[:, :, :COL_F].astype(BF16)
    w_in_b = w_in[:, :, COL_KPE + ROPE_DIM:].astype(BF16)
    w_q = jnp.pad(w_q_b.reshape(depth, Q_RANK, N_HEADS, QK_HEAD),
                  ((0, 0), (0, 0), (0, 0), (0, HEAD_PAD - QK_HEAD))).reshape(depth, Q_RANK, QK_PAD).astype(BF16)
    kv = w_kv_b.reshape(depth, KV_RANK, N_HEADS, QK_NOPE + V_HEAD)
    w_k = jnp.pad(kv[..., :QK_NOPE], ((0, 0), (0, 0), (0, 0), (0, HEAD_PAD - QK_NOPE))
                  ).reshape(depth, KV_RANK, QK_PAD).astype(BF16)
    w_v = kv[..., QK_NOPE:].reshape(depth, KV_RANK, V_WIDTH).astype(BF16)
    zq = jnp.zeros((depth, HEAD_PAD - QK_HEAD), F32)
    g_q = jnp.tile(jnp.concatenate([q_nope_g, q_rope_g, zq], axis=1), (1, N_HEADS))[:, None, :]
    g_k = jnp.tile(jnp.concatenate([k_nope_g, jnp.zeros((depth, HEAD_PAD - QK_NOPE), F32)], axis=1),
                   (1, N_HEADS))[:, None, :]
    g_kpe = jnp.concatenate([jnp.zeros((depth, QK_NOPE), F32), k_rope_g, zq], axis=1)[:, None, :]
    w_hi = w_router.astype(BF16)
    w_lo = (w_router - w_hi.astype(F32)).astype(BF16)
    w_r3 = jnp.pad(jnp.concatenate([w_hi, w_hi, w_lo], axis=1), ((0, 0), (0, 0), (0, ROUTER_PAD - N_EXPERTS)))
    b_r = jnp.pad(b_router, ((0, 0), (0, ROUTER_PAD - N_EXPERTS)), constant_values=NEG_BIG)[:, None, :]
    return dict(w_in_a=w_in_a, w_in_b=w_in_b, w_q=w_q, w_k=w_k, w_v=w_v, g_q=g_q, g_k=g_k, g_kpe=g_kpe, w_router3=w_r3, b_router=b_r)


def kernel(x_prompt, x_sample, cache_ckv, cache_kpe, c, c_ctx, w_mod, b_mod, norm1_g, norm2_g, w_in, q_a_g, kv_a_g, w_q_b, w_kv_b, q_nope_g, q_rope_g, k_nope_g, k_rope_g, w_attn_o, w_fnet_o, w_pool_grp, pool_scale, w_pool_o, w_out, w_router, b_router, w_gu, b_gu, w_down, b_down):
    batch, seq, d = x_prompt.shape
    dec_batch, dec_seq, _ = x_sample.shape
    depth = w_mod.shape[0]
    past = cache_ckv.shape[2]
    t_prompt = batch * seq
    t_sample = dec_batch * dec_seq
    assert d == D_MODEL and dec_batch + 1 <= 8
    assert t_prompt % TOKEN_TILE == 0 and dec_seq % TOKEN_TILE == 0 and seq % COMBINE_TILE == 0
    assert dec_seq % GRID_W == 0

    consts = _segment_matrices()
    consts["cosf"], consts["sinf"] = _rope_lane_tables(dec_seq, IN_TILE)
    consts["tri"] = jnp.asarray(np.tril(np.ones((TOKEN_TILE, TOKEN_TILE), np.float32), -1), BF16)

    wts = _layout_weights(w_in, w_q_b, w_kv_b, q_nope_g, q_rope_g, k_nope_g, k_rope_g, w_router, b_router)
    row3 = lambda a: a[:, None, :]
    wts.update(norm1_g=row3(norm1_g), norm2_g=row3(norm2_g), q_a_g=row3(q_a_g), kv_a_g=row3(kv_a_g),
               w_attn_o=w_attn_o.astype(BF16), w_fnet_o=w_fnet_o.astype(BF16), w_pool_o=w_pool_o.astype(BF16),
               w_out=w_out.astype(BF16), w_pool_grp=w_pool_grp.astype(BF16), pool_scale=row3(pool_scale))
    b_gu4 = b_gu[:, :, None, :]
    b_down4 = b_down[:, :, None, :]

    cond8 = jnp.concatenate([c_ctx[None, :], c, jnp.zeros((8 - 1 - dec_batch, d), F32)], axis=0)
    mods = _modulation(cond8, w_mod, b_mod).reshape(depth, 8, 6, D_MODEL)

    kpe_pad = jnp.pad(cache_kpe, ((0, 0), (0, 0), (0, 0), (QK_NOPE, HEAD_PAD - QK_HEAD)))
    k_ctx, v_ctx = _ctx_keys(cache_ckv, kpe_pad, wts, consts)

    streams = [
        dict(x=x_prompt.reshape(t_prompt, d), n_seq=batch, seq_len=seq, tok=dict(t_prompt=t_prompt, dec_seq=dec_seq),
             dft=_dft_tables(seq), ctx=None, moe=None),
        dict(x=x_sample.reshape(t_sample, d), n_seq=dec_batch, seq_len=dec_seq, tok=dict(t_prompt=0, dec_seq=dec_seq),
             dft=_dft_tables(dec_seq), ctx=(k_ctx, v_ctx, past), moe=None),
    ]
    ckv_list, kpe_list = [], []

    def mixers(st, l):
        moe = None if st["x"] is not None else (st["x1"], st["tw"], st["yg"])
        (q, k, v, ckv, kpe, f_in, p_in, st["gates"]), st["x"] = _in_projection(
            st["x"], moe, mods, l, wts, consts, st["tok"])
        if st["ctx"] is None:
            ckv_list.append(ckv.reshape(batch, seq, KV_RANK))
            kpe_list.append(kpe.reshape(batch, seq, ROPE_DIM))
        parts = [(k, v, None, st["seq_len"])]
        if st["ctx"] is not None:
            parts.append((st["ctx"][0], st["ctx"][1], l, st["ctx"][2]))
        st["attn"] = _attention(q, parts, st["n_seq"], st["seq_len"])
        st["fnet"] = _fourier(f_in, st["n_seq"], st["seq_len"], 0, st["dft"])
        st["pool"] = _pooling(p_in, st["n_seq"], st["seq_len"], 0, l, wts)

    def merge_and_route(st, l):
        t = st["n_seq"] * st["seq_len"]
        n_slots = (t * TOP_K + N_EXPERTS * (MOE_BLOCK - 1) + MOE_BLOCK - 1) // MOE_BLOCK * MOE_BLOCK
        st["x1"], st["h2p"], st["tw"], expert_rank, counts_pad = _merge(
            st["x"], mods, st["attn"], st["fnet"], st["pool"], st["gates"], l, wts, consts, st["tok"])
        st["slot_by_choice"], st["slot_tok"], st["plan"] = _route(expert_rank, counts_pad, n_slots)

    def dispatch(st):
        st["xs"] = _sc_row_gather(st["h2p"], st["slot_tok"])

    def experts(st, l):
        st["y"] = _experts(st["xs"], st["plan"], l, w_gu, b_gu4, w_down, b_down4)

    def gather_back(st):
        t = st["n_seq"] * st["seq_len"]
        st["yg"] = _sc_row_gather(st["y"], st["slot_by_choice"]).reshape(TOP_K, t, D_MODEL // 2)
        st["x"] = None

    def mixer_input(st):
        return "x" if st["x"] is not None else "yg"

    def tie(sa, ka, sb, kb):
        sa[ka], sb[kb] = lax.optimization_barrier((sa[ka], sb[kb]))

    a, b = streams
    mixers(a, 0)
    merge_and_route(a, 0)
    tie(b, mixer_input(b), a, "slot_tok")
    dispatch(a)
    mixers(b, 0)
    tie(a, "xs", b, "attn")
    experts(a, 0)
    for l in range(depth):
        tie(b, "x", a, "y")
        gather_back(a)
        merge_and_route(b, l)
        tie(a, "yg", b, "slot_tok")
        dispatch(b)
        if l + 1 == depth:
            break
        mixers(a, l + 1)
        tie(b, "xs", a, "attn")
        experts(b, l)
        tie(a, "x", b, "y")
        gather_back(b)
        merge_and_route(a, l + 1)
        tie(b, "yg", a, "slot_tok")
        dispatch(a)
        mixers(b, l + 1)
        tie(a, "xs", b, "attn")
        experts(a, l + 1)
    a["out"] = _combine(a["yg"], a["tw"], a["x1"], mods, depth - 1, a["tok"])
    tie(b, "xs", a, "out")
    experts(b, depth - 1)
    gather_back(b)
    out_a = a["out"]
    out_b = _combine(b["yg"], b["tw"], b["x1"], mods, depth - 1, b["tok"])
    y_prompt = out_a.reshape(batch, seq, d)
    y_sample = out_b.reshape(dec_batch, dec_seq, d)
    return (y_prompt, y_sample, jnp.stack(ckv_list, axis=1), jnp.stack(kpe_list, axis=1))
```

```python
import functools
import math

import numpy as np
import jax
import jax.numpy as jnp
from jax import lax
from jax.experimental import pallas as pl
from jax.experimental.pallas import tpu as pltpu
from jax.experimental.pallas import tpu_sc as plsc

D_MODEL = 1024
GRID_W = 64
N_HEADS = 8
QK_NOPE = 64
ROPE_DIM = 32
V_HEAD = 64
QK_HEAD = QK_NOPE + ROPE_DIM
Q_RANK = 384
KV_RANK = 256
ROPE_THETA = 10000.0
FNET_GROUPS = 4
FNET_GC = 128
FNET_WIDTH = FNET_GROUPS * FNET_GC
POOL_WINDOWS = (2, 4, 8, 16)
POOL_GC = 128
POOL_WIDTH = len(POOL_WINDOWS) * POOL_GC
N_BRANCHES = 3
N_EXPERTS = 32
TOP_K = 4
D_FF = D_MODEL
SWIGLU_LIMIT = 7.0
SWIGLU_ALPHA = 1.702
RMS_EPS = 1e-6

LANES = 128
HEAD_PAD = LANES
QK_PAD = N_HEADS * HEAD_PAD
V_WIDTH = N_HEADS * V_HEAD
ROPE_HALF = ROPE_DIM // 2
KPE_PAD = LANES
COL_KV = Q_RANK
COL_KPE = Q_RANK + KV_RANK
COL_F = COL_KPE + KPE_PAD
COL_P = COL_F + FNET_WIDTH
COL_G = COL_P + POOL_WIDTH
IN_PAD_WIDTH = COL_G + N_BRANCHES * D_MODEL
MIX_WIDTH = FNET_WIDTH + POOL_WIDTH
ROUTER_PAD = LANES
NEG_BIG = -1e30

TOKEN_TILE = 512
IN_TILE = 256
ATTN_Q_TILE = 512
SHORT_SEQ_ROWS = 1024
FNET_ROW_TILE = 1024
POOL_CHUNK = 256
POOL_HALO = 16
MOE_BLOCK = 512
MOE_STEP = 64
GATE_COL_CHUNK = 512
MERGE_COL_CHUNK = 256
COMBINE_TILE = 256
SC_GATHER_ROWS = 64
MOD_COL_TILE = 1536
VMEM_LIMIT = 56 * 1024 * 1024

F32 = jnp.float32
BF16 = jnp.bfloat16


def _cparams(*sem):
    return pltpu.CompilerParams(dimension_semantics=sem, vmem_limit_bytes=VMEM_LIMIT)


def _dot(a, b):
    return jnp.dot(a, b, preferred_element_type=F32)


def _rms(x):
    return x * lax.rsqrt(jnp.mean(x * x, axis=-1, keepdims=True) + RMS_EPS)


def _pack_bf16_pairs(x):
    half = x.shape[1] // 2
    bits = pltpu.bitcast(x.astype(BF16).astype(F32), jnp.int32)
    return bits[:, :half] | lax.shift_right_logical(bits[:, half:], 16)


def _unpack_bf16_pairs(w):
    return pltpu.bitcast(w & jnp.int32(-65536), F32), pltpu.bitcast(lax.shift_left(w, 16), F32)


def _seg_sums(x, m_ref):
    return _dot((x * x).astype(BF16), m_ref[...])


def _seg_inv_rms(ss, inv_cnt, mt2_ref):
    r = lax.rsqrt(ss * inv_cnt + RMS_EPS)
    r_hi = r.astype(BF16)
    r_lo = (r - r_hi.astype(F32)).astype(BF16)
    return _dot(jnp.concatenate([r_hi, r_lo], axis=1), mt2_ref[...])


def _seg_rms_scale(x, m_ref, mt2_ref, inv_cnt):
    return _seg_inv_rms(_seg_sums(x, m_ref), inv_cnt, mt2_ref)


def _rope_key_chunk(kpe_chunk, gkpe_ref, rope):
    ms = jnp.sum(kpe_chunk * kpe_chunk, axis=-1, keepdims=True) * (1.0 / ROPE_DIM)
    kp = kpe_chunk * lax.rsqrt(ms + RMS_EPS) * gkpe_ref[...]
    return kp if rope is None else _rope_chunk(kp, *rope)


def _rope_chunk(xc, cosf, sinf, first_half):
    partner = jnp.where(first_half, pltpu.roll(xc, HEAD_PAD - ROPE_HALF, 1), pltpu.roll(xc, ROPE_HALF, 1))
    return xc * cosf + partner * sinf


def _write_keys(ckv, kpe_chunk, wk_ref, wv_ref, gk_ref, gkpe_ref, mk_ref, mkt2_ref, rope, k_out, v_out):
    cb = ckv.astype(BF16)
    kn = _dot(cb, wk_ref[...])
    kn = kn * _seg_rms_scale(kn, mk_ref, mkt2_ref, 1.0 / QK_NOPE) * gk_ref[...]
    v_out[...] = _dot(cb, wv_ref[...]).astype(v_out.dtype)
    kp = _rope_key_chunk(kpe_chunk, gkpe_ref, rope)
    for h in range(N_HEADS):
        sl = slice(h * HEAD_PAD, (h + 1) * HEAD_PAD)
        k_out[:, sl] = (kn[:, sl] + kp).astype(k_out.dtype)


def _mod_kernel(c_ref, w_ref, b_ref, o_ref):
    c = c_ref[...]
    s = c * jax.nn.sigmoid(c)
    o_ref[...] = _dot(s.astype(BF16), w_ref[...].astype(BF16)) + b_ref[...]


def _modulation(cond8, w_mod, b_mod):
    depth = w_mod.shape[0]
    n = w_mod.shape[2]
    return pl.pallas_call(
        _mod_kernel,
        out_shape=jax.ShapeDtypeStruct((depth, 8, n), F32),
        grid=(depth, n // MOD_COL_TILE),
        in_specs=[
            pl.BlockSpec((8, D_MODEL), lambda l, j: (0, 0)),
            pl.BlockSpec((None, D_MODEL, MOD_COL_TILE), lambda l, j: (l, 0, j)),
            pl.BlockSpec((None, 1, MOD_COL_TILE), lambda l, j: (l, 0, j)),
        ],
        out_specs=pl.BlockSpec((None, 8, MOD_COL_TILE), lambda l, j: (l, 0, j)),
        compiler_params=_cparams("arbitrary", "arbitrary"),
        name="adaln_modulation",
    )(cond8, w_mod, b_mod.reshape(depth, 1, n))


def _moe_residual(tw_ref, x1_ref, mod_ref, yg_ref):
    acc = None
    for k in range(TOP_K):
        w = tw_ref[:, k:k + 1]
        part = [w * half for half in _unpack_bf16_pairs(yg_ref[k])]
        acc = part if acc is None else [a + p for a, p in zip(acc, part)]
    return x1_ref[...] + mod_ref[5:6, :] * jnp.concatenate(acc, axis=1)


def _in_kernel(*refs, after_moe):
    if after_moe:
        x = _moe_residual(*refs[:4])
        refs = refs[4:]
        refs[-1][...] = x
        refs = refs[:-1]
    else:
        x = refs[0][...]
        refs = refs[1:]
    (mod_ref, g1_ref, wa_ref, wb_ref, qag_ref, kvag_ref, wq_ref, wk_ref, wv_ref,
     gq_ref, gk_ref, gkpe_ref, cos_ref, sin_ref, mq_ref, mqt2_ref, icq_ref, mk_ref, mkt2_ref,
     q_out, k_out, v_out, ckv_out, kpe_out, f_out, p_out, g_out) = refs
    shift = mod_ref[0:1, :]
    scale = mod_ref[1:2, :]
    h = _rms(x) * g1_ref[...] * (1.0 + scale) + shift
    hb = h.astype(BF16)

    gate_chunks = iter(range(0, N_BRANCHES * D_MODEL, GATE_COL_CHUNK))

    def gate_chunk():
        c0 = next(gate_chunks)
        zc = _dot(hb, wb_ref[:, MIX_WIDTH + c0:MIX_WIDTH + c0 + GATE_COL_CHUNK])
        g_out[:, c0:c0 + GATE_COL_CHUNK] = jax.nn.sigmoid(zc).astype(g_out.dtype)

    za = _dot(hb, wa_ref[...])
    zb = _dot(hb, wb_ref[:, 0:MIX_WIDTH])
    f_out[...] = zb[:, 0:FNET_WIDTH].astype(f_out.dtype)
    p_out[...] = zb[:, FNET_WIDTH:]
    q_lat = za[:, 0:COL_KV]
    kv_lat = za[:, COL_KV:COL_KPE]
    key_lane = lax.broadcasted_iota(jnp.int32, (hb.shape[0], KPE_PAD), 1) < ROPE_DIM
    kpe_grp = jnp.where(key_lane, za[:, COL_KPE:COL_F], 0.0)
    ckv = _rms(kv_lat) * kvag_ref[...]
    ckv_out[...] = ckv
    kpe_out[...] = kpe_grp[:, 0:ROPE_DIM]
    qn = (_rms(q_lat) * qag_ref[...]).astype(BF16)
    cb = ckv.astype(BF16)
    gate_chunk()

    q = _dot(qn, wq_ref[...])
    kn = _dot(cb, wk_ref[...])
    v_out[...] = _dot(cb, wv_ref[...]).astype(v_out.dtype)
    gate_chunk()

    ss_q = _seg_sums(q, mq_ref)
    ss_k = _seg_sums(kn, mk_ref)
    gate_chunk()

    q = q * _seg_inv_rms(ss_q, icq_ref[...], mqt2_ref) * gq_ref[...]
    kn = kn * _seg_inv_rms(ss_k, 1.0 / QK_NOPE, mkt2_ref) * gk_ref[...]
    for _ in range(N_BRANCHES * D_MODEL // GATE_COL_CHUNK - 3):
        gate_chunk()

    cosf = cos_ref[...]
    sinf = sin_ref[...]
    lane = lax.broadcasted_iota(jnp.int32, cosf.shape, 1)
    rope = (cosf, sinf, lane < QK_NOPE + ROPE_HALF)
    sm_scale = 1.0 / math.sqrt(QK_HEAD)
    kp = _rope_key_chunk(pltpu.roll(kpe_grp, QK_NOPE, 1), gkpe_ref, rope)
    for hd in range(N_HEADS):
        sl = slice(hd * HEAD_PAD, (hd + 1) * HEAD_PAD)
        q_out[:, sl] = (_rope_chunk(q[:, sl], *rope) * sm_scale).astype(q_out.dtype)
        k_out[:, sl] = (kn[:, sl] + kp).astype(k_out.dtype)


def _const_spec(shape):
    nd = len(shape)
    return pl.BlockSpec(shape, lambda i, _n=nd: (0,) * _n, pipeline_mode=pl.Buffered(1))


def _layer_spec(shape, l):
    nd = len(shape)
    return pl.BlockSpec((None,) + tuple(shape), lambda i, _l=l, _n=nd: (_l,) + (0,) * _n,
                        pipeline_mode=pl.Buffered(1))


def _in_projection(x, moe, mods, l, wts, consts, tok):
    after_moe = x is None
    t = moe[0].shape[0] if after_moe else x.shape[0]
    tm = IN_TILE
    n_tiles = t // tm
    p_tiles = tok["t_prompt"] // tm
    s_tiles = tok["dec_seq"] // tm
    rope_blocks = tok["dec_seq"] // tm

    def mod_row(i):
        return jnp.where(i < p_tiles, 0, 1 + (i - p_tiles) // s_tiles)

    def rope_idx(i):
        return (jnp.where(i < p_tiles, rope_blocks, (i - p_tiles) % s_tiles), 0)

    row = lambda w: pl.BlockSpec((tm, w), lambda i: (i, 0))
    if after_moe:
        x1, tw, yg = moe
        lead_specs = [row(ROUTER_PAD), row(D_MODEL),
                      pl.BlockSpec((None, None, 6, D_MODEL), lambda i: (l - 1, mod_row(i), 0, 0)),
                      pl.BlockSpec((TOP_K, tm, D_MODEL // 2), lambda i: (0, i, 0))]
        lead_args = [tw, x1, mods, yg]
    else:
        lead_specs = [row(D_MODEL)]
        lead_args = [x]
    in_specs = lead_specs + [
        pl.BlockSpec((None, None, 6, D_MODEL), lambda i: (l, mod_row(i), 0, 0)),
        _layer_spec((1, D_MODEL), l),
        _layer_spec((D_MODEL, COL_F), l),
        _layer_spec((D_MODEL, MIX_WIDTH + N_BRANCHES * D_MODEL), l),
        _layer_spec((1, Q_RANK), l),
        _layer_spec((1, KV_RANK), l),
        _layer_spec((Q_RANK, QK_PAD), l),
        _layer_spec((KV_RANK, QK_PAD), l),
        _layer_spec((KV_RANK, V_WIDTH), l),
        _layer_spec((1, QK_PAD), l),
        _layer_spec((1, QK_PAD), l),
        _layer_spec((1, HEAD_PAD), l),
        pl.BlockSpec((tm, HEAD_PAD), rope_idx),
        pl.BlockSpec((tm, HEAD_PAD), rope_idx),
        _const_spec((QK_PAD, LANES)),
        _const_spec((2 * LANES, QK_PAD)),
        _const_spec((1, LANES)),
        _const_spec((QK_PAD, LANES)),
        _const_spec((2 * LANES, QK_PAD)),
    ]
    out_shape = [
        jax.ShapeDtypeStruct((t, QK_PAD), BF16),
        jax.ShapeDtypeStruct((t, QK_PAD), BF16),
        jax.ShapeDtypeStruct((t, V_WIDTH), BF16),
        jax.ShapeDtypeStruct((t, KV_RANK), F32),
        jax.ShapeDtypeStruct((t, ROPE_DIM), F32),
        jax.ShapeDtypeStruct((t, FNET_WIDTH), BF16),
        jax.ShapeDtypeStruct((t, POOL_WIDTH), F32),
        jax.ShapeDtypeStruct((t, N_BRANCHES * D_MODEL), BF16),
    ]
    if after_moe:
        out_shape.append(jax.ShapeDtypeStruct((t, D_MODEL), F32))
    out_specs = [row(s.shape[1]) for s in out_shape]
    outs = pl.pallas_call(
        functools.partial(_in_kernel, after_moe=after_moe),
        out_shape=out_shape,
        grid=(n_tiles,),
        in_specs=in_specs,
        out_specs=out_specs,
        compiler_params=_cparams("arbitrary"),
        cost_estimate=pl.CostEstimate(
            flops=2 * t * (D_MODEL * IN_PAD_WIDTH + Q_RANK * QK_PAD + KV_RANK * (QK_PAD + V_WIDTH)
                           + 2 * (QK_PAD * LANES + 2 * LANES * QK_PAD)),
            transcendentals=t * N_BRANCHES * D_MODEL,
            bytes_accessed=2 * D_MODEL * IN_PAD_WIDTH + sum(
                math.prod(s.shape) * s.dtype.itemsize for s in out_shape)
            + 4 * t * D_MODEL * (1 + (TOP_K // 2 if after_moe else 0))),
        name="in_projection",
    )(*lead_args, mods, wts["norm1_g"], wts["w_in_a"], wts["w_in_b"], wts["q_a_g"], wts["kv_a_g"], wts["w_q"], wts["w_k"], wts["w_v"],
      wts["g_q"], wts["g_k"], wts["g_kpe"], consts["cosf"], consts["sinf"],
      consts["m_q"], consts["mt2_q"], consts["inv_cnt_q"], consts["m_k"], consts["mt2_k"])
    return (outs[:8], outs[8]) if after_moe else (outs, x)


def _ctx_keys_kernel(ckv_ref, kpe_ref, wk_ref, wv_ref, gk_ref, gkpe_ref, mk_ref, mkt2_ref, k_out, v_out):
    _write_keys(ckv_ref[...], kpe_ref[...], wk_ref, wv_ref, gk_ref, gkpe_ref, mk_ref, mkt2_ref, None, k_out, v_out)


def _ctx_keys(cache_ckv, cache_kpe_pad, wts, consts):
    nb, depth, past, _ = cache_ckv.shape
    lw = lambda shape: pl.BlockSpec((None,) + shape, lambda l, b: (l,) + (0,) * len(shape))
    cs = lambda shape: pl.BlockSpec(shape, lambda l, b: (0,) * len(shape))
    return pl.pallas_call(
        _ctx_keys_kernel,
        out_shape=[jax.ShapeDtypeStruct((depth, nb * past, QK_PAD), BF16),
                   jax.ShapeDtypeStruct((depth, nb * past, V_WIDTH), BF16)],
        grid=(depth, nb),
        in_specs=[
            pl.BlockSpec((None, None, past, KV_RANK), lambda l, b: (b, l, 0, 0)),
            pl.BlockSpec((None, None, past, HEAD_PAD), lambda l, b: (b, l, 0, 0)),
            lw((KV_RANK, QK_PAD)), lw((KV_RANK, V_WIDTH)), lw((1, QK_PAD)), lw((1, HEAD_PAD)),
            cs((QK_PAD, LANES)), cs((2 * LANES, QK_PAD)),
        ],
        out_specs=[pl.BlockSpec((None, past, QK_PAD), lambda l, b: (l, b, 0)),
                   pl.BlockSpec((None, past, V_WIDTH), lambda l, b: (l, b, 0))],
        compiler_params=_cparams("arbitrary", "arbitrary"),
        name="context_keys",
    )(cache_ckv, cache_kpe_pad, wts["w_k"], wts["w_v"], wts["g_k"], wts["g_kpe"], consts["m_k"], consts["mt2_k"])


def _attn_kernel(*refs, n_parts, seqs):
    q_ref = refs[0]
    k_refs = refs[1:1 + 2 * n_parts:2]
    v_refs = refs[2:2 + 2 * n_parts:2]
    o_ref = refs[1 + 2 * n_parts]
    q_rows = q_ref.shape[0] // seqs
    lane = lax.broadcasted_iota(jnp.int32, (q_rows, 2 * V_HEAD), 1)
    for s in range(seqs):
        rows = slice(s * q_rows, (s + 1) * q_rows)
        keys = [slice(s * (k.shape[0] // seqs), (s + 1) * (k.shape[0] // seqs)) for k in k_refs]
        for pair in range(N_HEADS // 2):
            vsl = slice(pair * 2 * V_HEAD, (pair + 1) * 2 * V_HEAD)
            outs = []
            for hd in (2 * pair, 2 * pair + 1):
                sl = slice(hd * HEAD_PAD, (hd + 1) * HEAD_PAD)
                qh = q_ref[rows, sl]
                ss = [lax.dot_general(qh, k[ks, sl], (((1,), (1,)), ((), ())), preferred_element_type=F32)
                      for k, ks in zip(k_refs, keys)]
                m = functools.reduce(jnp.maximum, [jnp.max(sc, axis=-1, keepdims=True) for sc in ss])
                es = [jnp.exp(sc - m) for sc in ss]
                den = functools.reduce(jnp.add, [jnp.sum(e, axis=-1, keepdims=True) for e in es])
                acc = functools.reduce(jnp.add, [_dot(e.astype(BF16), v[ks, vsl])
                                                 for e, v, ks in zip(es, v_refs, keys)])
                outs.append(acc / den)
            o_ref[rows, vsl] = jnp.where(lane < V_HEAD, outs[0], outs[1]).astype(o_ref.dtype)


def _attention(q, kv_parts, n_seq, seq_len):
    tq = min(ATTN_Q_TILE, seq_len)
    nq = seq_len // tq
    n_keys = sum(rows for _, _, _, rows in kv_parts)
    seqs = math.gcd(n_seq, max(1, SHORT_SEQ_ROWS // seq_len)) if nq == 1 and len(kv_parts) == 1 else 1
    in_specs = [pl.BlockSpec((seqs * tq, QK_PAD), lambda b, i: (b * nq + i, 0))]
    args = [q]
    for k, v, layer, rows in kv_parts:
        for arr, width in ((k, QK_PAD), (v, V_WIDTH)):
            if layer is None:
                in_specs.append(pl.BlockSpec((seqs * rows, width), lambda b, i: (b, 0)))
            else:
                in_specs.append(pl.BlockSpec((None, rows, width), lambda b, i, _l=layer: (_l, b, 0)))
            args.append(arr)
    return pl.pallas_call(
        functools.partial(_attn_kernel, n_parts=len(kv_parts), seqs=seqs),
        out_shape=jax.ShapeDtypeStruct((n_seq * seq_len, V_WIDTH), BF16),
        grid=(n_seq // seqs, nq),
        in_specs=in_specs,
        out_specs=pl.BlockSpec((seqs * tq, V_WIDTH), lambda b, i: (b * nq + i, 0)),
        compiler_params=_cparams("arbitrary", "arbitrary"),
        cost_estimate=pl.CostEstimate(
            flops=2 * n_seq * seq_len * n_keys * N_HEADS * (HEAD_PAD + 2 * V_HEAD),
            transcendentals=n_seq * seq_len * n_keys * N_HEADS,
            bytes_accessed=2 * n_seq * (seq_len * (QK_PAD + V_WIDTH) + n_keys * (QK_PAD + V_WIDTH))),
        name="attention",
    )(*args)


def _fnet_kernel(f_ref, cs_ref, cl_ref, sl_ref, o_ref, top_ref, bot_ref, *, norm, seqs):
    @pl.when(pl.program_id(1) == 0)
    def _():
        for g in range(FNET_GROUPS):
            sl = slice(g * FNET_GC, (g + 1) * FNET_GC)
            a = _dot(f_ref[:, sl], cs_ref[...])
            top_ref[:, sl] = a[:, :FNET_GC].astype(BF16)
            bot_ref[:, sl] = a[:, FNET_GC:].astype(BF16)

    seq_len = f_ref.shape[0] // seqs
    rows = o_ref.shape[0] // seqs
    for s in range(seqs):
        src = slice(s * seq_len, (s + 1) * seq_len)
        y = _dot(cl_ref[...], top_ref[src, :]) - _dot(sl_ref[...], bot_ref[src, :])
        o_ref[s * rows:(s + 1) * rows, :] = (y * norm).astype(o_ref.dtype)


def _fourier(f_in, n_seq, seq_len, row0, tabs):
    tr = min(FNET_ROW_TILE, seq_len)
    nj = seq_len // tr
    seqs = math.gcd(n_seq, max(1, SHORT_SEQ_ROWS // seq_len)) if nj == 1 else 1
    sb0 = row0 // (seqs * seq_len)
    return pl.pallas_call(
        functools.partial(_fnet_kernel, norm=1.0 / math.sqrt(seq_len * FNET_GC), seqs=seqs),
        out_shape=jax.ShapeDtypeStruct((n_seq * seq_len, FNET_WIDTH), BF16),
        grid=(n_seq // seqs, nj),
        in_specs=[
            pl.BlockSpec((seqs * seq_len, FNET_WIDTH), lambda b, j: (sb0 + b, 0)),
            pl.BlockSpec((FNET_GC, 2 * FNET_GC), lambda b, j: (0, 0)),
            pl.BlockSpec((tr, seq_len), lambda b, j: (j, 0)),
            pl.BlockSpec((tr, seq_len), lambda b, j: (j, 0)),
        ],
        out_specs=pl.BlockSpec((seqs * tr, FNET_WIDTH), lambda b, j: (b * nj + j, 0)),
        scratch_shapes=[pltpu.VMEM((seqs * seq_len, FNET_WIDTH), BF16),
                        pltpu.VMEM((seqs * seq_len, FNET_WIDTH), BF16)],
        compiler_params=_cparams("arbitrary", "arbitrary"),
        name="fourier_mix",
    )(f_in, tabs["chan"], tabs["cos"], tabs["sin"])


def _pool_kernel(p_ref, wg_ref, ps_ref, o_ref, pad_ref, *, seqs):
    seq_len = p_ref.shape[0] // seqs
    zeros = jnp.zeros((POOL_HALO, POOL_WIDTH), F32)
    ch = min(POOL_CHUNK, seq_len)
    for s in range(seqs):
        s0 = s * seq_len
        pad_ref[s, 0:POOL_HALO, :] = zeros
        pad_ref[s, POOL_HALO + seq_len:, :] = zeros
        pad_ref[s, POOL_HALO:POOL_HALO + seq_len, :] = p_ref[s0:s0 + seq_len, :]
        for c in range(seq_len // ch):
            t = lax.broadcasted_iota(jnp.int32, (ch, 1), 0) + c * ch
            for g, w in enumerate(POOL_WINDOWS):
                half = w // 2
                sl = slice(g * POOL_GC, (g + 1) * POOL_GC)
                acc = None
                for j in range(-half, half):
                    r0 = POOL_HALO + c * ch + j
                    part = pad_ref[s, r0:r0 + ch, sl]
                    acc = part if acc is None else acc + part
                cnt = (jnp.minimum(t + half, seq_len) - jnp.maximum(t - half, 0)).astype(F32)
                pooled = acc / cnt - p_ref[s0 + c * ch:s0 + (c + 1) * ch, sl]
                mixed = _dot(pooled.astype(BF16), wg_ref[g]) * ps_ref[:, sl]
                o_ref[s0 + c * ch:s0 + (c + 1) * ch, sl] = mixed.astype(o_ref.dtype)


def _pooling(p_in, n_seq, seq_len, row0, l, wts):
    seqs = math.gcd(n_seq, max(1, SHORT_SEQ_ROWS // seq_len))
    sb0 = row0 // (seqs * seq_len)
    g = len(POOL_WINDOWS)
    return pl.pallas_call(
        functools.partial(_pool_kernel, seqs=seqs),
        out_shape=jax.ShapeDtypeStruct((n_seq * seq_len, POOL_WIDTH), BF16),
        grid=(n_seq // seqs,),
        in_specs=[
            pl.BlockSpec((seqs * seq_len, POOL_WIDTH), lambda b: (sb0 + b, 0)),
            pl.BlockSpec((None, g, POOL_GC, POOL_GC), lambda b: (l, 0, 0, 0)),
            pl.BlockSpec((None, 1, POOL_WIDTH), lambda b: (l, 0, 0)),
        ],
        out_specs=pl.BlockSpec((seqs * seq_len, POOL_WIDTH), lambda b: (b, 0)),
        scratch_shapes=[pltpu.VMEM((seqs, seq_len + 2 * POOL_HALO, POOL_WIDTH), F32)],
        compiler_params=_cparams("arbitrary"),
        name="pool_mix",
    )(p_in, wts["w_pool_grp"], wts["pool_scale"])


def _merge_kernel(x_ref, mod_ref, a_ref, f_ref, p_ref, g_ref,
                  wa_ref, wf_ref, wp_ref, wo_ref, g2_ref, wr_ref, br_ref, tri_ref,
                  x1_out, hp_out, tw_out, er_out, cnt_out, mix_scr, carry_ref):
    j = pl.program_id(0)

    @pl.when(j == 0)
    def _():
        mix_scr[1] = jnp.zeros(mix_scr.shape[1:], F32)

    @pl.when(j <= 1)
    def _():
        carry_ref[...] = jnp.zeros(carry_ref.shape, F32)

    def branch_chunks(lo, hi):
        out = []
        for c0 in range(lo, hi, MERGE_COL_CHUNK):
            cs = slice(c0, c0 + MERGE_COL_CHUNK)
            a = _dot(a_ref[...], wa_ref[:, cs])
            f = _dot(f_ref[...], wf_ref[:, cs])
            p = _dot(p_ref[...], wp_ref[:, cs])
            out.append((g_ref[:, c0:c0 + MERGE_COL_CHUNK].astype(F32) * a
                        + g_ref[:, D_MODEL + c0:D_MODEL + c0 + MERGE_COL_CHUNK].astype(F32) * f
                        + g_ref[:, 2 * D_MODEL + c0:2 * D_MODEL + c0 + MERGE_COL_CHUNK].astype(F32) * p
                        ).astype(BF16))
        return out

    def step(cur, prev):
        chunks = branch_chunks(0, D_MODEL // 2)

        gate1 = mod_ref[2:3, :]
        shift2 = mod_ref[3:4, :]
        scale2 = mod_ref[4:5, :]
        x1 = x_ref[...] + gate1 * mix_scr[prev]
        x1_out[...] = x1
        h2 = _rms(x1) * g2_ref[...] * (1.0 + scale2) + shift2
        hp_out[...] = _pack_bf16_pairs(h2)
        h_hi = h2.astype(BF16)
        h_lo = (h2 - h_hi.astype(F32)).astype(BF16)
        logits = _dot(jnp.concatenate([h_hi, h_lo, h_hi], axis=1), wr_ref[...]) + br_ref[...]

        chunks += branch_chunks(D_MODEL // 2, D_MODEL)
        mix_scr[cur] = _dot(jnp.concatenate(chunks, axis=1), wo_ref[...])

        lane = lax.broadcasted_iota(jnp.int32, logits.shape, 1)
        work = logits
        vals, idxs = [], []
        for _ in range(TOP_K):
            m = jnp.max(work, axis=-1, keepdims=True)
            idx = jnp.min(jnp.where(work == m, lane, ROUTER_PAD), axis=-1, keepdims=True)
            vals.append(m)
            idxs.append(idx)
            work = jnp.where(lane == idx, -jnp.inf, work)
        es = [jnp.exp(v - vals[0]) for v in vals]
        den = functools.reduce(jnp.add, es)

        chosen = functools.reduce(jnp.logical_or, [lane == idx for idx in idxs])
        hot = jnp.where(chosen, 1.0, 0.0)
        before = _dot(tri_ref[...], hot.astype(BF16)) + carry_ref[...]
        carry_ref[...] = carry_ref[...] + jnp.sum(hot, axis=0, keepdims=True)
        cnt_out[...] = carry_ref[...].astype(jnp.int32)

        er = jnp.zeros(logits.shape, jnp.int32)
        tw = jnp.zeros(logits.shape, F32)
        for k in range(TOP_K):
            rank_k = jnp.sum(jnp.where(lane == idxs[k], before, 0.0), axis=-1, keepdims=True).astype(jnp.int32)
            er = jnp.where(lane == k, idxs[k], er)
            er = jnp.where(lane == TOP_K + k, rank_k, er)
            tw = jnp.where(lane == k, es[k] / den, tw)
        tw_out[...] = tw
        er_out[...] = jnp.transpose(er)[0:2 * TOP_K, :]

    for parity in (0, 1):
        pl.when(j % 2 == parity)(functools.partial(step, parity, 1 - parity))


def _merge(x, mods, attn, fnet, pool, gates, l, wts, consts, tok):
    t = x.shape[0]
    tm = TOKEN_TILE
    p_tiles = tok["t_prompt"] // tm
    s_tiles = tok["dec_seq"] // tm

    n_tiles = t // tm
    routed = lambda j: jnp.maximum(j - 1, 0)
    mixed = lambda j: jnp.minimum(j, n_tiles - 1)

    def mod_idx(j):
        i = routed(j)
        row = jnp.where(i < p_tiles, 0, 1 + (i - p_tiles) // s_tiles)
        return (l, row, 0, 0)

    row = lambda w: pl.BlockSpec((tm, w), lambda j: (routed(j), 0))
    head = lambda w: pl.BlockSpec((tm, w), lambda j: (mixed(j), 0))
    out_shape = [
        jax.ShapeDtypeStruct((t, D_MODEL), F32),
        jax.ShapeDtypeStruct((t, D_MODEL // 2), jnp.int32),
        jax.ShapeDtypeStruct((t, ROUTER_PAD), F32),
    ]
    return pl.pallas_call(
        _merge_kernel,
        out_shape=out_shape + [jax.ShapeDtypeStruct((2 * TOP_K, t), jnp.int32),
                               jax.ShapeDtypeStruct((1, ROUTER_PAD), jnp.int32)],
        grid=(n_tiles + 1,),
        in_specs=[
            row(D_MODEL),
            pl.BlockSpec((None, None, 6, D_MODEL), mod_idx),
            head(V_WIDTH), head(FNET_WIDTH), head(POOL_WIDTH), head(N_BRANCHES * D_MODEL),
            _layer_spec((V_WIDTH, D_MODEL), l),
            _layer_spec((FNET_WIDTH, D_MODEL), l),
            _layer_spec((POOL_WIDTH, D_MODEL), l),
            _layer_spec((D_MODEL, D_MODEL), l),
            _layer_spec((1, D_MODEL), l),
            _layer_spec((3 * D_MODEL, ROUTER_PAD), l),
            _layer_spec((1, ROUTER_PAD), l),
            _const_spec((tm, tm)),
        ],
        out_specs=[row(s.shape[1]) for s in out_shape] + [
            pl.BlockSpec((2 * TOP_K, tm), lambda j: (0, routed(j))),
            pl.BlockSpec((1, ROUTER_PAD), lambda j: (0, 0))],
        scratch_shapes=[pltpu.VMEM((2, tm, D_MODEL), F32), pltpu.VMEM((1, ROUTER_PAD), F32)],
        compiler_params=_cparams("arbitrary"),
        cost_estimate=pl.CostEstimate(
            flops=2 * t * (D_MODEL * (V_WIDTH + FNET_WIDTH + POOL_WIDTH + D_MODEL) + 3 * D_MODEL * ROUTER_PAD
                           + tm * ROUTER_PAD),
            transcendentals=t * TOP_K,
            bytes_accessed=t * (2 * (V_WIDTH + FNET_WIDTH + POOL_WIDTH + N_BRANCHES * D_MODEL) + 4 * 2 * D_MODEL
                                + 4 * (D_MODEL // 2) + 3 * 4 * ROUTER_PAD)
            + 2 * D_MODEL * (V_WIDTH + FNET_WIDTH + POOL_WIDTH + D_MODEL + 3 * ROUTER_PAD)),
        name="merge_router",
    )(x, mods, attn, fnet, pool, gates, wts["w_attn_o"], wts["w_fnet_o"], wts["w_pool_o"], wts["w_out"],
      wts["norm2_g"], wts["w_router3"], wts["b_router"], consts["tri"])


def _sc_row_gather(table, idx):
    n = idx.shape[0]
    d = table.shape[1]
    info = plsc.get_sparse_core_info()
    n_cores = info.num_cores
    n_workers = n_cores * info.num_subcores
    per_worker = n // n_workers
    n_chunks = per_worker // SC_GATHER_ROWS
    assert n_chunks * SC_GATHER_ROWS * n_workers == n
    mesh = plsc.VectorSubcoreMesh(core_axis_name="core", subcore_axis_name="subcore")

    @functools.partial(
        pl.kernel, mesh=mesh, out_type=jax.ShapeDtypeStruct((n, d), table.dtype),
        scratch_types=[pltpu.VMEM((SC_GATHER_ROWS,), jnp.int32), pltpu.VMEM((SC_GATHER_ROWS, d), table.dtype),
                       pltpu.SemaphoreType.DMA],
        cost_estimate=pl.CostEstimate(flops=0, transcendentals=0,
                                      bytes_accessed=2 * n * d * table.dtype.itemsize + 4 * n),
        name="sc_row_gather")
    def gather(table_hbm, idx_hbm, out_hbm, idx_v, rows_v, sem):
        worker = lax.axis_index("subcore") * n_cores + lax.axis_index("core")

        @pl.loop(0, n_chunks)
        def _(c):
            base = worker * per_worker + c * SC_GATHER_ROWS
            pltpu.sync_copy(idx_hbm.at[pl.ds(base, SC_GATHER_ROWS)], idx_v)
            pltpu.async_copy(table_hbm.at[idx_v], rows_v, sem).wait()
            pltpu.sync_copy(rows_v, out_hbm.at[pl.ds(base, SC_GATHER_ROWS)])

    return gather(table, idx)


PLAN_EXPERT, PLAN_VALID, PLAN_FIRST, PLAN_SLOT, PLAN_NEXT, PLAN_USED = range(6)


def _expert_kernel(plan_ref, xs_ref, bgu_ref, bd_ref, wgu_hbm, wd_hbm, y_ref,
                   wgu_f32, wd_f32, wgu_bf, wd_bf, sems, *, layer):
    b = pl.program_id(0)

    def weight_copies(expert, slot):
        return (pltpu.make_async_copy(wgu_hbm.at[layer, expert], wgu_f32.at[slot], sems.at[0, slot]),
                pltpu.make_async_copy(wd_hbm.at[layer, expert], wd_f32.at[slot], sems.at[1, slot]))

    def ffn(rows):
        x = jnp.concatenate(_unpack_bf16_pairs(xs_ref[0:rows, :]), axis=1).astype(BF16)
        gu = _dot(x, wgu_bf[...]) + bgu_ref[...]
        glu = jnp.minimum(gu[:, :D_FF], SWIGLU_LIMIT)
        lin = jnp.clip(gu[:, D_FF:], -SWIGLU_LIMIT, SWIGLU_LIMIT)
        act = glu * jax.nn.sigmoid(SWIGLU_ALPHA * glu) * (lin + 1.0)
        y_ref[0:rows, :] = _pack_bf16_pairs(_dot(act.astype(BF16), wd_bf[...]) + bd_ref[...])
        if rows < MOE_BLOCK:
            y_ref[rows:, :] = jnp.zeros((MOE_BLOCK - rows, D_MODEL // 2), y_ref.dtype)

    n_used = plan_ref[PLAN_USED, 0]

    @pl.when(b < n_used)
    def _():
        slot = plan_ref[PLAN_SLOT, b]

        @pl.when(b == 0)
        def _():
            for cp in weight_copies(plan_ref[PLAN_EXPERT, 0], 0):
                cp.start()

        @pl.when(plan_ref[PLAN_FIRST, b] == 1)
        def _():
            for cp in weight_copies(plan_ref[PLAN_EXPERT, b], slot):
                cp.wait()
            wgu_bf[...] = wgu_f32[slot].astype(BF16)
            wd_bf[...] = wd_f32[slot].astype(BF16)

            @pl.when(plan_ref[PLAN_NEXT, b] >= 0)
            def _():
                for cp in weight_copies(plan_ref[PLAN_NEXT, b], 1 - slot):
                    cp.start()

        steps = (plan_ref[PLAN_VALID, b] + (MOE_STEP - 1)) // MOE_STEP
        for rows in range(MOE_STEP, MOE_BLOCK + 1, MOE_STEP):
            want = steps <= 1 if rows == MOE_STEP else steps == rows // MOE_STEP
            pl.when(want)(functools.partial(ffn, rows))

    @pl.when(b >= n_used)
    def _():
        y_ref[...] = jnp.zeros(y_ref.shape, y_ref.dtype)


def _experts(xs, plan, l, w_gu, b_gu4, w_down, b_down4):
    n_slots = xs.shape[0]
    bm = MOE_BLOCK
    n_blocks = n_slots // bm

    def blk(b, plan):
        return jnp.minimum(b, plan[PLAN_USED, 0] - 1)

    def bspec(width):
        return pl.BlockSpec((None, None, 1, width), lambda b, plan: (l, plan[PLAN_EXPERT, blk(b, plan)], 0, 0))

    grid_spec = pltpu.PrefetchScalarGridSpec(
        num_scalar_prefetch=1,
        grid=(n_blocks,),
        in_specs=[
            pl.BlockSpec((bm, D_MODEL // 2), lambda b, plan: (blk(b, plan), 0)),
            bspec(2 * D_FF), bspec(D_MODEL),
            pl.BlockSpec(memory_space=pl.ANY), pl.BlockSpec(memory_space=pl.ANY),
        ],
        out_specs=pl.BlockSpec((bm, D_MODEL // 2), lambda b, plan: (b, 0)),
        scratch_shapes=[pltpu.VMEM((2, D_MODEL, 2 * D_FF), F32), pltpu.VMEM((2, D_FF, D_MODEL), F32),
                        pltpu.VMEM((D_MODEL, 2 * D_FF), BF16), pltpu.VMEM((D_FF, D_MODEL), BF16),
                        pltpu.SemaphoreType.DMA((2, 2))],
    )
    return pl.pallas_call(
        functools.partial(_expert_kernel, layer=l),
        out_shape=jax.ShapeDtypeStruct((n_slots, D_MODEL // 2), jnp.int32),
        grid_spec=grid_spec,
        compiler_params=_cparams("arbitrary"),
        cost_estimate=pl.CostEstimate(
            flops=2 * n_slots * 3 * D_MODEL * D_FF, transcendentals=n_slots * D_FF,
            bytes_accessed=4 * N_EXPERTS * 3 * D_MODEL * D_FF + 2 * 4 * n_slots * (D_MODEL // 2)),
        name="moe_experts",
    )(plan, xs, b_gu4, b_down4, w_gu, w_down)


def _combine_kernel(tw_ref, x_ref, mod_ref, yg_ref, o_ref):
    o_ref[...] = _moe_residual(tw_ref, x_ref, mod_ref, yg_ref)


def _combine(yg, tw, x1, mods, l, tok):
    t = x1.shape[0]
    tc = COMBINE_TILE
    p_tiles = tok["t_prompt"] // tc
    s_tiles = tok["dec_seq"] // tc

    def mod_idx(i):
        row = jnp.where(i < p_tiles, 0, 1 + (i - p_tiles) // s_tiles)
        return (l, row, 0, 0)

    return pl.pallas_call(
        _combine_kernel,
        out_shape=jax.ShapeDtypeStruct((t, D_MODEL), F32),
        grid=(t // tc,),
        in_specs=[
            pl.BlockSpec((tc, ROUTER_PAD), lambda i: (i, 0)),
            pl.BlockSpec((tc, D_MODEL), lambda i: (i, 0)),
            pl.BlockSpec((None, None, 6, D_MODEL), mod_idx),
            pl.BlockSpec((TOP_K, tc, D_MODEL // 2), lambda i: (0, i, 0)),
        ],
        out_specs=pl.BlockSpec((tc, D_MODEL), lambda i: (i, 0)),
        compiler_params=_cparams("arbitrary"),
        name="moe_combine",
    )(tw, x1, mods, yg)


def _slot_tokens(slot, counts, padded_start, padded_end, n_slots, t):
    n = slot.shape[0]
    j = jnp.arange(MOE_BLOCK, dtype=jnp.int32)
    pad_key = (padded_start + counts)[:, None] + j[None, :]
    pad_key = jnp.where(pad_key < padded_end[:, None], pad_key, n_slots).reshape(N_EXPERTS * MOE_BLOCK)
    pad_val = jnp.arange(N_EXPERTS * MOE_BLOCK, dtype=jnp.int32) % t
    keys = jnp.concatenate([slot, pad_key])
    vals = jnp.concatenate([jnp.arange(n, dtype=jnp.int32) % t, pad_val])
    tok_bits = max(1, (t - 1).bit_length())
    assert (n_slots + 1) << tok_bits < 2 ** 31
    packed = jnp.sort(lax.shift_left(keys, tok_bits) | vals)
    return packed[:n_slots] & ((1 << tok_bits) - 1)


def _route(expert_rank, counts_pad, n_slots):
    t = expert_rank.shape[1]
    flat_e = expert_rank[:TOP_K].reshape(TOP_K * t)
    onehot = (flat_e[:, None] == jnp.arange(N_EXPERTS, dtype=jnp.int32)[None, :]).astype(jnp.int32)
    rank = expert_rank[TOP_K:].reshape(TOP_K * t)
    counts = counts_pad[0, :N_EXPERTS]
    padded = (counts + MOE_BLOCK - 1) // MOE_BLOCK * MOE_BLOCK
    padded_end = jnp.cumsum(padded)
    padded_start = padded_end - padded
    slot = jnp.sum(onehot * padded_start[None, :], axis=1) + rank
    n_blocks = n_slots // MOE_BLOCK
    block_row0 = jnp.arange(n_blocks, dtype=jnp.int32) * MOE_BLOCK
    block_exp = jnp.minimum(
        jnp.sum((padded_end[None, :] <= block_row0[:, None]).astype(jnp.int32), axis=1),
        N_EXPERTS - 1).astype(jnp.int32)
    exp_hot = (block_exp[:, None] == jnp.arange(N_EXPERTS, dtype=jnp.int32)[None, :]).astype(jnp.int32)
    rows_end = jnp.sum(exp_hot * (padded_start + counts)[None, :], axis=1)
    block_valid = jnp.clip(rows_end - block_row0, 0, MOE_BLOCK).astype(jnp.int32)
    n_used = (padded_end[-1] // MOE_BLOCK).astype(jnp.int32)
    block_ids = jnp.arange(n_blocks, dtype=jnp.int32)
    prev_exp = jnp.concatenate([jnp.full((1,), -1, jnp.int32), block_exp[:-1]])
    first = ((block_exp != prev_exp) & (block_ids < n_used)).astype(jnp.int32)
    buf_slot = (jnp.cumsum(first) - 1) % 2
    experts = jnp.arange(N_EXPERTS, dtype=jnp.int32)
    holder = jnp.where(counts > 0, experts, N_EXPERTS)
    later = lax.cummin(jnp.concatenate([holder[1:], jnp.full((1,), N_EXPERTS, jnp.int32)]), reverse=True)
    next_exp = jnp.where(later < N_EXPERTS, later, -1)
    block_next = jnp.sum(exp_hot * next_exp[None, :], axis=1)
    plan = jnp.stack([block_exp, block_valid, first, buf_slot, block_next,
                      jnp.full((n_blocks,), n_used, jnp.int32)]).astype(jnp.int32)
    slot = slot.astype(jnp.int32)
    slot_tok = _slot_tokens(slot, counts, padded_start, padded_end, n_slots, t)
    return slot, slot_tok, plan


def _segment_matrices():
    m_q = np.zeros((QK_PAD, LANES), np.float32)
    m_k = np.zeros((QK_PAD, LANES), np.float32)
    inv_cnt_q = np.ones((1, LANES), np.float32)
    for h in range(N_HEADS):
        m_q[h * HEAD_PAD:h * HEAD_PAD + QK_NOPE, 2 * h] = 1.0
        m_q[h * HEAD_PAD + QK_NOPE:h * HEAD_PAD + QK_HEAD, 2 * h + 1] = 1.0
        inv_cnt_q[0, 2 * h] = 1.0 / QK_NOPE
        inv_cnt_q[0, 2 * h + 1] = 1.0 / ROPE_DIM
        m_k[h * HEAD_PAD:h * HEAD_PAD + QK_NOPE, h] = 1.0
    dup = lambda m: np.concatenate([m.T, m.T], axis=0)
    return dict(m_q=jnp.asarray(m_q, BF16), mt2_q=jnp.asarray(dup(m_q), BF16), inv_cnt_q=jnp.asarray(inv_cnt_q),
                m_k=jnp.asarray(m_k, BF16), mt2_k=jnp.asarray(dup(m_k), BF16))


def _rope_lane_tables(n_tokens, ident_rows):
    rows = n_tokens // GRID_W
    row = jnp.broadcast_to(jnp.arange(rows)[:, None], (rows, GRID_W)).reshape(n_tokens)
    col = jnp.broadcast_to(jnp.arange(GRID_W)[None, :], (rows, GRID_W)).reshape(n_tokens)
    n_freq = ROPE_DIM // 4
    inv = 1.0 / (ROPE_THETA ** (jnp.arange(n_freq, dtype=F32) / n_freq))
    ang = jnp.concatenate([row[:, None].astype(F32) * inv, col[:, None].astype(F32) * inv], axis=-1)
    cos, sin = jnp.cos(ang), jnp.sin(ang)
    ones = jnp.ones((n_tokens, QK_NOPE), F32)
    tail = HEAD_PAD - QK_HEAD
    cosf = jnp.concatenate([ones, cos, cos, jnp.ones((n_tokens, tail), F32)], axis=1)
    sinf = jnp.concatenate([0.0 * ones, -sin, sin, jnp.zeros((n_tokens, tail), F32)], axis=1)
    cosf = jnp.concatenate([cosf, jnp.ones((ident_rows, HEAD_PAD), F32)], axis=0)
    sinf = jnp.concatenate([sinf, jnp.zeros((ident_rows, HEAD_PAD), F32)], axis=0)
    return cosf, sinf


def _dft_tables(seq_len):
    def cs(n):
        k = np.arange(n, dtype=np.int64)
        ang = 2.0 * np.pi * ((k[:, None] * k[None, :]) % n).astype(np.float64) / n
        return np.cos(ang).astype(np.float32), np.sin(ang).astype(np.float32)

    cl, sl = cs(seq_len)
    cc, sc = cs(FNET_GC)
    return dict(cos=jnp.asarray(cl).astype(BF16), sin=jnp.asarray(sl).astype(BF16),
                chan=jnp.asarray(np.concatenate([cc, sc], axis=1)).astype(BF16))


def _layout_weights(w_in, w_q_b, w_kv_b, q_nope_g, q_rope_g, k_nope_g, k_rope_g, w_router, b_router):
    depth = w_in.shape[0]
    w_in_a = w_in[:, :, :COL_F].astype(BF16)
    w_in_b = w_in[:, :, COL_KPE + ROPE_DIM:].astype(BF16)
    w_q = jnp.pad(w_q_b.reshape(depth, Q_RANK, N_HEADS, QK_HEAD),
                  ((0, 0), (0, 0), (0, 0), (0, HEAD_PAD - QK_HEAD))).reshape(depth, Q_RANK, QK_PAD).astype(BF16)
    kv = w_kv_b.reshape(depth, KV_RANK, N_HEADS, QK_NOPE + V_HEAD)
    w_k = jnp.pad(kv[..., :QK_NOPE], ((0, 0), (0, 0), (0, 0), (0, HEAD_PAD - QK_NOPE))
                  ).reshape(depth, KV_RANK, QK_PAD).astype(BF16)
    w_v = kv[..., QK_NOPE:].reshape(depth, KV_RANK, V_WIDTH).astype(BF16)
    zq = jnp.zeros((depth, HEAD_PAD - QK_HEAD), F32)
    g_q = jnp.tile(jnp.concatenate([q_nope_g, q_rope_g, zq], axis=1), (1, N_HEADS))[:, None, :]
    g_k = jnp.tile(jnp.concatenate([k_nope_g, jnp.zeros((depth, HEAD_PAD - QK_NOPE), F32)], axis=1),
                   (1, N_HEADS))[:, None, :]
    g_kpe = jnp.concatenate([jnp.zeros((depth, QK_NOPE), F32), k_rope_g, zq], axis=1)[:, None, :]
    w_hi = w_router.astype(BF16)
    w_lo = (w_router - w_hi.astype(F32)).astype(BF16)
    w_r3 = jnp.pad(jnp.concatenate([w_hi, w_hi, w_lo], axis=1), ((0, 0), (0, 0), (0, ROUTER_PAD - N_EXPERTS)))
    b_r = jnp.pad(b_router, ((0, 0), (0, ROUTER_PAD - N_EXPERTS)), constant_values=NEG_BIG)[:, None, :]
    return dict(w_in_a=w_in_a, w_in_b=w_in_b, w_q=w_q, w_k=w_k, w_v=w_v, g_q=g_q, g_k=g_k, g_kpe=g_kpe, w_router3=w_r3, b_router=b_r)


def kernel(x_prompt, x_sample, cache_ckv, cache_kpe, c, c_ctx, w_mod, b_mod, norm1_g, norm2_g, w_in, q_a_g, kv_a_g, w_q_b, w_kv_b, q_nope_g, q_rope_g, k_nope_g, k_rope_g, w_attn_o, w_fnet_o, w_pool_grp, pool_scale, w_pool_o, w_out, w_router, b_router, w_gu, b_gu, w_down, b_down):
    batch, seq, d = x_prompt.shape
    dec_batch, dec_seq, _ = x_sample.shape
    depth = w_mod.shape[0]
    past = cache_ckv.shape[2]
    t_prompt = batch * seq
    t_sample = dec_batch * dec_seq
    assert d == D_MODEL and dec_batch + 1 <= 8
    assert t_prompt % TOKEN_TILE == 0 and dec_seq % TOKEN_TILE == 0 and seq % COMBINE_TILE == 0
    assert dec_seq % GRID_W == 0

    consts = _segment_matrices()
    consts["cosf"], consts["sinf"] = _rope_lane_tables(dec_seq, IN_TILE)
    consts["tri"] = jnp.asarray(np.tril(np.ones((TOKEN_TILE, TOKEN_TILE), np.float32), -1), BF16)

    wts = _layout_weights(w_in, w_q_b, w_kv_b, q_nope_g, q_rope_g, k_nope_g, k_rope_g, w_router, b_router)
    row3 = lambda a: a[:, None, :]
    wts.update(norm1_g=row3(norm1_g), norm2_g=row3(norm2_g), q_a_g=row3(q_a_g), kv_a_g=row3(kv_a_g),
               w_attn_o=w_attn_o.astype(BF16), w_fnet_o=w_fnet_o.astype(BF16), w_pool_o=w_pool_o.astype(BF16),
               w_out=w_out.astype(BF16), w_pool_grp=w_pool_grp.astype(BF16), pool_scale=row3(pool_scale))
    b_gu4 = b_gu[:, :, None, :]
    b_down4 = b_down[:, :, None, :]

    cond8 = jnp.concatenate([c_ctx[None, :], c, jnp.zeros((8 - 1 - dec_batch, d), F32)], axis=0)
    mods = _modulation(cond8, w_mod, b_mod).reshape(depth, 8, 6, D_MODEL)

    kpe_pad = jnp.pad(cache_kpe, ((0, 0), (0, 0), (0, 0), (QK_NOPE, HEAD_PAD - QK_HEAD)))
    k_ctx, v_ctx = _ctx_keys(cache_ckv, kpe_pad, wts, consts)

    streams = [
        dict(x=x_prompt.reshape(t_prompt, d), n_seq=batch, seq_len=seq, tok=dict(t_prompt=t_prompt, dec_seq=dec_seq),
             dft=_dft_tables(seq), ctx=None, moe=None),
        dict(x=x_sample.reshape(t_sample, d), n_seq=dec_batch, seq_len=dec_seq, tok=dict(t_prompt=0, dec_seq=dec_seq),
             dft=_dft_tables(dec_seq), ctx=(k_ctx, v_ctx, past), moe=None),
    ]
    ckv_list, kpe_list = [], []

    def mixers(st, l):
        moe = None if st["x"] is not None else (st["x1"], st["tw"], st["yg"])
        (q, k, v, ckv, kpe, f_in, p_in, st["gates"]), st["x"] = _in_projection(
            st["x"], moe, mods, l, wts, consts, st["tok"])
        if st["ctx"] is None:
            ckv_list.append(ckv.reshape(batch, seq, KV_RANK))
            kpe_list.append(kpe.reshape(batch, seq, ROPE_DIM))
        parts = [(k, v, None, st["seq_len"])]
        if st["ctx"] is not None:
            parts.append((st["ctx"][0], st["ctx"][1], l, st["ctx"][2]))
        st["attn"] = _attention(q, parts, st["n_seq"], st["seq_len"])
        st["fnet"] = _fourier(f_in, st["n_seq"], st["seq_len"], 0, st["dft"])
        st["pool"] = _pooling(p_in, st["n_seq"], st["seq_len"], 0, l, wts)

    def merge_and_route(st, l):
        t = st["n_seq"] * st["seq_len"]
        n_slots = (t * TOP_K + N_EXPERTS * (MOE_BLOCK - 1) + MOE_BLOCK - 1) // MOE_BLOCK * MOE_BLOCK
        st["x1"], st["h2p"], st["tw"], expert_rank, counts_pad = _merge(
            st["x"], mods, st["attn"], st["fnet"], st["pool"], st["gates"], l, wts, consts, st["tok"])
        st["slot_by_choice"], st["slot_tok"], st["plan"] = _route(expert_rank, counts_pad, n_slots)

    def dispatch(st):
        st["xs"] = _sc_row_gather(st["h2p"], st["slot_tok"])

    def experts(st, l):
        st["y"] = _experts(st["xs"], st["plan"], l, w_gu, b_gu4, w_down, b_down4)

    def gather_back(st):
        t = st["n_seq"] * st["seq_len"]
        st["yg"] = _sc_row_gather(st["y"], st["slot_by_choice"]).reshape(TOP_K, t, D_MODEL // 2)
        st["x"] = None

    def mixer_input(st):
        return "x" if st["x"] is not None else "yg"

    def tie(sa, ka, sb, kb):
        sa[ka], sb[kb] = lax.optimization_barrier((sa[ka], sb[kb]))

    a, b = streams
    mixers(a, 0)
    merge_and_route(a, 0)
    tie(b, mixer_input(b), a, "slot_tok")
    dispatch(a)
    mixers(b, 0)
    tie(a, "xs", b, "attn")
    experts(a, 0)
    for l in range(depth):
        tie(b, "x", a, "y")
        gather_back(a)
        merge_and_route(b, l)
        tie(a, "yg", b, "slot_tok")
        dispatch(b)
        if l + 1 == depth:
            break
        mixers(a, l + 1)
        tie(b, "xs", a, "attn")
        experts(b, l)
        tie(a, "x", b, "y")
        gather_back(b)
        merge_and_route(a, l + 1)
        tie(b, "yg", a, "slot_tok")
        dispatch(a)
        mixers(b, l + 1)
        tie(a, "xs", b, "attn")
        experts(a, l + 1)
    a["out"] = _combine(a["yg"], a["tw"], a["x1"], mods, depth - 1, a["tok"])
    tie(b, "xs", a, "out")
    experts(b, depth - 1)
    gather_back(b)
    out_a = a["out"]
    out_b = _combine(b["yg"], b["tw"], b["x1"], mods, depth - 1, b["tok"])
    y_prompt = out_a.reshape(batch, seq, d)
    y_sample = out_b.reshape(dec_batch, dec_seq, d)
    return (y_prompt, y_sample, jnp.stack(ckv_list, axis=1), jnp.stack(kpe_list, axis=1))
```

```python
import functools
import math

import numpy as np
import jax
import jax.numpy as jnp
from jax import lax
from jax.experimental import pallas as pl
from jax.experimental.pallas import tpu as pltpu
from jax.experimental.pallas import tpu_sc as plsc

D_MODEL = 1024
GRID_W = 64
N_HEADS = 8
QK_NOPE = 64
ROPE_DIM = 32
V_HEAD = 64
QK_HEAD = QK_NOPE + ROPE_DIM
Q_RANK = 384
KV_RANK = 256
ROPE_THETA = 10000.0
FNET_GROUPS = 4
FNET_GC = 128
FNET_WIDTH = FNET_GROUPS * FNET_GC
POOL_WINDOWS = (2, 4, 8, 16)
POOL_GC = 128
POOL_WIDTH = len(POOL_WINDOWS) * POOL_GC
N_BRANCHES = 3
N_EXPERTS = 32
TOP_K = 4
D_FF = D_MODEL
SWIGLU_LIMIT = 7.0
SWIGLU_ALPHA = 1.702
RMS_EPS = 1e-6

LANES = 128
HEAD_PAD = LANES
QK_PAD = N_HEADS * HEAD_PAD
V_WIDTH = N_HEADS * V_HEAD
ROPE_HALF = ROPE_DIM // 2
KPE_PAD = LANES
COL_KV = Q_RANK
COL_KPE = Q_RANK + KV_RANK
COL_F = COL_KPE + KPE_PAD
COL_P = COL_F + FNET_WIDTH
COL_G = COL_P + POOL_WIDTH
IN_PAD_WIDTH = COL_G + N_BRANCHES * D_MODEL
MIX_WIDTH = FNET_WIDTH + POOL_WIDTH
ROUTER_PAD = LANES
NEG_BIG = -1e30

TOKEN_TILE = 512
IN_TILE = 512
ATTN_Q_TILE = 512
SHORT_SEQ_ROWS = 1024
FNET_ROW_TILE = 1024
POOL_CHUNK = 256
POOL_HALO = 16
MOE_BLOCK = 512
MOE_STEP = 64
GATE_COL_CHUNK = 512
MERGE_COL_CHUNK = 256
COMBINE_TILE = 256
SC_GATHER_ROWS = 64
MOD_COL_TILE = 1536
VMEM_LIMIT = 56 * 1024 * 1024

F32 = jnp.float32
BF16 = jnp.bfloat16


def _cparams(*sem):
    return pltpu.CompilerParams(dimension_semantics=sem, vmem_limit_bytes=VMEM_LIMIT)


def _dot(a, b):
    return jnp.dot(a, b, preferred_element_type=F32)


def _rms(x):
    return x * lax.rsqrt(jnp.mean(x * x, axis=-1, keepdims=True) + RMS_EPS)


def _pack_bf16_pairs(x):
    half = x.shape[1] // 2
    bits = pltpu.bitcast(x.astype(BF16).astype(F32), jnp.int32)
    return bits[:, :half] | lax.shift_right_logical(bits[:, half:], 16)


def _unpack_bf16_pairs(w):
    return pltpu.bitcast(w & jnp.int32(-65536), F32), pltpu.bitcast(lax.shift_left(w, 16), F32)


def _seg_sums(x, m_ref):
    return _dot((x * x).astype(BF16), m_ref[...])


def _seg_inv_rms(ss, inv_cnt, mt2_ref):
    r = lax.rsqrt(ss * inv_cnt + RMS_EPS)
    r_hi = r.astype(BF16)
    r_lo = (r - r_hi.astype(F32)).astype(BF16)
    return _dot(jnp.concatenate([r_hi, r_lo], axis=1), mt2_ref[...])


def _seg_rms_scale(x, m_ref, mt2_ref, inv_cnt):
    return _seg_inv_rms(_seg_sums(x, m_ref), inv_cnt, mt2_ref)


def _rope_key_chunk(kpe_chunk, gkpe_ref, rope):
    ms = jnp.sum(kpe_chunk * kpe_chunk, axis=-1, keepdims=True) * (1.0 / ROPE_DIM)
    kp = kpe_chunk * lax.rsqrt(ms + RMS_EPS) * gkpe_ref[...]
    return kp if rope is None else _rope_chunk(kp, *rope)


def _rope_chunk(xc, cosf, sinf, first_half):
    partner = jnp.where(first_half, pltpu.roll(xc, HEAD_PAD - ROPE_HALF, 1), pltpu.roll(xc, ROPE_HALF, 1))
    return xc * cosf + partner * sinf


def _write_keys(ckv, kpe_chunk, wk_ref, wv_ref, gk_ref, gkpe_ref, mk_ref, mkt2_ref, rope, k_out, v_out):
    cb = ckv.astype(BF16)
    kn = _dot(cb, wk_ref[...])
    kn = kn * _seg_rms_scale(kn, mk_ref, mkt2_ref, 1.0 / QK_NOPE) * gk_ref[...]
    v_out[...] = _dot(cb, wv_ref[...]).astype(v_out.dtype)
    kp = _rope_key_chunk(kpe_chunk, gkpe_ref, rope)
    for h in range(N_HEADS):
        sl = slice(h * HEAD_PAD, (h + 1) * HEAD_PAD)
        k_out[:, sl] = (kn[:, sl] + kp).astype(k_out.dtype)


def _mod_kernel(c_ref, w_ref, b_ref, o_ref):
    c = c_ref[...]
    s = c * jax.nn.sigmoid(c)
    o_ref[...] = _dot(s.astype(BF16), w_ref[...].astype(BF16)) + b_ref[...]


def _modulation(cond8, w_mod, b_mod):
    depth = w_mod.shape[0]
    n = w_mod.shape[2]
    return pl.pallas_call(
        _mod_kernel,
        out_shape=jax.ShapeDtypeStruct((depth, 8, n), F32),
        grid=(depth, n // MOD_COL_TILE),
        in_specs=[
            pl.BlockSpec((8, D_MODEL), lambda l, j: (0, 0)),
            pl.BlockSpec((None, D_MODEL, MOD_COL_TILE), lambda l, j: (l, 0, j)),
            pl.BlockSpec((None, 1, MOD_COL_TILE), lambda l, j: (l, 0, j)),
        ],
        out_specs=pl.BlockSpec((None, 8, MOD_COL_TILE), lambda l, j: (l, 0, j)),
        compiler_params=_cparams("arbitrary", "arbitrary"),
        name="adaln_modulation",
    )(cond8, w_mod, b_mod.reshape(depth, 1, n))


def _moe_residual(tw_ref, x1_ref, mod_ref, yg_ref):
    acc = None
    for k in range(TOP_K):
        w = tw_ref[:, k:k + 1]
        part = [w * half for half in _unpack_bf16_pairs(yg_ref[k])]
        acc = part if acc is None else [a + p for a, p in zip(acc, part)]
    return x1_ref[...] + mod_ref[5:6, :] * jnp.concatenate(acc, axis=1)


def _in_kernel(*refs, after_moe):
    if after_moe:
        x = _moe_residual(*refs[:4])
        refs = refs[4:]
        refs[-1][...] = x
        refs = refs[:-1]
    else:
        x = refs[0][...]
        refs = refs[1:]
    (mod_ref, g1_ref, wa_ref, wb_ref, qag_ref, kvag_ref, wq_ref, wk_ref, wv_ref,
     gq_ref, gk_ref, gkpe_ref, cos_ref, sin_ref, mq_ref, mqt2_ref, icq_ref, mk_ref, mkt2_ref,
     q_out, k_out, v_out, ckv_out, kpe_out, f_out, p_out, g_out) = refs
    shift = mod_ref[0:1, :]
    scale = mod_ref[1:2, :]
    h = _rms(x) * g1_ref[...] * (1.0 + scale) + shift
    hb = h.astype(BF16)

    gate_chunks = iter(range(0, N_BRANCHES * D_MODEL, GATE_COL_CHUNK))

    def gate_chunk():
        c0 = next(gate_chunks)
        zc = _dot(hb, wb_ref[:, MIX_WIDTH + c0:MIX_WIDTH + c0 + GATE_COL_CHUNK])
        g_out[:, c0:c0 + GATE_COL_CHUNK] = jax.nn.sigmoid(zc).astype(g_out.dtype)

    za = _dot(hb, wa_ref[...])
    zb = _dot(hb, wb_ref[:, 0:MIX_WIDTH])
    f_out[...] = zb[:, 0:FNET_WIDTH].astype(f_out.dtype)
    p_out[...] = zb[:, FNET_WIDTH:]
    q_lat = za[:, 0:COL_KV]
    kv_lat = za[:, COL_KV:COL_KPE]
    key_lane = lax.broadcasted_iota(jnp.int32, (hb.shape[0], KPE_PAD), 1) < ROPE_DIM
    kpe_grp = jnp.where(key_lane, za[:, COL_KPE:COL_F], 0.0)
    ckv = _rms(kv_lat) * kvag_ref[...]
    ckv_out[...] = ckv
    kpe_out[...] = kpe_grp[:, 0:ROPE_DIM]
    qn = (_rms(q_lat) * qag_ref[...]).astype(BF16)
    cb = ckv.astype(BF16)
    gate_chunk()

    q = _dot(qn, wq_ref[...])
    kn = _dot(cb, wk_ref[...])
    v_out[...] = _dot(cb, wv_ref[...]).astype(v_out.dtype)
    gate_chunk()

    ss_q = _seg_sums(q, mq_ref)
    ss_k = _seg_sums(kn, mk_ref)
    gate_chunk()

    q = q * _seg_inv_rms(ss_q, icq_ref[...], mqt2_ref) * gq_ref[...]
    kn = kn * _seg_inv_rms(ss_k, 1.0 / QK_NOPE, mkt2_ref) * gk_ref[...]
    for _ in range(N_BRANCHES * D_MODEL // GATE_COL_CHUNK - 3):
        gate_chunk()

    cosf = cos_ref[...]
    sinf = sin_ref[...]
    lane = lax.broadcasted_iota(jnp.int32, cosf.shape, 1)
    rope = (cosf, sinf, lane < QK_NOPE + ROPE_HALF)
    sm_scale = 1.0 / math.sqrt(QK_HEAD)
    kp = _rope_key_chunk(pltpu.roll(kpe_grp, QK_NOPE, 1), gkpe_ref, rope)
    for hd in range(N_HEADS):
        sl = slice(hd * HEAD_PAD, (hd + 1) * HEAD_PAD)
        q_out[:, sl] = (_rope_chunk(q[:, sl], *rope) * sm_scale).astype(q_out.dtype)
        k_out[:, sl] = (kn[:, sl] + kp).astype(k_out.dtype)


def _const_spec(shape):
    nd = len(shape)
    return pl.BlockSpec(shape, lambda i, _n=nd: (0,) * _n, pipeline_mode=pl.Buffered(1))


def _layer_spec(shape, l):
    nd = len(shape)
    return pl.BlockSpec((None,) + tuple(shape), lambda i, _l=l, _n=nd: (_l,) + (0,) * _n,
                        pipeline_mode=pl.Buffered(1))


def _in_projection(x, moe, mods, l, wts, consts, tok):
    after_moe = x is None
    t = moe[0].shape[0] if after_moe else x.shape[0]
    tm = IN_TILE
    n_tiles = t // tm
    p_tiles = tok["t_prompt"] // tm
    s_tiles = tok["dec_seq"] // tm
    rope_blocks = tok["dec_seq"] // tm

    def mod_row(i):
        return jnp.where(i < p_tiles, 0, 1 + (i - p_tiles) // s_tiles)

    def rope_idx(i):
        return (jnp.where(i < p_tiles, rope_blocks, (i - p_tiles) % s_tiles), 0)

    row = lambda w: pl.BlockSpec((tm, w), lambda i: (i, 0))
    if after_moe:
        x1, tw, yg = moe
        lead_specs = [row(ROUTER_PAD), row(D_MODEL),
                      pl.BlockSpec((None, None, 6, D_MODEL), lambda i: (l - 1, mod_row(i), 0, 0)),
                      pl.BlockSpec((TOP_K, tm, D_MODEL // 2), lambda i: (0, i, 0))]
        lead_args = [tw, x1, mods, yg]
    else:
        lead_specs = [row(D_MODEL)]
        lead_args = [x]
    in_specs = lead_specs + [
        pl.BlockSpec((None, None, 6, D_MODEL), lambda i: (l, mod_row(i), 0, 0)),
        _layer_spec((1, D_MODEL), l),
        _layer_spec((D_MODEL, COL_F), l),
        _layer_spec((D_MODEL, MIX_WIDTH + N_BRANCHES * D_MODEL), l),
        _layer_spec((1, Q_RANK), l),
        _layer_spec((1, KV_RANK), l),
        _layer_spec((Q_RANK, QK_PAD), l),
        _layer_spec((KV_RANK, QK_PAD), l),
        _layer_spec((KV_RANK, V_WIDTH), l),
        _layer_spec((1, QK_PAD), l),
        _layer_spec((1, QK_PAD), l),
        _layer_spec((1, HEAD_PAD), l),
        pl.BlockSpec((tm, HEAD_PAD), rope_idx),
        pl.BlockSpec((tm, HEAD_PAD), rope_idx),
        _const_spec((QK_PAD, LANES)),
        _const_spec((2 * LANES, QK_PAD)),
        _const_spec((1, LANES)),
        _const_spec((QK_PAD, LANES)),
        _const_spec((2 * LANES, QK_PAD)),
    ]
    out_shape = [
        jax.ShapeDtypeStruct((t, QK_PAD), BF16),
        jax.ShapeDtypeStruct((t, QK_PAD), BF16),
        jax.ShapeDtypeStruct((t, V_WIDTH), BF16),
        jax.ShapeDtypeStruct((t, KV_RANK), F32),
        jax.ShapeDtypeStruct((t, ROPE_DIM), F32),
        jax.ShapeDtypeStruct((t, FNET_WIDTH), BF16),
        jax.ShapeDtypeStruct((t, POOL_WIDTH), F32),
        jax.ShapeDtypeStruct((t, N_BRANCHES * D_MODEL), BF16),
    ]
    if after_moe:
        out_shape.append(jax.ShapeDtypeStruct((t, D_MODEL), F32))
    out_specs = [row(s.shape[1]) for s in out_shape]
    outs = pl.pallas_call(
        functools.partial(_in_kernel, after_moe=after_moe),
        out_shape=out_shape,
        grid=(n_tiles,),
        in_specs=in_specs,
        out_specs=out_specs,
        compiler_params=_cparams("arbitrary"),
        cost_estimate=pl.CostEstimate(
            flops=2 * t * (D_MODEL * IN_PAD_WIDTH + Q_RANK * QK_PAD + KV_RANK * (QK_PAD + V_WIDTH)
                           + 2 * (QK_PAD * LANES + 2 * LANES * QK_PAD)),
            transcendentals=t * N_BRANCHES * D_MODEL,
            bytes_accessed=2 * D_MODEL * IN_PAD_WIDTH + sum(
                math.prod(s.shape) * s.dtype.itemsize for s in out_shape)
            + 4 * t * D_MODEL * (1 + (TOP_K // 2 if after_moe else 0))),
        name="in_projection",
    )(*lead_args, mods, wts["norm1_g"], wts["w_in_a"], wts["w_in_b"], wts["q_a_g"], wts["kv_a_g"], wts["w_q"], wts["w_k"], wts["w_v"],
      wts["g_q"], wts["g_k"], wts["g_kpe"], consts["cosf"], consts["sinf"],
      consts["m_q"], consts["mt2_q"], consts["inv_cnt_q"], consts["m_k"], consts["mt2_k"])
    return (outs[:8], outs[8]) if after_moe else (outs, x)


def _ctx_keys_kernel(ckv_ref, kpe_ref, wk_ref, wv_ref, gk_ref, gkpe_ref, mk_ref, mkt2_ref, k_out, v_out):
    _write_keys(ckv_ref[...], kpe_ref[...], wk_ref, wv_ref, gk_ref, gkpe_ref, mk_ref, mkt2_ref, None, k_out, v_out)


def _ctx_keys(cache_ckv, cache_kpe_pad, wts, consts):
    nb, depth, past, _ = cache_ckv.shape
    lw = lambda shape: pl.BlockSpec((None,) + shape, lambda l, b: (l,) + (0,) * len(shape))
    cs = lambda shape: pl.BlockSpec(shape, lambda l, b: (0,) * len(shape))
    return pl.pallas_call(
        _ctx_keys_kernel,
        out_shape=[jax.ShapeDtypeStruct((depth, nb * past, QK_PAD), BF16),
                   jax.ShapeDtypeStruct((depth, nb * past, V_WIDTH), BF16)],
        grid=(depth, nb),
        in_specs=[
            pl.BlockSpec((None, None, past, KV_RANK), lambda l, b: (b, l, 0, 0)),
            pl.BlockSpec((None, None, past, HEAD_PAD), lambda l, b: (b, l, 0, 0)),
            lw((KV_RANK, QK_PAD)), lw((KV_RANK, V_WIDTH)), lw((1, QK_PAD)), lw((1, HEAD_PAD)),
            cs((QK_PAD, LANES)), cs((2 * LANES, QK_PAD)),
        ],
        out_specs=[pl.BlockSpec((None, past, QK_PAD), lambda l, b: (l, b, 0)),
                   pl.BlockSpec((None, past, V_WIDTH), lambda l, b: (l, b, 0))],
        compiler_params=_cparams("arbitrary", "arbitrary"),
        name="context_keys",
    )(cache_ckv, cache_kpe_pad, wts["w_k"], wts["w_v"], wts["g_k"], wts["g_kpe"], consts["m_k"], consts["mt2_k"])


def _attn_kernel(*refs, n_parts, seqs):
    q_ref = refs[0]
    k_refs = refs[1:1 + 2 * n_parts:2]
    v_refs = refs[2:2 + 2 * n_parts:2]
    o_ref = refs[1 + 2 * n_parts]
    q_rows = q_ref.shape[0] // seqs
    lane = lax.broadcasted_iota(jnp.int32, (q_rows, 2 * V_HEAD), 1)
    for s in range(seqs):
        rows = slice(s * q_rows, (s + 1) * q_rows)
        keys = [slice(s * (k.shape[0] // seqs), (s + 1) * (k.shape[0] // seqs)) for k in k_refs]
        for pair in range(N_HEADS // 2):
            vsl = slice(pair * 2 * V_HEAD, (pair + 1) * 2 * V_HEAD)
            outs = []
            for hd in (2 * pair, 2 * pair + 1):
                sl = slice(hd * HEAD_PAD, (hd + 1) * HEAD_PAD)
                qh = q_ref[rows, sl]
                ss = [lax.dot_general(qh, k[ks, sl], (((1,), (1,)), ((), ())), preferred_element_type=F32)
                      for k, ks in zip(k_refs, keys)]
                m = functools.reduce(jnp.maximum, [jnp.max(sc, axis=-1, keepdims=True) for sc in ss])
                es = [jnp.exp(sc - m) for sc in ss]
                den = functools.reduce(jnp.add, [jnp.sum(e, axis=-1, keepdims=True) for e in es])
                acc = functools.reduce(jnp.add, [_dot(e.astype(BF16), v[ks, vsl])
                                                 for e, v, ks in zip(es, v_refs, keys)])
                outs.append(acc / den)
            o_ref[rows, vsl] = jnp.where(lane < V_HEAD, outs[0], outs[1]).astype(o_ref.dtype)


def _attention(q, kv_parts, n_seq, seq_len):
    tq = min(ATTN_Q_TILE, seq_len)
    nq = seq_len // tq
    n_keys = sum(rows for _, _, _, rows in kv_parts)
    seqs = math.gcd(n_seq, max(1, SHORT_SEQ_ROWS // seq_len)) if nq == 1 and len(kv_parts) == 1 else 1
    in_specs = [pl.BlockSpec((seqs * tq, QK_PAD), lambda b, i: (b * nq + i, 0))]
    args = [q]
    for k, v, layer, rows in kv_parts:
        for arr, width in ((k, QK_PAD), (v, V_WIDTH)):
            if layer is None:
                in_specs.append(pl.BlockSpec((seqs * rows, width), lambda b, i: (b, 0)))
            else:
                in_specs.append(pl.BlockSpec((None, rows, width), lambda b, i, _l=layer: (_l, b, 0)))
            args.append(arr)
    return pl.pallas_call(
        functools.partial(_attn_kernel, n_parts=len(kv_parts), seqs=seqs),
        out_shape=jax.ShapeDtypeStruct((n_seq * seq_len, V_WIDTH), BF16),
        grid=(n_seq // seqs, nq),
        in_specs=in_specs,
        out_specs=pl.BlockSpec((seqs * tq, V_WIDTH), lambda b, i: (b * nq + i, 0)),
        compiler_params=_cparams("arbitrary", "arbitrary"),
        cost_estimate=pl.CostEstimate(
            flops=2 * n_seq * seq_len * n_keys * N_HEADS * (HEAD_PAD + 2 * V_HEAD),
            transcendentals=n_seq * seq_len * n_keys * N_HEADS,
            bytes_accessed=2 * n_seq * (seq_len * (QK_PAD + V_WIDTH) + n_keys * (QK_PAD + V_WIDTH))),
        name="attention",
    )(*args)


def _fnet_kernel(f_ref, cs_ref, cl_ref, sl_ref, o_ref, top_ref, bot_ref, *, norm, seqs):
    @pl.when(pl.program_id(1) == 0)
    def _():
        for g in range(FNET_GROUPS):
            sl = slice(g * FNET_GC, (g + 1) * FNET_GC)
            a = _dot(f_ref[:, sl], cs_ref[...])
            top_ref[:, sl] = a[:, :FNET_GC].astype(BF16)
            bot_ref[:, sl] = a[:, FNET_GC:].astype(BF16)

    seq_len = f_ref.shape[0] // seqs
    rows = o_ref.shape[0] // seqs
    for s in range(seqs):
        src = slice(s * seq_len, (s + 1) * seq_len)
        y = _dot(cl_ref[...], top_ref[src, :]) - _dot(sl_ref[...], bot_ref[src, :])
        o_ref[s * rows:(s + 1) * rows, :] = (y * norm).astype(o_ref.dtype)


def _fourier(f_in, n_seq, seq_len, row0, tabs):
    tr = min(FNET_ROW_TILE, seq_len)
    nj = seq_len // tr
    seqs = math.gcd(n_seq, max(1, SHORT_SEQ_ROWS // seq_len)) if nj == 1 else 1
    sb0 = row0 // (seqs * seq_len)
    return pl.pallas_call(
        functools.partial(_fnet_kernel, norm=1.0 / math.sqrt(seq_len * FNET_GC), seqs=seqs),
        out_shape=jax.ShapeDtypeStruct((n_seq * seq_len, FNET_WIDTH), BF16),
        grid=(n_seq // seqs, nj),
        in_specs=[
            pl.BlockSpec((seqs * seq_len, FNET_WIDTH), lambda b, j: (sb0 + b, 0)),
            pl.BlockSpec((FNET_GC, 2 * FNET_GC), lambda b, j: (0, 0)),
            pl.BlockSpec((tr, seq_len), lambda b, j: (j, 0)),
            pl.BlockSpec((tr, seq_len), lambda b, j: (j, 0)),
        ],
        out_specs=pl.BlockSpec((seqs * tr, FNET_WIDTH), lambda b, j: (b * nj + j, 0)),
        scratch_shapes=[pltpu.VMEM((seqs * seq_len, FNET_WIDTH), BF16),
                        pltpu.VMEM((seqs * seq_len, FNET_WIDTH), BF16)],
        compiler_params=_cparams("arbitrary", "arbitrary"),
        name="fourier_mix",
    )(f_in, tabs["chan"], tabs["cos"], tabs["sin"])


def _pool_kernel(p_ref, wg_ref, ps_ref, o_ref, pad_ref, *, seqs):
    seq_len = p_ref.shape[0] // seqs
    zeros = jnp.zeros((POOL_HALO, POOL_WIDTH), F32)
    ch = min(POOL_CHUNK, seq_len)
    for s in range(seqs):
        s0 = s * seq_len
        pad_ref[s, 0:POOL_HALO, :] = zeros
        pad_ref[s, POOL_HALO + seq_len:, :] = zeros
        pad_ref[s, POOL_HALO:POOL_HALO + seq_len, :] = p_ref[s0:s0 + seq_len, :]
        for c in range(seq_len // ch):
            t = lax.broadcasted_iota(jnp.int32, (ch, 1), 0) + c * ch
            for g, w in enumerate(POOL_WINDOWS):
                half = w // 2
                sl = slice(g * POOL_GC, (g + 1) * POOL_GC)
                acc = None
                for j in range(-half, half):
                    r0 = POOL_HALO + c * ch + j
                    part = pad_ref[s, r0:r0 + ch, sl]
                    acc = part if acc is None else acc + part
                cnt = (jnp.minimum(t + half, seq_len) - jnp.maximum(t - half, 0)).astype(F32)
                pooled = acc / cnt - p_ref[s0 + c * ch:s0 + (c + 1) * ch, sl]
                mixed = _dot(pooled.astype(BF16), wg_ref[g]) * ps_ref[:, sl]
                o_ref[s0 + c * ch:s0 + (c + 1) * ch, sl] = mixed.astype(o_ref.dtype)


def _pooling(p_in, n_seq, seq_len, row0, l, wts):
    seqs = math.gcd(n_seq, max(1, SHORT_SEQ_ROWS // seq_len))
    sb0 = row0 // (seqs * seq_len)
    g = len(POOL_WINDOWS)
    return pl.pallas_call(
        functools.partial(_pool_kernel, seqs=seqs),
        out_shape=jax.ShapeDtypeStruct((n_seq * seq_len, POOL_WIDTH), BF16),
        grid=(n_seq // seqs,),
        in_specs=[
            pl.BlockSpec((seqs * seq_len, POOL_WIDTH), lambda b: (sb0 + b, 0)),
            pl.BlockSpec((None, g, POOL_GC, POOL_GC), lambda b: (l, 0, 0, 0)),
            pl.BlockSpec((None, 1, POOL_WIDTH), lambda b: (l, 0, 0)),
        ],
        out_specs=pl.BlockSpec((seqs * seq_len, POOL_WIDTH), lambda b: (b, 0)),
        scratch_shapes=[pltpu.VMEM((seqs, seq_len + 2 * POOL_HALO, POOL_WIDTH), F32)],
        compiler_params=_cparams("arbitrary"),
        name="pool_mix",
    )(p_in, wts["w_pool_grp"], wts["pool_scale"])


def _merge_kernel(x_ref, mod_ref, a_ref, f_ref, p_ref, g_ref,
                  wa_ref, wf_ref, wp_ref, wo_ref, g2_ref, wr_ref, br_ref, tri_ref,
                  x1_out, hp_out, tw_out, er_out, cnt_out, mix_scr, carry_ref):
    j = pl.program_id(0)

    @pl.when(j == 0)
    def _():
        mix_scr[1] = jnp.zeros(mix_scr.shape[1:], F32)

    @pl.when(j <= 1)
    def _():
        carry_ref[...] = jnp.zeros(carry_ref.shape, F32)

    def branch_chunks(lo, hi):
        out = []
        for c0 in range(lo, hi, MERGE_COL_CHUNK):
            cs = slice(c0, c0 + MERGE_COL_CHUNK)
            a = _dot(a_ref[...], wa_ref[:, cs])
            f = _dot(f_ref[...], wf_ref[:, cs])
            p = _dot(p_ref[...], wp_ref[:, cs])
            out.append((g_ref[:, c0:c0 + MERGE_COL_CHUNK].astype(F32) * a
                        + g_ref[:, D_MODEL + c0:D_MODEL + c0 + MERGE_COL_CHUNK].astype(F32) * f
                        + g_ref[:, 2 * D_MODEL + c0:2 * D_MODEL + c0 + MERGE_COL_CHUNK].astype(F32) * p
                        ).astype(BF16))
        return out

    def step(cur, prev):
        chunks = branch_chunks(0, D_MODEL // 2)

        gate1 = mod_ref[2:3, :]
        shift2 = mod_ref[3:4, :]
        scale2 = mod_ref[4:5, :]
        x1 = x_ref[...] + gate1 * mix_scr[prev]
        x1_out[...] = x1
        h2 = _rms(x1) * g2_ref[...] * (1.0 + scale2) + shift2
        hp_out[...] = _pack_bf16_pairs(h2)
        h_hi = h2.astype(BF16)
        h_lo = (h2 - h_hi.astype(F32)).astype(BF16)
        logits = _dot(jnp.concatenate([h_hi, h_lo, h_hi], axis=1), wr_ref[...]) + br_ref[...]

        chunks += branch_chunks(D_MODEL // 2, D_MODEL)
        mix_scr[cur] = _dot(jnp.concatenate(chunks, axis=1), wo_ref[...])

        lane = lax.broadcasted_iota(jnp.int32, logits.shape, 1)
        work = logits
        vals, idxs = [], []
        for _ in range(TOP_K):
            m = jnp.max(work, axis=-1, keepdims=True)
            idx = jnp.min(jnp.where(work == m, lane, ROUTER_PAD), axis=-1, keepdims=True)
            vals.append(m)
            idxs.append(idx)
            work = jnp.where(lane == idx, -jnp.inf, work)
        es = [jnp.exp(v - vals[0]) for v in vals]
        den = functools.reduce(jnp.add, es)

        chosen = functools.reduce(jnp.logical_or, [lane == idx for idx in idxs])
        hot = jnp.where(chosen, 1.0, 0.0)
        before = _dot(tri_ref[...], hot.astype(BF16)) + carry_ref[...]
        carry_ref[...] = carry_ref[...] + jnp.sum(hot, axis=0, keepdims=True)
        cnt_out[...] = carry_ref[...].astype(jnp.int32)

        er = jnp.zeros(logits.shape, jnp.int32)
        tw = jnp.zeros(logits.shape, F32)
        for k in range(TOP_K):
            rank_k = jnp.sum(jnp.where(lane == idxs[k], before, 0.0), axis=-1, keepdims=True).astype(jnp.int32)
            er = jnp.where(lane == k, idxs[k], er)
            er = jnp.where(lane == TOP_K + k, rank_k, er)
            tw = jnp.where(lane == k, es[k] / den, tw)
        tw_out[...] = tw
        er_out[...] = jnp.transpose(er)[0:2 * TOP_K, :]

    for parity in (0, 1):
        pl.when(j % 2 == parity)(functools.partial(step, parity, 1 - parity))


def _merge(x, mods, attn, fnet, pool, gates, l, wts, consts, tok):
    t = x.shape[0]
    tm = TOKEN_TILE
    p_tiles = tok["t_prompt"] // tm
    s_tiles = tok["dec_seq"] // tm

    n_tiles = t // tm
    routed = lambda j: jnp.maximum(j - 1, 0)
    mixed = lambda j: jnp.minimum(j, n_tiles - 1)

    def mod_idx(j):
        i = routed(j)
        row = jnp.where(i < p_tiles, 0, 1 + (i - p_tiles) // s_tiles)
        return (l, row, 0, 0)

    row = lambda w: pl.BlockSpec((tm, w), lambda j: (routed(j), 0))
    head = lambda w: pl.BlockSpec((tm, w), lambda j: (mixed(j), 0))
    out_shape = [
        jax.ShapeDtypeStruct((t, D_MODEL), F32),
        jax.ShapeDtypeStruct((t, D_MODEL // 2), jnp.int32),
        jax.ShapeDtypeStruct((t, ROUTER_PAD), F32),
    ]
    return pl.pallas_call(
        _merge_kernel,
        out_shape=out_shape + [jax.ShapeDtypeStruct((2 * TOP_K, t), jnp.int32),
                               jax.ShapeDtypeStruct((1, ROUTER_PAD), jnp.int32)],
        grid=(n_tiles + 1,),
        in_specs=[
            row(D_MODEL),
            pl.BlockSpec((None, None, 6, D_MODEL), mod_idx),
            head(V_WIDTH), head(FNET_WIDTH), head(POOL_WIDTH), head(N_BRANCHES * D_MODEL),
            _layer_spec((V_WIDTH, D_MODEL), l),
            _layer_spec((FNET_WIDTH, D_MODEL), l),
            _layer_spec((POOL_WIDTH, D_MODEL), l),
            _layer_spec((D_MODEL, D_MODEL), l),
            _layer_spec((1, D_MODEL), l),
            _layer_spec((3 * D_MODEL, ROUTER_PAD), l),
            _layer_spec((1, ROUTER_PAD), l),
            _const_spec((tm, tm)),
        ],
        out_specs=[row(s.shape[1]) for s in out_shape] + [
            pl.BlockSpec((2 * TOP_K, tm), lambda j: (0, routed(j))),
            pl.BlockSpec((1, ROUTER_PAD), lambda j: (0, 0))],
        scratch_shapes=[pltpu.VMEM((2, tm, D_MODEL), F32), pltpu.VMEM((1, ROUTER_PAD), F32)],
        compiler_params=_cparams("arbitrary"),
        cost_estimate=pl.CostEstimate(
            flops=2 * t * (D_MODEL * (V_WIDTH + FNET_WIDTH + POOL_WIDTH + D_MODEL) + 3 * D_MODEL * ROUTER_PAD
                           + tm * ROUTER_PAD),
            transcendentals=t * TOP_K,
            bytes_accessed=t * (2 * (V_WIDTH + FNET_WIDTH + POOL_WIDTH + N_BRANCHES * D_MODEL) + 4 * 2 * D_MODEL
                                + 4 * (D_MODEL // 2) + 3 * 4 * ROUTER_PAD)
            + 2 * D_MODEL * (V_WIDTH + FNET_WIDTH + POOL_WIDTH + D_MODEL + 3 * ROUTER_PAD)),
        name="merge_router",
    )(x, mods, attn, fnet, pool, gates, wts["w_attn_o"], wts["w_fnet_o"], wts["w_pool_o"], wts["w_out"],
      wts["norm2_g"], wts["w_router3"], wts["b_router"], consts["tri"])


def _sc_row_gather(table, idx):
    n = idx.shape[0]
    d = table.shape[1]
    info = plsc.get_sparse_core_info()
    n_cores = info.num_cores
    n_workers = n_cores * info.num_subcores
    per_worker = n // n_workers
    n_chunks = per_worker // SC_GATHER_ROWS
    assert n_chunks * SC_GATHER_ROWS * n_workers == n
    mesh = plsc.VectorSubcoreMesh(core_axis_name="core", subcore_axis_name="subcore")

    @functools.partial(
        pl.kernel, mesh=mesh, out_type=jax.ShapeDtypeStruct((n, d), table.dtype),
        scratch_types=[pltpu.VMEM((SC_GATHER_ROWS,), jnp.int32), pltpu.VMEM((SC_GATHER_ROWS, d), table.dtype),
                       pltpu.SemaphoreType.DMA],
        cost_estimate=pl.CostEstimate(flops=0, transcendentals=0,
                                      bytes_accessed=2 * n * d * table.dtype.itemsize + 4 * n),
        name="sc_row_gather")
    def gather(table_hbm, idx_hbm, out_hbm, idx_v, rows_v, sem):
        worker = lax.axis_index("subcore") * n_cores + lax.axis_index("core")

        @pl.loop(0, n_chunks)
        def _(c):
            base = worker * per_worker + c * SC_GATHER_ROWS
            pltpu.sync_copy(idx_hbm.at[pl.ds(base, SC_GATHER_ROWS)], idx_v)
            pltpu.async_copy(table_hbm.at[idx_v], rows_v, sem).wait()
            pltpu.sync_copy(rows_v, out_hbm.at[pl.ds(base, SC_GATHER_ROWS)])

    return gather(table, idx)


PLAN_EXPERT, PLAN_VALID, PLAN_FIRST, PLAN_SLOT, PLAN_NEXT, PLAN_USED = range(6)


def _expert_kernel(plan_ref, xs_ref, bgu_ref, bd_ref, wgu_hbm, wd_hbm, y_ref,
                   wgu_f32, wd_f32, wgu_bf, wd_bf, sems, *, layer):
    b = pl.program_id(0)

    def weight_copies(expert, slot):
        return (pltpu.make_async_copy(wgu_hbm.at[layer, expert], wgu_f32.at[slot], sems.at[0, slot]),
                pltpu.make_async_copy(wd_hbm.at[layer, expert], wd_f32.at[slot], sems.at[1, slot]))

    def ffn(rows):
        x = jnp.concatenate(_unpack_bf16_pairs(xs_ref[0:rows, :]), axis=1).astype(BF16)
        gu = _dot(x, wgu_bf[...]) + bgu_ref[...]
        glu = jnp.minimum(gu[:, :D_FF], SWIGLU_LIMIT)
        lin = jnp.clip(gu[:, D_FF:], -SWIGLU_LIMIT, SWIGLU_LIMIT)
        act = glu * jax.nn.sigmoid(SWIGLU_ALPHA * glu) * (lin + 1.0)
        y_ref[0:rows, :] = _pack_bf16_pairs(_dot(act.astype(BF16), wd_bf[...]) + bd_ref[...])
        if rows < MOE_BLOCK:
            y_ref[rows:, :] = jnp.zeros((MOE_BLOCK - rows, D_MODEL // 2), y_ref.dtype)

    n_used = plan_ref[PLAN_USED, 0]

    @pl.when(b < n_used)
    def _():
        slot = plan_ref[PLAN_SLOT, b]

        @pl.when(b == 0)
        def _():
            for cp in weight_copies(plan_ref[PLAN_EXPERT, 0], 0):
                cp.start()

        @pl.when(plan_ref[PLAN_FIRST, b] == 1)
        def _():
            for cp in weight_copies(plan_ref[PLAN_EXPERT, b], slot):
                cp.wait()
            wgu_bf[...] = wgu_f32[slot].astype(BF16)
            wd_bf[...] = wd_f32[slot].astype(BF16)

            @pl.when(plan_ref[PLAN_NEXT, b] >= 0)
            def _():
                for cp in weight_copies(plan_ref[PLAN_NEXT, b], 1 - slot):
                    cp.start()

        steps = (plan_ref[PLAN_VALID, b] + (MOE_STEP - 1)) // MOE_STEP
        for rows in range(MOE_STEP, MOE_BLOCK + 1, MOE_STEP):
            want = steps <= 1 if rows == MOE_STEP else steps == rows // MOE_STEP
            pl.when(want)(functools.partial(ffn, rows))

    @pl.when(b >= n_used)
    def _():
        y_ref[...] = jnp.zeros(y_ref.shape, y_ref.dtype)


def _experts(xs, plan, l, w_gu, b_gu4, w_down, b_down4):
    n_slots = xs.shape[0]
    bm = MOE_BLOCK
    n_blocks = n_slots // bm

    def blk(b, plan):
        return jnp.minimum(b, plan[PLAN_USED, 0] - 1)

    def bspec(width):
        return pl.BlockSpec((None, None, 1, width), lambda b, plan: (l, plan[PLAN_EXPERT, blk(b, plan)], 0, 0))

    grid_spec = pltpu.PrefetchScalarGridSpec(
        num_scalar_prefetch=1,
        grid=(n_blocks,),
        in_specs=[
            pl.BlockSpec((bm, D_MODEL // 2), lambda b, plan: (blk(b, plan), 0)),
            bspec(2 * D_FF), bspec(D_MODEL),
            pl.BlockSpec(memory_space=pl.ANY), pl.BlockSpec(memory_space=pl.ANY),
        ],
        out_specs=pl.BlockSpec((bm, D_MODEL // 2), lambda b, plan: (b, 0)),
        scratch_shapes=[pltpu.VMEM((2, D_MODEL, 2 * D_FF), F32), pltpu.VMEM((2, D_FF, D_MODEL), F32),
                        pltpu.VMEM((D_MODEL, 2 * D_FF), BF16), pltpu.VMEM((D_FF, D_MODEL), BF16),
                        pltpu.SemaphoreType.DMA((2, 2))],
    )
    return pl.pallas_call(
        functools.partial(_expert_kernel, layer=l),
        out_shape=jax.ShapeDtypeStruct((n_slots, D_MODEL // 2), jnp.int32),
        grid_spec=grid_spec,
        compiler_params=_cparams("arbitrary"),
        cost_estimate=pl.CostEstimate(
            flops=2 * n_slots * 3 * D_MODEL * D_FF, transcendentals=n_slots * D_FF,
            bytes_accessed=4 * N_EXPERTS * 3 * D_MODEL * D_FF + 2 * 4 * n_slots * (D_MODEL // 2)),
        name="moe_experts",
    )(plan, xs, b_gu4, b_down4, w_gu, w_down)


def _combine_kernel(tw_ref, x_ref, mod_ref, yg_ref, o_ref):
    o_ref[...] = _moe_residual(tw_ref, x_ref, mod_ref, yg_ref)


def _combine(yg, tw, x1, mods, l, tok):
    t = x1.shape[0]
    tc = COMBINE_TILE
    p_tiles = tok["t_prompt"] // tc
    s_tiles = tok["dec_seq"] // tc

    def mod_idx(i):
        row = jnp.where(i < p_tiles, 0, 1 + (i - p_tiles) // s_tiles)
        return (l, row, 0, 0)

    return pl.pallas_call(
        _combine_kernel,
        out_shape=jax.ShapeDtypeStruct((t, D_MODEL), F32),
        grid=(t // tc,),
        in_specs=[
            pl.BlockSpec((tc, ROUTER_PAD), lambda i: (i, 0)),
            pl.BlockSpec((tc, D_MODEL), lambda i: (i, 0)),
            pl.BlockSpec((None, None, 6, D_MODEL), mod_idx),
            pl.BlockSpec((TOP_K, tc, D_MODEL // 2), lambda i: (0, i, 0)),
        ],
        out_specs=pl.BlockSpec((tc, D_MODEL), lambda i: (i, 0)),
        compiler_params=_cparams("arbitrary"),
        name="moe_combine",
    )(tw, x1, mods, yg)


def _slot_tokens(slot, counts, padded_start, padded_end, n_slots, t):
    n = slot.shape[0]
    j = jnp.arange(MOE_BLOCK, dtype=jnp.int32)
    pad_key = (padded_start + counts)[:, None] + j[None, :]
    pad_key = jnp.where(pad_key < padded_end[:, None], pad_key, n_slots).reshape(N_EXPERTS * MOE_BLOCK)
    pad_val = jnp.arange(N_EXPERTS * MOE_BLOCK, dtype=jnp.int32) % t
    keys = jnp.concatenate([slot, pad_key])
    vals = jnp.concatenate([jnp.arange(n, dtype=jnp.int32) % t, pad_val])
    tok_bits = max(1, (t - 1).bit_length())
    assert (n_slots + 1) << tok_bits < 2 ** 31
    packed = jnp.sort(lax.shift_left(keys, tok_bits) | vals)
    return packed[:n_slots] & ((1 << tok_bits) - 1)


def _route(expert_rank, counts_pad, n_slots):
    t = expert_rank.shape[1]
    flat_e = expert_rank[:TOP_K].reshape(TOP_K * t)
    onehot = (flat_e[:, None] == jnp.arange(N_EXPERTS, dtype=jnp.int32)[None, :]).astype(jnp.int32)
    rank = expert_rank[TOP_K:].reshape(TOP_K * t)
    counts = counts_pad[0, :N_EXPERTS]
    padded = (counts + MOE_BLOCK - 1) // MOE_BLOCK * MOE_BLOCK
    padded_end = jnp.cumsum(padded)
    padded_start = padded_end - padded
    slot = jnp.sum(onehot * padded_start[None, :], axis=1) + rank
    n_blocks = n_slots // MOE_BLOCK
    block_row0 = jnp.arange(n_blocks, dtype=jnp.int32) * MOE_BLOCK
    block_exp = jnp.minimum(
        jnp.sum((padded_end[None, :] <= block_row0[:, None]).astype(jnp.int32), axis=1),
        N_EXPERTS - 1).astype(jnp.int32)
    exp_hot = (block_exp[:, None] == jnp.arange(N_EXPERTS, dtype=jnp.int32)[None, :]).astype(jnp.int32)
    rows_end = jnp.sum(exp_hot * (padded_start + counts)[None, :], axis=1)
    block_valid = jnp.clip(rows_end - block_row0, 0, MOE_BLOCK).astype(jnp.int32)
    n_used = (padded_end[-1] // MOE_BLOCK).astype(jnp.int32)
    block_ids = jnp.arange(n_blocks, dtype=jnp.int32)
    prev_exp = jnp.concatenate([jnp.full((1,), -1, jnp.int32), block_exp[:-1]])
    first = ((block_exp != prev_exp) & (block_ids < n_used)).astype(jnp.int32)
    buf_slot = (jnp.cumsum(first) - 1) % 2
    experts = jnp.arange(N_EXPERTS, dtype=jnp.int32)
    holder = jnp.where(counts > 0, experts, N_EXPERTS)
    later = lax.cummin(jnp.concatenate([holder[1:], jnp.full((1,), N_EXPERTS, jnp.int32)]), reverse=True)
    next_exp = jnp.where(later < N_EXPERTS, later, -1)
    block_next = jnp.sum(exp_hot * next_exp[None, :], axis=1)
    plan = jnp.stack([block_exp, block_valid, first, buf_slot, block_next,
                      jnp.full((n_blocks,), n_used, jnp.int32)]).astype(jnp.int32)
    slot = slot.astype(jnp.int32)
    slot_tok = _slot_tokens(slot, counts, padded_start, padded_end, n_slots, t)
    return slot, slot_tok, plan


def _segment_matrices():
    m_q = np.zeros((QK_PAD, LANES), np.float32)
    m_k = np.zeros((QK_PAD, LANES), np.float32)
    inv_cnt_q = np.ones((1, LANES), np.float32)
    for h in range(N_HEADS):
        m_q[h * HEAD_PAD:h * HEAD_PAD + QK_NOPE, 2 * h] = 1.0
        m_q[h * HEAD_PAD + QK_NOPE:h * HEAD_PAD + QK_HEAD, 2 * h + 1] = 1.0
        inv_cnt_q[0, 2 * h] = 1.0 / QK_NOPE
        inv_cnt_q[0, 2 * h + 1] = 1.0 / ROPE_DIM
        m_k[h * HEAD_PAD:h * HEAD_PAD + QK_NOPE, h] = 1.0
    dup = lambda m: np.concatenate([m.T, m.T], axis=0)
    return dict(m_q=jnp.asarray(m_q, BF16), mt2_q=jnp.asarray(dup(m_q), BF16), inv_cnt_q=jnp.asarray(inv_cnt_q),
                m_k=jnp.asarray(m_k, BF16), mt2_k=jnp.asarray(dup(m_k), BF16))


def _rope_lane_tables(n_tokens, ident_rows):
    rows = n_tokens // GRID_W
    row = jnp.broadcast_to(jnp.arange(rows)[:, None], (rows, GRID_W)).reshape(n_tokens)
    col = jnp.broadcast_to(jnp.arange(GRID_W)[None, :], (rows, GRID_W)).reshape(n_tokens)
    n_freq = ROPE_DIM // 4
    inv = 1.0 / (ROPE_THETA ** (jnp.arange(n_freq, dtype=F32) / n_freq))
    ang = jnp.concatenate([row[:, None].astype(F32) * inv, col[:, None].astype(F32) * inv], axis=-1)
    cos, sin = jnp.cos(ang), jnp.sin(ang)
    ones = jnp.ones((n_tokens, QK_NOPE), F32)
    tail = HEAD_PAD - QK_HEAD
    cosf = jnp.concatenate([ones, cos, cos, jnp.ones((n_tokens, tail), F32)], axis=1)
    sinf = jnp.concatenate([0.0 * ones, -sin, sin, jnp.zeros((n_tokens, tail), F32)], axis=1)
    cosf = jnp.concatenate([cosf, jnp.ones((ident_rows, HEAD_PAD), F32)], axis=0)
    sinf = jnp.concatenate([sinf, jnp.zeros((ident_rows, HEAD_PAD), F32)], axis=0)
    return cosf, sinf


def _dft_tables(seq_len):
    def cs(n):
        k = np.arange(n, dtype=np.int64)
        ang = 2.0 * np.pi * ((k[:, None] * k[None, :]) % n).astype(np.float64) / n
        return np.cos(ang).astype(np.float32), np.sin(ang).astype(np.float32)

    cl, sl = cs(seq_len)
    cc, sc = cs(FNET_GC)
    return dict(cos=jnp.asarray(cl).astype(BF16), sin=jnp.asarray(sl).astype(BF16),
                chan=jnp.asarray(np.concatenate([cc, sc], axis=1)).astype(BF16))


def _layout_weights(w_in, w_q_b, w_kv_b, q_nope_g, q_rope_g, k_nope_g, k_rope_g, w_router, b_router):
    depth = w_in.shape[0]
    w_in_a = w_in[:, :, :COL_F].astype(BF16)
    w_in_b = w_in[:, :, COL_KPE + ROPE_DIM:].astype(BF16)
    w_q = jnp.pad(w_q_b.reshape(depth, Q_RANK, N_HEADS, QK_HEAD),
                  ((0, 0), (0, 0), (0, 0), (0, HEAD_PAD - QK_HEAD))).reshape(depth, Q_RANK, QK_PAD).astype(BF16)
    kv = w_kv_b.reshape(depth, KV_RANK, N_HEADS, QK_NOPE + V_HEAD)
    w_k = jnp.pad(kv[..., :QK_NOPE], ((0, 0), (0, 0), (0, 0), (0, HEAD_PAD - QK_NOPE))
                  ).reshape(depth, KV_RANK, QK_PAD).astype(BF16)
    w_v = kv[..., QK_NOPE:].reshape(depth, KV_RANK, V_WIDTH).astype(BF16)
    zq = jnp.zeros((depth, HEAD_PAD - QK_HEAD), F32)
    g_q = jnp.tile(jnp.concatenate([q_nope_g, q_rope_g, zq], axis=1), (1, N_HEADS))[:, None, :]
    g_k = jnp.tile(jnp.concatenate([k_nope_g, jnp.zeros((depth, HEAD_PAD - QK_NOPE), F32)], axis=1),
                   (1, N_HEADS))[:, None, :]
    g_kpe = jnp.concatenate([jnp.zeros((depth, QK_NOPE), F32), k_rope_g, zq], axis=1)[:, None, :]
    w_hi = w_router.astype(BF16)
    w_lo = (w_router - w_hi.astype(F32)).astype(BF16)
    w_r3 = jnp.pad(jnp.concatenate([w_hi, w_hi, w_lo], axis=1), ((0, 0), (0, 0), (0, ROUTER_PAD - N_EXPERTS)))
    b_r = jnp.pad(b_router, ((0, 0), (0, ROUTER_PAD - N_EXPERTS)), constant_values=NEG_BIG)[:, None, :]
    return dict(w_in_a=w_in_a, w_in_b=w_in_b, w_q=w_q, w_k=w_k, w_v=w_v, g_q=g_q, g_k=g_k, g_kpe=g_kpe, w_router3=w_r3, b_router=b_r)


def kernel(x_prompt, x_sample, cache_ckv, cache_kpe, c, c_ctx, w_mod, b_mod, norm1_g, norm2_g, w_in, q_a_g, kv_a_g, w_q_b, w_kv_b, q_nope_g, q_rope_g, k_nope_g, k_rope_g, w_attn_o, w_fnet_o, w_pool_grp, pool_scale, w_pool_o, w_out, w_router, b_router, w_gu, b_gu, w_down, b_down):
    batch, seq, d = x_prompt.shape
    dec_batch, dec_seq, _ = x_sample.shape
    depth = w_mod.shape[0]
    past = cache_ckv.shape[2]
    t_prompt = batch * seq
    t_sample = dec_batch * dec_seq
    assert d == D_MODEL and dec_batch + 1 <= 8
    assert t_prompt % TOKEN_TILE == 0 and dec_seq % TOKEN_TILE == 0 and seq % COMBINE_TILE == 0
    assert dec_seq % GRID_W == 0

    consts = _segment_matrices()
    consts["cosf"], consts["sinf"] = _rope_lane_tables(dec_seq, IN_TILE)
    consts["tri"] = jnp.asarray(np.tril(np.ones((TOKEN_TILE, TOKEN_TILE), np.float32), -1), BF16)

    wts = _layout_weights(w_in, w_q_b, w_kv_b, q_nope_g, q_rope_g, k_nope_g, k_rope_g, w_router, b_router)
    row3 = lambda a: a[:, None, :]
    wts.update(norm1_g=row3(norm1_g), norm2_g=row3(norm2_g), q_a_g=row3(q_a_g), kv_a_g=row3(kv_a_g),
               w_attn_o=w_attn_o.astype(BF16), w_fnet_o=w_fnet_o.astype(BF16), w_pool_o=w_pool_o.astype(BF16),
               w_out=w_out.astype(BF16), w_pool_grp=w_pool_grp.astype(BF16), pool_scale=row3(pool_scale))
    b_gu4 = b_gu[:, :, None, :]
    b_down4 = b_down[:, :, None, :]

    cond8 = jnp.concatenate([c_ctx[None, :], c, jnp.zeros((8 - 1 - dec_batch, d), F32)], axis=0)
    mods = _modulation(cond8, w_mod, b_mod).reshape(depth, 8, 6, D_MODEL)

    kpe_pad = jnp.pad(cache_kpe, ((0, 0), (0, 0), (0, 0), (QK_NOPE, HEAD_PAD - QK_HEAD)))
    k_ctx, v_ctx = _ctx_keys(cache_ckv, kpe_pad, wts, consts)

    streams = [
        dict(x=x_prompt.reshape(t_prompt, d), n_seq=batch, seq_len=seq, tok=dict(t_prompt=t_prompt, dec_seq=dec_seq),
             dft=_dft_tables(seq), ctx=None, moe=None),
        dict(x=x_sample.reshape(t_sample, d), n_seq=dec_batch, seq_len=dec_seq, tok=dict(t_prompt=0, dec_seq=dec_seq),
             dft=_dft_tables(dec_seq), ctx=(k_ctx, v_ctx, past), moe=None),
    ]
    ckv_list, kpe_list = [], []

    def mixers(st, l):
        moe = None if st["x"] is not None else (st["x1"], st["tw"], st["yg"])
        (q, k, v, ckv, kpe, f_in, p_in, st["gates"]), st["x"] = _in_projection(
            st["x"], moe, mods, l, wts, consts, st["tok"])
        if st["ctx"] is None:
            ckv_list.append(ckv.reshape(batch, seq, KV_RANK))
            kpe_list.append(kpe.reshape(batch, seq, ROPE_DIM))
        parts = [(k, v, None, st["seq_len"])]
        if st["ctx"] is not None:
            parts.append((st["ctx"][0], st["ctx"][1], l, st["ctx"][2]))
        st["attn"] = _attention(q, parts, st["n_seq"], st["seq_len"])
        st["fnet"] = _fourier(f_in, st["n_seq"], st["seq_len"], 0, st["dft"])
        st["pool"] = _pooling(p_in, st["n_seq"], st["seq_len"], 0, l, wts)

    def merge_and_route(st, l):
        t = st["n_seq"] * st["seq_len"]
        n_slots = (t * TOP_K + N_EXPERTS * (MOE_BLOCK - 1) + MOE_BLOCK - 1) // MOE_BLOCK * MOE_BLOCK
        st["x1"], st["h2p"], st["tw"], expert_rank, counts_pad = _merge(
            st["x"], mods, st["attn"], st["fnet"], st["pool"], st["gates"], l, wts, consts, st["tok"])
        st["slot_by_choice"], st["slot_tok"], st["plan"] = _route(expert_rank, counts_pad, n_slots)

    def dispatch(st):
        st["xs"] = _sc_row_gather(st["h2p"], st["slot_tok"])

    def experts(st, l):
        st["y"] = _experts(st["xs"], st["plan"], l, w_gu, b_gu4, w_down, b_down4)

    def gather_back(st):
        t = st["n_seq"] * st["seq_len"]
        st["yg"] = _sc_row_gather(st["y"], st["slot_by_choice"]).reshape(TOP_K, t, D_MODEL // 2)
        st["x"] = None

    def mixer_input(st):
        return "x" if st["x"] is not None else "yg"

    def tie(sa, ka, sb, kb):
        sa[ka], sb[kb] = lax.optimization_barrier((sa[ka], sb[kb]))

    a, b = streams
    mixers(a, 0)
    merge_and_route(a, 0)
    tie(b, mixer_input(b), a, "slot_tok")
    dispatch(a)
    mixers(b, 0)
    tie(a, "xs", b, "attn")
    experts(a, 0)
    for l in range(depth):
        tie(b, "x", a, "y")
        gather_back(a)
        merge_and_route(b, l)
        tie(a, "yg", b, "slot_tok")
        dispatch(b)
        if l + 1 == depth:
            break
        mixers(a, l + 1)
        tie(b, "xs", a, "attn")
        experts(b, l)
        tie(a, "x", b, "y")
        gather_back(b)
        merge_and_route(a, l + 1)
        tie(b, "yg", a, "slot_tok")
        dispatch(a)
        mixers(b, l + 1)
        tie(a, "xs", b, "attn")
        experts(a, l + 1)
    a["out"] = _combine(a["yg"], a["tw"], a["x1"], mods, depth - 1, a["tok"])
    tie(b, "xs", a, "out")
    experts(b, depth - 1)
    gather_back(b)
    out_a = a["out"]
    out_b = _combine(b["yg"], b["tw"], b["x1"], mods, depth - 1, b["tok"])
    y_prompt = out_a.reshape(batch, seq, d)
    y_sample = out_b.reshape(dec_batch, dec_seq, d)
    return (y_prompt, y_sample, jnp.stack(ckv_list, axis=1), jnp.stack(kpe_list, axis=1))
```
